```python
import jax, jax.numpy as jnp
from jax import lax
import numpy as np

D_MODEL = 2048
BATCH = 8
SEQ = 4096
DEPTH = 1

CHUNK = 64
Q_BLOCK = 2 * CHUNK
POOL_WIDTH = D_MODEL // 2
POOL_WINDOWS = (2, 4, 8, 16)
POOL_GROUPS = len(POOL_WINDOWS)
POOL_GROUP_WIDTH = POOL_WIDTH // POOL_GROUPS
SB_WIDTH = D_MODEL - POOL_WIDTH
SB_HEAD_DIM = 128
SB_HEADS = SB_WIDTH // SB_HEAD_DIM
MIX_WIDTH = POOL_WIDTH + SB_WIDTH
IN_PROJ_WIDTH = POOL_WIDTH + 3 * SB_WIDTH
D_FF = 4 * D_MODEL
DEEPNORM_ALPHA = (2.0 * DEPTH) ** 0.25
DEEPNORM_BETA = (8.0 * DEPTH) ** -0.25
LN_EPS = 1e-5

kernel_name = "hybrid_pool_stickbreaking_deepnorm_block"


def layer_norm(x, g, b):
    xf = x.astype(jnp.float32)
    mu = jnp.mean(xf, axis=-1, keepdims=True)
    var = jnp.mean(jnp.square(xf - mu), axis=-1, keepdims=True)
    y = (xf - mu) * lax.rsqrt(var + LN_EPS)
    return (y * g.astype(jnp.float32) + b.astype(jnp.float32)).astype(x.dtype)


def multi_scale_pool(u, w_pool, pool_scale):
    b, s, _ = u.shape
    ug = u.reshape(b, s, POOL_GROUPS, POOL_GROUP_WIDTH).astype(jnp.float32)
    csum = jnp.concatenate(
        [jnp.zeros((b, 1, POOL_GROUPS, POOL_GROUP_WIDTH), jnp.float32),
         jnp.cumsum(ug, axis=1)], axis=1)
    t = jnp.arange(s, dtype=jnp.int32)
    windows = jnp.asarray(POOL_WINDOWS, dtype=jnp.int32)
    start = jnp.maximum(t[:, None] + 1 - windows[None, :], 0)
    count = (t[:, None] + 1 - start).astype(jnp.float32)
    group_idx = jnp.arange(POOL_GROUPS, dtype=jnp.int32)[None, :]
    c_start = csum[:, start, group_idx]
    mean = (csum[:, 1:] - c_start) / count[None, :, :, None]
    y = mean - ug
    y = jnp.einsum('bsgc,gcd->bsgd', y, w_pool.astype(jnp.float32))
    y = y * pool_scale.astype(jnp.float32)[None, None]
    return y.reshape(b, s, POOL_WIDTH).astype(u.dtype)


def stick_breaking_attention(q, k, v):
    b, s, h, dh = q.shape
    n_blocks = s // Q_BLOCK
    scale = 1.0 / np.sqrt(dh).astype(np.float32)
    qb = q.reshape(b, n_blocks, Q_BLOCK, h, dh).transpose(1, 0, 2, 3, 4)
    key_pos = jnp.arange(s, dtype=jnp.int32)

    def one_block(args):
        qi, i = args
        z = jnp.einsum('bqhd,bkhd->bhqk', qi, k).astype(jnp.float32) * scale
        q_pos = i * Q_BLOCK + jnp.arange(Q_BLOCK, dtype=jnp.int32)
        mask = (key_pos[None, :] < q_pos[:, None])[None, None]
        log_not = jnp.where(mask, jax.nn.log_sigmoid(-z), 0.0)
        after = lax.cumsum(log_not, axis=3, reverse=True) - log_not
        a = jnp.where(mask, jnp.exp(jax.nn.log_sigmoid(z) + after), 0.0)
        return jnp.einsum('bhqk,bkhd->bqhd', a.astype(v.dtype), v)

    out = lax.map(one_block, (qb, jnp.arange(n_blocks, dtype=jnp.int32)))
    return out.transpose(1, 0, 2, 3, 4).reshape(b, s, h * dh)


def _fwd_setup_inputs(seed: int = 0) -> dict:
    key = jax.random.key(seed)
    ks = jax.random.split(key, 16)
    f32 = jnp.float32
    x = jax.random.normal(ks[0], (BATCH, SEQ, D_MODEL), f32)
    ln_in_g = 1.0 + 0.02 * jax.random.normal(ks[1], (D_MODEL,), f32)
    ln_in_b = 0.02 * jax.random.normal(ks[2], (D_MODEL,), f32)
    w_in = jax.random.normal(ks[3], (DEPTH, D_MODEL, IN_PROJ_WIDTH), f32) * D_MODEL ** -0.5
    w_pool = jax.random.normal(ks[4], (DEPTH, POOL_GROUPS, POOL_GROUP_WIDTH, POOL_GROUP_WIDTH), f32) * POOL_GROUP_WIDTH ** -0.5
    pool_scale = 1.0 + 0.02 * jax.random.normal(ks[5], (DEPTH, POOL_GROUPS, POOL_GROUP_WIDTH), f32)
    w_out = jax.random.normal(ks[6], (DEPTH, MIX_WIDTH, D_MODEL), f32) * (MIX_WIDTH ** -0.5 * DEEPNORM_BETA)
    ln1_g = 1.0 + 0.02 * jax.random.normal(ks[7], (DEPTH, D_MODEL), f32)
    ln1_b = 0.02 * jax.random.normal(ks[8], (DEPTH, D_MODEL), f32)
    w_ff1 = jax.random.normal(ks[9], (DEPTH, D_MODEL, D_FF), f32) * D_MODEL ** -0.5
    b_ff1 = 0.02 * jax.random.normal(ks[10], (DEPTH, D_FF), f32)
    w_ff2 = jax.random.normal(ks[11], (DEPTH, D_FF, D_MODEL), f32) * (D_FF ** -0.5 * DEEPNORM_BETA)
    b_ff2 = 0.02 * jax.random.normal(ks[12], (DEPTH, D_MODEL), f32)
    ln2_g = 1.0 + 0.02 * jax.random.normal(ks[13], (DEPTH, D_MODEL), f32)
    ln2_b = 0.02 * jax.random.normal(ks[14], (DEPTH, D_MODEL), f32)
    return {"x": x, "ln_in_g": ln_in_g, "ln_in_b": ln_in_b, "w_in": w_in,
            "w_pool": w_pool, "pool_scale": pool_scale, "w_out": w_out,
            "ln1_g": ln1_g, "ln1_b": ln1_b, "w_ff1": w_ff1, "b_ff1": b_ff1,
            "w_ff2": w_ff2, "b_ff2": b_ff2, "ln2_g": ln2_g, "ln2_b": ln2_b}


def _fwd_reference(x, ln_in_g, ln_in_b, w_in, w_pool, pool_scale, w_out,
              ln1_g, ln1_b, w_ff1, b_ff1, w_ff2, b_ff2, ln2_g, ln2_b):
    b, s, _ = x.shape
    h = layer_norm(x, ln_in_g, ln_in_b)
    for layer in range(DEPTH):
        u = jnp.einsum('bsd,de->bse', h, w_in[layer])
        u_pool = u[..., :POOL_WIDTH]
        q, k, v = jnp.split(u[..., POOL_WIDTH:], 3, axis=-1)
        q = q.reshape(b, s, SB_HEADS, SB_HEAD_DIM)
        k = k.reshape(b, s, SB_HEADS, SB_HEAD_DIM)
        v = v.reshape(b, s, SB_HEADS, SB_HEAD_DIM)
        y_pool = multi_scale_pool(u_pool, w_pool[layer], pool_scale[layer])
        y_sb = stick_breaking_attention(q, k, v)
        mix = jnp.concatenate([y_pool, y_sb], axis=-1)
        mix = jnp.einsum('bse,ed->bsd', mix, w_out[layer])
        h = layer_norm(DEEPNORM_ALPHA * h + mix, ln1_g[layer], ln1_b[layer])
        f = jnp.einsum('bsd,df->bsf', h, w_ff1[layer]) + b_ff1[layer]
        f = jnp.square(jax.nn.relu(f))
        f = jnp.einsum('bsf,fd->bsd', f, w_ff2[layer]) + b_ff2[layer]
        h = layer_norm(DEEPNORM_ALPHA * h + f, ln2_g[layer], ln2_b[layer])
    return h


import jax as _jax
import jax.numpy as _jnp

TWIN_FORMAT = 'train_step'
FWD_PARAMS = ['x', 'ln_in_g', 'ln_in_b', 'w_in', 'w_pool', 'pool_scale', 'w_out', 'ln1_g', 'ln1_b', 'w_ff1', 'b_ff1', 'w_ff2', 'b_ff2', 'ln2_g', 'ln2_b']
TWIN_WEIGHTS = ['ln_in_g', 'ln_in_b', 'w_in', 'w_pool', 'pool_scale', 'w_out', 'ln1_g', 'ln1_b', 'w_ff1', 'b_ff1', 'w_ff2', 'b_ff2', 'ln2_g', 'ln2_b']
TWIN_DIFF_INPUT = 'x'
TWIN_INPUTS = ['x', 'ln_in_g', 'ln_in_b', 'w_in', 'w_pool', 'pool_scale', 'w_out', 'ln1_g', 'ln1_b', 'w_ff1', 'b_ff1', 'w_ff2', 'b_ff2', 'ln2_g', 'ln2_b', 'loss_target', 'm_ln_in_g', 'm_ln_in_b', 'm_w_in', 'm_w_pool', 'm_pool_scale', 'm_w_out', 'm_ln1_g', 'm_ln1_b', 'm_w_ff1', 'm_b_ff1', 'm_w_ff2', 'm_b_ff2', 'm_ln2_g', 'm_ln2_b', 'v_ln_in_g', 'v_ln_in_b', 'v_w_in', 'v_w_pool', 'v_pool_scale', 'v_w_out', 'v_ln1_g', 'v_ln1_b', 'v_w_ff1', 'v_b_ff1', 'v_w_ff2', 'v_b_ff2', 'v_ln2_g', 'v_ln2_b']
TWIN_OUTPUTS = ['loss', 'grad_x', 'grad_ln_in_g', 'grad_ln_in_b', 'grad_w_in', 'grad_w_pool', 'grad_pool_scale', 'grad_w_out', 'grad_ln1_g', 'grad_ln1_b', 'grad_w_ff1', 'grad_b_ff1', 'grad_w_ff2', 'grad_b_ff2', 'grad_ln2_g', 'grad_ln2_b', 'delta_ln_in_g', 'delta_ln_in_b', 'delta_w_in', 'delta_w_pool', 'delta_pool_scale', 'delta_w_out', 'delta_ln1_g', 'delta_ln1_b', 'delta_w_ff1', 'delta_b_ff1', 'delta_w_ff2', 'delta_b_ff2', 'delta_ln2_g', 'delta_ln2_b', 'new_m_ln_in_g', 'new_m_ln_in_b', 'new_m_w_in', 'new_m_w_pool', 'new_m_pool_scale', 'new_m_w_out', 'new_m_ln1_g', 'new_m_ln1_b', 'new_m_w_ff1', 'new_m_b_ff1', 'new_m_w_ff2', 'new_m_b_ff2', 'new_m_ln2_g', 'new_m_ln2_b', 'new_v_ln_in_g', 'new_v_ln_in_b', 'new_v_w_in', 'new_v_w_pool', 'new_v_pool_scale', 'new_v_w_out', 'new_v_ln1_g', 'new_v_ln1_b', 'new_v_w_ff1', 'new_v_b_ff1', 'new_v_w_ff2', 'new_v_b_ff2', 'new_v_ln2_g', 'new_v_ln2_b']
TWIN_LEAF_KINDS = {'loss': 'loss', 'grad_x': 'grad_x', 'grad_ln_in_g': 'grad_w', 'grad_ln_in_b': 'grad_w', 'grad_w_in': 'grad_w', 'grad_w_pool': 'grad_w', 'grad_pool_scale': 'grad_w', 'grad_w_out': 'grad_w', 'grad_ln1_g': 'grad_w', 'grad_ln1_b': 'grad_w', 'grad_w_ff1': 'grad_w', 'grad_b_ff1': 'grad_w', 'grad_w_ff2': 'grad_w', 'grad_b_ff2': 'grad_w', 'grad_ln2_g': 'grad_w', 'grad_ln2_b': 'grad_w', 'delta_ln_in_g': 'delta_w', 'delta_ln_in_b': 'delta_w', 'delta_w_in': 'delta_w', 'delta_w_pool': 'delta_w', 'delta_pool_scale': 'delta_w', 'delta_w_out': 'delta_w', 'delta_ln1_g': 'delta_w', 'delta_ln1_b': 'delta_w', 'delta_w_ff1': 'delta_w', 'delta_b_ff1': 'delta_w', 'delta_w_ff2': 'delta_w', 'delta_b_ff2': 'delta_w', 'delta_ln2_g': 'delta_w', 'delta_ln2_b': 'delta_w', 'new_m_ln_in_g': 'new_m', 'new_m_ln_in_b': 'new_m', 'new_m_w_in': 'new_m', 'new_m_w_pool': 'new_m', 'new_m_pool_scale': 'new_m', 'new_m_w_out': 'new_m', 'new_m_ln1_g': 'new_m', 'new_m_ln1_b': 'new_m', 'new_m_w_ff1': 'new_m', 'new_m_b_ff1': 'new_m', 'new_m_w_ff2': 'new_m', 'new_m_b_ff2': 'new_m', 'new_m_ln2_g': 'new_m', 'new_m_ln2_b': 'new_m', 'new_v_ln_in_g': 'new_v', 'new_v_ln_in_b': 'new_v', 'new_v_w_in': 'new_v', 'new_v_w_pool': 'new_v', 'new_v_pool_scale': 'new_v', 'new_v_w_out': 'new_v', 'new_v_ln1_g': 'new_v', 'new_v_ln1_b': 'new_v', 'new_v_w_ff1': 'new_v', 'new_v_b_ff1': 'new_v', 'new_v_w_ff2': 'new_v', 'new_v_b_ff2': 'new_v', 'new_v_ln2_g': 'new_v', 'new_v_ln2_b': 'new_v'}


def _forward(args):
    return _fwd_reference(*[args[k] for k in FWD_PARAMS])


def _output_shape():
    def fwd():
        inp = _fwd_setup_inputs(0)
        return _fwd_reference(*[inp[k] for k in FWD_PARAMS])
    out = _jax.eval_shape(fwd)
    return out.shape, out.dtype

N_MICROBATCH = 1
ADAM_LR = 0.001
ADAM_B1 = 0.9
ADAM_B2 = 0.999
ADAM_EPS = 1e-08
ADAM_WD = 0.01
ADAM_STEP = 10
PER_EXAMPLE_BATCH_AXIS = {'x': 0, 'loss_target': 0}
SHARED_INPUTS = []
_WEIGHT_DTYPES = {'ln_in_g': _jnp.float32, 'ln_in_b': _jnp.float32, 'w_in': _jnp.float32, 'w_pool': _jnp.float32, 'pool_scale': _jnp.float32, 'w_out': _jnp.float32, 'ln1_g': _jnp.float32, 'ln1_b': _jnp.float32, 'w_ff1': _jnp.float32, 'b_ff1': _jnp.float32, 'w_ff2': _jnp.float32, 'b_ff2': _jnp.float32, 'ln2_g': _jnp.float32, 'ln2_b': _jnp.float32}
MOMENT_SCALE = {'ln_in_g': 3.584429e-01, 'ln_in_b': 2.908828e-01, 'w_in': 2.525642e-02, 'w_pool': 3.790664e-02, 'pool_scale': 4.192974e-02, 'w_out': 5.610587e-02, 'ln1_g': 4.257008e-01, 'ln1_b': 2.971036e-01, 'w_ff1': 2.678011e-02, 'b_ff1': 5.350909e-02, 'w_ff2': 1.217477e-01, 'b_ff2': 2.305564e-01, 'ln2_g': 1.604485e+01, 'ln2_b': 3.467140e+00}


def _to_microbatches(a, axis):
    t = _jnp.moveaxis(a, axis, 0)
    t = t.reshape((N_MICROBATCH, t.shape[0] // N_MICROBATCH) + t.shape[1:])
    return _jnp.moveaxis(t, 1, axis + 1)


def setup_inputs(seed: int = 0) -> dict:
    inp = _fwd_setup_inputs(seed)
    key = _jax.random.fold_in(_jax.random.key(seed), 7919)
    shape, _ = _output_shape()
    out = dict(inp)
    out["loss_target"] = _jax.random.normal(_jax.random.fold_in(key, 0), shape, _jnp.float32)
    for i, name in enumerate(TWIN_WEIGHTS):
        w = inp[name].astype(_jnp.float32)
        if MOMENT_SCALE is None:
            s = _jnp.sqrt(_jnp.mean(_jnp.square(w)) + 1e-30)
        else:
            s = MOMENT_SCALE[name]
        km, kv = _jax.random.split(_jax.random.fold_in(key, i + 1))
        out[name] = w
        out["m_" + name] = s * _jax.random.normal(km, w.shape, _jnp.float32)
        out["v_" + name] = (s * s) * _jax.random.uniform(kv, w.shape, _jnp.float32, 0.5, 1.5)
    if N_MICROBATCH > 1:
        for name, axis in PER_EXAMPLE_BATCH_AXIS.items():
            out[name] = _to_microbatches(out[name], axis)
    return {'x': out['x'], 'ln_in_g': out['ln_in_g'], 'ln_in_b': out['ln_in_b'], 'w_in': out['w_in'], 'w_pool': out['w_pool'], 'pool_scale': out['pool_scale'], 'w_out': out['w_out'], 'ln1_g': out['ln1_g'], 'ln1_b': out['ln1_b'], 'w_ff1': out['w_ff1'], 'b_ff1': out['b_ff1'], 'w_ff2': out['w_ff2'], 'b_ff2': out['b_ff2'], 'ln2_g': out['ln2_g'], 'ln2_b': out['ln2_b'], 'loss_target': out['loss_target'], 'm_ln_in_g': out['m_ln_in_g'], 'm_ln_in_b': out['m_ln_in_b'], 'm_w_in': out['m_w_in'], 'm_w_pool': out['m_w_pool'], 'm_pool_scale': out['m_pool_scale'], 'm_w_out': out['m_w_out'], 'm_ln1_g': out['m_ln1_g'], 'm_ln1_b': out['m_ln1_b'], 'm_w_ff1': out['m_w_ff1'], 'm_b_ff1': out['m_b_ff1'], 'm_w_ff2': out['m_w_ff2'], 'm_b_ff2': out['m_b_ff2'], 'm_ln2_g': out['m_ln2_g'], 'm_ln2_b': out['m_ln2_b'], 'v_ln_in_g': out['v_ln_in_g'], 'v_ln_in_b': out['v_ln_in_b'], 'v_w_in': out['v_w_in'], 'v_w_pool': out['v_w_pool'], 'v_pool_scale': out['v_pool_scale'], 'v_w_out': out['v_w_out'], 'v_ln1_g': out['v_ln1_g'], 'v_ln1_b': out['v_ln1_b'], 'v_w_ff1': out['v_w_ff1'], 'v_b_ff1': out['v_b_ff1'], 'v_w_ff2': out['v_w_ff2'], 'v_b_ff2': out['v_b_ff2'], 'v_ln2_g': out['v_ln2_g'], 'v_ln2_b': out['v_ln2_b']}


def _loss(weights, diff, rest, loss_target):
    with _jax.named_scope("forward"):
        args = {**rest, TWIN_DIFF_INPUT: diff, **{k: w.astype(_WEIGHT_DTYPES[k]) for k, w in weights.items()}}
        y = _forward(args)
    with _jax.named_scope("loss_head"):
        err = _jnp.square(y.astype(_jnp.float32) - loss_target)
        return 0.5 * _jnp.sum(_jnp.mean(err, axis=-1)) if err.ndim else 0.5 * err


def _adamw(w, g, m, v):
    m = ADAM_B1 * m + (1.0 - ADAM_B1) * g
    v = ADAM_B2 * v + (1.0 - ADAM_B2) * _jnp.square(g)
    m_hat = m / (1.0 - ADAM_B1 ** ADAM_STEP)
    v_hat = v / (1.0 - ADAM_B2 ** ADAM_STEP)
    delta = -ADAM_LR * (m_hat / (_jnp.sqrt(v_hat) + ADAM_EPS) + ADAM_WD * w)
    return delta, m, v


def reference(x, ln_in_g, ln_in_b, w_in, w_pool, pool_scale, w_out, ln1_g, ln1_b, w_ff1, b_ff1, w_ff2, b_ff2, ln2_g, ln2_b, loss_target, m_ln_in_g, m_ln_in_b, m_w_in, m_w_pool, m_pool_scale, m_w_out, m_ln1_g, m_ln1_b, m_w_ff1, m_b_ff1, m_w_ff2, m_b_ff2, m_ln2_g, m_ln2_b, v_ln_in_g, v_ln_in_b, v_w_in, v_w_pool, v_pool_scale, v_w_out, v_ln1_g, v_ln1_b, v_w_ff1, v_b_ff1, v_w_ff2, v_b_ff2, v_ln2_g, v_ln2_b):
    given = dict(x=x, ln_in_g=ln_in_g, ln_in_b=ln_in_b, w_in=w_in, w_pool=w_pool, pool_scale=pool_scale, w_out=w_out, ln1_g=ln1_g, ln1_b=ln1_b, w_ff1=w_ff1, b_ff1=b_ff1, w_ff2=w_ff2, b_ff2=b_ff2, ln2_g=ln2_g, ln2_b=ln2_b, loss_target=loss_target, m_ln_in_g=m_ln_in_g, m_ln_in_b=m_ln_in_b, m_w_in=m_w_in, m_w_pool=m_w_pool, m_pool_scale=m_pool_scale, m_w_out=m_w_out, m_ln1_g=m_ln1_g, m_ln1_b=m_ln1_b, m_w_ff1=m_w_ff1, m_b_ff1=m_b_ff1, m_w_ff2=m_w_ff2, m_b_ff2=m_b_ff2, m_ln2_g=m_ln2_g, m_ln2_b=m_ln2_b, v_ln_in_g=v_ln_in_g, v_ln_in_b=v_ln_in_b, v_w_in=v_w_in, v_w_pool=v_w_pool, v_pool_scale=v_pool_scale, v_w_out=v_w_out, v_ln1_g=v_ln1_g, v_ln1_b=v_ln1_b, v_w_ff1=v_w_ff1, v_b_ff1=v_b_ff1, v_w_ff2=v_w_ff2, v_b_ff2=v_b_ff2, v_ln2_g=v_ln2_g, v_ln2_b=v_ln2_b)
    weights = {n: given[n] for n in TWIN_WEIGHTS}
    shared = {n: given[n] for n in SHARED_INPUTS}
    per_example = {n: given[n] for n in ['x']}
    grad_fn = _jax.value_and_grad(_loss, argnums=(0, 1))

    def one_microbatch(ex, loss_target):
        ex = dict(ex)
        diff = ex.pop(TWIN_DIFF_INPUT)
        return grad_fn(weights, diff, {**shared, **ex}, loss_target)

    if N_MICROBATCH == 1:
        loss, (grad_w, grad_x) = one_microbatch(per_example, given["loss_target"])
    else:
        def body(carry, xs):
            loss_sum, grad_sum = carry
            l_k, (gw_k, gx_k) = one_microbatch(xs[0], xs[1])
            with _jax.named_scope("update"):
                return (loss_sum + l_k, _jax.tree.map(_jnp.add, grad_sum, gw_k)), gx_k

        init = (_jnp.zeros((), _jnp.float32), _jax.tree.map(_jnp.zeros_like, weights))
        (loss, grad_w), grad_x = _jax.lax.scan(body, init, (per_example, given["loss_target"]))
    with _jax.named_scope("update"):
        delta_w, new_m, new_v = {}, {}, {}
        for n in TWIN_WEIGHTS:
            delta_w[n], new_m[n], new_v[n] = _adamw(weights[n], grad_w[n], given["m_" + n], given["v_" + n])
    return (loss, grad_x, *[grad_w[n] for n in TWIN_WEIGHTS], *[delta_w[n] for n in TWIN_WEIGHTS],
            *[new_m[n] for n in TWIN_WEIGHTS], *[new_v[n] for n in TWIN_WEIGHTS])
```

```python
import functools
import math

import jax
import jax.numpy as jnp
from jax import lax
from jax.experimental import pallas as pl
from jax.experimental.pallas import tpu as pltpu

F32 = jnp.float32
BF16 = jnp.bfloat16
MESH = pl.DeviceIdType.MESH

N_DEV = 8
HEAD_DIM = 128
POOL_WINDOWS = (2, 4, 8, 16)
DEEPNORM_ALPHA = (2.0 * 1) ** 0.25
LN_EPS = 1e-5
ADAM_LR = 0.001
ADAM_B1 = 0.9
ADAM_B2 = 0.999
ADAM_EPS = 1e-08
ADAM_WD = 0.01
ADAM_STEP = 10

V7X_VMEM_LIMIT = 56 * 1024 * 1024
ATT_BLOCK = 128
POOL_CHUNK = 128

NN = (((1,), (0,)), ((), ()))
NT = (((1,), (1,)), ((), ()))
TN = (((0,), (0,)), ((), ()))


def _dot(a, b, dims=NN):
    return lax.dot_general(a, b, dims, preferred_element_type=F32)


def _cparams(sem=None):
    return pltpu.CompilerParams(dimension_semantics=sem, vmem_limit_bytes=V7X_VMEM_LIMIT)


def _ln_stats(r):
    mu = jnp.mean(r, axis=-1, keepdims=True)
    xc = r - mu
    var = jnp.mean(xc * xc, axis=-1, keepdims=True)
    rstd = lax.rsqrt(var + LN_EPS)
    return xc * rstd, rstd


def _ln_bwd(dy, xhat, rstd, g):
    dxh = dy * g
    m1 = jnp.mean(dxh, axis=-1, keepdims=True)
    m2 = jnp.mean(dxh * xhat, axis=-1, keepdims=True)
    dx = rstd * (dxh - m1 - xhat * m2)
    dg = jnp.sum(dy * xhat, axis=0, keepdims=True)
    db = jnp.sum(dy, axis=0, keepdims=True)
    return dx, dg, db


def _acc_rows(first, ref, val):
    @pl.when(first)
    def _():
        ref[...] = val

    @pl.when(jnp.logical_not(first))
    def _():
        ref[...] += val


def _matmul(name, a, b, *, dims, grid, a_spec, b_spec, extras=(), extra_specs=(), out_shape, out_specs,
            acc_shape, epilogue, k_axis=2, sem=("parallel", "parallel", "arbitrary")):
    nk = grid[k_axis]
    n_extra = len(extras)
    n_out = len(out_shape)

    def body(a_ref, b_ref, *rest):
        extra_refs = rest[:n_extra]
        out_refs = rest[n_extra:n_extra + n_out]
        if nk == 1:
            epilogue(_dot(a_ref[...], b_ref[...], dims), extra_refs, out_refs)
            return
        acc_ref = rest[n_extra + n_out]
        k = pl.program_id(k_axis)

        @pl.when(k == 0)
        def _():
            acc_ref[...] = jnp.zeros(acc_shape, F32)

        acc_ref[...] += _dot(a_ref[...], b_ref[...], dims)

        @pl.when(k == nk - 1)
        def _():
            epilogue(acc_ref[...], extra_refs, out_refs)

    return pl.pallas_call(
        body, name=name, grid=grid,
        in_specs=[a_spec, b_spec, *extra_specs],
        out_specs=list(out_specs), out_shape=list(out_shape),
        scratch_shapes=[] if nk == 1 else [pltpu.VMEM(acc_shape, F32)],
        compiler_params=_cparams(sem),
    )(a, b, *extras)


def _store_epilogue(dtype):
    def ep(acc, extra_refs, out_refs):
        out_refs[0][...] = acc.astype(dtype)
    return ep


def _ln_in_fwd(x, g, b, tm):
    t, d = x.shape

    def body(x_ref, g_ref, b_ref, h_ref, hb_ref):
        xhat, _ = _ln_stats(x_ref[...])
        h = xhat * g_ref[...] + b_ref[...]
        h_ref[...] = h
        hb_ref[...] = h.astype(BF16)

    row = pl.BlockSpec((tm, d), lambda i: (i, 0))
    vec = pl.BlockSpec((1, d), lambda i: (0, 0))
    return pl.pallas_call(
        body, name="ln_in_fwd", grid=(t // tm,), in_specs=[row, vec, vec], out_specs=[row, row],
        out_shape=[jax.ShapeDtypeStruct((t, d), F32), jax.ShapeDtypeStruct((t, d), BF16)],
        compiler_params=_cparams(("parallel",)),
    )(x, g, b)


def _split3(x):
    hi = x.astype(BF16)
    r = x - hi.astype(F32)
    mid = r.astype(BF16)
    lo = (r - mid.astype(F32)).astype(BF16)
    return hi, mid, lo


def _split2(x):
    hi = x.astype(BF16)
    lo = (x - hi.astype(F32)).astype(BF16)
    return hi, lo


def _pool_fwd(u, wp, sc, t, c):
    n_groups = len(POOL_WINDOWS)
    tc = POOL_CHUNK
    n_chunks = t // tc

    def body(u_ref, wp_ref, sc_ref, y_ref, ypre_ref, xp_ref):
        g = pl.program_id(0)
        xp_ref[pl.ds(0, tc), :] = jnp.zeros((tc, c), F32)
        xp_ref[pl.ds(tc, t), :] = u_ref[...]
        out_i = lax.broadcasted_iota(jnp.int32, (tc, 2 * tc), 0)
        in_j = lax.broadcasted_iota(jnp.int32, (tc, 2 * tc), 1)
        lag = tc + out_i - in_j
        t_in_chunk = lax.broadcasted_iota(jnp.int32, (tc, 1), 0)
        for gi, w in enumerate(POOL_WINDOWS):
            @pl.when(g == gi)
            def _(w=w):
                band = jnp.logical_and(lag >= 0, lag < w).astype(BF16)

                def chunk(ci, carry):
                    start = pl.multiple_of(ci * tc, tc)
                    win = xp_ref[pl.ds(start, 2 * tc), :]
                    hi, mid, lo = _split3(win)
                    wsum = _dot(band, hi) + _dot(band, mid) + _dot(band, lo)
                    cnt = jnp.minimum(ci * tc + t_in_chunk + 1, w).astype(F32)
                    ypre = wsum * (1.0 / cnt) - win[tc:, :]
                    ypre_b = ypre.astype(BF16)
                    y = _dot(ypre_b, wp_ref[...]) * sc_ref[...]
                    ypre_ref[pl.ds(start, tc), :] = ypre_b
                    y_ref[pl.ds(start, tc), :] = y.astype(BF16)
                    return carry

                lax.fori_loop(0, n_chunks, chunk, 0)

    col = pl.BlockSpec((t, c), lambda g: (0, g))
    return pl.pallas_call(
        body, name="pool_fwd", grid=(n_groups,),
        in_specs=[col, pl.BlockSpec((None, c, c), lambda g: (g, 0, 0)), pl.BlockSpec((None, 1, c), lambda g: (g, 0, 0))],
        out_specs=[col, col],
        out_shape=[jax.ShapeDtypeStruct((t, n_groups * c), BF16), jax.ShapeDtypeStruct((t, n_groups * c), BF16)],
        scratch_shapes=[pltpu.VMEM((t + tc, c), F32)],
        compiler_params=_cparams(("parallel",)),
    )(u, wp, sc)


def _pool_bwd(dmixin, ypre, wp, sc, t, c):
    n_groups = len(POOL_WINDOWS)
    tc = POOL_CHUNK
    n_chunks = t // tc

    def body(dy_ref, ypre_ref, wp_ref, sc_ref, du_ref, dwp_ref, dsc_ref, zp_ref):
        g = pl.program_id(0)
        zp_ref[pl.ds(t, tc), :] = jnp.zeros((tc, c), F32)
        dwp_ref[...] = jnp.zeros((c, c), F32)
        dsc_ref[...] = jnp.zeros((1, c), F32)
        out_i = lax.broadcasted_iota(jnp.int32, (tc, 2 * tc), 0)
        in_j = lax.broadcasted_iota(jnp.int32, (tc, 2 * tc), 1)
        lead = in_j - out_i
        t_in_chunk = lax.broadcasted_iota(jnp.int32, (tc, 1), 0)
        for gi, w in enumerate(POOL_WINDOWS):
            @pl.when(g == gi)
            def _(w=w):
                band = jnp.logical_and(lead >= 0, lead < w).astype(BF16)

                def first(ci, carry):
                    start = pl.multiple_of(ci * tc, tc)
                    dy = dy_ref[pl.ds(start, tc), :]
                    yp = ypre_ref[pl.ds(start, tc), :]
                    ymm = _dot(yp, wp_ref[...])
                    dsc_ref[...] += jnp.sum(dy * ymm, axis=0, keepdims=True)
                    dys_b = (dy * sc_ref[...]).astype(BF16)
                    dwp_ref[...] += _dot(yp, dys_b, TN)
                    dyp = _dot(dys_b, wp_ref[...], NT)
                    cnt = jnp.minimum(ci * tc + t_in_chunk + 1, w).astype(F32)
                    zp_ref[pl.ds(start, tc), :] = dyp * (1.0 / cnt)
                    du_ref[pl.ds(start, tc), :] = -dyp
                    return carry

                lax.fori_loop(0, n_chunks, first, 0)

                def second(ci, carry):
                    start = pl.multiple_of(ci * tc, tc)
                    hi, mid, lo = _split3(zp_ref[pl.ds(start, 2 * tc), :])
                    du_ref[pl.ds(start, tc), :] += _dot(band, hi) + _dot(band, mid) + _dot(band, lo)
                    return carry

                lax.fori_loop(0, n_chunks, second, 0)

    col = pl.BlockSpec((t, c), lambda g: (0, g))
    return pl.pallas_call(
        body, name="pool_bwd", grid=(n_groups,),
        in_specs=[col, col, pl.BlockSpec((None, c, c), lambda g: (g, 0, 0)), pl.BlockSpec((None, 1, c), lambda g: (g, 0, 0))],
        out_specs=[col, pl.BlockSpec((None, c, c), lambda g: (g, 0, 0)), pl.BlockSpec((None, 1, c), lambda g: (g, 0, 0))],
        out_shape=[jax.ShapeDtypeStruct((t, n_groups * c), F32), jax.ShapeDtypeStruct((n_groups, c, c), F32),
                   jax.ShapeDtypeStruct((n_groups, 1, c), F32)],
        scratch_shapes=[pltpu.VMEM((t + tc, c), F32)],
        compiler_params=_cparams(("parallel",)),
    )(dmixin, ypre, wp, sc)


def _att_consts():
    b = ATT_BLOCK
    row = lax.broadcasted_iota(jnp.int32, (b, b), 0)
    col = lax.broadcasted_iota(jnp.int32, (b, b), 1)
    tri = (row >= col).astype(BF16)
    causal = col < row
    return tri, causal


def _suffix_sum(x, tri):
    hi, lo = _split2(x)
    return _dot(hi, tri) + _dot(lo, tri)


def _att_scores(qb, kb, mask):
    z = _dot(qb, kb, NT) * (1.0 / math.sqrt(HEAD_DIM))
    sp = jnp.maximum(z, 0.0) + jnp.log(1.0 + jnp.exp(-jnp.abs(z)))
    log_not = -sp if mask is None else jnp.where(mask, -sp, 0.0)
    return z, sp, log_not


def _attn_fwd(u, t, p, n_heads):
    b = ATT_BLOCK
    nq = t // b
    q0 = p // HEAD_DIM

    def body(q_ref, k_ref, v_ref, o_ref, qb_ref, kb_ref, vb_ref):
        qb_ref[...] = q_ref[...].astype(BF16)
        kb_ref[...] = k_ref[...].astype(BF16)
        vb_ref[...] = v_ref[...].astype(BF16)
        tri, causal = _att_consts()

        def block(qb, ks, carry, acc, mask):
            kb = kb_ref[pl.ds(ks, b), :]
            vb = vb_ref[pl.ds(ks, b), :]
            z, sp, log_not = _att_scores(qb, kb, mask)
            incl = _suffix_sum(log_not, tri)
            a = jnp.exp(z - sp + (incl - log_not) + carry)
            if mask is not None:
                a = jnp.where(mask, a, 0.0)
            a_hi, a_lo = _split2(a)
            acc = acc + _dot(a_hi, vb) + _dot(a_lo, vb)
            return carry + incl[:, 0:1], acc

        def q_loop(i, _):
            qs = pl.multiple_of(i * b, b)
            qb = qb_ref[pl.ds(qs, b), :]
            carry, acc = block(qb, qs, jnp.zeros((b, 1), F32), jnp.zeros((b, HEAD_DIM), F32), causal)

            def k_loop(jj, c):
                ks = pl.multiple_of((i - 1 - jj) * b, b)
                return block(qb, ks, c[0], c[1], None)

            carry, acc = lax.fori_loop(0, i, k_loop, (carry, acc))
            o_ref[pl.ds(qs, b), :] = acc
            return 0

        lax.fori_loop(0, nq, q_loop, 0)

    def head(off):
        return pl.BlockSpec((t, HEAD_DIM), lambda h: (0, off + h))

    return pl.pallas_call(
        body, name="attn_fwd", grid=(n_heads,),
        in_specs=[head(q0), head(q0 + n_heads), head(q0 + 2 * n_heads)],
        out_specs=pl.BlockSpec((t, HEAD_DIM), lambda h: (0, h)),
        out_shape=jax.ShapeDtypeStruct((t, n_heads * HEAD_DIM), F32),
        scratch_shapes=[pltpu.VMEM((t, HEAD_DIM), BF16)] * 3,
        compiler_params=_cparams(("parallel",)),
    )(u, u, u)


def _attn_bwd(u, dmixin, o, t, p, n_heads):
    b = ATT_BLOCK
    nq = t // b
    q0 = p // HEAD_DIM
    scale = 1.0 / math.sqrt(HEAD_DIM)

    def body(q_ref, k_ref, v_ref, do_ref, o_ref, dq_ref, dk_ref, dv_ref, qb_ref, kb_ref, vb_ref, dob_ref):
        qb_ref[...] = q_ref[...].astype(BF16)
        kb_ref[...] = k_ref[...].astype(BF16)
        vb_ref[...] = v_ref[...].astype(BF16)
        dob_ref[...] = do_ref[...].astype(BF16)
        dk_ref[...] = jnp.zeros((t, HEAD_DIM), F32)
        dv_ref[...] = jnp.zeros((t, HEAD_DIM), F32)
        tri, causal = _att_consts()

        def block(qb, dob, total, ks, carry_l, carry_g, dq, mask):
            kb = kb_ref[pl.ds(ks, b), :]
            vb = vb_ref[pl.ds(ks, b), :]
            z, sp, log_not = _att_scores(qb, kb, mask)
            incl = _suffix_sum(log_not, tri)
            log_beta = z - sp
            a = jnp.exp(log_beta + (incl - log_not) + carry_l)
            if mask is not None:
                a = jnp.where(mask, a, 0.0)
            g = a * _dot(dob, vb, NT)
            g_incl = _suffix_sum(g, tri)
            g_before = total - (carry_g + g_incl)
            sig = jnp.exp(log_beta)
            dz = (g * (1.0 - sig) - sig * g_before) * scale
            if mask is not None:
                dz = jnp.where(mask, dz, 0.0)
            dz_b = dz.astype(BF16)
            dk_ref[pl.ds(ks, b), :] += _dot(dz_b, qb, TN)
            dv_ref[pl.ds(ks, b), :] += _dot(a.astype(BF16), dob, TN)
            return carry_l + incl[:, 0:1], carry_g + g_incl[:, 0:1], dq + _dot(dz_b, kb)

        def q_loop(i, _):
            qs = pl.multiple_of(i * b, b)
            qb = qb_ref[pl.ds(qs, b), :]
            dob = dob_ref[pl.ds(qs, b), :]
            total = jnp.sum(dob.astype(F32) * o_ref[pl.ds(qs, b), :], axis=-1, keepdims=True)
            zero = jnp.zeros((b, 1), F32)
            state = block(qb, dob, total, qs, zero, zero, jnp.zeros((b, HEAD_DIM), F32), causal)

            def k_loop(jj, c):
                ks = pl.multiple_of((i - 1 - jj) * b, b)
                return block(qb, dob, total, ks, c[0], c[1], c[2], None)

            state = lax.fori_loop(0, i, k_loop, state)
            dq_ref[pl.ds(qs, b), :] = state[2]
            return 0

        lax.fori_loop(0, nq, q_loop, 0)

    def head(off):
        return pl.BlockSpec((t, HEAD_DIM), lambda h: (0, off + h))

    out = pl.BlockSpec((t, HEAD_DIM), lambda h: (0, h))
    shape = jax.ShapeDtypeStruct((t, n_heads * HEAD_DIM), F32)
    return pl.pallas_call(
        body, name="attn_bwd", grid=(n_heads,),
        in_specs=[head(q0), head(q0 + n_heads), head(q0 + 2 * n_heads), head(q0), out],
        out_specs=[out, out, out], out_shape=[shape, shape, shape],
        scratch_shapes=[pltpu.VMEM((t, HEAD_DIM), BF16)] * 4,
        compiler_params=_cparams(("parallel",)),
    )(u, u, u, dmixin, o)


def _place():
    x, y, c = lax.axis_index("x"), lax.axis_index("y"), lax.axis_index("c")
    return x, y, c


def _all_gather_weights(shards):
    n = len(shards)

    def body(*refs):
        ins, outs = refs[:n], refs[n:2 * n]
        send_sems, recv_sems, local_sems = refs[2 * n:]
        x, y, c = _place()
        me, sibling = (x, y, c), (x, y, 1 - c)
        chips = [(1 - x, y), (x, 1 - y), (1 - x, 1 - y)]

        def slot(px, py, pc):
            return 4 * px + 2 * py + pc

        def copy(ti, k, block, to, src=None):
            dst = outs[ti].at[slot(*block)]
            return pltpu.make_async_remote_copy(
                src_ref=dst if src is None else src, dst_ref=dst,
                send_sem=send_sems.at[ti, k], recv_sem=recv_sems.at[ti, k], device_id=to, device_id_type=MESH)

        mine = [pltpu.make_async_copy(ins[ti], outs[ti].at[slot(*me)], local_sems.at[ti]) for ti in range(n)]
        for cp in mine:
            cp.start()
        first = []
        for ti in range(n):
            first.append(copy(ti, 0, me, sibling, src=ins[ti]))
            first += [copy(ti, 1 + j, me, (*chip, c), src=ins[ti]) for j, chip in enumerate(chips)]
        for cp in first:
            cp.start()
        passed = []
        for j, chip in enumerate(chips):
            for ti in range(n):
                copy(ti, 1 + j, (*chip, c), me).wait_recv()
                fwd = copy(ti, 4 + j, (*chip, c), sibling)
                fwd.start()
                passed.append(fwd)
        for ti in range(n):
            copy(ti, 0, sibling, me).wait_recv()
            for j, chip in enumerate(chips):
                copy(ti, 4 + j, (*chip, 1 - c), me).wait_recv()
        for cp in first + passed:
            cp.wait_send()
        for cp in mine:
            cp.wait()

    any_spec = pl.BlockSpec(memory_space=pl.ANY)
    return pl.pallas_call(
        body, name="all_gather_weights",
        in_specs=[any_spec] * n, out_specs=[any_spec] * n,
        out_shape=[jax.ShapeDtypeStruct((N_DEV, *s.shape), s.dtype) for s in shards],
        scratch_shapes=[pltpu.SemaphoreType.DMA((n, 7)), pltpu.SemaphoreType.DMA((n, 7)), pltpu.SemaphoreType.DMA((n,))],
        compiler_params=pltpu.CompilerParams(has_side_effects=True),
    )(*shards)


def _rs_to_sibling(partials):
    n = len(partials)

    def body(*refs):
        ins, outs = refs[:n], refs[n:2 * n]
        send_sems, recv_sems = refs[2 * n:]
        x, y, c = _place()
        sibling = (x, y, 1 - c)
        copies = []
        for ti in range(n):
            for r in range(4):
                ox, oy = (1 - x if r & 2 else x), (1 - y if r & 1 else y)
                cp = pltpu.make_async_remote_copy(
                    src_ref=ins[ti].at[4 * ox + 2 * oy + (1 - c)], dst_ref=outs[ti].at[r],
                    send_sem=send_sems.at[ti, r], recv_sem=recv_sems.at[ti, r], device_id=sibling, device_id_type=MESH)
                cp.start()
                copies.append(cp)
        for cp in copies:
            cp.wait_recv()
        for cp in copies:
            cp.wait_send()

    any_spec = pl.BlockSpec(memory_space=pl.ANY)
    return pl.pallas_call(
        body, name="rs_to_sibling",
        in_specs=[any_spec] * n, out_specs=[any_spec] * n,
        out_shape=[jax.ShapeDtypeStruct((4, *s.shape[1:]), s.dtype) for s in partials],
        scratch_shapes=[pltpu.SemaphoreType.DMA((n, 4)), pltpu.SemaphoreType.DMA((n, 4))],
        compiler_params=pltpu.CompilerParams(has_side_effects=True),
    )(*partials)


def _rs_to_owner(sums):
    n = len(sums)

    def body(*refs):
        ins, outs = refs[:n], refs[n:2 * n]
        send_sems, recv_sems = refs[2 * n:]
        x, y, c = _place()
        copies = []
        for ti in range(n):
            for r in range(1, 4):
                ox, oy = (1 - x if r & 2 else x), (1 - y if r & 1 else y)
                cp = pltpu.make_async_remote_copy(
                    src_ref=ins[ti].at[r], dst_ref=outs[ti].at[r],
                    send_sem=send_sems.at[ti, r], recv_sem=recv_sems.at[ti, r], device_id=(ox, oy, c), device_id_type=MESH)
                cp.start()
                copies.append(cp)
        for cp in copies:
            cp.wait_recv()
        for cp in copies:
            cp.wait_send()

    any_spec = pl.BlockSpec(memory_space=pl.ANY)
    return pl.pallas_call(
        body, name="rs_to_owner",
        in_specs=[any_spec] * n, out_specs=[any_spec] * n,
        out_shape=[jax.ShapeDtypeStruct(s.shape, s.dtype) for s in sums],
        scratch_shapes=[pltpu.SemaphoreType.DMA((n, 4)), pltpu.SemaphoreType.DMA((n, 4))],
        compiler_params=pltpu.CompilerParams(has_side_effects=True),
    )(*sums)


def _owner_slots():
    x, y, c = _place()
    idx = []
    for r in range(4):
        ox, oy = (1 - x if r & 2 else x), (1 - y if r & 1 else y)
        idx.append(4 * ox + 2 * oy + c)
    return jnp.stack(idx).astype(jnp.int32)


def _row_tile(rows, cols):
    tr = max(8, min(rows, (1 << 19) // cols))
    while rows % tr:
        tr //= 2
    return tr


def _rs_chip_sum(name, slots, partial, from_sibling):
    _, rows, cols = partial.shape
    tr = _row_tile(rows, cols)

    def body(slots_ref, p_ref, s_ref, o_ref):
        o_ref[...] = (p_ref[...] + s_ref[...]).astype(BF16)

    grid_spec = pltpu.PrefetchScalarGridSpec(
        num_scalar_prefetch=1, grid=(3, rows // tr),
        in_specs=[pl.BlockSpec((None, tr, cols), lambda r, i, s: (s[r + 1], i, 0)),
                  pl.BlockSpec((None, tr, cols), lambda r, i, s: (r + 1, i, 0))],
        out_specs=pl.BlockSpec((None, tr, cols), lambda r, i, s: (r + 1, i, 0)))
    return pl.pallas_call(
        body, name=name, grid_spec=grid_spec, out_shape=jax.ShapeDtypeStruct((4, rows, cols), BF16),
        compiler_params=_cparams(("parallel", "parallel")),
    )(slots, partial, from_sibling)


def _adamw(w, g, m, v):
    m = ADAM_B1 * m + (1.0 - ADAM_B1) * g
    v = ADAM_B2 * v + (1.0 - ADAM_B2) * (g * g)
    m_hat = m / (1.0 - ADAM_B1 ** ADAM_STEP)
    v_hat = v / (1.0 - ADAM_B2 ** ADAM_STEP)
    delta = -ADAM_LR * (m_hat / (jnp.sqrt(v_hat) + ADAM_EPS) + ADAM_WD * w)
    return delta, m, v


def _rs_final_adamw(name, slots, partial, from_sibling, from_chips, w, m, v):
    rows, cols = w.shape
    tr = _row_tile(rows, cols)

    def body(slots_ref, p_ref, s_ref, c1_ref, c2_ref, c3_ref, w_ref, m_ref, v_ref, g_ref, d_ref, nm_ref, nv_ref):
        g = p_ref[...] + s_ref[...]
        g = g + c1_ref[...].astype(F32)
        g = g + c2_ref[...].astype(F32)
        g = g + c3_ref[...].astype(F32)
        delta, nm, nv = _adamw(w_ref[...], g, m_ref[...], v_ref[...])
        g_ref[...] = g
        d_ref[...] = delta
        nm_ref[...] = nm
        nv_ref[...] = nv

    def slot(r):
        return pl.BlockSpec((None, tr, cols), lambda i, s: (r, i, 0))

    flat = pl.BlockSpec((tr, cols), lambda i, s: (i, 0))
    grid_spec = pltpu.PrefetchScalarGridSpec(
        num_scalar_prefetch=1, grid=(rows // tr,),
        in_specs=[pl.BlockSpec((None, tr, cols), lambda i, s: (s[0], i, 0)), slot(0), slot(1), slot(2), slot(3), flat, flat, flat],
        out_specs=[flat] * 4)
    return pl.pallas_call(
        body, name=name, grid_spec=grid_spec, out_shape=[jax.ShapeDtypeStruct((rows, cols), F32)] * 4,
        compiler_params=_cparams(("parallel",)),
    )(slots, partial, from_sibling, from_chips, from_chips, from_chips, w, m, v)


def _small_all_reduce(packet):
    rows, d = packet.shape

    def body(p_ref, sum_ref, loss_ref, all_ref, send_sems, recv_sems):
        x, y, c = _place()
        me = 4 * x + 2 * y + c
        all_ref[me] = p_ref[...]
        copies = []
        for k in range(1, N_DEV):
            px, py, pc = (1 - x if k & 4 else x), (1 - y if k & 2 else y), (1 - c if k & 1 else c)
            cp = pltpu.make_async_remote_copy(
                src_ref=p_ref, dst_ref=all_ref.at[me], send_sem=send_sems.at[k], recv_sem=recv_sems.at[k],
                device_id=(px, py, pc), device_id_type=MESH)
            cp.start()
            copies.append(cp)
        for cp in copies:
            cp.wait_recv()
        for cp in copies:
            cp.wait_send()
        total = all_ref[0]
        for j in range(1, N_DEV):
            total = total + all_ref[j]
        sum_ref[...] = total
        loss_ref[...] = jnp.sum(total[0:1, :], axis=-1, keepdims=True)

    vmem = pl.BlockSpec(memory_space=pltpu.VMEM)
    return pl.pallas_call(
        body, name="small_all_reduce",
        in_specs=[vmem], out_specs=[vmem, vmem],
        out_shape=[jax.ShapeDtypeStruct((rows, d), F32), jax.ShapeDtypeStruct((1, 1), F32)],
        scratch_shapes=[pltpu.VMEM((N_DEV, rows, d), F32), pltpu.SemaphoreType.DMA((N_DEV,)), pltpu.SemaphoreType.DMA((N_DEV,))],
        compiler_params=pltpu.CompilerParams(has_side_effects=True),
    )(packet)


def _small_adamw(w, g, m, v):
    def body(w_ref, g_ref, m_ref, v_ref, d_ref, nm_ref, nv_ref):
        delta, nm, nv = _adamw(w_ref[...], g_ref[...], m_ref[...], v_ref[...])
        d_ref[...] = delta
        nm_ref[...] = nm
        nv_ref[...] = nv

    vmem = pl.BlockSpec(memory_space=pltpu.VMEM)
    return pl.pallas_call(
        body, name="small_adamw", in_specs=[vmem] * 4, out_specs=[vmem] * 3,
        out_shape=[jax.ShapeDtypeStruct(w.shape, F32)] * 3,
    )(w, g, m, v)


def kernel(x, ln_in_g, ln_in_b, w_in, w_pool, pool_scale, w_out, ln1_g, ln1_b, w_ff1, b_ff1, w_ff2, b_ff2, ln2_g, ln2_b, loss_target, m_ln_in_g, m_ln_in_b, m_w_in, m_w_pool, m_pool_scale, m_w_out, m_ln1_g, m_ln1_b, m_w_ff1, m_b_ff1, m_w_ff2, m_b_ff2, m_ln2_g, m_ln2_b, v_ln_in_g, v_ln_in_b, v_w_in, v_w_pool, v_pool_scale, v_w_out, v_ln1_g, v_ln1_b, v_w_ff1, v_b_ff1, v_w_ff2, v_b_ff2, v_ln2_g, v_ln2_b):
    t, d = x.shape[1], x.shape[2]
    n_groups = len(POOL_WINDOWS)
    c_pool = w_pool.shape[3]
    p = n_groups * c_pool
    n_heads = (d - p) // HEAD_DIM
    ws_in = w_in.shape[2]
    n_in = N_DEV * ws_in
    ws_out = w_out.shape[1]
    ws_f = w_ff1.shape[2]
    f = N_DEV * ws_f
    pr = w_pool.shape[2]
    assert n_in == p + 3 * n_heads * HEAD_DIM and N_DEV * ws_out == d and N_DEV * pr == c_pool

    tm_big = min(t, 1024)
    tm_ep = min(t, 512)
    tkk = min(t, 512)
    half_f = min(ws_f, 512)
    per_f = ws_f // half_f

    x2 = x.reshape(t, d)
    target = loss_target.reshape(t, d)
    g0, b0 = ln_in_g.reshape(1, d), ln_in_b.reshape(1, d)

    shards = [w_in.reshape(d, ws_in), w_out.reshape(ws_out, d), w_ff1.reshape(d, ws_f), w_ff2.reshape(ws_f, d),
              w_pool.reshape(n_groups * pr, c_pool)]
    win_g, wout_g, w1_g, w2_g, wpool_g, scale_g = _all_gather_weights(
        [s.astype(BF16) for s in shards] + [pool_scale.reshape(n_groups, pr)])
    wout_2d = wout_g.reshape(d, d)
    w2_2d = w2_g.reshape(f, d)
    wp_full = wpool_g.reshape(N_DEV, n_groups, pr, c_pool).transpose(1, 0, 2, 3).reshape(n_groups, c_pool, c_pool)
    sc_full = scale_g.transpose(1, 0, 2).reshape(n_groups, 1, c_pool)
    x_, y_, c_ = _place()
    me = 4 * x_ + 2 * y_ + c_

    def sds(shape, dtype=F32):
        return jax.ShapeDtypeStruct(shape, dtype)

    vec = pl.BlockSpec((1, d), lambda m, n, k: (0, 0))
    row_ep = pl.BlockSpec((tm_ep, d), lambda m, n, k: (m, 0))
    seq = ("arbitrary", "arbitrary", "arbitrary")

    h0, h0b = _ln_in_fwd(x2, g0, b0, tm_big)

    u = _matmul(
        "mm_u", h0b, win_g, dims=NN, grid=(t // tm_big, N_DEV, 1),
        a_spec=pl.BlockSpec((tm_big, d), lambda m, n, k: (m, 0)),
        b_spec=pl.BlockSpec((None, d, ws_in), lambda m, n, k: (n, 0, 0)),
        out_shape=[sds((t, n_in))],
        out_specs=[pl.BlockSpec((tm_big, ws_in), lambda m, n, k: (m, n))],
        acc_shape=(tm_big, ws_in), epilogue=_store_epilogue(F32))[0]

    y_pool, ypre = _pool_fwd(u, wp_full, sc_full, t, c_pool)
    o = _attn_fwd(u, t, p, n_heads)
    mixin = jnp.concatenate([y_pool, o.astype(BF16)], axis=1)

    def ep_ln1(acc, ex, outs):
        h0_ref, g_ref, b_ref = ex
        r1 = DEEPNORM_ALPHA * h0_ref[...] + acc
        xhat, _ = _ln_stats(r1)
        h1 = xhat * g_ref[...] + b_ref[...]
        outs[0][...] = r1
        outs[1][...] = h1
        outs[2][...] = h1.astype(BF16)

    r1, h1, h1b = _matmul(
        "mm_mix_ln1", mixin, wout_g, dims=NN, grid=(t // tm_ep, 1, N_DEV),
        a_spec=pl.BlockSpec((tm_ep, ws_out), lambda m, n, k: (m, k)),
        b_spec=pl.BlockSpec((None, ws_out, d), lambda m, n, k: (k, 0, 0)),
        extras=(h0, ln1_g, ln1_b), extra_specs=(row_ep, vec, vec),
        out_shape=[sds((t, d)), sds((t, d)), sds((t, d), BF16)], out_specs=[row_ep] * 3,
        acc_shape=(tm_ep, d), epilogue=ep_ln1)

    def ep_ff1(acc, ex, outs):
        f1 = acc + ex[0][...]
        outs[0][...] = f1
        r = jnp.maximum(f1, 0.0)
        outs[1][...] = (r * r).astype(BF16)

    ff_tile = pl.BlockSpec((tm_big, half_f), lambda m, n, k: (m, n))
    f1, act = _matmul(
        "mm_ff1", h1b, w1_g, dims=NN, grid=(t // tm_big, f // half_f, 1),
        a_spec=pl.BlockSpec((tm_big, d), lambda m, n, k: (m, 0)),
        b_spec=pl.BlockSpec((None, d, half_f), lambda m, n, k: (n // per_f, 0, n % per_f)),
        extras=(b_ff1,), extra_specs=(pl.BlockSpec((1, half_f), lambda m, n, k: (0, n)),),
        out_shape=[sds((t, f)), sds((t, f), BF16)], out_specs=[ff_tile, ff_tile],
        acc_shape=(tm_big, half_f), epilogue=ep_ff1)

    def ep_ln2(acc, ex, outs):
        h1_ref, tgt_ref, bf2_ref, g_ref, b_ref = ex
        dr2_ref, dr2b_ref, dg_ref, db_ref, dbf2_ref, loss_ref = outs
        r2 = DEEPNORM_ALPHA * h1_ref[...] + (acc + bf2_ref[...])
        xhat, rstd = _ln_stats(r2)
        err = xhat * g_ref[...] + b_ref[...] - tgt_ref[...]
        dr2, dg, db = _ln_bwd(err * (1.0 / d), xhat, rstd, g_ref[...])
        dr2_ref[...] = dr2
        dr2b_ref[...] = dr2.astype(BF16)
        first = pl.program_id(0) == 0
        _acc_rows(first, dg_ref, dg)
        _acc_rows(first, db_ref, db)
        _acc_rows(first, dbf2_ref, jnp.sum(dr2, axis=0, keepdims=True))
        _acc_rows(first, loss_ref, jnp.sum(err * err, axis=0, keepdims=True) * (0.5 / d))

    dr2, dr2b, dg2, db2, dbf2, loss_vec = _matmul(
        "mm_ff2_ln2_loss", act, w2_g, dims=NN, grid=(t // tm_ep, 1, f // half_f),
        a_spec=pl.BlockSpec((tm_ep, half_f), lambda m, n, k: (m, k)),
        b_spec=pl.BlockSpec((None, half_f, d), lambda m, n, k: (k // per_f, k % per_f, 0)),
        extras=(h1, target, b_ff2, ln2_g, ln2_b), extra_specs=(row_ep, row_ep, vec, vec, vec),
        out_shape=[sds((t, d)), sds((t, d), BF16)] + [sds((1, d))] * 4, out_specs=[row_ep, row_ep, vec, vec, vec, vec],
        acc_shape=(tm_ep, d), epilogue=ep_ln2, sem=seq)

    def ep_dff1(acc, ex, outs):
        df1 = acc * (2.0 * jnp.maximum(ex[0][...], 0.0))
        outs[0][...] = df1.astype(BF16)
        _acc_rows(pl.program_id(1) == 0, outs[1], jnp.sum(df1, axis=0, keepdims=True))

    df_tile = pl.BlockSpec((tm_big, ws_f), lambda n, m, k: (m, n))
    df1b, dbf1 = _matmul(
        "mm_dff1", dr2b, w2_g, dims=NT, grid=(N_DEV, t // tm_big, 1),
        a_spec=pl.BlockSpec((tm_big, d), lambda n, m, k: (m, 0)),
        b_spec=pl.BlockSpec((None, ws_f, d), lambda n, m, k: (n, 0, 0)),
        extras=(f1,), extra_specs=(df_tile,),
        out_shape=[sds((t, f), BF16), sds((1, f))], out_specs=[df_tile, pl.BlockSpec((1, ws_f), lambda n, m, k: (0, n))],
        acc_shape=(tm_big, ws_f), epilogue=ep_dff1, sem=("parallel", "arbitrary", "arbitrary"))

    tn_d = min(d, 1024)
    dw2 = _matmul(
        "mm_dw2", act, dr2b, dims=TN, grid=(N_DEV, d // tn_d, t // tkk),
        a_spec=pl.BlockSpec((tkk, ws_f), lambda m, n, k: (k, m)),
        b_spec=pl.BlockSpec((tkk, tn_d), lambda m, n, k: (k, n)),
        out_shape=[sds((N_DEV, ws_f, d))], out_specs=[pl.BlockSpec((None, ws_f, tn_d), lambda m, n, k: (m, 0, n))],
        acc_shape=(ws_f, tn_d), epilogue=_store_epilogue(F32))[0]

    dw1 = _matmul(
        "mm_dw1", h1b, df1b, dims=TN, grid=(d // tn_d, N_DEV, t // tkk),
        a_spec=pl.BlockSpec((tkk, tn_d), lambda m, n, k: (k, m)),
        b_spec=pl.BlockSpec((tkk, ws_f), lambda m, n, k: (k, n)),
        out_shape=[sds((N_DEV, d, ws_f))], out_specs=[pl.BlockSpec((None, tn_d, ws_f), lambda m, n, k: (n, m, 0))],
        acc_shape=(tn_d, ws_f), epilogue=_store_epilogue(F32))[0]

    def ep_ln1_bwd(acc, ex, outs):
        dr2_ref, r1_ref, g_ref = ex
        xhat, rstd = _ln_stats(r1_ref[...])
        dr1, dg, db = _ln_bwd(DEEPNORM_ALPHA * dr2_ref[...] + acc, xhat, rstd, g_ref[...])
        outs[0][...] = dr1
        outs[1][...] = dr1.astype(BF16)
        first = pl.program_id(0) == 0
        _acc_rows(first, outs[2], dg)
        _acc_rows(first, outs[3], db)

    dr1, dr1b, dg1, db1 = _matmul(
        "mm_dh1_ln1_bwd", df1b, w1_g, dims=NT, grid=(t // tm_ep, 1, f // half_f),
        a_spec=pl.BlockSpec((tm_ep, half_f), lambda m, n, k: (m, k)),
        b_spec=pl.BlockSpec((None, d, half_f), lambda m, n, k: (k // per_f, 0, k % per_f)),
        extras=(dr2, r1, ln1_g), extra_specs=(row_ep, row_ep, vec),
        out_shape=[sds((t, d)), sds((t, d), BF16), sds((1, d)), sds((1, d))], out_specs=[row_ep, row_ep, vec, vec],
        acc_shape=(tm_ep, d), epilogue=ep_ln1_bwd, sem=seq)

    dwout = _matmul(
        "mm_dwout", mixin, dr1b, dims=TN, grid=(d // tn_d, d // tn_d, t // tkk),
        a_spec=pl.BlockSpec((tkk, tn_d), lambda m, n, k: (k, m)),
        b_spec=pl.BlockSpec((tkk, tn_d), lambda m, n, k: (k, n)),
        out_shape=[sds((d, d))], out_specs=[pl.BlockSpec((tn_d, tn_d), lambda m, n, k: (m, n))],
        acc_shape=(tn_d, tn_d), epilogue=_store_epilogue(F32))[0]

    dmixin = _matmul(
        "mm_dmixin", dr1b, wout_2d, dims=NT, grid=(t // tm_big, d // tn_d, 1),
        a_spec=pl.BlockSpec((tm_big, d), lambda m, n, k: (m, 0)),
        b_spec=pl.BlockSpec((tn_d, d), lambda m, n, k: (n, 0)),
        out_shape=[sds((t, d))], out_specs=[pl.BlockSpec((tm_big, tn_d), lambda m, n, k: (m, n))],
        acc_shape=(tm_big, tn_d), epilogue=_store_epilogue(F32))[0]

    du_pool, dwp, dsc = _pool_bwd(dmixin, ypre, wp_full, sc_full, t, c_pool)
    dq, dk, dv = _attn_bwd(u, dmixin, o, t, p, n_heads)
    dub = jnp.concatenate([du_pool, dq, dk, dv], axis=1).astype(BF16)

    dwin = _matmul(
        "mm_dwin", h0b, dub, dims=TN, grid=(d // tn_d, N_DEV, t // tkk),
        a_spec=pl.BlockSpec((tkk, tn_d), lambda m, n, k: (k, m)),
        b_spec=pl.BlockSpec((tkk, ws_in), lambda m, n, k: (k, n)),
        out_shape=[sds((N_DEV, d, ws_in))], out_specs=[pl.BlockSpec((None, tn_d, ws_in), lambda m, n, k: (n, m, 0))],
        acc_shape=(tn_d, ws_in), epilogue=_store_epilogue(F32))[0]

    def ep_ln0_bwd(acc, ex, outs):
        dr1_ref, x_ref, g_ref = ex
        xhat, rstd = _ln_stats(x_ref[...])
        dx, dg, db = _ln_bwd(DEEPNORM_ALPHA * dr1_ref[...] + acc, xhat, rstd, g_ref[...])
        outs[0][...] = dx
        first = pl.program_id(0) == 0
        _acc_rows(first, outs[1], dg)
        _acc_rows(first, outs[2], db)

    dx, dg0, db0 = _matmul(
        "mm_dh0_ln0_bwd", dub, win_g, dims=NT, grid=(t // tm_ep, 1, N_DEV),
        a_spec=pl.BlockSpec((tm_ep, ws_in), lambda m, n, k: (m, k)),
        b_spec=pl.BlockSpec((None, d, ws_in), lambda m, n, k: (k, 0, 0)),
        extras=(dr1, x2, g0), extra_specs=(row_ep, row_ep, vec),
        out_shape=[sds((t, d)), sds((1, d)), sds((1, d))], out_specs=[row_ep, vec, vec],
        acc_shape=(tm_ep, d), epilogue=ep_ln0_bwd, sem=seq)

    dwp_g = dwp.reshape(n_groups, N_DEV, pr, c_pool).transpose(1, 0, 2, 3).reshape(N_DEV, n_groups * pr, c_pool)
    partials = [dwin, dwout.reshape(N_DEV, ws_out, d), dw1, dw2, dwp_g]
    names = ["w_in", "w_out", "w_ff1", "w_ff2", "w_pool"]
    from_sibling = _rs_to_sibling(partials)
    slots = _owner_slots()
    chip_sums = [_rs_chip_sum("rs_chip_sum_" + nm, slots, pt, fs) for nm, pt, fs in zip(names, partials, from_sibling)]
    from_chips = _rs_to_owner(chip_sums)
    moments = [(m_w_in, v_w_in), (m_w_out, v_w_out), (m_w_ff1, v_w_ff1), (m_w_ff2, v_w_ff2), (m_w_pool, v_w_pool)]
    big = {}
    for nm, pt, fs, fc, w2d, (m_, v_) in zip(names, partials, from_sibling, from_chips, shards, moments):
        res = _rs_final_adamw("rs_final_adamw_" + nm, slots, pt, fs, fc, w2d, m_.reshape(w2d.shape), v_.reshape(w2d.shape))
        big[nm] = res

    n_f_rows = f // d
    pad_sc = d - p
    packet = jnp.concatenate(
        [loss_vec, dg0, db0, dg1, db1, dbf2, dg2, db2, dbf1.reshape(n_f_rows, d),
         jnp.pad(dsc.reshape(1, p), ((0, 0), (0, pad_sc)))], axis=0)
    n_rows = packet.shape[0]
    n_pad = (-n_rows) % 8
    packet = jnp.pad(packet, ((0, n_pad), (0, 0)))
    sums, loss11 = _small_all_reduce(packet)
    dsc_full = sums[8 + n_f_rows, :p].reshape(n_groups, N_DEV, pr)
    dsc_mine = lax.dynamic_index_in_dim(dsc_full, me, axis=1, keepdims=False)

    def sc_row(a):
        return jnp.pad(a.reshape(1, n_groups * pr), ((0, 0), (0, d - n_groups * pr)))

    def small_pack(ln0g, ln0b, l1g, l1b, bf2, l2g, l2b, bf1, sc):
        rows = [jnp.zeros((1, d), F32), ln0g.reshape(1, d), ln0b.reshape(1, d), l1g, l1b, bf2, l2g, l2b,
                bf1.reshape(n_f_rows, d), sc_row(sc), jnp.zeros((n_pad, d), F32)]
        return jnp.concatenate(rows, axis=0)

    w_small = small_pack(ln_in_g, ln_in_b, ln1_g, ln1_b, b_ff2, ln2_g, ln2_b, b_ff1, pool_scale)
    m_small = small_pack(m_ln_in_g, m_ln_in_b, m_ln1_g, m_ln1_b, m_b_ff2, m_ln2_g, m_ln2_b, m_b_ff1, m_pool_scale)
    v_small = small_pack(v_ln_in_g, v_ln_in_b, v_ln1_g, v_ln1_b, v_b_ff2, v_ln2_g, v_ln2_b, v_b_ff1, v_pool_scale)
    g_small = jnp.concatenate([sums[:8 + n_f_rows], sc_row(dsc_mine), jnp.zeros((n_pad, d), F32)], axis=0)
    small = (g_small,) + tuple(_small_adamw(w_small, g_small, m_small, v_small))

    def unpack(a):
        sc = a[8 + n_f_rows, :n_groups * pr].reshape(1, n_groups, pr)
        return {"ln_in_g": a[1], "ln_in_b": a[2], "ln1_g": a[3:4], "ln1_b": a[4:5], "b_ff2": a[5:6], "ln2_g": a[6:7],
                "ln2_b": a[7:8], "b_ff1": a[8:8 + n_f_rows].reshape(1, f), "pool_scale": sc}

    shapes = {"w_in": w_in.shape, "w_out": w_out.shape, "w_ff1": w_ff1.shape, "w_ff2": w_ff2.shape, "w_pool": w_pool.shape}
    order = ["ln_in_g", "ln_in_b", "w_in", "w_pool", "pool_scale", "w_out", "ln1_g", "ln1_b", "w_ff1", "b_ff1", "w_ff2",
             "b_ff2", "ln2_g", "ln2_b"]
    outs = []
    for kind in range(4):
        small_k = unpack(small[kind])
        for nm in order:
            outs.append(big[nm][kind].reshape(shapes[nm]) if nm in big else small_k[nm])
    return (loss11.reshape(()), dx.reshape(x.shape), *outs)
```

```python
import functools
import math

import jax
import jax.numpy as jnp
from jax import lax
from jax.experimental import pallas as pl
from jax.experimental.pallas import tpu as pltpu

F32 = jnp.float32
BF16 = jnp.bfloat16
MESH = pl.DeviceIdType.MESH

N_DEV = 8
HEAD_DIM = 128
POOL_WINDOWS = (2, 4, 8, 16)
DEEPNORM_ALPHA = (2.0 * 1) ** 0.25
LN_EPS = 1e-5
ADAM_LR = 0.001
ADAM_B1 = 0.9
ADAM_B2 = 0.999
ADAM_EPS = 1e-08
ADAM_WD = 0.01
ADAM_STEP = 10

V7X_VMEM_LIMIT = 56 * 1024 * 1024
ATT_BLOCK = 256
POOL_CHUNK = 128

NN = (((1,), (0,)), ((), ()))
NT = (((1,), (1,)), ((), ()))
TN = (((0,), (0,)), ((), ()))


def _dot(a, b, dims=NN):
    return lax.dot_general(a, b, dims, preferred_element_type=F32)


def _cparams(sem=None):
    return pltpu.CompilerParams(dimension_semantics=sem, vmem_limit_bytes=V7X_VMEM_LIMIT)


def _ln_stats(r):
    mu = jnp.mean(r, axis=-1, keepdims=True)
    xc = r - mu
    var = jnp.mean(xc * xc, axis=-1, keepdims=True)
    rstd = lax.rsqrt(var + LN_EPS)
    return xc * rstd, rstd


def _ln_bwd(dy, xhat, rstd, g):
    dxh = dy * g
    m1 = jnp.mean(dxh, axis=-1, keepdims=True)
    m2 = jnp.mean(dxh * xhat, axis=-1, keepdims=True)
    dx = rstd * (dxh - m1 - xhat * m2)
    dg = jnp.sum(dy * xhat, axis=0, keepdims=True)
    db = jnp.sum(dy, axis=0, keepdims=True)
    return dx, dg, db


def _acc_rows(first, ref, val):
    @pl.when(first)
    def _():
        ref[...] = val

    @pl.when(jnp.logical_not(first))
    def _():
        ref[...] += val


def _matmul(name, a, b, *, dims, grid, a_spec, b_spec, extras=(), extra_specs=(), out_shape, out_specs,
            acc_shape, epilogue, k_axis=2, sem=("parallel", "parallel", "arbitrary")):
    nk = grid[k_axis]
    n_extra = len(extras)
    n_out = len(out_shape)

    def body(a_ref, b_ref, *rest):
        extra_refs = rest[:n_extra]
        out_refs = rest[n_extra:n_extra + n_out]
        if nk == 1:
            epilogue(_dot(a_ref[...], b_ref[...], dims), extra_refs, out_refs)
            return
        acc_ref = rest[n_extra + n_out]
        k = pl.program_id(k_axis)

        @pl.when(k == 0)
        def _():
            acc_ref[...] = jnp.zeros(acc_shape, F32)

        acc_ref[...] += _dot(a_ref[...], b_ref[...], dims)

        @pl.when(k == nk - 1)
        def _():
            epilogue(acc_ref[...], extra_refs, out_refs)

    return pl.pallas_call(
        body, name=name, grid=grid,
        in_specs=[a_spec, b_spec, *extra_specs],
        out_specs=list(out_specs), out_shape=list(out_shape),
        scratch_shapes=[] if nk == 1 else [pltpu.VMEM(acc_shape, F32)],
        compiler_params=_cparams(sem),
    )(a, b, *extras)


def _store_epilogue(dtype):
    def ep(acc, extra_refs, out_refs):
        out_refs[0][...] = acc.astype(dtype)
    return ep


def _ln_in_fwd(x, g, b, tm):
    t, d = x.shape

    def body(x_ref, g_ref, b_ref, h_ref, hb_ref):
        xhat, _ = _ln_stats(x_ref[...])
        h = xhat * g_ref[...] + b_ref[...]
        h_ref[...] = h
        hb_ref[...] = h.astype(BF16)

    row = pl.BlockSpec((tm, d), lambda i: (i, 0))
    vec = pl.BlockSpec((1, d), lambda i: (0, 0))
    return pl.pallas_call(
        body, name="ln_in_fwd", grid=(t // tm,), in_specs=[row, vec, vec], out_specs=[row, row],
        out_shape=[jax.ShapeDtypeStruct((t, d), F32), jax.ShapeDtypeStruct((t, d), BF16)],
        compiler_params=_cparams(("parallel",)),
    )(x, g, b)


def _split3(x):
    hi = x.astype(BF16)
    r = x - hi.astype(F32)
    mid = r.astype(BF16)
    lo = (r - mid.astype(F32)).astype(BF16)
    return hi, mid, lo


def _split2(x):
    hi = x.astype(BF16)
    lo = (x - hi.astype(F32)).astype(BF16)
    return hi, lo


def _pool_fwd(u, wp, sc, t, c):
    n_groups = len(POOL_WINDOWS)
    tc = POOL_CHUNK
    n_chunks = t // tc

    def body(u_ref, wp_ref, sc_ref, y_ref, ypre_ref, xp_ref):
        g = pl.program_id(0)
        xp_ref[pl.ds(0, tc), :] = jnp.zeros((tc, c), F32)
        xp_ref[pl.ds(tc, t), :] = u_ref[...]
        out_i = lax.broadcasted_iota(jnp.int32, (tc, 2 * tc), 0)
        in_j = lax.broadcasted_iota(jnp.int32, (tc, 2 * tc), 1)
        lag = tc + out_i - in_j
        t_in_chunk = lax.broadcasted_iota(jnp.int32, (tc, 1), 0)
        for gi, w in enumerate(POOL_WINDOWS):
            @pl.when(g == gi)
            def _(w=w):
                band = jnp.logical_and(lag >= 0, lag < w).astype(BF16)

                def chunk(ci, carry):
                    start = pl.multiple_of(ci * tc, tc)
                    win = xp_ref[pl.ds(start, 2 * tc), :]
                    hi, mid, lo = _split3(win)
                    wsum = _dot(band, hi) + _dot(band, mid) + _dot(band, lo)
                    cnt = jnp.minimum(ci * tc + t_in_chunk + 1, w).astype(F32)
                    ypre = wsum * (1.0 / cnt) - win[tc:, :]
                    ypre_b = ypre.astype(BF16)
                    y = _dot(ypre_b, wp_ref[...]) * sc_ref[...]
                    ypre_ref[pl.ds(start, tc), :] = ypre_b
                    y_ref[pl.ds(start, tc), :] = y.astype(BF16)
                    return carry

                lax.fori_loop(0, n_chunks, chunk, 0)

    col = pl.BlockSpec((t, c), lambda g: (0, g))
    return pl.pallas_call(
        body, name="pool_fwd", grid=(n_groups,),
        in_specs=[col, pl.BlockSpec((None, c, c), lambda g: (g, 0, 0)), pl.BlockSpec((None, 1, c), lambda g: (g, 0, 0))],
        out_specs=[col, col],
        out_shape=[jax.ShapeDtypeStruct((t, n_groups * c), BF16), jax.ShapeDtypeStruct((t, n_groups * c), BF16)],
        scratch_shapes=[pltpu.VMEM((t + tc, c), F32)],
        compiler_params=_cparams(("parallel",)),
    )(u, wp, sc)


def _pool_bwd(dmixin, ypre, wp, sc, t, c):
    n_groups = len(POOL_WINDOWS)
    tc = POOL_CHUNK
    n_chunks = t // tc

    def body(dy_ref, ypre_ref, wp_ref, sc_ref, du_ref, dwp_ref, dsc_ref, zp_ref):
        g = pl.program_id(0)
        zp_ref[pl.ds(t, tc), :] = jnp.zeros((tc, c), F32)
        dwp_ref[...] = jnp.zeros((c, c), F32)
        dsc_ref[...] = jnp.zeros((1, c), F32)
        out_i = lax.broadcasted_iota(jnp.int32, (tc, 2 * tc), 0)
        in_j = lax.broadcasted_iota(jnp.int32, (tc, 2 * tc), 1)
        lead = in_j - out_i
        t_in_chunk = lax.broadcasted_iota(jnp.int32, (tc, 1), 0)
        for gi, w in enumerate(POOL_WINDOWS):
            @pl.when(g == gi)
            def _(w=w):
                band = jnp.logical_and(lead >= 0, lead < w).astype(BF16)

                def first(ci, carry):
                    start = pl.multiple_of(ci * tc, tc)
                    dy = dy_ref[pl.ds(start, tc), :]
                    yp = ypre_ref[pl.ds(start, tc), :]
                    ymm = _dot(yp, wp_ref[...])
                    dsc_ref[...] += jnp.sum(dy * ymm, axis=0, keepdims=True)
                    dys_b = (dy * sc_ref[...]).astype(BF16)
                    dwp_ref[...] += _dot(yp, dys_b, TN)
                    dyp = _dot(dys_b, wp_ref[...], NT)
                    cnt = jnp.minimum(ci * tc + t_in_chunk + 1, w).astype(F32)
                    zp_ref[pl.ds(start, tc), :] = dyp * (1.0 / cnt)
                    du_ref[pl.ds(start, tc), :] = -dyp
                    return carry

                lax.fori_loop(0, n_chunks, first, 0)

                def second(ci, carry):
                    start = pl.multiple_of(ci * tc, tc)
                    hi, mid, lo = _split3(zp_ref[pl.ds(start, 2 * tc), :])
                    du_ref[pl.ds(start, tc), :] += _dot(band, hi) + _dot(band, mid) + _dot(band, lo)
                    return carry

                lax.fori_loop(0, n_chunks, second, 0)

    col = pl.BlockSpec((t, c), lambda g: (0, g))
    return pl.pallas_call(
        body, name="pool_bwd", grid=(n_groups,),
        in_specs=[col, col, pl.BlockSpec((None, c, c), lambda g: (g, 0, 0)), pl.BlockSpec((None, 1, c), lambda g: (g, 0, 0))],
        out_specs=[col, pl.BlockSpec((None, c, c), lambda g: (g, 0, 0)), pl.BlockSpec((None, 1, c), lambda g: (g, 0, 0))],
        out_shape=[jax.ShapeDtypeStruct((t, n_groups * c), F32), jax.ShapeDtypeStruct((n_groups, c, c), F32),
                   jax.ShapeDtypeStruct((n_groups, 1, c), F32)],
        scratch_shapes=[pltpu.VMEM((t + tc, c), F32)],
        compiler_params=_cparams(("parallel",)),
    )(dmixin, ypre, wp, sc)


def _att_consts():
    b = ATT_BLOCK
    row = lax.broadcasted_iota(jnp.int32, (b, b), 0)
    col = lax.broadcasted_iota(jnp.int32, (b, b), 1)
    tri = (row >= col).astype(BF16)
    causal = col < row
    return tri, causal


def _suffix_sum(x, tri):
    hi, lo = _split2(x)
    return _dot(hi, tri) + _dot(lo, tri)


def _att_scores(qb, kb, mask):
    z = _dot(qb, kb, NT) * (1.0 / math.sqrt(HEAD_DIM))
    sp = jnp.maximum(z, 0.0) + jnp.log(1.0 + jnp.exp(-jnp.abs(z)))
    log_not = -sp if mask is None else jnp.where(mask, -sp, 0.0)
    return z, sp, log_not


def _attn_fwd(u, t, p, n_heads):
    b = ATT_BLOCK
    nq = t // b
    q0 = p // HEAD_DIM

    def body(q_ref, k_ref, v_ref, o_ref, qb_ref, kb_ref, vb_ref):
        qb_ref[...] = q_ref[...].astype(BF16)
        kb_ref[...] = k_ref[...].astype(BF16)
        vb_ref[...] = v_ref[...].astype(BF16)
        tri, causal = _att_consts()

        def block(qb, ks, carry, acc, mask):
            kb = kb_ref[pl.ds(ks, b), :]
            vb = vb_ref[pl.ds(ks, b), :]
            z, sp, log_not = _att_scores(qb, kb, mask)
            incl = _suffix_sum(log_not, tri)
            a = jnp.exp(z - sp + (incl - log_not) + carry)
            if mask is not None:
                a = jnp.where(mask, a, 0.0)
            a_hi, a_lo = _split2(a)
            acc = acc + _dot(a_hi, vb) + _dot(a_lo, vb)
            return carry + incl[:, 0:1], acc

        def q_loop(i, _):
            qs = pl.multiple_of(i * b, b)
            qb = qb_ref[pl.ds(qs, b), :]
            carry, acc = block(qb, qs, jnp.zeros((b, 1), F32), jnp.zeros((b, HEAD_DIM), F32), causal)

            def k_loop(jj, c):
                ks = pl.multiple_of((i - 1 - jj) * b, b)
                return block(qb, ks, c[0], c[1], None)

            carry, acc = lax.fori_loop(0, i, k_loop, (carry, acc))
            o_ref[pl.ds(qs, b), :] = acc
            return 0

        lax.fori_loop(0, nq, q_loop, 0)

    def head(off):
        return pl.BlockSpec((t, HEAD_DIM), lambda h: (0, off + h))

    return pl.pallas_call(
        body, name="attn_fwd", grid=(n_heads,),
        in_specs=[head(q0), head(q0 + n_heads), head(q0 + 2 * n_heads)],
        out_specs=pl.BlockSpec((t, HEAD_DIM), lambda h: (0, h)),
        out_shape=jax.ShapeDtypeStruct((t, n_heads * HEAD_DIM), F32),
        scratch_shapes=[pltpu.VMEM((t, HEAD_DIM), BF16)] * 3,
        compiler_params=_cparams(("parallel",)),
    )(u, u, u)


def _attn_bwd(u, dmixin, o, t, p, n_heads):
    b = ATT_BLOCK
    nq = t // b
    q0 = p // HEAD_DIM
    scale = 1.0 / math.sqrt(HEAD_DIM)

    def body(q_ref, k_ref, v_ref, do_ref, o_ref, dq_ref, dk_ref, dv_ref,
             qb_ref, kb_ref, vb_ref, dob_ref, qt_ref, dot_ref, dkt_ref, dvt_ref):
        qb_ref[...] = q_ref[...].astype(BF16)
        kb_ref[...] = k_ref[...].astype(BF16)
        vb_ref[...] = v_ref[...].astype(BF16)
        dob_ref[...] = do_ref[...].astype(BF16)
        for j in range(nq):
            rows = pl.ds(j * b, b)
            qt_ref[j] = q_ref[rows, :].T.astype(BF16)
            dot_ref[j] = do_ref[rows, :].T.astype(BF16)
        dkt_ref[...] = jnp.zeros((nq, HEAD_DIM, b), F32)
        dvt_ref[...] = jnp.zeros((nq, HEAD_DIM, b), F32)
        tri, causal = _att_consts()

        def block(i, qb, dob, total, j, carry_l, carry_g, dq, mask):
            ks = pl.multiple_of(j * b, b)
            kb = kb_ref[pl.ds(ks, b), :]
            vb = vb_ref[pl.ds(ks, b), :]
            z, sp, log_not = _att_scores(qb, kb, mask)
            incl = _suffix_sum(log_not, tri)
            log_beta = z - sp
            a = jnp.exp(log_beta + (incl - log_not) + carry_l)
            if mask is not None:
                a = jnp.where(mask, a, 0.0)
            g = a * _dot(dob, vb, NT)
            g_incl = _suffix_sum(g, tri)
            g_before = total - (carry_g + g_incl)
            sig = jnp.exp(log_beta)
            dz = (g * (1.0 - sig) - sig * g_before) * scale
            if mask is not None:
                dz = jnp.where(mask, dz, 0.0)
            dz_b = dz.astype(BF16)
            dkt_ref[j] += _dot(qt_ref[i], dz_b)
            dvt_ref[j] += _dot(dot_ref[i], a.astype(BF16))
            return carry_l + incl[:, 0:1], carry_g + g_incl[:, 0:1], dq + _dot(dz_b, kb)

        def q_loop(i, _):
            qs = pl.multiple_of(i * b, b)
            qb = qb_ref[pl.ds(qs, b), :]
            dob = dob_ref[pl.ds(qs, b), :]
            total = jnp.sum(dob.astype(F32) * o_ref[pl.ds(qs, b), :], axis=-1, keepdims=True)
            zero = jnp.zeros((b, 1), F32)
            state = block(i, qb, dob, total, i, zero, zero, jnp.zeros((b, HEAD_DIM), F32), causal)

            def k_loop(jj, c):
                return block(i, qb, dob, total, i - 1 - jj, c[0], c[1], c[2], None)

            state = lax.fori_loop(0, i, k_loop, state)
            dq_ref[pl.ds(qs, b), :] = state[2]
            return 0

        lax.fori_loop(0, nq, q_loop, 0)
        for j in range(nq):
            rows = pl.ds(j * b, b)
            dk_ref[rows, :] = dkt_ref[j].T
            dv_ref[rows, :] = dvt_ref[j].T

    def head(off):
        return pl.BlockSpec((t, HEAD_DIM), lambda h: (0, off + h))

    out = pl.BlockSpec((t, HEAD_DIM), lambda h: (0, h))
    shape = jax.ShapeDtypeStruct((t, n_heads * HEAD_DIM), F32)
    return pl.pallas_call(
        body, name="attn_bwd", grid=(n_heads,),
        in_specs=[head(q0), head(q0 + n_heads), head(q0 + 2 * n_heads), head(q0), out],
        out_specs=[out, out, out], out_shape=[shape, shape, shape],
        scratch_shapes=[pltpu.VMEM((t, HEAD_DIM), BF16)] * 4 + [pltpu.VMEM((nq, HEAD_DIM, b), BF16)] * 2
        + [pltpu.VMEM((nq, HEAD_DIM, b), F32)] * 2,
        compiler_params=_cparams(("parallel",)),
    )(u, u, u, dmixin, o)


def _place():
    x, y, c = lax.axis_index("x"), lax.axis_index("y"), lax.axis_index("c")
    return x, y, c


def _all_gather_weights(shards):
    n = len(shards)

    def body(*refs):
        ins, outs = refs[:n], refs[n:2 * n]
        send_sems, recv_sems, local_sems = refs[2 * n:]
        x, y, c = _place()
        me, sibling = (x, y, c), (x, y, 1 - c)
        chips = [(1 - x, y), (x, 1 - y), (1 - x, 1 - y)]

        def slot(px, py, pc):
            return 4 * px + 2 * py + pc

        def copy(ti, k, block, to, src=None):
            dst = outs[ti].at[slot(*block)]
            return pltpu.make_async_remote_copy(
                src_ref=dst if src is None else src, dst_ref=dst,
                send_sem=send_sems.at[ti, k], recv_sem=recv_sems.at[ti, k], device_id=to, device_id_type=MESH)

        mine = [pltpu.make_async_copy(ins[ti], outs[ti].at[slot(*me)], local_sems.at[ti]) for ti in range(n)]
        for cp in mine:
            cp.start()
        first = []
        for ti in range(n):
            first.append(copy(ti, 0, me, sibling, src=ins[ti]))
            first += [copy(ti, 1 + j, me, (*chip, c), src=ins[ti]) for j, chip in enumerate(chips)]
        for cp in first:
            cp.start()
        passed = []
        for j, chip in enumerate(chips):
            for ti in range(n):
                copy(ti, 1 + j, (*chip, c), me).wait_recv()
                fwd = copy(ti, 4 + j, (*chip, c), sibling)
                fwd.start()
                passed.append(fwd)
        for ti in range(n):
            copy(ti, 0, sibling, me).wait_recv()
            for j, chip in enumerate(chips):
                copy(ti, 4 + j, (*chip, 1 - c), me).wait_recv()
        for cp in first + passed:
            cp.wait_send()
        for cp in mine:
            cp.wait()

    any_spec = pl.BlockSpec(memory_space=pl.ANY)
    return pl.pallas_call(
        body, name="all_gather_weights",
        in_specs=[any_spec] * n, out_specs=[any_spec] * n,
        out_shape=[jax.ShapeDtypeStruct((N_DEV, *s.shape), s.dtype) for s in shards],
        scratch_shapes=[pltpu.SemaphoreType.DMA((n, 7)), pltpu.SemaphoreType.DMA((n, 7)), pltpu.SemaphoreType.DMA((n,))],
        compiler_params=pltpu.CompilerParams(has_side_effects=True),
    )(*shards)


def _rs_to_sibling(partials):
    n = len(partials)

    def body(*refs):
        ins, outs = refs[:n], refs[n:2 * n]
        send_sems, recv_sems = refs[2 * n:]
        x, y, c = _place()
        sibling = (x, y, 1 - c)
        copies = []
        for ti in range(n):
            for r in range(4):
                ox, oy = (1 - x if r & 2 else x), (1 - y if r & 1 else y)
                cp = pltpu.make_async_remote_copy(
                    src_ref=ins[ti].at[4 * ox + 2 * oy + (1 - c)], dst_ref=outs[ti].at[r],
                    send_sem=send_sems.at[ti, r], recv_sem=recv_sems.at[ti, r], device_id=sibling, device_id_type=MESH)
                cp.start()
                copies.append(cp)
        for cp in copies:
            cp.wait_recv()
        for cp in copies:
            cp.wait_send()

    any_spec = pl.BlockSpec(memory_space=pl.ANY)
    return pl.pallas_call(
        body, name="rs_to_sibling",
        in_specs=[any_spec] * n, out_specs=[any_spec] * n,
        out_shape=[jax.ShapeDtypeStruct((4, *s.shape[1:]), s.dtype) for s in partials],
        scratch_shapes=[pltpu.SemaphoreType.DMA((n, 4)), pltpu.SemaphoreType.DMA((n, 4))],
        compiler_params=pltpu.CompilerParams(has_side_effects=True),
    )(*partials)


def _rs_to_owner(sums):
    n = len(sums)

    def body(*refs):
        ins, outs = refs[:n], refs[n:2 * n]
        send_sems, recv_sems = refs[2 * n:]
        x, y, c = _place()
        copies = []
        for ti in range(n):
            for r in range(1, 4):
                ox, oy = (1 - x if r & 2 else x), (1 - y if r & 1 else y)
                cp = pltpu.make_async_remote_copy(
                    src_ref=ins[ti].at[r], dst_ref=outs[ti].at[r],
                    send_sem=send_sems.at[ti, r], recv_sem=recv_sems.at[ti, r], device_id=(ox, oy, c), device_id_type=MESH)
                cp.start()
                copies.append(cp)
        for cp in copies:
            cp.wait_recv()
        for cp in copies:
            cp.wait_send()

    any_spec = pl.BlockSpec(memory_space=pl.ANY)
    return pl.pallas_call(
        body, name="rs_to_owner",
        in_specs=[any_spec] * n, out_specs=[any_spec] * n,
        out_shape=[jax.ShapeDtypeStruct(s.shape, s.dtype) for s in sums],
        scratch_shapes=[pltpu.SemaphoreType.DMA((n, 4)), pltpu.SemaphoreType.DMA((n, 4))],
        compiler_params=pltpu.CompilerParams(has_side_effects=True),
    )(*sums)


def _owner_slots():
    x, y, c = _place()
    idx = []
    for r in range(4):
        ox, oy = (1 - x if r & 2 else x), (1 - y if r & 1 else y)
        idx.append(4 * ox + 2 * oy + c)
    return jnp.stack(idx).astype(jnp.int32)


def _row_tile(rows, cols):
    tr = max(8, min(rows, (1 << 19) // cols))
    while rows % tr:
        tr //= 2
    return tr


def _rs_chip_sum(name, slots, partial, from_sibling):
    _, rows, cols = partial.shape
    tr = _row_tile(rows, cols)

    def body(slots_ref, p_ref, s_ref, o_ref):
        o_ref[...] = (p_ref[...] + s_ref[...]).astype(BF16)

    grid_spec = pltpu.PrefetchScalarGridSpec(
        num_scalar_prefetch=1, grid=(3, rows // tr),
        in_specs=[pl.BlockSpec((None, tr, cols), lambda r, i, s: (s[r + 1], i, 0)),
                  pl.BlockSpec((None, tr, cols), lambda r, i, s: (r + 1, i, 0))],
        out_specs=pl.BlockSpec((None, tr, cols), lambda r, i, s: (r + 1, i, 0)))
    return pl.pallas_call(
        body, name=name, grid_spec=grid_spec, out_shape=jax.ShapeDtypeStruct((4, rows, cols), BF16),
        compiler_params=_cparams(("parallel", "parallel")),
    )(slots, partial, from_sibling)


def _adamw(w, g, m, v):
    m = ADAM_B1 * m + (1.0 - ADAM_B1) * g
    v = ADAM_B2 * v + (1.0 - ADAM_B2) * (g * g)
    m_hat = m / (1.0 - ADAM_B1 ** ADAM_STEP)
    v_hat = v / (1.0 - ADAM_B2 ** ADAM_STEP)
    delta = -ADAM_LR * (m_hat / (jnp.sqrt(v_hat) + ADAM_EPS) + ADAM_WD * w)
    return delta, m, v


def _rs_final_adamw(name, slots, partial, from_sibling, from_chips, w, m, v):
    rows, cols = w.shape
    tr = _row_tile(rows, cols)

    def body(slots_ref, p_ref, s_ref, c1_ref, c2_ref, c3_ref, w_ref, m_ref, v_ref, g_ref, d_ref, nm_ref, nv_ref):
        g = p_ref[...] + s_ref[...]
        g = g + c1_ref[...].astype(F32)
        g = g + c2_ref[...].astype(F32)
        g = g + c3_ref[...].astype(F32)
        delta, nm, nv = _adamw(w_ref[...], g, m_ref[...], v_ref[...])
        g_ref[...] = g
        d_ref[...] = delta
        nm_ref[...] = nm
        nv_ref[...] = nv

    def slot(r):
        return pl.BlockSpec((None, tr, cols), lambda i, s: (r, i, 0))

    flat = pl.BlockSpec((tr, cols), lambda i, s: (i, 0))
    grid_spec = pltpu.PrefetchScalarGridSpec(
        num_scalar_prefetch=1, grid=(rows // tr,),
        in_specs=[pl.BlockSpec((None, tr, cols), lambda i, s: (s[0], i, 0)), slot(0), slot(1), slot(2), slot(3), flat, flat, flat],
        out_specs=[flat] * 4)
    return pl.pallas_call(
        body, name=name, grid_spec=grid_spec, out_shape=[jax.ShapeDtypeStruct((rows, cols), F32)] * 4,
        compiler_params=_cparams(("parallel",)),
    )(slots, partial, from_sibling, from_chips, from_chips, from_chips, w, m, v)


def _small_all_reduce(packet):
    rows, d = packet.shape

    def body(p_ref, sum_ref, loss_ref, all_ref, send_sems, recv_sems):
        x, y, c = _place()
        me = 4 * x + 2 * y + c
        all_ref[me] = p_ref[...]
        copies = []
        for k in range(1, N_DEV):
            px, py, pc = (1 - x if k & 4 else x), (1 - y if k & 2 else y), (1 - c if k & 1 else c)
            cp = pltpu.make_async_remote_copy(
                src_ref=p_ref, dst_ref=all_ref.at[me], send_sem=send_sems.at[k], recv_sem=recv_sems.at[k],
                device_id=(px, py, pc), device_id_type=MESH)
            cp.start()
            copies.append(cp)
        for cp in copies:
            cp.wait_recv()
        for cp in copies:
            cp.wait_send()
        total = all_ref[0]
        for j in range(1, N_DEV):
            total = total + all_ref[j]
        sum_ref[...] = total
        loss_ref[...] = jnp.sum(total[0:1, :], axis=-1, keepdims=True)

    vmem = pl.BlockSpec(memory_space=pltpu.VMEM)
    return pl.pallas_call(
        body, name="small_all_reduce",
        in_specs=[vmem], out_specs=[vmem, vmem],
        out_shape=[jax.ShapeDtypeStruct((rows, d), F32), jax.ShapeDtypeStruct((1, 1), F32)],
        scratch_shapes=[pltpu.VMEM((N_DEV, rows, d), F32), pltpu.SemaphoreType.DMA((N_DEV,)), pltpu.SemaphoreType.DMA((N_DEV,))],
        compiler_params=pltpu.CompilerParams(has_side_effects=True),
    )(packet)


def _small_adamw(w, g, m, v):
    def body(w_ref, g_ref, m_ref, v_ref, d_ref, nm_ref, nv_ref):
        delta, nm, nv = _adamw(w_ref[...], g_ref[...], m_ref[...], v_ref[...])
        d_ref[...] = delta
        nm_ref[...] = nm
        nv_ref[...] = nv

    vmem = pl.BlockSpec(memory_space=pltpu.VMEM)
    return pl.pallas_call(
        body, name="small_adamw", in_specs=[vmem] * 4, out_specs=[vmem] * 3,
        out_shape=[jax.ShapeDtypeStruct(w.shape, F32)] * 3,
    )(w, g, m, v)


def kernel(x, ln_in_g, ln_in_b, w_in, w_pool, pool_scale, w_out, ln1_g, ln1_b, w_ff1, b_ff1, w_ff2, b_ff2, ln2_g, ln2_b, loss_target, m_ln_in_g, m_ln_in_b, m_w_in, m_w_pool, m_pool_scale, m_w_out, m_ln1_g, m_ln1_b, m_w_ff1, m_b_ff1, m_w_ff2, m_b_ff2, m_ln2_g, m_ln2_b, v_ln_in_g, v_ln_in_b, v_w_in, v_w_pool, v_pool_scale, v_w_out, v_ln1_g, v_ln1_b, v_w_ff1, v_b_ff1, v_w_ff2, v_b_ff2, v_ln2_g, v_ln2_b):
    t, d = x.shape[1], x.shape[2]
    n_groups = len(POOL_WINDOWS)
    c_pool = w_pool.shape[3]
    p = n_groups * c_pool
    n_heads = (d - p) // HEAD_DIM
    ws_in = w_in.shape[2]
    n_in = N_DEV * ws_in
    ws_out = w_out.shape[1]
    ws_f = w_ff1.shape[2]
    f = N_DEV * ws_f
    pr = w_pool.shape[2]
    assert n_in == p + 3 * n_heads * HEAD_DIM and N_DEV * ws_out == d and N_DEV * pr == c_pool

    tm_big = min(t, 1024)
    tm_ep = min(t, 512)
    tkk = min(t, 512)
    half_f = min(ws_f, 512)
    per_f = ws_f // half_f

    x2 = x.reshape(t, d)
    target = loss_target.reshape(t, d)
    g0, b0 = ln_in_g.reshape(1, d), ln_in_b.reshape(1, d)

    shards = [w_in.reshape(d, ws_in), w_out.reshape(ws_out, d), w_ff1.reshape(d, ws_f), w_ff2.reshape(ws_f, d),
              w_pool.reshape(n_groups * pr, c_pool)]
    win_g, wout_g, w1_g, w2_g, wpool_g, scale_g = _all_gather_weights(
        [s.astype(BF16) for s in shards] + [pool_scale.reshape(n_groups, pr)])
    wout_2d = wout_g.reshape(d, d)
    w2_2d = w2_g.reshape(f, d)
    wp_full = wpool_g.reshape(N_DEV, n_groups, pr, c_pool).transpose(1, 0, 2, 3).reshape(n_groups, c_pool, c_pool)
    sc_full = scale_g.transpose(1, 0, 2).reshape(n_groups, 1, c_pool)
    x_, y_, c_ = _place()
    me = 4 * x_ + 2 * y_ + c_

    def sds(shape, dtype=F32):
        return jax.ShapeDtypeStruct(shape, dtype)

    vec = pl.BlockSpec((1, d), lambda m, n, k: (0, 0))
    row_ep = pl.BlockSpec((tm_ep, d), lambda m, n, k: (m, 0))
    seq = ("arbitrary", "arbitrary", "arbitrary")

    h0, h0b = _ln_in_fwd(x2, g0, b0, tm_big)

    u = _matmul(
        "mm_u", h0b, win_g, dims=NN, grid=(t // tm_big, N_DEV, 1),
        a_spec=pl.BlockSpec((tm_big, d), lambda m, n, k: (m, 0)),
        b_spec=pl.BlockSpec((None, d, ws_in), lambda m, n, k: (n, 0, 0)),
        out_shape=[sds((t, n_in))],
        out_specs=[pl.BlockSpec((tm_big, ws_in), lambda m, n, k: (m, n))],
        acc_shape=(tm_big, ws_in), epilogue=_store_epilogue(F32))[0]

    y_pool, ypre = _pool_fwd(u, wp_full, sc_full, t, c_pool)
    o = _attn_fwd(u, t, p, n_heads)
    mixin = jnp.concatenate([y_pool, o.astype(BF16)], axis=1)

    def ep_ln1(acc, ex, outs):
        h0_ref, g_ref, b_ref = ex
        r1 = DEEPNORM_ALPHA * h0_ref[...] + acc
        xhat, _ = _ln_stats(r1)
        h1 = xhat * g_ref[...] + b_ref[...]
        outs[0][...] = r1
        outs[1][...] = h1
        outs[2][...] = h1.astype(BF16)

    r1, h1, h1b = _matmul(
        "mm_mix_ln1", mixin, wout_g, dims=NN, grid=(t // tm_ep, 1, N_DEV),
        a_spec=pl.BlockSpec((tm_ep, ws_out), lambda m, n, k: (m, k)),
        b_spec=pl.BlockSpec((None, ws_out, d), lambda m, n, k: (k, 0, 0)),
        extras=(h0, ln1_g, ln1_b), extra_specs=(row_ep, vec, vec),
        out_shape=[sds((t, d)), sds((t, d)), sds((t, d), BF16)], out_specs=[row_ep] * 3,
        acc_shape=(tm_ep, d), epilogue=ep_ln1)

    def ep_ff1(acc, ex, outs):
        f1 = acc + ex[0][...]
        outs[0][...] = f1
        r = jnp.maximum(f1, 0.0)
        outs[1][...] = (r * r).astype(BF16)

    ff_tile = pl.BlockSpec((tm_big, half_f), lambda m, n, k: (m, n))
    f1, act = _matmul(
        "mm_ff1", h1b, w1_g, dims=NN, grid=(t // tm_big, f // half_f, 1),
        a_spec=pl.BlockSpec((tm_big, d), lambda m, n, k: (m, 0)),
        b_spec=pl.BlockSpec((None, d, half_f), lambda m, n, k: (n // per_f, 0, n % per_f)),
        extras=(b_ff1,), extra_specs=(pl.BlockSpec((1, half_f), lambda m, n, k: (0, n)),),
        out_shape=[sds((t, f)), sds((t, f), BF16)], out_specs=[ff_tile, ff_tile],
        acc_shape=(tm_big, half_f), epilogue=ep_ff1)

    def ep_ln2(acc, ex, outs):
        h1_ref, tgt_ref, bf2_ref, g_ref, b_ref = ex
        dr2_ref, dr2b_ref, dg_ref, db_ref, dbf2_ref, loss_ref = outs
        r2 = DEEPNORM_ALPHA * h1_ref[...] + (acc + bf2_ref[...])
        xhat, rstd = _ln_stats(r2)
        err = xhat * g_ref[...] + b_ref[...] - tgt_ref[...]
        dr2, dg, db = _ln_bwd(err * (1.0 / d), xhat, rstd, g_ref[...])
        dr2_ref[...] = dr2
        dr2b_ref[...] = dr2.astype(BF16)
        first = pl.program_id(0) == 0
        _acc_rows(first, dg_ref, dg)
        _acc_rows(first, db_ref, db)
        _acc_rows(first, dbf2_ref, jnp.sum(dr2, axis=0, keepdims=True))
        _acc_rows(first, loss_ref, jnp.sum(err * err, axis=0, keepdims=True) * (0.5 / d))

    dr2, dr2b, dg2, db2, dbf2, loss_vec = _matmul(
        "mm_ff2_ln2_loss", act, w2_g, dims=NN, grid=(t // tm_ep, 1, f // half_f),
        a_spec=pl.BlockSpec((tm_ep, half_f), lambda m, n, k: (m, k)),
        b_spec=pl.BlockSpec((None, half_f, d), lambda m, n, k: (k // per_f, k % per_f, 0)),
        extras=(h1, target, b_ff2, ln2_g, ln2_b), extra_specs=(row_ep, row_ep, vec, vec, vec),
        out_shape=[sds((t, d)), sds((t, d), BF16)] + [sds((1, d))] * 4, out_specs=[row_ep, row_ep, vec, vec, vec, vec],
        acc_shape=(tm_ep, d), epilogue=ep_ln2, sem=seq)

    def ep_dff1(acc, ex, outs):
        df1 = acc * (2.0 * jnp.maximum(ex[0][...], 0.0))
        outs[0][...] = df1.astype(BF16)
        _acc_rows(pl.program_id(1) == 0, outs[1], jnp.sum(df1, axis=0, keepdims=True))

    df_tile = pl.BlockSpec((tm_big, ws_f), lambda n, m, k: (m, n))
    df1b, dbf1 = _matmul(
        "mm_dff1", dr2b, w2_g, dims=NT, grid=(N_DEV, t // tm_big, 1),
        a_spec=pl.BlockSpec((tm_big, d), lambda n, m, k: (m, 0)),
        b_spec=pl.BlockSpec((None, ws_f, d), lambda n, m, k: (n, 0, 0)),
        extras=(f1,), extra_specs=(df_tile,),
        out_shape=[sds((t, f), BF16), sds((1, f))], out_specs=[df_tile, pl.BlockSpec((1, ws_f), lambda n, m, k: (0, n))],
        acc_shape=(tm_big, ws_f), epilogue=ep_dff1, sem=("parallel", "arbitrary", "arbitrary"))

    tn_d = min(d, 1024)
    dw2 = _matmul(
        "mm_dw2", act, dr2b, dims=TN, grid=(N_DEV, d // tn_d, t // tkk),
        a_spec=pl.BlockSpec((tkk, ws_f), lambda m, n, k: (k, m)),
        b_spec=pl.BlockSpec((tkk, tn_d), lambda m, n, k: (k, n)),
        out_shape=[sds((N_DEV, ws_f, d))], out_specs=[pl.BlockSpec((None, ws_f, tn_d), lambda m, n, k: (m, 0, n))],
        acc_shape=(ws_f, tn_d), epilogue=_store_epilogue(F32))[0]

    dw1 = _matmul(
        "mm_dw1", h1b, df1b, dims=TN, grid=(d // tn_d, N_DEV, t // tkk),
        a_spec=pl.BlockSpec((tkk, tn_d), lambda m, n, k: (k, m)),
        b_spec=pl.BlockSpec((tkk, ws_f), lambda m, n, k: (k, n)),
        out_shape=[sds((N_DEV, d, ws_f))], out_specs=[pl.BlockSpec((None, tn_d, ws_f), lambda m, n, k: (n, m, 0))],
        acc_shape=(tn_d, ws_f), epilogue=_store_epilogue(F32))[0]

    def ep_ln1_bwd(acc, ex, outs):
        dr2_ref, r1_ref, g_ref = ex
        xhat, rstd = _ln_stats(r1_ref[...])
        dr1, dg, db = _ln_bwd(DEEPNORM_ALPHA * dr2_ref[...] + acc, xhat, rstd, g_ref[...])
        outs[0][...] = dr1
        outs[1][...] = dr1.astype(BF16)
        first = pl.program_id(0) == 0
        _acc_rows(first, outs[2], dg)
        _acc_rows(first, outs[3], db)

    dr1, dr1b, dg1, db1 = _matmul(
        "mm_dh1_ln1_bwd", df1b, w1_g, dims=NT, grid=(t // tm_ep, 1, f // half_f),
        a_spec=pl.BlockSpec((tm_ep, half_f), lambda m, n, k: (m, k)),
        b_spec=pl.BlockSpec((None, d, half_f), lambda m, n, k: (k // per_f, 0, k % per_f)),
        extras=(dr2, r1, ln1_g), extra_specs=(row_ep, row_ep, vec),
        out_shape=[sds((t, d)), sds((t, d), BF16), sds((1, d)), sds((1, d))], out_specs=[row_ep, row_ep, vec, vec],
        acc_shape=(tm_ep, d), epilogue=ep_ln1_bwd, sem=seq)

    dwout = _matmul(
        "mm_dwout", mixin, dr1b, dims=TN, grid=(d // tn_d, d // tn_d, t // tkk),
        a_spec=pl.BlockSpec((tkk, tn_d), lambda m, n, k: (k, m)),
        b_spec=pl.BlockSpec((tkk, tn_d), lambda m, n, k: (k, n)),
        out_shape=[sds((d, d))], out_specs=[pl.BlockSpec((tn_d, tn_d), lambda m, n, k: (m, n))],
        acc_shape=(tn_d, tn_d), epilogue=_store_epilogue(F32))[0]

    dmixin = _matmul(
        "mm_dmixin", dr1b, wout_2d, dims=NT, grid=(t // tm_big, d // tn_d, 1),
        a_spec=pl.BlockSpec((tm_big, d), lambda m, n, k: (m, 0)),
        b_spec=pl.BlockSpec((tn_d, d), lambda m, n, k: (n, 0)),
        out_shape=[sds((t, d))], out_specs=[pl.BlockSpec((tm_big, tn_d), lambda m, n, k: (m, n))],
        acc_shape=(tm_big, tn_d), epilogue=_store_epilogue(F32))[0]

    du_pool, dwp, dsc = _pool_bwd(dmixin, ypre, wp_full, sc_full, t, c_pool)
    dq, dk, dv = _attn_bwd(u, dmixin, o, t, p, n_heads)
    dub = jnp.concatenate([du_pool, dq, dk, dv], axis=1).astype(BF16)

    dwin = _matmul(
        "mm_dwin", h0b, dub, dims=TN, grid=(d // tn_d, N_DEV, t // tkk),
        a_spec=pl.BlockSpec((tkk, tn_d), lambda m, n, k: (k, m)),
        b_spec=pl.BlockSpec((tkk, ws_in), lambda m, n, k: (k, n)),
        out_shape=[sds((N_DEV, d, ws_in))], out_specs=[pl.BlockSpec((None, tn_d, ws_in), lambda m, n, k: (n, m, 0))],
        acc_shape=(tn_d, ws_in), epilogue=_store_epilogue(F32))[0]

    def ep_ln0_bwd(acc, ex, outs):
        dr1_ref, x_ref, g_ref = ex
        xhat, rstd = _ln_stats(x_ref[...])
        dx, dg, db = _ln_bwd(DEEPNORM_ALPHA * dr1_ref[...] + acc, xhat, rstd, g_ref[...])
        outs[0][...] = dx
        first = pl.program_id(0) == 0
        _acc_rows(first, outs[1], dg)
        _acc_rows(first, outs[2], db)

    dx, dg0, db0 = _matmul(
        "mm_dh0_ln0_bwd", dub, win_g, dims=NT, grid=(t // tm_ep, 1, N_DEV),
        a_spec=pl.BlockSpec((tm_ep, ws_in), lambda m, n, k: (m, k)),
        b_spec=pl.BlockSpec((None, d, ws_in), lambda m, n, k: (k, 0, 0)),
        extras=(dr1, x2, g0), extra_specs=(row_ep, row_ep, vec),
        out_shape=[sds((t, d)), sds((1, d)), sds((1, d))], out_specs=[row_ep, vec, vec],
        acc_shape=(tm_ep, d), epilogue=ep_ln0_bwd, sem=seq)

    dwp_g = dwp.reshape(n_groups, N_DEV, pr, c_pool).transpose(1, 0, 2, 3).reshape(N_DEV, n_groups * pr, c_pool)
    partials = [dwin, dwout.reshape(N_DEV, ws_out, d), dw1, dw2, dwp_g]
    names = ["w_in", "w_out", "w_ff1", "w_ff2", "w_pool"]
    from_sibling = _rs_to_sibling(partials)
    slots = _owner_slots()
    chip_sums = [_rs_chip_sum("rs_chip_sum_" + nm, slots, pt, fs) for nm, pt, fs in zip(names, partials, from_sibling)]
    from_chips = _rs_to_owner(chip_sums)
    moments = [(m_w_in, v_w_in), (m_w_out, v_w_out), (m_w_ff1, v_w_ff1), (m_w_ff2, v_w_ff2), (m_w_pool, v_w_pool)]
    big = {}
    for nm, pt, fs, fc, w2d, (m_, v_) in zip(names, partials, from_sibling, from_chips, shards, moments):
        res = _rs_final_adamw("rs_final_adamw_" + nm, slots, pt, fs, fc, w2d, m_.reshape(w2d.shape), v_.reshape(w2d.shape))
        big[nm] = res

    n_f_rows = f // d
    pad_sc = d - p
    packet = jnp.concatenate(
        [loss_vec, dg0, db0, dg1, db1, dbf2, dg2, db2, dbf1.reshape(n_f_rows, d),
         jnp.pad(dsc.reshape(1, p), ((0, 0), (0, pad_sc)))], axis=0)
    n_rows = packet.shape[0]
    n_pad = (-n_rows) % 8
    packet = jnp.pad(packet, ((0, n_pad), (0, 0)))
    sums, loss11 = _small_all_reduce(packet)
    dsc_full = sums[8 + n_f_rows, :p].reshape(n_groups, N_DEV, pr)
    dsc_mine = lax.dynamic_index_in_dim(dsc_full, me, axis=1, keepdims=False)

    def sc_row(a):
        return jnp.pad(a.reshape(1, n_groups * pr), ((0, 0), (0, d - n_groups * pr)))

    def small_pack(ln0g, ln0b, l1g, l1b, bf2, l2g, l2b, bf1, sc):
        rows = [jnp.zeros((1, d), F32), ln0g.reshape(1, d), ln0b.reshape(1, d), l1g, l1b, bf2, l2g, l2b,
                bf1.reshape(n_f_rows, d), sc_row(sc), jnp.zeros((n_pad, d), F32)]
        return jnp.concatenate(rows, axis=0)

    w_small = small_pack(ln_in_g, ln_in_b, ln1_g, ln1_b, b_ff2, ln2_g, ln2_b, b_ff1, pool_scale)
    m_small = small_pack(m_ln_in_g, m_ln_in_b, m_ln1_g, m_ln1_b, m_b_ff2, m_ln2_g, m_ln2_b, m_b_ff1, m_pool_scale)
    v_small = small_pack(v_ln_in_g, v_ln_in_b, v_ln1_g, v_ln1_b, v_b_ff2, v_ln2_g, v_ln2_b, v_b_ff1, v_pool_scale)
    g_small = jnp.concatenate([sums[:8 + n_f_rows], sc_row(dsc_mine), jnp.zeros((n_pad, d), F32)], axis=0)
    small = (g_small,) + tuple(_small_adamw(w_small, g_small, m_small, v_small))

    def unpack(a):
        sc = a[8 + n_f_rows, :n_groups * pr].reshape(1, n_groups, pr)
        return {"ln_in_g": a[1], "ln_in_b": a[2], "ln1_g": a[3:4], "ln1_b": a[4:5], "b_ff2": a[5:6], "ln2_g": a[6:7],
                "ln2_b": a[7:8], "b_ff1": a[8:8 + n_f_rows].reshape(1, f), "pool_scale": sc}

    shapes = {"w_in": w_in.shape, "w_out": w_out.shape, "w_ff1": w_ff1.shape, "w_ff2": w_ff2.shape, "w_pool": w_pool.shape}
    order = ["ln_in_g", "ln_in_b", "w_in", "w_pool", "pool_scale", "w_out", "ln1_g", "ln1_b", "w_ff1", "b_ff1", "w_ff2",
             "b_ff2", "ln2_g", "ln2_b"]
    outs = []
    for kind in range(4):
        small_k = unpack(small[kind])
        for nm in order:
            outs.append(big[nm][kind].reshape(shapes[nm]) if nm in big else small_k[nm])
    return (loss11.reshape(()), dx.reshape(x.shape), *outs)
```

```python
import functools
import math

import jax
import jax.numpy as jnp
from jax import lax
from jax.experimental import pallas as pl
from jax.experimental.pallas import tpu as pltpu

F32 = jnp.float32
BF16 = jnp.bfloat16
MESH = pl.DeviceIdType.MESH

N_DEV = 8
HEAD_DIM = 128
POOL_WINDOWS = (2, 4, 8, 16)
DEEPNORM_ALPHA = (2.0 * 1) ** 0.25
LN_EPS = 1e-5
ADAM_LR = 0.001
ADAM_B1 = 0.9
ADAM_B2 = 0.999
ADAM_EPS = 1e-08
ADAM_WD = 0.01
ADAM_STEP = 10

V7X_VMEM_LIMIT = 56 * 1024 * 1024
ATT_BLOCK = 256
POOL_CHUNK = 128

NN = (((1,), (0,)), ((), ()))
NT = (((1,), (1,)), ((), ()))
TN = (((0,), (0,)), ((), ()))


def _dot(a, b, dims=NN):
    return lax.dot_general(a, b, dims, preferred_element_type=F32)


def _cparams(sem=None):
    return pltpu.CompilerParams(dimension_semantics=sem, vmem_limit_bytes=V7X_VMEM_LIMIT)


def _ln_stats(r):
    mu = jnp.mean(r, axis=-1, keepdims=True)
    xc = r - mu
    var = jnp.mean(xc * xc, axis=-1, keepdims=True)
    rstd = lax.rsqrt(var + LN_EPS)
    return xc * rstd, rstd


def _ln_bwd(dy, xhat, rstd, g):
    dxh = dy * g
    m1 = jnp.mean(dxh, axis=-1, keepdims=True)
    m2 = jnp.mean(dxh * xhat, axis=-1, keepdims=True)
    dx = rstd * (dxh - m1 - xhat * m2)
    dg = jnp.sum(dy * xhat, axis=0, keepdims=True)
    db = jnp.sum(dy, axis=0, keepdims=True)
    return dx, dg, db


def _acc_rows(first, ref, val):
    @pl.when(first)
    def _():
        ref[...] = val

    @pl.when(jnp.logical_not(first))
    def _():
        ref[...] += val


def _matmul(name, a, b, *, dims, grid, a_spec, b_spec, extras=(), extra_specs=(), out_shape, out_specs,
            acc_shape, epilogue, k_axis=2, sem=("parallel", "parallel", "arbitrary")):
    nk = grid[k_axis]
    n_extra = len(extras)
    n_out = len(out_shape)

    def body(a_ref, b_ref, *rest):
        extra_refs = rest[:n_extra]
        out_refs = rest[n_extra:n_extra + n_out]
        if nk == 1:
            epilogue(_dot(a_ref[...], b_ref[...], dims), extra_refs, out_refs)
            return
        acc_ref = rest[n_extra + n_out]
        k = pl.program_id(k_axis)

        @pl.when(k == 0)
        def _():
            acc_ref[...] = jnp.zeros(acc_shape, F32)

        acc_ref[...] += _dot(a_ref[...], b_ref[...], dims)

        @pl.when(k == nk - 1)
        def _():
            epilogue(acc_ref[...], extra_refs, out_refs)

    return pl.pallas_call(
        body, name=name, grid=grid,
        in_specs=[a_spec, b_spec, *extra_specs],
        out_specs=list(out_specs), out_shape=list(out_shape),
        scratch_shapes=[] if nk == 1 else [pltpu.VMEM(acc_shape, F32)],
        compiler_params=_cparams(sem),
    )(a, b, *extras)


def _store_epilogue(dtype):
    def ep(acc, extra_refs, out_refs):
        out_refs[0][...] = acc.astype(dtype)
    return ep


def _ln_in_fwd(x, g, b, tm):
    t, d = x.shape

    def body(x_ref, g_ref, b_ref, h_ref, hb_ref):
        xhat, _ = _ln_stats(x_ref[...])
        h = xhat * g_ref[...] + b_ref[...]
        h_ref[...] = h
        hb_ref[...] = h.astype(BF16)

    row = pl.BlockSpec((tm, d), lambda i: (i, 0))
    vec = pl.BlockSpec((1, d), lambda i: (0, 0))
    return pl.pallas_call(
        body, name="ln_in_fwd", grid=(t // tm,), in_specs=[row, vec, vec], out_specs=[row, row],
        out_shape=[jax.ShapeDtypeStruct((t, d), F32), jax.ShapeDtypeStruct((t, d), BF16)],
        compiler_params=_cparams(("parallel",)),
    )(x, g, b)


def _split3(x):
    hi = x.astype(BF16)
    r = x - hi.astype(F32)
    mid = r.astype(BF16)
    lo = (r - mid.astype(F32)).astype(BF16)
    return hi, mid, lo


def _split2(x):
    hi = x.astype(BF16)
    lo = (x - hi.astype(F32)).astype(BF16)
    return hi, lo


def _pool_fwd(u, wp, sc, t, c):
    n_groups = len(POOL_WINDOWS)
    tc = POOL_CHUNK
    n_chunks = t // tc

    def body(u_ref, wp_ref, sc_ref, y_ref, ypre_ref, xp_ref):
        g = pl.program_id(0)
        xp_ref[pl.ds(0, tc), :] = jnp.zeros((tc, c), F32)
        xp_ref[pl.ds(tc, t), :] = u_ref[...]
        out_i = lax.broadcasted_iota(jnp.int32, (tc, 2 * tc), 0)
        in_j = lax.broadcasted_iota(jnp.int32, (tc, 2 * tc), 1)
        lag = tc + out_i - in_j
        t_in_chunk = lax.broadcasted_iota(jnp.int32, (tc, 1), 0)
        for gi, w in enumerate(POOL_WINDOWS):
            @pl.when(g == gi)
            def _(w=w):
                band = jnp.logical_and(lag >= 0, lag < w).astype(BF16)

                def chunk(ci, carry):
                    start = pl.multiple_of(ci * tc, tc)
                    win = xp_ref[pl.ds(start, 2 * tc), :]
                    hi, mid, lo = _split3(win)
                    wsum = _dot(band, hi) + _dot(band, mid) + _dot(band, lo)
                    cnt = jnp.minimum(ci * tc + t_in_chunk + 1, w).astype(F32)
                    ypre = wsum * (1.0 / cnt) - win[tc:, :]
                    ypre_b = ypre.astype(BF16)
                    y = _dot(ypre_b, wp_ref[...]) * sc_ref[...]
                    ypre_ref[pl.ds(start, tc), :] = ypre_b
                    y_ref[pl.ds(start, tc), :] = y.astype(BF16)
                    return carry

                lax.fori_loop(0, n_chunks, chunk, 0)

    col = pl.BlockSpec((t, c), lambda g: (0, g))
    return pl.pallas_call(
        body, name="pool_fwd", grid=(n_groups,),
        in_specs=[col, pl.BlockSpec((None, c, c), lambda g: (g, 0, 0)), pl.BlockSpec((None, 1, c), lambda g: (g, 0, 0))],
        out_specs=[col, col],
        out_shape=[jax.ShapeDtypeStruct((t, n_groups * c), BF16), jax.ShapeDtypeStruct((t, n_groups * c), BF16)],
        scratch_shapes=[pltpu.VMEM((t + tc, c), F32)],
        compiler_params=_cparams(("parallel",)),
    )(u, wp, sc)


def _pool_bwd(dmixin, ypre, wp, sc, t, c):
    n_groups = len(POOL_WINDOWS)
    tc = POOL_CHUNK
    n_chunks = t // tc

    def body(dy_ref, ypre_ref, wp_ref, sc_ref, du_ref, dwp_ref, dsc_ref, zp_ref):
        g = pl.program_id(0)
        zp_ref[pl.ds(t, tc), :] = jnp.zeros((tc, c), F32)
        dwp_ref[...] = jnp.zeros((c, c), F32)
        dsc_ref[...] = jnp.zeros((1, c), F32)
        out_i = lax.broadcasted_iota(jnp.int32, (tc, 2 * tc), 0)
        in_j = lax.broadcasted_iota(jnp.int32, (tc, 2 * tc), 1)
        lead = in_j - out_i
        t_in_chunk = lax.broadcasted_iota(jnp.int32, (tc, 1), 0)
        for gi, w in enumerate(POOL_WINDOWS):
            @pl.when(g == gi)
            def _(w=w):
                band = jnp.logical_and(lead >= 0, lead < w).astype(BF16)

                def first(ci, carry):
                    start = pl.multiple_of(ci * tc, tc)
                    dy = dy_ref[pl.ds(start, tc), :]
                    yp = ypre_ref[pl.ds(start, tc), :]
                    ymm = _dot(yp, wp_ref[...])
                    dsc_ref[...] += jnp.sum(dy * ymm, axis=0, keepdims=True)
                    dys_b = (dy * sc_ref[...]).astype(BF16)
                    dwp_ref[...] += _dot(yp, dys_b, TN)
                    dyp = _dot(dys_b, wp_ref[...], NT)
                    cnt = jnp.minimum(ci * tc + t_in_chunk + 1, w).astype(F32)
                    zp_ref[pl.ds(start, tc), :] = dyp * (1.0 / cnt)
                    du_ref[pl.ds(start, tc), :] = -dyp
                    return carry

                lax.fori_loop(0, n_chunks, first, 0)

                def second(ci, carry):
                    start = pl.multiple_of(ci * tc, tc)
                    hi, mid, lo = _split3(zp_ref[pl.ds(start, 2 * tc), :])
                    du_ref[pl.ds(start, tc), :] += _dot(band, hi) + _dot(band, mid) + _dot(band, lo)
                    return carry

                lax.fori_loop(0, n_chunks, second, 0)

    col = pl.BlockSpec((t, c), lambda g: (0, g))
    return pl.pallas_call(
        body, name="pool_bwd", grid=(n_groups,),
        in_specs=[col, col, pl.BlockSpec((None, c, c), lambda g: (g, 0, 0)), pl.BlockSpec((None, 1, c), lambda g: (g, 0, 0))],
        out_specs=[col, pl.BlockSpec((None, c, c), lambda g: (g, 0, 0)), pl.BlockSpec((None, 1, c), lambda g: (g, 0, 0))],
        out_shape=[jax.ShapeDtypeStruct((t, n_groups * c), F32), jax.ShapeDtypeStruct((n_groups, c, c), F32),
                   jax.ShapeDtypeStruct((n_groups, 1, c), F32)],
        scratch_shapes=[pltpu.VMEM((t + tc, c), F32)],
        compiler_params=_cparams(("parallel",)),
    )(dmixin, ypre, wp, sc)


ROW_PARTS = 2


def _att_consts():
    b = ATT_BLOCK
    rp = b // ROW_PARTS
    row = lax.broadcasted_iota(jnp.int32, (b, b), 0)
    col = lax.broadcasted_iota(jnp.int32, (b, b), 1)
    tri = (row >= col).astype(BF16)
    prow = lax.broadcasted_iota(jnp.int32, (rp, b), 0)
    pcol = lax.broadcasted_iota(jnp.int32, (rp, b), 1)
    causal = [pcol < prow + r * rp for r in range(ROW_PARTS)]
    return tri, causal


def _suffix_sum(x, tri):
    hi, lo = _split2(x)
    return _dot(hi, tri) + _dot(lo, tri)


LOG2_E = 1.4426950408889634


def _att_scores(qb, kb, mask):
    z2 = _dot(qb, kb, NT) * (LOG2_E / math.sqrt(HEAD_DIM))
    sp2 = jnp.maximum(z2, 0.0) + jnp.log2(1.0 + jnp.exp2(-jnp.abs(z2)))
    return z2, sp2, (sp2 if mask is None else jnp.where(mask, sp2, 0.0))


HEADS_PER_STEP = 2
ATT_LANES = HEADS_PER_STEP * HEAD_DIM


def _head_lanes(s):
    return slice(s * HEAD_DIM, (s + 1) * HEAD_DIM)


def _attn_fwd(qkv, t, n_heads):
    b = ATT_BLOCK
    nq = t // b
    n_steps = n_heads // HEADS_PER_STEP

    rp = b // ROW_PARTS
    chains = [(s, r) for s in range(HEADS_PER_STEP) for r in range(ROW_PARTS)]
    no_mask = [None] * ROW_PARTS

    def body(q_ref, k_ref, v_ref, o_ref):
        tri, causal = _att_consts()

        def blocks(qbs, j, state, masks):
            ks = pl.multiple_of(j * b, b)
            scores = [_att_scores(qbs[ci], k_ref[pl.ds(ks, b), _head_lanes(s)], masks[r]) for ci, (s, r) in enumerate(chains)]
            incls = [_suffix_sum(sc[2], tri) for sc in scores]
            out = []
            for ci, (s, r) in enumerate(chains):
                carry, acc = state[2 * ci], state[2 * ci + 1]
                a = jnp.exp2(scores[ci][0] - (incls[ci] + carry))
                if masks[r] is not None:
                    a = jnp.where(masks[r], a, 0.0)
                out += [carry + incls[ci][:, 0:1], acc + _dot(a.astype(BF16), v_ref[pl.ds(ks, b), _head_lanes(s)])]
            return tuple(out)

        def q_loop(i, _):
            qs = pl.multiple_of(i * b, b)
            qbs = [q_ref[pl.ds(qs + r * rp, rp), _head_lanes(s)] for s, r in chains]
            zero = (jnp.zeros((rp, 1), F32), jnp.zeros((rp, HEAD_DIM), F32)) * len(chains)
            state = blocks(qbs, i, zero, causal)
            state = lax.fori_loop(0, i, lambda jj, c: blocks(qbs, i - 1 - jj, c, no_mask), state)
            for ci, (s, r) in enumerate(chains):
                o_ref[pl.ds(qs + r * rp, rp), _head_lanes(s)] = state[2 * ci + 1]
            return 0

        lax.fori_loop(0, nq, q_loop, 0)

    def heads(off):
        return pl.BlockSpec((t, ATT_LANES), lambda h: (0, off + h))

    return pl.pallas_call(
        body, name="attn_fwd", grid=(n_steps,),
        in_specs=[heads(0), heads(n_steps), heads(2 * n_steps)],
        out_specs=heads(0),
        out_shape=jax.ShapeDtypeStruct((t, n_heads * HEAD_DIM), F32),
        compiler_params=_cparams(("parallel",)),
    )(qkv, qkv, qkv)


def _attn_bwd(qkv, do, o, t, n_heads):
    b = ATT_BLOCK
    nq = t // b
    n_steps = n_heads // HEADS_PER_STEP
    scale = 1.0 / math.sqrt(HEAD_DIM)
    rp = b // ROW_PARTS
    chains = [(s, r) for s in range(HEADS_PER_STEP) for r in range(ROW_PARTS)]
    no_mask = [None] * ROW_PARTS

    def body(q_ref, k_ref, v_ref, do_ref, o_ref, dq_ref, dk_ref, dv_ref, qt_ref, dot_ref, dkt_ref, dvt_ref):
        for j in range(nq):
            rows = pl.ds(j * b, b)
            qt_ref[j] = q_ref[rows, :].astype(F32).T.astype(BF16)
            dot_ref[j] = do_ref[rows, :].astype(F32).T.astype(BF16)
        dkt_ref[...] = jnp.zeros((nq, ATT_LANES, b), F32)
        dvt_ref[...] = jnp.zeros((nq, ATT_LANES, b), F32)
        tri, causal = _att_consts()

        def blocks(i, fixed, j, state, masks):
            ks = pl.multiple_of(j * b, b)
            n = len(chains)
            kbs = [k_ref[pl.ds(ks, b), _head_lanes(s)] for s, _ in chains]
            scores = [_att_scores(fixed[ci][0], kbs[ci], masks[r]) for ci, (s, r) in enumerate(chains)]
            incls = [_suffix_sum(sc[2], tri) for sc in scores]
            das = [_dot(fixed[ci][1], v_ref[pl.ds(ks, b), _head_lanes(s)], NT) for ci, (s, r) in enumerate(chains)]
            a_bs, gs = [], []
            for ci, (s, r) in enumerate(chains):
                a = jnp.exp2(scores[ci][0] - (incls[ci] + state[3 * ci]))
                if masks[r] is not None:
                    a = jnp.where(masks[r], a, 0.0)
                a_bs.append(a.astype(BF16))
                gs.append(a_bs[ci].astype(F32) * das[ci])
            g_incls = [_suffix_sum(g, tri) for g in gs]
            dz_bs = []
            for ci, (s, r) in enumerate(chains):
                rest = (fixed[ci][2] - state[3 * ci + 1]) - (g_incls[ci] - gs[ci])
                sig = jnp.exp2(scores[ci][0] - scores[ci][1])
                dz = (gs[ci] - sig * rest) * scale
                if masks[r] is not None:
                    dz = jnp.where(masks[r], dz, 0.0)
                dz_bs.append(dz.astype(BF16))
            out = []
            for ci in range(n):
                out += [state[3 * ci] + incls[ci][:, 0:1], state[3 * ci + 1] + g_incls[ci][:, 0:1],
                        state[3 * ci + 2] + _dot(dz_bs[ci], kbs[ci])]
            for s in range(HEADS_PER_STEP):
                lanes = _head_lanes(s)
                dk_add, dv_add = None, None
                for ci, (cs, r) in enumerate(chains):
                    if cs == s:
                        part = slice(r * rp, (r + 1) * rp)
                        dk_c = _dot(qt_ref[i, lanes, part], dz_bs[ci])
                        dv_c = _dot(dot_ref[i, lanes, part], a_bs[ci])
                        dk_add = dk_c if dk_add is None else dk_add + dk_c
                        dv_add = dv_c if dv_add is None else dv_add + dv_c
                dkt_ref[j, lanes, :] += dk_add
                dvt_ref[j, lanes, :] += dv_add
            return tuple(out)

        def q_loop(i, _):
            qs = pl.multiple_of(i * b, b)
            fixed = []
            for s, r in chains:
                rows = pl.ds(qs + r * rp, rp)
                dob = do_ref[rows, _head_lanes(s)]
                total = jnp.sum(dob.astype(F32) * o_ref[rows, _head_lanes(s)], axis=-1, keepdims=True)
                fixed.append((q_ref[rows, _head_lanes(s)], dob, total))
            zero = (jnp.zeros((rp, 1), F32), jnp.zeros((rp, 1), F32), jnp.zeros((rp, HEAD_DIM), F32)) * len(chains)
            state = blocks(i, fixed, i, zero, causal)
            state = lax.fori_loop(0, i, lambda jj, c: blocks(i, fixed, i - 1 - jj, c, no_mask), state)
            for ci, (s, r) in enumerate(chains):
                dq_ref[pl.ds(qs + r * rp, rp), _head_lanes(s)] = state[3 * ci + 2].astype(BF16)
            return 0

        lax.fori_loop(0, nq, q_loop, 0)
        for j in range(nq):
            rows = pl.ds(j * b, b)
            dk_ref[rows, :] = dkt_ref[j].T.astype(BF16)
            dv_ref[rows, :] = dvt_ref[j].T.astype(BF16)

    def heads(off):
        return pl.BlockSpec((t, ATT_LANES), lambda h: (0, off + h))

    shape = jax.ShapeDtypeStruct((t, n_heads * HEAD_DIM), BF16)
    return pl.pallas_call(
        body, name="attn_bwd", grid=(n_steps,),
        in_specs=[heads(0), heads(n_steps), heads(2 * n_steps), heads(0), heads(0)],
        out_specs=[heads(0)] * 3, out_shape=[shape] * 3,
        scratch_shapes=[pltpu.VMEM((nq, ATT_LANES, b), BF16)] * 2 + [pltpu.VMEM((nq, ATT_LANES, b), F32)] * 2,
        compiler_params=_cparams(("parallel",)),
    )(qkv, qkv, qkv, do, o)


def _place():
    x, y, c = lax.axis_index("x"), lax.axis_index("y"), lax.axis_index("c")
    return x, y, c


def _all_gather_weights(shards):
    n = len(shards)

    def body(*refs):
        ins, outs = refs[:n], refs[n:2 * n]
        send_sems, recv_sems, local_sems = refs[2 * n:]
        x, y, c = _place()
        me, sibling = (x, y, c), (x, y, 1 - c)
        chips = [(1 - x, y), (x, 1 - y), (1 - x, 1 - y)]

        def slot(px, py, pc):
            return 4 * px + 2 * py + pc

        def copy(ti, k, block, to, src=None):
            dst = outs[ti].at[slot(*block)]
            return pltpu.make_async_remote_copy(
                src_ref=dst if src is None else src, dst_ref=dst,
                send_sem=send_sems.at[ti, k], recv_sem=recv_sems.at[ti, k], device_id=to, device_id_type=MESH)

        mine = [pltpu.make_async_copy(ins[ti], outs[ti].at[slot(*me)], local_sems.at[ti]) for ti in range(n)]
        for cp in mine:
            cp.start()
        first = []
        for ti in range(n):
            first.append(copy(ti, 0, me, sibling, src=ins[ti]))
            first += [copy(ti, 1 + j, me, (*chip, c), src=ins[ti]) for j, chip in enumerate(chips)]
        for cp in first:
            cp.start()
        passed = []
        for j, chip in enumerate(chips):
            for ti in range(n):
                copy(ti, 1 + j, (*chip, c), me).wait_recv()
                fwd = copy(ti, 4 + j, (*chip, c), sibling)
                fwd.start()
                passed.append(fwd)
        for ti in range(n):
            copy(ti, 0, sibling, me).wait_recv()
            for j, chip in enumerate(chips):
                copy(ti, 4 + j, (*chip, 1 - c), me).wait_recv()
        for cp in first + passed:
            cp.wait_send()
        for cp in mine:
            cp.wait()

    any_spec = pl.BlockSpec(memory_space=pl.ANY)
    return pl.pallas_call(
        body, name="all_gather_weights",
        in_specs=[any_spec] * n, out_specs=[any_spec] * n,
        out_shape=[jax.ShapeDtypeStruct((N_DEV, *s.shape), s.dtype) for s in shards],
        scratch_shapes=[pltpu.SemaphoreType.DMA((n, 7)), pltpu.SemaphoreType.DMA((n, 7)), pltpu.SemaphoreType.DMA((n,))],
        compiler_params=pltpu.CompilerParams(has_side_effects=True),
    )(*shards)


def _rs_to_sibling(partials):
    n = len(partials)

    def body(*refs):
        ins, outs = refs[:n], refs[n:2 * n]
        send_sems, recv_sems = refs[2 * n:]
        x, y, c = _place()
        sibling = (x, y, 1 - c)
        copies = []
        for ti in range(n):
            for r in range(4):
                ox, oy = (1 - x if r & 2 else x), (1 - y if r & 1 else y)
                cp = pltpu.make_async_remote_copy(
                    src_ref=ins[ti].at[4 * ox + 2 * oy + (1 - c)], dst_ref=outs[ti].at[r],
                    send_sem=send_sems.at[ti, r], recv_sem=recv_sems.at[ti, r], device_id=sibling, device_id_type=MESH)
                cp.start()
                copies.append(cp)
        for cp in copies:
            cp.wait_recv()
        for cp in copies:
            cp.wait_send()

    any_spec = pl.BlockSpec(memory_space=pl.ANY)
    return pl.pallas_call(
        body, name="rs_to_sibling",
        in_specs=[any_spec] * n, out_specs=[any_spec] * n,
        out_shape=[jax.ShapeDtypeStruct((4, *s.shape[1:]), s.dtype) for s in partials],
        scratch_shapes=[pltpu.SemaphoreType.DMA((n, 4)), pltpu.SemaphoreType.DMA((n, 4))],
        compiler_params=pltpu.CompilerParams(has_side_effects=True),
    )(*partials)


def _rs_to_owner(sums):
    n = len(sums)

    def body(*refs):
        ins, outs = refs[:n], refs[n:2 * n]
        send_sems, recv_sems = refs[2 * n:]
        x, y, c = _place()
        copies = []
        for ti in range(n):
            for r in range(1, 4):
                ox, oy = (1 - x if r & 2 else x), (1 - y if r & 1 else y)
                cp = pltpu.make_async_remote_copy(
                    src_ref=ins[ti].at[r], dst_ref=outs[ti].at[r],
                    send_sem=send_sems.at[ti, r], recv_sem=recv_sems.at[ti, r], device_id=(ox, oy, c), device_id_type=MESH)
                cp.start()
                copies.append(cp)
        for cp in copies:
            cp.wait_recv()
        for cp in copies:
            cp.wait_send()

    any_spec = pl.BlockSpec(memory_space=pl.ANY)
    return pl.pallas_call(
        body, name="rs_to_owner",
        in_specs=[any_spec] * n, out_specs=[any_spec] * n,
        out_shape=[jax.ShapeDtypeStruct(s.shape, s.dtype) for s in sums],
        scratch_shapes=[pltpu.SemaphoreType.DMA((n, 4)), pltpu.SemaphoreType.DMA((n, 4))],
        compiler_params=pltpu.CompilerParams(has_side_effects=True),
    )(*sums)


def _owner_slots():
    x, y, c = _place()
    idx = []
    for r in range(4):
        ox, oy = (1 - x if r & 2 else x), (1 - y if r & 1 else y)
        idx.append(4 * ox + 2 * oy + c)
    return jnp.stack(idx).astype(jnp.int32)


def _row_tile(rows, cols):
    tr = max(8, min(rows, (1 << 19) // cols))
    while rows % tr:
        tr //= 2
    return tr


def _rs_chip_sum(name, slots, partial, from_sibling):
    _, rows, cols = partial.shape
    tr = _row_tile(rows, cols)

    def body(slots_ref, p_ref, s_ref, o_ref):
        o_ref[...] = (p_ref[...] + s_ref[...]).astype(BF16)

    grid_spec = pltpu.PrefetchScalarGridSpec(
        num_scalar_prefetch=1, grid=(3, rows // tr),
        in_specs=[pl.BlockSpec((None, tr, cols), lambda r, i, s: (s[r + 1], i, 0)),
                  pl.BlockSpec((None, tr, cols), lambda r, i, s: (r + 1, i, 0))],
        out_specs=pl.BlockSpec((None, tr, cols), lambda r, i, s: (r + 1, i, 0)))
    return pl.pallas_call(
        body, name=name, grid_spec=grid_spec, out_shape=jax.ShapeDtypeStruct((4, rows, cols), BF16),
        compiler_params=_cparams(("parallel", "parallel")),
    )(slots, partial, from_sibling)


def _adamw(w, g, m, v):
    m = ADAM_B1 * m + (1.0 - ADAM_B1) * g
    v = ADAM_B2 * v + (1.0 - ADAM_B2) * (g * g)
    m_hat = m / (1.0 - ADAM_B1 ** ADAM_STEP)
    v_hat = v / (1.0 - ADAM_B2 ** ADAM_STEP)
    delta = -ADAM_LR * (m_hat / (jnp.sqrt(v_hat) + ADAM_EPS) + ADAM_WD * w)
    return delta, m, v


def _rs_final_adamw(name, slots, partial, from_sibling, from_chips, w, m, v):
    rows, cols = w.shape
    tr = _row_tile(rows, cols)

    def body(slots_ref, p_ref, s_ref, c1_ref, c2_ref, c3_ref, w_ref, m_ref, v_ref, g_ref, d_ref, nm_ref, nv_ref):
        g = p_ref[...] + s_ref[...]
        g = g + c1_ref[...].astype(F32)
        g = g + c2_ref[...].astype(F32)
        g = g + c3_ref[...].astype(F32)
        delta, nm, nv = _adamw(w_ref[...], g, m_ref[...], v_ref[...])
        g_ref[...] = g
        d_ref[...] = delta
        nm_ref[...] = nm
        nv_ref[...] = nv

    def slot(r):
        return pl.BlockSpec((None, tr, cols), lambda i, s: (r, i, 0))

    flat = pl.BlockSpec((tr, cols), lambda i, s: (i, 0))
    grid_spec = pltpu.PrefetchScalarGridSpec(
        num_scalar_prefetch=1, grid=(rows // tr,),
        in_specs=[pl.BlockSpec((None, tr, cols), lambda i, s: (s[0], i, 0)), slot(0), slot(1), slot(2), slot(3), flat, flat, flat],
        out_specs=[flat] * 4)
    return pl.pallas_call(
        body, name=name, grid_spec=grid_spec, out_shape=[jax.ShapeDtypeStruct((rows, cols), F32)] * 4,
        compiler_params=_cparams(("parallel",)),
    )(slots, partial, from_sibling, from_chips, from_chips, from_chips, w, m, v)


def _small_all_reduce(packet):
    rows, d = packet.shape

    def body(p_ref, sum_ref, loss_ref, all_ref, send_sems, recv_sems):
        x, y, c = _place()
        me = 4 * x + 2 * y + c
        all_ref[me] = p_ref[...]
        copies = []
        for k in range(1, N_DEV):
            px, py, pc = (1 - x if k & 4 else x), (1 - y if k & 2 else y), (1 - c if k & 1 else c)
            cp = pltpu.make_async_remote_copy(
                src_ref=p_ref, dst_ref=all_ref.at[me], send_sem=send_sems.at[k], recv_sem=recv_sems.at[k],
                device_id=(px, py, pc), device_id_type=MESH)
            cp.start()
            copies.append(cp)
        for cp in copies:
            cp.wait_recv()
        for cp in copies:
            cp.wait_send()
        total = all_ref[0]
        for j in range(1, N_DEV):
            total = total + all_ref[j]
        sum_ref[...] = total
        loss_ref[...] = jnp.sum(total[0:1, :], axis=-1, keepdims=True)

    vmem = pl.BlockSpec(memory_space=pltpu.VMEM)
    return pl.pallas_call(
        body, name="small_all_reduce",
        in_specs=[vmem], out_specs=[vmem, vmem],
        out_shape=[jax.ShapeDtypeStruct((rows, d), F32), jax.ShapeDtypeStruct((1, 1), F32)],
        scratch_shapes=[pltpu.VMEM((N_DEV, rows, d), F32), pltpu.SemaphoreType.DMA((N_DEV,)), pltpu.SemaphoreType.DMA((N_DEV,))],
        compiler_params=pltpu.CompilerParams(has_side_effects=True),
    )(packet)


def _small_adamw(w, g, m, v):
    def body(w_ref, g_ref, m_ref, v_ref, d_ref, nm_ref, nv_ref):
        delta, nm, nv = _adamw(w_ref[...], g_ref[...], m_ref[...], v_ref[...])
        d_ref[...] = delta
        nm_ref[...] = nm
        nv_ref[...] = nv

    vmem = pl.BlockSpec(memory_space=pltpu.VMEM)
    return pl.pallas_call(
        body, name="small_adamw", in_specs=[vmem] * 4, out_specs=[vmem] * 3,
        out_shape=[jax.ShapeDtypeStruct(w.shape, F32)] * 3,
    )(w, g, m, v)


def kernel(x, ln_in_g, ln_in_b, w_in, w_pool, pool_scale, w_out, ln1_g, ln1_b, w_ff1, b_ff1, w_ff2, b_ff2, ln2_g, ln2_b, loss_target, m_ln_in_g, m_ln_in_b, m_w_in, m_w_pool, m_pool_scale, m_w_out, m_ln1_g, m_ln1_b, m_w_ff1, m_b_ff1, m_w_ff2, m_b_ff2, m_ln2_g, m_ln2_b, v_ln_in_g, v_ln_in_b, v_w_in, v_w_pool, v_pool_scale, v_w_out, v_ln1_g, v_ln1_b, v_w_ff1, v_b_ff1, v_w_ff2, v_b_ff2, v_ln2_g, v_ln2_b):
    t, d = x.shape[1], x.shape[2]
    n_groups = len(POOL_WINDOWS)
    c_pool = w_pool.shape[3]
    p = n_groups * c_pool
    n_heads = (d - p) // HEAD_DIM
    ws_in = w_in.shape[2]
    n_in = N_DEV * ws_in
    ws_out = w_out.shape[1]
    ws_f = w_ff1.shape[2]
    f = N_DEV * ws_f
    pr = w_pool.shape[2]
    assert n_in == p + 3 * n_heads * HEAD_DIM and N_DEV * ws_out == d and N_DEV * pr == c_pool

    tm_big = min(t, 1024)
    tm_ep = min(t, 512)
    tkk = min(t, 512)
    half_f = min(ws_f, 512)
    per_f = ws_f // half_f

    x2 = x.reshape(t, d)
    target = loss_target.reshape(t, d)
    g0, b0 = ln_in_g.reshape(1, d), ln_in_b.reshape(1, d)

    shards = [w_in.reshape(d, ws_in), w_out.reshape(ws_out, d), w_ff1.reshape(d, ws_f), w_ff2.reshape(ws_f, d),
              w_pool.reshape(n_groups * pr, c_pool)]
    win_g, wout_g, w1_g, w2_g, wpool_g, scale_g = _all_gather_weights(
        [s.astype(BF16) for s in shards] + [pool_scale.reshape(n_groups, pr)])
    wout_2d = wout_g.reshape(d, d)
    w2_2d = w2_g.reshape(f, d)
    wp_full = wpool_g.reshape(N_DEV, n_groups, pr, c_pool).transpose(1, 0, 2, 3).reshape(n_groups, c_pool, c_pool)
    sc_full = scale_g.transpose(1, 0, 2).reshape(n_groups, 1, c_pool)
    x_, y_, c_ = _place()
    me = 4 * x_ + 2 * y_ + c_

    def sds(shape, dtype=F32):
        return jax.ShapeDtypeStruct(shape, dtype)

    vec = pl.BlockSpec((1, d), lambda m, n, k: (0, 0))
    row_ep = pl.BlockSpec((tm_ep, d), lambda m, n, k: (m, 0))
    seq = ("arbitrary", "arbitrary", "arbitrary")

    h0, h0b = _ln_in_fwd(x2, g0, b0, tm_big)

    pool_shards = p // ws_in

    def mm_u(name, first, count, dtype):
        return _matmul(
            name, h0b, win_g, dims=NN, grid=(t // tm_big, count, 1),
            a_spec=pl.BlockSpec((tm_big, d), lambda m, n, k: (m, 0)),
            b_spec=pl.BlockSpec((None, d, ws_in), lambda m, n, k: (n + first, 0, 0)),
            out_shape=[sds((t, count * ws_in), dtype)],
            out_specs=[pl.BlockSpec((tm_big, ws_in), lambda m, n, k: (m, n))],
            acc_shape=(tm_big, ws_in), epilogue=_store_epilogue(dtype))[0]

    u_pool = mm_u("mm_u_pool", 0, pool_shards, F32)
    qkv = mm_u("mm_u_qkv", pool_shards, N_DEV - pool_shards, BF16)

    y_pool, ypre = _pool_fwd(u_pool, wp_full, sc_full, t, c_pool)
    o = _attn_fwd(qkv, t, n_heads)
    mixin = jnp.concatenate([y_pool, o.astype(BF16)], axis=1)

    def ep_ln1(acc, ex, outs):
        h0_ref, g_ref, b_ref = ex
        r1 = DEEPNORM_ALPHA * h0_ref[...] + acc
        xhat, _ = _ln_stats(r1)
        h1 = xhat * g_ref[...] + b_ref[...]
        outs[0][...] = r1
        outs[1][...] = h1
        outs[2][...] = h1.astype(BF16)

    r1, h1, h1b = _matmul(
        "mm_mix_ln1", mixin, wout_g, dims=NN, grid=(t // tm_ep, 1, N_DEV),
        a_spec=pl.BlockSpec((tm_ep, ws_out), lambda m, n, k: (m, k)),
        b_spec=pl.BlockSpec((None, ws_out, d), lambda m, n, k: (k, 0, 0)),
        extras=(h0, ln1_g, ln1_b), extra_specs=(row_ep, vec, vec),
        out_shape=[sds((t, d)), sds((t, d)), sds((t, d), BF16)], out_specs=[row_ep] * 3,
        acc_shape=(tm_ep, d), epilogue=ep_ln1)

    def ep_ff1(acc, ex, outs):
        f1 = acc + ex[0][...]
        outs[0][...] = f1
        r = jnp.maximum(f1, 0.0)
        outs[1][...] = (r * r).astype(BF16)

    ff_tile = pl.BlockSpec((tm_big, half_f), lambda m, n, k: (m, n))
    f1, act = _matmul(
        "mm_ff1", h1b, w1_g, dims=NN, grid=(t // tm_big, f // half_f, 1),
        a_spec=pl.BlockSpec((tm_big, d), lambda m, n, k: (m, 0)),
        b_spec=pl.BlockSpec((None, d, half_f), lambda m, n, k: (n // per_f, 0, n % per_f)),
        extras=(b_ff1,), extra_specs=(pl.BlockSpec((1, half_f), lambda m, n, k: (0, n)),),
        out_shape=[sds((t, f)), sds((t, f), BF16)], out_specs=[ff_tile, ff_tile],
        acc_shape=(tm_big, half_f), epilogue=ep_ff1)

    def ep_ln2(acc, ex, outs):
        h1_ref, tgt_ref, bf2_ref, g_ref, b_ref = ex
        dr2_ref, dr2b_ref, dg_ref, db_ref, dbf2_ref, loss_ref = outs
        r2 = DEEPNORM_ALPHA * h1_ref[...] + (acc + bf2_ref[...])
        xhat, rstd = _ln_stats(r2)
        err = xhat * g_ref[...] + b_ref[...] - tgt_ref[...]
        dr2, dg, db = _ln_bwd(err * (1.0 / d), xhat, rstd, g_ref[...])
        dr2_ref[...] = dr2
        dr2b_ref[...] = dr2.astype(BF16)
        first = pl.program_id(0) == 0
        _acc_rows(first, dg_ref, dg)
        _acc_rows(first, db_ref, db)
        _acc_rows(first, dbf2_ref, jnp.sum(dr2, axis=0, keepdims=True))
        _acc_rows(first, loss_ref, jnp.sum(err * err, axis=0, keepdims=True) * (0.5 / d))

    dr2, dr2b, dg2, db2, dbf2, loss_vec = _matmul(
        "mm_ff2_ln2_loss", act, w2_g, dims=NN, grid=(t // tm_ep, 1, f // half_f),
        a_spec=pl.BlockSpec((tm_ep, half_f), lambda m, n, k: (m, k)),
        b_spec=pl.BlockSpec((None, half_f, d), lambda m, n, k: (k // per_f, k % per_f, 0)),
        extras=(h1, target, b_ff2, ln2_g, ln2_b), extra_specs=(row_ep, row_ep, vec, vec, vec),
        out_shape=[sds((t, d)), sds((t, d), BF16)] + [sds((1, d))] * 4, out_specs=[row_ep, row_ep, vec, vec, vec, vec],
        acc_shape=(tm_ep, d), epilogue=ep_ln2, sem=seq)

    def ep_dff1(acc, ex, outs):
        df1 = acc * (2.0 * jnp.maximum(ex[0][...], 0.0))
        outs[0][...] = df1.astype(BF16)
        _acc_rows(pl.program_id(1) == 0, outs[1], jnp.sum(df1, axis=0, keepdims=True))

    df_tile = pl.BlockSpec((tm_big, ws_f), lambda n, m, k: (m, n))
    df1b, dbf1 = _matmul(
        "mm_dff1", dr2b, w2_g, dims=NT, grid=(N_DEV, t // tm_big, 1),
        a_spec=pl.BlockSpec((tm_big, d), lambda n, m, k: (m, 0)),
        b_spec=pl.BlockSpec((None, ws_f, d), lambda n, m, k: (n, 0, 0)),
        extras=(f1,), extra_specs=(df_tile,),
        out_shape=[sds((t, f), BF16), sds((1, f))], out_specs=[df_tile, pl.BlockSpec((1, ws_f), lambda n, m, k: (0, n))],
        acc_shape=(tm_big, ws_f), epilogue=ep_dff1, sem=("parallel", "arbitrary", "arbitrary"))

    tn_d = min(d, 1024)
    dw2 = _matmul(
        "mm_dw2", act, dr2b, dims=TN, grid=(N_DEV, d // tn_d, t // tkk),
        a_spec=pl.BlockSpec((tkk, ws_f), lambda m, n, k: (k, m)),
        b_spec=pl.BlockSpec((tkk, tn_d), lambda m, n, k: (k, n)),
        out_shape=[sds((N_DEV, ws_f, d))], out_specs=[pl.BlockSpec((None, ws_f, tn_d), lambda m, n, k: (m, 0, n))],
        acc_shape=(ws_f, tn_d), epilogue=_store_epilogue(F32))[0]

    dw1 = _matmul(
        "mm_dw1", h1b, df1b, dims=TN, grid=(d // tn_d, N_DEV, t // tkk),
        a_spec=pl.BlockSpec((tkk, tn_d), lambda m, n, k: (k, m)),
        b_spec=pl.BlockSpec((tkk, ws_f), lambda m, n, k: (k, n)),
        out_shape=[sds((N_DEV, d, ws_f))], out_specs=[pl.BlockSpec((None, tn_d, ws_f), lambda m, n, k: (n, m, 0))],
        acc_shape=(tn_d, ws_f), epilogue=_store_epilogue(F32))[0]

    def ep_ln1_bwd(acc, ex, outs):
        dr2_ref, r1_ref, g_ref = ex
        xhat, rstd = _ln_stats(r1_ref[...])
        dr1, dg, db = _ln_bwd(DEEPNORM_ALPHA * dr2_ref[...] + acc, xhat, rstd, g_ref[...])
        outs[0][...] = dr1
        outs[1][...] = dr1.astype(BF16)
        first = pl.program_id(0) == 0
        _acc_rows(first, outs[2], dg)
        _acc_rows(first, outs[3], db)

    dr1, dr1b, dg1, db1 = _matmul(
        "mm_dh1_ln1_bwd", df1b, w1_g, dims=NT, grid=(t // tm_ep, 1, f // half_f),
        a_spec=pl.BlockSpec((tm_ep, half_f), lambda m, n, k: (m, k)),
        b_spec=pl.BlockSpec((None, d, half_f), lambda m, n, k: (k // per_f, 0, k % per_f)),
        extras=(dr2, r1, ln1_g), extra_specs=(row_ep, row_ep, vec),
        out_shape=[sds((t, d)), sds((t, d), BF16), sds((1, d)), sds((1, d))], out_specs=[row_ep, row_ep, vec, vec],
        acc_shape=(tm_ep, d), epilogue=ep_ln1_bwd, sem=seq)

    dwout = _matmul(
        "mm_dwout", mixin, dr1b, dims=TN, grid=(d // tn_d, d // tn_d, t // tkk),
        a_spec=pl.BlockSpec((tkk, tn_d), lambda m, n, k: (k, m)),
        b_spec=pl.BlockSpec((tkk, tn_d), lambda m, n, k: (k, n)),
        out_shape=[sds((d, d))], out_specs=[pl.BlockSpec((tn_d, tn_d), lambda m, n, k: (m, n))],
        acc_shape=(tn_d, tn_d), epilogue=_store_epilogue(F32))[0]

    tn_mix = min(tn_d, p, d - p)

    def mm_dmixin(name, first, width, dtype):
        return _matmul(
            name, dr1b, wout_2d, dims=NT, grid=(t // tm_big, width // tn_mix, 1),
            a_spec=pl.BlockSpec((tm_big, d), lambda m, n, k: (m, 0)),
            b_spec=pl.BlockSpec((tn_mix, d), lambda m, n, k: (n + first // tn_mix, 0)),
            out_shape=[sds((t, width), dtype)], out_specs=[pl.BlockSpec((tm_big, tn_mix), lambda m, n, k: (m, n))],
            acc_shape=(tm_big, tn_mix), epilogue=_store_epilogue(dtype))[0]

    dy_pool = mm_dmixin("mm_dmixin_pool", 0, p, F32)
    do = mm_dmixin("mm_dmixin_att", p, d - p, BF16)

    du_pool, dwp, dsc = _pool_bwd(dy_pool, ypre, wp_full, sc_full, t, c_pool)
    dq, dk, dv = _attn_bwd(qkv, do, o, t, n_heads)
    dub = jnp.concatenate([du_pool.astype(BF16), dq, dk, dv], axis=1)

    dwin = _matmul(
        "mm_dwin", h0b, dub, dims=TN, grid=(d // tn_d, N_DEV, t // tkk),
        a_spec=pl.BlockSpec((tkk, tn_d), lambda m, n, k: (k, m)),
        b_spec=pl.BlockSpec((tkk, ws_in), lambda m, n, k: (k, n)),
        out_shape=[sds((N_DEV, d, ws_in))], out_specs=[pl.BlockSpec((None, tn_d, ws_in), lambda m, n, k: (n, m, 0))],
        acc_shape=(tn_d, ws_in), epilogue=_store_epilogue(F32))[0]

    def ep_ln0_bwd(acc, ex, outs):
        dr1_ref, x_ref, g_ref = ex
        xhat, rstd = _ln_stats(x_ref[...])
        dx, dg, db = _ln_bwd(DEEPNORM_ALPHA * dr1_ref[...] + acc, xhat, rstd, g_ref[...])
        outs[0][...] = dx
        first = pl.program_id(0) == 0
        _acc_rows(first, outs[1], dg)
        _acc_rows(first, outs[2], db)

    dx, dg0, db0 = _matmul(
        "mm_dh0_ln0_bwd", dub, win_g, dims=NT, grid=(t // tm_ep, 1, N_DEV),
        a_spec=pl.BlockSpec((tm_ep, ws_in), lambda m, n, k: (m, k)),
        b_spec=pl.BlockSpec((None, d, ws_in), lambda m, n, k: (k, 0, 0)),
        extras=(dr1, x2, g0), extra_specs=(row_ep, row_ep, vec),
        out_shape=[sds((t, d)), sds((1, d)), sds((1, d))], out_specs=[row_ep, vec, vec],
        acc_shape=(tm_ep, d), epilogue=ep_ln0_bwd, sem=seq)

    dwp_g = dwp.reshape(n_groups, N_DEV, pr, c_pool).transpose(1, 0, 2, 3).reshape(N_DEV, n_groups * pr, c_pool)
    partials = [dwin, dwout.reshape(N_DEV, ws_out, d), dw1, dw2, dwp_g]
    names = ["w_in", "w_out", "w_ff1", "w_ff2", "w_pool"]
    from_sibling = _rs_to_sibling(partials)
    slots = _owner_slots()
    chip_sums = [_rs_chip_sum("rs_chip_sum_" + nm, slots, pt, fs) for nm, pt, fs in zip(names, partials, from_sibling)]
    from_chips = _rs_to_owner(chip_sums)
    moments = [(m_w_in, v_w_in), (m_w_out, v_w_out), (m_w_ff1, v_w_ff1), (m_w_ff2, v_w_ff2), (m_w_pool, v_w_pool)]
    big = {}
    for nm, pt, fs, fc, w2d, (m_, v_) in zip(names, partials, from_sibling, from_chips, shards, moments):
        res = _rs_final_adamw("rs_final_adamw_" + nm, slots, pt, fs, fc, w2d, m_.reshape(w2d.shape), v_.reshape(w2d.shape))
        big[nm] = res

    n_f_rows = f // d
    pad_sc = d - p
    packet = jnp.concatenate(
        [loss_vec, dg0, db0, dg1, db1, dbf2, dg2, db2, dbf1.reshape(n_f_rows, d),
         jnp.pad(dsc.reshape(1, p), ((0, 0), (0, pad_sc)))], axis=0)
    n_rows = packet.shape[0]
    n_pad = (-n_rows) % 8
    packet = jnp.pad(packet, ((0, n_pad), (0, 0)))
    sums, loss11 = _small_all_reduce(packet)
    dsc_full = sums[8 + n_f_rows, :p].reshape(n_groups, N_DEV, pr)
    dsc_mine = lax.dynamic_index_in_dim(dsc_full, me, axis=1, keepdims=False)

    def sc_row(a):
        return jnp.pad(a.reshape(1, n_groups * pr), ((0, 0), (0, d - n_groups * pr)))

    def small_pack(ln0g, ln0b, l1g, l1b, bf2, l2g, l2b, bf1, sc):
        rows = [jnp.zeros((1, d), F32), ln0g.reshape(1, d), ln0b.reshape(1, d), l1g, l1b, bf2, l2g, l2b,
                bf1.reshape(n_f_rows, d), sc_row(sc), jnp.zeros((n_pad, d), F32)]
        return jnp.concatenate(rows, axis=0)

    w_small = small_pack(ln_in_g, ln_in_b, ln1_g, ln1_b, b_ff2, ln2_g, ln2_b, b_ff1, pool_scale)
    m_small = small_pack(m_ln_in_g, m_ln_in_b, m_ln1_g, m_ln1_b, m_b_ff2, m_ln2_g, m_ln2_b, m_b_ff1, m_pool_scale)
    v_small = small_pack(v_ln_in_g, v_ln_in_b, v_ln1_g, v_ln1_b, v_b_ff2, v_ln2_g, v_ln2_b, v_b_ff1, v_pool_scale)
    g_small = jnp.concatenate([sums[:8 + n_f_rows], sc_row(dsc_mine), jnp.zeros((n_pad, d), F32)], axis=0)
    small = (g_small,) + tuple(_small_adamw(w_small, g_small, m_small, v_small))

    def unpack(a):
        sc = a[8 + n_f_rows, :n_groups * pr].reshape(1, n_groups, pr)
        return {"ln_in_g": a[1], "ln_in_b": a[2], "ln1_g": a[3:4], "ln1_b": a[4:5], "b_ff2": a[5:6], "ln2_g": a[6:7],
                "ln2_b": a[7:8], "b_ff1": a[8:8 + n_f_rows].reshape(1, f), "pool_scale": sc}

    shapes = {"w_in": w_in.shape, "w_out": w_out.shape, "w_ff1": w_ff1.shape, "w_ff2": w_ff2.shape, "w_pool": w_pool.shape}
    order = ["ln_in_g", "ln_in_b", "w_in", "w_pool", "pool_scale", "w_out", "ln1_g", "ln1_b", "w_ff1", "b_ff1", "w_ff2",
             "b_ff2", "ln2_g", "ln2_b"]
    outs = []
    for kind in range(4):
        small_k = unpack(small[kind])
        for nm in order:
            outs.append(big[nm][kind].reshape(shapes[nm]) if nm in big else small_k[nm])
    return (loss11.reshape(()), dx.reshape(x.shape), *outs)
```

```python
import functools
import math

import jax
import jax.numpy as jnp
from jax import lax
from jax.experimental import pallas as pl
from jax.experimental.pallas import tpu as pltpu

F32 = jnp.float32
BF16 = jnp.bfloat16
MESH = pl.DeviceIdType.MESH

N_DEV = 8
HEAD_DIM = 128
POOL_WINDOWS = (2, 4, 8, 16)
DEEPNORM_ALPHA = (2.0 * 1) ** 0.25
LN_EPS = 1e-5
ADAM_LR = 0.001
ADAM_B1 = 0.9
ADAM_B2 = 0.999
ADAM_EPS = 1e-08
ADAM_WD = 0.01
ADAM_STEP = 10

V7X_VMEM_LIMIT = 56 * 1024 * 1024
ATT_BLOCK = 256
POOL_CHUNK = 128

NN = (((1,), (0,)), ((), ()))
NT = (((1,), (1,)), ((), ()))
TN = (((0,), (0,)), ((), ()))


def _dot(a, b, dims=NN):
    return lax.dot_general(a, b, dims, preferred_element_type=F32)


def _cparams(sem=None):
    return pltpu.CompilerParams(dimension_semantics=sem, vmem_limit_bytes=V7X_VMEM_LIMIT)


def _ln_stats(r):
    mu = jnp.mean(r, axis=-1, keepdims=True)
    xc = r - mu
    var = jnp.mean(xc * xc, axis=-1, keepdims=True)
    rstd = lax.rsqrt(var + LN_EPS)
    return xc * rstd, rstd


def _ln_bwd(dy, xhat, rstd, g):
    dxh = dy * g
    m1 = jnp.mean(dxh, axis=-1, keepdims=True)
    m2 = jnp.mean(dxh * xhat, axis=-1, keepdims=True)
    dx = rstd * (dxh - m1 - xhat * m2)
    dg = jnp.sum(dy * xhat, axis=0, keepdims=True)
    db = jnp.sum(dy, axis=0, keepdims=True)
    return dx, dg, db


def _acc_rows(first, ref, val):
    @pl.when(first)
    def _():
        ref[...] = val

    @pl.when(jnp.logical_not(first))
    def _():
        ref[...] += val


def _matmul(name, a, b, *, dims, grid, a_spec, b_spec, extras=(), extra_specs=(), out_shape, out_specs,
            acc_shape, epilogue, k_axis=2, sem=("parallel", "parallel", "arbitrary"), deps=()):
    nk = grid[k_axis]
    n_extra = len(extras)
    n_out = len(out_shape)
    extras = tuple(extras) + tuple(deps)
    extra_specs = tuple(extra_specs) + tuple(pl.BlockSpec((8, 128), lambda *_: (0, 0)) for _ in deps)
    n_in = len(extras)

    def body(a_ref, b_ref, *rest):
        extra_refs = rest[:n_extra]
        out_refs = rest[n_in:n_in + n_out]
        if nk == 1:
            epilogue(_dot(a_ref[...], b_ref[...], dims), extra_refs, out_refs)
            return
        acc_ref = rest[n_in + n_out]
        k = pl.program_id(k_axis)

        @pl.when(k == 0)
        def _():
            acc_ref[...] = jnp.zeros(acc_shape, F32)

        acc_ref[...] += _dot(a_ref[...], b_ref[...], dims)

        @pl.when(k == nk - 1)
        def _():
            epilogue(acc_ref[...], extra_refs, out_refs)

    return pl.pallas_call(
        body, name=name, grid=grid,
        in_specs=[a_spec, b_spec, *extra_specs],
        out_specs=list(out_specs), out_shape=list(out_shape),
        scratch_shapes=[] if nk == 1 else [pltpu.VMEM(acc_shape, F32)],
        compiler_params=_cparams(sem),
    )(a, b, *extras)


def _store_epilogue(dtype):
    def ep(acc, extra_refs, out_refs):
        out_refs[0][...] = acc.astype(dtype)
    return ep


def _ln_in_fwd(x, g, b, tm):
    t, d = x.shape

    def body(x_ref, g_ref, b_ref, h_ref, hb_ref):
        xhat, _ = _ln_stats(x_ref[...])
        h = xhat * g_ref[...] + b_ref[...]
        h_ref[...] = h
        hb_ref[...] = h.astype(BF16)

    row = pl.BlockSpec((tm, d), lambda i: (i, 0))
    vec = pl.BlockSpec((1, d), lambda i: (0, 0))
    return pl.pallas_call(
        body, name="ln_in_fwd", grid=(t // tm,), in_specs=[row, vec, vec], out_specs=[row, row],
        out_shape=[jax.ShapeDtypeStruct((t, d), F32), jax.ShapeDtypeStruct((t, d), BF16)],
        compiler_params=_cparams(("parallel",)),
    )(x, g, b)


def _split3(x):
    hi = x.astype(BF16)
    r = x - hi.astype(F32)
    mid = r.astype(BF16)
    lo = (r - mid.astype(F32)).astype(BF16)
    return hi, mid, lo


def _split2(x):
    hi = x.astype(BF16)
    lo = (x - hi.astype(F32)).astype(BF16)
    return hi, lo


def _pool_fwd(u, wp, sc, t, c):
    n_groups = len(POOL_WINDOWS)
    tc = POOL_CHUNK
    n_chunks = t // tc

    def body(u_ref, wp_ref, sc_ref, y_ref, ypre_ref, xp_ref):
        g = pl.program_id(0)
        xp_ref[pl.ds(0, tc), :] = jnp.zeros((tc, c), F32)
        xp_ref[pl.ds(tc, t), :] = u_ref[...]
        out_i = lax.broadcasted_iota(jnp.int32, (tc, 2 * tc), 0)
        in_j = lax.broadcasted_iota(jnp.int32, (tc, 2 * tc), 1)
        lag = tc + out_i - in_j
        t_in_chunk = lax.broadcasted_iota(jnp.int32, (tc, 1), 0)
        for gi, w in enumerate(POOL_WINDOWS):
            @pl.when(g == gi)
            def _(w=w):
                band = jnp.logical_and(lag >= 0, lag < w).astype(BF16)

                def chunk(ci, carry):
                    start = pl.multiple_of(ci * tc, tc)
                    win = xp_ref[pl.ds(start, 2 * tc), :]
                    hi, mid, lo = _split3(win)
                    wsum = _dot(band, hi) + _dot(band, mid) + _dot(band, lo)
                    cnt = jnp.minimum(ci * tc + t_in_chunk + 1, w).astype(F32)
                    ypre = wsum * (1.0 / cnt) - win[tc:, :]
                    ypre_b = ypre.astype(BF16)
                    y = _dot(ypre_b, wp_ref[...]) * sc_ref[...]
                    ypre_ref[pl.ds(start, tc), :] = ypre_b
                    y_ref[pl.ds(start, tc), :] = y.astype(BF16)
                    return carry

                lax.fori_loop(0, n_chunks, chunk, 0)

    col = pl.BlockSpec((t, c), lambda g: (0, g))
    return pl.pallas_call(
        body, name="pool_fwd", grid=(n_groups,),
        in_specs=[col, pl.BlockSpec((None, c, c), lambda g: (g, 0, 0)), pl.BlockSpec((None, 1, c), lambda g: (g, 0, 0))],
        out_specs=[col, col],
        out_shape=[jax.ShapeDtypeStruct((t, n_groups * c), BF16), jax.ShapeDtypeStruct((t, n_groups * c), BF16)],
        scratch_shapes=[pltpu.VMEM((t + tc, c), F32)],
        compiler_params=_cparams(("parallel",)),
    )(u, wp, sc)


def _pool_bwd(dmixin, ypre, wp, sc, t, c):
    n_groups = len(POOL_WINDOWS)
    tc = POOL_CHUNK
    n_chunks = t // tc

    def body(dy_ref, ypre_ref, wp_ref, sc_ref, du_ref, dwp_ref, dsc_ref, zp_ref):
        g = pl.program_id(0)
        zp_ref[pl.ds(t, tc), :] = jnp.zeros((tc, c), F32)
        dwp_ref[...] = jnp.zeros((c, c), F32)
        dsc_ref[...] = jnp.zeros((1, c), F32)
        out_i = lax.broadcasted_iota(jnp.int32, (tc, 2 * tc), 0)
        in_j = lax.broadcasted_iota(jnp.int32, (tc, 2 * tc), 1)
        lead = in_j - out_i
        t_in_chunk = lax.broadcasted_iota(jnp.int32, (tc, 1), 0)
        for gi, w in enumerate(POOL_WINDOWS):
            @pl.when(g == gi)
            def _(w=w):
                band = jnp.logical_and(lead >= 0, lead < w).astype(BF16)

                def first(ci, carry):
                    start = pl.multiple_of(ci * tc, tc)
                    dy = dy_ref[pl.ds(start, tc), :]
                    yp = ypre_ref[pl.ds(start, tc), :]
                    ymm = _dot(yp, wp_ref[...])
                    dsc_ref[...] += jnp.sum(dy * ymm, axis=0, keepdims=True)
                    dys_b = (dy * sc_ref[...]).astype(BF16)
                    dwp_ref[...] += _dot(yp, dys_b, TN)
                    dyp = _dot(dys_b, wp_ref[...], NT)
                    cnt = jnp.minimum(ci * tc + t_in_chunk + 1, w).astype(F32)
                    zp_ref[pl.ds(start, tc), :] = dyp * (1.0 / cnt)
                    du_ref[pl.ds(start, tc), :] = -dyp
                    return carry

                lax.fori_loop(0, n_chunks, first, 0)

                def second(ci, carry):
                    start = pl.multiple_of(ci * tc, tc)
                    hi, mid, lo = _split3(zp_ref[pl.ds(start, 2 * tc), :])
                    du_ref[pl.ds(start, tc), :] += _dot(band, hi) + _dot(band, mid) + _dot(band, lo)
                    return carry

                lax.fori_loop(0, n_chunks, second, 0)

    col = pl.BlockSpec((t, c), lambda g: (0, g))
    return pl.pallas_call(
        body, name="pool_bwd", grid=(n_groups,),
        in_specs=[col, col, pl.BlockSpec((None, c, c), lambda g: (g, 0, 0)), pl.BlockSpec((None, 1, c), lambda g: (g, 0, 0))],
        out_specs=[col, pl.BlockSpec((None, c, c), lambda g: (g, 0, 0)), pl.BlockSpec((None, 1, c), lambda g: (g, 0, 0))],
        out_shape=[jax.ShapeDtypeStruct((t, n_groups * c), F32), jax.ShapeDtypeStruct((n_groups, c, c), F32),
                   jax.ShapeDtypeStruct((n_groups, 1, c), F32)],
        scratch_shapes=[pltpu.VMEM((t + tc, c), F32)],
        compiler_params=_cparams(("parallel",)),
    )(dmixin, ypre, wp, sc)


ROW_PARTS = 2


def _att_consts():
    b = ATT_BLOCK
    rp = b // ROW_PARTS
    row = lax.broadcasted_iota(jnp.int32, (b, b), 0)
    col = lax.broadcasted_iota(jnp.int32, (b, b), 1)
    tri = (row >= col).astype(BF16)
    prow = lax.broadcasted_iota(jnp.int32, (rp, b), 0)
    pcol = lax.broadcasted_iota(jnp.int32, (rp, b), 1)
    causal = [pcol < prow + r * rp for r in range(ROW_PARTS)]
    return tri, causal


def _suffix_sum(x, tri):
    hi, lo = _split2(x)
    return _dot(hi, tri) + _dot(lo, tri)


LOG2_E = 1.4426950408889634


def _att_scores(qb, kb, mask):
    z2 = _dot(qb, kb, NT) * (LOG2_E / math.sqrt(HEAD_DIM))
    sp2 = jnp.maximum(z2, 0.0) + jnp.log2(1.0 + jnp.exp2(-jnp.abs(z2)))
    return z2, sp2, (sp2 if mask is None else jnp.where(mask, sp2, 0.0))


HEADS_PER_STEP = 2
ATT_LANES = HEADS_PER_STEP * HEAD_DIM


def _head_lanes(s):
    return slice(s * HEAD_DIM, (s + 1) * HEAD_DIM)


def _attn_fwd(qkv, t, n_heads):
    b = ATT_BLOCK
    nq = t // b
    n_steps = n_heads // HEADS_PER_STEP

    rp = b // ROW_PARTS
    chains = [(s, r) for s in range(HEADS_PER_STEP) for r in range(ROW_PARTS)]
    no_mask = [None] * ROW_PARTS

    def body(q_ref, k_ref, v_ref, o_ref):
        tri, causal = _att_consts()

        def blocks(qbs, j, state, masks):
            ks = pl.multiple_of(j * b, b)
            scores = [_att_scores(qbs[ci], k_ref[pl.ds(ks, b), _head_lanes(s)], masks[r]) for ci, (s, r) in enumerate(chains)]
            incls = [_suffix_sum(sc[2], tri) for sc in scores]
            out = []
            for ci, (s, r) in enumerate(chains):
                carry, acc = state[2 * ci], state[2 * ci + 1]
                a = jnp.exp2(scores[ci][0] - (incls[ci] + carry))
                if masks[r] is not None:
                    a = jnp.where(masks[r], a, 0.0)
                out += [carry + incls[ci][:, 0:1], acc + _dot(a.astype(BF16), v_ref[pl.ds(ks, b), _head_lanes(s)])]
            return tuple(out)

        def q_loop(i, _):
            qs = pl.multiple_of(i * b, b)
            qbs = [q_ref[pl.ds(qs + r * rp, rp), _head_lanes(s)] for s, r in chains]
            zero = (jnp.zeros((rp, 1), F32), jnp.zeros((rp, HEAD_DIM), F32)) * len(chains)
            state = blocks(qbs, i, zero, causal)
            state = lax.fori_loop(0, i, lambda jj, c: blocks(qbs, i - 1 - jj, c, no_mask), state)
            for ci, (s, r) in enumerate(chains):
                o_ref[pl.ds(qs + r * rp, rp), _head_lanes(s)] = state[2 * ci + 1]
            return 0

        lax.fori_loop(0, nq, q_loop, 0)

    def heads(off):
        return pl.BlockSpec((t, ATT_LANES), lambda h: (0, off + h))

    return pl.pallas_call(
        body, name="attn_fwd", grid=(n_steps,),
        in_specs=[heads(0), heads(n_steps), heads(2 * n_steps)],
        out_specs=heads(0),
        out_shape=jax.ShapeDtypeStruct((t, n_heads * HEAD_DIM), F32),
        compiler_params=_cparams(("parallel",)),
    )(qkv, qkv, qkv)


def _attn_bwd(qkv, do, o, t, n_heads, dep):
    b = ATT_BLOCK
    nq = t // b
    n_steps = n_heads // HEADS_PER_STEP
    scale = 1.0 / math.sqrt(HEAD_DIM)
    rp = b // ROW_PARTS
    chains = [(s, r) for s in range(HEADS_PER_STEP) for r in range(ROW_PARTS)]
    no_mask = [None] * ROW_PARTS

    def body(q_ref, k_ref, v_ref, do_ref, o_ref, dep_ref, dq_ref, dk_ref, dv_ref, qt_ref, dot_ref, dkt_ref, dvt_ref):
        for j in range(nq):
            rows = pl.ds(j * b, b)
            qt_ref[j] = q_ref[rows, :].astype(F32).T.astype(BF16)
            dot_ref[j] = do_ref[rows, :].astype(F32).T.astype(BF16)
        dkt_ref[...] = jnp.zeros((nq, ATT_LANES, b), F32)
        dvt_ref[...] = jnp.zeros((nq, ATT_LANES, b), F32)
        tri, causal = _att_consts()

        def blocks(i, fixed, j, state, masks):
            ks = pl.multiple_of(j * b, b)
            n = len(chains)
            kbs = [k_ref[pl.ds(ks, b), _head_lanes(s)] for s, _ in chains]
            scores = [_att_scores(fixed[ci][0], kbs[ci], masks[r]) for ci, (s, r) in enumerate(chains)]
            incls = [_suffix_sum(sc[2], tri) for sc in scores]
            das = [_dot(fixed[ci][1], v_ref[pl.ds(ks, b), _head_lanes(s)], NT) for ci, (s, r) in enumerate(chains)]
            a_bs, gs = [], []
            for ci, (s, r) in enumerate(chains):
                a = jnp.exp2(scores[ci][0] - (incls[ci] + state[3 * ci]))
                if masks[r] is not None:
                    a = jnp.where(masks[r], a, 0.0)
                a_bs.append(a.astype(BF16))
                gs.append(a_bs[ci].astype(F32) * das[ci])
            g_incls = [_suffix_sum(g, tri) for g in gs]
            dz_bs = []
            for ci, (s, r) in enumerate(chains):
                rest = (fixed[ci][2] - state[3 * ci + 1]) - (g_incls[ci] - gs[ci])
                sig = jnp.exp2(scores[ci][0] - scores[ci][1])
                dz = (gs[ci] - sig * rest) * scale
                if masks[r] is not None:
                    dz = jnp.where(masks[r], dz, 0.0)
                dz_bs.append(dz.astype(BF16))
            out = []
            for ci in range(n):
                out += [state[3 * ci] + incls[ci][:, 0:1], state[3 * ci + 1] + g_incls[ci][:, 0:1],
                        state[3 * ci + 2] + _dot(dz_bs[ci], kbs[ci])]
            for s in range(HEADS_PER_STEP):
                lanes = _head_lanes(s)
                dk_add, dv_add = None, None
                for ci, (cs, r) in enumerate(chains):
                    if cs == s:
                        part = slice(r * rp, (r + 1) * rp)
                        dk_c = _dot(qt_ref[i, lanes, part], dz_bs[ci])
                        dv_c = _dot(dot_ref[i, lanes, part], a_bs[ci])
                        dk_add = dk_c if dk_add is None else dk_add + dk_c
                        dv_add = dv_c if dv_add is None else dv_add + dv_c
                dkt_ref[j, lanes, :] += dk_add
                dvt_ref[j, lanes, :] += dv_add
            return tuple(out)

        def q_loop(i, _):
            qs = pl.multiple_of(i * b, b)
            fixed = []
            for s, r in chains:
                rows = pl.ds(qs + r * rp, rp)
                dob = do_ref[rows, _head_lanes(s)]
                total = jnp.sum(dob.astype(F32) * o_ref[rows, _head_lanes(s)], axis=-1, keepdims=True)
                fixed.append((q_ref[rows, _head_lanes(s)], dob, total))
            zero = (jnp.zeros((rp, 1), F32), jnp.zeros((rp, 1), F32), jnp.zeros((rp, HEAD_DIM), F32)) * len(chains)
            state = blocks(i, fixed, i, zero, causal)
            state = lax.fori_loop(0, i, lambda jj, c: blocks(i, fixed, i - 1 - jj, c, no_mask), state)
            for ci, (s, r) in enumerate(chains):
                dq_ref[pl.ds(qs + r * rp, rp), _head_lanes(s)] = state[3 * ci + 2].astype(BF16)
            return 0

        lax.fori_loop(0, nq, q_loop, 0)
        for j in range(nq):
            rows = pl.ds(j * b, b)
            dk_ref[rows, :] = dkt_ref[j].T.astype(BF16)
            dv_ref[rows, :] = dvt_ref[j].T.astype(BF16)

    def heads(off):
        return pl.BlockSpec((t, ATT_LANES), lambda h: (0, off + h))

    shape = jax.ShapeDtypeStruct((t, n_heads * HEAD_DIM), BF16)
    return pl.pallas_call(
        body, name="attn_bwd", grid=(n_steps,),
        in_specs=[heads(0), heads(n_steps), heads(2 * n_steps), heads(0), heads(0), pl.BlockSpec((8, 128), lambda h: (0, 0))],
        out_specs=[heads(0)] * 3, out_shape=[shape] * 3,
        scratch_shapes=[pltpu.VMEM((nq, ATT_LANES, b), BF16)] * 2 + [pltpu.VMEM((nq, ATT_LANES, b), F32)] * 2,
        compiler_params=_cparams(("parallel",)),
    )(qkv, qkv, qkv, do, o, dep)


def _place():
    x, y, c = lax.axis_index("x"), lax.axis_index("y"), lax.axis_index("c")
    return x, y, c


def _all_gather_weights(shards):
    n = len(shards)

    def body(*refs):
        ins, outs = refs[:n], refs[n:2 * n]
        send_sems, recv_sems, local_sems = refs[2 * n:]
        x, y, c = _place()
        me, sibling = (x, y, c), (x, y, 1 - c)
        chips = [(1 - x, y), (x, 1 - y), (1 - x, 1 - y)]

        def slot(px, py, pc):
            return 4 * px + 2 * py + pc

        def copy(ti, k, block, to, src=None):
            dst = outs[ti].at[slot(*block)]
            return pltpu.make_async_remote_copy(
                src_ref=dst if src is None else src, dst_ref=dst,
                send_sem=send_sems.at[ti, k], recv_sem=recv_sems.at[ti, k], device_id=to, device_id_type=MESH)

        mine = [pltpu.make_async_copy(ins[ti], outs[ti].at[slot(*me)], local_sems.at[ti]) for ti in range(n)]
        for cp in mine:
            cp.start()
        first = []
        for ti in range(n):
            first.append(copy(ti, 0, me, sibling, src=ins[ti]))
            first += [copy(ti, 1 + j, me, (*chip, c), src=ins[ti]) for j, chip in enumerate(chips)]
        for cp in first:
            cp.start()
        passed = []
        for j, chip in enumerate(chips):
            for ti in range(n):
                copy(ti, 1 + j, (*chip, c), me).wait_recv()
                fwd = copy(ti, 4 + j, (*chip, c), sibling)
                fwd.start()
                passed.append(fwd)
        for ti in range(n):
            copy(ti, 0, sibling, me).wait_recv()
            for j, chip in enumerate(chips):
                copy(ti, 4 + j, (*chip, 1 - c), me).wait_recv()
        for cp in first + passed:
            cp.wait_send()
        for cp in mine:
            cp.wait()

    any_spec = pl.BlockSpec(memory_space=pl.ANY)
    return pl.pallas_call(
        body, name="all_gather_weights",
        in_specs=[any_spec] * n, out_specs=[any_spec] * n,
        out_shape=[jax.ShapeDtypeStruct((N_DEV, *s.shape), s.dtype) for s in shards],
        scratch_shapes=[pltpu.SemaphoreType.DMA((n, 7)), pltpu.SemaphoreType.DMA((n, 7)), pltpu.SemaphoreType.DMA((n,))],
        compiler_params=pltpu.CompilerParams(has_side_effects=True),
    )(*shards)


HBM_SPEC = pl.BlockSpec(memory_space=pltpu.HBM)
SEM_SPEC = pl.BlockSpec(memory_space=pltpu.SEMAPHORE)
DATAFLOW = pltpu.SideEffectType.DATAFLOW_SIDE_EFFECTING


def _flip(v, on):
    return 1 - v if on else v


def _plan_copies(plan, refs, send_sems, recv_sems):
    return [pltpu.make_async_remote_copy(src_ref=src, dst_ref=dst, send_sem=send_sems.at[k], recv_sem=recv_sems.at[k],
                                         device_id=dev, device_id_type=MESH)
            for k, (src, dst, dev) in enumerate(plan(refs))]


def _copies_start(name, arrays, plan, n_copies):
    n = len(arrays)

    def body(*refs):
        send_sems, recv_sems, token = refs[n], refs[n + 1], refs[2 * n + 2]
        for cp in _plan_copies(plan, refs[:n], send_sems, recv_sems):
            cp.start()
        token[...] = jnp.zeros_like(token)

    outs = pl.pallas_call(
        body, name=name,
        out_shape=(pltpu.SemaphoreType.DMA((n_copies,)), pltpu.SemaphoreType.DMA((n_copies,)),
                   *[pltpu.HBM(a.shape, a.dtype) for a in arrays], jax.ShapeDtypeStruct((8, 128), F32)),
        in_specs=[HBM_SPEC] * n,
        out_specs=(SEM_SPEC, SEM_SPEC, *[HBM_SPEC] * n, pl.BlockSpec(memory_space=pltpu.VMEM)),
        input_output_aliases={i: 2 + i for i in range(n)},
        compiler_params=pltpu.CompilerParams(has_side_effects=DATAFLOW),
    )(*[pltpu.with_memory_space_constraint(a, pltpu.HBM) for a in arrays])
    return outs[0], outs[1], list(outs[2:2 + n]), outs[2 + n]


def _copies_wait(name, send_sems, recv_sems, arrays, plan, after):
    n = len(arrays)

    def body(*refs):
        for cp in _plan_copies(plan, refs[:n], refs[n], refs[n + 1]):
            cp.wait_send()
            cp.wait_recv()

    outs = pl.pallas_call(
        body, name=name,
        out_shape=tuple(pltpu.HBM(a.shape, a.dtype) for a in arrays),
        in_specs=[HBM_SPEC] * n + [SEM_SPEC, SEM_SPEC] + [pl.BlockSpec(memory_space=pl.ANY)] * len(after),
        out_specs=tuple([HBM_SPEC] * n),
        input_output_aliases={i: i for i in range(n)},
        compiler_params=pltpu.CompilerParams(has_side_effects=DATAFLOW),
    )(*arrays, send_sems, recv_sems, *after)
    return list(outs)


def _plan_gather_first(n_direct, n_two_level):
    def plan(refs):
        x, y, c = _place()
        me = 4 * x + 2 * y + c
        out = []
        for ti in range(n_direct + n_two_level):
            mine = refs[ti].at[me]
            for k in range(1, N_DEV):
                if ti < n_direct or k == 1 or not k & 1:
                    out.append((mine, mine, (_flip(x, k & 4), _flip(y, k & 2), _flip(c, k & 1))))
        return out
    return plan, 7 * n_direct + 4 * n_two_level


def _plan_gather_forward(n):
    def plan(refs):
        x, y, c = _place()
        out = []
        for ti in range(n):
            for r in range(1, 4):
                blk = refs[ti].at[4 * _flip(x, r & 2) + 2 * _flip(y, r & 1) + c]
                out.append((blk, blk, (x, y, 1 - c)))
        return out
    return plan, 3 * n


def _plan_rs_sibling(n):
    def plan(refs):
        x, y, c = _place()
        out = []
        for ti in range(n):
            for r in range(4):
                src = refs[ti].at[4 * _flip(x, r & 2) + 2 * _flip(y, r & 1) + (1 - c)]
                out.append((src, refs[n + ti].at[r], (x, y, 1 - c)))
        return out
    return plan, 4 * n


def _plan_rs_owner(n):
    def plan(refs):
        x, y, c = _place()
        out = []
        for ti in range(n):
            for r in range(1, 4):
                out.append((refs[ti].at[r], refs[n + ti].at[r], (_flip(x, r & 2), _flip(y, r & 1), c)))
        return out
    return plan, 3 * n


def _rs_to_sibling(partials):
    n = len(partials)

    def body(*refs):
        ins, outs = refs[:n], refs[n:2 * n]
        send_sems, recv_sems = refs[2 * n:]
        x, y, c = _place()
        sibling = (x, y, 1 - c)
        copies = []
        for ti in range(n):
            for r in range(4):
                ox, oy = (1 - x if r & 2 else x), (1 - y if r & 1 else y)
                cp = pltpu.make_async_remote_copy(
                    src_ref=ins[ti].at[4 * ox + 2 * oy + (1 - c)], dst_ref=outs[ti].at[r],
                    send_sem=send_sems.at[ti, r], recv_sem=recv_sems.at[ti, r], device_id=sibling, device_id_type=MESH)
                cp.start()
                copies.append(cp)
        for cp in copies:
            cp.wait_recv()
        for cp in copies:
            cp.wait_send()

    any_spec = pl.BlockSpec(memory_space=pl.ANY)
    return pl.pallas_call(
        body, name="rs_to_sibling",
        in_specs=[any_spec] * n, out_specs=[any_spec] * n,
        out_shape=[jax.ShapeDtypeStruct((4, *s.shape[1:]), s.dtype) for s in partials],
        scratch_shapes=[pltpu.SemaphoreType.DMA((n, 4)), pltpu.SemaphoreType.DMA((n, 4))],
        compiler_params=pltpu.CompilerParams(has_side_effects=True),
    )(*partials)


def _rs_to_owner(sums):
    n = len(sums)

    def body(*refs):
        ins, outs = refs[:n], refs[n:2 * n]
        send_sems, recv_sems = refs[2 * n:]
        x, y, c = _place()
        copies = []
        for ti in range(n):
            for r in range(1, 4):
                ox, oy = (1 - x if r & 2 else x), (1 - y if r & 1 else y)
                cp = pltpu.make_async_remote_copy(
                    src_ref=ins[ti].at[r], dst_ref=outs[ti].at[r],
                    send_sem=send_sems.at[ti, r], recv_sem=recv_sems.at[ti, r], device_id=(ox, oy, c), device_id_type=MESH)
                cp.start()
                copies.append(cp)
        for cp in copies:
            cp.wait_recv()
        for cp in copies:
            cp.wait_send()

    any_spec = pl.BlockSpec(memory_space=pl.ANY)
    return pl.pallas_call(
        body, name="rs_to_owner",
        in_specs=[any_spec] * n, out_specs=[any_spec] * n,
        out_shape=[jax.ShapeDtypeStruct(s.shape, s.dtype) for s in sums],
        scratch_shapes=[pltpu.SemaphoreType.DMA((n, 4)), pltpu.SemaphoreType.DMA((n, 4))],
        compiler_params=pltpu.CompilerParams(has_side_effects=True),
    )(*sums)


def _owner_slots():
    x, y, c = _place()
    idx = []
    for r in range(4):
        ox, oy = (1 - x if r & 2 else x), (1 - y if r & 1 else y)
        idx.append(4 * ox + 2 * oy + c)
    return jnp.stack(idx).astype(jnp.int32)


def _row_tile(rows, cols):
    tr = max(8, min(rows, (1 << 19) // cols))
    while rows % tr:
        tr //= 2
    return tr


def _rs_chip_sum(name, slots, partial, from_sibling):
    _, rows, cols = partial.shape
    tr = _row_tile(rows, cols)

    def body(slots_ref, p_ref, s_ref, o_ref):
        o_ref[...] = (p_ref[...] + s_ref[...]).astype(BF16)

    grid_spec = pltpu.PrefetchScalarGridSpec(
        num_scalar_prefetch=1, grid=(3, rows // tr),
        in_specs=[pl.BlockSpec((None, tr, cols), lambda r, i, s: (s[r + 1], i, 0)),
                  pl.BlockSpec((None, tr, cols), lambda r, i, s: (r + 1, i, 0))],
        out_specs=pl.BlockSpec((None, tr, cols), lambda r, i, s: (r + 1, i, 0)))
    return pl.pallas_call(
        body, name=name, grid_spec=grid_spec, out_shape=jax.ShapeDtypeStruct((4, rows, cols), BF16),
        compiler_params=_cparams(("parallel", "parallel")),
    )(slots, partial, from_sibling)


def _adamw(w, g, m, v):
    m = ADAM_B1 * m + (1.0 - ADAM_B1) * g
    v = ADAM_B2 * v + (1.0 - ADAM_B2) * (g * g)
    m_hat = m / (1.0 - ADAM_B1 ** ADAM_STEP)
    v_hat = v / (1.0 - ADAM_B2 ** ADAM_STEP)
    delta = -ADAM_LR * (m_hat / (jnp.sqrt(v_hat) + ADAM_EPS) + ADAM_WD * w)
    return delta, m, v


def _rs_final_adamw(name, slots, partial, from_sibling, from_chips, w, m, v):
    rows, cols = w.shape
    tr = _row_tile(rows, cols)

    def body(slots_ref, p_ref, s_ref, c1_ref, c2_ref, c3_ref, w_ref, m_ref, v_ref, g_ref, d_ref, nm_ref, nv_ref):
        g = p_ref[...] + s_ref[...]
        g = g + c1_ref[...].astype(F32)
        g = g + c2_ref[...].astype(F32)
        g = g + c3_ref[...].astype(F32)
        delta, nm, nv = _adamw(w_ref[...], g, m_ref[...], v_ref[...])
        g_ref[...] = g
        d_ref[...] = delta
        nm_ref[...] = nm
        nv_ref[...] = nv

    def slot(r):
        return pl.BlockSpec((None, tr, cols), lambda i, s: (r, i, 0))

    flat = pl.BlockSpec((tr, cols), lambda i, s: (i, 0))
    grid_spec = pltpu.PrefetchScalarGridSpec(
        num_scalar_prefetch=1, grid=(rows // tr,),
        in_specs=[pl.BlockSpec((None, tr, cols), lambda i, s: (s[0], i, 0)), slot(0), slot(1), slot(2), slot(3), flat, flat, flat],
        out_specs=[flat] * 4)
    return pl.pallas_call(
        body, name=name, grid_spec=grid_spec, out_shape=[jax.ShapeDtypeStruct((rows, cols), F32)] * 4,
        compiler_params=_cparams(("parallel",)),
    )(slots, partial, from_sibling, from_chips, from_chips, from_chips, w, m, v)


def _small_all_reduce(packet):
    rows, d = packet.shape

    def body(p_ref, sum_ref, loss_ref, all_ref, send_sems, recv_sems):
        x, y, c = _place()
        me = 4 * x + 2 * y + c
        all_ref[me] = p_ref[...]
        copies = []
        for k in range(1, N_DEV):
            px, py, pc = (1 - x if k & 4 else x), (1 - y if k & 2 else y), (1 - c if k & 1 else c)
            cp = pltpu.make_async_remote_copy(
                src_ref=p_ref, dst_ref=all_ref.at[me], send_sem=send_sems.at[k], recv_sem=recv_sems.at[k],
                device_id=(px, py, pc), device_id_type=MESH)
            cp.start()
            copies.append(cp)
        for cp in copies:
            cp.wait_recv()
        for cp in copies:
            cp.wait_send()
        total = all_ref[0]
        for j in range(1, N_DEV):
            total = total + all_ref[j]
        sum_ref[...] = total
        loss_ref[...] = jnp.sum(total[0:1, :], axis=-1, keepdims=True)

    vmem = pl.BlockSpec(memory_space=pltpu.VMEM)
    return pl.pallas_call(
        body, name="small_all_reduce",
        in_specs=[vmem], out_specs=[vmem, vmem],
        out_shape=[jax.ShapeDtypeStruct((rows, d), F32), jax.ShapeDtypeStruct((1, 1), F32)],
        scratch_shapes=[pltpu.VMEM((N_DEV, rows, d), F32), pltpu.SemaphoreType.DMA((N_DEV,)), pltpu.SemaphoreType.DMA((N_DEV,))],
        compiler_params=pltpu.CompilerParams(has_side_effects=True),
    )(packet)


def _small_adamw(w, g, m, v):
    def body(w_ref, g_ref, m_ref, v_ref, d_ref, nm_ref, nv_ref):
        delta, nm, nv = _adamw(w_ref[...], g_ref[...], m_ref[...], v_ref[...])
        d_ref[...] = delta
        nm_ref[...] = nm
        nv_ref[...] = nv

    vmem = pl.BlockSpec(memory_space=pltpu.VMEM)
    return pl.pallas_call(
        body, name="small_adamw", in_specs=[vmem] * 4, out_specs=[vmem] * 3,
        out_shape=[jax.ShapeDtypeStruct(w.shape, F32)] * 3,
    )(w, g, m, v)


def kernel(x, ln_in_g, ln_in_b, w_in, w_pool, pool_scale, w_out, ln1_g, ln1_b, w_ff1, b_ff1, w_ff2, b_ff2, ln2_g, ln2_b, loss_target, m_ln_in_g, m_ln_in_b, m_w_in, m_w_pool, m_pool_scale, m_w_out, m_ln1_g, m_ln1_b, m_w_ff1, m_b_ff1, m_w_ff2, m_b_ff2, m_ln2_g, m_ln2_b, v_ln_in_g, v_ln_in_b, v_w_in, v_w_pool, v_pool_scale, v_w_out, v_ln1_g, v_ln1_b, v_w_ff1, v_b_ff1, v_w_ff2, v_b_ff2, v_ln2_g, v_ln2_b):
    t, d = x.shape[1], x.shape[2]
    n_groups = len(POOL_WINDOWS)
    c_pool = w_pool.shape[3]
    p = n_groups * c_pool
    n_heads = (d - p) // HEAD_DIM
    ws_in = w_in.shape[2]
    n_in = N_DEV * ws_in
    ws_out = w_out.shape[1]
    ws_f = w_ff1.shape[2]
    f = N_DEV * ws_f
    pr = w_pool.shape[2]
    assert n_in == p + 3 * n_heads * HEAD_DIM and N_DEV * ws_out == d and N_DEV * pr == c_pool

    tm_big = min(t, 1024)
    tm_ep = min(t, 512)
    tkk = min(t, 512)
    half_f = min(ws_f, 512)
    per_f = ws_f // half_f

    x2 = x.reshape(t, d)
    target = loss_target.reshape(t, d)
    g0, b0 = ln_in_g.reshape(1, d), ln_in_b.reshape(1, d)

    shards = [w_in.reshape(d, ws_in), w_out.reshape(ws_out, d), w_ff1.reshape(d, ws_f), w_ff2.reshape(ws_f, d),
              w_pool.reshape(n_groups * pr, c_pool)]
    x_, y_, c_ = _place()
    me = 4 * x_ + 2 * y_ + c_
    win_g, wpool_g, scale_g = _all_gather_weights(
        [shards[0].astype(BF16), shards[4].astype(BF16), pool_scale.reshape(n_groups, pr)])
    lands = [lax.dynamic_update_index_in_dim(lax.empty((N_DEV, *s.shape), BF16), s.astype(BF16), me, 0) for s in shards[1:4]]
    plan_g1, n_g1 = _plan_gather_first(1, 2)
    g1_send, g1_recv, lands, token_g1 = _copies_start("gather_start", lands, plan_g1, n_g1)
    wp_full = wpool_g.reshape(N_DEV, n_groups, pr, c_pool).transpose(1, 0, 2, 3).reshape(n_groups, c_pool, c_pool)
    sc_full = scale_g.transpose(1, 0, 2).reshape(n_groups, 1, c_pool)

    def sds(shape, dtype=F32):
        return jax.ShapeDtypeStruct(shape, dtype)

    vec = pl.BlockSpec((1, d), lambda m, n, k: (0, 0))
    row_ep = pl.BlockSpec((tm_ep, d), lambda m, n, k: (m, 0))
    seq = ("arbitrary", "arbitrary", "arbitrary")

    h0, h0b = _ln_in_fwd(x2, g0, b0, tm_big)

    pool_shards = p // ws_in

    def mm_u(name, first, count, dtype):
        return _matmul(
            name, h0b, win_g, dims=NN, grid=(t // tm_big, count, 1),
            a_spec=pl.BlockSpec((tm_big, d), lambda m, n, k: (m, 0)),
            b_spec=pl.BlockSpec((None, d, ws_in), lambda m, n, k: (n + first, 0, 0)),
            out_shape=[sds((t, count * ws_in), dtype)],
            out_specs=[pl.BlockSpec((tm_big, ws_in), lambda m, n, k: (m, n))],
            acc_shape=(tm_big, ws_in), epilogue=_store_epilogue(dtype), deps=(token_g1,))[0]

    u_pool = mm_u("mm_u_pool", 0, pool_shards, F32)
    qkv = mm_u("mm_u_qkv", pool_shards, N_DEV - pool_shards, BF16)

    y_pool, ypre = _pool_fwd(u_pool, wp_full, sc_full, t, c_pool)
    o = _attn_fwd(qkv, t, n_heads)
    mixin = jnp.concatenate([y_pool, o.astype(BF16)], axis=1)

    wout_g, w1_part, w2_part = _copies_wait("gather_wait", g1_send, g1_recv, lands, plan_g1, [mixin])
    plan_g2, n_g2 = _plan_gather_forward(2)
    g2_send, g2_recv, fwd_lands, token_g2 = _copies_start("gather_forward_start", [w1_part, w2_part], plan_g2, n_g2)
    wout_2d = wout_g.reshape(d, d)

    def ep_ln1(acc, ex, outs):
        h0_ref, g_ref, b_ref = ex
        r1 = DEEPNORM_ALPHA * h0_ref[...] + acc
        xhat, _ = _ln_stats(r1)
        h1 = xhat * g_ref[...] + b_ref[...]
        outs[0][...] = r1
        outs[1][...] = h1
        outs[2][...] = h1.astype(BF16)

    r1, h1, h1b = _matmul(
        "mm_mix_ln1", mixin, wout_g, dims=NN, grid=(t // tm_ep, 1, N_DEV),
        a_spec=pl.BlockSpec((tm_ep, ws_out), lambda m, n, k: (m, k)),
        b_spec=pl.BlockSpec((None, ws_out, d), lambda m, n, k: (k, 0, 0)),
        extras=(h0, ln1_g, ln1_b), extra_specs=(row_ep, vec, vec),
        out_shape=[sds((t, d)), sds((t, d)), sds((t, d), BF16)], out_specs=[row_ep] * 3,
        acc_shape=(tm_ep, d), epilogue=ep_ln1, deps=(token_g2,))
    w1_g, w2_g = _copies_wait("gather_forward_wait", g2_send, g2_recv, fwd_lands, plan_g2, [h1b])

    def ep_ff1(acc, ex, outs):
        f1 = acc + ex[0][...]
        outs[0][...] = f1
        r = jnp.maximum(f1, 0.0)
        outs[1][...] = (r * r).astype(BF16)

    ff_tile = pl.BlockSpec((tm_big, half_f), lambda m, n, k: (m, n))
    f1, act = _matmul(
        "mm_ff1", h1b, w1_g, dims=NN, grid=(t // tm_big, f // half_f, 1),
        a_spec=pl.BlockSpec((tm_big, d), lambda m, n, k: (m, 0)),
        b_spec=pl.BlockSpec((None, d, half_f), lambda m, n, k: (n // per_f, 0, n % per_f)),
        extras=(b_ff1,), extra_specs=(pl.BlockSpec((1, half_f), lambda m, n, k: (0, n)),),
        out_shape=[sds((t, f)), sds((t, f), BF16)], out_specs=[ff_tile, ff_tile],
        acc_shape=(tm_big, half_f), epilogue=ep_ff1)

    def ep_ln2(acc, ex, outs):
        h1_ref, tgt_ref, bf2_ref, g_ref, b_ref = ex
        dr2_ref, dr2b_ref, dg_ref, db_ref, dbf2_ref, loss_ref = outs
        r2 = DEEPNORM_ALPHA * h1_ref[...] + (acc + bf2_ref[...])
        xhat, rstd = _ln_stats(r2)
        err = xhat * g_ref[...] + b_ref[...] - tgt_ref[...]
        dr2, dg, db = _ln_bwd(err * (1.0 / d), xhat, rstd, g_ref[...])
        dr2_ref[...] = dr2
        dr2b_ref[...] = dr2.astype(BF16)
        first = pl.program_id(0) == 0
        _acc_rows(first, dg_ref, dg)
        _acc_rows(first, db_ref, db)
        _acc_rows(first, dbf2_ref, jnp.sum(dr2, axis=0, keepdims=True))
        _acc_rows(first, loss_ref, jnp.sum(err * err, axis=0, keepdims=True) * (0.5 / d))

    dr2, dr2b, dg2, db2, dbf2, loss_vec = _matmul(
        "mm_ff2_ln2_loss", act, w2_g, dims=NN, grid=(t // tm_ep, 1, f // half_f),
        a_spec=pl.BlockSpec((tm_ep, half_f), lambda m, n, k: (m, k)),
        b_spec=pl.BlockSpec((None, half_f, d), lambda m, n, k: (k // per_f, k % per_f, 0)),
        extras=(h1, target, b_ff2, ln2_g, ln2_b), extra_specs=(row_ep, row_ep, vec, vec, vec),
        out_shape=[sds((t, d)), sds((t, d), BF16)] + [sds((1, d))] * 4, out_specs=[row_ep, row_ep, vec, vec, vec, vec],
        acc_shape=(tm_ep, d), epilogue=ep_ln2, sem=seq)

    def ep_dff1(acc, ex, outs):
        df1 = acc * (2.0 * jnp.maximum(ex[0][...], 0.0))
        outs[0][...] = df1.astype(BF16)
        _acc_rows(pl.program_id(1) == 0, outs[1], jnp.sum(df1, axis=0, keepdims=True))

    df_tile = pl.BlockSpec((tm_big, ws_f), lambda n, m, k: (m, n))
    df1b, dbf1 = _matmul(
        "mm_dff1", dr2b, w2_g, dims=NT, grid=(N_DEV, t // tm_big, 1),
        a_spec=pl.BlockSpec((tm_big, d), lambda n, m, k: (m, 0)),
        b_spec=pl.BlockSpec((None, ws_f, d), lambda n, m, k: (n, 0, 0)),
        extras=(f1,), extra_specs=(df_tile,),
        out_shape=[sds((t, f), BF16), sds((1, f))], out_specs=[df_tile, pl.BlockSpec((1, ws_f), lambda n, m, k: (0, n))],
        acc_shape=(tm_big, ws_f), epilogue=ep_dff1, sem=("parallel", "arbitrary", "arbitrary"))

    tn_d = min(d, 1024)
    dw2 = _matmul(
        "mm_dw2", act, dr2b, dims=TN, grid=(N_DEV, d // tn_d, t // tkk),
        a_spec=pl.BlockSpec((tkk, ws_f), lambda m, n, k: (k, m)),
        b_spec=pl.BlockSpec((tkk, tn_d), lambda m, n, k: (k, n)),
        out_shape=[sds((N_DEV, ws_f, d))], out_specs=[pl.BlockSpec((None, ws_f, tn_d), lambda m, n, k: (m, 0, n))],
        acc_shape=(ws_f, tn_d), epilogue=_store_epilogue(F32))[0]

    dw1 = _matmul(
        "mm_dw1", h1b, df1b, dims=TN, grid=(d // tn_d, N_DEV, t // tkk),
        a_spec=pl.BlockSpec((tkk, tn_d), lambda m, n, k: (k, m)),
        b_spec=pl.BlockSpec((tkk, ws_f), lambda m, n, k: (k, n)),
        out_shape=[sds((N_DEV, d, ws_f))], out_specs=[pl.BlockSpec((None, tn_d, ws_f), lambda m, n, k: (n, m, 0))],
        acc_shape=(tn_d, ws_f), epilogue=_store_epilogue(F32))[0]

    def ep_ln1_bwd(acc, ex, outs):
        dr2_ref, r1_ref, g_ref = ex
        xhat, rstd = _ln_stats(r1_ref[...])
        dr1, dg, db = _ln_bwd(DEEPNORM_ALPHA * dr2_ref[...] + acc, xhat, rstd, g_ref[...])
        outs[0][...] = dr1
        outs[1][...] = dr1.astype(BF16)
        first = pl.program_id(0) == 0
        _acc_rows(first, outs[2], dg)
        _acc_rows(first, outs[3], db)

    dr1, dr1b, dg1, db1 = _matmul(
        "mm_dh1_ln1_bwd", df1b, w1_g, dims=NT, grid=(t // tm_ep, 1, f // half_f),
        a_spec=pl.BlockSpec((tm_ep, half_f), lambda m, n, k: (m, k)),
        b_spec=pl.BlockSpec((None, d, half_f), lambda m, n, k: (k // per_f, 0, k % per_f)),
        extras=(dr2, r1, ln1_g), extra_specs=(row_ep, row_ep, vec),
        out_shape=[sds((t, d)), sds((t, d), BF16), sds((1, d)), sds((1, d))], out_specs=[row_ep, row_ep, vec, vec],
        acc_shape=(tm_ep, d), epilogue=ep_ln1_bwd, sem=seq)

    dwout = _matmul(
        "mm_dwout", mixin, dr1b, dims=TN, grid=(d // tn_d, d // tn_d, t // tkk),
        a_spec=pl.BlockSpec((tkk, tn_d), lambda m, n, k: (k, m)),
        b_spec=pl.BlockSpec((tkk, tn_d), lambda m, n, k: (k, n)),
        out_shape=[sds((d, d))], out_specs=[pl.BlockSpec((tn_d, tn_d), lambda m, n, k: (m, n))],
        acc_shape=(tn_d, tn_d), epilogue=_store_epilogue(F32))[0]

    slots = _owner_slots()

    def rs_sibling_start(tag, parts):
        lands_ = [lax.empty((4, *pt.shape[1:]), F32) for pt in parts]
        plan, n_cp = _plan_rs_sibling(len(parts))
        send, recv, arrs, token = _copies_start("rs_sibling_start_" + tag, parts + lands_, plan, n_cp)
        return (send, recv, arrs, plan), token

    def rs_sibling_wait(tag, started, after):
        send, recv, arrs, plan = started
        arrs = _copies_wait("rs_sibling_wait_" + tag, send, recv, arrs, plan, after)
        return arrs[:len(arrs) // 2], arrs[len(arrs) // 2:]

    def rs_owner_start(tag, names_, parts, from_sib):
        sums = [_rs_chip_sum("rs_chip_sum_" + nm, slots, pt, fs) for nm, pt, fs in zip(names_, parts, from_sib)]
        lands_ = [lax.empty(cs.shape, BF16) for cs in sums]
        plan, n_cp = _plan_rs_owner(len(sums))
        send, recv, arrs, token = _copies_start("rs_owner_start_" + tag, sums + lands_, plan, n_cp)
        return (send, recv, arrs, plan), token

    def rs_owner_wait(tag, started, after):
        send, recv, arrs, plan = started
        return _copies_wait("rs_owner_wait_" + tag, send, recv, arrs, plan, after)[len(arrs) // 2:]

    names1 = ["w_ff1", "w_ff2", "w_out"]
    sib1, token_a1 = rs_sibling_start("1", [dw1, dw2, dwout.reshape(N_DEV, ws_out, d)])

    tn_mix = min(tn_d, p, d - p)

    def mm_dmixin(name, first, width, dtype):
        return _matmul(
            name, dr1b, wout_2d, dims=NT, grid=(t // tm_big, width // tn_mix, 1),
            a_spec=pl.BlockSpec((tm_big, d), lambda m, n, k: (m, 0)),
            b_spec=pl.BlockSpec((tn_mix, d), lambda m, n, k: (n + first // tn_mix, 0)),
            out_shape=[sds((t, width), dtype)], out_specs=[pl.BlockSpec((tm_big, tn_mix), lambda m, n, k: (m, n))],
            acc_shape=(tm_big, tn_mix), epilogue=_store_epilogue(dtype), deps=(token_a1,))[0]

    dy_pool = mm_dmixin("mm_dmixin_pool", 0, p, F32)
    do = mm_dmixin("mm_dmixin_att", p, d - p, BF16)

    du_pool, dwp, dsc = _pool_bwd(dy_pool, ypre, wp_full, sc_full, t, c_pool)
    parts1, from_sib1 = rs_sibling_wait("1", sib1, [du_pool, do])
    own1, token_c1 = rs_owner_start("1", names1, parts1, from_sib1)

    dq, dk, dv = _attn_bwd(qkv, do, o, t, n_heads, token_c1)
    dub = jnp.concatenate([du_pool.astype(BF16), dq, dk, dv], axis=1)

    dwin = _matmul(
        "mm_dwin", h0b, dub, dims=TN, grid=(d // tn_d, N_DEV, t // tkk),
        a_spec=pl.BlockSpec((tkk, tn_d), lambda m, n, k: (k, m)),
        b_spec=pl.BlockSpec((tkk, ws_in), lambda m, n, k: (k, n)),
        out_shape=[sds((N_DEV, d, ws_in))], out_specs=[pl.BlockSpec((None, tn_d, ws_in), lambda m, n, k: (n, m, 0))],
        acc_shape=(tn_d, ws_in), epilogue=_store_epilogue(F32))[0]

    names2 = ["w_in", "w_pool"]
    dwp_g = dwp.reshape(n_groups, N_DEV, pr, c_pool).transpose(1, 0, 2, 3).reshape(N_DEV, n_groups * pr, c_pool)
    sib2, token_a2 = rs_sibling_start("2", [dwin, dwp_g])

    def ep_ln0_bwd(acc, ex, outs):
        dr1_ref, x_ref, g_ref = ex
        xhat, rstd = _ln_stats(x_ref[...])
        dx, dg, db = _ln_bwd(DEEPNORM_ALPHA * dr1_ref[...] + acc, xhat, rstd, g_ref[...])
        outs[0][...] = dx
        first = pl.program_id(0) == 0
        _acc_rows(first, outs[1], dg)
        _acc_rows(first, outs[2], db)

    dx, dg0, db0 = _matmul(
        "mm_dh0_ln0_bwd", dub, win_g, dims=NT, grid=(t // tm_ep, 1, N_DEV),
        a_spec=pl.BlockSpec((tm_ep, ws_in), lambda m, n, k: (m, k)),
        b_spec=pl.BlockSpec((None, d, ws_in), lambda m, n, k: (k, 0, 0)),
        extras=(dr1, x2, g0), extra_specs=(row_ep, row_ep, vec),
        out_shape=[sds((t, d)), sds((1, d)), sds((1, d))], out_specs=[row_ep, vec, vec],
        acc_shape=(tm_ep, d), epilogue=ep_ln0_bwd, sem=seq, deps=(token_a2,))

    parts2, from_sib2 = rs_sibling_wait("2", sib2, [dx])
    own2, token_c2 = rs_owner_start("2", names2, parts2, from_sib2)
    from_chips1 = rs_owner_wait("1", own1, [token_c2])
    w_of = {"w_in": shards[0], "w_out": shards[1], "w_ff1": shards[2], "w_ff2": shards[3], "w_pool": shards[4]}
    mv_of = {"w_in": (m_w_in, v_w_in), "w_out": (m_w_out, v_w_out), "w_ff1": (m_w_ff1, v_w_ff1),
             "w_ff2": (m_w_ff2, v_w_ff2), "w_pool": (m_w_pool, v_w_pool)}
    big = {}

    def finals(names_, parts, from_sib, from_chips):
        for nm, pt, fs, fc in zip(names_, parts, from_sib, from_chips):
            w2d = w_of[nm]
            m_, v_ = mv_of[nm]
            big[nm] = _rs_final_adamw("rs_final_adamw_" + nm, slots, pt, fs, fc, w2d, m_.reshape(w2d.shape), v_.reshape(w2d.shape))

    finals(names1, parts1, from_sib1, from_chips1)

    n_f_rows = f // d
    pad_sc = d - p
    packet = jnp.concatenate(
        [loss_vec, dg0, db0, dg1, db1, dbf2, dg2, db2, dbf1.reshape(n_f_rows, d),
         jnp.pad(dsc.reshape(1, p), ((0, 0), (0, pad_sc)))], axis=0)
    n_rows = packet.shape[0]
    n_pad = (-n_rows) % 8
    packet = jnp.pad(packet, ((0, n_pad), (0, 0)))
    sums, loss11 = _small_all_reduce(packet)
    dsc_full = sums[8 + n_f_rows, :p].reshape(n_groups, N_DEV, pr)
    dsc_mine = lax.dynamic_index_in_dim(dsc_full, me, axis=1, keepdims=False)

    def sc_row(a):
        return jnp.pad(a.reshape(1, n_groups * pr), ((0, 0), (0, d - n_groups * pr)))

    def small_pack(ln0g, ln0b, l1g, l1b, bf2, l2g, l2b, bf1, sc):
        rows = [jnp.zeros((1, d), F32), ln0g.reshape(1, d), ln0b.reshape(1, d), l1g, l1b, bf2, l2g, l2b,
                bf1.reshape(n_f_rows, d), sc_row(sc), jnp.zeros((n_pad, d), F32)]
        return jnp.concatenate(rows, axis=0)

    w_small = small_pack(ln_in_g, ln_in_b, ln1_g, ln1_b, b_ff2, ln2_g, ln2_b, b_ff1, pool_scale)
    m_small = small_pack(m_ln_in_g, m_ln_in_b, m_ln1_g, m_ln1_b, m_b_ff2, m_ln2_g, m_ln2_b, m_b_ff1, m_pool_scale)
    v_small = small_pack(v_ln_in_g, v_ln_in_b, v_ln1_g, v_ln1_b, v_b_ff2, v_ln2_g, v_ln2_b, v_b_ff1, v_pool_scale)
    g_small = jnp.concatenate([sums[:8 + n_f_rows], sc_row(dsc_mine), jnp.zeros((n_pad, d), F32)], axis=0)
    small = (g_small,) + tuple(_small_adamw(w_small, g_small, m_small, v_small))

    from_chips2 = rs_owner_wait("2", own2, [small[1]] + [big[nm][0] for nm in names1])
    finals(names2, parts2, from_sib2, from_chips2)

    def unpack(a):
        sc = a[8 + n_f_rows, :n_groups * pr].reshape(1, n_groups, pr)
        return {"ln_in_g": a[1], "ln_in_b": a[2], "ln1_g": a[3:4], "ln1_b": a[4:5], "b_ff2": a[5:6], "ln2_g": a[6:7],
                "ln2_b": a[7:8], "b_ff1": a[8:8 + n_f_rows].reshape(1, f), "pool_scale": sc}

    shapes = {"w_in": w_in.shape, "w_out": w_out.shape, "w_ff1": w_ff1.shape, "w_ff2": w_ff2.shape, "w_pool": w_pool.shape}
    order = ["ln_in_g", "ln_in_b", "w_in", "w_pool", "pool_scale", "w_out", "ln1_g", "ln1_b", "w_ff1", "b_ff1", "w_ff2",
             "b_ff2", "ln2_g", "ln2_b"]
    outs = []
    for kind in range(4):
        small_k = unpack(small[kind])
        for nm in order:
            outs.append(big[nm][kind].reshape(shapes[nm]) if nm in big else small_k[nm])
    return (loss11.reshape(()), dx.reshape(x.shape), *outs)
```

```python
import functools
import math

import jax
import jax.numpy as jnp
from jax import lax
from jax.experimental import pallas as pl
from jax.experimental.pallas import tpu as pltpu

F32 = jnp.float32
BF16 = jnp.bfloat16
MESH = pl.DeviceIdType.MESH

N_DEV = 8
HEAD_DIM = 128
POOL_WINDOWS = (2, 4, 8, 16)
DEEPNORM_ALPHA = (2.0 * 1) ** 0.25
LN_EPS = 1e-5
ADAM_LR = 0.001
ADAM_B1 = 0.9
ADAM_B2 = 0.999
ADAM_EPS = 1e-08
ADAM_WD = 0.01
ADAM_STEP = 10

V7X_VMEM_LIMIT = 56 * 1024 * 1024
ATT_BLOCK = 256
POOL_CHUNK = 128

NN = (((1,), (0,)), ((), ()))
NT = (((1,), (1,)), ((), ()))
TN = (((0,), (0,)), ((), ()))


def _dot(a, b, dims=NN):
    return lax.dot_general(a, b, dims, preferred_element_type=F32)


def _cparams(sem=None):
    return pltpu.CompilerParams(dimension_semantics=sem, vmem_limit_bytes=V7X_VMEM_LIMIT)


def _ln_stats(r):
    mu = jnp.mean(r, axis=-1, keepdims=True)
    xc = r - mu
    var = jnp.mean(xc * xc, axis=-1, keepdims=True)
    rstd = lax.rsqrt(var + LN_EPS)
    return xc * rstd, rstd


def _ln_bwd(dy, xhat, rstd, g):
    dxh = dy * g
    m1 = jnp.mean(dxh, axis=-1, keepdims=True)
    m2 = jnp.mean(dxh * xhat, axis=-1, keepdims=True)
    dx = rstd * (dxh - m1 - xhat * m2)
    dg = jnp.sum(dy * xhat, axis=0, keepdims=True)
    db = jnp.sum(dy, axis=0, keepdims=True)
    return dx, dg, db


def _acc_rows(first, ref, val):
    @pl.when(first)
    def _():
        ref[...] = val

    @pl.when(jnp.logical_not(first))
    def _():
        ref[...] += val


def _matmul(name, a, b, *, dims, grid, a_spec, b_spec, extras=(), extra_specs=(), out_shape, out_specs,
            acc_shape, epilogue, k_axis=2, sem=("parallel", "parallel", "arbitrary"), deps=()):
    nk = grid[k_axis]
    n_extra = len(extras)
    n_out = len(out_shape)
    extras = tuple(extras) + tuple(deps)
    extra_specs = tuple(extra_specs) + tuple(pl.BlockSpec((8, 128), lambda *_: (0, 0)) for _ in deps)
    n_in = len(extras)

    def body(a_ref, b_ref, *rest):
        extra_refs = rest[:n_extra]
        out_refs = rest[n_in:n_in + n_out]
        if nk == 1:
            epilogue(_dot(a_ref[...], b_ref[...], dims), extra_refs, out_refs)
            return
        acc_ref = rest[n_in + n_out]
        k = pl.program_id(k_axis)

        @pl.when(k == 0)
        def _():
            acc_ref[...] = jnp.zeros(acc_shape, F32)

        acc_ref[...] += _dot(a_ref[...], b_ref[...], dims)

        @pl.when(k == nk - 1)
        def _():
            epilogue(acc_ref[...], extra_refs, out_refs)

    return pl.pallas_call(
        body, name=name, grid=grid,
        in_specs=[a_spec, b_spec, *extra_specs],
        out_specs=list(out_specs), out_shape=list(out_shape),
        scratch_shapes=[] if nk == 1 else [pltpu.VMEM(acc_shape, F32)],
        compiler_params=_cparams(sem),
    )(a, b, *extras)


def _store_epilogue(dtype):
    def ep(acc, extra_refs, out_refs):
        out_refs[0][...] = acc.astype(dtype)
    return ep


def _ln_in_fwd(x, g, b, tm):
    t, d = x.shape

    def body(x_ref, g_ref, b_ref, h_ref, hb_ref):
        xhat, _ = _ln_stats(x_ref[...])
        h = xhat * g_ref[...] + b_ref[...]
        h_ref[...] = h
        hb_ref[...] = h.astype(BF16)

    row = pl.BlockSpec((tm, d), lambda i: (i, 0))
    vec = pl.BlockSpec((1, d), lambda i: (0, 0))
    return pl.pallas_call(
        body, name="ln_in_fwd", grid=(t // tm,), in_specs=[row, vec, vec], out_specs=[row, row],
        out_shape=[jax.ShapeDtypeStruct((t, d), F32), jax.ShapeDtypeStruct((t, d), BF16)],
        compiler_params=_cparams(("parallel",)),
    )(x, g, b)


def _split3(x):
    hi = x.astype(BF16)
    r = x - hi.astype(F32)
    mid = r.astype(BF16)
    lo = (r - mid.astype(F32)).astype(BF16)
    return hi, mid, lo


def _split2(x):
    hi = x.astype(BF16)
    lo = (x - hi.astype(F32)).astype(BF16)
    return hi, lo


def _pool_fwd(u, wp, sc, t, c):
    n_groups = len(POOL_WINDOWS)
    tc = POOL_CHUNK
    n_chunks = t // tc

    def body(u_ref, wp_ref, sc_ref, y_ref, ypre_ref, xp_ref):
        g = pl.program_id(0)
        xp_ref[pl.ds(0, tc), :] = jnp.zeros((tc, c), F32)
        xp_ref[pl.ds(tc, t), :] = u_ref[...]
        out_i = lax.broadcasted_iota(jnp.int32, (tc, 2 * tc), 0)
        in_j = lax.broadcasted_iota(jnp.int32, (tc, 2 * tc), 1)
        lag = tc + out_i - in_j
        t_in_chunk = lax.broadcasted_iota(jnp.int32, (tc, 1), 0)
        for gi, w in enumerate(POOL_WINDOWS):
            @pl.when(g == gi)
            def _(w=w):
                band = jnp.logical_and(lag >= 0, lag < w).astype(BF16)

                def chunk(ci, carry):
                    start = pl.multiple_of(ci * tc, tc)
                    win = xp_ref[pl.ds(start, 2 * tc), :]
                    hi, mid, lo = _split3(win)
                    wsum = _dot(band, hi) + _dot(band, mid) + _dot(band, lo)
                    cnt = jnp.minimum(ci * tc + t_in_chunk + 1, w).astype(F32)
                    ypre = wsum * (1.0 / cnt) - win[tc:, :]
                    ypre_b = ypre.astype(BF16)
                    y = _dot(ypre_b, wp_ref[...]) * sc_ref[...]
                    ypre_ref[pl.ds(start, tc), :] = ypre_b
                    y_ref[pl.ds(start, tc), :] = y.astype(BF16)
                    return carry

                lax.fori_loop(0, n_chunks, chunk, 0)

    col = pl.BlockSpec((t, c), lambda g: (0, g))
    return pl.pallas_call(
        body, name="pool_fwd", grid=(n_groups,),
        in_specs=[col, pl.BlockSpec((None, c, c), lambda g: (g, 0, 0)), pl.BlockSpec((None, 1, c), lambda g: (g, 0, 0))],
        out_specs=[col, col],
        out_shape=[jax.ShapeDtypeStruct((t, n_groups * c), BF16), jax.ShapeDtypeStruct((t, n_groups * c), BF16)],
        scratch_shapes=[pltpu.VMEM((t + tc, c), F32)],
        compiler_params=_cparams(("parallel",)),
    )(u, wp, sc)


def _pool_bwd(dmixin, ypre, wp, sc, t, c):
    n_groups = len(POOL_WINDOWS)
    tc = POOL_CHUNK
    n_chunks = t // tc

    def body(dy_ref, ypre_ref, wp_ref, sc_ref, du_ref, dwp_ref, dsc_ref, zp_ref):
        g = pl.program_id(0)
        zp_ref[pl.ds(t, tc), :] = jnp.zeros((tc, c), F32)
        dwp_ref[...] = jnp.zeros((c, c), F32)
        dsc_ref[...] = jnp.zeros((1, c), F32)
        out_i = lax.broadcasted_iota(jnp.int32, (tc, 2 * tc), 0)
        in_j = lax.broadcasted_iota(jnp.int32, (tc, 2 * tc), 1)
        lead = in_j - out_i
        t_in_chunk = lax.broadcasted_iota(jnp.int32, (tc, 1), 0)
        for gi, w in enumerate(POOL_WINDOWS):
            @pl.when(g == gi)
            def _(w=w):
                band = jnp.logical_and(lead >= 0, lead < w).astype(BF16)

                def first(ci, carry):
                    start = pl.multiple_of(ci * tc, tc)
                    dy = dy_ref[pl.ds(start, tc), :]
                    yp = ypre_ref[pl.ds(start, tc), :]
                    ymm = _dot(yp, wp_ref[...])
                    dsc_ref[...] += jnp.sum(dy * ymm, axis=0, keepdims=True)
                    dys_b = (dy * sc_ref[...]).astype(BF16)
                    dwp_ref[...] += _dot(yp, dys_b, TN)
                    dyp = _dot(dys_b, wp_ref[...], NT)
                    cnt = jnp.minimum(ci * tc + t_in_chunk + 1, w).astype(F32)
                    zp_ref[pl.ds(start, tc), :] = dyp * (1.0 / cnt)
                    du_ref[pl.ds(start, tc), :] = -dyp
                    return carry

                lax.fori_loop(0, n_chunks, first, 0)

                def second(ci, carry):
                    start = pl.multiple_of(ci * tc, tc)
                    hi, mid, lo = _split3(zp_ref[pl.ds(start, 2 * tc), :])
                    du_ref[pl.ds(start, tc), :] += _dot(band, hi) + _dot(band, mid) + _dot(band, lo)
                    return carry

                lax.fori_loop(0, n_chunks, second, 0)

    col = pl.BlockSpec((t, c), lambda g: (0, g))
    return pl.pallas_call(
        body, name="pool_bwd", grid=(n_groups,),
        in_specs=[col, col, pl.BlockSpec((None, c, c), lambda g: (g, 0, 0)), pl.BlockSpec((None, 1, c), lambda g: (g, 0, 0))],
        out_specs=[col, pl.BlockSpec((None, c, c), lambda g: (g, 0, 0)), pl.BlockSpec((None, 1, c), lambda g: (g, 0, 0))],
        out_shape=[jax.ShapeDtypeStruct((t, n_groups * c), F32), jax.ShapeDtypeStruct((n_groups, c, c), F32),
                   jax.ShapeDtypeStruct((n_groups, 1, c), F32)],
        scratch_shapes=[pltpu.VMEM((t + tc, c), F32)],
        compiler_params=_cparams(("parallel",)),
    )(dmixin, ypre, wp, sc)


ROW_PARTS = 2


def _att_consts():
    b = ATT_BLOCK
    rp = b // ROW_PARTS
    row = lax.broadcasted_iota(jnp.int32, (b, b), 0)
    col = lax.broadcasted_iota(jnp.int32, (b, b), 1)
    tri = (row >= col).astype(BF16)
    prow = lax.broadcasted_iota(jnp.int32, (rp, b), 0)
    pcol = lax.broadcasted_iota(jnp.int32, (rp, b), 1)
    causal = [pcol < prow + r * rp for r in range(ROW_PARTS)]
    return tri, causal


def _suffix_sum(x, tri):
    hi, lo = _split2(x)
    return _dot(hi, tri) + _dot(lo, tri)


LOG2_E = 1.4426950408889634


def _att_scores(qb, kb, mask):
    z2 = _dot(qb, kb, NT) * (LOG2_E / math.sqrt(HEAD_DIM))
    sp2 = jnp.maximum(z2, 0.0) + jnp.log2(1.0 + jnp.exp2(-jnp.abs(z2)))
    return z2, sp2, (sp2 if mask is None else jnp.where(mask, sp2, 0.0))


HEADS_PER_STEP = 2
ATT_LANES = HEADS_PER_STEP * HEAD_DIM


def _head_lanes(s):
    return slice(s * HEAD_DIM, (s + 1) * HEAD_DIM)


def _attn_fwd(qkv, t, n_heads):
    b = ATT_BLOCK
    nq = t // b
    n_steps = n_heads // HEADS_PER_STEP

    rp = b // ROW_PARTS
    chains = [(s, r) for s in range(HEADS_PER_STEP) for r in range(ROW_PARTS)]
    no_mask = [None] * ROW_PARTS

    def body(q_ref, k_ref, v_ref, o_ref):
        tri, causal = _att_consts()

        def blocks(qbs, j, state, masks):
            ks = pl.multiple_of(j * b, b)
            scores = [_att_scores(qbs[ci], k_ref[pl.ds(ks, b), _head_lanes(s)], masks[r]) for ci, (s, r) in enumerate(chains)]
            incls = [_suffix_sum(sc[2], tri) for sc in scores]
            out = []
            for ci, (s, r) in enumerate(chains):
                carry, acc = state[2 * ci], state[2 * ci + 1]
                a = jnp.exp2(scores[ci][0] - (incls[ci] + carry))
                if masks[r] is not None:
                    a = jnp.where(masks[r], a, 0.0)
                out += [carry + incls[ci][:, 0:1], acc + _dot(a.astype(BF16), v_ref[pl.ds(ks, b), _head_lanes(s)])]
            return tuple(out)

        def q_loop(i, _):
            qs = pl.multiple_of(i * b, b)
            qbs = [q_ref[pl.ds(qs + r * rp, rp), _head_lanes(s)] for s, r in chains]
            zero = (jnp.zeros((rp, 1), F32), jnp.zeros((rp, HEAD_DIM), F32)) * len(chains)
            state = blocks(qbs, i, zero, causal)
            state = lax.fori_loop(0, i, lambda jj, c: blocks(qbs, i - 1 - jj, c, no_mask), state)
            for ci, (s, r) in enumerate(chains):
                o_ref[pl.ds(qs + r * rp, rp), _head_lanes(s)] = state[2 * ci + 1]
            return 0

        lax.fori_loop(0, nq, q_loop, 0)

    def heads(off):
        return pl.BlockSpec((t, ATT_LANES), lambda h: (0, off + h))

    return pl.pallas_call(
        body, name="attn_fwd", grid=(n_steps,),
        in_specs=[heads(0), heads(n_steps), heads(2 * n_steps)],
        out_specs=heads(0),
        out_shape=jax.ShapeDtypeStruct((t, n_heads * HEAD_DIM), F32),
        compiler_params=_cparams(("parallel",)),
    )(qkv, qkv, qkv)


def _attn_bwd(qkv, do, o, t, n_heads, dep):
    b = ATT_BLOCK
    nq = t // b
    n_steps = n_heads // HEADS_PER_STEP
    scale = 1.0 / math.sqrt(HEAD_DIM)
    rp = b // ROW_PARTS
    chains = [(s, r) for s in range(HEADS_PER_STEP) for r in range(ROW_PARTS)]
    no_mask = [None] * ROW_PARTS

    def body(q_ref, k_ref, v_ref, do_ref, o_ref, dep_ref, dq_ref, dk_ref, dv_ref, qt_ref, dot_ref, dkt_ref, dvt_ref):
        for j in range(nq):
            rows = pl.ds(j * b, b)
            qt_ref[j] = q_ref[rows, :].astype(F32).T.astype(BF16)
            dot_ref[j] = do_ref[rows, :].astype(F32).T.astype(BF16)
        dkt_ref[...] = jnp.zeros((nq, ATT_LANES, b), F32)
        dvt_ref[...] = jnp.zeros((nq, ATT_LANES, b), F32)
        tri, causal = _att_consts()

        def blocks(i, fixed, j, state, masks):
            ks = pl.multiple_of(j * b, b)
            n = len(chains)
            kbs = [k_ref[pl.ds(ks, b), _head_lanes(s)] for s, _ in chains]
            scores = [_att_scores(fixed[ci][0], kbs[ci], masks[r]) for ci, (s, r) in enumerate(chains)]
            incls = [_suffix_sum(sc[2], tri) for sc in scores]
            das = [_dot(fixed[ci][1], v_ref[pl.ds(ks, b), _head_lanes(s)], NT) for ci, (s, r) in enumerate(chains)]
            a_bs, gs = [], []
            for ci, (s, r) in enumerate(chains):
                a = jnp.exp2(scores[ci][0] - (incls[ci] + state[3 * ci]))
                if masks[r] is not None:
                    a = jnp.where(masks[r], a, 0.0)
                a_bs.append(a.astype(BF16))
                gs.append(a_bs[ci].astype(F32) * das[ci])
            g_incls = [_suffix_sum(g, tri) for g in gs]
            dz_bs = []
            for ci, (s, r) in enumerate(chains):
                rest = (fixed[ci][2] - state[3 * ci + 1]) - (g_incls[ci] - gs[ci])
                sig = jnp.exp2(scores[ci][0] - scores[ci][1])
                dz = (gs[ci] - sig * rest) * scale
                if masks[r] is not None:
                    dz = jnp.where(masks[r], dz, 0.0)
                dz_bs.append(dz.astype(BF16))
            out = []
            for ci in range(n):
                out += [state[3 * ci] + incls[ci][:, 0:1], state[3 * ci + 1] + g_incls[ci][:, 0:1],
                        state[3 * ci + 2] + _dot(dz_bs[ci], kbs[ci])]
            for s in range(HEADS_PER_STEP):
                lanes = _head_lanes(s)
                dk_add, dv_add = None, None
                for ci, (cs, r) in enumerate(chains):
                    if cs == s:
                        part = slice(r * rp, (r + 1) * rp)
                        dk_c = _dot(qt_ref[i, lanes, part], dz_bs[ci])
                        dv_c = _dot(dot_ref[i, lanes, part], a_bs[ci])
                        dk_add = dk_c if dk_add is None else dk_add + dk_c
                        dv_add = dv_c if dv_add is None else dv_add + dv_c
                dkt_ref[j, lanes, :] += dk_add
                dvt_ref[j, lanes, :] += dv_add
            return tuple(out)

        def q_loop(i, _):
            qs = pl.multiple_of(i * b, b)
            fixed = []
            for s, r in chains:
                rows = pl.ds(qs + r * rp, rp)
                dob = do_ref[rows, _head_lanes(s)]
                total = jnp.sum(dob.astype(F32) * o_ref[rows, _head_lanes(s)], axis=-1, keepdims=True)
                fixed.append((q_ref[rows, _head_lanes(s)], dob, total))
            zero = (jnp.zeros((rp, 1), F32), jnp.zeros((rp, 1), F32), jnp.zeros((rp, HEAD_DIM), F32)) * len(chains)
            state = blocks(i, fixed, i, zero, causal)
            state = lax.fori_loop(0, i, lambda jj, c: blocks(i, fixed, i - 1 - jj, c, no_mask), state)
            for ci, (s, r) in enumerate(chains):
                dq_ref[pl.ds(qs + r * rp, rp), _head_lanes(s)] = state[3 * ci + 2].astype(BF16)
            return 0

        lax.fori_loop(0, nq, q_loop, 0)
        for j in range(nq):
            rows = pl.ds(j * b, b)
            dk_ref[rows, :] = dkt_ref[j].T.astype(BF16)
            dv_ref[rows, :] = dvt_ref[j].T.astype(BF16)

    def heads(off):
        return pl.BlockSpec((t, ATT_LANES), lambda h: (0, off + h))

    shape = jax.ShapeDtypeStruct((t, n_heads * HEAD_DIM), BF16)
    return pl.pallas_call(
        body, name="attn_bwd", grid=(n_steps,),
        in_specs=[heads(0), heads(n_steps), heads(2 * n_steps), heads(0), heads(0), pl.BlockSpec((8, 128), lambda h: (0, 0))],
        out_specs=[heads(0)] * 3, out_shape=[shape] * 3,
        scratch_shapes=[pltpu.VMEM((nq, ATT_LANES, b), BF16)] * 2 + [pltpu.VMEM((nq, ATT_LANES, b), F32)] * 2,
        compiler_params=_cparams(("parallel",)),
    )(qkv, qkv, qkv, do, o, dep)


def _place():
    x, y, c = lax.axis_index("x"), lax.axis_index("y"), lax.axis_index("c")
    return x, y, c


def _all_gather_weights(shards):
    n = len(shards)

    def body(*refs):
        ins, outs = refs[:n], refs[n:2 * n]
        send_sems, recv_sems, local_sems = refs[2 * n:]
        x, y, c = _place()
        me, sibling = (x, y, c), (x, y, 1 - c)
        chips = [(1 - x, y), (x, 1 - y), (1 - x, 1 - y)]

        def slot(px, py, pc):
            return 4 * px + 2 * py + pc

        def copy(ti, k, block, to, src=None):
            dst = outs[ti].at[slot(*block)]
            return pltpu.make_async_remote_copy(
                src_ref=dst if src is None else src, dst_ref=dst,
                send_sem=send_sems.at[ti, k], recv_sem=recv_sems.at[ti, k], device_id=to, device_id_type=MESH)

        mine = [pltpu.make_async_copy(ins[ti], outs[ti].at[slot(*me)], local_sems.at[ti]) for ti in range(n)]
        for cp in mine:
            cp.start()
        first = []
        for ti in range(n):
            first.append(copy(ti, 0, me, sibling, src=ins[ti]))
            first += [copy(ti, 1 + j, me, (*chip, c), src=ins[ti]) for j, chip in enumerate(chips)]
        for cp in first:
            cp.start()
        passed = []
        for j, chip in enumerate(chips):
            for ti in range(n):
                copy(ti, 1 + j, (*chip, c), me).wait_recv()
                fwd = copy(ti, 4 + j, (*chip, c), sibling)
                fwd.start()
                passed.append(fwd)
        for ti in range(n):
            copy(ti, 0, sibling, me).wait_recv()
            for j, chip in enumerate(chips):
                copy(ti, 4 + j, (*chip, 1 - c), me).wait_recv()
        for cp in first + passed:
            cp.wait_send()
        for cp in mine:
            cp.wait()

    any_spec = pl.BlockSpec(memory_space=pl.ANY)
    return pl.pallas_call(
        body, name="all_gather_weights",
        in_specs=[any_spec] * n, out_specs=[any_spec] * n,
        out_shape=[jax.ShapeDtypeStruct((N_DEV, *s.shape), s.dtype) for s in shards],
        scratch_shapes=[pltpu.SemaphoreType.DMA((n, 7)), pltpu.SemaphoreType.DMA((n, 7)), pltpu.SemaphoreType.DMA((n,))],
        compiler_params=pltpu.CompilerParams(has_side_effects=True),
    )(*shards)


HBM_SPEC = pl.BlockSpec(memory_space=pltpu.HBM)
SEM_SPEC = pl.BlockSpec(memory_space=pltpu.SEMAPHORE)
DATAFLOW = pltpu.SideEffectType.DATAFLOW_SIDE_EFFECTING


def _flip(v, on):
    return 1 - v if on else v


def _plan_copies(plan, refs, send_sems, recv_sems):
    return [pltpu.make_async_remote_copy(src_ref=src, dst_ref=dst, send_sem=send_sems.at[k], recv_sem=recv_sems.at[k],
                                         device_id=dev, device_id_type=MESH)
            for k, (src, dst, dev) in enumerate(plan(refs))]


def _copies_start(name, arrays, plan, n_copies, after=()):
    n = len(arrays)
    n_in = n + len(after)

    def body(*refs):
        send_sems, recv_sems, token = refs[n_in], refs[n_in + 1], refs[n_in + n + 2]
        for cp in _plan_copies(plan, refs[:n], send_sems, recv_sems):
            cp.start()
        token[...] = jnp.zeros_like(token)

    outs = pl.pallas_call(
        body, name=name,
        out_shape=(pltpu.SemaphoreType.DMA((n_copies,)), pltpu.SemaphoreType.DMA((n_copies,)),
                   *[pltpu.HBM(a.shape, a.dtype) for a in arrays], jax.ShapeDtypeStruct((8, 128), F32)),
        in_specs=[HBM_SPEC] * n + [pl.BlockSpec(memory_space=pl.ANY)] * len(after),
        out_specs=(SEM_SPEC, SEM_SPEC, *[HBM_SPEC] * n, pl.BlockSpec(memory_space=pltpu.VMEM)),
        input_output_aliases={i: 2 + i for i in range(n)},
        compiler_params=pltpu.CompilerParams(has_side_effects=DATAFLOW),
    )(*[pltpu.with_memory_space_constraint(a, pltpu.HBM) for a in arrays], *after)
    return outs[0], outs[1], list(outs[2:2 + n]), outs[2 + n]


def _copies_wait(name, send_sems, recv_sems, arrays, plan, after):
    n = len(arrays)

    def body(*refs):
        for cp in _plan_copies(plan, refs[:n], refs[n], refs[n + 1]):
            cp.wait_send()
            cp.wait_recv()

    outs = pl.pallas_call(
        body, name=name,
        out_shape=tuple(pltpu.HBM(a.shape, a.dtype) for a in arrays),
        in_specs=[HBM_SPEC] * n + [SEM_SPEC, SEM_SPEC] + [pl.BlockSpec(memory_space=pl.ANY)] * len(after),
        out_specs=tuple([HBM_SPEC] * n),
        input_output_aliases={i: i for i in range(n)},
        compiler_params=pltpu.CompilerParams(has_side_effects=DATAFLOW),
    )(*arrays, send_sems, recv_sems, *after)
    return list(outs)


def _plan_gather_first(n_direct, n_two_level):
    def plan(refs):
        x, y, c = _place()
        me = 4 * x + 2 * y + c
        out = []
        for ti in range(n_direct + n_two_level):
            mine = refs[ti].at[me]
            for k in range(1, N_DEV):
                if ti < n_direct or k == 1 or not k & 1:
                    out.append((mine, mine, (_flip(x, k & 4), _flip(y, k & 2), _flip(c, k & 1))))
        return out
    return plan, 7 * n_direct + 4 * n_two_level


def _plan_gather_forward(n):
    def plan(refs):
        x, y, c = _place()
        out = []
        for ti in range(n):
            for r in range(1, 4):
                blk = refs[ti].at[4 * _flip(x, r & 2) + 2 * _flip(y, r & 1) + c]
                out.append((blk, blk, (x, y, 1 - c)))
        return out
    return plan, 3 * n


def _plan_rs_sibling(n):
    def plan(refs):
        x, y, c = _place()
        out = []
        for ti in range(n):
            for r in range(4):
                src = refs[ti].at[4 * _flip(x, r & 2) + 2 * _flip(y, r & 1) + (1 - c)]
                out.append((src, refs[n + ti].at[r], (x, y, 1 - c)))
        return out
    return plan, 4 * n


def _plan_rs_owner(n):
    def plan(refs):
        x, y, c = _place()
        out = []
        for ti in range(n):
            for r in range(1, 4):
                out.append((refs[ti].at[r], refs[n + ti].at[r], (_flip(x, r & 2), _flip(y, r & 1), c)))
        return out
    return plan, 3 * n


def _rs_to_sibling(partials):
    n = len(partials)

    def body(*refs):
        ins, outs = refs[:n], refs[n:2 * n]
        send_sems, recv_sems = refs[2 * n:]
        x, y, c = _place()
        sibling = (x, y, 1 - c)
        copies = []
        for ti in range(n):
            for r in range(4):
                ox, oy = (1 - x if r & 2 else x), (1 - y if r & 1 else y)
                cp = pltpu.make_async_remote_copy(
                    src_ref=ins[ti].at[4 * ox + 2 * oy + (1 - c)], dst_ref=outs[ti].at[r],
                    send_sem=send_sems.at[ti, r], recv_sem=recv_sems.at[ti, r], device_id=sibling, device_id_type=MESH)
                cp.start()
                copies.append(cp)
        for cp in copies:
            cp.wait_recv()
        for cp in copies:
            cp.wait_send()

    any_spec = pl.BlockSpec(memory_space=pl.ANY)
    return pl.pallas_call(
        body, name="rs_to_sibling",
        in_specs=[any_spec] * n, out_specs=[any_spec] * n,
        out_shape=[jax.ShapeDtypeStruct((4, *s.shape[1:]), s.dtype) for s in partials],
        scratch_shapes=[pltpu.SemaphoreType.DMA((n, 4)), pltpu.SemaphoreType.DMA((n, 4))],
        compiler_params=pltpu.CompilerParams(has_side_effects=True),
    )(*partials)


def _rs_to_owner(sums):
    n = len(sums)

    def body(*refs):
        ins, outs = refs[:n], refs[n:2 * n]
        send_sems, recv_sems = refs[2 * n:]
        x, y, c = _place()
        copies = []
        for ti in range(n):
            for r in range(1, 4):
                ox, oy = (1 - x if r & 2 else x), (1 - y if r & 1 else y)
                cp = pltpu.make_async_remote_copy(
                    src_ref=ins[ti].at[r], dst_ref=outs[ti].at[r],
                    send_sem=send_sems.at[ti, r], recv_sem=recv_sems.at[ti, r], device_id=(ox, oy, c), device_id_type=MESH)
                cp.start()
                copies.append(cp)
        for cp in copies:
            cp.wait_recv()
        for cp in copies:
            cp.wait_send()

    any_spec = pl.BlockSpec(memory_space=pl.ANY)
    return pl.pallas_call(
        body, name="rs_to_owner",
        in_specs=[any_spec] * n, out_specs=[any_spec] * n,
        out_shape=[jax.ShapeDtypeStruct(s.shape, s.dtype) for s in sums],
        scratch_shapes=[pltpu.SemaphoreType.DMA((n, 4)), pltpu.SemaphoreType.DMA((n, 4))],
        compiler_params=pltpu.CompilerParams(has_side_effects=True),
    )(*sums)


def _owner_slots():
    x, y, c = _place()
    idx = []
    for r in range(4):
        ox, oy = (1 - x if r & 2 else x), (1 - y if r & 1 else y)
        idx.append(4 * ox + 2 * oy + c)
    return jnp.stack(idx).astype(jnp.int32)


def _row_tile(rows, cols):
    tr = max(8, min(rows, (1 << 19) // cols))
    while rows % tr:
        tr //= 2
    return tr


def _rs_chip_sum(name, slots, partial, from_sibling):
    _, rows, cols = partial.shape
    tr = _row_tile(rows, cols)

    def body(slots_ref, p_ref, s_ref, o_ref):
        o_ref[...] = (p_ref[...] + s_ref[...]).astype(BF16)

    grid_spec = pltpu.PrefetchScalarGridSpec(
        num_scalar_prefetch=1, grid=(3, rows // tr),
        in_specs=[pl.BlockSpec((None, tr, cols), lambda r, i, s: (s[r + 1], i, 0)),
                  pl.BlockSpec((None, tr, cols), lambda r, i, s: (r + 1, i, 0))],
        out_specs=pl.BlockSpec((None, tr, cols), lambda r, i, s: (r + 1, i, 0)))
    return pl.pallas_call(
        body, name=name, grid_spec=grid_spec, out_shape=jax.ShapeDtypeStruct((4, rows, cols), BF16),
        compiler_params=_cparams(("parallel", "parallel")),
    )(slots, partial, from_sibling)


def _adamw(w, g, m, v):
    m = ADAM_B1 * m + (1.0 - ADAM_B1) * g
    v = ADAM_B2 * v + (1.0 - ADAM_B2) * (g * g)
    m_hat = m / (1.0 - ADAM_B1 ** ADAM_STEP)
    v_hat = v / (1.0 - ADAM_B2 ** ADAM_STEP)
    delta = -ADAM_LR * (m_hat / (jnp.sqrt(v_hat) + ADAM_EPS) + ADAM_WD * w)
    return delta, m, v


def _rs_final_adamw(name, slots, partial, from_sibling, from_chips, w, m, v):
    rows, cols = w.shape
    tr = _row_tile(rows, cols)

    def body(slots_ref, p_ref, s_ref, c1_ref, c2_ref, c3_ref, w_ref, m_ref, v_ref, g_ref, d_ref, nm_ref, nv_ref):
        g = p_ref[...] + s_ref[...]
        g = g + c1_ref[...].astype(F32)
        g = g + c2_ref[...].astype(F32)
        g = g + c3_ref[...].astype(F32)
        delta, nm, nv = _adamw(w_ref[...], g, m_ref[...], v_ref[...])
        g_ref[...] = g
        d_ref[...] = delta
        nm_ref[...] = nm
        nv_ref[...] = nv

    def slot(r):
        return pl.BlockSpec((None, tr, cols), lambda i, s: (r, i, 0))

    flat = pl.BlockSpec((tr, cols), lambda i, s: (i, 0))
    grid_spec = pltpu.PrefetchScalarGridSpec(
        num_scalar_prefetch=1, grid=(rows // tr,),
        in_specs=[pl.BlockSpec((None, tr, cols), lambda i, s: (s[0], i, 0)), slot(0), slot(1), slot(2), slot(3), flat, flat, flat],
        out_specs=[flat] * 4)
    return pl.pallas_call(
        body, name=name, grid_spec=grid_spec, out_shape=[jax.ShapeDtypeStruct((rows, cols), F32)] * 4,
        compiler_params=_cparams(("parallel",)),
    )(slots, partial, from_sibling, from_chips, from_chips, from_chips, w, m, v)


def _small_all_reduce(packet):
    rows, d = packet.shape

    def body(p_ref, sum_ref, loss_ref, all_ref, send_sems, recv_sems):
        x, y, c = _place()
        me = 4 * x + 2 * y + c
        all_ref[me] = p_ref[...]
        copies = []
        for k in range(1, N_DEV):
            px, py, pc = (1 - x if k & 4 else x), (1 - y if k & 2 else y), (1 - c if k & 1 else c)
            cp = pltpu.make_async_remote_copy(
                src_ref=p_ref, dst_ref=all_ref.at[me], send_sem=send_sems.at[k], recv_sem=recv_sems.at[k],
                device_id=(px, py, pc), device_id_type=MESH)
            cp.start()
            copies.append(cp)
        for cp in copies:
            cp.wait_recv()
        for cp in copies:
            cp.wait_send()
        total = all_ref[0]
        for j in range(1, N_DEV):
            total = total + all_ref[j]
        sum_ref[...] = total
        loss_ref[...] = jnp.sum(total[0:1, :], axis=-1, keepdims=True)

    vmem = pl.BlockSpec(memory_space=pltpu.VMEM)
    return pl.pallas_call(
        body, name="small_all_reduce",
        in_specs=[vmem], out_specs=[vmem, vmem],
        out_shape=[jax.ShapeDtypeStruct((rows, d), F32), jax.ShapeDtypeStruct((1, 1), F32)],
        scratch_shapes=[pltpu.VMEM((N_DEV, rows, d), F32), pltpu.SemaphoreType.DMA((N_DEV,)), pltpu.SemaphoreType.DMA((N_DEV,))],
        compiler_params=pltpu.CompilerParams(has_side_effects=True),
    )(packet)


def _small_adamw(w, g, m, v):
    def body(w_ref, g_ref, m_ref, v_ref, d_ref, nm_ref, nv_ref):
        delta, nm, nv = _adamw(w_ref[...], g_ref[...], m_ref[...], v_ref[...])
        d_ref[...] = delta
        nm_ref[...] = nm
        nv_ref[...] = nv

    vmem = pl.BlockSpec(memory_space=pltpu.VMEM)
    return pl.pallas_call(
        body, name="small_adamw", in_specs=[vmem] * 4, out_specs=[vmem] * 3,
        out_shape=[jax.ShapeDtypeStruct(w.shape, F32)] * 3,
    )(w, g, m, v)


def kernel(x, ln_in_g, ln_in_b, w_in, w_pool, pool_scale, w_out, ln1_g, ln1_b, w_ff1, b_ff1, w_ff2, b_ff2, ln2_g, ln2_b, loss_target, m_ln_in_g, m_ln_in_b, m_w_in, m_w_pool, m_pool_scale, m_w_out, m_ln1_g, m_ln1_b, m_w_ff1, m_b_ff1, m_w_ff2, m_b_ff2, m_ln2_g, m_ln2_b, v_ln_in_g, v_ln_in_b, v_w_in, v_w_pool, v_pool_scale, v_w_out, v_ln1_g, v_ln1_b, v_w_ff1, v_b_ff1, v_w_ff2, v_b_ff2, v_ln2_g, v_ln2_b):
    t, d = x.shape[1], x.shape[2]
    n_groups = len(POOL_WINDOWS)
    c_pool = w_pool.shape[3]
    p = n_groups * c_pool
    n_heads = (d - p) // HEAD_DIM
    ws_in = w_in.shape[2]
    n_in = N_DEV * ws_in
    ws_out = w_out.shape[1]
    ws_f = w_ff1.shape[2]
    f = N_DEV * ws_f
    pr = w_pool.shape[2]
    assert n_in == p + 3 * n_heads * HEAD_DIM and N_DEV * ws_out == d and N_DEV * pr == c_pool

    tm_big = min(t, 1024)
    tm_ep = min(t, 512)
    tkk = min(t, 2048)
    half_f = min(ws_f, 512)
    per_f = ws_f // half_f

    x2 = x.reshape(t, d)
    target = loss_target.reshape(t, d)
    g0, b0 = ln_in_g.reshape(1, d), ln_in_b.reshape(1, d)

    shards = [w_in.reshape(d, ws_in), w_out.reshape(ws_out, d), w_ff1.reshape(d, ws_f), w_ff2.reshape(ws_f, d),
              w_pool.reshape(n_groups * pr, c_pool)]
    x_, y_, c_ = _place()
    me = 4 * x_ + 2 * y_ + c_
    win_g, wpool_g, scale_g = _all_gather_weights(
        [shards[0].astype(BF16), shards[4].astype(BF16), pool_scale.reshape(n_groups, pr)])
    lands = [lax.dynamic_update_index_in_dim(lax.empty((N_DEV, *s.shape), BF16), s.astype(BF16), me, 0) for s in shards[1:4]]
    plan_g1, n_g1 = _plan_gather_first(1, 2)
    g1_send, g1_recv, lands, token_g1 = _copies_start("gather_start", lands, plan_g1, n_g1, after=[win_g])
    wp_full = wpool_g.reshape(N_DEV, n_groups, pr, c_pool).transpose(1, 0, 2, 3).reshape(n_groups, c_pool, c_pool)
    sc_full = scale_g.transpose(1, 0, 2).reshape(n_groups, 1, c_pool)

    def sds(shape, dtype=F32):
        return jax.ShapeDtypeStruct(shape, dtype)

    vec = pl.BlockSpec((1, d), lambda m, n, k: (0, 0))
    row_ep = pl.BlockSpec((tm_ep, d), lambda m, n, k: (m, 0))
    tm_res = min(t, 256)
    row_res = pl.BlockSpec((tm_res, d), lambda m, n, k: (m, 0))
    seq = ("arbitrary", "arbitrary", "arbitrary")

    h0, h0b = _ln_in_fwd(x2, g0, b0, tm_big)

    pool_shards = p // ws_in

    def mm_u(name, first, count, dtype):
        return _matmul(
            name, h0b, win_g, dims=NN, grid=(t // tm_big, count, 1),
            a_spec=pl.BlockSpec((tm_big, d), lambda m, n, k: (m, 0)),
            b_spec=pl.BlockSpec((None, d, ws_in), lambda m, n, k: (n + first, 0, 0)),
            out_shape=[sds((t, count * ws_in), dtype)],
            out_specs=[pl.BlockSpec((tm_big, ws_in), lambda m, n, k: (m, n))],
            acc_shape=(tm_big, ws_in), epilogue=_store_epilogue(dtype), deps=(token_g1,))[0]

    u_pool = mm_u("mm_u_pool", 0, pool_shards, F32)
    qkv = mm_u("mm_u_qkv", pool_shards, N_DEV - pool_shards, BF16)

    y_pool, ypre = _pool_fwd(u_pool, wp_full, sc_full, t, c_pool)
    o = _attn_fwd(qkv, t, n_heads)
    mixin = jnp.concatenate([y_pool, o.astype(BF16)], axis=1)

    wout_g, w1_part, w2_part = _copies_wait("gather_wait", g1_send, g1_recv, lands, plan_g1, [mixin])
    plan_g2, n_g2 = _plan_gather_forward(2)
    g2_send, g2_recv, fwd_lands, token_g2 = _copies_start("gather_forward_start", [w1_part, w2_part], plan_g2, n_g2)
    wout_2d = wout_g.reshape(d, d)

    def ep_ln1(acc, ex, outs):
        h0_ref, g_ref, b_ref = ex
        r1 = DEEPNORM_ALPHA * h0_ref[...] + acc
        xhat, _ = _ln_stats(r1)
        h1 = xhat * g_ref[...] + b_ref[...]
        outs[0][...] = r1
        outs[1][...] = h1
        outs[2][...] = h1.astype(BF16)

    r1, h1, h1b = _matmul(
        "mm_mix_ln1", mixin, wout_2d, dims=NN, grid=(t // tm_res, 1, 1),
        a_spec=pl.BlockSpec((tm_res, d), lambda m, n, k: (m, 0)),
        b_spec=pl.BlockSpec((d, d), lambda m, n, k: (0, 0)),
        extras=(h0, ln1_g, ln1_b), extra_specs=(row_res, vec, vec),
        out_shape=[sds((t, d)), sds((t, d)), sds((t, d), BF16)], out_specs=[row_res] * 3,
        acc_shape=(tm_res, d), epilogue=ep_ln1, deps=(token_g2,))
    w1_g, w2_g = _copies_wait("gather_forward_wait", g2_send, g2_recv, fwd_lands, plan_g2, [h1b])

    def ep_ff1(acc, ex, outs):
        f1 = acc + ex[0][...]
        outs[0][...] = f1
        r = jnp.maximum(f1, 0.0)
        outs[1][...] = (r * r).astype(BF16)

    ff_tile = pl.BlockSpec((tm_big, half_f), lambda m, n, k: (m, n))
    f1, act = _matmul(
        "mm_ff1", h1b, w1_g, dims=NN, grid=(t // tm_big, f // half_f, 1),
        a_spec=pl.BlockSpec((tm_big, d), lambda m, n, k: (m, 0)),
        b_spec=pl.BlockSpec((None, d, half_f), lambda m, n, k: (n // per_f, 0, n % per_f)),
        extras=(b_ff1,), extra_specs=(pl.BlockSpec((1, half_f), lambda m, n, k: (0, n)),),
        out_shape=[sds((t, f)), sds((t, f), BF16)], out_specs=[ff_tile, ff_tile],
        acc_shape=(tm_big, half_f), epilogue=ep_ff1)

    def ep_ln2(acc, ex, outs):
        h1_ref, tgt_ref, bf2_ref, g_ref, b_ref = ex
        dr2_ref, dr2b_ref, dg_ref, db_ref, dbf2_ref, loss_ref = outs
        r2 = DEEPNORM_ALPHA * h1_ref[...] + (acc + bf2_ref[...])
        xhat, rstd = _ln_stats(r2)
        err = xhat * g_ref[...] + b_ref[...] - tgt_ref[...]
        dr2, dg, db = _ln_bwd(err * (1.0 / d), xhat, rstd, g_ref[...])
        dr2_ref[...] = dr2
        dr2b_ref[...] = dr2.astype(BF16)
        first = pl.program_id(0) == 0
        _acc_rows(first, dg_ref, dg)
        _acc_rows(first, db_ref, db)
        _acc_rows(first, dbf2_ref, jnp.sum(dr2, axis=0, keepdims=True))
        _acc_rows(first, loss_ref, jnp.sum(err * err, axis=0, keepdims=True) * (0.5 / d))

    dr2, dr2b, dg2, db2, dbf2, loss_vec = _matmul(
        "mm_ff2_ln2_loss", act, w2_g, dims=NN, grid=(t // tm_ep, 1, N_DEV),
        a_spec=pl.BlockSpec((tm_ep, ws_f), lambda m, n, k: (m, k)),
        b_spec=pl.BlockSpec((None, ws_f, d), lambda m, n, k: (k, 0, 0)),
        extras=(h1, target, b_ff2, ln2_g, ln2_b), extra_specs=(row_ep, row_ep, vec, vec, vec),
        out_shape=[sds((t, d)), sds((t, d), BF16)] + [sds((1, d))] * 4, out_specs=[row_ep, row_ep, vec, vec, vec, vec],
        acc_shape=(tm_ep, d), epilogue=ep_ln2, sem=seq)

    def ep_dff1(acc, ex, outs):
        df1 = acc * (2.0 * jnp.maximum(ex[0][...], 0.0))
        outs[0][...] = df1.astype(BF16)
        _acc_rows(pl.program_id(1) == 0, outs[1], jnp.sum(df1, axis=0, keepdims=True))

    df_tile = pl.BlockSpec((tm_big, ws_f), lambda n, m, k: (m, n))
    df1b, dbf1 = _matmul(
        "mm_dff1", dr2b, w2_g, dims=NT, grid=(N_DEV, t // tm_big, 1),
        a_spec=pl.BlockSpec((tm_big, d), lambda n, m, k: (m, 0)),
        b_spec=pl.BlockSpec((None, ws_f, d), lambda n, m, k: (n, 0, 0)),
        extras=(f1,), extra_specs=(df_tile,),
        out_shape=[sds((t, f), BF16), sds((1, f))], out_specs=[df_tile, pl.BlockSpec((1, ws_f), lambda n, m, k: (0, n))],
        acc_shape=(tm_big, ws_f), epilogue=ep_dff1, sem=("parallel", "arbitrary", "arbitrary"))

    tn_d = min(d, 1024)
    dw2 = _matmul(
        "mm_dw2", act, dr2b, dims=TN, grid=(N_DEV, d // tn_d, t // tkk),
        a_spec=pl.BlockSpec((tkk, ws_f), lambda m, n, k: (k, m)),
        b_spec=pl.BlockSpec((tkk, tn_d), lambda m, n, k: (k, n)),
        out_shape=[sds((N_DEV, ws_f, d))], out_specs=[pl.BlockSpec((None, ws_f, tn_d), lambda m, n, k: (m, 0, n))],
        acc_shape=(ws_f, tn_d), epilogue=_store_epilogue(F32))[0]

    dw1 = _matmul(
        "mm_dw1", h1b, df1b, dims=TN, grid=(d // tn_d, N_DEV, t // tkk),
        a_spec=pl.BlockSpec((tkk, tn_d), lambda m, n, k: (k, m)),
        b_spec=pl.BlockSpec((tkk, ws_f), lambda m, n, k: (k, n)),
        out_shape=[sds((N_DEV, d, ws_f))], out_specs=[pl.BlockSpec((None, tn_d, ws_f), lambda m, n, k: (n, m, 0))],
        acc_shape=(tn_d, ws_f), epilogue=_store_epilogue(F32))[0]

    def ep_ln1_bwd(acc, ex, outs):
        dr2_ref, r1_ref, g_ref = ex
        xhat, rstd = _ln_stats(r1_ref[...])
        dr1, dg, db = _ln_bwd(DEEPNORM_ALPHA * dr2_ref[...] + acc, xhat, rstd, g_ref[...])
        outs[0][...] = dr1
        outs[1][...] = dr1.astype(BF16)
        first = pl.program_id(0) == 0
        _acc_rows(first, outs[2], dg)
        _acc_rows(first, outs[3], db)

    dr1, dr1b, dg1, db1 = _matmul(
        "mm_dh1_ln1_bwd", df1b, w1_g, dims=NT, grid=(t // tm_ep, 1, N_DEV),
        a_spec=pl.BlockSpec((tm_ep, ws_f), lambda m, n, k: (m, k)),
        b_spec=pl.BlockSpec((None, d, ws_f), lambda m, n, k: (k, 0, 0)),
        extras=(dr2, r1, ln1_g), extra_specs=(row_ep, row_ep, vec),
        out_shape=[sds((t, d)), sds((t, d), BF16), sds((1, d)), sds((1, d))], out_specs=[row_ep, row_ep, vec, vec],
        acc_shape=(tm_ep, d), epilogue=ep_ln1_bwd, sem=seq)

    dwout = _matmul(
        "mm_dwout", mixin, dr1b, dims=TN, grid=(d // tn_d, d // tn_d, t // tkk),
        a_spec=pl.BlockSpec((tkk, tn_d), lambda m, n, k: (k, m)),
        b_spec=pl.BlockSpec((tkk, tn_d), lambda m, n, k: (k, n)),
        out_shape=[sds((d, d))], out_specs=[pl.BlockSpec((tn_d, tn_d), lambda m, n, k: (m, n))],
        acc_shape=(tn_d, tn_d), epilogue=_store_epilogue(F32))[0]

    slots = _owner_slots()

    def rs_sibling_start(tag, parts):
        lands_ = [lax.empty((4, *pt.shape[1:]), F32) for pt in parts]
        plan, n_cp = _plan_rs_sibling(len(parts))
        send, recv, arrs, token = _copies_start("rs_sibling_start_" + tag, parts + lands_, plan, n_cp)
        return (send, recv, arrs, plan), token

    def rs_sibling_wait(tag, started, after):
        send, recv, arrs, plan = started
        arrs = _copies_wait("rs_sibling_wait_" + tag, send, recv, arrs, plan, after)
        return arrs[:len(arrs) // 2], arrs[len(arrs) // 2:]

    def rs_owner_start(tag, names_, parts, from_sib):
        sums = [_rs_chip_sum("rs_chip_sum_" + nm, slots, pt, fs) for nm, pt, fs in zip(names_, parts, from_sib)]
        lands_ = [lax.empty(cs.shape, BF16) for cs in sums]
        plan, n_cp = _plan_rs_owner(len(sums))
        send, recv, arrs, token = _copies_start("rs_owner_start_" + tag, sums + lands_, plan, n_cp)
        return (send, recv, arrs, plan), token

    def rs_owner_wait(tag, started, after):
        send, recv, arrs, plan = started
        return _copies_wait("rs_owner_wait_" + tag, send, recv, arrs, plan, after)[len(arrs) // 2:]

    names1 = ["w_ff1", "w_ff2", "w_out"]
    sib1, token_a1 = rs_sibling_start("1", [dw1, dw2, dwout.reshape(N_DEV, ws_out, d)])

    tn_mix = min(tn_d, p, d - p)

    def mm_dmixin(name, first, width, dtype):
        return _matmul(
            name, dr1b, wout_2d, dims=NT, grid=(t // tm_big, width // tn_mix, 1),
            a_spec=pl.BlockSpec((tm_big, d), lambda m, n, k: (m, 0)),
            b_spec=pl.BlockSpec((tn_mix, d), lambda m, n, k: (n + first // tn_mix, 0)),
            out_shape=[sds((t, width), dtype)], out_specs=[pl.BlockSpec((tm_big, tn_mix), lambda m, n, k: (m, n))],
            acc_shape=(tm_big, tn_mix), epilogue=_store_epilogue(dtype), deps=(token_a1,))[0]

    dy_pool = mm_dmixin("mm_dmixin_pool", 0, p, F32)
    do = mm_dmixin("mm_dmixin_att", p, d - p, BF16)

    du_pool, dwp, dsc = _pool_bwd(dy_pool, ypre, wp_full, sc_full, t, c_pool)
    parts1, from_sib1 = rs_sibling_wait("1", sib1, [du_pool, do])
    own1, token_c1 = rs_owner_start("1", names1, parts1, from_sib1)

    dq, dk, dv = _attn_bwd(qkv, do, o, t, n_heads, token_c1)
    dub = jnp.concatenate([du_pool.astype(BF16), dq, dk, dv], axis=1)

    dwin = _matmul(
        "mm_dwin", h0b, dub, dims=TN, grid=(d // tn_d, N_DEV, t // tkk),
        a_spec=pl.BlockSpec((tkk, tn_d), lambda m, n, k: (k, m)),
        b_spec=pl.BlockSpec((tkk, ws_in), lambda m, n, k: (k, n)),
        out_shape=[sds((N_DEV, d, ws_in))], out_specs=[pl.BlockSpec((None, tn_d, ws_in), lambda m, n, k: (n, m, 0))],
        acc_shape=(tn_d, ws_in), epilogue=_store_epilogue(F32))[0]

    names2 = ["w_in", "w_pool"]
    dwp_g = dwp.reshape(n_groups, N_DEV, pr, c_pool).transpose(1, 0, 2, 3).reshape(N_DEV, n_groups * pr, c_pool)
    sib2, token_a2 = rs_sibling_start("2", [dwin, dwp_g])

    def ep_ln0_bwd(acc, ex, outs):
        dr1_ref, x_ref, g_ref = ex
        xhat, rstd = _ln_stats(x_ref[...])
        dx, dg, db = _ln_bwd(DEEPNORM_ALPHA * dr1_ref[...] + acc, xhat, rstd, g_ref[...])
        outs[0][...] = dx
        first = pl.program_id(0) == 0
        _acc_rows(first, outs[1], dg)
        _acc_rows(first, outs[2], db)

    dx, dg0, db0 = _matmul(
        "mm_dh0_ln0_bwd", dub, win_g, dims=NT, grid=(t // tm_ep, 1, N_DEV),
        a_spec=pl.BlockSpec((tm_ep, ws_in), lambda m, n, k: (m, k)),
        b_spec=pl.BlockSpec((None, d, ws_in), lambda m, n, k: (k, 0, 0)),
        extras=(dr1, x2, g0), extra_specs=(row_ep, row_ep, vec),
        out_shape=[sds((t, d)), sds((1, d)), sds((1, d))], out_specs=[row_ep, vec, vec],
        acc_shape=(tm_ep, d), epilogue=ep_ln0_bwd, sem=seq, deps=(token_a2,))

    parts2, from_sib2 = rs_sibling_wait("2", sib2, [dx])
    own2, token_c2 = rs_owner_start("2", names2, parts2, from_sib2)
    from_chips1 = rs_owner_wait("1", own1, [token_c2])
    w_of = {"w_in": shards[0], "w_out": shards[1], "w_ff1": shards[2], "w_ff2": shards[3], "w_pool": shards[4]}
    mv_of = {"w_in": (m_w_in, v_w_in), "w_out": (m_w_out, v_w_out), "w_ff1": (m_w_ff1, v_w_ff1),
             "w_ff2": (m_w_ff2, v_w_ff2), "w_pool": (m_w_pool, v_w_pool)}
    big = {}

    def finals(names_, parts, from_sib, from_chips):
        for nm, pt, fs, fc in zip(names_, parts, from_sib, from_chips):
            w2d = w_of[nm]
            m_, v_ = mv_of[nm]
            big[nm] = _rs_final_adamw("rs_final_adamw_" + nm, slots, pt, fs, fc, w2d, m_.reshape(w2d.shape), v_.reshape(w2d.shape))

    finals(names1, parts1, from_sib1, from_chips1)

    n_f_rows = f // d
    pad_sc = d - p
    packet = jnp.concatenate(
        [loss_vec, dg0, db0, dg1, db1, dbf2, dg2, db2, dbf1.reshape(n_f_rows, d),
         jnp.pad(dsc.reshape(1, p), ((0, 0), (0, pad_sc)))], axis=0)
    n_rows = packet.shape[0]
    n_pad = (-n_rows) % 8
    packet = jnp.pad(packet, ((0, n_pad), (0, 0)))
    sums, loss11 = _small_all_reduce(packet)
    dsc_full = sums[8 + n_f_rows, :p].reshape(n_groups, N_DEV, pr)
    dsc_mine = lax.dynamic_index_in_dim(dsc_full, me, axis=1, keepdims=False)

    def sc_row(a):
        return jnp.pad(a.reshape(1, n_groups * pr), ((0, 0), (0, d - n_groups * pr)))

    def small_pack(ln0g, ln0b, l1g, l1b, bf2, l2g, l2b, bf1, sc):
        rows = [jnp.zeros((1, d), F32), ln0g.reshape(1, d), ln0b.reshape(1, d), l1g, l1b, bf2, l2g, l2b,
                bf1.reshape(n_f_rows, d), sc_row(sc), jnp.zeros((n_pad, d), F32)]
        return jnp.concatenate(rows, axis=0)

    w_small = small_pack(ln_in_g, ln_in_b, ln1_g, ln1_b, b_ff2, ln2_g, ln2_b, b_ff1, pool_scale)
    m_small = small_pack(m_ln_in_g, m_ln_in_b, m_ln1_g, m_ln1_b, m_b_ff2, m_ln2_g, m_ln2_b, m_b_ff1, m_pool_scale)
    v_small = small_pack(v_ln_in_g, v_ln_in_b, v_ln1_g, v_ln1_b, v_b_ff2, v_ln2_g, v_ln2_b, v_b_ff1, v_pool_scale)
    g_small = jnp.concatenate([sums[:8 + n_f_rows], sc_row(dsc_mine), jnp.zeros((n_pad, d), F32)], axis=0)
    small = (g_small,) + tuple(_small_adamw(w_small, g_small, m_small, v_small))

    from_chips2 = rs_owner_wait("2", own2, [small[1]] + [big[nm][0] for nm in names1])
    finals(names2, parts2, from_sib2, from_chips2)

    def unpack(a):
        sc = a[8 + n_f_rows, :n_groups * pr].reshape(1, n_groups, pr)
        return {"ln_in_g": a[1], "ln_in_b": a[2], "ln1_g": a[3:4], "ln1_b": a[4:5], "b_ff2": a[5:6], "ln2_g": a[6:7],
                "ln2_b": a[7:8], "b_ff1": a[8:8 + n_f_rows].reshape(1, f), "pool_scale": sc}

    shapes = {"w_in": w_in.shape, "w_out": w_out.shape, "w_ff1": w_ff1.shape, "w_ff2": w_ff2.shape, "w_pool": w_pool.shape}
    order = ["ln_in_g", "ln_in_b", "w_in", "w_pool", "pool_scale", "w_out", "ln1_g", "ln1_b", "w_ff1", "b_ff1", "w_ff2",
             "b_ff2", "ln2_g", "ln2_b"]
    outs = []
    for kind in range(4):
        small_k = unpack(small[kind])
        for nm in order:
            outs.append(big[nm][kind].reshape(shapes[nm]) if nm in big else small_k[nm])
    return (loss11.reshape(()), dx.reshape(x.shape), *outs)
```

```python
import functools
import math

import jax
import jax.numpy as jnp
from jax import lax
from jax.experimental import pallas as pl
from jax.experimental.pallas import tpu as pltpu

F32 = jnp.float32
BF16 = jnp.bfloat16
MESH = pl.DeviceIdType.MESH

N_DEV = 8
HEAD_DIM = 128
POOL_WINDOWS = (2, 4, 8, 16)
DEEPNORM_ALPHA = (2.0 * 1) ** 0.25
LN_EPS = 1e-5
ADAM_LR = 0.001
ADAM_B1 = 0.9
ADAM_B2 = 0.999
ADAM_EPS = 1e-08
ADAM_WD = 0.01
ADAM_STEP = 10

V7X_VMEM_LIMIT = 56 * 1024 * 1024
ATT_BLOCK = 256
POOL_CHUNK = 128

NN = (((1,), (0,)), ((), ()))
NT = (((1,), (1,)), ((), ()))
TN = (((0,), (0,)), ((), ()))


def _dot(a, b, dims=NN):
    return lax.dot_general(a, b, dims, preferred_element_type=F32)


def _cparams(sem=None):
    return pltpu.CompilerParams(dimension_semantics=sem, vmem_limit_bytes=V7X_VMEM_LIMIT)


def _ln_stats(r):
    mu = jnp.mean(r, axis=-1, keepdims=True)
    xc = r - mu
    var = jnp.mean(xc * xc, axis=-1, keepdims=True)
    rstd = lax.rsqrt(var + LN_EPS)
    return xc * rstd, rstd


def _ln_bwd(dy, xhat, rstd, g):
    dxh = dy * g
    m1 = jnp.mean(dxh, axis=-1, keepdims=True)
    m2 = jnp.mean(dxh * xhat, axis=-1, keepdims=True)
    dx = rstd * (dxh - m1 - xhat * m2)
    dg = jnp.sum(dy * xhat, axis=0, keepdims=True)
    db = jnp.sum(dy, axis=0, keepdims=True)
    return dx, dg, db


def _acc_rows(first, ref, val):
    @pl.when(first)
    def _():
        ref[...] = val

    @pl.when(jnp.logical_not(first))
    def _():
        ref[...] += val


def _call(body, *, name, grid, in_specs, out_specs, out_shape, inputs, scratch_shapes=(), sem=None, comm=None):
    in_specs, out_specs, out_shape, inputs = list(in_specs), list(out_specs), list(out_shape), list(inputs)
    if comm is None:
        outs = pl.pallas_call(
            body, name=name, grid=grid, in_specs=in_specs, out_specs=out_specs, out_shape=out_shape,
            scratch_shapes=list(scratch_shapes), compiler_params=_cparams(sem))(*inputs)
        return list(outs), []
    arrays, plan, n_copies = comm
    n_in, n_out, nc, n_scr = len(inputs), len(out_shape), len(arrays), len(scratch_shapes)

    def hosted(*refs):
        ins = refs[:n_in]
        outs = refs[n_in + nc:n_in + nc + n_out]
        passed = refs[n_in + nc + n_out:n_in + 2 * nc + n_out]
        scratch = refs[n_in + 2 * nc + n_out:n_in + 2 * nc + n_out + n_scr]
        send_sems, recv_sems = refs[-2], refs[-1]
        ids = [pl.program_id(ax) for ax in range(len(grid))]
        first = functools.reduce(jnp.logical_and, [i_ == 0 for i_ in ids])
        last = functools.reduce(jnp.logical_and, [i_ == g - 1 for i_, g in zip(ids, grid)])

        @pl.when(first)
        def _():
            for cp in _plan_copies(plan, passed, send_sems, recv_sems):
                cp.start()

        body(*ins, *outs, *scratch)

        @pl.when(last)
        def _():
            for cp in _plan_copies(plan, passed, send_sems, recv_sems):
                cp.wait_send()
                cp.wait_recv()

    any_spec = pl.BlockSpec(memory_space=pl.ANY)
    outs = pl.pallas_call(
        hosted, name=name, grid=grid,
        in_specs=in_specs + [any_spec] * nc, out_specs=out_specs + [any_spec] * nc,
        out_shape=out_shape + [jax.ShapeDtypeStruct(a.shape, a.dtype) for a in arrays],
        scratch_shapes=list(scratch_shapes) + [pltpu.SemaphoreType.DMA((n_copies,)), pltpu.SemaphoreType.DMA((n_copies,))],
        input_output_aliases={n_in + i: n_out + i for i in range(nc)},
        compiler_params=pltpu.CompilerParams(dimension_semantics=("arbitrary",) * len(grid),
                                             vmem_limit_bytes=V7X_VMEM_LIMIT, has_side_effects=True),
    )(*inputs, *arrays)
    return list(outs[:n_out]), list(outs[n_out:])


def _matmul(name, a, b, *, dims, grid, a_spec, b_spec, extras=(), extra_specs=(), out_shape, out_specs,
            acc_shape, epilogue, k_axis=2, sem=("parallel", "parallel", "arbitrary"), comm=None):
    nk = grid[k_axis]
    n_extra = len(extras)
    n_out = len(out_shape)

    def body(a_ref, b_ref, *rest):
        extra_refs = rest[:n_extra]
        out_refs = rest[n_extra:n_extra + n_out]
        if nk == 1:
            epilogue(_dot(a_ref[...], b_ref[...], dims), extra_refs, out_refs)
            return
        acc_ref = rest[n_extra + n_out]
        k = pl.program_id(k_axis)

        @pl.when(k == 0)
        def _():
            acc_ref[...] = jnp.zeros(acc_shape, F32)

        acc_ref[...] += _dot(a_ref[...], b_ref[...], dims)

        @pl.when(k == nk - 1)
        def _():
            epilogue(acc_ref[...], extra_refs, out_refs)

    outs, passed = _call(
        body, name=name, grid=grid, in_specs=[a_spec, b_spec, *extra_specs], out_specs=out_specs, out_shape=out_shape,
        inputs=[a, b, *extras], scratch_shapes=[] if nk == 1 else [pltpu.VMEM(acc_shape, F32)], sem=sem, comm=comm)
    return outs if comm is None else (outs, passed)


def _store_epilogue(dtype):
    def ep(acc, extra_refs, out_refs):
        out_refs[0][...] = acc.astype(dtype)
    return ep


def _ln_in_fwd(x, g, b, tm):
    t, d = x.shape

    def body(x_ref, g_ref, b_ref, h_ref, hb_ref):
        xhat, _ = _ln_stats(x_ref[...])
        h = xhat * g_ref[...] + b_ref[...]
        h_ref[...] = h
        hb_ref[...] = h.astype(BF16)

    row = pl.BlockSpec((tm, d), lambda i: (i, 0))
    vec = pl.BlockSpec((1, d), lambda i: (0, 0))
    return pl.pallas_call(
        body, name="ln_in_fwd", grid=(t // tm,), in_specs=[row, vec, vec], out_specs=[row, row],
        out_shape=[jax.ShapeDtypeStruct((t, d), F32), jax.ShapeDtypeStruct((t, d), BF16)],
        compiler_params=_cparams(("parallel",)),
    )(x, g, b)


def _split3(x):
    hi = x.astype(BF16)
    r = x - hi.astype(F32)
    mid = r.astype(BF16)
    lo = (r - mid.astype(F32)).astype(BF16)
    return hi, mid, lo


def _split2(x):
    hi = x.astype(BF16)
    lo = (x - hi.astype(F32)).astype(BF16)
    return hi, lo


def _pool_fwd(u, wp, sc, t, c):
    n_groups = len(POOL_WINDOWS)
    tc = POOL_CHUNK
    n_chunks = t // tc

    def body(u_ref, wp_ref, sc_ref, y_ref, ypre_ref, xp_ref):
        g = pl.program_id(0)
        xp_ref[pl.ds(0, tc), :] = jnp.zeros((tc, c), F32)
        xp_ref[pl.ds(tc, t), :] = u_ref[...]
        out_i = lax.broadcasted_iota(jnp.int32, (tc, 2 * tc), 0)
        in_j = lax.broadcasted_iota(jnp.int32, (tc, 2 * tc), 1)
        lag = tc + out_i - in_j
        t_in_chunk = lax.broadcasted_iota(jnp.int32, (tc, 1), 0)
        for gi, w in enumerate(POOL_WINDOWS):
            @pl.when(g == gi)
            def _(w=w):
                band = jnp.logical_and(lag >= 0, lag < w).astype(BF16)

                def chunk(ci, carry):
                    start = pl.multiple_of(ci * tc, tc)
                    win = xp_ref[pl.ds(start, 2 * tc), :]
                    hi, mid, lo = _split3(win)
                    wsum = _dot(band, hi) + _dot(band, mid) + _dot(band, lo)
                    cnt = jnp.minimum(ci * tc + t_in_chunk + 1, w).astype(F32)
                    ypre = wsum * (1.0 / cnt) - win[tc:, :]
                    ypre_b = ypre.astype(BF16)
                    y = _dot(ypre_b, wp_ref[...]) * sc_ref[...]
                    ypre_ref[pl.ds(start, tc), :] = ypre_b
                    y_ref[pl.ds(start, tc), :] = y.astype(BF16)
                    return carry

                lax.fori_loop(0, n_chunks, chunk, 0)

    col = pl.BlockSpec((t, c), lambda g: (0, g))
    return pl.pallas_call(
        body, name="pool_fwd", grid=(n_groups,),
        in_specs=[col, pl.BlockSpec((None, c, c), lambda g: (g, 0, 0)), pl.BlockSpec((None, 1, c), lambda g: (g, 0, 0))],
        out_specs=[col, col],
        out_shape=[jax.ShapeDtypeStruct((t, n_groups * c), BF16), jax.ShapeDtypeStruct((t, n_groups * c), BF16)],
        scratch_shapes=[pltpu.VMEM((t + tc, c), F32)],
        compiler_params=_cparams(("parallel",)),
    )(u, wp, sc)


def _pool_bwd(dmixin, ypre, wp, sc, t, c, comm=None):
    n_groups = len(POOL_WINDOWS)
    tc = POOL_CHUNK
    n_chunks = t // tc

    def body(dy_ref, ypre_ref, wp_ref, sc_ref, du_ref, dwp_ref, dsc_ref, zp_ref):
        g = pl.program_id(0)
        zp_ref[pl.ds(t, tc), :] = jnp.zeros((tc, c), F32)
        dwp_ref[...] = jnp.zeros((c, c), F32)
        dsc_ref[...] = jnp.zeros((1, c), F32)
        out_i = lax.broadcasted_iota(jnp.int32, (tc, 2 * tc), 0)
        in_j = lax.broadcasted_iota(jnp.int32, (tc, 2 * tc), 1)
        lead = in_j - out_i
        t_in_chunk = lax.broadcasted_iota(jnp.int32, (tc, 1), 0)
        for gi, w in enumerate(POOL_WINDOWS):
            @pl.when(g == gi)
            def _(w=w):
                band = jnp.logical_and(lead >= 0, lead < w).astype(BF16)

                def first(ci, carry):
                    start = pl.multiple_of(ci * tc, tc)
                    dy = dy_ref[pl.ds(start, tc), :]
                    yp = ypre_ref[pl.ds(start, tc), :]
                    ymm = _dot(yp, wp_ref[...])
                    dsc_ref[...] += jnp.sum(dy * ymm, axis=0, keepdims=True)
                    dys_b = (dy * sc_ref[...]).astype(BF16)
                    dwp_ref[...] += _dot(yp, dys_b, TN)
                    dyp = _dot(dys_b, wp_ref[...], NT)
                    cnt = jnp.minimum(ci * tc + t_in_chunk + 1, w).astype(F32)
                    zp_ref[pl.ds(start, tc), :] = dyp * (1.0 / cnt)
                    du_ref[pl.ds(start, tc), :] = -dyp
                    return carry

                lax.fori_loop(0, n_chunks, first, 0)

                def second(ci, carry):
                    start = pl.multiple_of(ci * tc, tc)
                    hi, mid, lo = _split3(zp_ref[pl.ds(start, 2 * tc), :])
                    du_ref[pl.ds(start, tc), :] += _dot(band, hi) + _dot(band, mid) + _dot(band, lo)
                    return carry

                lax.fori_loop(0, n_chunks, second, 0)

    col = pl.BlockSpec((t, c), lambda g: (0, g))
    return _call(
        body, name="pool_bwd", grid=(n_groups,),
        in_specs=[col, col, pl.BlockSpec((None, c, c), lambda g: (g, 0, 0)), pl.BlockSpec((None, 1, c), lambda g: (g, 0, 0))],
        out_specs=[col, pl.BlockSpec((None, c, c), lambda g: (g, 0, 0)), pl.BlockSpec((None, 1, c), lambda g: (g, 0, 0))],
        out_shape=[jax.ShapeDtypeStruct((t, n_groups * c), F32), jax.ShapeDtypeStruct((n_groups, c, c), F32),
                   jax.ShapeDtypeStruct((n_groups, 1, c), F32)],
        inputs=[dmixin, ypre, wp, sc], scratch_shapes=[pltpu.VMEM((t + tc, c), F32)], sem=("parallel",), comm=comm)


ROW_PARTS = 2


def _att_consts():
    b = ATT_BLOCK
    rp = b // ROW_PARTS
    row = lax.broadcasted_iota(jnp.int32, (b, b), 0)
    col = lax.broadcasted_iota(jnp.int32, (b, b), 1)
    tri = (row >= col).astype(BF16)
    prow = lax.broadcasted_iota(jnp.int32, (rp, b), 0)
    pcol = lax.broadcasted_iota(jnp.int32, (rp, b), 1)
    causal = [pcol < prow + r * rp for r in range(ROW_PARTS)]
    return tri, causal


def _suffix_sum(x, tri):
    hi, lo = _split2(x)
    return _dot(hi, tri) + _dot(lo, tri)


LOG2_E = 1.4426950408889634


def _att_scores(qb, kb, mask):
    z2 = _dot(qb, kb, NT) * (LOG2_E / math.sqrt(HEAD_DIM))
    sp2 = jnp.maximum(z2, 0.0) + jnp.log2(1.0 + jnp.exp2(-jnp.abs(z2)))
    return z2, sp2, (sp2 if mask is None else jnp.where(mask, sp2, 0.0))


HEADS_PER_STEP = 2
ATT_LANES = HEADS_PER_STEP * HEAD_DIM


def _head_lanes(s):
    return slice(s * HEAD_DIM, (s + 1) * HEAD_DIM)


UNDERFLOW_LOG2 = 160.0


def _sweep_earlier_blocks(i, state, per_chain, block):
    def lowest(st):
        low = st[0]
        for k in range(per_chain, len(st), per_chain):
            low = jnp.minimum(low, st[k])
        return jnp.min(low)

    def more(c):
        return jnp.logical_and(c[0] < i, c[1] < UNDERFLOW_LOG2)

    def trip(c):
        st = block(i - 1 - c[0], c[2:])
        return (c[0] + 1, lowest(st)) + tuple(st)

    return lax.while_loop(more, trip, (jnp.int32(0), lowest(state)) + tuple(state))[2:]


def _attn_fwd(qkv, t, n_heads, comm=None):
    b = ATT_BLOCK
    nq = t // b
    n_steps = n_heads // HEADS_PER_STEP

    rp = b // ROW_PARTS
    chains = [(s, r) for s in range(HEADS_PER_STEP) for r in range(ROW_PARTS)]
    no_mask = [None] * ROW_PARTS

    def body(q_ref, k_ref, v_ref, o_ref):
        tri, causal = _att_consts()

        def blocks(qbs, j, state, masks):
            ks = pl.multiple_of(j * b, b)
            scores = [_att_scores(qbs[ci], k_ref[pl.ds(ks, b), _head_lanes(s)], masks[r]) for ci, (s, r) in enumerate(chains)]
            incls = [_suffix_sum(sc[2], tri) for sc in scores]
            out = []
            for ci, (s, r) in enumerate(chains):
                carry, acc = state[2 * ci], state[2 * ci + 1]
                a = jnp.exp2(scores[ci][0] - (incls[ci] + carry))
                if masks[r] is not None:
                    a = jnp.where(masks[r], a, 0.0)
                out += [carry + incls[ci][:, 0:1], acc + _dot(a.astype(BF16), v_ref[pl.ds(ks, b), _head_lanes(s)])]
            return tuple(out)

        def q_loop(i, _):
            qs = pl.multiple_of(i * b, b)
            qbs = [q_ref[pl.ds(qs + r * rp, rp), _head_lanes(s)] for s, r in chains]
            zero = (jnp.zeros((rp, 1), F32), jnp.zeros((rp, HEAD_DIM), F32)) * len(chains)
            state = blocks(qbs, i, zero, causal)
            state = _sweep_earlier_blocks(i, state, 2, lambda j, st: blocks(qbs, j, st, no_mask))
            for ci, (s, r) in enumerate(chains):
                o_ref[pl.ds(qs + r * rp, rp), _head_lanes(s)] = state[2 * ci + 1]
            return 0

        lax.fori_loop(0, nq, q_loop, 0)

    def heads(off):
        return pl.BlockSpec((t, ATT_LANES), lambda h: (0, off + h))

    return _call(
        body, name="attn_fwd", grid=(n_steps,),
        in_specs=[heads(0), heads(n_steps), heads(2 * n_steps)], out_specs=[heads(0)],
        out_shape=[jax.ShapeDtypeStruct((t, n_heads * HEAD_DIM), F32)],
        inputs=[qkv, qkv, qkv], sem=("parallel",), comm=comm)


def _attn_bwd(qkv, do, o, t, n_heads, comm=None):
    b = ATT_BLOCK
    nq = t // b
    n_steps = n_heads // HEADS_PER_STEP
    scale = 1.0 / math.sqrt(HEAD_DIM)
    rp = b // ROW_PARTS
    chains = [(s, r) for s in range(HEADS_PER_STEP) for r in range(ROW_PARTS)]
    no_mask = [None] * ROW_PARTS

    def body(q_ref, k_ref, v_ref, do_ref, o_ref, dq_ref, dk_ref, dv_ref, qt_ref, dot_ref, dkt_ref, dvt_ref):
        for j in range(nq):
            rows = pl.ds(j * b, b)
            qt_ref[j] = q_ref[rows, :].astype(F32).T.astype(BF16)
            dot_ref[j] = do_ref[rows, :].astype(F32).T.astype(BF16)
        dkt_ref[...] = jnp.zeros((nq, ATT_LANES, b), F32)
        dvt_ref[...] = jnp.zeros((nq, ATT_LANES, b), F32)
        tri, causal = _att_consts()

        def blocks(i, fixed, j, state, masks):
            ks = pl.multiple_of(j * b, b)
            n = len(chains)
            kbs = [k_ref[pl.ds(ks, b), _head_lanes(s)] for s, _ in chains]
            scores = [_att_scores(fixed[ci][0], kbs[ci], masks[r]) for ci, (s, r) in enumerate(chains)]
            incls = [_suffix_sum(sc[2], tri) for sc in scores]
            das = [_dot(fixed[ci][1], v_ref[pl.ds(ks, b), _head_lanes(s)], NT) for ci, (s, r) in enumerate(chains)]
            a_bs, gs = [], []
            for ci, (s, r) in enumerate(chains):
                a = jnp.exp2(scores[ci][0] - (incls[ci] + state[3 * ci]))
                if masks[r] is not None:
                    a = jnp.where(masks[r], a, 0.0)
                a_bs.append(a.astype(BF16))
                gs.append(a_bs[ci].astype(F32) * das[ci])
            g_incls = [_suffix_sum(g, tri) for g in gs]
            dz_bs = []
            for ci, (s, r) in enumerate(chains):
                rest = (fixed[ci][2] - state[3 * ci + 1]) - (g_incls[ci] - gs[ci])
                sig = jnp.exp2(scores[ci][0] - scores[ci][1])
                dz = (gs[ci] - sig * rest) * scale
                if masks[r] is not None:
                    dz = jnp.where(masks[r], dz, 0.0)
                dz_bs.append(dz.astype(BF16))
            out = []
            for ci in range(n):
                out += [state[3 * ci] + incls[ci][:, 0:1], state[3 * ci + 1] + g_incls[ci][:, 0:1],
                        state[3 * ci + 2] + _dot(dz_bs[ci], kbs[ci])]
            for s in range(HEADS_PER_STEP):
                lanes = _head_lanes(s)
                dk_add, dv_add = None, None
                for ci, (cs, r) in enumerate(chains):
                    if cs == s:
                        part = slice(r * rp, (r + 1) * rp)
                        dk_c = _dot(qt_ref[i, lanes, part], dz_bs[ci])
                        dv_c = _dot(dot_ref[i, lanes, part], a_bs[ci])
                        dk_add = dk_c if dk_add is None else dk_add + dk_c
                        dv_add = dv_c if dv_add is None else dv_add + dv_c
                dkt_ref[j, lanes, :] += dk_add
                dvt_ref[j, lanes, :] += dv_add
            return tuple(out)

        def q_loop(i, _):
            qs = pl.multiple_of(i * b, b)
            fixed = []
            for s, r in chains:
                rows = pl.ds(qs + r * rp, rp)
                dob = do_ref[rows, _head_lanes(s)]
                total = jnp.sum(dob.astype(F32) * o_ref[rows, _head_lanes(s)], axis=-1, keepdims=True)
                fixed.append((q_ref[rows, _head_lanes(s)], dob, total))
            zero = (jnp.zeros((rp, 1), F32), jnp.zeros((rp, 1), F32), jnp.zeros((rp, HEAD_DIM), F32)) * len(chains)
            state = blocks(i, fixed, i, zero, causal)
            state = _sweep_earlier_blocks(i, state, 3, lambda j, st: blocks(i, fixed, j, st, no_mask))
            for ci, (s, r) in enumerate(chains):
                dq_ref[pl.ds(qs + r * rp, rp), _head_lanes(s)] = state[3 * ci + 2].astype(BF16)
            return 0

        lax.fori_loop(0, nq, q_loop, 0)
        for j in range(nq):
            rows = pl.ds(j * b, b)
            dk_ref[rows, :] = dkt_ref[j].T.astype(BF16)
            dv_ref[rows, :] = dvt_ref[j].T.astype(BF16)

    def heads(off):
        return pl.BlockSpec((t, ATT_LANES), lambda h: (0, off + h))

    shape = jax.ShapeDtypeStruct((t, n_heads * HEAD_DIM), BF16)
    return _call(
        body, name="attn_bwd", grid=(n_steps,),
        in_specs=[heads(0), heads(n_steps), heads(2 * n_steps), heads(0), heads(0)],
        out_specs=[heads(0)] * 3, out_shape=[shape] * 3, inputs=[qkv, qkv, qkv, do, o],
        scratch_shapes=[pltpu.VMEM((nq, ATT_LANES, b), BF16)] * 2 + [pltpu.VMEM((nq, ATT_LANES, b), F32)] * 2,
        sem=("parallel",), comm=comm)


def _place():
    x, y, c = lax.axis_index("x"), lax.axis_index("y"), lax.axis_index("c")
    return x, y, c


def _all_gather_weights(shards):
    n = len(shards)

    def body(*refs):
        ins, outs = refs[:n], refs[n:2 * n]
        send_sems, recv_sems, local_sems = refs[2 * n:]
        x, y, c = _place()
        me, sibling = (x, y, c), (x, y, 1 - c)
        chips = [(1 - x, y), (x, 1 - y), (1 - x, 1 - y)]

        def slot(px, py, pc):
            return 4 * px + 2 * py + pc

        def copy(ti, k, block, to, src=None):
            dst = outs[ti].at[slot(*block)]
            return pltpu.make_async_remote_copy(
                src_ref=dst if src is None else src, dst_ref=dst,
                send_sem=send_sems.at[ti, k], recv_sem=recv_sems.at[ti, k], device_id=to, device_id_type=MESH)

        mine = [pltpu.make_async_copy(ins[ti], outs[ti].at[slot(*me)], local_sems.at[ti]) for ti in range(n)]
        for cp in mine:
            cp.start()
        first = []
        for ti in range(n):
            first.append(copy(ti, 0, me, sibling, src=ins[ti]))
            first += [copy(ti, 1 + j, me, (*chip, c), src=ins[ti]) for j, chip in enumerate(chips)]
        for cp in first:
            cp.start()
        passed = []
        for j, chip in enumerate(chips):
            for ti in range(n):
                copy(ti, 1 + j, (*chip, c), me).wait_recv()
                fwd = copy(ti, 4 + j, (*chip, c), sibling)
                fwd.start()
                passed.append(fwd)
        for ti in range(n):
            copy(ti, 0, sibling, me).wait_recv()
            for j, chip in enumerate(chips):
                copy(ti, 4 + j, (*chip, 1 - c), me).wait_recv()
        for cp in first + passed:
            cp.wait_send()
        for cp in mine:
            cp.wait()

    any_spec = pl.BlockSpec(memory_space=pl.ANY)
    return pl.pallas_call(
        body, name="all_gather_weights",
        in_specs=[any_spec] * n, out_specs=[any_spec] * n,
        out_shape=[jax.ShapeDtypeStruct((N_DEV, *s.shape), s.dtype) for s in shards],
        scratch_shapes=[pltpu.SemaphoreType.DMA((n, 7)), pltpu.SemaphoreType.DMA((n, 7)), pltpu.SemaphoreType.DMA((n,))],
        compiler_params=pltpu.CompilerParams(has_side_effects=True),
    )(*shards)


def _flip(v, on):
    return 1 - v if on else v


def _plan_copies(plan, refs, send_sems, recv_sems):
    return [pltpu.make_async_remote_copy(src_ref=src, dst_ref=dst, send_sem=send_sems.at[k], recv_sem=recv_sems.at[k],
                                         device_id=dev, device_id_type=MESH)
            for k, (src, dst, dev) in enumerate(plan(refs))]


def _copies_now(name, arrays, plan, n_copies):
    n = len(arrays)

    def body(*refs):
        copies = _plan_copies(plan, refs[n:2 * n], refs[2 * n], refs[2 * n + 1])
        for cp in copies:
            cp.start()
        for cp in copies:
            cp.wait_send()
            cp.wait_recv()

    any_spec = pl.BlockSpec(memory_space=pl.ANY)
    return list(pl.pallas_call(
        body, name=name, in_specs=[any_spec] * n, out_specs=[any_spec] * n,
        out_shape=[jax.ShapeDtypeStruct(a.shape, a.dtype) for a in arrays],
        input_output_aliases={i: i for i in range(n)},
        scratch_shapes=[pltpu.SemaphoreType.DMA((n_copies,)), pltpu.SemaphoreType.DMA((n_copies,))],
        compiler_params=pltpu.CompilerParams(has_side_effects=True),
    )(*arrays))


def _plan_gather_first(n_direct, n_two_level):
    def plan(refs):
        x, y, c = _place()
        me = 4 * x + 2 * y + c
        out = []
        for ti in range(n_direct + n_two_level):
            mine = refs[ti].at[me]
            for k in range(1, N_DEV):
                if ti < n_direct or k == 1 or not k & 1:
                    out.append((mine, mine, (_flip(x, k & 4), _flip(y, k & 2), _flip(c, k & 1))))
        return out
    return plan, 7 * n_direct + 4 * n_two_level


def _plan_gather_forward(n):
    def plan(refs):
        x, y, c = _place()
        out = []
        for ti in range(n):
            for r in range(1, 4):
                blk = refs[ti].at[4 * _flip(x, r & 2) + 2 * _flip(y, r & 1) + c]
                out.append((blk, blk, (x, y, 1 - c)))
        return out
    return plan, 3 * n


def _join_plans(*parts):
    def plan(refs):
        out, at = [], 0
        for part, n_arrays, _ in parts:
            out += part(refs[at:at + n_arrays])
            at += n_arrays
        return out
    return plan, sum(n_cp for _, _, n_cp in parts)


def _plan_rs_sibling(n):
    def plan(refs):
        x, y, c = _place()
        out = []
        for ti in range(n):
            for r in range(4):
                src = refs[ti].at[4 * _flip(x, r & 2) + 2 * _flip(y, r & 1) + (1 - c)]
                out.append((src, refs[n + ti].at[r], (x, y, 1 - c)))
        return out
    return plan, 4 * n


def _plan_rs_owner(n):
    def plan(refs):
        x, y, c = _place()
        out = []
        for ti in range(n):
            for r in range(1, 4):
                out.append((refs[ti].at[r], refs[n + ti].at[r], (_flip(x, r & 2), _flip(y, r & 1), c)))
        return out
    return plan, 3 * n


def _owner_slots():
    x, y, c = _place()
    idx = []
    for r in range(4):
        ox, oy = (1 - x if r & 2 else x), (1 - y if r & 1 else y)
        idx.append(4 * ox + 2 * oy + c)
    return jnp.stack(idx).astype(jnp.int32)


def _row_tile(rows, cols):
    tr = max(8, min(rows, (1 << 19) // cols))
    while rows % tr:
        tr //= 2
    return tr


def _rs_chip_sum(name, slots, partial, from_sibling):
    _, rows, cols = partial.shape
    tr = _row_tile(rows, cols)

    def body(slots_ref, p_ref, s_ref, o_ref):
        o_ref[...] = (p_ref[...] + s_ref[...]).astype(BF16)

    grid_spec = pltpu.PrefetchScalarGridSpec(
        num_scalar_prefetch=1, grid=(3, rows // tr),
        in_specs=[pl.BlockSpec((None, tr, cols), lambda r, i, s: (s[r + 1], i, 0)),
                  pl.BlockSpec((None, tr, cols), lambda r, i, s: (r + 1, i, 0))],
        out_specs=pl.BlockSpec((None, tr, cols), lambda r, i, s: (r + 1, i, 0)))
    return pl.pallas_call(
        body, name=name, grid_spec=grid_spec, out_shape=jax.ShapeDtypeStruct((4, rows, cols), BF16),
        compiler_params=_cparams(("parallel", "parallel")),
    )(slots, partial, from_sibling)


def _adamw(w, g, m, v):
    m = ADAM_B1 * m + (1.0 - ADAM_B1) * g
    v = ADAM_B2 * v + (1.0 - ADAM_B2) * (g * g)
    m_hat = m / (1.0 - ADAM_B1 ** ADAM_STEP)
    v_hat = v / (1.0 - ADAM_B2 ** ADAM_STEP)
    delta = -ADAM_LR * (m_hat / (jnp.sqrt(v_hat) + ADAM_EPS) + ADAM_WD * w)
    return delta, m, v


def _rs_final_adamw(name, slots, partial, from_sibling, from_chips, w, m, v):
    rows, cols = w.shape
    tr = _row_tile(rows, cols)

    def body(slots_ref, p_ref, s_ref, c1_ref, c2_ref, c3_ref, w_ref, m_ref, v_ref, g_ref, d_ref, nm_ref, nv_ref):
        g = p_ref[...] + s_ref[...]
        g = g + c1_ref[...].astype(F32)
        g = g + c2_ref[...].astype(F32)
        g = g + c3_ref[...].astype(F32)
        delta, nm, nv = _adamw(w_ref[...], g, m_ref[...], v_ref[...])
        g_ref[...] = g
        d_ref[...] = delta
        nm_ref[...] = nm
        nv_ref[...] = nv

    def slot(r):
        return pl.BlockSpec((None, tr, cols), lambda i, s: (r, i, 0))

    flat = pl.BlockSpec((tr, cols), lambda i, s: (i, 0))
    grid_spec = pltpu.PrefetchScalarGridSpec(
        num_scalar_prefetch=1, grid=(rows // tr,),
        in_specs=[pl.BlockSpec((None, tr, cols), lambda i, s: (s[0], i, 0)), slot(0), slot(1), slot(2), slot(3), flat, flat, flat],
        out_specs=[flat] * 4)
    return pl.pallas_call(
        body, name=name, grid_spec=grid_spec, out_shape=[jax.ShapeDtypeStruct((rows, cols), F32)] * 4,
        compiler_params=_cparams(("parallel",)),
    )(slots, partial, from_sibling, from_chips, from_chips, from_chips, w, m, v)


def _small_all_reduce(packet):
    rows, d = packet.shape

    def body(p_ref, sum_ref, loss_ref, all_ref, send_sems, recv_sems):
        x, y, c = _place()
        me = 4 * x + 2 * y + c
        all_ref[me] = p_ref[...]
        copies = []
        for k in range(1, N_DEV):
            px, py, pc = (1 - x if k & 4 else x), (1 - y if k & 2 else y), (1 - c if k & 1 else c)
            cp = pltpu.make_async_remote_copy(
                src_ref=p_ref, dst_ref=all_ref.at[me], send_sem=send_sems.at[k], recv_sem=recv_sems.at[k],
                device_id=(px, py, pc), device_id_type=MESH)
            cp.start()
            copies.append(cp)
        for cp in copies:
            cp.wait_recv()
        for cp in copies:
            cp.wait_send()
        total = all_ref[0]
        for j in range(1, N_DEV):
            total = total + all_ref[j]
        sum_ref[...] = total
        loss_ref[...] = jnp.sum(total[0:1, :], axis=-1, keepdims=True)

    vmem = pl.BlockSpec(memory_space=pltpu.VMEM)
    return pl.pallas_call(
        body, name="small_all_reduce",
        in_specs=[vmem], out_specs=[vmem, vmem],
        out_shape=[jax.ShapeDtypeStruct((rows, d), F32), jax.ShapeDtypeStruct((1, 1), F32)],
        scratch_shapes=[pltpu.VMEM((N_DEV, rows, d), F32), pltpu.SemaphoreType.DMA((N_DEV,)), pltpu.SemaphoreType.DMA((N_DEV,))],
        compiler_params=pltpu.CompilerParams(has_side_effects=True),
    )(packet)


def _small_adamw(w, g, m, v):
    def body(w_ref, g_ref, m_ref, v_ref, d_ref, nm_ref, nv_ref):
        delta, nm, nv = _adamw(w_ref[...], g_ref[...], m_ref[...], v_ref[...])
        d_ref[...] = delta
        nm_ref[...] = nm
        nv_ref[...] = nv

    vmem = pl.BlockSpec(memory_space=pltpu.VMEM)
    return pl.pallas_call(
        body, name="small_adamw", in_specs=[vmem] * 4, out_specs=[vmem] * 3,
        out_shape=[jax.ShapeDtypeStruct(w.shape, F32)] * 3,
    )(w, g, m, v)


def kernel(x, ln_in_g, ln_in_b, w_in, w_pool, pool_scale, w_out, ln1_g, ln1_b, w_ff1, b_ff1, w_ff2, b_ff2, ln2_g, ln2_b, loss_target, m_ln_in_g, m_ln_in_b, m_w_in, m_w_pool, m_pool_scale, m_w_out, m_ln1_g, m_ln1_b, m_w_ff1, m_b_ff1, m_w_ff2, m_b_ff2, m_ln2_g, m_ln2_b, v_ln_in_g, v_ln_in_b, v_w_in, v_w_pool, v_pool_scale, v_w_out, v_ln1_g, v_ln1_b, v_w_ff1, v_b_ff1, v_w_ff2, v_b_ff2, v_ln2_g, v_ln2_b):
    t, d = x.shape[1], x.shape[2]
    n_groups = len(POOL_WINDOWS)
    c_pool = w_pool.shape[3]
    p = n_groups * c_pool
    n_heads = (d - p) // HEAD_DIM
    ws_in = w_in.shape[2]
    n_in = N_DEV * ws_in
    ws_out = w_out.shape[1]
    ws_f = w_ff1.shape[2]
    f = N_DEV * ws_f
    pr = w_pool.shape[2]
    assert n_in == p + 3 * n_heads * HEAD_DIM and N_DEV * ws_out == d and N_DEV * pr == c_pool

    tm_big = min(t, 1024)
    tm_ep = min(t, 512)
    tkk = min(t, 2048)
    half_f = min(ws_f, 512)
    per_f = ws_f // half_f

    x2 = x.reshape(t, d)
    target = loss_target.reshape(t, d)
    g0, b0 = ln_in_g.reshape(1, d), ln_in_b.reshape(1, d)

    shards = [w_in.reshape(d, ws_in), w_out.reshape(ws_out, d), w_ff1.reshape(d, ws_f), w_ff2.reshape(ws_f, d),
              w_pool.reshape(n_groups * pr, c_pool)]
    x_, y_, c_ = _place()
    me = 4 * x_ + 2 * y_ + c_
    win_g, wpool_g, scale_g = _all_gather_weights(
        [shards[0].astype(BF16), shards[4].astype(BF16), pool_scale.reshape(n_groups, pr)])
    land_out, land_1, land_2 = [
        lax.dynamic_update_index_in_dim(lax.empty((N_DEV, *s.shape), BF16), s.astype(BF16), me, 0) for s in shards[1:4]]
    wp_full = wpool_g.reshape(N_DEV, n_groups, pr, c_pool).transpose(1, 0, 2, 3).reshape(n_groups, c_pool, c_pool)
    sc_full = scale_g.transpose(1, 0, 2).reshape(n_groups, 1, c_pool)

    def sds(shape, dtype=F32):
        return jax.ShapeDtypeStruct(shape, dtype)

    vec = pl.BlockSpec((1, d), lambda m, n, k: (0, 0))
    row_ep = pl.BlockSpec((tm_ep, d), lambda m, n, k: (m, 0))
    tm_res = min(t, 256)
    row_res = pl.BlockSpec((tm_res, d), lambda m, n, k: (m, 0))
    seq = ("arbitrary", "arbitrary", "arbitrary")

    h0, h0b = _ln_in_fwd(x2, g0, b0, tm_big)

    pool_shards = p // ws_in

    def mm_u(name, first, count, dtype, comm=None):
        return _matmul(
            name, h0b, win_g, dims=NN, grid=(t // tm_big, count, 1),
            a_spec=pl.BlockSpec((tm_big, d), lambda m, n, k: (m, 0)),
            b_spec=pl.BlockSpec((None, d, ws_in), lambda m, n, k: (n + first, 0, 0)),
            out_shape=[sds((t, count * ws_in), dtype)],
            out_specs=[pl.BlockSpec((tm_big, ws_in), lambda m, n, k: (m, n))],
            acc_shape=(tm_big, ws_in), epilogue=_store_epilogue(dtype), comm=comm)

    direct, two_level, forward = _plan_gather_first(1, 0), _plan_gather_first(0, 1), _plan_gather_forward(1)
    (u_pool,) = mm_u("mm_u_pool", 0, pool_shards, F32)
    (qkv,), (wout_g,) = mm_u("mm_u_qkv", pool_shards, N_DEV - pool_shards, BF16, comm=([land_out], *direct))

    y_pool, ypre = _pool_fwd(u_pool, wp_full, sc_full, t, c_pool)
    (o,), (w1_part,) = _attn_fwd(qkv, t, n_heads, comm=([land_1], *two_level))
    mixin = jnp.concatenate([y_pool, o.astype(BF16)], axis=1)
    wout_2d = wout_g.reshape(d, d)

    def ep_ln1(acc, ex, outs):
        h0_ref, g_ref, b_ref = ex
        r1 = DEEPNORM_ALPHA * h0_ref[...] + acc
        xhat, _ = _ln_stats(r1)
        h1 = xhat * g_ref[...] + b_ref[...]
        outs[0][...] = r1
        outs[1][...] = h1
        outs[2][...] = h1.astype(BF16)

    (r1, h1, h1b), (w1_g,) = _matmul(
        "mm_mix_ln1", mixin, wout_2d, dims=NN, grid=(t // tm_res, 1, 1),
        a_spec=pl.BlockSpec((tm_res, d), lambda m, n, k: (m, 0)),
        b_spec=pl.BlockSpec((d, d), lambda m, n, k: (0, 0)),
        extras=(h0, ln1_g, ln1_b), extra_specs=(row_res, vec, vec),
        out_shape=[sds((t, d)), sds((t, d)), sds((t, d), BF16)], out_specs=[row_res] * 3,
        acc_shape=(tm_res, d), epilogue=ep_ln1, comm=([w1_part], *forward))

    def ep_ff1(acc, ex, outs):
        f1 = acc + ex[0][...]
        outs[0][...] = f1
        r = jnp.maximum(f1, 0.0)
        outs[1][...] = (r * r).astype(BF16)

    ff_tile = pl.BlockSpec((tm_big, half_f), lambda m, n, k: (m, n))
    (f1, act), (w2_part,) = _matmul(
        "mm_ff1", h1b, w1_g, dims=NN, grid=(t // tm_big, f // half_f, 1),
        a_spec=pl.BlockSpec((tm_big, d), lambda m, n, k: (m, 0)),
        b_spec=pl.BlockSpec((None, d, half_f), lambda m, n, k: (n // per_f, 0, n % per_f)),
        extras=(b_ff1,), extra_specs=(pl.BlockSpec((1, half_f), lambda m, n, k: (0, n)),),
        out_shape=[sds((t, f)), sds((t, f), BF16)], out_specs=[ff_tile, ff_tile],
        acc_shape=(tm_big, half_f), epilogue=ep_ff1, comm=([land_2], *two_level))
    (w2_g,) = _copies_now("gather_forward_w_ff2", [w2_part], *forward)

    def ep_ln2(acc, ex, outs):
        h1_ref, tgt_ref, bf2_ref, g_ref, b_ref = ex
        dr2_ref, dr2b_ref, dg_ref, db_ref, dbf2_ref, loss_ref = outs
        r2 = DEEPNORM_ALPHA * h1_ref[...] + (acc + bf2_ref[...])
        xhat, rstd = _ln_stats(r2)
        err = xhat * g_ref[...] + b_ref[...] - tgt_ref[...]
        dr2, dg, db = _ln_bwd(err * (1.0 / d), xhat, rstd, g_ref[...])
        dr2_ref[...] = dr2
        dr2b_ref[...] = dr2.astype(BF16)
        first = pl.program_id(0) == 0
        _acc_rows(first, dg_ref, dg)
        _acc_rows(first, db_ref, db)
        _acc_rows(first, dbf2_ref, jnp.sum(dr2, axis=0, keepdims=True))
        _acc_rows(first, loss_ref, jnp.sum(err * err, axis=0, keepdims=True) * (0.5 / d))

    dr2, dr2b, dg2, db2, dbf2, loss_vec = _matmul(
        "mm_ff2_ln2_loss", act, w2_g, dims=NN, grid=(t // tm_ep, 1, N_DEV),
        a_spec=pl.BlockSpec((tm_ep, ws_f), lambda m, n, k: (m, k)),
        b_spec=pl.BlockSpec((None, ws_f, d), lambda m, n, k: (k, 0, 0)),
        extras=(h1, target, b_ff2, ln2_g, ln2_b), extra_specs=(row_ep, row_ep, vec, vec, vec),
        out_shape=[sds((t, d)), sds((t, d), BF16)] + [sds((1, d))] * 4, out_specs=[row_ep, row_ep, vec, vec, vec, vec],
        acc_shape=(tm_ep, d), epilogue=ep_ln2, sem=seq)

    def ep_dff1(acc, ex, outs):
        df1 = acc * (2.0 * jnp.maximum(ex[0][...], 0.0))
        outs[0][...] = df1.astype(BF16)
        _acc_rows(pl.program_id(1) == 0, outs[1], jnp.sum(df1, axis=0, keepdims=True))

    df_tile = pl.BlockSpec((tm_big, ws_f), lambda n, m, k: (m, n))
    df1b, dbf1 = _matmul(
        "mm_dff1", dr2b, w2_g, dims=NT, grid=(N_DEV, t // tm_big, 1),
        a_spec=pl.BlockSpec((tm_big, d), lambda n, m, k: (m, 0)),
        b_spec=pl.BlockSpec((None, ws_f, d), lambda n, m, k: (n, 0, 0)),
        extras=(f1,), extra_specs=(df_tile,),
        out_shape=[sds((t, f), BF16), sds((1, f))], out_specs=[df_tile, pl.BlockSpec((1, ws_f), lambda n, m, k: (0, n))],
        acc_shape=(tm_big, ws_f), epilogue=ep_dff1, sem=("parallel", "arbitrary", "arbitrary"))

    tn_d = min(d, 1024)
    dw2 = _matmul(
        "mm_dw2", act, dr2b, dims=TN, grid=(N_DEV, d // tn_d, t // tkk),
        a_spec=pl.BlockSpec((tkk, ws_f), lambda m, n, k: (k, m)),
        b_spec=pl.BlockSpec((tkk, tn_d), lambda m, n, k: (k, n)),
        out_shape=[sds((N_DEV, ws_f, d))], out_specs=[pl.BlockSpec((None, ws_f, tn_d), lambda m, n, k: (m, 0, n))],
        acc_shape=(ws_f, tn_d), epilogue=_store_epilogue(F32))[0]

    dw1 = _matmul(
        "mm_dw1", h1b, df1b, dims=TN, grid=(d // tn_d, N_DEV, t // tkk),
        a_spec=pl.BlockSpec((tkk, tn_d), lambda m, n, k: (k, m)),
        b_spec=pl.BlockSpec((tkk, ws_f), lambda m, n, k: (k, n)),
        out_shape=[sds((N_DEV, d, ws_f))], out_specs=[pl.BlockSpec((None, tn_d, ws_f), lambda m, n, k: (n, m, 0))],
        acc_shape=(tn_d, ws_f), epilogue=_store_epilogue(F32))[0]

    def ep_ln1_bwd(acc, ex, outs):
        dr2_ref, r1_ref, g_ref = ex
        xhat, rstd = _ln_stats(r1_ref[...])
        dr1, dg, db = _ln_bwd(DEEPNORM_ALPHA * dr2_ref[...] + acc, xhat, rstd, g_ref[...])
        outs[0][...] = dr1
        outs[1][...] = dr1.astype(BF16)
        first = pl.program_id(0) == 0
        _acc_rows(first, outs[2], dg)
        _acc_rows(first, outs[3], db)

    slots = _owner_slots()

    def to_sibling(parts):
        return (parts + [lax.empty((4, *pt.shape[1:]), F32) for pt in parts], *_plan_rs_sibling(len(parts)))

    def to_owner(names_, parts, from_sib):
        sums = [_rs_chip_sum("rs_chip_sum_" + nm, slots, pt, fs) for nm, pt, fs in zip(names_, parts, from_sib)]
        return (sums + [lax.empty(cs.shape, BF16) for cs in sums], *_plan_rs_owner(len(sums)))

    (dr1, dr1b, dg1, db1), (dw1, dw2, sib_1, sib_2) = _matmul(
        "mm_dh1_ln1_bwd", df1b, w1_g, dims=NT, grid=(t // tm_ep, 1, N_DEV),
        a_spec=pl.BlockSpec((tm_ep, ws_f), lambda m, n, k: (m, k)),
        b_spec=pl.BlockSpec((None, d, ws_f), lambda m, n, k: (k, 0, 0)),
        extras=(dr2, r1, ln1_g), extra_specs=(row_ep, row_ep, vec),
        out_shape=[sds((t, d)), sds((t, d), BF16), sds((1, d)), sds((1, d))], out_specs=[row_ep, row_ep, vec, vec],
        acc_shape=(tm_ep, d), epilogue=ep_ln1_bwd, sem=seq, comm=to_sibling([dw1, dw2]))
    own_1 = to_owner(["w_ff1"], [dw1], [sib_1])
    own_2 = to_owner(["w_ff2"], [dw2], [sib_2])

    dwout = _matmul(
        "mm_dwout", mixin, dr1b, dims=TN, grid=(d // tn_d, d // tn_d, t // tkk),
        a_spec=pl.BlockSpec((tkk, tn_d), lambda m, n, k: (k, m)),
        b_spec=pl.BlockSpec((tkk, tn_d), lambda m, n, k: (k, n)),
        out_shape=[sds((d, d))], out_specs=[pl.BlockSpec((tn_d, tn_d), lambda m, n, k: (m, n))],
        acc_shape=(tn_d, tn_d), epilogue=_store_epilogue(F32))[0].reshape(N_DEV, ws_out, d)

    tn_mix = min(tn_d, p, d - p)

    def mm_dmixin(name, first, width, dtype, comm=None):
        return _matmul(
            name, dr1b, wout_2d, dims=NT, grid=(t // tm_big, width // tn_mix, 1),
            a_spec=pl.BlockSpec((tm_big, d), lambda m, n, k: (m, 0)),
            b_spec=pl.BlockSpec((tn_mix, d), lambda m, n, k: (n + first // tn_mix, 0)),
            out_shape=[sds((t, width), dtype)], out_specs=[pl.BlockSpec((tm_big, tn_mix), lambda m, n, k: (m, n))],
            acc_shape=(tm_big, tn_mix), epilogue=_store_epilogue(dtype), comm=comm)

    (dy_pool,), (dwout, sib_out) = mm_dmixin("mm_dmixin_pool", 0, p, F32, comm=to_sibling([dwout]))
    (do,) = mm_dmixin("mm_dmixin_att", p, d - p, BF16)

    (du_pool, dwp, dsc), (_, chips_out) = _pool_bwd(
        dy_pool, ypre, wp_full, sc_full, t, c_pool, comm=to_owner(["w_out"], [dwout], [sib_out]))
    (dq, dk, dv), (_, chips_1) = _attn_bwd(qkv, do, o, t, n_heads, comm=own_1)
    dub = jnp.concatenate([du_pool.astype(BF16), dq, dk, dv], axis=1)

    dwin = _matmul(
        "mm_dwin", h0b, dub, dims=TN, grid=(d // tn_d, N_DEV, t // tkk),
        a_spec=pl.BlockSpec((tkk, tn_d), lambda m, n, k: (k, m)),
        b_spec=pl.BlockSpec((tkk, ws_in), lambda m, n, k: (k, n)),
        out_shape=[sds((N_DEV, d, ws_in))], out_specs=[pl.BlockSpec((None, tn_d, ws_in), lambda m, n, k: (n, m, 0))],
        acc_shape=(tn_d, ws_in), epilogue=_store_epilogue(F32))[0]
    dwp_g = dwp.reshape(n_groups, N_DEV, pr, c_pool).transpose(1, 0, 2, 3).reshape(N_DEV, n_groups * pr, c_pool)

    def ep_ln0_bwd(acc, ex, outs):
        dr1_ref, x_ref, g_ref = ex
        xhat, rstd = _ln_stats(x_ref[...])
        dx, dg, db = _ln_bwd(DEEPNORM_ALPHA * dr1_ref[...] + acc, xhat, rstd, g_ref[...])
        outs[0][...] = dx
        first = pl.program_id(0) == 0
        _acc_rows(first, outs[1], dg)
        _acc_rows(first, outs[2], db)

    sib_in = to_sibling([dwin, dwp_g])
    last_host = (own_2[0] + sib_in[0], *_join_plans((own_2[1], 2, own_2[2]), (sib_in[1], 4, sib_in[2])))
    (dx, dg0, db0), (_, chips_2, dwin, dwp_g, sib_in_, sib_p) = _matmul(
        "mm_dh0_ln0_bwd", dub, win_g, dims=NT, grid=(t // tm_ep, 1, N_DEV),
        a_spec=pl.BlockSpec((tm_ep, ws_in), lambda m, n, k: (m, k)),
        b_spec=pl.BlockSpec((None, d, ws_in), lambda m, n, k: (k, 0, 0)),
        extras=(dr1, x2, g0), extra_specs=(row_ep, row_ep, vec),
        out_shape=[sds((t, d)), sds((1, d)), sds((1, d))], out_specs=[row_ep, vec, vec],
        acc_shape=(tm_ep, d), epilogue=ep_ln0_bwd, sem=seq, comm=last_host)

    _, _, chips_in, chips_p = _copies_now("rs_owner_w_in", *to_owner(["w_in", "w_pool"], [dwin, dwp_g], [sib_in_, sib_p]))
    w_of = {"w_in": shards[0], "w_out": shards[1], "w_ff1": shards[2], "w_ff2": shards[3], "w_pool": shards[4]}
    mv_of = {"w_in": (m_w_in, v_w_in), "w_out": (m_w_out, v_w_out), "w_ff1": (m_w_ff1, v_w_ff1),
             "w_ff2": (m_w_ff2, v_w_ff2), "w_pool": (m_w_pool, v_w_pool)}
    big = {}
    for nm, pt, fs, fc in [("w_ff1", dw1, sib_1, chips_1), ("w_ff2", dw2, sib_2, chips_2), ("w_out", dwout, sib_out, chips_out),
                           ("w_in", dwin, sib_in_, chips_in), ("w_pool", dwp_g, sib_p, chips_p)]:
        w2d = w_of[nm]
        m_, v_ = mv_of[nm]
        big[nm] = _rs_final_adamw("rs_final_adamw_" + nm, slots, pt, fs, fc, w2d, m_.reshape(w2d.shape), v_.reshape(w2d.shape))

    n_f_rows = f // d
    pad_sc = d - p
    packet = jnp.concatenate(
        [loss_vec, dg0, db0, dg1, db1, dbf2, dg2, db2, dbf1.reshape(n_f_rows, d),
         jnp.pad(dsc.reshape(1, p), ((0, 0), (0, pad_sc)))], axis=0)
    n_rows = packet.shape[0]
    n_pad = (-n_rows) % 8
    packet = jnp.pad(packet, ((0, n_pad), (0, 0)))
    sums, loss11 = _small_all_reduce(packet)
    dsc_full = sums[8 + n_f_rows, :p].reshape(n_groups, N_DEV, pr)
    dsc_mine = lax.dynamic_index_in_dim(dsc_full, me, axis=1, keepdims=False)

    def sc_row(a):
        return jnp.pad(a.reshape(1, n_groups * pr), ((0, 0), (0, d - n_groups * pr)))

    def small_pack(ln0g, ln0b, l1g, l1b, bf2, l2g, l2b, bf1, sc):
        rows = [jnp.zeros((1, d), F32), ln0g.reshape(1, d), ln0b.reshape(1, d), l1g, l1b, bf2, l2g, l2b,
                bf1.reshape(n_f_rows, d), sc_row(sc), jnp.zeros((n_pad, d), F32)]
        return jnp.concatenate(rows, axis=0)

    w_small = small_pack(ln_in_g, ln_in_b, ln1_g, ln1_b, b_ff2, ln2_g, ln2_b, b_ff1, pool_scale)
    m_small = small_pack(m_ln_in_g, m_ln_in_b, m_ln1_g, m_ln1_b, m_b_ff2, m_ln2_g, m_ln2_b, m_b_ff1, m_pool_scale)
    v_small = small_pack(v_ln_in_g, v_ln_in_b, v_ln1_g, v_ln1_b, v_b_ff2, v_ln2_g, v_ln2_b, v_b_ff1, v_pool_scale)
    g_small = jnp.concatenate([sums[:8 + n_f_rows], sc_row(dsc_mine), jnp.zeros((n_pad, d), F32)], axis=0)
    small = (g_small,) + tuple(_small_adamw(w_small, g_small, m_small, v_small))

    def unpack(a):
        sc = a[8 + n_f_rows, :n_groups * pr].reshape(1, n_groups, pr)
        return {"ln_in_g": a[1], "ln_in_b": a[2], "ln1_g": a[3:4], "ln1_b": a[4:5], "b_ff2": a[5:6], "ln2_g": a[6:7],
                "ln2_b": a[7:8], "b_ff1": a[8:8 + n_f_rows].reshape(1, f), "pool_scale": sc}

    shapes = {"w_in": w_in.shape, "w_out": w_out.shape, "w_ff1": w_ff1.shape, "w_ff2": w_ff2.shape, "w_pool": w_pool.shape}
    order = ["ln_in_g", "ln_in_b", "w_in", "w_pool", "pool_scale", "w_out", "ln1_g", "ln1_b", "w_ff1", "b_ff1", "w_ff2",
             "b_ff2", "ln2_g", "ln2_b"]
    outs = []
    for kind in range(4):
        small_k = unpack(small[kind])
        for nm in order:
            outs.append(big[nm][kind].reshape(shapes[nm]) if nm in big else small_k[nm])
    return (loss11.reshape(()), dx.reshape(x.shape), *outs)
```

```python
import functools
import math

import jax
import jax.numpy as jnp
from jax import lax
from jax.experimental import pallas as pl
from jax.experimental.pallas import tpu as pltpu

F32 = jnp.float32
BF16 = jnp.bfloat16
MESH = pl.DeviceIdType.MESH

N_DEV = 8
HEAD_DIM = 128
POOL_WINDOWS = (2, 4, 8, 16)
DEEPNORM_ALPHA = (2.0 * 1) ** 0.25
LN_EPS = 1e-5
ADAM_LR = 0.001
ADAM_B1 = 0.9
ADAM_B2 = 0.999
ADAM_EPS = 1e-08
ADAM_WD = 0.01
ADAM_STEP = 10

V7X_VMEM_LIMIT = 56 * 1024 * 1024
ATT_BLOCK = 256
POOL_CHUNK = 128

NN = (((1,), (0,)), ((), ()))
NT = (((1,), (1,)), ((), ()))
TN = (((0,), (0,)), ((), ()))


def _dot(a, b, dims=NN):
    return lax.dot_general(a, b, dims, preferred_element_type=F32)


def _cparams(sem=None):
    return pltpu.CompilerParams(dimension_semantics=sem, vmem_limit_bytes=V7X_VMEM_LIMIT)


def _ln_stats(r):
    mu = jnp.mean(r, axis=-1, keepdims=True)
    xc = r - mu
    var = jnp.mean(xc * xc, axis=-1, keepdims=True)
    rstd = lax.rsqrt(var + LN_EPS)
    return xc * rstd, rstd


def _ln_bwd(dy, xhat, rstd, g):
    dxh = dy * g
    m1 = jnp.mean(dxh, axis=-1, keepdims=True)
    m2 = jnp.mean(dxh * xhat, axis=-1, keepdims=True)
    dx = rstd * (dxh - m1 - xhat * m2)
    dg = jnp.sum(dy * xhat, axis=0, keepdims=True)
    db = jnp.sum(dy, axis=0, keepdims=True)
    return dx, dg, db


def _acc_rows(first, ref, val):
    @pl.when(first)
    def _():
        ref[...] = val

    @pl.when(jnp.logical_not(first))
    def _():
        ref[...] += val


def _call(body, *, name, grid, in_specs, out_specs, out_shape, inputs, scratch_shapes=(), sem=None, comm=None):
    in_specs, out_specs, out_shape, inputs = list(in_specs), list(out_specs), list(out_shape), list(inputs)
    if comm is None:
        outs = pl.pallas_call(
            body, name=name, grid=grid, in_specs=in_specs, out_specs=out_specs, out_shape=out_shape,
            scratch_shapes=list(scratch_shapes), compiler_params=_cparams(sem))(*inputs)
        return list(outs), []
    arrays, plan, n_copies = comm
    n_in, n_out, nc, n_scr = len(inputs), len(out_shape), len(arrays), len(scratch_shapes)

    def hosted(*refs):
        ins = refs[:n_in]
        outs = refs[n_in + nc:n_in + nc + n_out]
        passed = refs[n_in + nc + n_out:n_in + 2 * nc + n_out]
        scratch = refs[n_in + 2 * nc + n_out:n_in + 2 * nc + n_out + n_scr]
        send_sems, recv_sems = refs[-2], refs[-1]
        ids = [pl.program_id(ax) for ax in range(len(grid))]
        first = functools.reduce(jnp.logical_and, [i_ == 0 for i_ in ids])
        last = functools.reduce(jnp.logical_and, [i_ == g - 1 for i_, g in zip(ids, grid)])

        @pl.when(first)
        def _():
            for cp in _plan_copies(plan, passed, send_sems, recv_sems):
                cp.start()

        body(*ins, *outs, *scratch)

        @pl.when(last)
        def _():
            for cp in _plan_copies(plan, passed, send_sems, recv_sems):
                cp.wait_send()
                cp.wait_recv()

    any_spec = pl.BlockSpec(memory_space=pl.ANY)
    outs = pl.pallas_call(
        hosted, name=name, grid=grid,
        in_specs=in_specs + [any_spec] * nc, out_specs=out_specs + [any_spec] * nc,
        out_shape=out_shape + [jax.ShapeDtypeStruct(a.shape, a.dtype) for a in arrays],
        scratch_shapes=list(scratch_shapes) + [pltpu.SemaphoreType.DMA((n_copies,)), pltpu.SemaphoreType.DMA((n_copies,))],
        input_output_aliases={n_in + i: n_out + i for i in range(nc)},
        compiler_params=pltpu.CompilerParams(dimension_semantics=("arbitrary",) * len(grid),
                                             vmem_limit_bytes=V7X_VMEM_LIMIT, has_side_effects=True),
    )(*inputs, *arrays)
    return list(outs[:n_out]), list(outs[n_out:])


def _matmul(name, a, b, *, dims, grid, a_spec, b_spec, extras=(), extra_specs=(), out_shape, out_specs,
            acc_shape, epilogue, k_axis=2, sem=("parallel", "parallel", "arbitrary"), comm=None):
    nk = grid[k_axis]
    n_extra = len(extras)
    n_out = len(out_shape)

    def body(a_ref, b_ref, *rest):
        extra_refs = rest[:n_extra]
        out_refs = rest[n_extra:n_extra + n_out]
        if nk == 1:
            epilogue(_dot(a_ref[...], b_ref[...], dims), extra_refs, out_refs)
            return
        acc_ref = rest[n_extra + n_out]
        k = pl.program_id(k_axis)

        @pl.when(k == 0)
        def _():
            acc_ref[...] = jnp.zeros(acc_shape, F32)

        acc_ref[...] += _dot(a_ref[...], b_ref[...], dims)

        @pl.when(k == nk - 1)
        def _():
            epilogue(acc_ref[...], extra_refs, out_refs)

    outs, passed = _call(
        body, name=name, grid=grid, in_specs=[a_spec, b_spec, *extra_specs], out_specs=out_specs, out_shape=out_shape,
        inputs=[a, b, *extras], scratch_shapes=[] if nk == 1 else [pltpu.VMEM(acc_shape, F32)], sem=sem, comm=comm)
    return outs if comm is None else (outs, passed)


def _matmul_rows(name, a, b, *, dims, tm, nk, a_block, a_index, b_block, b_index, extras, outs, epilogue, comm=None):
    t = a.shape[0]
    d = extras[0][1].shape[1]
    m_tiles = t // tm
    n_extra, n_out = len(extras), len(outs)

    def spec(kind):
        if kind == "row":
            return pl.BlockSpec((tm, d), lambda m, k: (jnp.maximum(m - 1, 0), 0))
        return pl.BlockSpec((1, d), lambda m, k: (0, 0))

    def body(a_ref, b_ref, *rest):
        extra_refs = rest[:n_extra]
        out_refs = rest[n_extra:n_extra + n_out]
        acc_ref = rest[n_extra + n_out]
        m, k = pl.program_id(0), pl.program_id(1)
        slot = m % 2

        @pl.when(jnp.logical_and(m == 0, k == 0))
        def _():
            acc_ref[1] = jnp.zeros((tm, d), F32)

        @pl.when(k == 0)
        def _():
            acc_ref[slot] = _dot(a_ref[...], b_ref[...], dims)
            epilogue(acc_ref[1 - slot], extra_refs, out_refs, m <= 1)

        if nk > 1:
            @pl.when(k > 0)
            def _():
                acc_ref[slot] += _dot(a_ref[...], b_ref[...], dims)

    outs_, passed = _call(
        body, name=name, grid=(m_tiles + 1, nk),
        in_specs=[pl.BlockSpec(a_block, lambda m, k: a_index(jnp.minimum(m, m_tiles - 1), k)),
                  pl.BlockSpec(b_block, lambda m, k: b_index(k)), *[spec(kind) for kind, _ in extras]],
        out_specs=[spec(kind) for kind, _ in outs],
        out_shape=[jax.ShapeDtypeStruct((t, d) if kind == "row" else (1, d), dtype) for kind, dtype in outs],
        inputs=[a, b, *[x for _, x in extras]], scratch_shapes=[pltpu.VMEM((2, tm, d), F32)],
        sem=("arbitrary", "arbitrary"), comm=comm)
    return outs_ if comm is None else (outs_, passed)


def _store_epilogue(dtype):

    def ep(acc, extra_refs, out_refs):
        out_refs[0][...] = acc.astype(dtype)
    return ep


def _ln_in_fwd(x, g, b, tm):
    t, d = x.shape

    def body(x_ref, g_ref, b_ref, h_ref, hb_ref):
        xhat, _ = _ln_stats(x_ref[...])
        h = xhat * g_ref[...] + b_ref[...]
        h_ref[...] = h
        hb_ref[...] = h.astype(BF16)

    row = pl.BlockSpec((tm, d), lambda i: (i, 0))
    vec = pl.BlockSpec((1, d), lambda i: (0, 0))
    return pl.pallas_call(
        body, name="ln_in_fwd", grid=(t // tm,), in_specs=[row, vec, vec], out_specs=[row, row],
        out_shape=[jax.ShapeDtypeStruct((t, d), F32), jax.ShapeDtypeStruct((t, d), BF16)],
        compiler_params=_cparams(("parallel",)),
    )(x, g, b)


def _split3(x):
    hi = x.astype(BF16)
    r = x - hi.astype(F32)
    mid = r.astype(BF16)
    lo = (r - mid.astype(F32)).astype(BF16)
    return hi, mid, lo


def _split2(x):
    hi = x.astype(BF16)
    lo = (x - hi.astype(F32)).astype(BF16)
    return hi, lo


def _pool_fwd(u, wp, sc, t, c):
    n_groups = len(POOL_WINDOWS)
    tc = POOL_CHUNK
    n_chunks = t // tc

    def body(u_ref, wp_ref, sc_ref, y_ref, ypre_ref, xp_ref):
        g = pl.program_id(0)
        xp_ref[pl.ds(0, tc), :] = jnp.zeros((tc, c), F32)
        xp_ref[pl.ds(tc, t), :] = u_ref[...]
        out_i = lax.broadcasted_iota(jnp.int32, (tc, 2 * tc), 0)
        in_j = lax.broadcasted_iota(jnp.int32, (tc, 2 * tc), 1)
        lag = tc + out_i - in_j
        t_in_chunk = lax.broadcasted_iota(jnp.int32, (tc, 1), 0)
        for gi, w in enumerate(POOL_WINDOWS):
            @pl.when(g == gi)
            def _(w=w):
                band = jnp.logical_and(lag >= 0, lag < w).astype(BF16)

                def chunk(ci, carry):
                    start = pl.multiple_of(ci * tc, tc)
                    win = xp_ref[pl.ds(start, 2 * tc), :]
                    hi, mid, lo = _split3(win)
                    wsum = _dot(band, hi) + _dot(band, mid) + _dot(band, lo)
                    cnt = jnp.minimum(ci * tc + t_in_chunk + 1, w).astype(F32)
                    ypre = wsum * (1.0 / cnt) - win[tc:, :]
                    ypre_b = ypre.astype(BF16)
                    y = _dot(ypre_b, wp_ref[...]) * sc_ref[...]
                    ypre_ref[pl.ds(start, tc), :] = ypre_b
                    y_ref[pl.ds(start, tc), :] = y.astype(BF16)
                    return carry

                lax.fori_loop(0, n_chunks, chunk, 0)

    col = pl.BlockSpec((t, c), lambda g: (0, g))
    return pl.pallas_call(
        body, name="pool_fwd", grid=(n_groups,),
        in_specs=[col, pl.BlockSpec((None, c, c), lambda g: (g, 0, 0)), pl.BlockSpec((None, 1, c), lambda g: (g, 0, 0))],
        out_specs=[col, col],
        out_shape=[jax.ShapeDtypeStruct((t, n_groups * c), BF16), jax.ShapeDtypeStruct((t, n_groups * c), BF16)],
        scratch_shapes=[pltpu.VMEM((t + tc, c), F32)],
        compiler_params=_cparams(("parallel",)),
    )(u, wp, sc)


def _pool_bwd(dmixin, ypre, wp, sc, t, c, comm=None):
    n_groups = len(POOL_WINDOWS)
    tc = POOL_CHUNK
    n_chunks = t // tc

    def body(dy_ref, ypre_ref, wp_ref, sc_ref, du_ref, dwp_ref, dsc_ref, zp_ref):
        g = pl.program_id(0)
        zp_ref[pl.ds(t, tc), :] = jnp.zeros((tc, c), F32)
        dwp_ref[...] = jnp.zeros((c, c), F32)
        dsc_ref[...] = jnp.zeros((1, c), F32)
        out_i = lax.broadcasted_iota(jnp.int32, (tc, 2 * tc), 0)
        in_j = lax.broadcasted_iota(jnp.int32, (tc, 2 * tc), 1)
        lead = in_j - out_i
        t_in_chunk = lax.broadcasted_iota(jnp.int32, (tc, 1), 0)
        for gi, w in enumerate(POOL_WINDOWS):
            @pl.when(g == gi)
            def _(w=w):
                band = jnp.logical_and(lead >= 0, lead < w).astype(BF16)

                def first(ci, carry):
                    start = pl.multiple_of(ci * tc, tc)
                    dy = dy_ref[pl.ds(start, tc), :]
                    yp = ypre_ref[pl.ds(start, tc), :]
                    ymm = _dot(yp, wp_ref[...])
                    dsc_ref[...] += jnp.sum(dy * ymm, axis=0, keepdims=True)
                    dys_b = (dy * sc_ref[...]).astype(BF16)
                    dwp_ref[...] += _dot(yp, dys_b, TN)
                    dyp = _dot(dys_b, wp_ref[...], NT)
                    cnt = jnp.minimum(ci * tc + t_in_chunk + 1, w).astype(F32)
                    zp_ref[pl.ds(start, tc), :] = dyp * (1.0 / cnt)
                    du_ref[pl.ds(start, tc), :] = -dyp
                    return carry

                lax.fori_loop(0, n_chunks, first, 0)

                def second(ci, carry):
                    start = pl.multiple_of(ci * tc, tc)
                    hi, mid, lo = _split3(zp_ref[pl.ds(start, 2 * tc), :])
                    du_ref[pl.ds(start, tc), :] += _dot(band, hi) + _dot(band, mid) + _dot(band, lo)
                    return carry

                lax.fori_loop(0, n_chunks, second, 0)

    col = pl.BlockSpec((t, c), lambda g: (0, g))
    return _call(
        body, name="pool_bwd", grid=(n_groups,),
        in_specs=[col, col, pl.BlockSpec((None, c, c), lambda g: (g, 0, 0)), pl.BlockSpec((None, 1, c), lambda g: (g, 0, 0))],
        out_specs=[col, pl.BlockSpec((None, c, c), lambda g: (g, 0, 0)), pl.BlockSpec((None, 1, c), lambda g: (g, 0, 0))],
        out_shape=[jax.ShapeDtypeStruct((t, n_groups * c), F32), jax.ShapeDtypeStruct((n_groups, c, c), F32),
                   jax.ShapeDtypeStruct((n_groups, 1, c), F32)],
        inputs=[dmixin, ypre, wp, sc], scratch_shapes=[pltpu.VMEM((t + tc, c), F32)], sem=("parallel",), comm=comm)


ROW_PARTS = 2


def _att_consts():
    b = ATT_BLOCK
    rp = b // ROW_PARTS
    row = lax.broadcasted_iota(jnp.int32, (b, b), 0)
    col = lax.broadcasted_iota(jnp.int32, (b, b), 1)
    tri = (row >= col).astype(BF16)
    prow = lax.broadcasted_iota(jnp.int32, (rp, b), 0)
    pcol = lax.broadcasted_iota(jnp.int32, (rp, b), 1)
    causal = [pcol < prow + r * rp for r in range(ROW_PARTS)]
    return tri, causal


def _suffix_sum(x, tri):
    hi, lo = _split2(x)
    return _dot(hi, tri) + _dot(lo, tri)


LOG2_E = 1.4426950408889634


def _att_scores(qb, kb, mask):
    z2 = _dot(qb, kb, NT) * (LOG2_E / math.sqrt(HEAD_DIM))
    sp2 = jnp.maximum(z2, 0.0) + jnp.log2(1.0 + jnp.exp2(-jnp.abs(z2)))
    return z2, sp2, (sp2 if mask is None else jnp.where(mask, sp2, 0.0))


HEADS_PER_STEP = 2
ATT_LANES = HEADS_PER_STEP * HEAD_DIM


def _head_lanes(s):
    return slice(s * HEAD_DIM, (s + 1) * HEAD_DIM)


UNDERFLOW_LOG2 = 160.0


def _sweep_earlier_blocks(i, state, per_chain, block):
    def lowest(st):
        low = st[0]
        for k in range(per_chain, len(st), per_chain):
            low = jnp.minimum(low, st[k])
        return jnp.min(low)

    def more(c):
        return jnp.logical_and(c[0] < i, c[1] < UNDERFLOW_LOG2)

    def trip(c):
        st = block(i - 1 - c[0], c[2:])
        return (c[0] + 1, lowest(st)) + tuple(st)

    return lax.while_loop(more, trip, (jnp.int32(0), lowest(state)) + tuple(state))[2:]


def _attn_fwd(qkv, t, n_heads, comm=None):
    b = ATT_BLOCK
    nq = t // b
    n_steps = n_heads // HEADS_PER_STEP

    rp = b // ROW_PARTS
    chains = [(s, r) for s in range(HEADS_PER_STEP) for r in range(ROW_PARTS)]
    no_mask = [None] * ROW_PARTS

    def body(q_ref, k_ref, v_ref, o_ref):
        tri, causal = _att_consts()

        def blocks(qbs, j, state, masks):
            ks = pl.multiple_of(j * b, b)
            scores = [_att_scores(qbs[ci], k_ref[pl.ds(ks, b), _head_lanes(s)], masks[r]) for ci, (s, r) in enumerate(chains)]
            incls = [_suffix_sum(sc[2], tri) for sc in scores]
            out = []
            for ci, (s, r) in enumerate(chains):
                carry, acc = state[2 * ci], state[2 * ci + 1]
                a = jnp.exp2(scores[ci][0] - (incls[ci] + carry))
                if masks[r] is not None:
                    a = jnp.where(masks[r], a, 0.0)
                out += [carry + incls[ci][:, 0:1], acc + _dot(a.astype(BF16), v_ref[pl.ds(ks, b), _head_lanes(s)])]
            return tuple(out)

        def q_loop(i, _):
            qs = pl.multiple_of(i * b, b)
            qbs = [q_ref[pl.ds(qs + r * rp, rp), _head_lanes(s)] for s, r in chains]
            zero = (jnp.zeros((rp, 1), F32), jnp.zeros((rp, HEAD_DIM), F32)) * len(chains)
            state = blocks(qbs, i, zero, causal)
            state = _sweep_earlier_blocks(i, state, 2, lambda j, st: blocks(qbs, j, st, no_mask))
            for ci, (s, r) in enumerate(chains):
                o_ref[pl.ds(qs + r * rp, rp), _head_lanes(s)] = state[2 * ci + 1]
            return 0

        lax.fori_loop(0, nq, q_loop, 0)

    def heads(off):
        return pl.BlockSpec((t, ATT_LANES), lambda h: (0, off + h))

    return _call(
        body, name="attn_fwd", grid=(n_steps,),
        in_specs=[heads(0), heads(n_steps), heads(2 * n_steps)], out_specs=[heads(0)],
        out_shape=[jax.ShapeDtypeStruct((t, n_heads * HEAD_DIM), F32)],
        inputs=[qkv, qkv, qkv], sem=("parallel",), comm=comm)


def _attn_bwd(qkv, do, o, t, n_heads, comm=None):
    b = ATT_BLOCK
    nq = t // b
    n_steps = n_heads // HEADS_PER_STEP
    scale = 1.0 / math.sqrt(HEAD_DIM)
    rp = b // ROW_PARTS
    chains = [(s, r) for s in range(HEADS_PER_STEP) for r in range(ROW_PARTS)]
    no_mask = [None] * ROW_PARTS

    def body(q_ref, k_ref, v_ref, do_ref, o_ref, dq_ref, dk_ref, dv_ref, qt_ref, dot_ref, dkt_ref, dvt_ref):
        for j in range(nq):
            rows = pl.ds(j * b, b)
            qt_ref[j] = q_ref[rows, :].astype(F32).T.astype(BF16)
            dot_ref[j] = do_ref[rows, :].astype(F32).T.astype(BF16)
        dkt_ref[...] = jnp.zeros((nq, ATT_LANES, b), F32)
        dvt_ref[...] = jnp.zeros((nq, ATT_LANES, b), F32)
        tri, causal = _att_consts()

        def blocks(i, fixed, j, state, masks):
            ks = pl.multiple_of(j * b, b)
            n = len(chains)
            kbs = [k_ref[pl.ds(ks, b), _head_lanes(s)] for s, _ in chains]
            scores = [_att_scores(fixed[ci][0], kbs[ci], masks[r]) for ci, (s, r) in enumerate(chains)]
            incls = [_suffix_sum(sc[2], tri) for sc in scores]
            das = [_dot(fixed[ci][1], v_ref[pl.ds(ks, b), _head_lanes(s)], NT) for ci, (s, r) in enumerate(chains)]
            a_bs, gs = [], []
            for ci, (s, r) in enumerate(chains):
                a = jnp.exp2(scores[ci][0] - (incls[ci] + state[3 * ci]))
                if masks[r] is not None:
                    a = jnp.where(masks[r], a, 0.0)
                a_bs.append(a.astype(BF16))
                gs.append(a_bs[ci].astype(F32) * das[ci])
            g_incls = [_suffix_sum(g, tri) for g in gs]
            dz_bs = []
            for ci, (s, r) in enumerate(chains):
                rest = (fixed[ci][2] - state[3 * ci + 1]) - (g_incls[ci] - gs[ci])
                sig = jnp.exp2(scores[ci][0] - scores[ci][1])
                dz = (gs[ci] - sig * rest) * scale
                if masks[r] is not None:
                    dz = jnp.where(masks[r], dz, 0.0)
                dz_bs.append(dz.astype(BF16))
            out = []
            for ci in range(n):
                out += [state[3 * ci] + incls[ci][:, 0:1], state[3 * ci + 1] + g_incls[ci][:, 0:1],
                        state[3 * ci + 2] + _dot(dz_bs[ci], kbs[ci])]
            for s in range(HEADS_PER_STEP):
                lanes = _head_lanes(s)
                dk_add, dv_add = None, None
                for ci, (cs, r) in enumerate(chains):
                    if cs == s:
                        part = slice(r * rp, (r + 1) * rp)
                        dk_c = _dot(qt_ref[i, lanes, part], dz_bs[ci])
                        dv_c = _dot(dot_ref[i, lanes, part], a_bs[ci])
                        dk_add = dk_c if dk_add is None else dk_add + dk_c
                        dv_add = dv_c if dv_add is None else dv_add + dv_c
                dkt_ref[j, lanes, :] += dk_add
                dvt_ref[j, lanes, :] += dv_add
            return tuple(out)

        def q_loop(i, _):
            qs = pl.multiple_of(i * b, b)
            fixed = []
            for s, r in chains:
                rows = pl.ds(qs + r * rp, rp)
                dob = do_ref[rows, _head_lanes(s)]
                total = jnp.sum(dob.astype(F32) * o_ref[rows, _head_lanes(s)], axis=-1, keepdims=True)
                fixed.append((q_ref[rows, _head_lanes(s)], dob, total))
            zero = (jnp.zeros((rp, 1), F32), jnp.zeros((rp, 1), F32), jnp.zeros((rp, HEAD_DIM), F32)) * len(chains)
            state = blocks(i, fixed, i, zero, causal)
            state = _sweep_earlier_blocks(i, state, 3, lambda j, st: blocks(i, fixed, j, st, no_mask))
            for ci, (s, r) in enumerate(chains):
                dq_ref[pl.ds(qs + r * rp, rp), _head_lanes(s)] = state[3 * ci + 2].astype(BF16)
            return 0

        lax.fori_loop(0, nq, q_loop, 0)
        for j in range(nq):
            rows = pl.ds(j * b, b)
            dk_ref[rows, :] = dkt_ref[j].T.astype(BF16)
            dv_ref[rows, :] = dvt_ref[j].T.astype(BF16)

    def heads(off):
        return pl.BlockSpec((t, ATT_LANES), lambda h: (0, off + h))

    shape = jax.ShapeDtypeStruct((t, n_heads * HEAD_DIM), BF16)
    return _call(
        body, name="attn_bwd", grid=(n_steps,),
        in_specs=[heads(0), heads(n_steps), heads(2 * n_steps), heads(0), heads(0)],
        out_specs=[heads(0)] * 3, out_shape=[shape] * 3, inputs=[qkv, qkv, qkv, do, o],
        scratch_shapes=[pltpu.VMEM((nq, ATT_LANES, b), BF16)] * 2 + [pltpu.VMEM((nq, ATT_LANES, b), F32)] * 2,
        sem=("parallel",), comm=comm)


def _place():
    x, y, c = lax.axis_index("x"), lax.axis_index("y"), lax.axis_index("c")
    return x, y, c


def _all_gather_weights(shards):
    n = len(shards)

    def body(*refs):
        ins, outs = refs[:n], refs[n:2 * n]
        send_sems, recv_sems, local_sems = refs[2 * n:]
        x, y, c = _place()
        me, sibling = (x, y, c), (x, y, 1 - c)
        chips = [(1 - x, y), (x, 1 - y), (1 - x, 1 - y)]

        def slot(px, py, pc):
            return 4 * px + 2 * py + pc

        def copy(ti, k, block, to, src=None):
            dst = outs[ti].at[slot(*block)]
            return pltpu.make_async_remote_copy(
                src_ref=dst if src is None else src, dst_ref=dst,
                send_sem=send_sems.at[ti, k], recv_sem=recv_sems.at[ti, k], device_id=to, device_id_type=MESH)

        mine = [pltpu.make_async_copy(ins[ti], outs[ti].at[slot(*me)], local_sems.at[ti]) for ti in range(n)]
        for cp in mine:
            cp.start()
        first = []
        for ti in range(n):
            first.append(copy(ti, 0, me, sibling, src=ins[ti]))
            first += [copy(ti, 1 + j, me, (*chip, c), src=ins[ti]) for j, chip in enumerate(chips)]
        for cp in first:
            cp.start()
        passed = []
        for j, chip in enumerate(chips):
            for ti in range(n):
                copy(ti, 1 + j, (*chip, c), me).wait_recv()
                fwd = copy(ti, 4 + j, (*chip, c), sibling)
                fwd.start()
                passed.append(fwd)
        for ti in range(n):
            copy(ti, 0, sibling, me).wait_recv()
            for j, chip in enumerate(chips):
                copy(ti, 4 + j, (*chip, 1 - c), me).wait_recv()
        for cp in first + passed:
            cp.wait_send()
        for cp in mine:
            cp.wait()

    any_spec = pl.BlockSpec(memory_space=pl.ANY)
    return pl.pallas_call(
        body, name="all_gather_weights",
        in_specs=[any_spec] * n, out_specs=[any_spec] * n,
        out_shape=[jax.ShapeDtypeStruct((N_DEV, *s.shape), s.dtype) for s in shards],
        scratch_shapes=[pltpu.SemaphoreType.DMA((n, 7)), pltpu.SemaphoreType.DMA((n, 7)), pltpu.SemaphoreType.DMA((n,))],
        compiler_params=pltpu.CompilerParams(has_side_effects=True),
    )(*shards)


def _flip(v, on):
    return 1 - v if on else v


def _plan_copies(plan, refs, send_sems, recv_sems):
    return [pltpu.make_async_remote_copy(src_ref=src, dst_ref=dst, send_sem=send_sems.at[k], recv_sem=recv_sems.at[k],
                                         device_id=dev, device_id_type=MESH)
            for k, (src, dst, dev) in enumerate(plan(refs))]


def _copies_now(name, arrays, plan, n_copies):
    n = len(arrays)

    def body(*refs):
        copies = _plan_copies(plan, refs[n:2 * n], refs[2 * n], refs[2 * n + 1])
        for cp in copies:
            cp.start()
        for cp in copies:
            cp.wait_send()
            cp.wait_recv()

    any_spec = pl.BlockSpec(memory_space=pl.ANY)
    return list(pl.pallas_call(
        body, name=name, in_specs=[any_spec] * n, out_specs=[any_spec] * n,
        out_shape=[jax.ShapeDtypeStruct(a.shape, a.dtype) for a in arrays],
        input_output_aliases={i: i for i in range(n)},
        scratch_shapes=[pltpu.SemaphoreType.DMA((n_copies,)), pltpu.SemaphoreType.DMA((n_copies,))],
        compiler_params=pltpu.CompilerParams(has_side_effects=True),
    )(*arrays))


def _plan_gather_first(n_direct, n_two_level):
    def plan(refs):
        x, y, c = _place()
        me = 4 * x + 2 * y + c
        out = []
        for ti in range(n_direct + n_two_level):
            mine = refs[ti].at[me]
            for k in range(1, N_DEV):
                if ti < n_direct or k == 1 or not k & 1:
                    out.append((mine, mine, (_flip(x, k & 4), _flip(y, k & 2), _flip(c, k & 1))))
        return out
    return plan, 7 * n_direct + 4 * n_two_level


def _plan_gather_forward(n):
    def plan(refs):
        x, y, c = _place()
        out = []
        for ti in range(n):
            for r in range(1, 4):
                blk = refs[ti].at[4 * _flip(x, r & 2) + 2 * _flip(y, r & 1) + c]
                out.append((blk, blk, (x, y, 1 - c)))
        return out
    return plan, 3 * n


def _join_plans(*parts):
    def plan(refs):
        out, at = [], 0
        for part, n_arrays, _ in parts:
            out += part(refs[at:at + n_arrays])
            at += n_arrays
        return out
    return plan, sum(n_cp for _, _, n_cp in parts)


def _plan_rs_sibling(n):
    def plan(refs):
        x, y, c = _place()
        out = []
        for ti in range(n):
            for r in range(4):
                src = refs[ti].at[4 * _flip(x, r & 2) + 2 * _flip(y, r & 1) + (1 - c)]
                out.append((src, refs[n + ti].at[r], (x, y, 1 - c)))
        return out
    return plan, 4 * n


def _plan_rs_owner(n):
    def plan(refs):
        x, y, c = _place()
        out = []
        for ti in range(n):
            for r in range(1, 4):
                out.append((refs[ti].at[r], refs[n + ti].at[r], (_flip(x, r & 2), _flip(y, r & 1), c)))
        return out
    return plan, 3 * n


def _owner_slots():
    x, y, c = _place()
    idx = []
    for r in range(4):
        ox, oy = (1 - x if r & 2 else x), (1 - y if r & 1 else y)
        idx.append(4 * ox + 2 * oy + c)
    return jnp.stack(idx).astype(jnp.int32)


def _row_tile(rows, cols):
    tr = max(8, min(rows, (1 << 19) // cols))
    while rows % tr:
        tr //= 2
    return tr


def _rs_chip_sum(name, slots, partial, from_sibling):
    _, rows, cols = partial.shape
    tr = _row_tile(rows, cols)

    def body(slots_ref, p_ref, s_ref, o_ref):
        o_ref[...] = (p_ref[...] + s_ref[...]).astype(BF16)

    grid_spec = pltpu.PrefetchScalarGridSpec(
        num_scalar_prefetch=1, grid=(3, rows // tr),
        in_specs=[pl.BlockSpec((None, tr, cols), lambda r, i, s: (s[r + 1], i, 0)),
                  pl.BlockSpec((None, tr, cols), lambda r, i, s: (r + 1, i, 0))],
        out_specs=pl.BlockSpec((None, tr, cols), lambda r, i, s: (r + 1, i, 0)))
    return pl.pallas_call(
        body, name=name, grid_spec=grid_spec, out_shape=jax.ShapeDtypeStruct((4, rows, cols), BF16),
        compiler_params=_cparams(("parallel", "parallel")),
    )(slots, partial, from_sibling)


def _adamw(w, g, m, v):
    m = ADAM_B1 * m + (1.0 - ADAM_B1) * g
    v = ADAM_B2 * v + (1.0 - ADAM_B2) * (g * g)
    m_hat = m / (1.0 - ADAM_B1 ** ADAM_STEP)
    v_hat = v / (1.0 - ADAM_B2 ** ADAM_STEP)
    delta = -ADAM_LR * (m_hat / (jnp.sqrt(v_hat) + ADAM_EPS) + ADAM_WD * w)
    return delta, m, v


def _rs_final_adamw(name, slots, partial, from_sibling, from_chips, w, m, v):
    rows, cols = w.shape
    tr = _row_tile(rows, cols)

    def body(slots_ref, p_ref, s_ref, c1_ref, c2_ref, c3_ref, w_ref, m_ref, v_ref, g_ref, d_ref, nm_ref, nv_ref):
        g = p_ref[...] + s_ref[...]
        g = g + c1_ref[...].astype(F32)
        g = g + c2_ref[...].astype(F32)
        g = g + c3_ref[...].astype(F32)
        delta, nm, nv = _adamw(w_ref[...], g, m_ref[...], v_ref[...])
        g_ref[...] = g
        d_ref[...] = delta
        nm_ref[...] = nm
        nv_ref[...] = nv

    def slot(r):
        return pl.BlockSpec((None, tr, cols), lambda i, s: (r, i, 0))

    flat = pl.BlockSpec((tr, cols), lambda i, s: (i, 0))
    grid_spec = pltpu.PrefetchScalarGridSpec(
        num_scalar_prefetch=1, grid=(rows // tr,),
        in_specs=[pl.BlockSpec((None, tr, cols), lambda i, s: (s[0], i, 0)), slot(0), slot(1), slot(2), slot(3), flat, flat, flat],
        out_specs=[flat] * 4)
    return pl.pallas_call(
        body, name=name, grid_spec=grid_spec, out_shape=[jax.ShapeDtypeStruct((rows, cols), F32)] * 4,
        compiler_params=_cparams(("parallel",)),
    )(slots, partial, from_sibling, from_chips, from_chips, from_chips, w, m, v)


def _small_all_reduce(packet):
    rows, d = packet.shape

    def body(p_ref, sum_ref, loss_ref, all_ref, send_sems, recv_sems):
        x, y, c = _place()
        me = 4 * x + 2 * y + c
        all_ref[me] = p_ref[...]
        copies = []
        for k in range(1, N_DEV):
            px, py, pc = (1 - x if k & 4 else x), (1 - y if k & 2 else y), (1 - c if k & 1 else c)
            cp = pltpu.make_async_remote_copy(
                src_ref=p_ref, dst_ref=all_ref.at[me], send_sem=send_sems.at[k], recv_sem=recv_sems.at[k],
                device_id=(px, py, pc), device_id_type=MESH)
            cp.start()
            copies.append(cp)
        for cp in copies:
            cp.wait_recv()
        for cp in copies:
            cp.wait_send()
        total = all_ref[0]
        for j in range(1, N_DEV):
            total = total + all_ref[j]
        sum_ref[...] = total
        loss_ref[...] = jnp.sum(total[0:1, :], axis=-1, keepdims=True)

    vmem = pl.BlockSpec(memory_space=pltpu.VMEM)
    return pl.pallas_call(
        body, name="small_all_reduce",
        in_specs=[vmem], out_specs=[vmem, vmem],
        out_shape=[jax.ShapeDtypeStruct((rows, d), F32), jax.ShapeDtypeStruct((1, 1), F32)],
        scratch_shapes=[pltpu.VMEM((N_DEV, rows, d), F32), pltpu.SemaphoreType.DMA((N_DEV,)), pltpu.SemaphoreType.DMA((N_DEV,))],
        compiler_params=pltpu.CompilerParams(has_side_effects=True),
    )(packet)


def _small_adamw(w, g, m, v):
    def body(w_ref, g_ref, m_ref, v_ref, d_ref, nm_ref, nv_ref):
        delta, nm, nv = _adamw(w_ref[...], g_ref[...], m_ref[...], v_ref[...])
        d_ref[...] = delta
        nm_ref[...] = nm
        nv_ref[...] = nv

    vmem = pl.BlockSpec(memory_space=pltpu.VMEM)
    return pl.pallas_call(
        body, name="small_adamw", in_specs=[vmem] * 4, out_specs=[vmem] * 3,
        out_shape=[jax.ShapeDtypeStruct(w.shape, F32)] * 3,
    )(w, g, m, v)


def kernel(x, ln_in_g, ln_in_b, w_in, w_pool, pool_scale, w_out, ln1_g, ln1_b, w_ff1, b_ff1, w_ff2, b_ff2, ln2_g, ln2_b, loss_target, m_ln_in_g, m_ln_in_b, m_w_in, m_w_pool, m_pool_scale, m_w_out, m_ln1_g, m_ln1_b, m_w_ff1, m_b_ff1, m_w_ff2, m_b_ff2, m_ln2_g, m_ln2_b, v_ln_in_g, v_ln_in_b, v_w_in, v_w_pool, v_pool_scale, v_w_out, v_ln1_g, v_ln1_b, v_w_ff1, v_b_ff1, v_w_ff2, v_b_ff2, v_ln2_g, v_ln2_b):
    t, d = x.shape[1], x.shape[2]
    n_groups = len(POOL_WINDOWS)
    c_pool = w_pool.shape[3]
    p = n_groups * c_pool
    n_heads = (d - p) // HEAD_DIM
    ws_in = w_in.shape[2]
    n_in = N_DEV * ws_in
    ws_out = w_out.shape[1]
    ws_f = w_ff1.shape[2]
    f = N_DEV * ws_f
    pr = w_pool.shape[2]
    assert n_in == p + 3 * n_heads * HEAD_DIM and N_DEV * ws_out == d and N_DEV * pr == c_pool

    tm_big = min(t, 1024)
    tm_ep = min(t, 512)
    tkk = min(t, 2048)
    half_f = min(ws_f, 512)
    per_f = ws_f // half_f

    x2 = x.reshape(t, d)
    target = loss_target.reshape(t, d)
    g0, b0 = ln_in_g.reshape(1, d), ln_in_b.reshape(1, d)

    shards = [w_in.reshape(d, ws_in), w_out.reshape(ws_out, d), w_ff1.reshape(d, ws_f), w_ff2.reshape(ws_f, d),
              w_pool.reshape(n_groups * pr, c_pool)]
    x_, y_, c_ = _place()
    me = 4 * x_ + 2 * y_ + c_
    win_g, wpool_g, scale_g = _all_gather_weights(
        [shards[0].astype(BF16), shards[4].astype(BF16), pool_scale.reshape(n_groups, pr)])
    land_out, land_1, land_2 = [
        lax.dynamic_update_index_in_dim(lax.empty((N_DEV, *s.shape), BF16), s.astype(BF16), me, 0) for s in shards[1:4]]
    wp_full = wpool_g.reshape(N_DEV, n_groups, pr, c_pool).transpose(1, 0, 2, 3).reshape(n_groups, c_pool, c_pool)
    sc_full = scale_g.transpose(1, 0, 2).reshape(n_groups, 1, c_pool)

    def sds(shape, dtype=F32):
        return jax.ShapeDtypeStruct(shape, dtype)

    vec = pl.BlockSpec((1, d), lambda m, n, k: (0, 0))
    row_ep = pl.BlockSpec((tm_ep, d), lambda m, n, k: (m, 0))
    tm_res = min(t, 256)
    row_res = pl.BlockSpec((tm_res, d), lambda m, n, k: (m, 0))
    seq = ("arbitrary", "arbitrary", "arbitrary")

    h0, h0b = _ln_in_fwd(x2, g0, b0, tm_big)

    pool_shards = p // ws_in

    def mm_u(name, first, count, dtype, comm=None):
        return _matmul(
            name, h0b, win_g, dims=NN, grid=(t // tm_big, count, 1),
            a_spec=pl.BlockSpec((tm_big, d), lambda m, n, k: (m, 0)),
            b_spec=pl.BlockSpec((None, d, ws_in), lambda m, n, k: (n + first, 0, 0)),
            out_shape=[sds((t, count * ws_in), dtype)],
            out_specs=[pl.BlockSpec((tm_big, ws_in), lambda m, n, k: (m, n))],
            acc_shape=(tm_big, ws_in), epilogue=_store_epilogue(dtype), comm=comm)

    direct, two_level, forward = _plan_gather_first(1, 0), _plan_gather_first(0, 1), _plan_gather_forward(1)
    (u_pool,) = mm_u("mm_u_pool", 0, pool_shards, F32)
    (qkv,), (wout_g,) = mm_u("mm_u_qkv", pool_shards, N_DEV - pool_shards, BF16, comm=([land_out], *direct))

    y_pool, ypre = _pool_fwd(u_pool, wp_full, sc_full, t, c_pool)
    (o,), (w1_part,) = _attn_fwd(qkv, t, n_heads, comm=([land_1], *two_level))
    mixin = jnp.concatenate([y_pool, o.astype(BF16)], axis=1)
    wout_2d = wout_g.reshape(d, d)

    def ep_ln1(acc, ex, outs, first):
        h0_ref, g_ref, b_ref = ex
        r1 = DEEPNORM_ALPHA * h0_ref[...] + acc
        xhat, _ = _ln_stats(r1)
        h1 = xhat * g_ref[...] + b_ref[...]
        outs[0][...] = r1
        outs[1][...] = h1
        outs[2][...] = h1.astype(BF16)

    (r1, h1, h1b), (w1_g,) = _matmul_rows(
        "mm_mix_ln1", mixin, wout_2d, dims=NN, tm=tm_res, nk=1,
        a_block=(tm_res, d), a_index=lambda m, k: (m, 0), b_block=(d, d), b_index=lambda k: (0, 0),
        extras=[("row", h0), ("vec", ln1_g), ("vec", ln1_b)], outs=[("row", F32), ("row", F32), ("row", BF16)],
        epilogue=ep_ln1, comm=([w1_part], *forward))

    def ep_ff1(acc, ex, outs):
        f1 = acc + ex[0][...]
        outs[0][...] = f1
        r = jnp.maximum(f1, 0.0)
        outs[1][...] = (r * r).astype(BF16)

    ff_tile = pl.BlockSpec((tm_big, half_f), lambda m, n, k: (m, n))
    (f1, act), (w2_part,) = _matmul(
        "mm_ff1", h1b, w1_g, dims=NN, grid=(t // tm_big, f // half_f, 1),
        a_spec=pl.BlockSpec((tm_big, d), lambda m, n, k: (m, 0)),
        b_spec=pl.BlockSpec((None, d, half_f), lambda m, n, k: (n // per_f, 0, n % per_f)),
        extras=(b_ff1,), extra_specs=(pl.BlockSpec((1, half_f), lambda m, n, k: (0, n)),),
        out_shape=[sds((t, f)), sds((t, f), BF16)], out_specs=[ff_tile, ff_tile],
        acc_shape=(tm_big, half_f), epilogue=ep_ff1, comm=([land_2], *two_level))
    (w2_g,) = _copies_now("gather_forward_w_ff2", [w2_part], *forward)

    def ep_ln2(acc, ex, outs, first):
        h1_ref, tgt_ref, bf2_ref, g_ref, b_ref = ex
        dr2_ref, dr2b_ref, dg_ref, db_ref, dbf2_ref, loss_ref = outs
        r2 = DEEPNORM_ALPHA * h1_ref[...] + (acc + bf2_ref[...])
        xhat, rstd = _ln_stats(r2)
        err = xhat * g_ref[...] + b_ref[...] - tgt_ref[...]
        dr2, dg, db = _ln_bwd(err * (1.0 / d), xhat, rstd, g_ref[...])
        dr2_ref[...] = dr2
        dr2b_ref[...] = dr2.astype(BF16)
        _acc_rows(first, dg_ref, dg)
        _acc_rows(first, db_ref, db)
        _acc_rows(first, dbf2_ref, jnp.sum(dr2, axis=0, keepdims=True))
        _acc_rows(first, loss_ref, jnp.sum(err * err, axis=0, keepdims=True) * (0.5 / d))

    dr2, dr2b, dg2, db2, dbf2, loss_vec = _matmul(
        "mm_ff2_ln2_loss", act, w2_g, dims=NN, grid=(t // tm_ep, 1, N_DEV),
        a_spec=pl.BlockSpec((tm_ep, ws_f), lambda m, n, k: (m, k)),
        b_spec=pl.BlockSpec((None, ws_f, d), lambda m, n, k: (k, 0, 0)),
        extras=(h1, target, b_ff2, ln2_g, ln2_b), extra_specs=(row_ep, row_ep, vec, vec, vec),
        out_shape=[sds((t, d)), sds((t, d), BF16)] + [sds((1, d))] * 4, out_specs=[row_ep, row_ep, vec, vec, vec, vec],
        acc_shape=(tm_ep, d), epilogue=lambda acc, ex, outs: ep_ln2(acc, ex, outs, pl.program_id(0) == 0), sem=seq)

    def ep_dff1(acc, ex, outs):
        df1 = acc * (2.0 * jnp.maximum(ex[0][...], 0.0))
        outs[0][...] = df1.astype(BF16)
        _acc_rows(pl.program_id(1) == 0, outs[1], jnp.sum(df1, axis=0, keepdims=True))

    df_tile = pl.BlockSpec((tm_big, ws_f), lambda n, m, k: (m, n))
    df1b, dbf1 = _matmul(
        "mm_dff1", dr2b, w2_g, dims=NT, grid=(N_DEV, t // tm_big, 1),
        a_spec=pl.BlockSpec((tm_big, d), lambda n, m, k: (m, 0)),
        b_spec=pl.BlockSpec((None, ws_f, d), lambda n, m, k: (n, 0, 0)),
        extras=(f1,), extra_specs=(df_tile,),
        out_shape=[sds((t, f), BF16), sds((1, f))], out_specs=[df_tile, pl.BlockSpec((1, ws_f), lambda n, m, k: (0, n))],
        acc_shape=(tm_big, ws_f), epilogue=ep_dff1, sem=("parallel", "arbitrary", "arbitrary"))

    tn_d = min(d, 1024)
    dw2 = _matmul(
        "mm_dw2", act, dr2b, dims=TN, grid=(N_DEV, d // tn_d, t // tkk),
        a_spec=pl.BlockSpec((tkk, ws_f), lambda m, n, k: (k, m)),
        b_spec=pl.BlockSpec((tkk, tn_d), lambda m, n, k: (k, n)),
        out_shape=[sds((N_DEV, ws_f, d))], out_specs=[pl.BlockSpec((None, ws_f, tn_d), lambda m, n, k: (m, 0, n))],
        acc_shape=(ws_f, tn_d), epilogue=_store_epilogue(F32))[0]

    dw1 = _matmul(
        "mm_dw1", h1b, df1b, dims=TN, grid=(d // tn_d, N_DEV, t // tkk),
        a_spec=pl.BlockSpec((tkk, tn_d), lambda m, n, k: (k, m)),
        b_spec=pl.BlockSpec((tkk, ws_f), lambda m, n, k: (k, n)),
        out_shape=[sds((N_DEV, d, ws_f))], out_specs=[pl.BlockSpec((None, tn_d, ws_f), lambda m, n, k: (n, m, 0))],
        acc_shape=(tn_d, ws_f), epilogue=_store_epilogue(F32))[0]

    def ep_ln1_bwd(acc, ex, outs, first):
        dr2_ref, r1_ref, g_ref = ex
        xhat, rstd = _ln_stats(r1_ref[...])
        dr1, dg, db = _ln_bwd(DEEPNORM_ALPHA * dr2_ref[...] + acc, xhat, rstd, g_ref[...])
        outs[0][...] = dr1
        outs[1][...] = dr1.astype(BF16)
        _acc_rows(first, outs[2], dg)
        _acc_rows(first, outs[3], db)

    slots = _owner_slots()

    def to_sibling(parts):
        return (parts + [lax.empty((4, *pt.shape[1:]), F32) for pt in parts], *_plan_rs_sibling(len(parts)))

    def to_owner(names_, parts, from_sib):
        sums = [_rs_chip_sum("rs_chip_sum_" + nm, slots, pt, fs) for nm, pt, fs in zip(names_, parts, from_sib)]
        return (sums + [lax.empty(cs.shape, BF16) for cs in sums], *_plan_rs_owner(len(sums)))

    (dr1, dr1b, dg1, db1), (dw1, dw2, sib_1, sib_2) = _matmul(
        "mm_dh1_ln1_bwd", df1b, w1_g, dims=NT, grid=(t // tm_ep, 1, N_DEV),
        a_spec=pl.BlockSpec((tm_ep, ws_f), lambda m, n, k: (m, k)),
        b_spec=pl.BlockSpec((None, d, ws_f), lambda m, n, k: (k, 0, 0)),
        extras=(dr2, r1, ln1_g), extra_specs=(row_ep, row_ep, vec),
        out_shape=[sds((t, d)), sds((t, d), BF16), sds((1, d)), sds((1, d))], out_specs=[row_ep, row_ep, vec, vec],
        acc_shape=(tm_ep, d), epilogue=lambda acc, ex, outs: ep_ln1_bwd(acc, ex, outs, pl.program_id(0) == 0), sem=seq,
        comm=to_sibling([dw1, dw2]))
    own_1 = to_owner(["w_ff1"], [dw1], [sib_1])
    own_2 = to_owner(["w_ff2"], [dw2], [sib_2])

    dwout = _matmul(
        "mm_dwout", mixin, dr1b, dims=TN, grid=(d // tn_d, d // tn_d, t // tkk),
        a_spec=pl.BlockSpec((tkk, tn_d), lambda m, n, k: (k, m)),
        b_spec=pl.BlockSpec((tkk, tn_d), lambda m, n, k: (k, n)),
        out_shape=[sds((d, d))], out_specs=[pl.BlockSpec((tn_d, tn_d), lambda m, n, k: (m, n))],
        acc_shape=(tn_d, tn_d), epilogue=_store_epilogue(F32))[0].reshape(N_DEV, ws_out, d)

    tn_mix = min(tn_d, p, d - p)

    def mm_dmixin(name, first, width, dtype, comm=None):
        return _matmul(
            name, dr1b, wout_2d, dims=NT, grid=(t // tm_big, width // tn_mix, 1),
            a_spec=pl.BlockSpec((tm_big, d), lambda m, n, k: (m, 0)),
            b_spec=pl.BlockSpec((tn_mix, d), lambda m, n, k: (n + first // tn_mix, 0)),
            out_shape=[sds((t, width), dtype)], out_specs=[pl.BlockSpec((tm_big, tn_mix), lambda m, n, k: (m, n))],
            acc_shape=(tm_big, tn_mix), epilogue=_store_epilogue(dtype), comm=comm)

    (dy_pool,), (dwout, sib_out) = mm_dmixin("mm_dmixin_pool", 0, p, F32, comm=to_sibling([dwout]))
    (do,) = mm_dmixin("mm_dmixin_att", p, d - p, BF16)

    (du_pool, dwp, dsc), (_, chips_out) = _pool_bwd(
        dy_pool, ypre, wp_full, sc_full, t, c_pool, comm=to_owner(["w_out"], [dwout], [sib_out]))
    (dq, dk, dv), (_, chips_1) = _attn_bwd(qkv, do, o, t, n_heads, comm=own_1)
    dub = jnp.concatenate([du_pool.astype(BF16), dq, dk, dv], axis=1)

    dwin = _matmul(
        "mm_dwin", h0b, dub, dims=TN, grid=(d // tn_d, N_DEV, t // tkk),
        a_spec=pl.BlockSpec((tkk, tn_d), lambda m, n, k: (k, m)),
        b_spec=pl.BlockSpec((tkk, ws_in), lambda m, n, k: (k, n)),
        out_shape=[sds((N_DEV, d, ws_in))], out_specs=[pl.BlockSpec((None, tn_d, ws_in), lambda m, n, k: (n, m, 0))],
        acc_shape=(tn_d, ws_in), epilogue=_store_epilogue(F32))[0]
    dwp_g = dwp.reshape(n_groups, N_DEV, pr, c_pool).transpose(1, 0, 2, 3).reshape(N_DEV, n_groups * pr, c_pool)

    def ep_ln0_bwd(acc, ex, outs, first):
        dr1_ref, x_ref, g_ref = ex
        xhat, rstd = _ln_stats(x_ref[...])
        dx, dg, db = _ln_bwd(DEEPNORM_ALPHA * dr1_ref[...] + acc, xhat, rstd, g_ref[...])
        outs[0][...] = dx
        _acc_rows(first, outs[1], dg)
        _acc_rows(first, outs[2], db)

    sib_in = to_sibling([dwin, dwp_g])
    last_host = (own_2[0] + sib_in[0], *_join_plans((own_2[1], 2, own_2[2]), (sib_in[1], 4, sib_in[2])))
    (dx, dg0, db0), (_, chips_2, dwin, dwp_g, sib_in_, sib_p) = _matmul_rows(
        "mm_dh0_ln0_bwd", dub, win_g, dims=NT, tm=tm_ep, nk=N_DEV,
        a_block=(tm_ep, ws_in), a_index=lambda m, k: (m, k), b_block=(None, d, ws_in), b_index=lambda k: (k, 0, 0),
        extras=[("row", dr1), ("row", x2), ("vec", g0)], outs=[("row", F32), ("vec", F32), ("vec", F32)],
        epilogue=ep_ln0_bwd, comm=last_host)

    _, _, chips_in, chips_p = _copies_now("rs_owner_w_in", *to_owner(["w_in", "w_pool"], [dwin, dwp_g], [sib_in_, sib_p]))
    w_of = {"w_in": shards[0], "w_out": shards[1], "w_ff1": shards[2], "w_ff2": shards[3], "w_pool": shards[4]}
    mv_of = {"w_in": (m_w_in, v_w_in), "w_out": (m_w_out, v_w_out), "w_ff1": (m_w_ff1, v_w_ff1),
             "w_ff2": (m_w_ff2, v_w_ff2), "w_pool": (m_w_pool, v_w_pool)}
    big = {}
    for nm, pt, fs, fc in [("w_ff1", dw1, sib_1, chips_1), ("w_ff2", dw2, sib_2, chips_2), ("w_out", dwout, sib_out, chips_out),
                           ("w_in", dwin, sib_in_, chips_in), ("w_pool", dwp_g, sib_p, chips_p)]:
        w2d = w_of[nm]
        m_, v_ = mv_of[nm]
        big[nm] = _rs_final_adamw("rs_final_adamw_" + nm, slots, pt, fs, fc, w2d, m_.reshape(w2d.shape), v_.reshape(w2d.shape))

    n_f_rows = f // d
    pad_sc = d - p
    packet = jnp.concatenate(
        [loss_vec, dg0, db0, dg1, db1, dbf2, dg2, db2, dbf1.reshape(n_f_rows, d),
         jnp.pad(dsc.reshape(1, p), ((0, 0), (0, pad_sc)))], axis=0)
    n_rows = packet.shape[0]
    n_pad = (-n_rows) % 8
    packet = jnp.pad(packet, ((0, n_pad), (0, 0)))
    sums, loss11 = _small_all_reduce(packet)
    dsc_full = sums[8 + n_f_rows, :p].reshape(n_groups, N_DEV, pr)
    dsc_mine = lax.dynamic_index_in_dim(dsc_full, me, axis=1, keepdims=False)

    def sc_row(a):
        return jnp.pad(a.reshape(1, n_groups * pr), ((0, 0), (0, d - n_groups * pr)))

    def small_pack(ln0g, ln0b, l1g, l1b, bf2, l2g, l2b, bf1, sc):
        rows = [jnp.zeros((1, d), F32), ln0g.reshape(1, d), ln0b.reshape(1, d), l1g, l1b, bf2, l2g, l2b,
                bf1.reshape(n_f_rows, d), sc_row(sc), jnp.zeros((n_pad, d), F32)]
        return jnp.concatenate(rows, axis=0)

    w_small = small_pack(ln_in_g, ln_in_b, ln1_g, ln1_b, b_ff2, ln2_g, ln2_b, b_ff1, pool_scale)
    m_small = small_pack(m_ln_in_g, m_ln_in_b, m_ln1_g, m_ln1_b, m_b_ff2, m_ln2_g, m_ln2_b, m_b_ff1, m_pool_scale)
    v_small = small_pack(v_ln_in_g, v_ln_in_b, v_ln1_g, v_ln1_b, v_b_ff2, v_ln2_g, v_ln2_b, v_b_ff1, v_pool_scale)
    g_small = jnp.concatenate([sums[:8 + n_f_rows], sc_row(dsc_mine), jnp.zeros((n_pad, d), F32)], axis=0)
    small = (g_small,) + tuple(_small_adamw(w_small, g_small, m_small, v_small))

    def unpack(a):
        sc = a[8 + n_f_rows, :n_groups * pr].reshape(1, n_groups, pr)
        return {"ln_in_g": a[1], "ln_in_b": a[2], "ln1_g": a[3:4], "ln1_b": a[4:5], "b_ff2": a[5:6], "ln2_g": a[6:7],
                "ln2_b": a[7:8], "b_ff1": a[8:8 + n_f_rows].reshape(1, f), "pool_scale": sc}

    shapes = {"w_in": w_in.shape, "w_out": w_out.shape, "w_ff1": w_ff1.shape, "w_ff2": w_ff2.shape, "w_pool": w_pool.shape}
    order = ["ln_in_g", "ln_in_b", "w_in", "w_pool", "pool_scale", "w_out", "ln1_g", "ln1_b", "w_ff1", "b_ff1", "w_ff2",
             "b_ff2", "ln2_g", "ln2_b"]
    outs = []
    for kind in range(4):
        small_k = unpack(small[kind])
        for nm in order:
            outs.append(big[nm][kind].reshape(shapes[nm]) if nm in big else small_k[nm])
    return (loss11.reshape(()), dx.reshape(x.shape), *outs)
```

```python
import functools
import math

import jax
import jax.numpy as jnp
from jax import lax
from jax.experimental import pallas as pl
from jax.experimental.pallas import tpu as pltpu

F32 = jnp.float32
BF16 = jnp.bfloat16
MESH = pl.DeviceIdType.MESH

N_DEV = 8
HEAD_DIM = 128
POOL_WINDOWS = (2, 4, 8, 16)
DEEPNORM_ALPHA = (2.0 * 1) ** 0.25
LN_EPS = 1e-5
ADAM_LR = 0.001
ADAM_B1 = 0.9
ADAM_B2 = 0.999
ADAM_EPS = 1e-08
ADAM_WD = 0.01
ADAM_STEP = 10

V7X_VMEM_LIMIT = 56 * 1024 * 1024
ATT_BLOCK = 256
POOL_CHUNK = 128

NN = (((1,), (0,)), ((), ()))
NT = (((1,), (1,)), ((), ()))
TN = (((0,), (0,)), ((), ()))


def _dot(a, b, dims=NN):
    return lax.dot_general(a, b, dims, preferred_element_type=F32)


def _cparams(sem=None):
    return pltpu.CompilerParams(dimension_semantics=sem, vmem_limit_bytes=V7X_VMEM_LIMIT)


def _ln_stats(r):
    mu = jnp.mean(r, axis=-1, keepdims=True)
    xc = r - mu
    var = jnp.mean(xc * xc, axis=-1, keepdims=True)
    rstd = lax.rsqrt(var + LN_EPS)
    return xc * rstd, rstd


def _ln_bwd(dy, xhat, rstd, g):
    dxh = dy * g
    m1 = jnp.mean(dxh, axis=-1, keepdims=True)
    m2 = jnp.mean(dxh * xhat, axis=-1, keepdims=True)
    dx = rstd * (dxh - m1 - xhat * m2)
    dg = jnp.sum(dy * xhat, axis=0, keepdims=True)
    db = jnp.sum(dy, axis=0, keepdims=True)
    return dx, dg, db


def _acc_rows(first, ref, val):
    @pl.when(first)
    def _():
        ref[...] = val

    @pl.when(jnp.logical_not(first))
    def _():
        ref[...] += val


def _call(body, *, name, grid, in_specs, out_specs, out_shape, inputs, scratch_shapes=(), sem=None, comm=None):
    in_specs, out_specs, out_shape, inputs = list(in_specs), list(out_specs), list(out_shape), list(inputs)
    if comm is None:
        outs = pl.pallas_call(
            body, name=name, grid=grid, in_specs=in_specs, out_specs=out_specs, out_shape=out_shape,
            scratch_shapes=list(scratch_shapes), compiler_params=_cparams(sem))(*inputs)
        return list(outs), []
    arrays, plan, n_copies = comm
    n_in, n_out, nc, n_scr = len(inputs), len(out_shape), len(arrays), len(scratch_shapes)

    def hosted(*refs):
        ins = refs[:n_in]
        outs = refs[n_in + nc:n_in + nc + n_out]
        passed = refs[n_in + nc + n_out:n_in + 2 * nc + n_out]
        scratch = refs[n_in + 2 * nc + n_out:n_in + 2 * nc + n_out + n_scr]
        send_sems, recv_sems = refs[-2], refs[-1]
        ids = [pl.program_id(ax) for ax in range(len(grid))]
        first = functools.reduce(jnp.logical_and, [i_ == 0 for i_ in ids])
        last = functools.reduce(jnp.logical_and, [i_ == g - 1 for i_, g in zip(ids, grid)])

        @pl.when(first)
        def _():
            for cp in _plan_copies(plan, passed, send_sems, recv_sems):
                cp.start()

        body(*ins, *outs, *scratch)

        @pl.when(last)
        def _():
            for cp in _plan_copies(plan, passed, send_sems, recv_sems):
                cp.wait_send()
                cp.wait_recv()

    any_spec = pl.BlockSpec(memory_space=pl.ANY)
    outs = pl.pallas_call(
        hosted, name=name, grid=grid,
        in_specs=in_specs + [any_spec] * nc, out_specs=out_specs + [any_spec] * nc,
        out_shape=out_shape + [jax.ShapeDtypeStruct(a.shape, a.dtype) for a in arrays],
        scratch_shapes=list(scratch_shapes) + [pltpu.SemaphoreType.DMA((n_copies,)), pltpu.SemaphoreType.DMA((n_copies,))],
        input_output_aliases={n_in + i: n_out + i for i in range(nc)},
        compiler_params=pltpu.CompilerParams(dimension_semantics=("arbitrary",) * len(grid),
                                             vmem_limit_bytes=V7X_VMEM_LIMIT, has_side_effects=True),
    )(*inputs, *arrays)
    return list(outs[:n_out]), list(outs[n_out:])


def _matmul(name, a, b, *, dims, grid, a_spec, b_spec, extras=(), extra_specs=(), out_shape, out_specs,
            acc_shape, epilogue, k_axis=2, sem=("parallel", "parallel", "arbitrary"), comm=None):
    nk = grid[k_axis]
    n_extra = len(extras)
    n_out = len(out_shape)

    def body(a_ref, b_ref, *rest):
        extra_refs = rest[:n_extra]
        out_refs = rest[n_extra:n_extra + n_out]
        if nk == 1:
            epilogue(_dot(a_ref[...], b_ref[...], dims), extra_refs, out_refs)
            return
        acc_ref = rest[n_extra + n_out]
        k = pl.program_id(k_axis)

        @pl.when(k == 0)
        def _():
            acc_ref[...] = jnp.zeros(acc_shape, F32)

        acc_ref[...] += _dot(a_ref[...], b_ref[...], dims)

        @pl.when(k == nk - 1)
        def _():
            epilogue(acc_ref[...], extra_refs, out_refs)

    outs, passed = _call(
        body, name=name, grid=grid, in_specs=[a_spec, b_spec, *extra_specs], out_specs=out_specs, out_shape=out_shape,
        inputs=[a, b, *extras], scratch_shapes=[] if nk == 1 else [pltpu.VMEM(acc_shape, F32)], sem=sem, comm=comm)
    return outs if comm is None else (outs, passed)


def _matmul_rows(name, a, b, *, dims, tm, nk, a_block, a_index, b_block, b_index, extras, outs, epilogue, comm=None):
    t = a.shape[0]
    d = extras[0][1].shape[1]
    m_tiles = t // tm
    n_extra, n_out = len(extras), len(outs)

    def spec(kind):
        if kind == "row":
            return pl.BlockSpec((tm, d), lambda m, k: (jnp.maximum(m - 1, 0), 0))
        return pl.BlockSpec((1, d), lambda m, k: (0, 0))

    def body(a_ref, b_ref, *rest):
        extra_refs = rest[:n_extra]
        out_refs = rest[n_extra:n_extra + n_out]
        acc_ref = rest[n_extra + n_out]
        m, k = pl.program_id(0), pl.program_id(1)
        slot = m % 2

        @pl.when(jnp.logical_and(m == 0, k == 0))
        def _():
            acc_ref[1] = jnp.zeros((tm, d), F32)

        @pl.when(k == 0)
        def _():
            acc_ref[slot] = _dot(a_ref[...], b_ref[...], dims)
            epilogue(acc_ref[1 - slot], extra_refs, out_refs, m <= 1)

        if nk > 1:
            @pl.when(k > 0)
            def _():
                acc_ref[slot] += _dot(a_ref[...], b_ref[...], dims)

    outs_, passed = _call(
        body, name=name, grid=(m_tiles + 1, nk),
        in_specs=[pl.BlockSpec(a_block, lambda m, k: a_index(jnp.minimum(m, m_tiles - 1), k)),
                  pl.BlockSpec(b_block, lambda m, k: b_index(k)), *[spec(kind) for kind, _ in extras]],
        out_specs=[spec(kind) for kind, _ in outs],
        out_shape=[jax.ShapeDtypeStruct((t, d) if kind == "row" else (1, d), dtype) for kind, dtype in outs],
        inputs=[a, b, *[x for _, x in extras]], scratch_shapes=[pltpu.VMEM((2, tm, d), F32)],
        sem=("arbitrary", "arbitrary"), comm=comm)
    return outs_ if comm is None else (outs_, passed)


def _store_epilogue(dtype):

    def ep(acc, extra_refs, out_refs):
        out_refs[0][...] = acc.astype(dtype)
    return ep


def _ln_in_fwd(x, g, b, tm):
    t, d = x.shape

    def body(x_ref, g_ref, b_ref, h_ref, hb_ref):
        xhat, _ = _ln_stats(x_ref[...])
        h = xhat * g_ref[...] + b_ref[...]
        h_ref[...] = h
        hb_ref[...] = h.astype(BF16)

    row = pl.BlockSpec((tm, d), lambda i: (i, 0))
    vec = pl.BlockSpec((1, d), lambda i: (0, 0))
    return pl.pallas_call(
        body, name="ln_in_fwd", grid=(t // tm,), in_specs=[row, vec, vec], out_specs=[row, row],
        out_shape=[jax.ShapeDtypeStruct((t, d), F32), jax.ShapeDtypeStruct((t, d), BF16)],
        compiler_params=_cparams(("parallel",)),
    )(x, g, b)


def _split3(x):
    hi = x.astype(BF16)
    r = x - hi.astype(F32)
    mid = r.astype(BF16)
    lo = (r - mid.astype(F32)).astype(BF16)
    return hi, mid, lo


def _split2(x):
    hi = x.astype(BF16)
    lo = (x - hi.astype(F32)).astype(BF16)
    return hi, lo


def _pool_fwd(u, wp, sc, t, c, comm=None):
    n_groups = len(POOL_WINDOWS)
    tc = POOL_CHUNK
    n_chunks = t // tc

    def body(u_ref, wp_ref, sc_ref, y_ref, ypre_ref, xp_ref):
        g = pl.program_id(0)
        xp_ref[pl.ds(0, tc), :] = jnp.zeros((tc, c), F32)
        xp_ref[pl.ds(tc, t), :] = u_ref[...]
        out_i = lax.broadcasted_iota(jnp.int32, (tc, 2 * tc), 0)
        in_j = lax.broadcasted_iota(jnp.int32, (tc, 2 * tc), 1)
        lag = tc + out_i - in_j
        t_in_chunk = lax.broadcasted_iota(jnp.int32, (tc, 1), 0)
        for gi, w in enumerate(POOL_WINDOWS):
            @pl.when(g == gi)
            def _(w=w):
                band = jnp.logical_and(lag >= 0, lag < w).astype(BF16)

                def chunk(ci, carry):
                    start = pl.multiple_of(ci * tc, tc)
                    win = xp_ref[pl.ds(start, 2 * tc), :]
                    hi, mid, lo = _split3(win)
                    wsum = _dot(band, hi) + _dot(band, mid) + _dot(band, lo)
                    cnt = jnp.minimum(ci * tc + t_in_chunk + 1, w).astype(F32)
                    ypre = wsum * (1.0 / cnt) - win[tc:, :]
                    ypre_b = ypre.astype(BF16)
                    y = _dot(ypre_b, wp_ref[...]) * sc_ref[...]
                    ypre_ref[pl.ds(start, tc), :] = ypre_b
                    y_ref[pl.ds(start, tc), :] = y.astype(BF16)
                    return carry

                lax.fori_loop(0, n_chunks, chunk, 0)

    col = pl.BlockSpec((t, c), lambda g: (0, g))
    return _call(
        body, name="pool_fwd", grid=(n_groups,),
        in_specs=[col, pl.BlockSpec((None, c, c), lambda g: (g, 0, 0)), pl.BlockSpec((None, 1, c), lambda g: (g, 0, 0))],
        out_specs=[col, col],
        out_shape=[jax.ShapeDtypeStruct((t, n_groups * c), BF16), jax.ShapeDtypeStruct((t, n_groups * c), BF16)],
        inputs=[u, wp, sc], scratch_shapes=[pltpu.VMEM((t + tc, c), F32)], sem=("parallel",), comm=comm)


def _pool_bwd(dmixin, ypre, wp, sc, t, c, comm=None):
    n_groups = len(POOL_WINDOWS)
    tc = POOL_CHUNK
    n_chunks = t // tc

    def body(dy_ref, ypre_ref, wp_ref, sc_ref, du_ref, dwp_ref, dsc_ref, zp_ref):
        g = pl.program_id(0)
        zp_ref[pl.ds(t, tc), :] = jnp.zeros((tc, c), F32)
        dwp_ref[...] = jnp.zeros((c, c), F32)
        dsc_ref[...] = jnp.zeros((1, c), F32)
        out_i = lax.broadcasted_iota(jnp.int32, (tc, 2 * tc), 0)
        in_j = lax.broadcasted_iota(jnp.int32, (tc, 2 * tc), 1)
        lead = in_j - out_i
        t_in_chunk = lax.broadcasted_iota(jnp.int32, (tc, 1), 0)
        for gi, w in enumerate(POOL_WINDOWS):
            @pl.when(g == gi)
            def _(w=w):
                band = jnp.logical_and(lead >= 0, lead < w).astype(BF16)

                def first(ci, carry):
                    start = pl.multiple_of(ci * tc, tc)
                    dy = dy_ref[pl.ds(start, tc), :]
                    yp = ypre_ref[pl.ds(start, tc), :]
                    ymm = _dot(yp, wp_ref[...])
                    dsc_ref[...] += jnp.sum(dy * ymm, axis=0, keepdims=True)
                    dys_b = (dy * sc_ref[...]).astype(BF16)
                    dwp_ref[...] += _dot(yp, dys_b, TN)
                    dyp = _dot(dys_b, wp_ref[...], NT)
                    cnt = jnp.minimum(ci * tc + t_in_chunk + 1, w).astype(F32)
                    zp_ref[pl.ds(start, tc), :] = dyp * (1.0 / cnt)
                    du_ref[pl.ds(start, tc), :] = -dyp
                    return carry

                lax.fori_loop(0, n_chunks, first, 0)

                def second(ci, carry):
                    start = pl.multiple_of(ci * tc, tc)
                    hi, mid, lo = _split3(zp_ref[pl.ds(start, 2 * tc), :])
                    du_ref[pl.ds(start, tc), :] += _dot(band, hi) + _dot(band, mid) + _dot(band, lo)
                    return carry

                lax.fori_loop(0, n_chunks, second, 0)

    col = pl.BlockSpec((t, c), lambda g: (0, g))
    return _call(
        body, name="pool_bwd", grid=(n_groups,),
        in_specs=[col, col, pl.BlockSpec((None, c, c), lambda g: (g, 0, 0)), pl.BlockSpec((None, 1, c), lambda g: (g, 0, 0))],
        out_specs=[col, pl.BlockSpec((None, c, c), lambda g: (g, 0, 0)), pl.BlockSpec((None, 1, c), lambda g: (g, 0, 0))],
        out_shape=[jax.ShapeDtypeStruct((t, n_groups * c), F32), jax.ShapeDtypeStruct((n_groups, c, c), F32),
                   jax.ShapeDtypeStruct((n_groups, 1, c), F32)],
        inputs=[dmixin, ypre, wp, sc], scratch_shapes=[pltpu.VMEM((t + tc, c), F32)], sem=("parallel",), comm=comm)


ROW_PARTS = 2


def _att_consts():
    b = ATT_BLOCK
    rp = b // ROW_PARTS
    row = lax.broadcasted_iota(jnp.int32, (b, b), 0)
    col = lax.broadcasted_iota(jnp.int32, (b, b), 1)
    tri = (row >= col).astype(BF16)
    prow = lax.broadcasted_iota(jnp.int32, (rp, b), 0)
    pcol = lax.broadcasted_iota(jnp.int32, (rp, b), 1)
    causal = [pcol < prow + r * rp for r in range(ROW_PARTS)]
    return tri, causal


def _suffix_sum(x, tri):
    hi, lo = _split2(x)
    return _dot(hi, tri) + _dot(lo, tri)


LOG2_E = 1.4426950408889634


def _att_scores(qb, kb, mask):
    z2 = _dot(qb, kb, NT) * (LOG2_E / math.sqrt(HEAD_DIM))
    sp2 = jnp.maximum(z2, 0.0) + jnp.log2(1.0 + jnp.exp2(-jnp.abs(z2)))
    return z2, sp2, (sp2 if mask is None else jnp.where(mask, sp2, 0.0))


HEADS_PER_STEP = 2
ATT_LANES = HEADS_PER_STEP * HEAD_DIM


def _head_lanes(s):
    return slice(s * HEAD_DIM, (s + 1) * HEAD_DIM)


UNDERFLOW_LOG2 = 160.0


def _sweep_earlier_blocks(i, state, per_chain, block):
    def lowest(st):
        low = st[0]
        for k in range(per_chain, len(st), per_chain):
            low = jnp.minimum(low, st[k])
        return jnp.min(low)

    def more(c):
        return jnp.logical_and(c[0] < i, c[1] < UNDERFLOW_LOG2)

    def trip(c):
        st = block(i - 1 - c[0], c[2:])
        return (c[0] + 1, lowest(st)) + tuple(st)

    return lax.while_loop(more, trip, (jnp.int32(0), lowest(state)) + tuple(state))[2:]


def _attn_fwd(qkv, t, n_heads, comm=None):
    b = ATT_BLOCK
    nq = t // b
    n_steps = n_heads // HEADS_PER_STEP

    rp = b // ROW_PARTS
    chains = [(s, r) for s in range(HEADS_PER_STEP) for r in range(ROW_PARTS)]
    no_mask = [None] * ROW_PARTS

    def body(q_ref, k_ref, v_ref, o_ref):
        tri, causal = _att_consts()

        def blocks(qbs, j, state, masks):
            ks = pl.multiple_of(j * b, b)
            scores = [_att_scores(qbs[ci], k_ref[pl.ds(ks, b), _head_lanes(s)], masks[r]) for ci, (s, r) in enumerate(chains)]
            incls = [_suffix_sum(sc[2], tri) for sc in scores]
            out = []
            for ci, (s, r) in enumerate(chains):
                carry, acc = state[2 * ci], state[2 * ci + 1]
                a = jnp.exp2(scores[ci][0] - (incls[ci] + carry))
                if masks[r] is not None:
                    a = jnp.where(masks[r], a, 0.0)
                out += [carry + incls[ci][:, 0:1], acc + _dot(a.astype(BF16), v_ref[pl.ds(ks, b), _head_lanes(s)])]
            return tuple(out)

        def q_loop(i, _):
            qs = pl.multiple_of(i * b, b)
            qbs = [q_ref[pl.ds(qs + r * rp, rp), _head_lanes(s)] for s, r in chains]
            zero = (jnp.zeros((rp, 1), F32), jnp.zeros((rp, HEAD_DIM), F32)) * len(chains)
            state = blocks(qbs, i, zero, causal)
            state = _sweep_earlier_blocks(i, state, 2, lambda j, st: blocks(qbs, j, st, no_mask))
            for ci, (s, r) in enumerate(chains):
                o_ref[pl.ds(qs + r * rp, rp), _head_lanes(s)] = state[2 * ci + 1]
            return 0

        lax.fori_loop(0, nq, q_loop, 0)

    def heads(off):
        return pl.BlockSpec((t, ATT_LANES), lambda h: (0, off + h))

    return _call(
        body, name="attn_fwd", grid=(n_steps,),
        in_specs=[heads(0), heads(n_steps), heads(2 * n_steps)], out_specs=[heads(0)],
        out_shape=[jax.ShapeDtypeStruct((t, n_heads * HEAD_DIM), F32)],
        inputs=[qkv, qkv, qkv], sem=("parallel",), comm=comm)


def _attn_bwd(qkv, do, o, t, n_heads, comm=None):
    b = ATT_BLOCK
    nq = t // b
    n_steps = n_heads // HEADS_PER_STEP
    scale = 1.0 / math.sqrt(HEAD_DIM)
    rp = b // ROW_PARTS
    chains = [(s, r) for s in range(HEADS_PER_STEP) for r in range(ROW_PARTS)]
    no_mask = [None] * ROW_PARTS

    def body(q_ref, k_ref, v_ref, do_ref, o_ref, dq_ref, dk_ref, dv_ref, qt_ref, dot_ref, dkt_ref, dvt_ref):
        for j in range(nq):
            rows = pl.ds(j * b, b)
            qt_ref[j] = q_ref[rows, :].astype(F32).T.astype(BF16)
            dot_ref[j] = do_ref[rows, :].astype(F32).T.astype(BF16)
        dkt_ref[...] = jnp.zeros((nq, ATT_LANES, b), F32)
        dvt_ref[...] = jnp.zeros((nq, ATT_LANES, b), F32)
        tri, causal = _att_consts()

        def blocks(i, fixed, j, state, masks):
            ks = pl.multiple_of(j * b, b)
            n = len(chains)
            kbs = [k_ref[pl.ds(ks, b), _head_lanes(s)] for s, _ in chains]
            scores = [_att_scores(fixed[ci][0], kbs[ci], masks[r]) for ci, (s, r) in enumerate(chains)]
            incls = [_suffix_sum(sc[2], tri) for sc in scores]
            das = [_dot(fixed[ci][1], v_ref[pl.ds(ks, b), _head_lanes(s)], NT) for ci, (s, r) in enumerate(chains)]
            a_bs, gs = [], []
            for ci, (s, r) in enumerate(chains):
                a = jnp.exp2(scores[ci][0] - (incls[ci] + state[3 * ci]))
                if masks[r] is not None:
                    a = jnp.where(masks[r], a, 0.0)
                a_bs.append(a.astype(BF16))
                gs.append(a_bs[ci].astype(F32) * das[ci])
            g_incls = [_suffix_sum(g, tri) for g in gs]
            dz_bs = []
            for ci, (s, r) in enumerate(chains):
                rest = (fixed[ci][2] - state[3 * ci + 1]) - (g_incls[ci] - gs[ci])
                sig = jnp.exp2(scores[ci][0] - scores[ci][1])
                dz = (gs[ci] - sig * rest) * scale
                if masks[r] is not None:
                    dz = jnp.where(masks[r], dz, 0.0)
                dz_bs.append(dz.astype(BF16))
            out = []
            for ci in range(n):
                out += [state[3 * ci] + incls[ci][:, 0:1], state[3 * ci + 1] + g_incls[ci][:, 0:1],
                        state[3 * ci + 2] + _dot(dz_bs[ci], kbs[ci])]
            for s in range(HEADS_PER_STEP):
                lanes = _head_lanes(s)
                dk_add, dv_add = None, None
                for ci, (cs, r) in enumerate(chains):
                    if cs == s:
                        part = slice(r * rp, (r + 1) * rp)
                        dk_c = _dot(qt_ref[i, lanes, part], dz_bs[ci])
                        dv_c = _dot(dot_ref[i, lanes, part], a_bs[ci])
                        dk_add = dk_c if dk_add is None else dk_add + dk_c
                        dv_add = dv_c if dv_add is None else dv_add + dv_c
                dkt_ref[j, lanes, :] += dk_add
                dvt_ref[j, lanes, :] += dv_add
            return tuple(out)

        def q_loop(i, _):
            qs = pl.multiple_of(i * b, b)
            fixed = []
            for s, r in chains:
                rows = pl.ds(qs + r * rp, rp)
                dob = do_ref[rows, _head_lanes(s)]
                total = jnp.sum(dob.astype(F32) * o_ref[rows, _head_lanes(s)], axis=-1, keepdims=True)
                fixed.append((q_ref[rows, _head_lanes(s)], dob, total))
            zero = (jnp.zeros((rp, 1), F32), jnp.zeros((rp, 1), F32), jnp.zeros((rp, HEAD_DIM), F32)) * len(chains)
            state = blocks(i, fixed, i, zero, causal)
            state = _sweep_earlier_blocks(i, state, 3, lambda j, st: blocks(i, fixed, j, st, no_mask))
            for ci, (s, r) in enumerate(chains):
                dq_ref[pl.ds(qs + r * rp, rp), _head_lanes(s)] = state[3 * ci + 2].astype(BF16)
            return 0

        lax.fori_loop(0, nq, q_loop, 0)
        for j in range(nq):
            rows = pl.ds(j * b, b)
            dk_ref[rows, :] = dkt_ref[j].T.astype(BF16)
            dv_ref[rows, :] = dvt_ref[j].T.astype(BF16)

    def heads(off):
        return pl.BlockSpec((t, ATT_LANES), lambda h: (0, off + h))

    shape = jax.ShapeDtypeStruct((t, n_heads * HEAD_DIM), BF16)
    return _call(
        body, name="attn_bwd", grid=(n_steps,),
        in_specs=[heads(0), heads(n_steps), heads(2 * n_steps), heads(0), heads(0)],
        out_specs=[heads(0)] * 3, out_shape=[shape] * 3, inputs=[qkv, qkv, qkv, do, o],
        scratch_shapes=[pltpu.VMEM((nq, ATT_LANES, b), BF16)] * 2 + [pltpu.VMEM((nq, ATT_LANES, b), F32)] * 2,
        sem=("parallel",), comm=comm)


def _place():
    x, y, c = lax.axis_index("x"), lax.axis_index("y"), lax.axis_index("c")
    return x, y, c


def _all_gather_weights(shards):
    n = len(shards)

    def body(*refs):
        ins, outs = refs[:n], refs[n:2 * n]
        send_sems, recv_sems, local_sems = refs[2 * n:]
        x, y, c = _place()
        me, sibling = (x, y, c), (x, y, 1 - c)
        chips = [(1 - x, y), (x, 1 - y), (1 - x, 1 - y)]

        def slot(px, py, pc):
            return 4 * px + 2 * py + pc

        def copy(ti, k, block, to, src=None):
            dst = outs[ti].at[slot(*block)]
            return pltpu.make_async_remote_copy(
                src_ref=dst if src is None else src, dst_ref=dst,
                send_sem=send_sems.at[ti, k], recv_sem=recv_sems.at[ti, k], device_id=to, device_id_type=MESH)

        mine = [pltpu.make_async_copy(ins[ti], outs[ti].at[slot(*me)], local_sems.at[ti]) for ti in range(n)]
        for cp in mine:
            cp.start()
        first = []
        for ti in range(n):
            first.append(copy(ti, 0, me, sibling, src=ins[ti]))
            first += [copy(ti, 1 + j, me, (*chip, c), src=ins[ti]) for j, chip in enumerate(chips)]
        for cp in first:
            cp.start()
        passed = []
        for j, chip in enumerate(chips):
            for ti in range(n):
                copy(ti, 1 + j, (*chip, c), me).wait_recv()
                fwd = copy(ti, 4 + j, (*chip, c), sibling)
                fwd.start()
                passed.append(fwd)
        for ti in range(n):
            copy(ti, 0, sibling, me).wait_recv()
            for j, chip in enumerate(chips):
                copy(ti, 4 + j, (*chip, 1 - c), me).wait_recv()
        for cp in first + passed:
            cp.wait_send()
        for cp in mine:
            cp.wait()

    any_spec = pl.BlockSpec(memory_space=pl.ANY)
    return pl.pallas_call(
        body, name="all_gather_weights",
        in_specs=[any_spec] * n, out_specs=[any_spec] * n,
        out_shape=[jax.ShapeDtypeStruct((N_DEV, *s.shape), s.dtype) for s in shards],
        scratch_shapes=[pltpu.SemaphoreType.DMA((n, 7)), pltpu.SemaphoreType.DMA((n, 7)), pltpu.SemaphoreType.DMA((n,))],
        compiler_params=pltpu.CompilerParams(has_side_effects=True),
    )(*shards)


def _flip(v, on):
    return 1 - v if on else v


def _plan_copies(plan, refs, send_sems, recv_sems):
    return [pltpu.make_async_remote_copy(src_ref=src, dst_ref=dst, send_sem=send_sems.at[k], recv_sem=recv_sems.at[k],
                                         device_id=dev, device_id_type=MESH)
            for k, (src, dst, dev) in enumerate(plan(refs))]


def _copies_now(name, arrays, plan, n_copies):
    n = len(arrays)

    def body(*refs):
        copies = _plan_copies(plan, refs[n:2 * n], refs[2 * n], refs[2 * n + 1])
        for cp in copies:
            cp.start()
        for cp in copies:
            cp.wait_send()
            cp.wait_recv()

    any_spec = pl.BlockSpec(memory_space=pl.ANY)
    return list(pl.pallas_call(
        body, name=name, in_specs=[any_spec] * n, out_specs=[any_spec] * n,
        out_shape=[jax.ShapeDtypeStruct(a.shape, a.dtype) for a in arrays],
        input_output_aliases={i: i for i in range(n)},
        scratch_shapes=[pltpu.SemaphoreType.DMA((n_copies,)), pltpu.SemaphoreType.DMA((n_copies,))],
        compiler_params=pltpu.CompilerParams(has_side_effects=True),
    )(*arrays))


SIBLING, ACROSS_Y, ACROSS_X, DIAGONAL = 1, 2, 4, 6


def _plan_gather_own(peers):
    def plan(refs):
        x, y, c = _place()
        mine = refs[0].at[4 * x + 2 * y + c]
        return [(mine, mine, (_flip(x, k & 4), _flip(y, k & 2), _flip(c, k & 1))) for k in peers]
    return plan, len(peers)


def _plan_gather_forward(n):
    def plan(refs):
        x, y, c = _place()
        out = []
        for ti in range(n):
            for r in range(1, 4):
                blk = refs[ti].at[4 * _flip(x, r & 2) + 2 * _flip(y, r & 1) + c]
                out.append((blk, blk, (x, y, 1 - c)))
        return out
    return plan, 3 * n


def _join_plans(*parts):
    def plan(refs):
        out, at = [], 0
        for part, n_arrays, _ in parts:
            out += part(refs[at:at + n_arrays])
            at += n_arrays
        return out
    return plan, sum(n_cp for _, _, n_cp in parts)


def _plan_rs_sibling(n):
    def plan(refs):
        x, y, c = _place()
        out = []
        for ti in range(n):
            for r in range(4):
                src = refs[ti].at[4 * _flip(x, r & 2) + 2 * _flip(y, r & 1) + (1 - c)]
                out.append((src, refs[n + ti].at[r], (x, y, 1 - c)))
        return out
    return plan, 4 * n


def _plan_rs_owner(n, relations=(1, 2, 3)):
    def plan(refs):
        x, y, c = _place()
        out = []
        for ti in range(n):
            for r in relations:
                out.append((refs[ti].at[r], refs[n + ti].at[r], (_flip(x, r & 2), _flip(y, r & 1), c)))
        return out
    return plan, len(relations) * n


def _owner_slots():
    x, y, c = _place()
    idx = []
    for r in range(4):
        ox, oy = (1 - x if r & 2 else x), (1 - y if r & 1 else y)
        idx.append(4 * ox + 2 * oy + c)
    return jnp.stack(idx).astype(jnp.int32)


def _row_tile(rows, cols):
    tr = max(8, min(rows, (1 << 19) // cols))
    while rows % tr:
        tr //= 2
    return tr


def _rs_chip_sum(name, slots, partial, from_sibling):
    _, rows, cols = partial.shape
    tr = _row_tile(rows, cols)

    def body(slots_ref, p_ref, s_ref, o_ref):
        o_ref[...] = (p_ref[...] + s_ref[...]).astype(BF16)

    grid_spec = pltpu.PrefetchScalarGridSpec(
        num_scalar_prefetch=1, grid=(3, rows // tr),
        in_specs=[pl.BlockSpec((None, tr, cols), lambda r, i, s: (s[r + 1], i, 0)),
                  pl.BlockSpec((None, tr, cols), lambda r, i, s: (r + 1, i, 0))],
        out_specs=pl.BlockSpec((None, tr, cols), lambda r, i, s: (r + 1, i, 0)))
    return pl.pallas_call(
        body, name=name, grid_spec=grid_spec, out_shape=jax.ShapeDtypeStruct((4, rows, cols), BF16),
        compiler_params=_cparams(("parallel", "parallel")),
    )(slots, partial, from_sibling)


def _adamw(w, g, m, v):
    m = ADAM_B1 * m + (1.0 - ADAM_B1) * g
    v = ADAM_B2 * v + (1.0 - ADAM_B2) * (g * g)
    m_hat = m / (1.0 - ADAM_B1 ** ADAM_STEP)
    v_hat = v / (1.0 - ADAM_B2 ** ADAM_STEP)
    delta = -ADAM_LR * (m_hat / (jnp.sqrt(v_hat) + ADAM_EPS) + ADAM_WD * w)
    return delta, m, v


def _rs_final_adamw(name, slots, partial, from_sibling, from_chips, w, m, v):
    rows, cols = w.shape
    tr = _row_tile(rows, cols)

    def body(slots_ref, p_ref, s_ref, c1_ref, c2_ref, c3_ref, w_ref, m_ref, v_ref, g_ref, d_ref, nm_ref, nv_ref):
        g = p_ref[...] + s_ref[...]
        g = g + c1_ref[...].astype(F32)
        g = g + c2_ref[...].astype(F32)
        g = g + c3_ref[...].astype(F32)
        delta, nm, nv = _adamw(w_ref[...], g, m_ref[...], v_ref[...])
        g_ref[...] = g
        d_ref[...] = delta
        nm_ref[...] = nm
        nv_ref[...] = nv

    def slot(r):
        return pl.BlockSpec((None, tr, cols), lambda i, s: (r, i, 0))

    flat = pl.BlockSpec((tr, cols), lambda i, s: (i, 0))
    grid_spec = pltpu.PrefetchScalarGridSpec(
        num_scalar_prefetch=1, grid=(rows // tr,),
        in_specs=[pl.BlockSpec((None, tr, cols), lambda i, s: (s[0], i, 0)), slot(0), slot(1), slot(2), slot(3), flat, flat, flat],
        out_specs=[flat] * 4)
    return pl.pallas_call(
        body, name=name, grid_spec=grid_spec, out_shape=[jax.ShapeDtypeStruct((rows, cols), F32)] * 4,
        compiler_params=_cparams(("parallel",)),
    )(slots, partial, from_sibling, from_chips, from_chips, from_chips, w, m, v)


def _small_all_reduce(packet):
    rows, d = packet.shape

    def body(p_ref, sum_ref, loss_ref, all_ref, send_sems, recv_sems):
        x, y, c = _place()
        me = 4 * x + 2 * y + c
        all_ref[me] = p_ref[...]
        copies = []
        for k in range(1, N_DEV):
            px, py, pc = (1 - x if k & 4 else x), (1 - y if k & 2 else y), (1 - c if k & 1 else c)
            cp = pltpu.make_async_remote_copy(
                src_ref=p_ref, dst_ref=all_ref.at[me], send_sem=send_sems.at[k], recv_sem=recv_sems.at[k],
                device_id=(px, py, pc), device_id_type=MESH)
            cp.start()
            copies.append(cp)
        for cp in copies:
            cp.wait_recv()
        for cp in copies:
            cp.wait_send()
        total = all_ref[0]
        for j in range(1, N_DEV):
            total = total + all_ref[j]
        sum_ref[...] = total
        loss_ref[...] = jnp.sum(total[0:1, :], axis=-1, keepdims=True)

    vmem = pl.BlockSpec(memory_space=pltpu.VMEM)
    return pl.pallas_call(
        body, name="small_all_reduce",
        in_specs=[vmem], out_specs=[vmem, vmem],
        out_shape=[jax.ShapeDtypeStruct((rows, d), F32), jax.ShapeDtypeStruct((1, 1), F32)],
        scratch_shapes=[pltpu.VMEM((N_DEV, rows, d), F32), pltpu.SemaphoreType.DMA((N_DEV,)), pltpu.SemaphoreType.DMA((N_DEV,))],
        compiler_params=pltpu.CompilerParams(has_side_effects=True),
    )(packet)


def _small_adamw(w, g, m, v):
    def body(w_ref, g_ref, m_ref, v_ref, d_ref, nm_ref, nv_ref):
        delta, nm, nv = _adamw(w_ref[...], g_ref[...], m_ref[...], v_ref[...])
        d_ref[...] = delta
        nm_ref[...] = nm
        nv_ref[...] = nv

    vmem = pl.BlockSpec(memory_space=pltpu.VMEM)
    return pl.pallas_call(
        body, name="small_adamw", in_specs=[vmem] * 4, out_specs=[vmem] * 3,
        out_shape=[jax.ShapeDtypeStruct(w.shape, F32)] * 3,
    )(w, g, m, v)


def kernel(x, ln_in_g, ln_in_b, w_in, w_pool, pool_scale, w_out, ln1_g, ln1_b, w_ff1, b_ff1, w_ff2, b_ff2, ln2_g, ln2_b, loss_target, m_ln_in_g, m_ln_in_b, m_w_in, m_w_pool, m_pool_scale, m_w_out, m_ln1_g, m_ln1_b, m_w_ff1, m_b_ff1, m_w_ff2, m_b_ff2, m_ln2_g, m_ln2_b, v_ln_in_g, v_ln_in_b, v_w_in, v_w_pool, v_pool_scale, v_w_out, v_ln1_g, v_ln1_b, v_w_ff1, v_b_ff1, v_w_ff2, v_b_ff2, v_ln2_g, v_ln2_b):
    t, d = x.shape[1], x.shape[2]
    n_groups = len(POOL_WINDOWS)
    c_pool = w_pool.shape[3]
    p = n_groups * c_pool
    n_heads = (d - p) // HEAD_DIM
    ws_in = w_in.shape[2]
    n_in = N_DEV * ws_in
    ws_out = w_out.shape[1]
    ws_f = w_ff1.shape[2]
    f = N_DEV * ws_f
    pr = w_pool.shape[2]
    assert n_in == p + 3 * n_heads * HEAD_DIM and N_DEV * ws_out == d and N_DEV * pr == c_pool

    tm_big = min(t, 1024)
    tm_ep = min(t, 512)
    tkk = min(t, 2048)
    half_f = min(ws_f, 512)
    per_f = ws_f // half_f

    x2 = x.reshape(t, d)
    target = loss_target.reshape(t, d)
    g0, b0 = ln_in_g.reshape(1, d), ln_in_b.reshape(1, d)

    shards = [w_in.reshape(d, ws_in), w_out.reshape(ws_out, d), w_ff1.reshape(d, ws_f), w_ff2.reshape(ws_f, d),
              w_pool.reshape(n_groups * pr, c_pool)]
    x_, y_, c_ = _place()
    me = 4 * x_ + 2 * y_ + c_
    win_g, wpool_g, scale_g = _all_gather_weights(
        [shards[0].astype(BF16), shards[4].astype(BF16), pool_scale.reshape(n_groups, pr)])
    land_out, land_1, land_2 = [
        lax.dynamic_update_index_in_dim(lax.empty((N_DEV, *s.shape), BF16), s.astype(BF16), me, 0) for s in shards[1:4]]
    wp_full = wpool_g.reshape(N_DEV, n_groups, pr, c_pool).transpose(1, 0, 2, 3).reshape(n_groups, c_pool, c_pool)
    sc_full = scale_g.transpose(1, 0, 2).reshape(n_groups, 1, c_pool)

    def sds(shape, dtype=F32):
        return jax.ShapeDtypeStruct(shape, dtype)

    vec = pl.BlockSpec((1, d), lambda m, n, k: (0, 0))
    row_ep = pl.BlockSpec((tm_ep, d), lambda m, n, k: (m, 0))
    tm_res = min(t, 256)
    row_res = pl.BlockSpec((tm_res, d), lambda m, n, k: (m, 0))
    seq = ("arbitrary", "arbitrary", "arbitrary")

    h0, h0b = _ln_in_fwd(x2, g0, b0, tm_big)

    pool_shards = p // ws_in

    def mm_u(name, first, count, dtype, comm=None):
        return _matmul(
            name, h0b, win_g, dims=NN, grid=(t // tm_big, count, 1),
            a_spec=pl.BlockSpec((tm_big, d), lambda m, n, k: (m, 0)),
            b_spec=pl.BlockSpec((None, d, ws_in), lambda m, n, k: (n + first, 0, 0)),
            out_shape=[sds((t, count * ws_in), dtype)],
            out_specs=[pl.BlockSpec((tm_big, ws_in), lambda m, n, k: (m, n))],
            acc_shape=(tm_big, ws_in), epilogue=_store_epilogue(dtype), comm=comm)

    two_level = _plan_gather_own([SIBLING, ACROSS_Y, ACROSS_X, DIAGONAL])
    forward = _plan_gather_forward(1)
    diagonal = _plan_gather_own([DIAGONAL])
    (u_pool,) = mm_u("mm_u_pool", 0, pool_shards, F32)
    (qkv,), (wout_part,) = mm_u("mm_u_qkv", pool_shards, N_DEV - pool_shards, BF16, comm=([land_out], *two_level))

    (y_pool, ypre), (wout_g, w1_diag) = _pool_fwd(
        u_pool, wp_full, sc_full, t, c_pool,
        comm=([wout_part, land_1], *_join_plans((forward[0], 1, forward[1]), (diagonal[0], 1, diagonal[1]))))
    (o,), (w1_part,) = _attn_fwd(qkv, t, n_heads, comm=([w1_diag], *_plan_gather_own([SIBLING, ACROSS_Y, ACROSS_X])))
    mixin = jnp.concatenate([y_pool, o.astype(BF16)], axis=1)
    wout_2d = wout_g.reshape(d, d)

    def ep_ln1(acc, ex, outs, first):
        h0_ref, g_ref, b_ref = ex
        r1 = DEEPNORM_ALPHA * h0_ref[...] + acc
        xhat, _ = _ln_stats(r1)
        h1 = xhat * g_ref[...] + b_ref[...]
        outs[0][...] = r1
        outs[1][...] = h1
        outs[2][...] = h1.astype(BF16)

    (r1, h1, h1b), (w1_g,) = _matmul(
        "mm_mix_ln1", mixin, wout_2d, dims=NN, grid=(t // tm_res, 1, 1),
        a_spec=pl.BlockSpec((tm_res, d), lambda m, n, k: (m, 0)),
        b_spec=pl.BlockSpec((d, d), lambda m, n, k: (0, 0)),
        extras=(h0, ln1_g, ln1_b), extra_specs=(row_res, vec, vec),
        out_shape=[sds((t, d)), sds((t, d)), sds((t, d), BF16)], out_specs=[row_res] * 3,
        acc_shape=(tm_res, d), epilogue=lambda acc, ex, outs: ep_ln1(acc, ex, outs, None), comm=([w1_part], *forward))

    def ep_ff1(acc, ex, outs):
        f1 = acc + ex[0][...]
        outs[0][...] = f1
        r = jnp.maximum(f1, 0.0)
        outs[1][...] = (r * r).astype(BF16)

    ff_tile = pl.BlockSpec((tm_big, half_f), lambda m, n, k: (m, n))
    (f1, act), (w2_part,) = _matmul(
        "mm_ff1", h1b, w1_g, dims=NN, grid=(t // tm_big, f // half_f, 1),
        a_spec=pl.BlockSpec((tm_big, d), lambda m, n, k: (m, 0)),
        b_spec=pl.BlockSpec((None, d, half_f), lambda m, n, k: (n // per_f, 0, n % per_f)),
        extras=(b_ff1,), extra_specs=(pl.BlockSpec((1, half_f), lambda m, n, k: (0, n)),),
        out_shape=[sds((t, f)), sds((t, f), BF16)], out_specs=[ff_tile, ff_tile],
        acc_shape=(tm_big, half_f), epilogue=ep_ff1, comm=([land_2], *two_level))
    (w2_g,) = _copies_now("gather_forward_w_ff2", [w2_part], *forward)

    def ep_ln2(acc, ex, outs, first):
        h1_ref, tgt_ref, bf2_ref, g_ref, b_ref = ex
        dr2_ref, dr2b_ref, dg_ref, db_ref, dbf2_ref, loss_ref = outs
        r2 = DEEPNORM_ALPHA * h1_ref[...] + (acc + bf2_ref[...])
        xhat, rstd = _ln_stats(r2)
        err = xhat * g_ref[...] + b_ref[...] - tgt_ref[...]
        dr2, dg, db = _ln_bwd(err * (1.0 / d), xhat, rstd, g_ref[...])
        dr2_ref[...] = dr2
        dr2b_ref[...] = dr2.astype(BF16)
        _acc_rows(first, dg_ref, dg)
        _acc_rows(first, db_ref, db)
        _acc_rows(first, dbf2_ref, jnp.sum(dr2, axis=0, keepdims=True))
        _acc_rows(first, loss_ref, jnp.sum(err * err, axis=0, keepdims=True) * (0.5 / d))

    dr2, dr2b, dg2, db2, dbf2, loss_vec = _matmul(
        "mm_ff2_ln2_loss", act, w2_g, dims=NN, grid=(t // tm_ep, 1, N_DEV),
        a_spec=pl.BlockSpec((tm_ep, ws_f), lambda m, n, k: (m, k)),
        b_spec=pl.BlockSpec((None, ws_f, d), lambda m, n, k: (k, 0, 0)),
        extras=(h1, target, b_ff2, ln2_g, ln2_b), extra_specs=(row_ep, row_ep, vec, vec, vec),
        out_shape=[sds((t, d)), sds((t, d), BF16)] + [sds((1, d))] * 4, out_specs=[row_ep, row_ep, vec, vec, vec, vec],
        acc_shape=(tm_ep, d), epilogue=lambda acc, ex, outs: ep_ln2(acc, ex, outs, pl.program_id(0) == 0), sem=seq)

    def ep_dff1(acc, ex, outs):
        df1 = acc * (2.0 * jnp.maximum(ex[0][...], 0.0))
        outs[0][...] = df1.astype(BF16)
        _acc_rows(pl.program_id(1) == 0, outs[1], jnp.sum(df1, axis=0, keepdims=True))

    df_tile = pl.BlockSpec((tm_big, ws_f), lambda n, m, k: (m, n))
    df1b, dbf1 = _matmul(
        "mm_dff1", dr2b, w2_g, dims=NT, grid=(N_DEV, t // tm_big, 1),
        a_spec=pl.BlockSpec((tm_big, d), lambda n, m, k: (m, 0)),
        b_spec=pl.BlockSpec((None, ws_f, d), lambda n, m, k: (n, 0, 0)),
        extras=(f1,), extra_specs=(df_tile,),
        out_shape=[sds((t, f), BF16), sds((1, f))], out_specs=[df_tile, pl.BlockSpec((1, ws_f), lambda n, m, k: (0, n))],
        acc_shape=(tm_big, ws_f), epilogue=ep_dff1, sem=("parallel", "arbitrary", "arbitrary"))

    tn_d = min(d, 1024)
    dw2 = _matmul(
        "mm_dw2", act, dr2b, dims=TN, grid=(N_DEV, d // tn_d, t // tkk),
        a_spec=pl.BlockSpec((tkk, ws_f), lambda m, n, k: (k, m)),
        b_spec=pl.BlockSpec((tkk, tn_d), lambda m, n, k: (k, n)),
        out_shape=[sds((N_DEV, ws_f, d))], out_specs=[pl.BlockSpec((None, ws_f, tn_d), lambda m, n, k: (m, 0, n))],
        acc_shape=(ws_f, tn_d), epilogue=_store_epilogue(F32))[0]

    dw1 = _matmul(
        "mm_dw1", h1b, df1b, dims=TN, grid=(d // tn_d, N_DEV, t // tkk),
        a_spec=pl.BlockSpec((tkk, tn_d), lambda m, n, k: (k, m)),
        b_spec=pl.BlockSpec((tkk, ws_f), lambda m, n, k: (k, n)),
        out_shape=[sds((N_DEV, d, ws_f))], out_specs=[pl.BlockSpec((None, tn_d, ws_f), lambda m, n, k: (n, m, 0))],
        acc_shape=(tn_d, ws_f), epilogue=_store_epilogue(F32))[0]

    def ep_ln1_bwd(acc, ex, outs, first):
        dr2_ref, r1_ref, g_ref = ex
        xhat, rstd = _ln_stats(r1_ref[...])
        dr1, dg, db = _ln_bwd(DEEPNORM_ALPHA * dr2_ref[...] + acc, xhat, rstd, g_ref[...])
        outs[0][...] = dr1
        outs[1][...] = dr1.astype(BF16)
        _acc_rows(first, outs[2], dg)
        _acc_rows(first, outs[3], db)

    slots = _owner_slots()

    def to_sibling(parts):
        return (parts + [lax.empty((4, *pt.shape[1:]), F32) for pt in parts], *_plan_rs_sibling(len(parts)))

    def to_owner(names_, parts, from_sib):
        sums = [_rs_chip_sum("rs_chip_sum_" + nm, slots, pt, fs) for nm, pt, fs in zip(names_, parts, from_sib)]
        return (sums + [lax.empty(cs.shape, BF16) for cs in sums], *_plan_rs_owner(len(sums)))

    (dr1, dr1b, dg1, db1), (dw1, dw2, sib_1, sib_2) = _matmul(
        "mm_dh1_ln1_bwd", df1b, w1_g, dims=NT, grid=(t // tm_ep, 1, N_DEV),
        a_spec=pl.BlockSpec((tm_ep, ws_f), lambda m, n, k: (m, k)),
        b_spec=pl.BlockSpec((None, d, ws_f), lambda m, n, k: (k, 0, 0)),
        extras=(dr2, r1, ln1_g), extra_specs=(row_ep, row_ep, vec),
        out_shape=[sds((t, d)), sds((t, d), BF16), sds((1, d)), sds((1, d))], out_specs=[row_ep, row_ep, vec, vec],
        acc_shape=(tm_ep, d), epilogue=lambda acc, ex, outs: ep_ln1_bwd(acc, ex, outs, pl.program_id(0) == 0), sem=seq,
        comm=to_sibling([dw1, dw2]))
    own_1 = to_owner(["w_ff1"], [dw1], [sib_1])
    own_2 = to_owner(["w_ff2"], [dw2], [sib_2])[0]

    dwout = _matmul(
        "mm_dwout", mixin, dr1b, dims=TN, grid=(d // tn_d, d // tn_d, t // tkk),
        a_spec=pl.BlockSpec((tkk, tn_d), lambda m, n, k: (k, m)),
        b_spec=pl.BlockSpec((tkk, tn_d), lambda m, n, k: (k, n)),
        out_shape=[sds((d, d))], out_specs=[pl.BlockSpec((tn_d, tn_d), lambda m, n, k: (m, n))],
        acc_shape=(tn_d, tn_d), epilogue=_store_epilogue(F32))[0].reshape(N_DEV, ws_out, d)

    tn_mix = min(tn_d, p, d - p)

    def mm_dmixin(name, first, width, dtype, comm=None):
        return _matmul(
            name, dr1b, wout_2d, dims=NT, grid=(t // tm_big, width // tn_mix, 1),
            a_spec=pl.BlockSpec((tm_big, d), lambda m, n, k: (m, 0)),
            b_spec=pl.BlockSpec((tn_mix, d), lambda m, n, k: (n + first // tn_mix, 0)),
            out_shape=[sds((t, width), dtype)], out_specs=[pl.BlockSpec((tm_big, tn_mix), lambda m, n, k: (m, n))],
            acc_shape=(tm_big, tn_mix), epilogue=_store_epilogue(dtype), comm=comm)

    (dy_pool,), (dwout, sib_out) = mm_dmixin("mm_dmixin_pool", 0, p, F32, comm=to_sibling([dwout]))
    (do,) = mm_dmixin("mm_dmixin_att", p, d - p, BF16)

    (du_pool, dwp, dsc), (_, chips_out) = _pool_bwd(
        dy_pool, ypre, wp_full, sc_full, t, c_pool, comm=to_owner(["w_out"], [dwout], [sib_out]))
    (dq, dk, dv), (_, chips_1) = _attn_bwd(qkv, do, o, t, n_heads, comm=own_1)
    dub = jnp.concatenate([du_pool.astype(BF16), dq, dk, dv], axis=1)

    (dwin,), own_2 = _matmul(
        "mm_dwin", h0b, dub, dims=TN, grid=(d // tn_d, N_DEV, t // tkk),
        a_spec=pl.BlockSpec((tkk, tn_d), lambda m, n, k: (k, m)),
        b_spec=pl.BlockSpec((tkk, ws_in), lambda m, n, k: (k, n)),
        out_shape=[sds((N_DEV, d, ws_in))], out_specs=[pl.BlockSpec((None, tn_d, ws_in), lambda m, n, k: (n, m, 0))],
        acc_shape=(tn_d, ws_in), epilogue=_store_epilogue(F32), comm=(own_2, *_plan_rs_owner(1, (1, 2))))
    dwp_g = dwp.reshape(n_groups, N_DEV, pr, c_pool).transpose(1, 0, 2, 3).reshape(N_DEV, n_groups * pr, c_pool)

    def ep_ln0_bwd(acc, ex, outs, first):
        dr1_ref, x_ref, g_ref = ex
        xhat, rstd = _ln_stats(x_ref[...])
        dx, dg, db = _ln_bwd(DEEPNORM_ALPHA * dr1_ref[...] + acc, xhat, rstd, g_ref[...])
        outs[0][...] = dx
        _acc_rows(first, outs[1], dg)
        _acc_rows(first, outs[2], db)

    sib_in = to_sibling([dwin, dwp_g])
    to_diagonal = _plan_rs_owner(1, (3,))
    last_host = (own_2 + sib_in[0], *_join_plans((to_diagonal[0], 2, to_diagonal[1]), (sib_in[1], 4, sib_in[2])))
    (dx, dg0, db0), (_, chips_2, dwin, dwp_g, sib_in_, sib_p) = _matmul_rows(
        "mm_dh0_ln0_bwd", dub, win_g, dims=NT, tm=tm_ep, nk=N_DEV,
        a_block=(tm_ep, ws_in), a_index=lambda m, k: (m, k), b_block=(None, d, ws_in), b_index=lambda k: (k, 0, 0),
        extras=[("row", dr1), ("row", x2), ("vec", g0)], outs=[("row", F32), ("vec", F32), ("vec", F32)],
        epilogue=ep_ln0_bwd, comm=last_host)

    _, _, chips_in, chips_p = _copies_now("rs_owner_w_in", *to_owner(["w_in", "w_pool"], [dwin, dwp_g], [sib_in_, sib_p]))
    w_of = {"w_in": shards[0], "w_out": shards[1], "w_ff1": shards[2], "w_ff2": shards[3], "w_pool": shards[4]}
    mv_of = {"w_in": (m_w_in, v_w_in), "w_out": (m_w_out, v_w_out), "w_ff1": (m_w_ff1, v_w_ff1),
             "w_ff2": (m_w_ff2, v_w_ff2), "w_pool": (m_w_pool, v_w_pool)}
    big = {}
    for nm, pt, fs, fc in [("w_ff1", dw1, sib_1, chips_1), ("w_ff2", dw2, sib_2, chips_2), ("w_out", dwout, sib_out, chips_out),
                           ("w_in", dwin, sib_in_, chips_in), ("w_pool", dwp_g, sib_p, chips_p)]:
        w2d = w_of[nm]
        m_, v_ = mv_of[nm]
        big[nm] = _rs_final_adamw("rs_final_adamw_" + nm, slots, pt, fs, fc, w2d, m_.reshape(w2d.shape), v_.reshape(w2d.shape))

    n_f_rows = f // d
    pad_sc = d - p
    packet = jnp.concatenate(
        [loss_vec, dg0, db0, dg1, db1, dbf2, dg2, db2, dbf1.reshape(n_f_rows, d),
         jnp.pad(dsc.reshape(1, p), ((0, 0), (0, pad_sc)))], axis=0)
    n_rows = packet.shape[0]
    n_pad = (-n_rows) % 8
    packet = jnp.pad(packet, ((0, n_pad), (0, 0)))
    sums, loss11 = _small_all_reduce(packet)
    dsc_full = sums[8 + n_f_rows, :p].reshape(n_groups, N_DEV, pr)
    dsc_mine = lax.dynamic_index_in_dim(dsc_full, me, axis=1, keepdims=False)

    def sc_row(a):
        return jnp.pad(a.reshape(1, n_groups * pr), ((0, 0), (0, d - n_groups * pr)))

    def small_pack(ln0g, ln0b, l1g, l1b, bf2, l2g, l2b, bf1, sc):
        rows = [jnp.zeros((1, d), F32), ln0g.reshape(1, d), ln0b.reshape(1, d), l1g, l1b, bf2, l2g, l2b,
                bf1.reshape(n_f_rows, d), sc_row(sc), jnp.zeros((n_pad, d), F32)]
        return jnp.concatenate(rows, axis=0)

    w_small = small_pack(ln_in_g, ln_in_b, ln1_g, ln1_b, b_ff2, ln2_g, ln2_b, b_ff1, pool_scale)
    m_small = small_pack(m_ln_in_g, m_ln_in_b, m_ln1_g, m_ln1_b, m_b_ff2, m_ln2_g, m_ln2_b, m_b_ff1, m_pool_scale)
    v_small = small_pack(v_ln_in_g, v_ln_in_b, v_ln1_g, v_ln1_b, v_b_ff2, v_ln2_g, v_ln2_b, v_b_ff1, v_pool_scale)
    g_small = jnp.concatenate([sums[:8 + n_f_rows], sc_row(dsc_mine), jnp.zeros((n_pad, d), F32)], axis=0)
    small = (g_small,) + tuple(_small_adamw(w_small, g_small, m_small, v_small))

    def unpack(a):
        sc = a[8 + n_f_rows, :n_groups * pr].reshape(1, n_groups, pr)
        return {"ln_in_g": a[1], "ln_in_b": a[2], "ln1_g": a[3:4], "ln1_b": a[4:5], "b_ff2": a[5:6], "ln2_g": a[6:7],
                "ln2_b": a[7:8], "b_ff1": a[8:8 + n_f_rows].reshape(1, f), "pool_scale": sc}

    shapes = {"w_in": w_in.shape, "w_out": w_out.shape, "w_ff1": w_ff1.shape, "w_ff2": w_ff2.shape, "w_pool": w_pool.shape}
    order = ["ln_in_g", "ln_in_b", "w_in", "w_pool", "pool_scale", "w_out", "ln1_g", "ln1_b", "w_ff1", "b_ff1", "w_ff2",
             "b_ff2", "ln2_g", "ln2_b"]
    outs = []
    for kind in range(4):
        small_k = unpack(small[kind])
        for nm in order:
            outs.append(big[nm][kind].reshape(shapes[nm]) if nm in big else small_k[nm])
    return (loss11.reshape(()), dx.reshape(x.shape), *outs)
```

```python
import functools
import math

import jax
import jax.numpy as jnp
from jax import lax
from jax.experimental import pallas as pl
from jax.experimental.pallas import tpu as pltpu

F32 = jnp.float32
BF16 = jnp.bfloat16
MESH = pl.DeviceIdType.MESH

N_DEV = 8
HEAD_DIM = 128
POOL_WINDOWS = (2, 4, 8, 16)
DEEPNORM_ALPHA = (2.0 * 1) ** 0.25
LN_EPS = 1e-5
ADAM_LR = 0.001
ADAM_B1 = 0.9
ADAM_B2 = 0.999
ADAM_EPS = 1e-08
ADAM_WD = 0.01
ADAM_STEP = 10

V7X_VMEM_LIMIT = 56 * 1024 * 1024
ATT_BLOCK = 256
POOL_CHUNK = 128

NN = (((1,), (0,)), ((), ()))
NT = (((1,), (1,)), ((), ()))
TN = (((0,), (0,)), ((), ()))


def _dot(a, b, dims=NN):
    return lax.dot_general(a, b, dims, preferred_element_type=F32)


def _cparams(sem=None):
    return pltpu.CompilerParams(dimension_semantics=sem, vmem_limit_bytes=V7X_VMEM_LIMIT)


def _ln_stats(r):
    mu = jnp.mean(r, axis=-1, keepdims=True)
    xc = r - mu
    var = jnp.mean(xc * xc, axis=-1, keepdims=True)
    rstd = lax.rsqrt(var + LN_EPS)
    return xc * rstd, rstd


def _ln_bwd(dy, xhat, rstd, g):
    dxh = dy * g
    m1 = jnp.mean(dxh, axis=-1, keepdims=True)
    m2 = jnp.mean(dxh * xhat, axis=-1, keepdims=True)
    dx = rstd * (dxh - m1 - xhat * m2)
    dg = jnp.sum(dy * xhat, axis=0, keepdims=True)
    db = jnp.sum(dy, axis=0, keepdims=True)
    return dx, dg, db


def _acc_rows(first, ref, val):
    @pl.when(first)
    def _():
        ref[...] = val

    @pl.when(jnp.logical_not(first))
    def _():
        ref[...] += val


def _call(body, *, name, grid, in_specs, out_specs, out_shape, inputs, scratch_shapes=(), sem=None, comm=None):
    in_specs, out_specs, out_shape, inputs = list(in_specs), list(out_specs), list(out_shape), list(inputs)
    if comm is None:
        outs = pl.pallas_call(
            body, name=name, grid=grid, in_specs=in_specs, out_specs=out_specs, out_shape=out_shape,
            scratch_shapes=list(scratch_shapes), compiler_params=_cparams(sem))(*inputs)
        return list(outs), []
    arrays, plan, n_copies = comm
    n_in, n_out, nc, n_scr = len(inputs), len(out_shape), len(arrays), len(scratch_shapes)

    def hosted(*refs):
        ins = refs[:n_in]
        outs = refs[n_in + nc:n_in + nc + n_out]
        passed = refs[n_in + nc + n_out:n_in + 2 * nc + n_out]
        scratch = refs[n_in + 2 * nc + n_out:n_in + 2 * nc + n_out + n_scr]
        send_sems, recv_sems = refs[-2], refs[-1]
        ids = [pl.program_id(ax) for ax in range(len(grid))]
        first = functools.reduce(jnp.logical_and, [i_ == 0 for i_ in ids])
        last = functools.reduce(jnp.logical_and, [i_ == g - 1 for i_, g in zip(ids, grid)])

        @pl.when(first)
        def _():
            for cp in _plan_copies(plan, passed, send_sems, recv_sems):
                cp.start()

        body(*ins, *outs, *scratch)

        @pl.when(last)
        def _():
            for cp in _plan_copies(plan, passed, send_sems, recv_sems):
                cp.wait_send()
                cp.wait_recv()

    any_spec = pl.BlockSpec(memory_space=pl.ANY)
    outs = pl.pallas_call(
        hosted, name=name, grid=grid,
        in_specs=in_specs + [any_spec] * nc, out_specs=out_specs + [any_spec] * nc,
        out_shape=out_shape + [jax.ShapeDtypeStruct(a.shape, a.dtype) for a in arrays],
        scratch_shapes=list(scratch_shapes) + [pltpu.SemaphoreType.DMA((n_copies,)), pltpu.SemaphoreType.DMA((n_copies,))],
        input_output_aliases={n_in + i: n_out + i for i in range(nc)},
        compiler_params=pltpu.CompilerParams(dimension_semantics=("arbitrary",) * len(grid),
                                             vmem_limit_bytes=V7X_VMEM_LIMIT, has_side_effects=True),
    )(*inputs, *arrays)
    return list(outs[:n_out]), list(outs[n_out:])


def _matmul(name, a, b, *, dims, grid, a_spec, b_spec, extras=(), extra_specs=(), out_shape, out_specs,
            acc_shape, epilogue, k_axis=2, sem=("parallel", "parallel", "arbitrary"), comm=None, dot_fn=None):
    nk = grid[k_axis]
    n_extra = len(extras)
    n_out = len(out_shape)
    if dot_fn is None:
        def dot_fn(a_ref, b_ref):
            return _dot(a_ref[...], b_ref[...], dims)

    def body(a_ref, b_ref, *rest):
        extra_refs = rest[:n_extra]
        out_refs = rest[n_extra:n_extra + n_out]
        if nk == 1:
            epilogue(dot_fn(a_ref, b_ref), extra_refs, out_refs)
            return
        acc_ref = rest[n_extra + n_out]
        k = pl.program_id(k_axis)

        @pl.when(k == 0)
        def _():
            acc_ref[...] = jnp.zeros(acc_shape, F32)

        acc_ref[...] += dot_fn(a_ref, b_ref)

        @pl.when(k == nk - 1)
        def _():
            epilogue(acc_ref[...], extra_refs, out_refs)

    outs, passed = _call(
        body, name=name, grid=grid, in_specs=[a_spec, b_spec, *extra_specs], out_specs=out_specs, out_shape=out_shape,
        inputs=[a, b, *extras], scratch_shapes=[] if nk == 1 else [pltpu.VMEM(acc_shape, F32)], sem=sem, comm=comm)
    return outs if comm is None else (outs, passed)


def _store_epilogue(dtype):

    def ep(acc, extra_refs, out_refs):
        out_refs[0][...] = acc.astype(dtype)
    return ep


def _ln_in_fwd(x, g, b, tm):
    t, d = x.shape

    def body(x_ref, g_ref, b_ref, h_ref, hb_ref):
        xhat, _ = _ln_stats(x_ref[...])
        h = xhat * g_ref[...] + b_ref[...]
        h_ref[...] = h
        hb_ref[...] = h.astype(BF16)

    row = pl.BlockSpec((tm, d), lambda i: (i, 0))
    vec = pl.BlockSpec((1, d), lambda i: (0, 0))
    return pl.pallas_call(
        body, name="ln_in_fwd", grid=(t // tm,), in_specs=[row, vec, vec], out_specs=[row, row],
        out_shape=[jax.ShapeDtypeStruct((t, d), F32), jax.ShapeDtypeStruct((t, d), BF16)],
        compiler_params=_cparams(("parallel",)),
    )(x, g, b)


def _split3(x):
    hi = x.astype(BF16)
    r = x - hi.astype(F32)
    mid = r.astype(BF16)
    lo = (r - mid.astype(F32)).astype(BF16)
    return hi, mid, lo


def _split2(x):
    hi = x.astype(BF16)
    lo = (x - hi.astype(F32)).astype(BF16)
    return hi, lo


def _pool_fwd(u, wp, sc, t, c, comm=None):
    n_groups = len(POOL_WINDOWS)
    tc = POOL_CHUNK
    n_chunks = t // tc

    def body(u_ref, wp_ref, sc_ref, y_ref, ypre_ref, xp_ref):
        g = pl.program_id(0)
        xp_ref[pl.ds(0, tc), :] = jnp.zeros((tc, c), F32)
        xp_ref[pl.ds(tc, t), :] = u_ref[...]
        out_i = lax.broadcasted_iota(jnp.int32, (tc, 2 * tc), 0)
        in_j = lax.broadcasted_iota(jnp.int32, (tc, 2 * tc), 1)
        lag = tc + out_i - in_j
        t_in_chunk = lax.broadcasted_iota(jnp.int32, (tc, 1), 0)
        for gi, w in enumerate(POOL_WINDOWS):
            @pl.when(g == gi)
            def _(w=w):
                band = jnp.logical_and(lag >= 0, lag < w).astype(BF16)

                def chunk(ci, carry):
                    start = pl.multiple_of(ci * tc, tc)
                    win = xp_ref[pl.ds(start, 2 * tc), :]
                    hi, mid, lo = _split3(win)
                    wsum = _dot(band, hi) + _dot(band, mid) + _dot(band, lo)
                    cnt = jnp.minimum(ci * tc + t_in_chunk + 1, w).astype(F32)
                    ypre = wsum * (1.0 / cnt) - win[tc:, :]
                    ypre_b = ypre.astype(BF16)
                    y = _dot(ypre_b, wp_ref[...]) * sc_ref[...]
                    ypre_ref[pl.ds(start, tc), :] = ypre_b
                    y_ref[pl.ds(start, tc), :] = y.astype(BF16)
                    return carry

                lax.fori_loop(0, n_chunks, chunk, 0)

    col = pl.BlockSpec((t, c), lambda g: (0, g))
    return _call(
        body, name="pool_fwd", grid=(n_groups,),
        in_specs=[col, pl.BlockSpec((None, c, c), lambda g: (g, 0, 0)), pl.BlockSpec((None, 1, c), lambda g: (g, 0, 0))],
        out_specs=[col, col],
        out_shape=[jax.ShapeDtypeStruct((t, n_groups * c), BF16), jax.ShapeDtypeStruct((t, n_groups * c), BF16)],
        inputs=[u, wp, sc], scratch_shapes=[pltpu.VMEM((t + tc, c), F32)], sem=("parallel",), comm=comm)


def _pool_bwd(dmixin, ypre, wp, sc, t, c, comm=None):
    n_groups = len(POOL_WINDOWS)
    tc = POOL_CHUNK
    n_chunks = t // tc

    def body(dy_ref, ypre_ref, wp_ref, sc_ref, du_ref, dwp_ref, dsc_ref, zp_ref):
        g = pl.program_id(0)
        zp_ref[pl.ds(t, tc), :] = jnp.zeros((tc, c), F32)
        dwp_ref[...] = jnp.zeros((c, c), F32)
        dsc_ref[...] = jnp.zeros((1, c), F32)
        out_i = lax.broadcasted_iota(jnp.int32, (tc, 2 * tc), 0)
        in_j = lax.broadcasted_iota(jnp.int32, (tc, 2 * tc), 1)
        lead = in_j - out_i
        t_in_chunk = lax.broadcasted_iota(jnp.int32, (tc, 1), 0)
        for gi, w in enumerate(POOL_WINDOWS):
            @pl.when(g == gi)
            def _(w=w):
                band = jnp.logical_and(lead >= 0, lead < w).astype(BF16)

                def first(ci, carry):
                    start = pl.multiple_of(ci * tc, tc)
                    dy = dy_ref[pl.ds(start, tc), :]
                    yp = ypre_ref[pl.ds(start, tc), :]
                    ymm = _dot(yp, wp_ref[...])
                    dsc_ref[...] += jnp.sum(dy * ymm, axis=0, keepdims=True)
                    dys_b = (dy * sc_ref[...]).astype(BF16)
                    dwp_ref[...] += _dot(yp, dys_b, TN)
                    dyp = _dot(dys_b, wp_ref[...], NT)
                    cnt = jnp.minimum(ci * tc + t_in_chunk + 1, w).astype(F32)
                    zp_ref[pl.ds(start, tc), :] = dyp * (1.0 / cnt)
                    du_ref[pl.ds(start, tc), :] = -dyp
                    return carry

                lax.fori_loop(0, n_chunks, first, 0)

                def second(ci, carry):
                    start = pl.multiple_of(ci * tc, tc)
                    hi, mid, lo = _split3(zp_ref[pl.ds(start, 2 * tc), :])
                    du_ref[pl.ds(start, tc), :] += _dot(band, hi) + _dot(band, mid) + _dot(band, lo)
                    return carry

                lax.fori_loop(0, n_chunks, second, 0)

    col = pl.BlockSpec((t, c), lambda g: (0, g))
    return _call(
        body, name="pool_bwd", grid=(n_groups,),
        in_specs=[col, col, pl.BlockSpec((None, c, c), lambda g: (g, 0, 0)), pl.BlockSpec((None, 1, c), lambda g: (g, 0, 0))],
        out_specs=[col, pl.BlockSpec((None, c, c), lambda g: (g, 0, 0)), pl.BlockSpec((None, 1, c), lambda g: (g, 0, 0))],
        out_shape=[jax.ShapeDtypeStruct((t, n_groups * c), F32), jax.ShapeDtypeStruct((n_groups, c, c), F32),
                   jax.ShapeDtypeStruct((n_groups, 1, c), F32)],
        inputs=[dmixin, ypre, wp, sc], scratch_shapes=[pltpu.VMEM((t + tc, c), F32)], sem=("parallel",), comm=comm)


ROW_PARTS = 2


def _att_consts():
    b = ATT_BLOCK
    rp = b // ROW_PARTS
    row = lax.broadcasted_iota(jnp.int32, (b, b), 0)
    col = lax.broadcasted_iota(jnp.int32, (b, b), 1)
    tri = (row >= col).astype(BF16)
    prow = lax.broadcasted_iota(jnp.int32, (rp, b), 0)
    pcol = lax.broadcasted_iota(jnp.int32, (rp, b), 1)
    causal = [pcol < prow + r * rp for r in range(ROW_PARTS)]
    return tri, causal


def _suffix_sum(x, tri):
    hi, lo = _split2(x)
    return _dot(hi, tri) + _dot(lo, tri)


LOG2_E = 1.4426950408889634


def _att_scores(qb, kb, mask):
    z2 = _dot(qb, kb, NT) * (LOG2_E / math.sqrt(HEAD_DIM))
    sp2 = jnp.maximum(z2, 0.0) + jnp.log2(1.0 + jnp.exp2(-jnp.abs(z2)))
    return z2, sp2, (sp2 if mask is None else jnp.where(mask, sp2, 0.0))


HEADS_PER_STEP = 2
ATT_LANES = HEADS_PER_STEP * HEAD_DIM


def _head_lanes(s):
    return slice(s * HEAD_DIM, (s + 1) * HEAD_DIM)


UNDERFLOW_LOG2 = 160.0


def _sweep_earlier_blocks(i, state, per_chain, block):
    def lowest(st):
        low = st[0]
        for k in range(per_chain, len(st), per_chain):
            low = jnp.minimum(low, st[k])
        return jnp.min(low)

    def more(c):
        return jnp.logical_and(c[0] < i, c[1] < UNDERFLOW_LOG2)

    def trip(c):
        st = block(i - 1 - c[0], c[2:])
        return (c[0] + 1, lowest(st)) + tuple(st)

    return lax.while_loop(more, trip, (jnp.int32(0), lowest(state)) + tuple(state))[2:]


def _attn_fwd(qkv, t, n_heads, comm=None):
    b = ATT_BLOCK
    nq = t // b
    n_steps = n_heads // HEADS_PER_STEP

    rp = b // ROW_PARTS
    chains = [(s, r) for s in range(HEADS_PER_STEP) for r in range(ROW_PARTS)]
    no_mask = [None] * ROW_PARTS

    def body(q_ref, k_ref, v_ref, o_ref):
        tri, causal = _att_consts()

        def blocks(qbs, j, state, masks):
            ks = pl.multiple_of(j * b, b)
            scores = [_att_scores(qbs[ci], k_ref[pl.ds(ks, b), _head_lanes(s)], masks[r]) for ci, (s, r) in enumerate(chains)]
            incls = [_suffix_sum(sc[2], tri) for sc in scores]
            out = []
            for ci, (s, r) in enumerate(chains):
                carry, acc = state[2 * ci], state[2 * ci + 1]
                a = jnp.exp2(scores[ci][0] - (incls[ci] + carry))
                if masks[r] is not None:
                    a = jnp.where(masks[r], a, 0.0)
                out += [carry + incls[ci][:, 0:1], acc + _dot(a.astype(BF16), v_ref[pl.ds(ks, b), _head_lanes(s)])]
            return tuple(out)

        def q_loop(i, _):
            qs = pl.multiple_of(i * b, b)
            qbs = [q_ref[pl.ds(qs + r * rp, rp), _head_lanes(s)] for s, r in chains]
            zero = (jnp.zeros((rp, 1), F32), jnp.zeros((rp, HEAD_DIM), F32)) * len(chains)
            state = blocks(qbs, i, zero, causal)
            state = _sweep_earlier_blocks(i, state, 2, lambda j, st: blocks(qbs, j, st, no_mask))
            for ci, (s, r) in enumerate(chains):
                o_ref[pl.ds(qs + r * rp, rp), _head_lanes(s)] = state[2 * ci + 1]
            return 0

        lax.fori_loop(0, nq, q_loop, 0)

    def heads(off):
        return pl.BlockSpec((t, ATT_LANES), lambda h: (0, off + h))

    return _call(
        body, name="attn_fwd", grid=(n_steps,),
        in_specs=[heads(0), heads(n_steps), heads(2 * n_steps)], out_specs=[heads(0)],
        out_shape=[jax.ShapeDtypeStruct((t, n_heads * HEAD_DIM), F32)],
        inputs=[qkv, qkv, qkv], sem=("parallel",), comm=comm)


def _attn_bwd(qkv, do, o, t, n_heads, comm=None):
    b = ATT_BLOCK
    nq = t // b
    n_steps = n_heads // HEADS_PER_STEP
    scale = 1.0 / math.sqrt(HEAD_DIM)
    rp = b // ROW_PARTS
    chains = [(s, r) for s in range(HEADS_PER_STEP) for r in range(ROW_PARTS)]
    no_mask = [None] * ROW_PARTS

    def body(q_ref, k_ref, v_ref, do_ref, o_ref, dq_ref, dk_ref, dv_ref, qt_ref, dot_ref, dkt_ref, dvt_ref):
        for j in range(nq):
            rows = pl.ds(j * b, b)
            qt_ref[j] = q_ref[rows, :].astype(F32).T.astype(BF16)
            dot_ref[j] = do_ref[rows, :].astype(F32).T.astype(BF16)
        dkt_ref[...] = jnp.zeros((nq, ATT_LANES, b), F32)
        dvt_ref[...] = jnp.zeros((nq, ATT_LANES, b), F32)
        tri, causal = _att_consts()

        def blocks(i, fixed, j, state, masks):
            ks = pl.multiple_of(j * b, b)
            n = len(chains)
            kbs = [k_ref[pl.ds(ks, b), _head_lanes(s)] for s, _ in chains]
            scores = [_att_scores(fixed[ci][0], kbs[ci], masks[r]) for ci, (s, r) in enumerate(chains)]
            incls = [_suffix_sum(sc[2], tri) for sc in scores]
            das = [_dot(fixed[ci][1], v_ref[pl.ds(ks, b), _head_lanes(s)], NT) for ci, (s, r) in enumerate(chains)]
            a_bs, gs = [], []
            for ci, (s, r) in enumerate(chains):
                a = jnp.exp2(scores[ci][0] - (incls[ci] + state[3 * ci]))
                if masks[r] is not None:
                    a = jnp.where(masks[r], a, 0.0)
                a_bs.append(a.astype(BF16))
                gs.append(a_bs[ci].astype(F32) * das[ci])
            g_incls = [_suffix_sum(g, tri) for g in gs]
            dz_bs = []
            for ci, (s, r) in enumerate(chains):
                rest = (fixed[ci][2] - state[3 * ci + 1]) - (g_incls[ci] - gs[ci])
                sig = jnp.exp2(scores[ci][0] - scores[ci][1])
                dz = (gs[ci] - sig * rest) * scale
                if masks[r] is not None:
                    dz = jnp.where(masks[r], dz, 0.0)
                dz_bs.append(dz.astype(BF16))
            out = []
            for ci in range(n):
                out += [state[3 * ci] + incls[ci][:, 0:1], state[3 * ci + 1] + g_incls[ci][:, 0:1],
                        state[3 * ci + 2] + _dot(dz_bs[ci], kbs[ci])]
            for s in range(HEADS_PER_STEP):
                lanes = _head_lanes(s)
                dk_add, dv_add = None, None
                for ci, (cs, r) in enumerate(chains):
                    if cs == s:
                        part = slice(r * rp, (r + 1) * rp)
                        dk_c = _dot(qt_ref[i, lanes, part], dz_bs[ci])
                        dv_c = _dot(dot_ref[i, lanes, part], a_bs[ci])
                        dk_add = dk_c if dk_add is None else dk_add + dk_c
                        dv_add = dv_c if dv_add is None else dv_add + dv_c
                dkt_ref[j, lanes, :] += dk_add
                dvt_ref[j, lanes, :] += dv_add
            return tuple(out)

        def q_loop(i, _):
            qs = pl.multiple_of(i * b, b)
            fixed = []
            for s, r in chains:
                rows = pl.ds(qs + r * rp, rp)
                dob = do_ref[rows, _head_lanes(s)]
                total = jnp.sum(dob.astype(F32) * o_ref[rows, _head_lanes(s)], axis=-1, keepdims=True)
                fixed.append((q_ref[rows, _head_lanes(s)], dob, total))
            zero = (jnp.zeros((rp, 1), F32), jnp.zeros((rp, 1), F32), jnp.zeros((rp, HEAD_DIM), F32)) * len(chains)
            state = blocks(i, fixed, i, zero, causal)
            state = _sweep_earlier_blocks(i, state, 3, lambda j, st: blocks(i, fixed, j, st, no_mask))
            for ci, (s, r) in enumerate(chains):
                dq_ref[pl.ds(qs + r * rp, rp), _head_lanes(s)] = state[3 * ci + 2].astype(BF16)
            return 0

        lax.fori_loop(0, nq, q_loop, 0)
        for j in range(nq):
            rows = pl.ds(j * b, b)
            dk_ref[rows, :] = dkt_ref[j].T.astype(BF16)
            dv_ref[rows, :] = dvt_ref[j].T.astype(BF16)

    def heads(off):
        return pl.BlockSpec((t, ATT_LANES), lambda h: (0, off + h))

    shape = jax.ShapeDtypeStruct((t, n_heads * HEAD_DIM), BF16)
    return _call(
        body, name="attn_bwd", grid=(n_steps,),
        in_specs=[heads(0), heads(n_steps), heads(2 * n_steps), heads(0), heads(0)],
        out_specs=[heads(0)] * 3, out_shape=[shape] * 3, inputs=[qkv, qkv, qkv, do, o],
        scratch_shapes=[pltpu.VMEM((nq, ATT_LANES, b), BF16)] * 2 + [pltpu.VMEM((nq, ATT_LANES, b), F32)] * 2,
        sem=("parallel",), comm=comm)


def _place():
    x, y, c = lax.axis_index("x"), lax.axis_index("y"), lax.axis_index("c")
    return x, y, c


def _all_gather_weights(shards):
    n = len(shards)

    def body(*refs):
        ins, outs = refs[:n], refs[n:2 * n]
        send_sems, recv_sems, local_sems = refs[2 * n:]
        x, y, c = _place()
        me, sibling = (x, y, c), (x, y, 1 - c)
        chips = [(1 - x, y), (x, 1 - y), (1 - x, 1 - y)]

        def slot(px, py, pc):
            return 4 * px + 2 * py + pc

        def copy(ti, k, block, to, src=None):
            dst = outs[ti].at[slot(*block)]
            return pltpu.make_async_remote_copy(
                src_ref=dst if src is None else src, dst_ref=dst,
                send_sem=send_sems.at[ti, k], recv_sem=recv_sems.at[ti, k], device_id=to, device_id_type=MESH)

        mine = [pltpu.make_async_copy(ins[ti], outs[ti].at[slot(*me)], local_sems.at[ti]) for ti in range(n)]
        for cp in mine:
            cp.start()
        first = []
        for ti in range(n):
            first.append(copy(ti, 0, me, sibling, src=ins[ti]))
            first += [copy(ti, 1 + j, me, (*chip, c), src=ins[ti]) for j, chip in enumerate(chips)]
        for cp in first:
            cp.start()
        passed = []
        for j, chip in enumerate(chips):
            for ti in range(n):
                copy(ti, 1 + j, (*chip, c), me).wait_recv()
                fwd = copy(ti, 4 + j, (*chip, c), sibling)
                fwd.start()
                passed.append(fwd)
        for ti in range(n):
            copy(ti, 0, sibling, me).wait_recv()
            for j, chip in enumerate(chips):
                copy(ti, 4 + j, (*chip, 1 - c), me).wait_recv()
        for cp in first + passed:
            cp.wait_send()
        for cp in mine:
            cp.wait()

    any_spec = pl.BlockSpec(memory_space=pl.ANY)
    return pl.pallas_call(
        body, name="all_gather_weights",
        in_specs=[any_spec] * n, out_specs=[any_spec] * n,
        out_shape=[jax.ShapeDtypeStruct((N_DEV, *s.shape), s.dtype) for s in shards],
        scratch_shapes=[pltpu.SemaphoreType.DMA((n, 7)), pltpu.SemaphoreType.DMA((n, 7)), pltpu.SemaphoreType.DMA((n,))],
        compiler_params=pltpu.CompilerParams(has_side_effects=True),
    )(*shards)


def _flip(v, on):
    return 1 - v if on else v


def _plan_copies(plan, refs, send_sems, recv_sems):
    return [pltpu.make_async_remote_copy(src_ref=src, dst_ref=dst, send_sem=send_sems.at[k], recv_sem=recv_sems.at[k],
                                         device_id=dev, device_id_type=MESH)
            for k, (src, dst, dev) in enumerate(plan(refs))]


def _copies_now(name, arrays, plan, n_copies):
    n = len(arrays)

    def body(*refs):
        copies = _plan_copies(plan, refs[n:2 * n], refs[2 * n], refs[2 * n + 1])
        for cp in copies:
            cp.start()
        for cp in copies:
            cp.wait_send()
            cp.wait_recv()

    any_spec = pl.BlockSpec(memory_space=pl.ANY)
    return list(pl.pallas_call(
        body, name=name, in_specs=[any_spec] * n, out_specs=[any_spec] * n,
        out_shape=[jax.ShapeDtypeStruct(a.shape, a.dtype) for a in arrays],
        input_output_aliases={i: i for i in range(n)},
        scratch_shapes=[pltpu.SemaphoreType.DMA((n_copies,)), pltpu.SemaphoreType.DMA((n_copies,))],
        compiler_params=pltpu.CompilerParams(has_side_effects=True),
    )(*arrays))


SIBLING, ACROSS_Y, ACROSS_X, DIAGONAL = 1, 2, 4, 6


def _plan_gather_own(peers, rows=None):
    def plan(refs):
        x, y, c = _place()
        mine = refs[0].at[4 * x + 2 * y + c]
        if rows is not None:
            mine = mine.at[pl.ds(*rows)]
        return [(mine, mine, (_flip(x, k & 4), _flip(y, k & 2), _flip(c, k & 1))) for k in peers]
    return plan, len(peers)


def _plan_gather_forward(n):
    def plan(refs):
        x, y, c = _place()
        out = []
        for ti in range(n):
            for r in range(1, 4):
                blk = refs[ti].at[4 * _flip(x, r & 2) + 2 * _flip(y, r & 1) + c]
                out.append((blk, blk, (x, y, 1 - c)))
        return out
    return plan, 3 * n


def _join_plans(*parts):
    def plan(refs):
        out, at = [], 0
        for part, n_arrays, _ in parts:
            out += part(refs[at:at + n_arrays])
            at += n_arrays
        return out
    return plan, sum(n_cp for _, _, n_cp in parts)


def _plan_rs_sibling(n):
    def plan(refs):
        x, y, c = _place()
        out = []
        for ti in range(n):
            for r in range(4):
                src = refs[ti].at[4 * _flip(x, r & 2) + 2 * _flip(y, r & 1) + (1 - c)]
                out.append((src, refs[n + ti].at[r], (x, y, 1 - c)))
        return out
    return plan, 4 * n


def _plan_rs_owner(n, relations=(1, 2, 3)):
    def plan(refs):
        x, y, c = _place()
        out = []
        for ti in range(n):
            for r in relations:
                out.append((refs[ti].at[r], refs[n + ti].at[r], (_flip(x, r & 2), _flip(y, r & 1), c)))
        return out
    return plan, len(relations) * n


def _owner_slots():
    x, y, c = _place()
    idx = []
    for r in range(4):
        ox, oy = (1 - x if r & 2 else x), (1 - y if r & 1 else y)
        idx.append(4 * ox + 2 * oy + c)
    return jnp.stack(idx).astype(jnp.int32)


def _row_tile(rows, cols):
    tr = max(8, min(rows, (1 << 19) // cols))
    while rows % tr:
        tr //= 2
    return tr


def _rs_chip_sum(name, slots, partial, from_sibling):
    _, rows, cols = partial.shape
    tr = _row_tile(rows, cols)

    def body(slots_ref, p_ref, s_ref, o_ref):
        o_ref[...] = (p_ref[...] + s_ref[...]).astype(BF16)

    grid_spec = pltpu.PrefetchScalarGridSpec(
        num_scalar_prefetch=1, grid=(3, rows // tr),
        in_specs=[pl.BlockSpec((None, tr, cols), lambda r, i, s: (s[r + 1], i, 0)),
                  pl.BlockSpec((None, tr, cols), lambda r, i, s: (r + 1, i, 0))],
        out_specs=pl.BlockSpec((None, tr, cols), lambda r, i, s: (r + 1, i, 0)))
    return pl.pallas_call(
        body, name=name, grid_spec=grid_spec, out_shape=jax.ShapeDtypeStruct((4, rows, cols), BF16),
        compiler_params=_cparams(("parallel", "parallel")),
    )(slots, partial, from_sibling)


def _adamw(w, g, m, v):
    m = ADAM_B1 * m + (1.0 - ADAM_B1) * g
    v = ADAM_B2 * v + (1.0 - ADAM_B2) * (g * g)
    m_hat = m / (1.0 - ADAM_B1 ** ADAM_STEP)
    v_hat = v / (1.0 - ADAM_B2 ** ADAM_STEP)
    delta = -ADAM_LR * (m_hat / (jnp.sqrt(v_hat) + ADAM_EPS) + ADAM_WD * w)
    return delta, m, v


def _rs_final_adamw(name, slots, partial, from_sibling, from_chips, w, m, v):
    rows, cols = w.shape
    tr = _row_tile(rows, cols)

    def body(slots_ref, p_ref, s_ref, c1_ref, c2_ref, c3_ref, w_ref, m_ref, v_ref, g_ref, d_ref, nm_ref, nv_ref):
        g = p_ref[...] + s_ref[...]
        g = g + c1_ref[...].astype(F32)
        g = g + c2_ref[...].astype(F32)
        g = g + c3_ref[...].astype(F32)
        delta, nm, nv = _adamw(w_ref[...], g, m_ref[...], v_ref[...])
        g_ref[...] = g
        d_ref[...] = delta
        nm_ref[...] = nm
        nv_ref[...] = nv

    def slot(r):
        return pl.BlockSpec((None, tr, cols), lambda i, s: (r, i, 0))

    flat = pl.BlockSpec((tr, cols), lambda i, s: (i, 0))
    grid_spec = pltpu.PrefetchScalarGridSpec(
        num_scalar_prefetch=1, grid=(rows // tr,),
        in_specs=[pl.BlockSpec((None, tr, cols), lambda i, s: (s[0], i, 0)), slot(0), slot(1), slot(2), slot(3), flat, flat, flat],
        out_specs=[flat] * 4)
    return pl.pallas_call(
        body, name=name, grid_spec=grid_spec, out_shape=[jax.ShapeDtypeStruct((rows, cols), F32)] * 4,
        compiler_params=_cparams(("parallel",)),
    )(slots, partial, from_sibling, from_chips, from_chips, from_chips, w, m, v)


def _small_all_reduce(packet):
    rows, d = packet.shape

    def body(p_ref, sum_ref, loss_ref, all_ref, send_sems, recv_sems):
        x, y, c = _place()
        me = 4 * x + 2 * y + c
        all_ref[me] = p_ref[...]
        copies = []
        for k in range(1, N_DEV):
            px, py, pc = (1 - x if k & 4 else x), (1 - y if k & 2 else y), (1 - c if k & 1 else c)
            cp = pltpu.make_async_remote_copy(
                src_ref=p_ref, dst_ref=all_ref.at[me], send_sem=send_sems.at[k], recv_sem=recv_sems.at[k],
                device_id=(px, py, pc), device_id_type=MESH)
            cp.start()
            copies.append(cp)
        for cp in copies:
            cp.wait_recv()
        for cp in copies:
            cp.wait_send()
        total = all_ref[0]
        for j in range(1, N_DEV):
            total = total + all_ref[j]
        sum_ref[...] = total
        loss_ref[...] = jnp.sum(total[0:1, :], axis=-1, keepdims=True)

    vmem = pl.BlockSpec(memory_space=pltpu.VMEM)
    return pl.pallas_call(
        body, name="small_all_reduce",
        in_specs=[vmem], out_specs=[vmem, vmem],
        out_shape=[jax.ShapeDtypeStruct((rows, d), F32), jax.ShapeDtypeStruct((1, 1), F32)],
        scratch_shapes=[pltpu.VMEM((N_DEV, rows, d), F32), pltpu.SemaphoreType.DMA((N_DEV,)), pltpu.SemaphoreType.DMA((N_DEV,))],
        compiler_params=pltpu.CompilerParams(has_side_effects=True),
    )(packet)


def _small_adamw(w, g, m, v):
    def body(w_ref, g_ref, m_ref, v_ref, d_ref, nm_ref, nv_ref):
        delta, nm, nv = _adamw(w_ref[...], g_ref[...], m_ref[...], v_ref[...])
        d_ref[...] = delta
        nm_ref[...] = nm
        nv_ref[...] = nv

    vmem = pl.BlockSpec(memory_space=pltpu.VMEM)
    return pl.pallas_call(
        body, name="small_adamw", in_specs=[vmem] * 4, out_specs=[vmem] * 3,
        out_shape=[jax.ShapeDtypeStruct(w.shape, F32)] * 3,
    )(w, g, m, v)


def kernel(x, ln_in_g, ln_in_b, w_in, w_pool, pool_scale, w_out, ln1_g, ln1_b, w_ff1, b_ff1, w_ff2, b_ff2, ln2_g, ln2_b, loss_target, m_ln_in_g, m_ln_in_b, m_w_in, m_w_pool, m_pool_scale, m_w_out, m_ln1_g, m_ln1_b, m_w_ff1, m_b_ff1, m_w_ff2, m_b_ff2, m_ln2_g, m_ln2_b, v_ln_in_g, v_ln_in_b, v_w_in, v_w_pool, v_pool_scale, v_w_out, v_ln1_g, v_ln1_b, v_w_ff1, v_b_ff1, v_w_ff2, v_b_ff2, v_ln2_g, v_ln2_b):
    t, d = x.shape[1], x.shape[2]
    n_groups = len(POOL_WINDOWS)
    c_pool = w_pool.shape[3]
    p = n_groups * c_pool
    n_heads = (d - p) // HEAD_DIM
    ws_in = w_in.shape[2]
    n_in = N_DEV * ws_in
    ws_out = w_out.shape[1]
    ws_f = w_ff1.shape[2]
    f = N_DEV * ws_f
    pr = w_pool.shape[2]
    assert n_in == p + 3 * n_heads * HEAD_DIM and N_DEV * ws_out == d and N_DEV * pr == c_pool

    tm_big = min(t, 1024)
    tm_ep = min(t, 512)
    tkk = min(t, 2048)
    half_f = min(ws_f, 512)
    per_f = ws_f // half_f

    x2 = x.reshape(t, d)
    target = loss_target.reshape(t, d)
    g0, b0 = ln_in_g.reshape(1, d), ln_in_b.reshape(1, d)

    shards = [w_in.reshape(d, ws_in), w_out.reshape(ws_out, d), w_ff1.reshape(d, ws_f), w_ff2.reshape(ws_f, d),
              w_pool.reshape(n_groups * pr, c_pool)]
    x_, y_, c_ = _place()
    me = 4 * x_ + 2 * y_ + c_
    win_g, wpool_g, scale_g = _all_gather_weights(
        [shards[0].astype(BF16), shards[4].astype(BF16), pool_scale.reshape(n_groups, pr)])
    land_out, land_1, land_2 = [
        lax.dynamic_update_index_in_dim(lax.empty((N_DEV, *s.shape), BF16), s.astype(BF16), me, 0) for s in shards[1:4]]
    wp_full = wpool_g.reshape(N_DEV, n_groups, pr, c_pool).transpose(1, 0, 2, 3).reshape(n_groups, c_pool, c_pool)
    sc_full = scale_g.transpose(1, 0, 2).reshape(n_groups, 1, c_pool)

    def sds(shape, dtype=F32):
        return jax.ShapeDtypeStruct(shape, dtype)

    vec = pl.BlockSpec((1, d), lambda m, n, k: (0, 0))
    row_ep = pl.BlockSpec((tm_ep, d), lambda m, n, k: (m, 0))
    tm_res = min(t, 256)
    row_res = pl.BlockSpec((tm_res, d), lambda m, n, k: (m, 0))
    seq = ("arbitrary", "arbitrary", "arbitrary")

    h0, h0b = _ln_in_fwd(x2, g0, b0, tm_big)

    pool_shards = p // ws_in

    def mm_u(name, first, count, dtype, comm=None):
        return _matmul(
            name, h0b, win_g, dims=NN, grid=(t // tm_big, count, 1),
            a_spec=pl.BlockSpec((tm_big, d), lambda m, n, k: (m, 0)),
            b_spec=pl.BlockSpec((None, d, ws_in), lambda m, n, k: (n + first, 0, 0)),
            out_shape=[sds((t, count * ws_in), dtype)],
            out_specs=[pl.BlockSpec((tm_big, ws_in), lambda m, n, k: (m, n))],
            acc_shape=(tm_big, ws_in), epilogue=_store_epilogue(dtype), comm=comm)

    two_level = _plan_gather_own([SIBLING, ACROSS_Y, ACROSS_X, DIAGONAL])
    forward = _plan_gather_forward(1)
    half = land_1.shape[1] // 2
    diag_a, diag_b = _plan_gather_own([DIAGONAL], (0, half)), _plan_gather_own([DIAGONAL], (half, half))
    (u_pool,) = mm_u("mm_u_pool", 0, pool_shards, F32)
    (qkv,), (wout_part, w1_diag) = mm_u(
        "mm_u_qkv", pool_shards, N_DEV - pool_shards, BF16,
        comm=([land_out, land_1], *_join_plans((two_level[0], 1, two_level[1]), (diag_a[0], 1, diag_a[1]))))

    (y_pool, ypre), (wout_g, w1_diag) = _pool_fwd(
        u_pool, wp_full, sc_full, t, c_pool,
        comm=([wout_part, w1_diag], *_join_plans((forward[0], 1, forward[1]), (diag_b[0], 1, diag_b[1]))))
    (o,), (w1_part,) = _attn_fwd(qkv, t, n_heads, comm=([w1_diag], *_plan_gather_own([SIBLING, ACROSS_Y, ACROSS_X])))
    mixin = jnp.concatenate([y_pool, o.astype(BF16)], axis=1)
    wout_2d = wout_g.reshape(d, d)

    def ep_ln1(acc, ex, outs, first):
        h0_ref, g_ref, b_ref = ex
        r1 = DEEPNORM_ALPHA * h0_ref[...] + acc
        xhat, _ = _ln_stats(r1)
        h1 = xhat * g_ref[...] + b_ref[...]
        outs[0][...] = r1
        outs[1][...] = h1
        outs[2][...] = h1.astype(BF16)

    (r1, h1, h1b), (w1_g,) = _matmul(
        "mm_mix_ln1", mixin, wout_2d, dims=NN, grid=(t // tm_res, 1, 1),
        a_spec=pl.BlockSpec((tm_res, d), lambda m, n, k: (m, 0)),
        b_spec=pl.BlockSpec((d, d), lambda m, n, k: (0, 0)),
        extras=(h0, ln1_g, ln1_b), extra_specs=(row_res, vec, vec),
        out_shape=[sds((t, d)), sds((t, d)), sds((t, d), BF16)], out_specs=[row_res] * 3,
        acc_shape=(tm_res, d), epilogue=lambda acc, ex, outs: ep_ln1(acc, ex, outs, None), comm=([w1_part], *forward))

    def ep_ff1(acc, ex, outs):
        f1 = acc + ex[0][...]
        outs[0][...] = f1
        r = jnp.maximum(f1, 0.0)
        outs[1][...] = (r * r).astype(BF16)

    ff_tile = pl.BlockSpec((tm_big, half_f), lambda m, n, k: (m, n))
    (f1, act), (w2_part,) = _matmul(
        "mm_ff1", h1b, w1_g, dims=NN, grid=(t // tm_big, f // half_f, 1),
        a_spec=pl.BlockSpec((tm_big, d), lambda m, n, k: (m, 0)),
        b_spec=pl.BlockSpec((None, d, half_f), lambda m, n, k: (n // per_f, 0, n % per_f)),
        extras=(b_ff1,), extra_specs=(pl.BlockSpec((1, half_f), lambda m, n, k: (0, n)),),
        out_shape=[sds((t, f)), sds((t, f), BF16)], out_specs=[ff_tile, ff_tile],
        acc_shape=(tm_big, half_f), epilogue=ep_ff1, comm=([land_2], *two_level))
    (w2_g,) = _copies_now("gather_forward_w_ff2", [w2_part], *forward)

    def ep_ln2(acc, ex, outs, first):
        h1_ref, tgt_ref, bf2_ref, g_ref, b_ref = ex
        dr2_ref, dr2b_ref, dg_ref, db_ref, dbf2_ref, loss_ref = outs
        r2 = DEEPNORM_ALPHA * h1_ref[...] + (acc + bf2_ref[...])
        xhat, rstd = _ln_stats(r2)
        err = xhat * g_ref[...] + b_ref[...] - tgt_ref[...]
        dr2, dg, db = _ln_bwd(err * (1.0 / d), xhat, rstd, g_ref[...])
        dr2_ref[...] = dr2
        dr2b_ref[...] = dr2.astype(BF16)
        _acc_rows(first, dg_ref, dg)
        _acc_rows(first, db_ref, db)
        _acc_rows(first, dbf2_ref, jnp.sum(dr2, axis=0, keepdims=True))
        _acc_rows(first, loss_ref, jnp.sum(err * err, axis=0, keepdims=True) * (0.5 / d))

    dr2, dr2b, dg2, db2, dbf2, loss_vec = _matmul(
        "mm_ff2_ln2_loss", act, w2_g, dims=NN, grid=(t // tm_ep, 1, N_DEV),
        a_spec=pl.BlockSpec((tm_ep, ws_f), lambda m, n, k: (m, k)),
        b_spec=pl.BlockSpec((None, ws_f, d), lambda m, n, k: (k, 0, 0)),
        extras=(h1, target, b_ff2, ln2_g, ln2_b), extra_specs=(row_ep, row_ep, vec, vec, vec),
        out_shape=[sds((t, d)), sds((t, d), BF16)] + [sds((1, d))] * 4, out_specs=[row_ep, row_ep, vec, vec, vec, vec],
        acc_shape=(tm_ep, d), epilogue=lambda acc, ex, outs: ep_ln2(acc, ex, outs, pl.program_id(0) == 0), sem=seq)

    def ep_dff1(acc, ex, outs):
        df1 = acc * (2.0 * jnp.maximum(ex[0][...], 0.0))
        outs[0][...] = df1.astype(BF16)
        _acc_rows(pl.program_id(1) == 0, outs[1], jnp.sum(df1, axis=0, keepdims=True))

    df_tile = pl.BlockSpec((tm_big, ws_f), lambda n, m, k: (m, n))
    df1b, dbf1 = _matmul(
        "mm_dff1", dr2b, w2_g, dims=NT, grid=(N_DEV, t // tm_big, 1),
        a_spec=pl.BlockSpec((tm_big, d), lambda n, m, k: (m, 0)),
        b_spec=pl.BlockSpec((None, ws_f, d), lambda n, m, k: (n, 0, 0)),
        extras=(f1,), extra_specs=(df_tile,),
        out_shape=[sds((t, f), BF16), sds((1, f))], out_specs=[df_tile, pl.BlockSpec((1, ws_f), lambda n, m, k: (0, n))],
        acc_shape=(tm_big, ws_f), epilogue=ep_dff1, sem=("parallel", "arbitrary", "arbitrary"))

    tn_d = min(d, 1024)
    dw2 = _matmul(
        "mm_dw2", act, dr2b, dims=TN, grid=(N_DEV, d // tn_d, t // tkk),
        a_spec=pl.BlockSpec((tkk, ws_f), lambda m, n, k: (k, m)),
        b_spec=pl.BlockSpec((tkk, tn_d), lambda m, n, k: (k, n)),
        out_shape=[sds((N_DEV, ws_f, d))], out_specs=[pl.BlockSpec((None, ws_f, tn_d), lambda m, n, k: (m, 0, n))],
        acc_shape=(ws_f, tn_d), epilogue=_store_epilogue(F32))[0]

    dw1 = _matmul(
        "mm_dw1", h1b, df1b, dims=TN, grid=(d // tn_d, N_DEV, t // tkk),
        a_spec=pl.BlockSpec((tkk, tn_d), lambda m, n, k: (k, m)),
        b_spec=pl.BlockSpec((tkk, ws_f), lambda m, n, k: (k, n)),
        out_shape=[sds((N_DEV, d, ws_f))], out_specs=[pl.BlockSpec((None, tn_d, ws_f), lambda m, n, k: (n, m, 0))],
        acc_shape=(tn_d, ws_f), epilogue=_store_epilogue(F32))[0]

    def ep_ln1_bwd(acc, ex, outs, first):
        dr2_ref, r1_ref, g_ref = ex
        xhat, rstd = _ln_stats(r1_ref[...])
        dr1, dg, db = _ln_bwd(DEEPNORM_ALPHA * dr2_ref[...] + acc, xhat, rstd, g_ref[...])
        outs[0][...] = dr1
        outs[1][...] = dr1.astype(BF16)
        _acc_rows(first, outs[2], dg)
        _acc_rows(first, outs[3], db)

    slots = _owner_slots()

    def to_sibling(parts):
        return (parts + [lax.empty((4, *pt.shape[1:]), F32) for pt in parts], *_plan_rs_sibling(len(parts)))

    def to_owner(names_, parts, from_sib):
        sums = [_rs_chip_sum("rs_chip_sum_" + nm, slots, pt, fs) for nm, pt, fs in zip(names_, parts, from_sib)]
        return (sums + [lax.empty(cs.shape, BF16) for cs in sums], *_plan_rs_owner(len(sums)))

    (dr1, dr1b, dg1, db1), (dw1, dw2, sib_1, sib_2) = _matmul(
        "mm_dh1_ln1_bwd", df1b, w1_g, dims=NT, grid=(t // tm_ep, 1, N_DEV),
        a_spec=pl.BlockSpec((tm_ep, ws_f), lambda m, n, k: (m, k)),
        b_spec=pl.BlockSpec((None, d, ws_f), lambda m, n, k: (k, 0, 0)),
        extras=(dr2, r1, ln1_g), extra_specs=(row_ep, row_ep, vec),
        out_shape=[sds((t, d)), sds((t, d), BF16), sds((1, d)), sds((1, d))], out_specs=[row_ep, row_ep, vec, vec],
        acc_shape=(tm_ep, d), epilogue=lambda acc, ex, outs: ep_ln1_bwd(acc, ex, outs, pl.program_id(0) == 0), sem=seq,
        comm=to_sibling([dw1, dw2]))
    own_1 = to_owner(["w_ff1"], [dw1], [sib_1])
    own_2 = to_owner(["w_ff2"], [dw2], [sib_2])[0]

    dwout = _matmul(
        "mm_dwout", mixin, dr1b, dims=TN, grid=(d // tn_d, d // tn_d, t // tkk),
        a_spec=pl.BlockSpec((tkk, tn_d), lambda m, n, k: (k, m)),
        b_spec=pl.BlockSpec((tkk, tn_d), lambda m, n, k: (k, n)),
        out_shape=[sds((d, d))], out_specs=[pl.BlockSpec((tn_d, tn_d), lambda m, n, k: (m, n))],
        acc_shape=(tn_d, tn_d), epilogue=_store_epilogue(F32))[0].reshape(N_DEV, ws_out, d)

    tn_mix = min(tn_d, p, d - p)

    def mm_dmixin(name, first, width, dtype, comm=None):
        return _matmul(
            name, dr1b, wout_2d, dims=NT, grid=(t // tm_big, width // tn_mix, 1),
            a_spec=pl.BlockSpec((tm_big, d), lambda m, n, k: (m, 0)),
            b_spec=pl.BlockSpec((tn_mix, d), lambda m, n, k: (n + first // tn_mix, 0)),
            out_shape=[sds((t, width), dtype)], out_specs=[pl.BlockSpec((tm_big, tn_mix), lambda m, n, k: (m, n))],
            acc_shape=(tm_big, tn_mix), epilogue=_store_epilogue(dtype), comm=comm)

    (dy_pool,), (dwout, sib_out) = mm_dmixin("mm_dmixin_pool", 0, p, F32, comm=to_sibling([dwout]))
    (do,) = mm_dmixin("mm_dmixin_att", p, d - p, BF16)

    (du_pool, dwp, dsc), (_, chips_out) = _pool_bwd(
        dy_pool, ypre, wp_full, sc_full, t, c_pool, comm=to_owner(["w_out"], [dwout], [sib_out]))
    (dq, dk, dv), (_, chips_1) = _attn_bwd(qkv, do, o, t, n_heads, comm=own_1)
    dub = jnp.concatenate([du_pool.astype(BF16), dq, dk, dv], axis=1)

    (dwin,), own_2 = _matmul(
        "mm_dwin", h0b, dub, dims=TN, grid=(d // tn_d, N_DEV, t // tkk),
        a_spec=pl.BlockSpec((tkk, tn_d), lambda m, n, k: (k, m)),
        b_spec=pl.BlockSpec((tkk, ws_in), lambda m, n, k: (k, n)),
        out_shape=[sds((N_DEV, d, ws_in))], out_specs=[pl.BlockSpec((None, tn_d, ws_in), lambda m, n, k: (n, m, 0))],
        acc_shape=(tn_d, ws_in), epilogue=_store_epilogue(F32), comm=(own_2, *_plan_rs_owner(1, (1, 2))))
    dwp_g = dwp.reshape(n_groups, N_DEV, pr, c_pool).transpose(1, 0, 2, 3).reshape(N_DEV, n_groups * pr, c_pool)

    def ep_ln0_bwd(acc, ex, outs, first):
        dr1_ref, x_ref, g_ref = ex
        xhat, rstd = _ln_stats(x_ref[...])
        dx, dg, db = _ln_bwd(DEEPNORM_ALPHA * dr1_ref[...] + acc, xhat, rstd, g_ref[...])
        outs[0][...] = dx
        _acc_rows(first, outs[1], dg)
        _acc_rows(first, outs[2], db)

    sib_in = to_sibling([dwin, dwp_g])
    to_diagonal = _plan_rs_owner(1, (3,))
    last_host = (own_2 + sib_in[0], *_join_plans((to_diagonal[0], 2, to_diagonal[1]), (sib_in[1], 4, sib_in[2])))
    def two_blocks(a_ref, b_ref):
        return _dot(a_ref[:, :ws_in], b_ref[0], NT) + _dot(a_ref[:, ws_in:], b_ref[1], NT)

    (dx, dg0, db0), (_, chips_2, dwin, dwp_g, sib_in_, sib_p) = _matmul(
        "mm_dh0_ln0_bwd", dub, win_g, dims=NT, grid=(t // tm_ep, 1, N_DEV // 2),
        a_spec=pl.BlockSpec((tm_ep, 2 * ws_in), lambda m, n, k: (m, k)),
        b_spec=pl.BlockSpec((2, d, ws_in), lambda m, n, k: (k, 0, 0)),
        extras=(dr1, x2, g0), extra_specs=(row_ep, row_ep, vec),
        out_shape=[sds((t, d)), sds((1, d)), sds((1, d))], out_specs=[row_ep, vec, vec],
        acc_shape=(tm_ep, d), epilogue=lambda acc, ex, outs: ep_ln0_bwd(acc, ex, outs, pl.program_id(0) == 0), sem=seq,
        comm=last_host, dot_fn=two_blocks)

    _, _, chips_in, chips_p = _copies_now("rs_owner_w_in", *to_owner(["w_in", "w_pool"], [dwin, dwp_g], [sib_in_, sib_p]))
    w_of = {"w_in": shards[0], "w_out": shards[1], "w_ff1": shards[2], "w_ff2": shards[3], "w_pool": shards[4]}
    mv_of = {"w_in": (m_w_in, v_w_in), "w_out": (m_w_out, v_w_out), "w_ff1": (m_w_ff1, v_w_ff1),
             "w_ff2": (m_w_ff2, v_w_ff2), "w_pool": (m_w_pool, v_w_pool)}
    big = {}
    for nm, pt, fs, fc in [("w_ff1", dw1, sib_1, chips_1), ("w_ff2", dw2, sib_2, chips_2), ("w_out", dwout, sib_out, chips_out),
                           ("w_in", dwin, sib_in_, chips_in), ("w_pool", dwp_g, sib_p, chips_p)]:
        w2d = w_of[nm]
        m_, v_ = mv_of[nm]
        big[nm] = _rs_final_adamw("rs_final_adamw_" + nm, slots, pt, fs, fc, w2d, m_.reshape(w2d.shape), v_.reshape(w2d.shape))

    n_f_rows = f // d
    pad_sc = d - p
    packet = jnp.concatenate(
        [loss_vec, dg0, db0, dg1, db1, dbf2, dg2, db2, dbf1.reshape(n_f_rows, d),
         jnp.pad(dsc.reshape(1, p), ((0, 0), (0, pad_sc)))], axis=0)
    n_rows = packet.shape[0]
    n_pad = (-n_rows) % 8
    packet = jnp.pad(packet, ((0, n_pad), (0, 0)))
    sums, loss11 = _small_all_reduce(packet)
    dsc_full = sums[8 + n_f_rows, :p].reshape(n_groups, N_DEV, pr)
    dsc_mine = lax.dynamic_index_in_dim(dsc_full, me, axis=1, keepdims=False)

    def sc_row(a):
        return jnp.pad(a.reshape(1, n_groups * pr), ((0, 0), (0, d - n_groups * pr)))

    def small_pack(ln0g, ln0b, l1g, l1b, bf2, l2g, l2b, bf1, sc):
        rows = [jnp.zeros((1, d), F32), ln0g.reshape(1, d), ln0b.reshape(1, d), l1g, l1b, bf2, l2g, l2b,
                bf1.reshape(n_f_rows, d), sc_row(sc), jnp.zeros((n_pad, d), F32)]
        return jnp.concatenate(rows, axis=0)

    w_small = small_pack(ln_in_g, ln_in_b, ln1_g, ln1_b, b_ff2, ln2_g, ln2_b, b_ff1, pool_scale)
    m_small = small_pack(m_ln_in_g, m_ln_in_b, m_ln1_g, m_ln1_b, m_b_ff2, m_ln2_g, m_ln2_b, m_b_ff1, m_pool_scale)
    v_small = small_pack(v_ln_in_g, v_ln_in_b, v_ln1_g, v_ln1_b, v_b_ff2, v_ln2_g, v_ln2_b, v_b_ff1, v_pool_scale)
    g_small = jnp.concatenate([sums[:8 + n_f_rows], sc_row(dsc_mine), jnp.zeros((n_pad, d), F32)], axis=0)
    small = (g_small,) + tuple(_small_adamw(w_small, g_small, m_small, v_small))

    def unpack(a):
        sc = a[8 + n_f_rows, :n_groups * pr].reshape(1, n_groups, pr)
        return {"ln_in_g": a[1], "ln_in_b": a[2], "ln1_g": a[3:4], "ln1_b": a[4:5], "b_ff2": a[5:6], "ln2_g": a[6:7],
                "ln2_b": a[7:8], "b_ff1": a[8:8 + n_f_rows].reshape(1, f), "pool_scale": sc}

    shapes = {"w_in": w_in.shape, "w_out": w_out.shape, "w_ff1": w_ff1.shape, "w_ff2": w_ff2.shape, "w_pool": w_pool.shape}
    order = ["ln_in_g", "ln_in_b", "w_in", "w_pool", "pool_scale", "w_out", "ln1_g", "ln1_b", "w_ff1", "b_ff1", "w_ff2",
             "b_ff2", "ln2_g", "ln2_b"]
    outs = []
    for kind in range(4):
        small_k = unpack(small[kind])
        for nm in order:
            outs.append(big[nm][kind].reshape(shapes[nm]) if nm in big else small_k[nm])
    return (loss11.reshape(()), dx.reshape(x.shape), *outs)
```

```python
import functools
import math

import jax
import jax.numpy as jnp
from jax import lax
from jax.experimental import pallas as pl
from jax.experimental.pallas import tpu as pltpu

F32 = jnp.float32
BF16 = jnp.bfloat16
MESH = pl.DeviceIdType.MESH

N_DEV = 8
HEAD_DIM = 128
POOL_WINDOWS = (2, 4, 8, 16)
DEEPNORM_ALPHA = (2.0 * 1) ** 0.25
LN_EPS = 1e-5
ADAM_LR = 0.001
ADAM_B1 = 0.9
ADAM_B2 = 0.999
ADAM_EPS = 1e-08
ADAM_WD = 0.01
ADAM_STEP = 10

V7X_VMEM_LIMIT = 56 * 1024 * 1024
ATT_BLOCK = 256
POOL_CHUNK = 128

NN = (((1,), (0,)), ((), ()))
NT = (((1,), (1,)), ((), ()))
TN = (((0,), (0,)), ((), ()))


def _dot(a, b, dims=NN):
    return lax.dot_general(a, b, dims, preferred_element_type=F32)


def _cparams(sem=None):
    return pltpu.CompilerParams(dimension_semantics=sem, vmem_limit_bytes=V7X_VMEM_LIMIT)


def _ln_stats(r):
    mu = jnp.mean(r, axis=-1, keepdims=True)
    xc = r - mu
    var = jnp.mean(xc * xc, axis=-1, keepdims=True)
    rstd = lax.rsqrt(var + LN_EPS)
    return xc * rstd, rstd


def _ln_bwd(dy, xhat, rstd, g):
    dxh = dy * g
    m1 = jnp.mean(dxh, axis=-1, keepdims=True)
    m2 = jnp.mean(dxh * xhat, axis=-1, keepdims=True)
    dx = rstd * (dxh - m1 - xhat * m2)
    dg = jnp.sum(dy * xhat, axis=0, keepdims=True)
    db = jnp.sum(dy, axis=0, keepdims=True)
    return dx, dg, db


def _acc_rows(first, ref, val):
    @pl.when(first)
    def _():
        ref[...] = val

    @pl.when(jnp.logical_not(first))
    def _():
        ref[...] += val


def _call(body, *, name, grid, in_specs, out_specs, out_shape, inputs, scratch_shapes=(), sem=None, comm=None):
    in_specs, out_specs, out_shape, inputs = list(in_specs), list(out_specs), list(out_shape), list(inputs)
    if comm is None:
        outs = pl.pallas_call(
            body, name=name, grid=grid, in_specs=in_specs, out_specs=out_specs, out_shape=out_shape,
            scratch_shapes=list(scratch_shapes), compiler_params=_cparams(sem))(*inputs)
        return list(outs), []
    arrays, plan, n_copies = comm
    n_in, n_out, nc, n_scr = len(inputs), len(out_shape), len(arrays), len(scratch_shapes)

    def hosted(*refs):
        ins = refs[:n_in]
        outs = refs[n_in + nc:n_in + nc + n_out]
        passed = refs[n_in + nc + n_out:n_in + 2 * nc + n_out]
        scratch = refs[n_in + 2 * nc + n_out:n_in + 2 * nc + n_out + n_scr]
        send_sems, recv_sems = refs[-2], refs[-1]
        ids = [pl.program_id(ax) for ax in range(len(grid))]
        first = functools.reduce(jnp.logical_and, [i_ == 0 for i_ in ids])
        last = functools.reduce(jnp.logical_and, [i_ == g - 1 for i_, g in zip(ids, grid)])

        @pl.when(first)
        def _():
            for cp in _plan_copies(plan, passed, send_sems, recv_sems):
                cp.start()

        body(*ins, *outs, *scratch)

        @pl.when(last)
        def _():
            for cp in _plan_copies(plan, passed, send_sems, recv_sems):
                cp.wait_send()
                cp.wait_recv()

    any_spec = pl.BlockSpec(memory_space=pl.ANY)
    outs = pl.pallas_call(
        hosted, name=name, grid=grid,
        in_specs=in_specs + [any_spec] * nc, out_specs=out_specs + [any_spec] * nc,
        out_shape=out_shape + [jax.ShapeDtypeStruct(a.shape, a.dtype) for a in arrays],
        scratch_shapes=list(scratch_shapes) + [pltpu.SemaphoreType.DMA((n_copies,)), pltpu.SemaphoreType.DMA((n_copies,))],
        input_output_aliases={n_in + i: n_out + i for i in range(nc)},
        compiler_params=pltpu.CompilerParams(dimension_semantics=("arbitrary",) * len(grid),
                                             vmem_limit_bytes=V7X_VMEM_LIMIT, has_side_effects=True),
    )(*inputs, *arrays)
    return list(outs[:n_out]), list(outs[n_out:])


def _matmul(name, a, b, *, dims, grid, a_spec, b_spec, extras=(), extra_specs=(), out_shape, out_specs,
            acc_shape, epilogue, k_axis=2, sem=("parallel", "parallel", "arbitrary"), comm=None, dot_fn=None):
    nk = grid[k_axis]
    n_extra = len(extras)
    n_out = len(out_shape)
    if dot_fn is None:
        def dot_fn(a_ref, b_ref):
            return _dot(a_ref[...], b_ref[...], dims)

    def body(a_ref, b_ref, *rest):
        extra_refs = rest[:n_extra]
        out_refs = rest[n_extra:n_extra + n_out]
        if nk == 1:
            epilogue(dot_fn(a_ref, b_ref), extra_refs, out_refs)
            return
        acc_ref = rest[n_extra + n_out]
        k = pl.program_id(k_axis)

        @pl.when(k == 0)
        def _():
            acc_ref[...] = jnp.zeros(acc_shape, F32)

        acc_ref[...] += dot_fn(a_ref, b_ref)

        @pl.when(k == nk - 1)
        def _():
            epilogue(acc_ref[...], extra_refs, out_refs)

    outs, passed = _call(
        body, name=name, grid=grid, in_specs=[a_spec, b_spec, *extra_specs], out_specs=out_specs, out_shape=out_shape,
        inputs=[a, b, *extras], scratch_shapes=[] if nk == 1 else [pltpu.VMEM(acc_shape, F32)], sem=sem, comm=comm)
    return outs if comm is None else (outs, passed)


def _store_epilogue(dtype):

    def ep(acc, extra_refs, out_refs):
        out_refs[0][...] = acc.astype(dtype)
    return ep


def _ln_in_fwd(x, g, b, tm, comm=None):
    t, d = x.shape

    def body(x_ref, g_ref, b_ref, h_ref, hb_ref):
        xhat, _ = _ln_stats(x_ref[...])
        h = xhat * g_ref[...] + b_ref[...]
        h_ref[...] = h
        hb_ref[...] = h.astype(BF16)

    row = pl.BlockSpec((tm, d), lambda i: (i, 0))
    vec = pl.BlockSpec((1, d), lambda i: (0, 0))
    return _call(
        body, name="ln_in_fwd", grid=(t // tm,), in_specs=[row, vec, vec], out_specs=[row, row],
        out_shape=[jax.ShapeDtypeStruct((t, d), F32), jax.ShapeDtypeStruct((t, d), BF16)],
        inputs=[x, g, b], sem=("parallel",), comm=comm)


def _split3(x):
    hi = x.astype(BF16)
    r = x - hi.astype(F32)
    mid = r.astype(BF16)
    lo = (r - mid.astype(F32)).astype(BF16)
    return hi, mid, lo


def _split2(x):
    hi = x.astype(BF16)
    lo = (x - hi.astype(F32)).astype(BF16)
    return hi, lo


def _pool_fwd(u, wp, sc, t, c, comm=None):
    n_groups = len(POOL_WINDOWS)
    tc = POOL_CHUNK
    n_chunks = t // tc

    def body(u_ref, wp_ref, sc_ref, y_ref, ypre_ref, xp_ref):
        g = pl.program_id(0)
        xp_ref[pl.ds(0, tc), :] = jnp.zeros((tc, c), F32)
        xp_ref[pl.ds(tc, t), :] = u_ref[...]
        out_i = lax.broadcasted_iota(jnp.int32, (tc, 2 * tc), 0)
        in_j = lax.broadcasted_iota(jnp.int32, (tc, 2 * tc), 1)
        lag = tc + out_i - in_j
        t_in_chunk = lax.broadcasted_iota(jnp.int32, (tc, 1), 0)
        for gi, w in enumerate(POOL_WINDOWS):
            @pl.when(g == gi)
            def _(w=w):
                band = jnp.logical_and(lag >= 0, lag < w).astype(BF16)

                def chunk(ci, carry):
                    start = pl.multiple_of(ci * tc, tc)
                    win = xp_ref[pl.ds(start, 2 * tc), :]
                    hi, mid, lo = _split3(win)
                    wsum = _dot(band, hi) + _dot(band, mid) + _dot(band, lo)
                    cnt = jnp.minimum(ci * tc + t_in_chunk + 1, w).astype(F32)
                    ypre = wsum * (1.0 / cnt) - win[tc:, :]
                    ypre_b = ypre.astype(BF16)
                    y = _dot(ypre_b, wp_ref[...]) * sc_ref[...]
                    ypre_ref[pl.ds(start, tc), :] = ypre_b
                    y_ref[pl.ds(start, tc), :] = y.astype(BF16)
                    return carry

                lax.fori_loop(0, n_chunks, chunk, 0)

    col = pl.BlockSpec((t, c), lambda g: (0, g))
    return _call(
        body, name="pool_fwd", grid=(n_groups,),
        in_specs=[col, pl.BlockSpec((None, c, c), lambda g: (g, 0, 0)), pl.BlockSpec((None, 1, c), lambda g: (g, 0, 0))],
        out_specs=[col, col],
        out_shape=[jax.ShapeDtypeStruct((t, n_groups * c), BF16), jax.ShapeDtypeStruct((t, n_groups * c), BF16)],
        inputs=[u, wp, sc], scratch_shapes=[pltpu.VMEM((t + tc, c), F32)], sem=("parallel",), comm=comm)


def _pool_bwd(dmixin, ypre, wp, sc, t, c, comm=None):
    n_groups = len(POOL_WINDOWS)
    tc = POOL_CHUNK
    n_chunks = t // tc

    def body(dy_ref, ypre_ref, wp_ref, sc_ref, du_ref, dwp_ref, dsc_ref, zp_ref):
        g = pl.program_id(0)
        zp_ref[pl.ds(t, tc), :] = jnp.zeros((tc, c), F32)
        dwp_ref[...] = jnp.zeros((c, c), F32)
        dsc_ref[...] = jnp.zeros((1, c), F32)
        out_i = lax.broadcasted_iota(jnp.int32, (tc, 2 * tc), 0)
        in_j = lax.broadcasted_iota(jnp.int32, (tc, 2 * tc), 1)
        lead = in_j - out_i
        t_in_chunk = lax.broadcasted_iota(jnp.int32, (tc, 1), 0)
        for gi, w in enumerate(POOL_WINDOWS):
            @pl.when(g == gi)
            def _(w=w):
                band = jnp.logical_and(lead >= 0, lead < w).astype(BF16)

                def first(ci, carry):
                    start = pl.multiple_of(ci * tc, tc)
                    dy = dy_ref[pl.ds(start, tc), :]
                    yp = ypre_ref[pl.ds(start, tc), :]
                    ymm = _dot(yp, wp_ref[...])
                    dsc_ref[...] += jnp.sum(dy * ymm, axis=0, keepdims=True)
                    dys_b = (dy * sc_ref[...]).astype(BF16)
                    dwp_ref[...] += _dot(yp, dys_b, TN)
                    dyp = _dot(dys_b, wp_ref[...], NT)
                    cnt = jnp.minimum(ci * tc + t_in_chunk + 1, w).astype(F32)
                    zp_ref[pl.ds(start, tc), :] = dyp * (1.0 / cnt)
                    du_ref[pl.ds(start, tc), :] = -dyp
                    return carry

                lax.fori_loop(0, n_chunks, first, 0)

                def second(ci, carry):
                    start = pl.multiple_of(ci * tc, tc)
                    hi, mid, lo = _split3(zp_ref[pl.ds(start, 2 * tc), :])
                    du_ref[pl.ds(start, tc), :] += _dot(band, hi) + _dot(band, mid) + _dot(band, lo)
                    return carry

                lax.fori_loop(0, n_chunks, second, 0)

    col = pl.BlockSpec((t, c), lambda g: (0, g))
    return _call(
        body, name="pool_bwd", grid=(n_groups,),
        in_specs=[col, col, pl.BlockSpec((None, c, c), lambda g: (g, 0, 0)), pl.BlockSpec((None, 1, c), lambda g: (g, 0, 0))],
        out_specs=[col, pl.BlockSpec((None, c, c), lambda g: (g, 0, 0)), pl.BlockSpec((None, 1, c), lambda g: (g, 0, 0))],
        out_shape=[jax.ShapeDtypeStruct((t, n_groups * c), F32), jax.ShapeDtypeStruct((n_groups, c, c), F32),
                   jax.ShapeDtypeStruct((n_groups, 1, c), F32)],
        inputs=[dmixin, ypre, wp, sc], scratch_shapes=[pltpu.VMEM((t + tc, c), F32)], sem=("parallel",), comm=comm)


ROW_PARTS = 2


def _att_consts():
    b = ATT_BLOCK
    rp = b // ROW_PARTS
    row = lax.broadcasted_iota(jnp.int32, (b, b), 0)
    col = lax.broadcasted_iota(jnp.int32, (b, b), 1)
    tri = (row >= col).astype(BF16)
    prow = lax.broadcasted_iota(jnp.int32, (rp, b), 0)
    pcol = lax.broadcasted_iota(jnp.int32, (rp, b), 1)
    causal = [pcol < prow + r * rp for r in range(ROW_PARTS)]
    return tri, causal


def _suffix_sum(x, tri):
    hi, lo = _split2(x)
    return _dot(hi, tri) + _dot(lo, tri)


LOG2_E = 1.4426950408889634


def _att_scores(qb, kb, mask):
    z2 = _dot(qb, kb, NT) * (LOG2_E / math.sqrt(HEAD_DIM))
    sp2 = jnp.maximum(z2, 0.0) + jnp.log2(1.0 + jnp.exp2(-jnp.abs(z2)))
    return z2, sp2, (sp2 if mask is None else jnp.where(mask, sp2, 0.0))


HEADS_PER_STEP = 2
ATT_LANES = HEADS_PER_STEP * HEAD_DIM


def _head_lanes(s):
    return slice(s * HEAD_DIM, (s + 1) * HEAD_DIM)


UNDERFLOW_LOG2 = 160.0


def _sweep_earlier_blocks(i, state, per_chain, block):
    def lowest(st):
        low = st[0]
        for k in range(per_chain, len(st), per_chain):
            low = jnp.minimum(low, st[k])
        return jnp.min(low)

    def more(c):
        return jnp.logical_and(c[0] < i, c[1] < UNDERFLOW_LOG2)

    def trip(c):
        st = block(i - 1 - c[0], c[2:])
        return (c[0] + 1, lowest(st)) + tuple(st)

    return lax.while_loop(more, trip, (jnp.int32(0), lowest(state)) + tuple(state))[2:]


def _attn_fwd(qkv, t, n_heads, comm=None):
    b = ATT_BLOCK
    nq = t // b
    n_steps = n_heads // HEADS_PER_STEP

    rp = b // ROW_PARTS
    chains = [(s, r) for s in range(HEADS_PER_STEP) for r in range(ROW_PARTS)]
    no_mask = [None] * ROW_PARTS

    def body(q_ref, k_ref, v_ref, o_ref):
        tri, causal = _att_consts()

        def blocks(qbs, j, state, masks):
            ks = pl.multiple_of(j * b, b)
            scores = [_att_scores(qbs[ci], k_ref[pl.ds(ks, b), _head_lanes(s)], masks[r]) for ci, (s, r) in enumerate(chains)]
            incls = [_suffix_sum(sc[2], tri) for sc in scores]
            out = []
            for ci, (s, r) in enumerate(chains):
                carry, acc = state[2 * ci], state[2 * ci + 1]
                a = jnp.exp2(scores[ci][0] - (incls[ci] + carry))
                if masks[r] is not None:
                    a = jnp.where(masks[r], a, 0.0)
                out += [carry + incls[ci][:, 0:1], acc + _dot(a.astype(BF16), v_ref[pl.ds(ks, b), _head_lanes(s)])]
            return tuple(out)

        def q_loop(i, _):
            qs = pl.multiple_of(i * b, b)
            qbs = [q_ref[pl.ds(qs + r * rp, rp), _head_lanes(s)] for s, r in chains]
            zero = (jnp.zeros((rp, 1), F32), jnp.zeros((rp, HEAD_DIM), F32)) * len(chains)
            state = blocks(qbs, i, zero, causal)
            state = _sweep_earlier_blocks(i, state, 2, lambda j, st: blocks(qbs, j, st, no_mask))
            for ci, (s, r) in enumerate(chains):
                o_ref[pl.ds(qs + r * rp, rp), _head_lanes(s)] = state[2 * ci + 1]
            return 0

        lax.fori_loop(0, nq, q_loop, 0)

    def heads(off):
        return pl.BlockSpec((t, ATT_LANES), lambda h: (0, off + h))

    return _call(
        body, name="attn_fwd", grid=(n_steps,),
        in_specs=[heads(0), heads(n_steps), heads(2 * n_steps)], out_specs=[heads(0)],
        out_shape=[jax.ShapeDtypeStruct((t, n_heads * HEAD_DIM), F32)],
        inputs=[qkv, qkv, qkv], sem=("parallel",), comm=comm)


def _attn_bwd(qkv, do, o, t, n_heads, comm=None):
    b = ATT_BLOCK
    nq = t // b
    n_steps = n_heads // HEADS_PER_STEP
    scale = 1.0 / math.sqrt(HEAD_DIM)
    rp = b // ROW_PARTS
    chains = [(s, r) for s in range(HEADS_PER_STEP) for r in range(ROW_PARTS)]
    no_mask = [None] * ROW_PARTS

    def body(q_ref, k_ref, v_ref, do_ref, o_ref, dq_ref, dk_ref, dv_ref, qt_ref, dot_ref, dkt_ref, dvt_ref):
        for j in range(nq):
            rows = pl.ds(j * b, b)
            qt_ref[j] = q_ref[rows, :].astype(F32).T.astype(BF16)
            dot_ref[j] = do_ref[rows, :].astype(F32).T.astype(BF16)
        dkt_ref[...] = jnp.zeros((nq, ATT_LANES, b), F32)
        dvt_ref[...] = jnp.zeros((nq, ATT_LANES, b), F32)
        tri, causal = _att_consts()

        def blocks(i, fixed, j, state, masks):
            ks = pl.multiple_of(j * b, b)
            n = len(chains)
            kbs = [k_ref[pl.ds(ks, b), _head_lanes(s)] for s, _ in chains]
            scores = [_att_scores(fixed[ci][0], kbs[ci], masks[r]) for ci, (s, r) in enumerate(chains)]
            incls = [_suffix_sum(sc[2], tri) for sc in scores]
            das = [_dot(fixed[ci][1], v_ref[pl.ds(ks, b), _head_lanes(s)], NT) for ci, (s, r) in enumerate(chains)]
            a_bs, gs = [], []
            for ci, (s, r) in enumerate(chains):
                a = jnp.exp2(scores[ci][0] - (incls[ci] + state[3 * ci]))
                if masks[r] is not None:
                    a = jnp.where(masks[r], a, 0.0)
                a_bs.append(a.astype(BF16))
                gs.append(a_bs[ci].astype(F32) * das[ci])
            g_incls = [_suffix_sum(g, tri) for g in gs]
            dz_bs = []
            for ci, (s, r) in enumerate(chains):
                rest = (fixed[ci][2] - state[3 * ci + 1]) - (g_incls[ci] - gs[ci])
                sig = jnp.exp2(scores[ci][0] - scores[ci][1])
                dz = (gs[ci] - sig * rest) * scale
                if masks[r] is not None:
                    dz = jnp.where(masks[r], dz, 0.0)
                dz_bs.append(dz.astype(BF16))
            out = []
            for ci in range(n):
                out += [state[3 * ci] + incls[ci][:, 0:1], state[3 * ci + 1] + g_incls[ci][:, 0:1],
                        state[3 * ci + 2] + _dot(dz_bs[ci], kbs[ci])]
            for s in range(HEADS_PER_STEP):
                lanes = _head_lanes(s)
                dk_add, dv_add = None, None
                for ci, (cs, r) in enumerate(chains):
                    if cs == s:
                        part = slice(r * rp, (r + 1) * rp)
                        dk_c = _dot(qt_ref[i, lanes, part], dz_bs[ci])
                        dv_c = _dot(dot_ref[i, lanes, part], a_bs[ci])
                        dk_add = dk_c if dk_add is None else dk_add + dk_c
                        dv_add = dv_c if dv_add is None else dv_add + dv_c
                dkt_ref[j, lanes, :] += dk_add
                dvt_ref[j, lanes, :] += dv_add
            return tuple(out)

        def q_loop(i, _):
            qs = pl.multiple_of(i * b, b)
            fixed = []
            for s, r in chains:
                rows = pl.ds(qs + r * rp, rp)
                dob = do_ref[rows, _head_lanes(s)]
                total = jnp.sum(dob.astype(F32) * o_ref[rows, _head_lanes(s)], axis=-1, keepdims=True)
                fixed.append((q_ref[rows, _head_lanes(s)], dob, total))
            zero = (jnp.zeros((rp, 1), F32), jnp.zeros((rp, 1), F32), jnp.zeros((rp, HEAD_DIM), F32)) * len(chains)
            state = blocks(i, fixed, i, zero, causal)
            state = _sweep_earlier_blocks(i, state, 3, lambda j, st: blocks(i, fixed, j, st, no_mask))
            for ci, (s, r) in enumerate(chains):
                dq_ref[pl.ds(qs + r * rp, rp), _head_lanes(s)] = state[3 * ci + 2].astype(BF16)
            return 0

        lax.fori_loop(0, nq, q_loop, 0)
        for j in range(nq):
            rows = pl.ds(j * b, b)
            dk_ref[rows, :] = dkt_ref[j].T.astype(BF16)
            dv_ref[rows, :] = dvt_ref[j].T.astype(BF16)

    def heads(off):
        return pl.BlockSpec((t, ATT_LANES), lambda h: (0, off + h))

    shape = jax.ShapeDtypeStruct((t, n_heads * HEAD_DIM), BF16)
    return _call(
        body, name="attn_bwd", grid=(n_steps,),
        in_specs=[heads(0), heads(n_steps), heads(2 * n_steps), heads(0), heads(0)],
        out_specs=[heads(0)] * 3, out_shape=[shape] * 3, inputs=[qkv, qkv, qkv, do, o],
        scratch_shapes=[pltpu.VMEM((nq, ATT_LANES, b), BF16)] * 2 + [pltpu.VMEM((nq, ATT_LANES, b), F32)] * 2,
        sem=("parallel",), comm=comm)


def _place():
    x, y, c = lax.axis_index("x"), lax.axis_index("y"), lax.axis_index("c")
    return x, y, c


def _flip(v, on):
    return 1 - v if on else v


def _plan_copies(plan, refs, send_sems, recv_sems):
    return [pltpu.make_async_remote_copy(src_ref=src, dst_ref=dst, send_sem=send_sems.at[k], recv_sem=recv_sems.at[k],
                                         device_id=dev, device_id_type=MESH)
            for k, (src, dst, dev) in enumerate(plan(refs))]


def _copies_now(name, arrays, plan, n_copies):
    n = len(arrays)

    def body(*refs):
        copies = _plan_copies(plan, refs[n:2 * n], refs[2 * n], refs[2 * n + 1])
        for cp in copies:
            cp.start()
        for cp in copies:
            cp.wait_send()
            cp.wait_recv()

    any_spec = pl.BlockSpec(memory_space=pl.ANY)
    return list(pl.pallas_call(
        body, name=name, in_specs=[any_spec] * n, out_specs=[any_spec] * n,
        out_shape=[jax.ShapeDtypeStruct(a.shape, a.dtype) for a in arrays],
        input_output_aliases={i: i for i in range(n)},
        scratch_shapes=[pltpu.SemaphoreType.DMA((n_copies,)), pltpu.SemaphoreType.DMA((n_copies,))],
        compiler_params=pltpu.CompilerParams(has_side_effects=True),
    )(*arrays))


SIBLING, ACROSS_Y, ACROSS_X, DIAGONAL = 1, 2, 4, 6


def _plan_gather_own(peers, rows=None):
    def plan(refs):
        x, y, c = _place()
        mine = refs[0].at[4 * x + 2 * y + c]
        if rows is not None:
            mine = mine.at[pl.ds(*rows)]
        return [(mine, mine, (_flip(x, k & 4), _flip(y, k & 2), _flip(c, k & 1))) for k in peers]
    return plan, len(peers)


def _plan_gather_forward(n):
    def plan(refs):
        x, y, c = _place()
        out = []
        for ti in range(n):
            for r in range(1, 4):
                blk = refs[ti].at[4 * _flip(x, r & 2) + 2 * _flip(y, r & 1) + c]
                out.append((blk, blk, (x, y, 1 - c)))
        return out
    return plan, 3 * n


def _join_plans(*parts):
    def plan(refs):
        out, at = [], 0
        for part, n_arrays, _ in parts:
            out += part(refs[at:at + n_arrays])
            at += n_arrays
        return out
    return plan, sum(n_cp for _, _, n_cp in parts)


def _plan_rs_sibling(n):
    def plan(refs):
        x, y, c = _place()
        out = []
        for ti in range(n):
            for r in range(4):
                src = refs[ti].at[4 * _flip(x, r & 2) + 2 * _flip(y, r & 1) + (1 - c)]
                out.append((src, refs[n + ti].at[r], (x, y, 1 - c)))
        return out
    return plan, 4 * n


def _plan_rs_owner(n, relations=(1, 2, 3)):
    def plan(refs):
        x, y, c = _place()
        out = []
        for ti in range(n):
            for r in relations:
                out.append((refs[ti].at[r], refs[n + ti].at[r], (_flip(x, r & 2), _flip(y, r & 1), c)))
        return out
    return plan, len(relations) * n


def _owner_slots():
    x, y, c = _place()
    idx = []
    for r in range(4):
        ox, oy = (1 - x if r & 2 else x), (1 - y if r & 1 else y)
        idx.append(4 * ox + 2 * oy + c)
    return jnp.stack(idx).astype(jnp.int32)


def _row_tile(rows, cols):
    tr = max(8, min(rows, (1 << 19) // cols))
    while rows % tr:
        tr //= 2
    return tr


def _rs_chip_sum(name, slots, partial, from_sibling):
    _, rows, cols = partial.shape
    tr = _row_tile(rows, cols)

    def body(slots_ref, p_ref, s_ref, o_ref):
        o_ref[...] = (p_ref[...] + s_ref[...]).astype(BF16)

    grid_spec = pltpu.PrefetchScalarGridSpec(
        num_scalar_prefetch=1, grid=(3, rows // tr),
        in_specs=[pl.BlockSpec((None, tr, cols), lambda r, i, s: (s[r + 1], i, 0)),
                  pl.BlockSpec((None, tr, cols), lambda r, i, s: (r + 1, i, 0))],
        out_specs=pl.BlockSpec((None, tr, cols), lambda r, i, s: (r + 1, i, 0)))
    return pl.pallas_call(
        body, name=name, grid_spec=grid_spec, out_shape=jax.ShapeDtypeStruct((4, rows, cols), BF16),
        compiler_params=_cparams(("parallel", "parallel")),
    )(slots, partial, from_sibling)


def _adamw(w, g, m, v):
    m = ADAM_B1 * m + (1.0 - ADAM_B1) * g
    v = ADAM_B2 * v + (1.0 - ADAM_B2) * (g * g)
    m_hat = m / (1.0 - ADAM_B1 ** ADAM_STEP)
    v_hat = v / (1.0 - ADAM_B2 ** ADAM_STEP)
    delta = -ADAM_LR * (m_hat / (jnp.sqrt(v_hat) + ADAM_EPS) + ADAM_WD * w)
    return delta, m, v


def _rs_final_adamw(name, slots, partial, from_sibling, from_chips, w, m, v):
    rows, cols = w.shape
    tr = _row_tile(rows, cols)

    def body(slots_ref, p_ref, s_ref, c1_ref, c2_ref, c3_ref, w_ref, m_ref, v_ref, g_ref, d_ref, nm_ref, nv_ref):
        g = p_ref[...] + s_ref[...]
        g = g + c1_ref[...].astype(F32)
        g = g + c2_ref[...].astype(F32)
        g = g + c3_ref[...].astype(F32)
        delta, nm, nv = _adamw(w_ref[...], g, m_ref[...], v_ref[...])
        g_ref[...] = g
        d_ref[...] = delta
        nm_ref[...] = nm
        nv_ref[...] = nv

    def slot(r):
        return pl.BlockSpec((None, tr, cols), lambda i, s: (r, i, 0))

    flat = pl.BlockSpec((tr, cols), lambda i, s: (i, 0))
    grid_spec = pltpu.PrefetchScalarGridSpec(
        num_scalar_prefetch=1, grid=(rows // tr,),
        in_specs=[pl.BlockSpec((None, tr, cols), lambda i, s: (s[0], i, 0)), slot(0), slot(1), slot(2), slot(3), flat, flat, flat],
        out_specs=[flat] * 4)
    return pl.pallas_call(
        body, name=name, grid_spec=grid_spec, out_shape=[jax.ShapeDtypeStruct((rows, cols), F32)] * 4,
        compiler_params=_cparams(("parallel",)),
    )(slots, partial, from_sibling, from_chips, from_chips, from_chips, w, m, v)


def _small_all_reduce(packet):
    rows, d = packet.shape

    def body(p_ref, sum_ref, loss_ref, all_ref, send_sems, recv_sems):
        x, y, c = _place()
        me = 4 * x + 2 * y + c
        all_ref[me] = p_ref[...]
        copies = []
        for k in range(1, N_DEV):
            px, py, pc = (1 - x if k & 4 else x), (1 - y if k & 2 else y), (1 - c if k & 1 else c)
            cp = pltpu.make_async_remote_copy(
                src_ref=p_ref, dst_ref=all_ref.at[me], send_sem=send_sems.at[k], recv_sem=recv_sems.at[k],
                device_id=(px, py, pc), device_id_type=MESH)
            cp.start()
            copies.append(cp)
        for cp in copies:
            cp.wait_recv()
        for cp in copies:
            cp.wait_send()
        total = all_ref[0]
        for j in range(1, N_DEV):
            total = total + all_ref[j]
        sum_ref[...] = total
        loss_ref[...] = jnp.sum(total[0:1, :], axis=-1, keepdims=True)

    vmem = pl.BlockSpec(memory_space=pltpu.VMEM)
    return pl.pallas_call(
        body, name="small_all_reduce",
        in_specs=[vmem], out_specs=[vmem, vmem],
        out_shape=[jax.ShapeDtypeStruct((rows, d), F32), jax.ShapeDtypeStruct((1, 1), F32)],
        scratch_shapes=[pltpu.VMEM((N_DEV, rows, d), F32), pltpu.SemaphoreType.DMA((N_DEV,)), pltpu.SemaphoreType.DMA((N_DEV,))],
        compiler_params=pltpu.CompilerParams(has_side_effects=True),
    )(packet)


def _small_adamw(w, g, m, v):
    def body(w_ref, g_ref, m_ref, v_ref, d_ref, nm_ref, nv_ref):
        delta, nm, nv = _adamw(w_ref[...], g_ref[...], m_ref[...], v_ref[...])
        d_ref[...] = delta
        nm_ref[...] = nm
        nv_ref[...] = nv

    vmem = pl.BlockSpec(memory_space=pltpu.VMEM)
    return pl.pallas_call(
        body, name="small_adamw", in_specs=[vmem] * 4, out_specs=[vmem] * 3,
        out_shape=[jax.ShapeDtypeStruct(w.shape, F32)] * 3,
    )(w, g, m, v)


def kernel(x, ln_in_g, ln_in_b, w_in, w_pool, pool_scale, w_out, ln1_g, ln1_b, w_ff1, b_ff1, w_ff2, b_ff2, ln2_g, ln2_b, loss_target, m_ln_in_g, m_ln_in_b, m_w_in, m_w_pool, m_pool_scale, m_w_out, m_ln1_g, m_ln1_b, m_w_ff1, m_b_ff1, m_w_ff2, m_b_ff2, m_ln2_g, m_ln2_b, v_ln_in_g, v_ln_in_b, v_w_in, v_w_pool, v_pool_scale, v_w_out, v_ln1_g, v_ln1_b, v_w_ff1, v_b_ff1, v_w_ff2, v_b_ff2, v_ln2_g, v_ln2_b):
    t, d = x.shape[1], x.shape[2]
    n_groups = len(POOL_WINDOWS)
    c_pool = w_pool.shape[3]
    p = n_groups * c_pool
    n_heads = (d - p) // HEAD_DIM
    ws_in = w_in.shape[2]
    n_in = N_DEV * ws_in
    ws_out = w_out.shape[1]
    ws_f = w_ff1.shape[2]
    f = N_DEV * ws_f
    pr = w_pool.shape[2]
    assert n_in == p + 3 * n_heads * HEAD_DIM and N_DEV * ws_out == d and N_DEV * pr == c_pool

    tm_big = min(t, 1024)
    tm_ep = min(t, 512)
    tkk = min(t, 2048)
    half_f = min(ws_f, 512)
    per_f = ws_f // half_f

    x2 = x.reshape(t, d)
    target = loss_target.reshape(t, d)
    g0, b0 = ln_in_g.reshape(1, d), ln_in_b.reshape(1, d)

    shards = [w_in.reshape(d, ws_in), w_out.reshape(ws_out, d), w_ff1.reshape(d, ws_f), w_ff2.reshape(ws_f, d),
              w_pool.reshape(n_groups * pr, c_pool)]
    x_, y_, c_ = _place()
    me = 4 * x_ + 2 * y_ + c_
    def landing(block):
        return lax.dynamic_update_index_in_dim(lax.empty((N_DEV, *block.shape), block.dtype), block, me, 0)

    land_in, land_out, land_1, land_2, land_pool = [landing(s.astype(BF16)) for s in shards]
    land_scale = landing(pool_scale.reshape(n_groups, pr))

    def sds(shape, dtype=F32):
        return jax.ShapeDtypeStruct(shape, dtype)

    vec = pl.BlockSpec((1, d), lambda m, n, k: (0, 0))
    row_ep = pl.BlockSpec((tm_ep, d), lambda m, n, k: (m, 0))
    tm_res = min(t, 256)
    row_res = pl.BlockSpec((tm_res, d), lambda m, n, k: (m, 0))
    seq = ("arbitrary", "arbitrary", "arbitrary")

    two_level = _plan_gather_own([SIBLING, ACROSS_Y, ACROSS_X, DIAGONAL])
    forward = _plan_gather_forward(1)
    first_needed = [land_in, land_pool, land_scale]
    (h0, h0b), first_needed = _ln_in_fwd(
        x2, g0, b0, tm_big, comm=(first_needed, *_join_plans(*[(two_level[0], 1, two_level[1])] * 3)))
    win_g, wpool_g, scale_g = _copies_now("gather_forward_w_in", first_needed, *_plan_gather_forward(3))
    wp_full = wpool_g.reshape(N_DEV, n_groups, pr, c_pool).transpose(1, 0, 2, 3).reshape(n_groups, c_pool, c_pool)
    sc_full = scale_g.transpose(1, 0, 2).reshape(n_groups, 1, c_pool)

    pool_shards = p // ws_in

    def mm_u(name, first, count, dtype, comm=None):
        return _matmul(
            name, h0b, win_g, dims=NN, grid=(t // tm_big, count, 1),
            a_spec=pl.BlockSpec((tm_big, d), lambda m, n, k: (m, 0)),
            b_spec=pl.BlockSpec((None, d, ws_in), lambda m, n, k: (n + first, 0, 0)),
            out_shape=[sds((t, count * ws_in), dtype)],
            out_specs=[pl.BlockSpec((tm_big, ws_in), lambda m, n, k: (m, n))],
            acc_shape=(tm_big, ws_in), epilogue=_store_epilogue(dtype), comm=comm)

    half = land_1.shape[1] // 2
    diag_a, diag_b = _plan_gather_own([DIAGONAL], (0, half)), _plan_gather_own([DIAGONAL], (half, half))
    (u_pool,) = mm_u("mm_u_pool", 0, pool_shards, F32)
    (qkv,), (wout_part, w1_diag) = mm_u(
        "mm_u_qkv", pool_shards, N_DEV - pool_shards, BF16,
        comm=([land_out, land_1], *_join_plans((two_level[0], 1, two_level[1]), (diag_a[0], 1, diag_a[1]))))

    (y_pool, ypre), (wout_g, w1_diag) = _pool_fwd(
        u_pool, wp_full, sc_full, t, c_pool,
        comm=([wout_part, w1_diag], *_join_plans((forward[0], 1, forward[1]), (diag_b[0], 1, diag_b[1]))))
    (o,), (w1_part,) = _attn_fwd(qkv, t, n_heads, comm=([w1_diag], *_plan_gather_own([SIBLING, ACROSS_Y, ACROSS_X])))
    mixin = jnp.concatenate([y_pool, o.astype(BF16)], axis=1)
    wout_2d = wout_g.reshape(d, d)

    def ep_ln1(acc, ex, outs):
        h0_ref, g_ref, b_ref = ex
        r1 = DEEPNORM_ALPHA * h0_ref[...] + acc
        xhat, _ = _ln_stats(r1)
        h1 = xhat * g_ref[...] + b_ref[...]
        outs[0][...] = r1
        outs[1][...] = h1
        outs[2][...] = h1.astype(BF16)

    across = _plan_gather_own([ACROSS_Y, ACROSS_X])
    (r1, h1, h1b), (w1_g, w2_across) = _matmul(
        "mm_mix_ln1", mixin, wout_2d, dims=NN, grid=(t // tm_res, 1, 1),
        a_spec=pl.BlockSpec((tm_res, d), lambda m, n, k: (m, 0)),
        b_spec=pl.BlockSpec((d, d), lambda m, n, k: (0, 0)),
        extras=(h0, ln1_g, ln1_b), extra_specs=(row_res, vec, vec),
        out_shape=[sds((t, d)), sds((t, d)), sds((t, d), BF16)], out_specs=[row_res] * 3,
        acc_shape=(tm_res, d), epilogue=ep_ln1,
        comm=([w1_part, land_2], *_join_plans((forward[0], 1, forward[1]), (across[0], 1, across[1]))))

    def ep_ff1(acc, ex, outs):
        f1 = acc + ex[0][...]
        outs[0][...] = f1
        r = jnp.maximum(f1, 0.0)
        outs[1][...] = (r * r).astype(BF16)

    ff_tile = pl.BlockSpec((tm_big, half_f), lambda m, n, k: (m, n))
    (f1, act), (w2_part,) = _matmul(
        "mm_ff1", h1b, w1_g, dims=NN, grid=(t // tm_big, f // half_f, 1),
        a_spec=pl.BlockSpec((tm_big, d), lambda m, n, k: (m, 0)),
        b_spec=pl.BlockSpec((None, d, half_f), lambda m, n, k: (n // per_f, 0, n % per_f)),
        extras=(b_ff1,), extra_specs=(pl.BlockSpec((1, half_f), lambda m, n, k: (0, n)),),
        out_shape=[sds((t, f)), sds((t, f), BF16)], out_specs=[ff_tile, ff_tile],
        acc_shape=(tm_big, half_f), epilogue=ep_ff1, comm=([w2_across], *_plan_gather_own([SIBLING, DIAGONAL])))
    (w2_g,) = _copies_now("gather_forward_w_ff2", [w2_part], *forward)

    def ep_ln2(acc, ex, outs):
        h1_ref, tgt_ref, bf2_ref, g_ref, b_ref = ex
        dr2_ref, dr2b_ref, dg_ref, db_ref, dbf2_ref, loss_ref = outs
        first = pl.program_id(0) == 0
        r2 = DEEPNORM_ALPHA * h1_ref[...] + (acc + bf2_ref[...])
        xhat, rstd = _ln_stats(r2)
        err = xhat * g_ref[...] + b_ref[...] - tgt_ref[...]
        dr2, dg, db = _ln_bwd(err * (1.0 / d), xhat, rstd, g_ref[...])
        dr2_ref[...] = dr2
        dr2b_ref[...] = dr2.astype(BF16)
        _acc_rows(first, dg_ref, dg)
        _acc_rows(first, db_ref, db)
        _acc_rows(first, dbf2_ref, jnp.sum(dr2, axis=0, keepdims=True))
        _acc_rows(first, loss_ref, jnp.sum(err * err, axis=0, keepdims=True) * (0.5 / d))

    dr2, dr2b, dg2, db2, dbf2, loss_vec = _matmul(
        "mm_ff2_ln2_loss", act, w2_g, dims=NN, grid=(t // tm_ep, 1, N_DEV),
        a_spec=pl.BlockSpec((tm_ep, ws_f), lambda m, n, k: (m, k)),
        b_spec=pl.BlockSpec((None, ws_f, d), lambda m, n, k: (k, 0, 0)),
        extras=(h1, target, b_ff2, ln2_g, ln2_b), extra_specs=(row_ep, row_ep, vec, vec, vec),
        out_shape=[sds((t, d)), sds((t, d), BF16)] + [sds((1, d))] * 4, out_specs=[row_ep, row_ep, vec, vec, vec, vec],
        acc_shape=(tm_ep, d), epilogue=ep_ln2, sem=seq)

    def ep_dff1(acc, ex, outs):
        df1 = acc * (2.0 * jnp.maximum(ex[0][...], 0.0))
        outs[0][...] = df1.astype(BF16)
        _acc_rows(pl.program_id(1) == 0, outs[1], jnp.sum(df1, axis=0, keepdims=True))

    df_tile = pl.BlockSpec((tm_big, ws_f), lambda n, m, k: (m, n))
    df1b, dbf1 = _matmul(
        "mm_dff1", dr2b, w2_g, dims=NT, grid=(N_DEV, t // tm_big, 1),
        a_spec=pl.BlockSpec((tm_big, d), lambda n, m, k: (m, 0)),
        b_spec=pl.BlockSpec((None, ws_f, d), lambda n, m, k: (n, 0, 0)),
        extras=(f1,), extra_specs=(df_tile,),
        out_shape=[sds((t, f), BF16), sds((1, f))], out_specs=[df_tile, pl.BlockSpec((1, ws_f), lambda n, m, k: (0, n))],
        acc_shape=(tm_big, ws_f), epilogue=ep_dff1, sem=("parallel", "arbitrary", "arbitrary"))

    tn_d = min(d, 1024)
    dw2 = _matmul(
        "mm_dw2", act, dr2b, dims=TN, grid=(N_DEV, d // tn_d, t // tkk),
        a_spec=pl.BlockSpec((tkk, ws_f), lambda m, n, k: (k, m)),
        b_spec=pl.BlockSpec((tkk, tn_d), lambda m, n, k: (k, n)),
        out_shape=[sds((N_DEV, ws_f, d))], out_specs=[pl.BlockSpec((None, ws_f, tn_d), lambda m, n, k: (m, 0, n))],
        acc_shape=(ws_f, tn_d), epilogue=_store_epilogue(F32))[0]

    dw1 = _matmul(
        "mm_dw1", h1b, df1b, dims=TN, grid=(d // tn_d, N_DEV, t // tkk),
        a_spec=pl.BlockSpec((tkk, tn_d), lambda m, n, k: (k, m)),
        b_spec=pl.BlockSpec((tkk, ws_f), lambda m, n, k: (k, n)),
        out_shape=[sds((N_DEV, d, ws_f))], out_specs=[pl.BlockSpec((None, tn_d, ws_f), lambda m, n, k: (n, m, 0))],
        acc_shape=(tn_d, ws_f), epilogue=_store_epilogue(F32))[0]

    def ep_ln1_bwd(acc, ex, outs):
        dr2_ref, r1_ref, g_ref = ex
        first = pl.program_id(0) == 0
        xhat, rstd = _ln_stats(r1_ref[...])
        dr1, dg, db = _ln_bwd(DEEPNORM_ALPHA * dr2_ref[...] + acc, xhat, rstd, g_ref[...])
        outs[0][...] = dr1
        outs[1][...] = dr1.astype(BF16)
        _acc_rows(first, outs[2], dg)
        _acc_rows(first, outs[3], db)

    slots = _owner_slots()

    def to_sibling(parts):
        return (parts + [lax.empty((4, *pt.shape[1:]), F32) for pt in parts], *_plan_rs_sibling(len(parts)))

    def to_owner(names_, parts, from_sib):
        sums = [_rs_chip_sum("rs_chip_sum_" + nm, slots, pt, fs) for nm, pt, fs in zip(names_, parts, from_sib)]
        return (sums + [lax.empty(cs.shape, BF16) for cs in sums], *_plan_rs_owner(len(sums)))

    (dr1, dr1b, dg1, db1), (dw1, dw2, sib_1, sib_2) = _matmul(
        "mm_dh1_ln1_bwd", df1b, w1_g, dims=NT, grid=(t // tm_ep, 1, N_DEV),
        a_spec=pl.BlockSpec((tm_ep, ws_f), lambda m, n, k: (m, k)),
        b_spec=pl.BlockSpec((None, d, ws_f), lambda m, n, k: (k, 0, 0)),
        extras=(dr2, r1, ln1_g), extra_specs=(row_ep, row_ep, vec),
        out_shape=[sds((t, d)), sds((t, d), BF16), sds((1, d)), sds((1, d))], out_specs=[row_ep, row_ep, vec, vec],
        acc_shape=(tm_ep, d), epilogue=ep_ln1_bwd, sem=seq,
        comm=to_sibling([dw1, dw2]))
    own_1 = to_owner(["w_ff1"], [dw1], [sib_1])
    own_2 = to_owner(["w_ff2"], [dw2], [sib_2])[0]

    dwout = _matmul(
        "mm_dwout", mixin, dr1b, dims=TN, grid=(d // tn_d, d // tn_d, t // tkk),
        a_spec=pl.BlockSpec((tkk, tn_d), lambda m, n, k: (k, m)),
        b_spec=pl.BlockSpec((tkk, tn_d), lambda m, n, k: (k, n)),
        out_shape=[sds((d, d))], out_specs=[pl.BlockSpec((tn_d, tn_d), lambda m, n, k: (m, n))],
        acc_shape=(tn_d, tn_d), epilogue=_store_epilogue(F32))[0].reshape(N_DEV, ws_out, d)

    tn_mix = min(tn_d, p, d - p)

    def mm_dmixin(name, first, width, dtype, comm=None):
        return _matmul(
            name, dr1b, wout_2d, dims=NT, grid=(t // tm_big, width // tn_mix, 1),
            a_spec=pl.BlockSpec((tm_big, d), lambda m, n, k: (m, 0)),
            b_spec=pl.BlockSpec((tn_mix, d), lambda m, n, k: (n + first // tn_mix, 0)),
            out_shape=[sds((t, width), dtype)], out_specs=[pl.BlockSpec((tm_big, tn_mix), lambda m, n, k: (m, n))],
            acc_shape=(tm_big, tn_mix), epilogue=_store_epilogue(dtype), comm=comm)

    (dy_pool,), (dwout, sib_out) = mm_dmixin("mm_dmixin_pool", 0, p, F32, comm=to_sibling([dwout]))
    (do,) = mm_dmixin("mm_dmixin_att", p, d - p, BF16)

    (du_pool, dwp, dsc), (_, chips_out) = _pool_bwd(
        dy_pool, ypre, wp_full, sc_full, t, c_pool, comm=to_owner(["w_out"], [dwout], [sib_out]))
    (dq, dk, dv), (_, chips_1) = _attn_bwd(qkv, do, o, t, n_heads, comm=own_1)
    dub = jnp.concatenate([du_pool.astype(BF16), dq, dk, dv], axis=1)

    (dwin,), own_2 = _matmul(
        "mm_dwin", h0b, dub, dims=TN, grid=(d // tn_d, N_DEV, t // tkk),
        a_spec=pl.BlockSpec((tkk, tn_d), lambda m, n, k: (k, m)),
        b_spec=pl.BlockSpec((tkk, ws_in), lambda m, n, k: (k, n)),
        out_shape=[sds((N_DEV, d, ws_in))], out_specs=[pl.BlockSpec((None, tn_d, ws_in), lambda m, n, k: (n, m, 0))],
        acc_shape=(tn_d, ws_in), epilogue=_store_epilogue(F32), comm=(own_2, *_plan_rs_owner(1, (1, 2))))
    dwp_g = dwp.reshape(n_groups, N_DEV, pr, c_pool).transpose(1, 0, 2, 3).reshape(N_DEV, n_groups * pr, c_pool)

    def ep_ln0_bwd(acc, ex, outs):
        dr1_ref, x_ref, g_ref = ex
        first = pl.program_id(0) == 0
        xhat, rstd = _ln_stats(x_ref[...])
        dx, dg, db = _ln_bwd(DEEPNORM_ALPHA * dr1_ref[...] + acc, xhat, rstd, g_ref[...])
        outs[0][...] = dx
        _acc_rows(first, outs[1], dg)
        _acc_rows(first, outs[2], db)

    sib_in = to_sibling([dwin, dwp_g])
    to_diagonal = _plan_rs_owner(1, (3,))
    last_host = (own_2 + sib_in[0], *_join_plans((to_diagonal[0], 2, to_diagonal[1]), (sib_in[1], 4, sib_in[2])))
    def two_blocks(a_ref, b_ref):
        return _dot(a_ref[:, :ws_in], b_ref[0], NT) + _dot(a_ref[:, ws_in:], b_ref[1], NT)

    (dx, dg0, db0), (_, chips_2, dwin, dwp_g, sib_in_, sib_p) = _matmul(
        "mm_dh0_ln0_bwd", dub, win_g, dims=NT, grid=(t // tm_ep, 1, N_DEV // 2),
        a_spec=pl.BlockSpec((tm_ep, 2 * ws_in), lambda m, n, k: (m, k)),
        b_spec=pl.BlockSpec((2, d, ws_in), lambda m, n, k: (k, 0, 0)),
        extras=(dr1, x2, g0), extra_specs=(row_ep, row_ep, vec),
        out_shape=[sds((t, d)), sds((1, d)), sds((1, d))], out_specs=[row_ep, vec, vec],
        acc_shape=(tm_ep, d), epilogue=ep_ln0_bwd, sem=seq,
        comm=last_host, dot_fn=two_blocks)

    _, _, chips_in, chips_p = _copies_now("rs_owner_w_in", *to_owner(["w_in", "w_pool"], [dwin, dwp_g], [sib_in_, sib_p]))
    w_of = {"w_in": shards[0], "w_out": shards[1], "w_ff1": shards[2], "w_ff2": shards[3], "w_pool": shards[4]}
    mv_of = {"w_in": (m_w_in, v_w_in), "w_out": (m_w_out, v_w_out), "w_ff1": (m_w_ff1, v_w_ff1),
             "w_ff2": (m_w_ff2, v_w_ff2), "w_pool": (m_w_pool, v_w_pool)}
    big = {}
    for nm, pt, fs, fc in [("w_ff1", dw1, sib_1, chips_1), ("w_ff2", dw2, sib_2, chips_2), ("w_out", dwout, sib_out, chips_out),
                           ("w_in", dwin, sib_in_, chips_in), ("w_pool", dwp_g, sib_p, chips_p)]:
        w2d = w_of[nm]
        m_, v_ = mv_of[nm]
        big[nm] = _rs_final_adamw("rs_final_adamw_" + nm, slots, pt, fs, fc, w2d, m_.reshape(w2d.shape), v_.reshape(w2d.shape))

    n_f_rows = f // d
    pad_sc = d - p
    packet = jnp.concatenate(
        [loss_vec, dg0, db0, dg1, db1, dbf2, dg2, db2, dbf1.reshape(n_f_rows, d),
         jnp.pad(dsc.reshape(1, p), ((0, 0), (0, pad_sc)))], axis=0)
    n_rows = packet.shape[0]
    n_pad = (-n_rows) % 8
    packet = jnp.pad(packet, ((0, n_pad), (0, 0)))
    sums, loss11 = _small_all_reduce(packet)
    dsc_full = sums[8 + n_f_rows, :p].reshape(n_groups, N_DEV, pr)
    dsc_mine = lax.dynamic_index_in_dim(dsc_full, me, axis=1, keepdims=False)

    def sc_row(a):
        return jnp.pad(a.reshape(1, n_groups * pr), ((0, 0), (0, d - n_groups * pr)))

    def small_pack(ln0g, ln0b, l1g, l1b, bf2, l2g, l2b, bf1, sc):
        rows = [jnp.zeros((1, d), F32), ln0g.reshape(1, d), ln0b.reshape(1, d), l1g, l1b, bf2, l2g, l2b,
                bf1.reshape(n_f_rows, d), sc_row(sc), jnp.zeros((n_pad, d), F32)]
        return jnp.concatenate(rows, axis=0)

    w_small = small_pack(ln_in_g, ln_in_b, ln1_g, ln1_b, b_ff2, ln2_g, ln2_b, b_ff1, pool_scale)
    m_small = small_pack(m_ln_in_g, m_ln_in_b, m_ln1_g, m_ln1_b, m_b_ff2, m_ln2_g, m_ln2_b, m_b_ff1, m_pool_scale)
    v_small = small_pack(v_ln_in_g, v_ln_in_b, v_ln1_g, v_ln1_b, v_b_ff2, v_ln2_g, v_ln2_b, v_b_ff1, v_pool_scale)
    g_small = jnp.concatenate([sums[:8 + n_f_rows], sc_row(dsc_mine), jnp.zeros((n_pad, d), F32)], axis=0)
    small = (g_small,) + tuple(_small_adamw(w_small, g_small, m_small, v_small))

    def unpack(a):
        sc = a[8 + n_f_rows, :n_groups * pr].reshape(1, n_groups, pr)
        return {"ln_in_g": a[1], "ln_in_b": a[2], "ln1_g": a[3:4], "ln1_b": a[4:5], "b_ff2": a[5:6], "ln2_g": a[6:7],
                "ln2_b": a[7:8], "b_ff1": a[8:8 + n_f_rows].reshape(1, f), "pool_scale": sc}

    shapes = {"w_in": w_in.shape, "w_out": w_out.shape, "w_ff1": w_ff1.shape, "w_ff2": w_ff2.shape, "w_pool": w_pool.shape}
    order = ["ln_in_g", "ln_in_b", "w_in", "w_pool", "pool_scale", "w_out", "ln1_g", "ln1_b", "w_ff1", "b_ff1", "w_ff2",
             "b_ff2", "ln2_g", "ln2_b"]
    outs = []
    for kind in range(4):
        small_k = unpack(small[kind])
        for nm in order:
            outs.append(big[nm][kind].reshape(shapes[nm]) if nm in big else small_k[nm])
    return (loss11.reshape(()), dx.reshape(x.shape), *outs)
```

```python
import functools
import math

import jax
import jax.numpy as jnp
from jax import lax
from jax.experimental import pallas as pl
from jax.experimental.pallas import tpu as pltpu

F32 = jnp.float32
BF16 = jnp.bfloat16
MESH = pl.DeviceIdType.MESH

N_DEV = 8
HEAD_DIM = 128
POOL_WINDOWS = (2, 4, 8, 16)
DEEPNORM_ALPHA = (2.0 * 1) ** 0.25
LN_EPS = 1e-5
ADAM_LR = 0.001
ADAM_B1 = 0.9
ADAM_B2 = 0.999
ADAM_EPS = 1e-08
ADAM_WD = 0.01
ADAM_STEP = 10

V7X_VMEM_LIMIT = 56 * 1024 * 1024
ATT_BLOCK = 256
POOL_CHUNK = 256

NN = (((1,), (0,)), ((), ()))
NT = (((1,), (1,)), ((), ()))
TN = (((0,), (0,)), ((), ()))


def _dot(a, b, dims=NN):
    return lax.dot_general(a, b, dims, preferred_element_type=F32)


def _cparams(sem=None):
    return pltpu.CompilerParams(dimension_semantics=sem, vmem_limit_bytes=V7X_VMEM_LIMIT)


def _ln_stats(r):
    mu = jnp.mean(r, axis=-1, keepdims=True)
    xc = r - mu
    var = jnp.mean(xc * xc, axis=-1, keepdims=True)
    rstd = lax.rsqrt(var + LN_EPS)
    return xc * rstd, rstd


def _ln_bwd(dy, xhat, rstd, g):
    dxh = dy * g
    m1 = jnp.mean(dxh, axis=-1, keepdims=True)
    m2 = jnp.mean(dxh * xhat, axis=-1, keepdims=True)
    dx = rstd * (dxh - m1 - xhat * m2)
    dg = jnp.sum(dy * xhat, axis=0, keepdims=True)
    db = jnp.sum(dy, axis=0, keepdims=True)
    return dx, dg, db


def _acc_rows(first, ref, val):
    @pl.when(first)
    def _():
        ref[...] = val

    @pl.when(jnp.logical_not(first))
    def _():
        ref[...] += val


def _call(body, *, name, grid, in_specs, out_specs, out_shape, inputs, scratch_shapes=(), sem=None, comm=None):
    in_specs, out_specs, out_shape, inputs = list(in_specs), list(out_specs), list(out_shape), list(inputs)
    if comm is None:
        outs = pl.pallas_call(
            body, name=name, grid=grid, in_specs=in_specs, out_specs=out_specs, out_shape=out_shape,
            scratch_shapes=list(scratch_shapes), compiler_params=_cparams(sem))(*inputs)
        return list(outs), []
    arrays, plan, n_copies = comm
    n_in, n_out, nc, n_scr = len(inputs), len(out_shape), len(arrays), len(scratch_shapes)

    def hosted(*refs):
        ins = refs[:n_in]
        outs = refs[n_in + nc:n_in + nc + n_out]
        passed = refs[n_in + nc + n_out:n_in + 2 * nc + n_out]
        scratch = refs[n_in + 2 * nc + n_out:n_in + 2 * nc + n_out + n_scr]
        send_sems, recv_sems = refs[-2], refs[-1]
        ids = [pl.program_id(ax) for ax in range(len(grid))]
        first = functools.reduce(jnp.logical_and, [i_ == 0 for i_ in ids])
        last = functools.reduce(jnp.logical_and, [i_ == g - 1 for i_, g in zip(ids, grid)])

        @pl.when(first)
        def _():
            for cp in _plan_copies(plan, passed, send_sems, recv_sems):
                cp.start()

        body(*ins, *outs, *scratch)

        @pl.when(last)
        def _():
            for cp in _plan_copies(plan, passed, send_sems, recv_sems):
                cp.wait_send()
                cp.wait_recv()

    any_spec = pl.BlockSpec(memory_space=pl.ANY)
    outs = pl.pallas_call(
        hosted, name=name, grid=grid,
        in_specs=in_specs + [any_spec] * nc, out_specs=out_specs + [any_spec] * nc,
        out_shape=out_shape + [jax.ShapeDtypeStruct(a.shape, a.dtype) for a in arrays],
        scratch_shapes=list(scratch_shapes) + [pltpu.SemaphoreType.DMA((n_copies,)), pltpu.SemaphoreType.DMA((n_copies,))],
        input_output_aliases={n_in + i: n_out + i for i in range(nc)},
        compiler_params=pltpu.CompilerParams(dimension_semantics=("arbitrary",) * len(grid),
                                             vmem_limit_bytes=V7X_VMEM_LIMIT, has_side_effects=True),
    )(*inputs, *arrays)
    return list(outs[:n_out]), list(outs[n_out:])


def _matmul(name, a, b, *, dims, grid, a_spec, b_spec, extras=(), extra_specs=(), out_shape, out_specs,
            acc_shape, epilogue, k_axis=2, sem=("parallel", "parallel", "arbitrary"), comm=None, dot_fn=None):
    nk = grid[k_axis]
    n_extra = len(extras)
    n_out = len(out_shape)
    if dot_fn is None:
        def dot_fn(a_ref, b_ref):
            return _dot(a_ref[...], b_ref[...], dims)

    def body(a_ref, b_ref, *rest):
        extra_refs = rest[:n_extra]
        out_refs = rest[n_extra:n_extra + n_out]
        if nk == 1:
            epilogue(dot_fn(a_ref, b_ref), extra_refs, out_refs)
            return
        acc_ref = rest[n_extra + n_out]
        k = pl.program_id(k_axis)

        @pl.when(k == 0)
        def _():
            acc_ref[...] = jnp.zeros(acc_shape, F32)

        acc_ref[...] += dot_fn(a_ref, b_ref)

        @pl.when(k == nk - 1)
        def _():
            epilogue(acc_ref[...], extra_refs, out_refs)

    outs, passed = _call(
        body, name=name, grid=grid, in_specs=[a_spec, b_spec, *extra_specs], out_specs=out_specs, out_shape=out_shape,
        inputs=[a, b, *extras], scratch_shapes=[] if nk == 1 else [pltpu.VMEM(acc_shape, F32)], sem=sem, comm=comm)
    return outs if comm is None else (outs, passed)


def _store_epilogue(dtype):

    def ep(acc, extra_refs, out_refs):
        out_refs[0][...] = acc.astype(dtype)
    return ep


def _ln_in_fwd(x, g, b, tm, comm=None):
    t, d = x.shape

    def body(x_ref, g_ref, b_ref, h_ref, hb_ref):
        xhat, _ = _ln_stats(x_ref[...])
        h = xhat * g_ref[...] + b_ref[...]
        h_ref[...] = h
        hb_ref[...] = h.astype(BF16)

    row = pl.BlockSpec((tm, d), lambda i: (i, 0))
    vec = pl.BlockSpec((1, d), lambda i: (0, 0))
    return _call(
        body, name="ln_in_fwd", grid=(t // tm,), in_specs=[row, vec, vec], out_specs=[row, row],
        out_shape=[jax.ShapeDtypeStruct((t, d), F32), jax.ShapeDtypeStruct((t, d), BF16)],
        inputs=[x, g, b], sem=("parallel",), comm=comm)


def _split3(x):
    hi = x.astype(BF16)
    r = x - hi.astype(F32)
    mid = r.astype(BF16)
    lo = (r - mid.astype(F32)).astype(BF16)
    return hi, mid, lo


def _split2(x):
    hi = x.astype(BF16)
    lo = (x - hi.astype(F32)).astype(BF16)
    return hi, lo


def _pool_fwd(u, wp, sc, t, c, comm=None):
    n_groups = len(POOL_WINDOWS)
    tc = POOL_CHUNK
    n_chunks = t // tc

    def body(u_ref, wp_ref, sc_ref, y_ref, ypre_ref, xp_ref):
        g = pl.program_id(0)
        xp_ref[pl.ds(0, tc), :] = jnp.zeros((tc, c), F32)
        xp_ref[pl.ds(tc, t), :] = u_ref[...]
        out_i = lax.broadcasted_iota(jnp.int32, (tc, 2 * tc), 0)
        in_j = lax.broadcasted_iota(jnp.int32, (tc, 2 * tc), 1)
        lag = tc + out_i - in_j
        t_in_chunk = lax.broadcasted_iota(jnp.int32, (tc, 1), 0)
        for gi, w in enumerate(POOL_WINDOWS):
            @pl.when(g == gi)
            def _(w=w):
                band = jnp.logical_and(lag >= 0, lag < w).astype(BF16)

                def chunk(ci, carry):
                    start = pl.multiple_of(ci * tc, tc)
                    win = xp_ref[pl.ds(start, 2 * tc), :]
                    hi, mid, lo = _split3(win)
                    wsum = _dot(band, hi) + _dot(band, mid) + _dot(band, lo)
                    cnt = jnp.minimum(ci * tc + t_in_chunk + 1, w).astype(F32)
                    ypre = wsum * (1.0 / cnt) - win[tc:, :]
                    ypre_b = ypre.astype(BF16)
                    y = _dot(ypre_b, wp_ref[...]) * sc_ref[...]
                    ypre_ref[pl.ds(start, tc), :] = ypre_b
                    y_ref[pl.ds(start, tc), :] = y.astype(BF16)
                    return carry

                lax.fori_loop(0, n_chunks, chunk, 0)

    col = pl.BlockSpec((t, c), lambda g: (0, g))
    return _call(
        body, name="pool_fwd", grid=(n_groups,),
        in_specs=[col, pl.BlockSpec((None, c, c), lambda g: (g, 0, 0)), pl.BlockSpec((None, 1, c), lambda g: (g, 0, 0))],
        out_specs=[col, col],
        out_shape=[jax.ShapeDtypeStruct((t, n_groups * c), BF16), jax.ShapeDtypeStruct((t, n_groups * c), BF16)],
        inputs=[u, wp, sc], scratch_shapes=[pltpu.VMEM((t + tc, c), F32)], sem=("parallel",), comm=comm)


def _pool_bwd(dmixin, ypre, wp, sc, t, c, comm=None):
    n_groups = len(POOL_WINDOWS)
    tc = POOL_CHUNK
    n_chunks = t // tc

    def body(dy_ref, ypre_ref, wp_ref, sc_ref, du_ref, dwp_ref, dsc_ref, zp_ref):
        g = pl.program_id(0)
        zp_ref[pl.ds(t, tc), :] = jnp.zeros((tc, c), F32)
        dwp_ref[...] = jnp.zeros((c, c), F32)
        dsc_ref[...] = jnp.zeros((1, c), F32)
        out_i = lax.broadcasted_iota(jnp.int32, (tc, 2 * tc), 0)
        in_j = lax.broadcasted_iota(jnp.int32, (tc, 2 * tc), 1)
        lead = in_j - out_i
        t_in_chunk = lax.broadcasted_iota(jnp.int32, (tc, 1), 0)
        for gi, w in enumerate(POOL_WINDOWS):
            @pl.when(g == gi)
            def _(w=w):
                band = jnp.logical_and(lead >= 0, lead < w).astype(BF16)

                def first(ci, carry):
                    start = pl.multiple_of(ci * tc, tc)
                    dy = dy_ref[pl.ds(start, tc), :]
                    yp = ypre_ref[pl.ds(start, tc), :]
                    ymm = _dot(yp, wp_ref[...])
                    dsc_ref[...] += jnp.sum(dy * ymm, axis=0, keepdims=True)
                    dys_b = (dy * sc_ref[...]).astype(BF16)
                    dwp_ref[...] += _dot(yp, dys_b, TN)
                    dyp = _dot(dys_b, wp_ref[...], NT)
                    cnt = jnp.minimum(ci * tc + t_in_chunk + 1, w).astype(F32)
                    zp_ref[pl.ds(start, tc), :] = dyp * (1.0 / cnt)
                    du_ref[pl.ds(start, tc), :] = -dyp
                    return carry

                lax.fori_loop(0, n_chunks, first, 0)

                def second(ci, carry):
                    start = pl.multiple_of(ci * tc, tc)
                    hi, mid, lo = _split3(zp_ref[pl.ds(start, 2 * tc), :])
                    du_ref[pl.ds(start, tc), :] += _dot(band, hi) + _dot(band, mid) + _dot(band, lo)
                    return carry

                lax.fori_loop(0, n_chunks, second, 0)

    col = pl.BlockSpec((t, c), lambda g: (0, g))
    return _call(
        body, name="pool_bwd", grid=(n_groups,),
        in_specs=[col, col, pl.BlockSpec((None, c, c), lambda g: (g, 0, 0)), pl.BlockSpec((None, 1, c), lambda g: (g, 0, 0))],
        out_specs=[col, pl.BlockSpec((None, c, c), lambda g: (g, 0, 0)), pl.BlockSpec((None, 1, c), lambda g: (g, 0, 0))],
        out_shape=[jax.ShapeDtypeStruct((t, n_groups * c), F32), jax.ShapeDtypeStruct((n_groups, c, c), F32),
                   jax.ShapeDtypeStruct((n_groups, 1, c), F32)],
        inputs=[dmixin, ypre, wp, sc], scratch_shapes=[pltpu.VMEM((t + tc, c), F32)], sem=("parallel",), comm=comm)


ROW_PARTS = 2


def _att_consts():
    b = ATT_BLOCK
    rp = b // ROW_PARTS
    row = lax.broadcasted_iota(jnp.int32, (b, b), 0)
    col = lax.broadcasted_iota(jnp.int32, (b, b), 1)
    tri = (row >= col).astype(BF16)
    prow = lax.broadcasted_iota(jnp.int32, (rp, b), 0)
    pcol = lax.broadcasted_iota(jnp.int32, (rp, b), 1)
    causal = [pcol < prow + r * rp for r in range(ROW_PARTS)]
    return tri, causal


def _suffix_sum(x, tri):
    hi, lo = _split2(x)
    return _dot(hi, tri) + _dot(lo, tri)


LOG2_E = 1.4426950408889634


def _att_scores(qb, kb, mask):
    z2 = _dot(qb, kb, NT) * (LOG2_E / math.sqrt(HEAD_DIM))
    sp2 = jnp.maximum(z2, 0.0) + jnp.log2(1.0 + jnp.exp2(-jnp.abs(z2)))
    return z2, sp2, (sp2 if mask is None else jnp.where(mask, sp2, 0.0))


HEADS_PER_STEP = 2
ATT_LANES = HEADS_PER_STEP * HEAD_DIM


def _head_lanes(s):
    return slice(s * HEAD_DIM, (s + 1) * HEAD_DIM)


UNDERFLOW_LOG2 = 160.0


def _sweep_earlier_blocks(i, state, per_chain, block):
    def lowest(st):
        low = st[0]
        for k in range(per_chain, len(st), per_chain):
            low = jnp.minimum(low, st[k])
        return jnp.min(low)

    def more(c):
        return jnp.logical_and(c[0] < i, c[1] < UNDERFLOW_LOG2)

    def trip(c):
        st = block(i - 1 - c[0], c[2:])
        return (c[0] + 1, lowest(st)) + tuple(st)

    return lax.while_loop(more, trip, (jnp.int32(0), lowest(state)) + tuple(state))[2:]


def _attn_fwd(qkv, t, n_heads, comm=None):
    b = ATT_BLOCK
    nq = t // b
    n_steps = n_heads // HEADS_PER_STEP

    rp = b // ROW_PARTS
    chains = [(s, r) for s in range(HEADS_PER_STEP) for r in range(ROW_PARTS)]
    no_mask = [None] * ROW_PARTS

    def body(q_ref, k_ref, v_ref, o_ref):
        tri, causal = _att_consts()

        def blocks(qbs, j, state, masks):
            ks = pl.multiple_of(j * b, b)
            scores = [_att_scores(qbs[ci], k_ref[pl.ds(ks, b), _head_lanes(s)], masks[r]) for ci, (s, r) in enumerate(chains)]
            incls = [_suffix_sum(sc[2], tri) for sc in scores]
            out = []
            for ci, (s, r) in enumerate(chains):
                carry, acc = state[2 * ci], state[2 * ci + 1]
                a = jnp.exp2(scores[ci][0] - (incls[ci] + carry))
                if masks[r] is not None:
                    a = jnp.where(masks[r], a, 0.0)
                out += [carry + incls[ci][:, 0:1], acc + _dot(a.astype(BF16), v_ref[pl.ds(ks, b), _head_lanes(s)])]
            return tuple(out)

        def q_loop(i, _):
            qs = pl.multiple_of(i * b, b)
            qbs = [q_ref[pl.ds(qs + r * rp, rp), _head_lanes(s)] for s, r in chains]
            zero = (jnp.zeros((rp, 1), F32), jnp.zeros((rp, HEAD_DIM), F32)) * len(chains)
            state = blocks(qbs, i, zero, causal)
            state = _sweep_earlier_blocks(i, state, 2, lambda j, st: blocks(qbs, j, st, no_mask))
            for ci, (s, r) in enumerate(chains):
                o_ref[pl.ds(qs + r * rp, rp), _head_lanes(s)] = state[2 * ci + 1]
            return 0

        lax.fori_loop(0, nq, q_loop, 0)

    def heads(off):
        return pl.BlockSpec((t, ATT_LANES), lambda h: (0, off + h))

    return _call(
        body, name="attn_fwd", grid=(n_steps,),
        in_specs=[heads(0), heads(n_steps), heads(2 * n_steps)], out_specs=[heads(0)],
        out_shape=[jax.ShapeDtypeStruct((t, n_heads * HEAD_DIM), F32)],
        inputs=[qkv, qkv, qkv], sem=("parallel",), comm=comm)


def _attn_bwd(qkv, do, o, t, n_heads, comm=None):
    b = ATT_BLOCK
    nq = t // b
    n_steps = n_heads // HEADS_PER_STEP
    scale = 1.0 / math.sqrt(HEAD_DIM)
    rp = b // ROW_PARTS
    chains = [(s, r) for s in range(HEADS_PER_STEP) for r in range(ROW_PARTS)]
    no_mask = [None] * ROW_PARTS

    def body(q_ref, k_ref, v_ref, do_ref, o_ref, dq_ref, dk_ref, dv_ref, qt_ref, dot_ref, dkt_ref, dvt_ref):
        for j in range(nq):
            rows = pl.ds(j * b, b)
            qt_ref[j] = q_ref[rows, :].astype(F32).T.astype(BF16)
            dot_ref[j] = do_ref[rows, :].astype(F32).T.astype(BF16)
        dkt_ref[...] = jnp.zeros((nq, ATT_LANES, b), F32)
        dvt_ref[...] = jnp.zeros((nq, ATT_LANES, b), F32)
        tri, causal = _att_consts()

        def blocks(i, fixed, j, state, masks):
            ks = pl.multiple_of(j * b, b)
            n = len(chains)
            kbs = [k_ref[pl.ds(ks, b), _head_lanes(s)] for s, _ in chains]
            scores = [_att_scores(fixed[ci][0], kbs[ci], masks[r]) for ci, (s, r) in enumerate(chains)]
            incls = [_suffix_sum(sc[2], tri) for sc in scores]
            das = [_dot(fixed[ci][1], v_ref[pl.ds(ks, b), _head_lanes(s)], NT) for ci, (s, r) in enumerate(chains)]
            a_bs, gs = [], []
            for ci, (s, r) in enumerate(chains):
                a = jnp.exp2(scores[ci][0] - (incls[ci] + state[3 * ci]))
                if masks[r] is not None:
                    a = jnp.where(masks[r], a, 0.0)
                a_bs.append(a.astype(BF16))
                gs.append(a_bs[ci].astype(F32) * das[ci])
            g_incls = [_suffix_sum(g, tri) for g in gs]
            dz_bs = []
            for ci, (s, r) in enumerate(chains):
                rest = (fixed[ci][2] - state[3 * ci + 1]) - (g_incls[ci] - gs[ci])
                sig = jnp.exp2(scores[ci][0] - scores[ci][1])
                dz = (gs[ci] - sig * rest) * scale
                if masks[r] is not None:
                    dz = jnp.where(masks[r], dz, 0.0)
                dz_bs.append(dz.astype(BF16))
            out = []
            for ci in range(n):
                out += [state[3 * ci] + incls[ci][:, 0:1], state[3 * ci + 1] + g_incls[ci][:, 0:1],
                        state[3 * ci + 2] + _dot(dz_bs[ci], kbs[ci])]
            for s in range(HEADS_PER_STEP):
                lanes = _head_lanes(s)
                dk_add, dv_add = None, None
                for ci, (cs, r) in enumerate(chains):
                    if cs == s:
                        part = slice(r * rp, (r + 1) * rp)
                        dk_c = _dot(qt_ref[i, lanes, part], dz_bs[ci])
                        dv_c = _dot(dot_ref[i, lanes, part], a_bs[ci])
                        dk_add = dk_c if dk_add is None else dk_add + dk_c
                        dv_add = dv_c if dv_add is None else dv_add + dv_c
                dkt_ref[j, lanes, :] += dk_add
                dvt_ref[j, lanes, :] += dv_add
            return tuple(out)

        def q_loop(i, _):
            qs = pl.multiple_of(i * b, b)
            fixed = []
            for s, r in chains:
                rows = pl.ds(qs + r * rp, rp)
                dob = do_ref[rows, _head_lanes(s)]
                total = jnp.sum(dob.astype(F32) * o_ref[rows, _head_lanes(s)], axis=-1, keepdims=True)
                fixed.append((q_ref[rows, _head_lanes(s)], dob, total))
            zero = (jnp.zeros((rp, 1), F32), jnp.zeros((rp, 1), F32), jnp.zeros((rp, HEAD_DIM), F32)) * len(chains)
            state = blocks(i, fixed, i, zero, causal)
            state = _sweep_earlier_blocks(i, state, 3, lambda j, st: blocks(i, fixed, j, st, no_mask))
            for ci, (s, r) in enumerate(chains):
                dq_ref[pl.ds(qs + r * rp, rp), _head_lanes(s)] = state[3 * ci + 2].astype(BF16)
            return 0

        lax.fori_loop(0, nq, q_loop, 0)
        for j in range(nq):
            rows = pl.ds(j * b, b)
            dk_ref[rows, :] = dkt_ref[j].T.astype(BF16)
            dv_ref[rows, :] = dvt_ref[j].T.astype(BF16)

    def heads(off):
        return pl.BlockSpec((t, ATT_LANES), lambda h: (0, off + h))

    shape = jax.ShapeDtypeStruct((t, n_heads * HEAD_DIM), BF16)
    return _call(
        body, name="attn_bwd", grid=(n_steps,),
        in_specs=[heads(0), heads(n_steps), heads(2 * n_steps), heads(0), heads(0)],
        out_specs=[heads(0)] * 3, out_shape=[shape] * 3, inputs=[qkv, qkv, qkv, do, o],
        scratch_shapes=[pltpu.VMEM((nq, ATT_LANES, b), BF16)] * 2 + [pltpu.VMEM((nq, ATT_LANES, b), F32)] * 2,
        sem=("parallel",), comm=comm)


def _place():
    x, y, c = lax.axis_index("x"), lax.axis_index("y"), lax.axis_index("c")
    return x, y, c


def _flip(v, on):
    return 1 - v if on else v


def _plan_copies(plan, refs, send_sems, recv_sems):
    return [pltpu.make_async_remote_copy(src_ref=src, dst_ref=dst, send_sem=send_sems.at[k], recv_sem=recv_sems.at[k],
                                         device_id=dev, device_id_type=MESH)
            for k, (src, dst, dev) in enumerate(plan(refs))]


def _copies_now(name, arrays, plan, n_copies):
    n = len(arrays)

    def body(*refs):
        copies = _plan_copies(plan, refs[n:2 * n], refs[2 * n], refs[2 * n + 1])
        for cp in copies:
            cp.start()
        for cp in copies:
            cp.wait_send()
            cp.wait_recv()

    any_spec = pl.BlockSpec(memory_space=pl.ANY)
    return list(pl.pallas_call(
        body, name=name, in_specs=[any_spec] * n, out_specs=[any_spec] * n,
        out_shape=[jax.ShapeDtypeStruct(a.shape, a.dtype) for a in arrays],
        input_output_aliases={i: i for i in range(n)},
        scratch_shapes=[pltpu.SemaphoreType.DMA((n_copies,)), pltpu.SemaphoreType.DMA((n_copies,))],
        compiler_params=pltpu.CompilerParams(has_side_effects=True),
    )(*arrays))


SIBLING, ACROSS_Y, ACROSS_X, DIAGONAL = 1, 2, 4, 6


def _plan_gather_own(peers, rows=None):
    def plan(refs):
        x, y, c = _place()
        mine = refs[0].at[4 * x + 2 * y + c]
        if rows is not None:
            mine = mine.at[pl.ds(*rows)]
        return [(mine, mine, (_flip(x, k & 4), _flip(y, k & 2), _flip(c, k & 1))) for k in peers]
    return plan, len(peers)


def _plan_gather_forward(n):
    def plan(refs):
        x, y, c = _place()
        out = []
        for ti in range(n):
            for r in range(1, 4):
                blk = refs[ti].at[4 * _flip(x, r & 2) + 2 * _flip(y, r & 1) + c]
                out.append((blk, blk, (x, y, 1 - c)))
        return out
    return plan, 3 * n


def _join_plans(*parts):
    def plan(refs):
        out, at = [], 0
        for part, n_arrays, _ in parts:
            out += part(refs[at:at + n_arrays])
            at += n_arrays
        return out
    return plan, sum(n_cp for _, _, n_cp in parts)


def _plan_rs_sibling(n):
    def plan(refs):
        x, y, c = _place()
        out = []
        for ti in range(n):
            for r in range(4):
                src = refs[ti].at[4 * _flip(x, r & 2) + 2 * _flip(y, r & 1) + (1 - c)]
                out.append((src, refs[n + ti].at[r], (x, y, 1 - c)))
        return out
    return plan, 4 * n


def _plan_rs_owner(n, relations=(1, 2, 3)):
    def plan(refs):
        x, y, c = _place()
        out = []
        for ti in range(n):
            for r in relations:
                out.append((refs[ti].at[r], refs[n + ti].at[r], (_flip(x, r & 2), _flip(y, r & 1), c)))
        return out
    return plan, len(relations) * n


def _owner_slots():
    x, y, c = _place()
    idx = []
    for r in range(4):
        ox, oy = (1 - x if r & 2 else x), (1 - y if r & 1 else y)
        idx.append(4 * ox + 2 * oy + c)
    return jnp.stack(idx).astype(jnp.int32)


def _row_tile(rows, cols):
    tr = max(8, min(rows, (1 << 19) // cols))
    while rows % tr:
        tr //= 2
    return tr


def _rs_chip_sum(name, slots, partial, from_sibling):
    _, rows, cols = partial.shape
    tr = _row_tile(rows, cols)

    def body(slots_ref, p_ref, s_ref, o_ref):
        o_ref[...] = (p_ref[...] + s_ref[...]).astype(BF16)

    grid_spec = pltpu.PrefetchScalarGridSpec(
        num_scalar_prefetch=1, grid=(3, rows // tr),
        in_specs=[pl.BlockSpec((None, tr, cols), lambda r, i, s: (s[r + 1], i, 0)),
                  pl.BlockSpec((None, tr, cols), lambda r, i, s: (r + 1, i, 0))],
        out_specs=pl.BlockSpec((None, tr, cols), lambda r, i, s: (r + 1, i, 0)))
    return pl.pallas_call(
        body, name=name, grid_spec=grid_spec, out_shape=jax.ShapeDtypeStruct((4, rows, cols), BF16),
        compiler_params=_cparams(("parallel", "parallel")),
    )(slots, partial, from_sibling)


def _adamw(w, g, m, v):
    m = ADAM_B1 * m + (1.0 - ADAM_B1) * g
    v = ADAM_B2 * v + (1.0 - ADAM_B2) * (g * g)
    m_hat = m / (1.0 - ADAM_B1 ** ADAM_STEP)
    v_hat = v / (1.0 - ADAM_B2 ** ADAM_STEP)
    delta = -ADAM_LR * (m_hat / (jnp.sqrt(v_hat) + ADAM_EPS) + ADAM_WD * w)
    return delta, m, v


def _rs_final_adamw(name, slots, partial, from_sibling, from_chips, w, m, v):
    rows, cols = w.shape
    tr = _row_tile(rows, cols)

    def body(slots_ref, p_ref, s_ref, c1_ref, c2_ref, c3_ref, w_ref, m_ref, v_ref, g_ref, d_ref, nm_ref, nv_ref):
        g = p_ref[...] + s_ref[...]
        g = g + c1_ref[...].astype(F32)
        g = g + c2_ref[...].astype(F32)
        g = g + c3_ref[...].astype(F32)
        delta, nm, nv = _adamw(w_ref[...], g, m_ref[...], v_ref[...])
        g_ref[...] = g
        d_ref[...] = delta
        nm_ref[...] = nm
        nv_ref[...] = nv

    def slot(r):
        return pl.BlockSpec((None, tr, cols), lambda i, s: (r, i, 0))

    flat = pl.BlockSpec((tr, cols), lambda i, s: (i, 0))
    grid_spec = pltpu.PrefetchScalarGridSpec(
        num_scalar_prefetch=1, grid=(rows // tr,),
        in_specs=[pl.BlockSpec((None, tr, cols), lambda i, s: (s[0], i, 0)), slot(0), slot(1), slot(2), slot(3), flat, flat, flat],
        out_specs=[flat] * 4)
    return pl.pallas_call(
        body, name=name, grid_spec=grid_spec, out_shape=[jax.ShapeDtypeStruct((rows, cols), F32)] * 4,
        compiler_params=_cparams(("parallel",)),
    )(slots, partial, from_sibling, from_chips, from_chips, from_chips, w, m, v)


def _small_all_reduce(packet):
    rows, d = packet.shape

    def body(p_ref, sum_ref, loss_ref, all_ref, send_sems, recv_sems):
        x, y, c = _place()
        me = 4 * x + 2 * y + c
        all_ref[me] = p_ref[...]
        copies = []
        for k in range(1, N_DEV):
            px, py, pc = (1 - x if k & 4 else x), (1 - y if k & 2 else y), (1 - c if k & 1 else c)
            cp = pltpu.make_async_remote_copy(
                src_ref=p_ref, dst_ref=all_ref.at[me], send_sem=send_sems.at[k], recv_sem=recv_sems.at[k],
                device_id=(px, py, pc), device_id_type=MESH)
            cp.start()
            copies.append(cp)
        for cp in copies:
            cp.wait_recv()
        for cp in copies:
            cp.wait_send()
        total = all_ref[0]
        for j in range(1, N_DEV):
            total = total + all_ref[j]
        sum_ref[...] = total
        loss_ref[...] = jnp.sum(total[0:1, :], axis=-1, keepdims=True)

    vmem = pl.BlockSpec(memory_space=pltpu.VMEM)
    return pl.pallas_call(
        body, name="small_all_reduce",
        in_specs=[vmem], out_specs=[vmem, vmem],
        out_shape=[jax.ShapeDtypeStruct((rows, d), F32), jax.ShapeDtypeStruct((1, 1), F32)],
        scratch_shapes=[pltpu.VMEM((N_DEV, rows, d), F32), pltpu.SemaphoreType.DMA((N_DEV,)), pltpu.SemaphoreType.DMA((N_DEV,))],
        compiler_params=pltpu.CompilerParams(has_side_effects=True),
    )(packet)


def _small_adamw(w, g, m, v):
    def body(w_ref, g_ref, m_ref, v_ref, d_ref, nm_ref, nv_ref):
        delta, nm, nv = _adamw(w_ref[...], g_ref[...], m_ref[...], v_ref[...])
        d_ref[...] = delta
        nm_ref[...] = nm
        nv_ref[...] = nv

    vmem = pl.BlockSpec(memory_space=pltpu.VMEM)
    return pl.pallas_call(
        body, name="small_adamw", in_specs=[vmem] * 4, out_specs=[vmem] * 3,
        out_shape=[jax.ShapeDtypeStruct(w.shape, F32)] * 3,
    )(w, g, m, v)


def kernel(x, ln_in_g, ln_in_b, w_in, w_pool, pool_scale, w_out, ln1_g, ln1_b, w_ff1, b_ff1, w_ff2, b_ff2, ln2_g, ln2_b, loss_target, m_ln_in_g, m_ln_in_b, m_w_in, m_w_pool, m_pool_scale, m_w_out, m_ln1_g, m_ln1_b, m_w_ff1, m_b_ff1, m_w_ff2, m_b_ff2, m_ln2_g, m_ln2_b, v_ln_in_g, v_ln_in_b, v_w_in, v_w_pool, v_pool_scale, v_w_out, v_ln1_g, v_ln1_b, v_w_ff1, v_b_ff1, v_w_ff2, v_b_ff2, v_ln2_g, v_ln2_b):
    t, d = x.shape[1], x.shape[2]
    n_groups = len(POOL_WINDOWS)
    c_pool = w_pool.shape[3]
    p = n_groups * c_pool
    n_heads = (d - p) // HEAD_DIM
    ws_in = w_in.shape[2]
    n_in = N_DEV * ws_in
    ws_out = w_out.shape[1]
    ws_f = w_ff1.shape[2]
    f = N_DEV * ws_f
    pr = w_pool.shape[2]
    assert n_in == p + 3 * n_heads * HEAD_DIM and N_DEV * ws_out == d and N_DEV * pr == c_pool

    tm_big = min(t, 1024)
    tm_ep = min(t, 512)
    tkk = min(t, 2048)
    half_f = min(ws_f, 512)
    per_f = ws_f // half_f

    x2 = x.reshape(t, d)
    target = loss_target.reshape(t, d)
    g0, b0 = ln_in_g.reshape(1, d), ln_in_b.reshape(1, d)

    shards = [w_in.reshape(d, ws_in), w_out.reshape(ws_out, d), w_ff1.reshape(d, ws_f), w_ff2.reshape(ws_f, d),
              w_pool.reshape(n_groups * pr, c_pool)]
    x_, y_, c_ = _place()
    me = 4 * x_ + 2 * y_ + c_
    def landing(block):
        return lax.dynamic_update_index_in_dim(lax.empty((N_DEV, *block.shape), block.dtype), block, me, 0)

    land_in, land_out, land_1, land_2, land_pool = [landing(s.astype(BF16)) for s in shards]
    land_scale = landing(pool_scale.reshape(n_groups, pr))

    def sds(shape, dtype=F32):
        return jax.ShapeDtypeStruct(shape, dtype)

    vec = pl.BlockSpec((1, d), lambda m, n, k: (0, 0))
    row_ep = pl.BlockSpec((tm_ep, d), lambda m, n, k: (m, 0))
    tm_res = min(t, 256)
    row_res = pl.BlockSpec((tm_res, d), lambda m, n, k: (m, 0))
    seq = ("arbitrary", "arbitrary", "arbitrary")

    two_level = _plan_gather_own([SIBLING, ACROSS_Y, ACROSS_X, DIAGONAL])
    forward = _plan_gather_forward(1)
    first_needed = [land_in, land_pool, land_scale]
    (h0, h0b), first_needed = _ln_in_fwd(
        x2, g0, b0, tm_big, comm=(first_needed, *_join_plans(*[(two_level[0], 1, two_level[1])] * 3)))
    win_g, wpool_g, scale_g = _copies_now("gather_forward_w_in", first_needed, *_plan_gather_forward(3))
    wp_full = wpool_g.reshape(N_DEV, n_groups, pr, c_pool).transpose(1, 0, 2, 3).reshape(n_groups, c_pool, c_pool)
    sc_full = scale_g.transpose(1, 0, 2).reshape(n_groups, 1, c_pool)

    pool_shards = p // ws_in

    def mm_u(name, first, count, dtype, comm=None):
        return _matmul(
            name, h0b, win_g, dims=NN, grid=(t // tm_big, count, 1),
            a_spec=pl.BlockSpec((tm_big, d), lambda m, n, k: (m, 0)),
            b_spec=pl.BlockSpec((None, d, ws_in), lambda m, n, k: (n + first, 0, 0)),
            out_shape=[sds((t, count * ws_in), dtype)],
            out_specs=[pl.BlockSpec((tm_big, ws_in), lambda m, n, k: (m, n))],
            acc_shape=(tm_big, ws_in), epilogue=_store_epilogue(dtype), comm=comm)

    half = land_1.shape[1] // 2
    diag_a, diag_b = _plan_gather_own([DIAGONAL], (0, half)), _plan_gather_own([DIAGONAL], (half, half))
    (u_pool,), (w1_diag,) = mm_u("mm_u_pool", 0, pool_shards, F32, comm=([land_1], *diag_a))
    (qkv,), (wout_part,) = mm_u("mm_u_qkv", pool_shards, N_DEV - pool_shards, BF16, comm=([land_out], *two_level))

    (y_pool, ypre), (wout_g, w1_diag) = _pool_fwd(
        u_pool, wp_full, sc_full, t, c_pool,
        comm=([wout_part, w1_diag], *_join_plans((forward[0], 1, forward[1]), (diag_b[0], 1, diag_b[1]))))
    (o,), (w1_part,) = _attn_fwd(qkv, t, n_heads, comm=([w1_diag], *_plan_gather_own([SIBLING, ACROSS_Y, ACROSS_X])))
    mixin = jnp.concatenate([y_pool, o.astype(BF16)], axis=1)
    wout_2d = wout_g.reshape(d, d)

    def ep_ln1(acc, ex, outs):
        h0_ref, g_ref, b_ref = ex
        r1 = DEEPNORM_ALPHA * h0_ref[...] + acc
        xhat, _ = _ln_stats(r1)
        h1 = xhat * g_ref[...] + b_ref[...]
        outs[0][...] = r1
        outs[1][...] = h1
        outs[2][...] = h1.astype(BF16)

    (r1, h1, h1b), (w1_g,) = _matmul(
        "mm_mix_ln1", mixin, wout_2d, dims=NN, grid=(t // tm_res, 1, 1),
        a_spec=pl.BlockSpec((tm_res, d), lambda m, n, k: (m, 0)),
        b_spec=pl.BlockSpec((d, d), lambda m, n, k: (0, 0)),
        extras=(h0, ln1_g, ln1_b), extra_specs=(row_res, vec, vec),
        out_shape=[sds((t, d)), sds((t, d)), sds((t, d), BF16)], out_specs=[row_res] * 3,
        acc_shape=(tm_res, d), epilogue=ep_ln1, comm=([w1_part], *forward))

    def ep_ff1(acc, ex, outs):
        f1 = acc + ex[0][...]
        outs[0][...] = f1
        r = jnp.maximum(f1, 0.0)
        outs[1][...] = (r * r).astype(BF16)

    ff_tile = pl.BlockSpec((tm_big, half_f), lambda m, n, k: (m, n))
    (f1, act), (w2_part,) = _matmul(
        "mm_ff1", h1b, w1_g, dims=NN, grid=(t // tm_big, f // half_f, 1),
        a_spec=pl.BlockSpec((tm_big, d), lambda m, n, k: (m, 0)),
        b_spec=pl.BlockSpec((None, d, half_f), lambda m, n, k: (n // per_f, 0, n % per_f)),
        extras=(b_ff1,), extra_specs=(pl.BlockSpec((1, half_f), lambda m, n, k: (0, n)),),
        out_shape=[sds((t, f)), sds((t, f), BF16)], out_specs=[ff_tile, ff_tile],
        acc_shape=(tm_big, half_f), epilogue=ep_ff1, comm=([land_2], *two_level))
    (w2_g,) = _copies_now("gather_forward_w_ff2", [w2_part], *forward)

    def ep_ln2(acc, ex, outs):
        h1_ref, tgt_ref, bf2_ref, g_ref, b_ref = ex
        dr2_ref, dr2b_ref, dg_ref, db_ref, dbf2_ref, loss_ref = outs
        first = pl.program_id(0) == 0
        r2 = DEEPNORM_ALPHA * h1_ref[...] + (acc + bf2_ref[...])
        xhat, rstd = _ln_stats(r2)
        err = xhat * g_ref[...] + b_ref[...] - tgt_ref[...]
        dr2, dg, db = _ln_bwd(err * (1.0 / d), xhat, rstd, g_ref[...])
        dr2_ref[...] = dr2
        dr2b_ref[...] = dr2.astype(BF16)
        _acc_rows(first, dg_ref, dg)
        _acc_rows(first, db_ref, db)
        _acc_rows(first, dbf2_ref, jnp.sum(dr2, axis=0, keepdims=True))
        _acc_rows(first, loss_ref, jnp.sum(err * err, axis=0, keepdims=True) * (0.5 / d))

    dr2, dr2b, dg2, db2, dbf2, loss_vec = _matmul(
        "mm_ff2_ln2_loss", act, w2_g, dims=NN, grid=(t // tm_ep, 1, N_DEV),
        a_spec=pl.BlockSpec((tm_ep, ws_f), lambda m, n, k: (m, k)),
        b_spec=pl.BlockSpec((None, ws_f, d), lambda m, n, k: (k, 0, 0)),
        extras=(h1, target, b_ff2, ln2_g, ln2_b), extra_specs=(row_ep, row_ep, vec, vec, vec),
        out_shape=[sds((t, d)), sds((t, d), BF16)] + [sds((1, d))] * 4, out_specs=[row_ep, row_ep, vec, vec, vec, vec],
        acc_shape=(tm_ep, d), epilogue=ep_ln2, sem=seq)

    def ep_dff1(acc, ex, outs):
        df1 = acc * (2.0 * jnp.maximum(ex[0][...], 0.0))
        outs[0][...] = df1.astype(BF16)
        _acc_rows(pl.program_id(1) == 0, outs[1], jnp.sum(df1, axis=0, keepdims=True))

    df_tile = pl.BlockSpec((tm_big, ws_f), lambda n, m, k: (m, n))
    df1b, dbf1 = _matmul(
        "mm_dff1", dr2b, w2_g, dims=NT, grid=(N_DEV, t // tm_big, 1),
        a_spec=pl.BlockSpec((tm_big, d), lambda n, m, k: (m, 0)),
        b_spec=pl.BlockSpec((None, ws_f, d), lambda n, m, k: (n, 0, 0)),
        extras=(f1,), extra_specs=(df_tile,),
        out_shape=[sds((t, f), BF16), sds((1, f))], out_specs=[df_tile, pl.BlockSpec((1, ws_f), lambda n, m, k: (0, n))],
        acc_shape=(tm_big, ws_f), epilogue=ep_dff1, sem=("parallel", "arbitrary", "arbitrary"))

    tn_d = min(d, 1024)
    dw2 = _matmul(
        "mm_dw2", act, dr2b, dims=TN, grid=(N_DEV, d // tn_d, t // tkk),
        a_spec=pl.BlockSpec((tkk, ws_f), lambda m, n, k: (k, m)),
        b_spec=pl.BlockSpec((tkk, tn_d), lambda m, n, k: (k, n)),
        out_shape=[sds((N_DEV, ws_f, d))], out_specs=[pl.BlockSpec((None, ws_f, tn_d), lambda m, n, k: (m, 0, n))],
        acc_shape=(ws_f, tn_d), epilogue=_store_epilogue(F32))[0]

    dw1 = _matmul(
        "mm_dw1", h1b, df1b, dims=TN, grid=(d // tn_d, N_DEV, t // tkk),
        a_spec=pl.BlockSpec((tkk, tn_d), lambda m, n, k: (k, m)),
        b_spec=pl.BlockSpec((tkk, ws_f), lambda m, n, k: (k, n)),
        out_shape=[sds((N_DEV, d, ws_f))], out_specs=[pl.BlockSpec((None, tn_d, ws_f), lambda m, n, k: (n, m, 0))],
        acc_shape=(tn_d, ws_f), epilogue=_store_epilogue(F32))[0]

    def ep_ln1_bwd(acc, ex, outs):
        dr2_ref, r1_ref, g_ref = ex
        first = pl.program_id(0) == 0
        xhat, rstd = _ln_stats(r1_ref[...])
        dr1, dg, db = _ln_bwd(DEEPNORM_ALPHA * dr2_ref[...] + acc, xhat, rstd, g_ref[...])
        outs[0][...] = dr1
        outs[1][...] = dr1.astype(BF16)
        _acc_rows(first, outs[2], dg)
        _acc_rows(first, outs[3], db)

    slots = _owner_slots()

    def to_sibling(parts):
        return (parts + [lax.empty((4, *pt.shape[1:]), F32) for pt in parts], *_plan_rs_sibling(len(parts)))

    def to_owner(names_, parts, from_sib):
        sums = [_rs_chip_sum("rs_chip_sum_" + nm, slots, pt, fs) for nm, pt, fs in zip(names_, parts, from_sib)]
        return (sums + [lax.empty(cs.shape, BF16) for cs in sums], *_plan_rs_owner(len(sums)))

    (dr1, dr1b, dg1, db1), (dw1, dw2, sib_1, sib_2) = _matmul(
        "mm_dh1_ln1_bwd", df1b, w1_g, dims=NT, grid=(t // tm_ep, 1, N_DEV),
        a_spec=pl.BlockSpec((tm_ep, ws_f), lambda m, n, k: (m, k)),
        b_spec=pl.BlockSpec((None, d, ws_f), lambda m, n, k: (k, 0, 0)),
        extras=(dr2, r1, ln1_g), extra_specs=(row_ep, row_ep, vec),
        out_shape=[sds((t, d)), sds((t, d), BF16), sds((1, d)), sds((1, d))], out_specs=[row_ep, row_ep, vec, vec],
        acc_shape=(tm_ep, d), epilogue=ep_ln1_bwd, sem=seq,
        comm=to_sibling([dw1, dw2]))
    own_1 = to_owner(["w_ff1"], [dw1], [sib_1])
    own_2 = to_owner(["w_ff2"], [dw2], [sib_2])[0]

    dwout = _matmul(
        "mm_dwout", mixin, dr1b, dims=TN, grid=(d // tn_d, d // tn_d, t // tkk),
        a_spec=pl.BlockSpec((tkk, tn_d), lambda m, n, k: (k, m)),
        b_spec=pl.BlockSpec((tkk, tn_d), lambda m, n, k: (k, n)),
        out_shape=[sds((d, d))], out_specs=[pl.BlockSpec((tn_d, tn_d), lambda m, n, k: (m, n))],
        acc_shape=(tn_d, tn_d), epilogue=_store_epilogue(F32))[0].reshape(N_DEV, ws_out, d)

    tn_mix = min(tn_d, p, d - p)

    def mm_dmixin(name, first, width, dtype, comm=None):
        return _matmul(
            name, dr1b, wout_2d, dims=NT, grid=(t // tm_big, width // tn_mix, 1),
            a_spec=pl.BlockSpec((tm_big, d), lambda m, n, k: (m, 0)),
            b_spec=pl.BlockSpec((tn_mix, d), lambda m, n, k: (n + first // tn_mix, 0)),
            out_shape=[sds((t, width), dtype)], out_specs=[pl.BlockSpec((tm_big, tn_mix), lambda m, n, k: (m, n))],
            acc_shape=(tm_big, tn_mix), epilogue=_store_epilogue(dtype), comm=comm)

    (dy_pool,), (dwout, sib_out) = mm_dmixin("mm_dmixin_pool", 0, p, F32, comm=to_sibling([dwout]))
    (do,) = mm_dmixin("mm_dmixin_att", p, d - p, BF16)

    (du_pool, dwp, dsc), (_, chips_out) = _pool_bwd(
        dy_pool, ypre, wp_full, sc_full, t, c_pool, comm=to_owner(["w_out"], [dwout], [sib_out]))
    (dq, dk, dv), (_, chips_1) = _attn_bwd(qkv, do, o, t, n_heads, comm=own_1)
    dub = jnp.concatenate([du_pool.astype(BF16), dq, dk, dv], axis=1)

    (dwin,), own_2 = _matmul(
        "mm_dwin", h0b, dub, dims=TN, grid=(d // tn_d, N_DEV, t // tkk),
        a_spec=pl.BlockSpec((tkk, tn_d), lambda m, n, k: (k, m)),
        b_spec=pl.BlockSpec((tkk, ws_in), lambda m, n, k: (k, n)),
        out_shape=[sds((N_DEV, d, ws_in))], out_specs=[pl.BlockSpec((None, tn_d, ws_in), lambda m, n, k: (n, m, 0))],
        acc_shape=(tn_d, ws_in), epilogue=_store_epilogue(F32), comm=(own_2, *_plan_rs_owner(1, (1, 2))))
    dwp_g = dwp.reshape(n_groups, N_DEV, pr, c_pool).transpose(1, 0, 2, 3).reshape(N_DEV, n_groups * pr, c_pool)

    def ep_ln0_bwd(acc, ex, outs):
        dr1_ref, x_ref, g_ref = ex
        first = pl.program_id(0) == 0
        xhat, rstd = _ln_stats(x_ref[...])
        dx, dg, db = _ln_bwd(DEEPNORM_ALPHA * dr1_ref[...] + acc, xhat, rstd, g_ref[...])
        outs[0][...] = dx
        _acc_rows(first, outs[1], dg)
        _acc_rows(first, outs[2], db)

    sib_in = to_sibling([dwin, dwp_g])
    to_diagonal = _plan_rs_owner(1, (3,))
    last_host = (own_2 + sib_in[0], *_join_plans((to_diagonal[0], 2, to_diagonal[1]), (sib_in[1], 4, sib_in[2])))
    def two_blocks(a_ref, b_ref):
        return _dot(a_ref[:, :ws_in], b_ref[0], NT) + _dot(a_ref[:, ws_in:], b_ref[1], NT)

    (dx, dg0, db0), (_, chips_2, dwin, dwp_g, sib_in_, sib_p) = _matmul(
        "mm_dh0_ln0_bwd", dub, win_g, dims=NT, grid=(t // tm_ep, 1, N_DEV // 2),
        a_spec=pl.BlockSpec((tm_ep, 2 * ws_in), lambda m, n, k: (m, k)),
        b_spec=pl.BlockSpec((2, d, ws_in), lambda m, n, k: (k, 0, 0)),
        extras=(dr1, x2, g0), extra_specs=(row_ep, row_ep, vec),
        out_shape=[sds((t, d)), sds((1, d)), sds((1, d))], out_specs=[row_ep, vec, vec],
        acc_shape=(tm_ep, d), epilogue=ep_ln0_bwd, sem=seq,
        comm=last_host, dot_fn=two_blocks)

    _, _, chips_in, chips_p = _copies_now("rs_owner_w_in", *to_owner(["w_in", "w_pool"], [dwin, dwp_g], [sib_in_, sib_p]))
    w_of = {"w_in": shards[0], "w_out": shards[1], "w_ff1": shards[2], "w_ff2": shards[3], "w_pool": shards[4]}
    mv_of = {"w_in": (m_w_in, v_w_in), "w_out": (m_w_out, v_w_out), "w_ff1": (m_w_ff1, v_w_ff1),
             "w_ff2": (m_w_ff2, v_w_ff2), "w_pool": (m_w_pool, v_w_pool)}
    big = {}
    for nm, pt, fs, fc in [("w_ff1", dw1, sib_1, chips_1), ("w_ff2", dw2, sib_2, chips_2), ("w_out", dwout, sib_out, chips_out),
                           ("w_in", dwin, sib_in_, chips_in), ("w_pool", dwp_g, sib_p, chips_p)]:
        w2d = w_of[nm]
        m_, v_ = mv_of[nm]
        big[nm] = _rs_final_adamw("rs_final_adamw_" + nm, slots, pt, fs, fc, w2d, m_.reshape(w2d.shape), v_.reshape(w2d.shape))

    n_f_rows = f // d
    pad_sc = d - p
    packet = jnp.concatenate(
        [loss_vec, dg0, db0, dg1, db1, dbf2, dg2, db2, dbf1.reshape(n_f_rows, d),
         jnp.pad(dsc.reshape(1, p), ((0, 0), (0, pad_sc)))], axis=0)
    n_rows = packet.shape[0]
    n_pad = (-n_rows) % 8
    packet = jnp.pad(packet, ((0, n_pad), (0, 0)))
    sums, loss11 = _small_all_reduce(packet)
    dsc_full = sums[8 + n_f_rows, :p].reshape(n_groups, N_DEV, pr)
    dsc_mine = lax.dynamic_index_in_dim(dsc_full, me, axis=1, keepdims=False)

    def sc_row(a):
        return jnp.pad(a.reshape(1, n_groups * pr), ((0, 0), (0, d - n_groups * pr)))

    def small_pack(ln0g, ln0b, l1g, l1b, bf2, l2g, l2b, bf1, sc):
        rows = [jnp.zeros((1, d), F32), ln0g.reshape(1, d), ln0b.reshape(1, d), l1g, l1b, bf2, l2g, l2b,
                bf1.reshape(n_f_rows, d), sc_row(sc), jnp.zeros((n_pad, d), F32)]
        return jnp.concatenate(rows, axis=0)

    w_small = small_pack(ln_in_g, ln_in_b, ln1_g, ln1_b, b_ff2, ln2_g, ln2_b, b_ff1, pool_scale)
    m_small = small_pack(m_ln_in_g, m_ln_in_b, m_ln1_g, m_ln1_b, m_b_ff2, m_ln2_g, m_ln2_b, m_b_ff1, m_pool_scale)
    v_small = small_pack(v_ln_in_g, v_ln_in_b, v_ln1_g, v_ln1_b, v_b_ff2, v_ln2_g, v_ln2_b, v_b_ff1, v_pool_scale)
    g_small = jnp.concatenate([sums[:8 + n_f_rows], sc_row(dsc_mine), jnp.zeros((n_pad, d), F32)], axis=0)
    small = (g_small,) + tuple(_small_adamw(w_small, g_small, m_small, v_small))

    def unpack(a):
        sc = a[8 + n_f_rows, :n_groups * pr].reshape(1, n_groups, pr)
        return {"ln_in_g": a[1], "ln_in_b": a[2], "ln1_g": a[3:4], "ln1_b": a[4:5], "b_ff2": a[5:6], "ln2_g": a[6:7],
                "ln2_b": a[7:8], "b_ff1": a[8:8 + n_f_rows].reshape(1, f), "pool_scale": sc}

    shapes = {"w_in": w_in.shape, "w_out": w_out.shape, "w_ff1": w_ff1.shape, "w_ff2": w_ff2.shape, "w_pool": w_pool.shape}
    order = ["ln_in_g", "ln_in_b", "w_in", "w_pool", "pool_scale", "w_out", "ln1_g", "ln1_b", "w_ff1", "b_ff1", "w_ff2",
             "b_ff2", "ln2_g", "ln2_b"]
    outs = []
    for kind in range(4):
        small_k = unpack(small[kind])
        for nm in order:
            outs.append(big[nm][kind].reshape(shapes[nm]) if nm in big else small_k[nm])
    return (loss11.reshape(()), dx.reshape(x.shape), *outs)
```

```python
import functools
import math

import jax
import jax.numpy as jnp
from jax import lax
from jax.experimental import pallas as pl
from jax.experimental.pallas import tpu as pltpu

F32 = jnp.float32
BF16 = jnp.bfloat16
MESH = pl.DeviceIdType.MESH

N_DEV = 8
HEAD_DIM = 128
POOL_WINDOWS = (2, 4, 8, 16)
DEEPNORM_ALPHA = (2.0 * 1) ** 0.25
LN_EPS = 1e-5
ADAM_LR = 0.001
ADAM_B1 = 0.9
ADAM_B2 = 0.999
ADAM_EPS = 1e-08
ADAM_WD = 0.01
ADAM_STEP = 10

V7X_VMEM_LIMIT = 56 * 1024 * 1024
ATT_BLOCK = 256
POOL_CHUNK = 256

NN = (((1,), (0,)), ((), ()))
NT = (((1,), (1,)), ((), ()))
TN = (((0,), (0,)), ((), ()))


def _dot(a, b, dims=NN):
    return lax.dot_general(a, b, dims, preferred_element_type=F32)


def _cparams(sem=None):
    return pltpu.CompilerParams(dimension_semantics=sem, vmem_limit_bytes=V7X_VMEM_LIMIT)


def _ln_stats(r):
    mu = jnp.mean(r, axis=-1, keepdims=True)
    xc = r - mu
    var = jnp.mean(xc * xc, axis=-1, keepdims=True)
    rstd = lax.rsqrt(var + LN_EPS)
    return xc * rstd, rstd


def _ln_bwd(dy, xhat, rstd, g):
    dxh = dy * g
    m1 = jnp.mean(dxh, axis=-1, keepdims=True)
    m2 = jnp.mean(dxh * xhat, axis=-1, keepdims=True)
    dx = rstd * (dxh - m1 - xhat * m2)
    dg = jnp.sum(dy * xhat, axis=0, keepdims=True)
    db = jnp.sum(dy, axis=0, keepdims=True)
    return dx, dg, db


def _acc_rows(first, ref, val):
    @pl.when(first)
    def _():
        ref[...] = val

    @pl.when(jnp.logical_not(first))
    def _():
        ref[...] += val


def _call(body, *, name, grid, in_specs, out_specs, out_shape, inputs, scratch_shapes=(), sem=None, comm=None):
    in_specs, out_specs, out_shape, inputs = list(in_specs), list(out_specs), list(out_shape), list(inputs)
    if comm is None:
        outs = pl.pallas_call(
            body, name=name, grid=grid, in_specs=in_specs, out_specs=out_specs, out_shape=out_shape,
            scratch_shapes=list(scratch_shapes), compiler_params=_cparams(sem))(*inputs)
        return list(outs), []
    arrays, plan, n_copies = comm
    n_in, n_out, nc, n_scr = len(inputs), len(out_shape), len(arrays), len(scratch_shapes)

    def hosted(*refs):
        ins = refs[:n_in]
        outs = refs[n_in + nc:n_in + nc + n_out]
        passed = refs[n_in + nc + n_out:n_in + 2 * nc + n_out]
        scratch = refs[n_in + 2 * nc + n_out:n_in + 2 * nc + n_out + n_scr]
        send_sems, recv_sems = refs[-2], refs[-1]
        ids = [pl.program_id(ax) for ax in range(len(grid))]
        first = functools.reduce(jnp.logical_and, [i_ == 0 for i_ in ids])
        last = functools.reduce(jnp.logical_and, [i_ == g - 1 for i_, g in zip(ids, grid)])

        @pl.when(first)
        def _():
            for cp in _plan_copies(plan, passed, send_sems, recv_sems):
                cp.start()

        body(*ins, *outs, *scratch)

        @pl.when(last)
        def _():
            for cp in _plan_copies(plan, passed, send_sems, recv_sems):
                cp.wait_send()
                cp.wait_recv()

    any_spec = pl.BlockSpec(memory_space=pl.ANY)
    outs = pl.pallas_call(
        hosted, name=name, grid=grid,
        in_specs=in_specs + [any_spec] * nc, out_specs=out_specs + [any_spec] * nc,
        out_shape=out_shape + [jax.ShapeDtypeStruct(a.shape, a.dtype) for a in arrays],
        scratch_shapes=list(scratch_shapes) + [pltpu.SemaphoreType.DMA((n_copies,)), pltpu.SemaphoreType.DMA((n_copies,))],
        input_output_aliases={n_in + i: n_out + i for i in range(nc)},
        compiler_params=pltpu.CompilerParams(dimension_semantics=("arbitrary",) * len(grid),
                                             vmem_limit_bytes=V7X_VMEM_LIMIT, has_side_effects=True),
    )(*inputs, *arrays)
    return list(outs[:n_out]), list(outs[n_out:])


def _matmul(name, a, b, *, dims, grid, a_spec, b_spec, extras=(), extra_specs=(), out_shape, out_specs,
            acc_shape, epilogue, k_axis=2, sem=("parallel", "parallel", "arbitrary"), comm=None, dot_fn=None):
    nk = grid[k_axis]
    n_extra = len(extras)
    n_out = len(out_shape)
    if dot_fn is None:
        def dot_fn(a_ref, b_ref):
            return _dot(a_ref[...], b_ref[...], dims)

    def body(a_ref, b_ref, *rest):
        extra_refs = rest[:n_extra]
        out_refs = rest[n_extra:n_extra + n_out]
        if nk == 1:
            epilogue(dot_fn(a_ref, b_ref), extra_refs, out_refs)
            return
        acc_ref = rest[n_extra + n_out]
        k = pl.program_id(k_axis)

        @pl.when(k == 0)
        def _():
            acc_ref[...] = jnp.zeros(acc_shape, F32)

        acc_ref[...] += dot_fn(a_ref, b_ref)

        @pl.when(k == nk - 1)
        def _():
            epilogue(acc_ref[...], extra_refs, out_refs)

    outs, passed = _call(
        body, name=name, grid=grid, in_specs=[a_spec, b_spec, *extra_specs], out_specs=out_specs, out_shape=out_shape,
        inputs=[a, b, *extras], scratch_shapes=[] if nk == 1 else [pltpu.VMEM(acc_shape, F32)], sem=sem, comm=comm)
    return outs if comm is None else (outs, passed)


def _store_epilogue(dtype):
    def ep(acc, extra_refs, out_refs):
        out_refs[0][...] = acc.astype(dtype)
    return ep


def _store_f32_and_bf16(acc, extra_refs, out_refs):
    out_refs[0][...] = acc
    out_refs[1][...] = acc.astype(BF16)


def _ln_in_fwd(x, g, b, tm, comm=None):
    t, d = x.shape

    def body(x_ref, g_ref, b_ref, h_ref, hb_ref):
        xhat, _ = _ln_stats(x_ref[...])
        h = xhat * g_ref[...] + b_ref[...]
        h_ref[...] = h
        hb_ref[...] = h.astype(BF16)

    row = pl.BlockSpec((tm, d), lambda i: (i, 0))
    vec = pl.BlockSpec((1, d), lambda i: (0, 0))
    return _call(
        body, name="ln_in_fwd", grid=(t // tm,), in_specs=[row, vec, vec], out_specs=[row, row],
        out_shape=[jax.ShapeDtypeStruct((t, d), F32), jax.ShapeDtypeStruct((t, d), BF16)],
        inputs=[x, g, b], sem=("parallel",), comm=comm)


def _split3(x):
    hi = x.astype(BF16)
    r = x - hi.astype(F32)
    mid = r.astype(BF16)
    lo = (r - mid.astype(F32)).astype(BF16)
    return hi, mid, lo


def _split2(x):
    hi = x.astype(BF16)
    lo = (x - hi.astype(F32)).astype(BF16)
    return hi, lo


def _pool_fwd(u, wp, sc, t, c, comm=None):
    n_groups = len(POOL_WINDOWS)
    tc = POOL_CHUNK
    n_chunks = t // tc

    def body(u_ref, wp_ref, sc_ref, y_ref, ypre_ref, xp_ref):
        g = pl.program_id(0)
        xp_ref[pl.ds(0, tc), :] = jnp.zeros((tc, c), F32)
        xp_ref[pl.ds(tc, t), :] = u_ref[...]
        out_i = lax.broadcasted_iota(jnp.int32, (tc, 2 * tc), 0)
        in_j = lax.broadcasted_iota(jnp.int32, (tc, 2 * tc), 1)
        lag = tc + out_i - in_j
        t_in_chunk = lax.broadcasted_iota(jnp.int32, (tc, 1), 0)
        for gi, w in enumerate(POOL_WINDOWS):
            @pl.when(g == gi)
            def _(w=w):
                band = jnp.logical_and(lag >= 0, lag < w).astype(BF16)

                def chunk(ci, carry):
                    start = pl.multiple_of(ci * tc, tc)
                    win = xp_ref[pl.ds(start, 2 * tc), :]
                    hi, mid, lo = _split3(win)
                    wsum = _dot(band, hi) + _dot(band, mid) + _dot(band, lo)
                    cnt = jnp.minimum(ci * tc + t_in_chunk + 1, w).astype(F32)
                    ypre = wsum * (1.0 / cnt) - win[tc:, :]
                    ypre_b = ypre.astype(BF16)
                    y = _dot(ypre_b, wp_ref[...]) * sc_ref[...]
                    ypre_ref[pl.ds(start, tc), :] = ypre_b
                    y_ref[pl.ds(start, tc), :] = y.astype(BF16)
                    return carry

                lax.fori_loop(0, n_chunks, chunk, 0)

    col = pl.BlockSpec((t, c), lambda g: (0, g))
    return _call(
        body, name="pool_fwd", grid=(n_groups,),
        in_specs=[col, pl.BlockSpec((None, c, c), lambda g: (g, 0, 0)), pl.BlockSpec((None, 1, c), lambda g: (g, 0, 0))],
        out_specs=[col, col],
        out_shape=[jax.ShapeDtypeStruct((t, n_groups * c), BF16), jax.ShapeDtypeStruct((t, n_groups * c), BF16)],
        inputs=[u, wp, sc], scratch_shapes=[pltpu.VMEM((t + tc, c), F32)], sem=("parallel",), comm=comm)


def _pool_bwd(dmixin, ypre, wp, sc, t, c, comm=None):
    n_groups = len(POOL_WINDOWS)
    tc = POOL_CHUNK
    n_chunks = t // tc

    def body(dy_ref, ypre_ref, wp_ref, sc_ref, du_ref, dwp_ref, dsc_ref, zp_ref, neg_ref):
        g = pl.program_id(0)
        zp_ref[pl.ds(t, tc), :] = jnp.zeros((tc, c), F32)
        dwp_ref[...] = jnp.zeros((c, c), F32)
        dsc_ref[...] = jnp.zeros((1, c), F32)
        out_i = lax.broadcasted_iota(jnp.int32, (tc, 2 * tc), 0)
        in_j = lax.broadcasted_iota(jnp.int32, (tc, 2 * tc), 1)
        lead = in_j - out_i
        t_in_chunk = lax.broadcasted_iota(jnp.int32, (tc, 1), 0)
        for gi, w in enumerate(POOL_WINDOWS):
            @pl.when(g == gi)
            def _(w=w):
                band = jnp.logical_and(lead >= 0, lead < w).astype(BF16)

                def first(ci, carry):
                    start = pl.multiple_of(ci * tc, tc)
                    dy = dy_ref[pl.ds(start, tc), :]
                    yp = ypre_ref[pl.ds(start, tc), :]
                    ymm = _dot(yp, wp_ref[...])
                    dsc_ref[...] += jnp.sum(dy * ymm, axis=0, keepdims=True)
                    dys_b = (dy * sc_ref[...]).astype(BF16)
                    dwp_ref[...] += _dot(yp, dys_b, TN)
                    dyp = _dot(dys_b, wp_ref[...], NT)
                    cnt = jnp.minimum(ci * tc + t_in_chunk + 1, w).astype(F32)
                    zp_ref[pl.ds(start, tc), :] = dyp * (1.0 / cnt)
                    neg_ref[pl.ds(start, tc), :] = -dyp
                    return carry

                lax.fori_loop(0, n_chunks, first, 0)

                def second(ci, carry):
                    start = pl.multiple_of(ci * tc, tc)
                    hi, mid, lo = _split3(zp_ref[pl.ds(start, 2 * tc), :])
                    ahead = _dot(band, hi) + _dot(band, mid) + _dot(band, lo)
                    du_ref[pl.ds(start, tc), :] = (neg_ref[pl.ds(start, tc), :] + ahead).astype(BF16)
                    return carry

                lax.fori_loop(0, n_chunks, second, 0)

    col = pl.BlockSpec((t, c), lambda g: (0, g))
    return _call(
        body, name="pool_bwd", grid=(n_groups,),
        in_specs=[col, col, pl.BlockSpec((None, c, c), lambda g: (g, 0, 0)), pl.BlockSpec((None, 1, c), lambda g: (g, 0, 0))],
        out_specs=[col, pl.BlockSpec((None, c, c), lambda g: (g, 0, 0)), pl.BlockSpec((None, 1, c), lambda g: (g, 0, 0))],
        out_shape=[jax.ShapeDtypeStruct((t, n_groups * c), BF16), jax.ShapeDtypeStruct((n_groups, c, c), F32),
                   jax.ShapeDtypeStruct((n_groups, 1, c), F32)],
        inputs=[dmixin, ypre, wp, sc], scratch_shapes=[pltpu.VMEM((t + tc, c), F32), pltpu.VMEM((t, c), F32)],
        sem=("parallel",), comm=comm)


ROW_PARTS = 2


def _att_consts():
    b = ATT_BLOCK
    rp = b // ROW_PARTS
    row = lax.broadcasted_iota(jnp.int32, (b, b), 0)
    col = lax.broadcasted_iota(jnp.int32, (b, b), 1)
    tri = (row >= col).astype(BF16)
    prow = lax.broadcasted_iota(jnp.int32, (rp, b), 0)
    pcol = lax.broadcasted_iota(jnp.int32, (rp, b), 1)
    causal = [pcol < prow + r * rp for r in range(ROW_PARTS)]
    return tri, causal


def _suffix_sum(x, tri):
    hi, lo = _split2(x)
    return _dot(hi, tri) + _dot(lo, tri)


LOG2_E = 1.4426950408889634


def _att_scores(qb, kb, mask):
    z2 = _dot(qb, kb, NT) * (LOG2_E / math.sqrt(HEAD_DIM))
    sp2 = jnp.maximum(z2, 0.0) + jnp.log2(1.0 + jnp.exp2(-jnp.abs(z2)))
    return z2, sp2, (sp2 if mask is None else jnp.where(mask, sp2, 0.0))


HEADS_PER_STEP = 2
ATT_LANES = HEADS_PER_STEP * HEAD_DIM


def _head_lanes(s):
    return slice(s * HEAD_DIM, (s + 1) * HEAD_DIM)


UNDERFLOW_LOG2 = 160.0


def _sweep_earlier_blocks(i, state, per_chain, block):
    def lowest(st):
        low = st[0]
        for k in range(per_chain, len(st), per_chain):
            low = jnp.minimum(low, st[k])
        return jnp.min(low)

    def more(c):
        return jnp.logical_and(c[0] < i, c[1] < UNDERFLOW_LOG2)

    def trip(c):
        st = block(i - 1 - c[0], c[2:])
        return (c[0] + 1, lowest(st)) + tuple(st)

    return lax.while_loop(more, trip, (jnp.int32(0), lowest(state)) + tuple(state))[2:]


def _attn_fwd(qkv, t, n_heads, comm=None):
    b = ATT_BLOCK
    nq = t // b
    n_steps = n_heads // HEADS_PER_STEP

    rp = b // ROW_PARTS
    chains = [(s, r) for s in range(HEADS_PER_STEP) for r in range(ROW_PARTS)]
    no_mask = [None] * ROW_PARTS

    def body(q_ref, k_ref, v_ref, o_ref, ob_ref):
        tri, causal = _att_consts()

        def blocks(qbs, j, state, masks):
            ks = pl.multiple_of(j * b, b)
            scores = [_att_scores(qbs[ci], k_ref[pl.ds(ks, b), _head_lanes(s)], masks[r]) for ci, (s, r) in enumerate(chains)]
            incls = [_suffix_sum(sc[2], tri) for sc in scores]
            out = []
            for ci, (s, r) in enumerate(chains):
                carry, acc = state[2 * ci], state[2 * ci + 1]
                a = jnp.exp2(scores[ci][0] - (incls[ci] + carry))
                if masks[r] is not None:
                    a = jnp.where(masks[r], a, 0.0)
                out += [carry + incls[ci][:, 0:1], acc + _dot(a.astype(BF16), v_ref[pl.ds(ks, b), _head_lanes(s)])]
            return tuple(out)

        def q_loop(i, _):
            qs = pl.multiple_of(i * b, b)
            qbs = [q_ref[pl.ds(qs + r * rp, rp), _head_lanes(s)] for s, r in chains]
            zero = (jnp.zeros((rp, 1), F32), jnp.zeros((rp, HEAD_DIM), F32)) * len(chains)
            state = blocks(qbs, i, zero, causal)
            state = _sweep_earlier_blocks(i, state, 2, lambda j, st: blocks(qbs, j, st, no_mask))
            for ci, (s, r) in enumerate(chains):
                o_ref[pl.ds(qs + r * rp, rp), _head_lanes(s)] = state[2 * ci + 1]
                ob_ref[pl.ds(qs + r * rp, rp), _head_lanes(s)] = state[2 * ci + 1].astype(BF16)
            return 0

        lax.fori_loop(0, nq, q_loop, 0)

    def heads(off):
        return pl.BlockSpec((t, ATT_LANES), lambda h: (0, off + h))

    return _call(
        body, name="attn_fwd", grid=(n_steps,),
        in_specs=[heads(0), heads(n_steps), heads(2 * n_steps)], out_specs=[heads(0), heads(0)],
        out_shape=[jax.ShapeDtypeStruct((t, n_heads * HEAD_DIM), F32), jax.ShapeDtypeStruct((t, n_heads * HEAD_DIM), BF16)],
        inputs=[qkv, qkv, qkv], sem=("parallel",), comm=comm)


def _attn_bwd(qkv, do, o, t, n_heads, comm=None):
    b = ATT_BLOCK
    nq = t // b
    n_steps = n_heads // HEADS_PER_STEP
    scale = 1.0 / math.sqrt(HEAD_DIM)
    rp = b // ROW_PARTS
    chains = [(s, r) for s in range(HEADS_PER_STEP) for r in range(ROW_PARTS)]
    no_mask = [None] * ROW_PARTS

    def body(q_ref, k_ref, v_ref, do_ref, o_ref, dq_ref, dk_ref, dv_ref, qt_ref, dot_ref, dkt_ref, dvt_ref):
        for j in range(nq):
            rows = pl.ds(j * b, b)
            qt_ref[j] = q_ref[rows, :].astype(F32).T.astype(BF16)
            dot_ref[j] = do_ref[rows, :].astype(F32).T.astype(BF16)
        dkt_ref[...] = jnp.zeros((nq, ATT_LANES, b), F32)
        dvt_ref[...] = jnp.zeros((nq, ATT_LANES, b), F32)
        tri, causal = _att_consts()

        def blocks(i, fixed, j, state, masks):
            ks = pl.multiple_of(j * b, b)
            n = len(chains)
            kbs = [k_ref[pl.ds(ks, b), _head_lanes(s)] for s, _ in chains]
            scores = [_att_scores(fixed[ci][0], kbs[ci], masks[r]) for ci, (s, r) in enumerate(chains)]
            incls = [_suffix_sum(sc[2], tri) for sc in scores]
            das = [_dot(fixed[ci][1], v_ref[pl.ds(ks, b), _head_lanes(s)], NT) for ci, (s, r) in enumerate(chains)]
            a_bs, gs = [], []
            for ci, (s, r) in enumerate(chains):
                a = jnp.exp2(scores[ci][0] - (incls[ci] + state[3 * ci]))
                if masks[r] is not None:
                    a = jnp.where(masks[r], a, 0.0)
                a_bs.append(a.astype(BF16))
                gs.append(a_bs[ci].astype(F32) * das[ci])
            g_incls = [_suffix_sum(g, tri) for g in gs]
            dz_bs = []
            for ci, (s, r) in enumerate(chains):
                rest = (fixed[ci][2] - state[3 * ci + 1]) - (g_incls[ci] - gs[ci])
                sig = jnp.exp2(scores[ci][0] - scores[ci][1])
                dz = (gs[ci] - sig * rest) * scale
                if masks[r] is not None:
                    dz = jnp.where(masks[r], dz, 0.0)
                dz_bs.append(dz.astype(BF16))
            out = []
            for ci in range(n):
                out += [state[3 * ci] + incls[ci][:, 0:1], state[3 * ci + 1] + g_incls[ci][:, 0:1],
                        state[3 * ci + 2] + _dot(dz_bs[ci], kbs[ci])]
            for s in range(HEADS_PER_STEP):
                lanes = _head_lanes(s)
                dk_add, dv_add = None, None
                for ci, (cs, r) in enumerate(chains):
                    if cs == s:
                        part = slice(r * rp, (r + 1) * rp)
                        dk_c = _dot(qt_ref[i, lanes, part], dz_bs[ci])
                        dv_c = _dot(dot_ref[i, lanes, part], a_bs[ci])
                        dk_add = dk_c if dk_add is None else dk_add + dk_c
                        dv_add = dv_c if dv_add is None else dv_add + dv_c
                dkt_ref[j, lanes, :] += dk_add
                dvt_ref[j, lanes, :] += dv_add
            return tuple(out)

        def q_loop(i, _):
            qs = pl.multiple_of(i * b, b)
            fixed = []
            for s, r in chains:
                rows = pl.ds(qs + r * rp, rp)
                dob = do_ref[rows, _head_lanes(s)]
                total = jnp.sum(dob.astype(F32) * o_ref[rows, _head_lanes(s)], axis=-1, keepdims=True)
                fixed.append((q_ref[rows, _head_lanes(s)], dob, total))
            zero = (jnp.zeros((rp, 1), F32), jnp.zeros((rp, 1), F32), jnp.zeros((rp, HEAD_DIM), F32)) * len(chains)
            state = blocks(i, fixed, i, zero, causal)
            state = _sweep_earlier_blocks(i, state, 3, lambda j, st: blocks(i, fixed, j, st, no_mask))
            for ci, (s, r) in enumerate(chains):
                dq_ref[pl.ds(qs + r * rp, rp), _head_lanes(s)] = state[3 * ci + 2].astype(BF16)
            return 0

        lax.fori_loop(0, nq, q_loop, 0)
        for j in range(nq):
            rows = pl.ds(j * b, b)
            dk_ref[rows, :] = dkt_ref[j].T.astype(BF16)
            dv_ref[rows, :] = dvt_ref[j].T.astype(BF16)

    def heads(off):
        return pl.BlockSpec((t, ATT_LANES), lambda h: (0, off + h))

    shape = jax.ShapeDtypeStruct((t, n_heads * HEAD_DIM), BF16)
    return _call(
        body, name="attn_bwd", grid=(n_steps,),
        in_specs=[heads(0), heads(n_steps), heads(2 * n_steps), heads(0), heads(0)],
        out_specs=[heads(0)] * 3, out_shape=[shape] * 3, inputs=[qkv, qkv, qkv, do, o],
        scratch_shapes=[pltpu.VMEM((nq, ATT_LANES, b), BF16)] * 2 + [pltpu.VMEM((nq, ATT_LANES, b), F32)] * 2,
        sem=("parallel",), comm=comm)


def _place():
    x, y, c = lax.axis_index("x"), lax.axis_index("y"), lax.axis_index("c")
    return x, y, c


def _flip(v, on):
    return 1 - v if on else v


def _plan_copies(plan, refs, send_sems, recv_sems):
    return [pltpu.make_async_remote_copy(src_ref=src, dst_ref=dst, send_sem=send_sems.at[k], recv_sem=recv_sems.at[k],
                                         device_id=dev, device_id_type=MESH)
            for k, (src, dst, dev) in enumerate(plan(refs))]


def _copies_now(name, arrays, plan, n_copies):
    n = len(arrays)

    def body(*refs):
        copies = _plan_copies(plan, refs[n:2 * n], refs[2 * n], refs[2 * n + 1])
        for cp in copies:
            cp.start()
        for cp in copies:
            cp.wait_send()
            cp.wait_recv()

    any_spec = pl.BlockSpec(memory_space=pl.ANY)
    return list(pl.pallas_call(
        body, name=name, in_specs=[any_spec] * n, out_specs=[any_spec] * n,
        out_shape=[jax.ShapeDtypeStruct(a.shape, a.dtype) for a in arrays],
        input_output_aliases={i: i for i in range(n)},
        scratch_shapes=[pltpu.SemaphoreType.DMA((n_copies,)), pltpu.SemaphoreType.DMA((n_copies,))],
        compiler_params=pltpu.CompilerParams(has_side_effects=True),
    )(*arrays))


SIBLING, ACROSS_Y, ACROSS_X, DIAGONAL = 1, 2, 4, 6


def _plan_gather_own(peers, rows=None):
    def plan(refs):
        x, y, c = _place()
        mine = refs[0].at[4 * x + 2 * y + c]
        if rows is not None:
            mine = mine.at[pl.ds(*rows)]
        return [(mine, mine, (_flip(x, k & 4), _flip(y, k & 2), _flip(c, k & 1))) for k in peers]
    return plan, len(peers)


def _plan_gather_forward(n):
    def plan(refs):
        x, y, c = _place()
        out = []
        for ti in range(n):
            for r in range(1, 4):
                blk = refs[ti].at[4 * _flip(x, r & 2) + 2 * _flip(y, r & 1) + c]
                out.append((blk, blk, (x, y, 1 - c)))
        return out
    return plan, 3 * n


def _join_plans(*parts):
    def plan(refs):
        out, at = [], 0
        for part, n_arrays, _ in parts:
            out += part(refs[at:at + n_arrays])
            at += n_arrays
        return out
    return plan, sum(n_cp for _, _, n_cp in parts)


def _plan_rs_sibling(n):
    def plan(refs):
        x, y, c = _place()
        out = []
        for ti in range(n):
            for r in range(4):
                src = refs[ti].at[4 * _flip(x, r & 2) + 2 * _flip(y, r & 1) + (1 - c)]
                out.append((src, refs[n + ti].at[r], (x, y, 1 - c)))
        return out
    return plan, 4 * n


def _plan_rs_owner(n, relations=(1, 2, 3)):
    def plan(refs):
        x, y, c = _place()
        out = []
        for ti in range(n):
            for r in relations:
                out.append((refs[ti].at[r], refs[n + ti].at[r], (_flip(x, r & 2), _flip(y, r & 1), c)))
        return out
    return plan, len(relations) * n


def _owner_slots():
    x, y, c = _place()
    idx = []
    for r in range(4):
        ox, oy = (1 - x if r & 2 else x), (1 - y if r & 1 else y)
        idx.append(4 * ox + 2 * oy + c)
    return jnp.stack(idx).astype(jnp.int32)


def _row_tile(rows, cols):
    tr = max(8, min(rows, (1 << 19) // cols))
    while rows % tr:
        tr //= 2
    return tr


def _rs_chip_sum(name, slots, partial, from_sibling):
    _, rows, cols = partial.shape
    tr = _row_tile(rows, cols)

    def body(slots_ref, p_ref, s_ref, o_ref):
        o_ref[...] = (p_ref[...] + s_ref[...].astype(F32)).astype(BF16)

    grid_spec = pltpu.PrefetchScalarGridSpec(
        num_scalar_prefetch=1, grid=(3, rows // tr),
        in_specs=[pl.BlockSpec((None, tr, cols), lambda r, i, s: (s[r + 1], i, 0)),
                  pl.BlockSpec((None, tr, cols), lambda r, i, s: (r + 1, i, 0))],
        out_specs=pl.BlockSpec((None, tr, cols), lambda r, i, s: (r + 1, i, 0)))
    return pl.pallas_call(
        body, name=name, grid_spec=grid_spec, out_shape=jax.ShapeDtypeStruct((4, rows, cols), BF16),
        compiler_params=_cparams(("parallel", "parallel")),
    )(slots, partial, from_sibling)


def _adamw(w, g, m, v):
    m = ADAM_B1 * m + (1.0 - ADAM_B1) * g
    v = ADAM_B2 * v + (1.0 - ADAM_B2) * (g * g)
    m_hat = m / (1.0 - ADAM_B1 ** ADAM_STEP)
    v_hat = v / (1.0 - ADAM_B2 ** ADAM_STEP)
    delta = -ADAM_LR * (m_hat / (jnp.sqrt(v_hat) + ADAM_EPS) + ADAM_WD * w)
    return delta, m, v


def _rs_final_adamw(name, slots, partial, from_sibling, from_chips, w, m, v):
    rows, cols = w.shape
    tr = _row_tile(rows, cols)

    def body(slots_ref, p_ref, s_ref, c1_ref, c2_ref, c3_ref, w_ref, m_ref, v_ref, g_ref, d_ref, nm_ref, nv_ref):
        g = p_ref[...] + s_ref[...].astype(F32)
        g = g + c1_ref[...].astype(F32)
        g = g + c2_ref[...].astype(F32)
        g = g + c3_ref[...].astype(F32)
        delta, nm, nv = _adamw(w_ref[...], g, m_ref[...], v_ref[...])
        g_ref[...] = g
        d_ref[...] = delta
        nm_ref[...] = nm
        nv_ref[...] = nv

    def slot(r):
        return pl.BlockSpec((None, tr, cols), lambda i, s: (r, i, 0))

    flat = pl.BlockSpec((tr, cols), lambda i, s: (i, 0))
    grid_spec = pltpu.PrefetchScalarGridSpec(
        num_scalar_prefetch=1, grid=(rows // tr,),
        in_specs=[pl.BlockSpec((None, tr, cols), lambda i, s: (s[0], i, 0)), slot(0), slot(1), slot(2), slot(3), flat, flat, flat],
        out_specs=[flat] * 4)
    return pl.pallas_call(
        body, name=name, grid_spec=grid_spec, out_shape=[jax.ShapeDtypeStruct((rows, cols), F32)] * 4,
        compiler_params=_cparams(("parallel",)),
    )(slots, partial, from_sibling, from_chips, from_chips, from_chips, w, m, v)


def _small_all_reduce(packet):
    rows, d = packet.shape

    def body(p_ref, sum_ref, loss_ref, all_ref, send_sems, recv_sems):
        x, y, c = _place()
        me = 4 * x + 2 * y + c
        all_ref[me] = p_ref[...]
        copies = []
        for k in range(1, N_DEV):
            px, py, pc = (1 - x if k & 4 else x), (1 - y if k & 2 else y), (1 - c if k & 1 else c)
            cp = pltpu.make_async_remote_copy(
                src_ref=p_ref, dst_ref=all_ref.at[me], send_sem=send_sems.at[k], recv_sem=recv_sems.at[k],
                device_id=(px, py, pc), device_id_type=MESH)
            cp.start()
            copies.append(cp)
        for cp in copies:
            cp.wait_recv()
        for cp in copies:
            cp.wait_send()
        total = all_ref[0]
        for j in range(1, N_DEV):
            total = total + all_ref[j]
        sum_ref[...] = total
        loss_ref[...] = jnp.sum(total[0:1, :], axis=-1, keepdims=True)

    vmem = pl.BlockSpec(memory_space=pltpu.VMEM)
    return pl.pallas_call(
        body, name="small_all_reduce",
        in_specs=[vmem], out_specs=[vmem, vmem],
        out_shape=[jax.ShapeDtypeStruct((rows, d), F32), jax.ShapeDtypeStruct((1, 1), F32)],
        scratch_shapes=[pltpu.VMEM((N_DEV, rows, d), F32), pltpu.SemaphoreType.DMA((N_DEV,)), pltpu.SemaphoreType.DMA((N_DEV,))],
        compiler_params=pltpu.CompilerParams(has_side_effects=True),
    )(packet)


def _small_adamw(w, g, m, v):
    def body(w_ref, g_ref, m_ref, v_ref, d_ref, nm_ref, nv_ref):
        delta, nm, nv = _adamw(w_ref[...], g_ref[...], m_ref[...], v_ref[...])
        d_ref[...] = delta
        nm_ref[...] = nm
        nv_ref[...] = nv

    vmem = pl.BlockSpec(memory_space=pltpu.VMEM)
    return pl.pallas_call(
        body, name="small_adamw", in_specs=[vmem] * 4, out_specs=[vmem] * 3,
        out_shape=[jax.ShapeDtypeStruct(w.shape, F32)] * 3,
    )(w, g, m, v)


def kernel(x, ln_in_g, ln_in_b, w_in, w_pool, pool_scale, w_out, ln1_g, ln1_b, w_ff1, b_ff1, w_ff2, b_ff2, ln2_g, ln2_b, loss_target, m_ln_in_g, m_ln_in_b, m_w_in, m_w_pool, m_pool_scale, m_w_out, m_ln1_g, m_ln1_b, m_w_ff1, m_b_ff1, m_w_ff2, m_b_ff2, m_ln2_g, m_ln2_b, v_ln_in_g, v_ln_in_b, v_w_in, v_w_pool, v_pool_scale, v_w_out, v_ln1_g, v_ln1_b, v_w_ff1, v_b_ff1, v_w_ff2, v_b_ff2, v_ln2_g, v_ln2_b):
    t, d = x.shape[1], x.shape[2]
    n_groups = len(POOL_WINDOWS)
    c_pool = w_pool.shape[3]
    p = n_groups * c_pool
    n_heads = (d - p) // HEAD_DIM
    ws_in = w_in.shape[2]
    n_in = N_DEV * ws_in
    ws_out = w_out.shape[1]
    ws_f = w_ff1.shape[2]
    f = N_DEV * ws_f
    pr = w_pool.shape[2]
    assert n_in == p + 3 * n_heads * HEAD_DIM and N_DEV * ws_out == d and N_DEV * pr == c_pool

    tm_big = min(t, 1024)
    tm_ep = min(t, 512)
    tkk = min(t, 2048)
    half_f = min(ws_f, 512)
    per_f = ws_f // half_f

    x2 = x.reshape(t, d)
    target = loss_target.reshape(t, d)
    g0, b0 = ln_in_g.reshape(1, d), ln_in_b.reshape(1, d)

    shards = [w_in.reshape(d, ws_in), w_out.reshape(ws_out, d), w_ff1.reshape(d, ws_f), w_ff2.reshape(ws_f, d),
              w_pool.reshape(n_groups * pr, c_pool)]
    x_, y_, c_ = _place()
    me = 4 * x_ + 2 * y_ + c_
    def landing(block):
        return lax.dynamic_update_index_in_dim(lax.empty((N_DEV, *block.shape), block.dtype), block, me, 0)

    land_in, land_out, land_1, land_2, land_pool = [landing(s.astype(BF16)) for s in shards]
    land_scale = landing(pool_scale.reshape(n_groups, pr))

    def sds(shape, dtype=F32):
        return jax.ShapeDtypeStruct(shape, dtype)

    vec = pl.BlockSpec((1, d), lambda m, n, k: (0, 0))
    row_ep = pl.BlockSpec((tm_ep, d), lambda m, n, k: (m, 0))
    tm_res = min(t, 256)
    row_res = pl.BlockSpec((tm_res, d), lambda m, n, k: (m, 0))
    seq = ("arbitrary", "arbitrary", "arbitrary")

    two_level = _plan_gather_own([SIBLING, ACROSS_Y, ACROSS_X, DIAGONAL])
    forward = _plan_gather_forward(1)
    first_needed = [land_in, land_pool, land_scale]
    (h0, h0b), first_needed = _ln_in_fwd(
        x2, g0, b0, tm_big, comm=(first_needed, *_join_plans(*[(two_level[0], 1, two_level[1])] * 3)))
    win_g, wpool_g, scale_g = _copies_now("gather_forward_w_in", first_needed, *_plan_gather_forward(3))
    wp_full = wpool_g.reshape(N_DEV, n_groups, pr, c_pool).transpose(1, 0, 2, 3).reshape(n_groups, c_pool, c_pool)
    sc_full = scale_g.transpose(1, 0, 2).reshape(n_groups, 1, c_pool)

    pool_shards = p // ws_in

    def mm_u(name, first, count, dtype, comm=None):
        return _matmul(
            name, h0b, win_g, dims=NN, grid=(t // tm_big, count, 1),
            a_spec=pl.BlockSpec((tm_big, d), lambda m, n, k: (m, 0)),
            b_spec=pl.BlockSpec((None, d, ws_in), lambda m, n, k: (n + first, 0, 0)),
            out_shape=[sds((t, count * ws_in), dtype)],
            out_specs=[pl.BlockSpec((tm_big, ws_in), lambda m, n, k: (m, n))],
            acc_shape=(tm_big, ws_in), epilogue=_store_epilogue(dtype), comm=comm)

    half = land_1.shape[1] // 2
    diag_a, diag_b = _plan_gather_own([DIAGONAL], (0, half)), _plan_gather_own([DIAGONAL], (half, half))
    (u_pool,), (w1_diag,) = mm_u("mm_u_pool", 0, pool_shards, F32, comm=([land_1], *diag_a))
    (qkv,), (wout_part,) = mm_u("mm_u_qkv", pool_shards, N_DEV - pool_shards, BF16, comm=([land_out], *two_level))

    (y_pool, ypre), (wout_g, w1_diag) = _pool_fwd(
        u_pool, wp_full, sc_full, t, c_pool,
        comm=([wout_part, w1_diag], *_join_plans((forward[0], 1, forward[1]), (diag_b[0], 1, diag_b[1]))))
    (o, o_b), (w1_part,) = _attn_fwd(qkv, t, n_heads, comm=([w1_diag], *_plan_gather_own([SIBLING, ACROSS_Y, ACROSS_X])))
    mixin = jnp.concatenate([y_pool, o_b], axis=1)
    wout_2d = wout_g.reshape(d, d)

    def ep_ln1(acc, ex, outs):
        h0_ref, g_ref, b_ref = ex
        r1 = DEEPNORM_ALPHA * h0_ref[...] + acc
        xhat, _ = _ln_stats(r1)
        h1 = xhat * g_ref[...] + b_ref[...]
        outs[0][...] = r1
        outs[1][...] = h1
        outs[2][...] = h1.astype(BF16)

    (r1, h1, h1b), (w1_g,) = _matmul(
        "mm_mix_ln1", mixin, wout_2d, dims=NN, grid=(t // tm_res, 1, 1),
        a_spec=pl.BlockSpec((tm_res, d), lambda m, n, k: (m, 0)),
        b_spec=pl.BlockSpec((d, d), lambda m, n, k: (0, 0)),
        extras=(h0, ln1_g, ln1_b), extra_specs=(row_res, vec, vec),
        out_shape=[sds((t, d)), sds((t, d)), sds((t, d), BF16)], out_specs=[row_res] * 3,
        acc_shape=(tm_res, d), epilogue=ep_ln1, comm=([w1_part], *forward))

    def ep_ff1(acc, ex, outs):
        f1 = acc + ex[0][...]
        outs[0][...] = f1
        r = jnp.maximum(f1, 0.0)
        outs[1][...] = (r * r).astype(BF16)

    ff_tile = pl.BlockSpec((tm_big, half_f), lambda m, n, k: (m, n))
    (f1, act), (w2_part,) = _matmul(
        "mm_ff1", h1b, w1_g, dims=NN, grid=(t // tm_big, f // half_f, 1),
        a_spec=pl.BlockSpec((tm_big, d), lambda m, n, k: (m, 0)),
        b_spec=pl.BlockSpec((None, d, half_f), lambda m, n, k: (n // per_f, 0, n % per_f)),
        extras=(b_ff1,), extra_specs=(pl.BlockSpec((1, half_f), lambda m, n, k: (0, n)),),
        out_shape=[sds((t, f)), sds((t, f), BF16)], out_specs=[ff_tile, ff_tile],
        acc_shape=(tm_big, half_f), epilogue=ep_ff1, comm=([land_2], *two_level))
    (w2_g,) = _copies_now("gather_forward_w_ff2", [w2_part], *forward)

    def ep_ln2(acc, ex, outs):
        h1_ref, tgt_ref, bf2_ref, g_ref, b_ref = ex
        dr2_ref, dr2b_ref, dg_ref, db_ref, dbf2_ref, loss_ref = outs
        first = pl.program_id(0) == 0
        r2 = DEEPNORM_ALPHA * h1_ref[...] + (acc + bf2_ref[...])
        xhat, rstd = _ln_stats(r2)
        err = xhat * g_ref[...] + b_ref[...] - tgt_ref[...]
        dr2, dg, db = _ln_bwd(err * (1.0 / d), xhat, rstd, g_ref[...])
        dr2_ref[...] = dr2
        dr2b_ref[...] = dr2.astype(BF16)
        _acc_rows(first, dg_ref, dg)
        _acc_rows(first, db_ref, db)
        _acc_rows(first, dbf2_ref, jnp.sum(dr2, axis=0, keepdims=True))
        _acc_rows(first, loss_ref, jnp.sum(err * err, axis=0, keepdims=True) * (0.5 / d))

    dr2, dr2b, dg2, db2, dbf2, loss_vec = _matmul(
        "mm_ff2_ln2_loss", act, w2_g, dims=NN, grid=(t // tm_ep, 1, N_DEV),
        a_spec=pl.BlockSpec((tm_ep, ws_f), lambda m, n, k: (m, k)),
        b_spec=pl.BlockSpec((None, ws_f, d), lambda m, n, k: (k, 0, 0)),
        extras=(h1, target, b_ff2, ln2_g, ln2_b), extra_specs=(row_ep, row_ep, vec, vec, vec),
        out_shape=[sds((t, d)), sds((t, d), BF16)] + [sds((1, d))] * 4, out_specs=[row_ep, row_ep, vec, vec, vec, vec],
        acc_shape=(tm_ep, d), epilogue=ep_ln2, sem=seq)

    def ep_dff1(acc, ex, outs):
        df1 = acc * (2.0 * jnp.maximum(ex[0][...], 0.0))
        outs[0][...] = df1.astype(BF16)
        _acc_rows(pl.program_id(1) == 0, outs[1], jnp.sum(df1, axis=0, keepdims=True))

    df_tile = pl.BlockSpec((tm_big, ws_f), lambda n, m, k: (m, n))
    df1b, dbf1 = _matmul(
        "mm_dff1", dr2b, w2_g, dims=NT, grid=(N_DEV, t // tm_big, 1),
        a_spec=pl.BlockSpec((tm_big, d), lambda n, m, k: (m, 0)),
        b_spec=pl.BlockSpec((None, ws_f, d), lambda n, m, k: (n, 0, 0)),
        extras=(f1,), extra_specs=(df_tile,),
        out_shape=[sds((t, f), BF16), sds((1, f))], out_specs=[df_tile, pl.BlockSpec((1, ws_f), lambda n, m, k: (0, n))],
        acc_shape=(tm_big, ws_f), epilogue=ep_dff1, sem=("parallel", "arbitrary", "arbitrary"))

    tn_d = min(d, 1024)
    dw2, dw2_b = _matmul(
        "mm_dw2", act, dr2b, dims=TN, grid=(N_DEV, d // tn_d, t // tkk),
        a_spec=pl.BlockSpec((tkk, ws_f), lambda m, n, k: (k, m)),
        b_spec=pl.BlockSpec((tkk, tn_d), lambda m, n, k: (k, n)),
        out_shape=[sds((N_DEV, ws_f, d)), sds((N_DEV, ws_f, d), BF16)],
        out_specs=[pl.BlockSpec((None, ws_f, tn_d), lambda m, n, k: (m, 0, n))] * 2,
        acc_shape=(ws_f, tn_d), epilogue=_store_f32_and_bf16)

    dw1, dw1_b = _matmul(
        "mm_dw1", h1b, df1b, dims=TN, grid=(d // tn_d, N_DEV, t // tkk),
        a_spec=pl.BlockSpec((tkk, tn_d), lambda m, n, k: (k, m)),
        b_spec=pl.BlockSpec((tkk, ws_f), lambda m, n, k: (k, n)),
        out_shape=[sds((N_DEV, d, ws_f)), sds((N_DEV, d, ws_f), BF16)],
        out_specs=[pl.BlockSpec((None, tn_d, ws_f), lambda m, n, k: (n, m, 0))] * 2,
        acc_shape=(tn_d, ws_f), epilogue=_store_f32_and_bf16)

    def ep_ln1_bwd(acc, ex, outs):
        dr2_ref, r1_ref, g_ref = ex
        first = pl.program_id(0) == 0
        xhat, rstd = _ln_stats(r1_ref[...])
        dr1, dg, db = _ln_bwd(DEEPNORM_ALPHA * dr2_ref[...] + acc, xhat, rstd, g_ref[...])
        outs[0][...] = dr1
        outs[1][...] = dr1.astype(BF16)
        _acc_rows(first, outs[2], dg)
        _acc_rows(first, outs[3], db)

    slots = _owner_slots()

    def to_sibling(parts):
        return (parts + [lax.empty((4, *pt.shape[1:]), BF16) for pt in parts], *_plan_rs_sibling(len(parts)))

    def to_owner(names_, parts, from_sib):
        sums = [_rs_chip_sum("rs_chip_sum_" + nm, slots, pt, fs) for nm, pt, fs in zip(names_, parts, from_sib)]
        return (sums + [lax.empty(cs.shape, BF16) for cs in sums], *_plan_rs_owner(len(sums)))

    (dr1, dr1b, dg1, db1), (_, _, sib_1, sib_2) = _matmul(
        "mm_dh1_ln1_bwd", df1b, w1_g, dims=NT, grid=(t // tm_ep, 1, N_DEV),
        a_spec=pl.BlockSpec((tm_ep, ws_f), lambda m, n, k: (m, k)),
        b_spec=pl.BlockSpec((None, d, ws_f), lambda m, n, k: (k, 0, 0)),
        extras=(dr2, r1, ln1_g), extra_specs=(row_ep, row_ep, vec),
        out_shape=[sds((t, d)), sds((t, d), BF16), sds((1, d)), sds((1, d))], out_specs=[row_ep, row_ep, vec, vec],
        acc_shape=(tm_ep, d), epilogue=ep_ln1_bwd, sem=seq,
        comm=to_sibling([dw1_b, dw2_b]))
    own_1 = to_owner(["w_ff1"], [dw1], [sib_1])
    own_2 = to_owner(["w_ff2"], [dw2], [sib_2])[0]

    dwout, dwout_b = [a_.reshape(N_DEV, ws_out, d) for a_ in _matmul(
        "mm_dwout", mixin, dr1b, dims=TN, grid=(d // tn_d, d // tn_d, t // tkk),
        a_spec=pl.BlockSpec((tkk, tn_d), lambda m, n, k: (k, m)),
        b_spec=pl.BlockSpec((tkk, tn_d), lambda m, n, k: (k, n)),
        out_shape=[sds((d, d)), sds((d, d), BF16)], out_specs=[pl.BlockSpec((tn_d, tn_d), lambda m, n, k: (m, n))] * 2,
        acc_shape=(tn_d, tn_d), epilogue=_store_f32_and_bf16)]

    tn_mix = min(tn_d, p, d - p)

    def mm_dmixin(name, first, width, dtype, comm=None):
        return _matmul(
            name, dr1b, wout_2d, dims=NT, grid=(t // tm_big, width // tn_mix, 1),
            a_spec=pl.BlockSpec((tm_big, d), lambda m, n, k: (m, 0)),
            b_spec=pl.BlockSpec((tn_mix, d), lambda m, n, k: (n + first // tn_mix, 0)),
            out_shape=[sds((t, width), dtype)], out_specs=[pl.BlockSpec((tm_big, tn_mix), lambda m, n, k: (m, n))],
            acc_shape=(tm_big, tn_mix), epilogue=_store_epilogue(dtype), comm=comm)

    (dy_pool,), (_, sib_out) = mm_dmixin("mm_dmixin_pool", 0, p, F32, comm=to_sibling([dwout_b]))
    (do,) = mm_dmixin("mm_dmixin_att", p, d - p, BF16)

    (du_pool, dwp, dsc), (_, chips_out) = _pool_bwd(
        dy_pool, ypre, wp_full, sc_full, t, c_pool, comm=to_owner(["w_out"], [dwout], [sib_out]))
    (dq, dk, dv), (_, chips_1) = _attn_bwd(qkv, do, o, t, n_heads, comm=own_1)
    dub = jnp.concatenate([du_pool, dq, dk, dv], axis=1)

    (dwin, dwin_b), own_2 = _matmul(
        "mm_dwin", h0b, dub, dims=TN, grid=(d // tn_d, N_DEV, t // tkk),
        a_spec=pl.BlockSpec((tkk, tn_d), lambda m, n, k: (k, m)),
        b_spec=pl.BlockSpec((tkk, ws_in), lambda m, n, k: (k, n)),
        out_shape=[sds((N_DEV, d, ws_in)), sds((N_DEV, d, ws_in), BF16)],
        out_specs=[pl.BlockSpec((None, tn_d, ws_in), lambda m, n, k: (n, m, 0))] * 2,
        acc_shape=(tn_d, ws_in), epilogue=_store_f32_and_bf16, comm=(own_2, *_plan_rs_owner(1, (1, 2))))
    dwp_g = dwp.reshape(n_groups, N_DEV, pr, c_pool).transpose(1, 0, 2, 3).reshape(N_DEV, n_groups * pr, c_pool)

    def ep_ln0_bwd(acc, ex, outs):
        dr1_ref, x_ref, g_ref = ex
        first = pl.program_id(0) == 0
        xhat, rstd = _ln_stats(x_ref[...])
        dx, dg, db = _ln_bwd(DEEPNORM_ALPHA * dr1_ref[...] + acc, xhat, rstd, g_ref[...])
        outs[0][...] = dx
        _acc_rows(first, outs[1], dg)
        _acc_rows(first, outs[2], db)

    sib_in = to_sibling([dwin_b, dwp_g.astype(BF16)])
    to_diagonal = _plan_rs_owner(1, (3,))
    last_host = (own_2 + sib_in[0], *_join_plans((to_diagonal[0], 2, to_diagonal[1]), (sib_in[1], 4, sib_in[2])))
    def two_blocks(a_ref, b_ref):
        return _dot(a_ref[:, :ws_in], b_ref[0], NT) + _dot(a_ref[:, ws_in:], b_ref[1], NT)

    (dx, dg0, db0), (_, chips_2, _, _, sib_in_, sib_p) = _matmul(
        "mm_dh0_ln0_bwd", dub, win_g, dims=NT, grid=(t // tm_ep, 1, N_DEV // 2),
        a_spec=pl.BlockSpec((tm_ep, 2 * ws_in), lambda m, n, k: (m, k)),
        b_spec=pl.BlockSpec((2, d, ws_in), lambda m, n, k: (k, 0, 0)),
        extras=(dr1, x2, g0), extra_specs=(row_ep, row_ep, vec),
        out_shape=[sds((t, d)), sds((1, d)), sds((1, d))], out_specs=[row_ep, vec, vec],
        acc_shape=(tm_ep, d), epilogue=ep_ln0_bwd, sem=seq,
        comm=last_host, dot_fn=two_blocks)

    _, _, chips_in, chips_p = _copies_now("rs_owner_w_in", *to_owner(["w_in", "w_pool"], [dwin, dwp_g], [sib_in_, sib_p]))
    w_of = {"w_in": shards[0], "w_out": shards[1], "w_ff1": shards[2], "w_ff2": shards[3], "w_pool": shards[4]}
    mv_of = {"w_in": (m_w_in, v_w_in), "w_out": (m_w_out, v_w_out), "w_ff1": (m_w_ff1, v_w_ff1),
             "w_ff2": (m_w_ff2, v_w_ff2), "w_pool": (m_w_pool, v_w_pool)}
    big = {}
    for nm, pt, fs, fc in [("w_ff1", dw1, sib_1, chips_1), ("w_ff2", dw2, sib_2, chips_2), ("w_out", dwout, sib_out, chips_out),
                           ("w_in", dwin, sib_in_, chips_in), ("w_pool", dwp_g, sib_p, chips_p)]:
        w2d = w_of[nm]
        m_, v_ = mv_of[nm]
        big[nm] = _rs_final_adamw("rs_final_adamw_" + nm, slots, pt, fs, fc, w2d, m_.reshape(w2d.shape), v_.reshape(w2d.shape))

    n_f_rows = f // d
    pad_sc = d - p
    packet = jnp.concatenate(
        [loss_vec, dg0, db0, dg1, db1, dbf2, dg2, db2, dbf1.reshape(n_f_rows, d),
         jnp.pad(dsc.reshape(1, p), ((0, 0), (0, pad_sc)))], axis=0)
    n_rows = packet.shape[0]
    n_pad = (-n_rows) % 8
    packet = jnp.pad(packet, ((0, n_pad), (0, 0)))
    sums, loss11 = _small_all_reduce(packet)
    dsc_full = sums[8 + n_f_rows, :p].reshape(n_groups, N_DEV, pr)
    dsc_mine = lax.dynamic_index_in_dim(dsc_full, me, axis=1, keepdims=False)

    def sc_row(a):
        return jnp.pad(a.reshape(1, n_groups * pr), ((0, 0), (0, d - n_groups * pr)))

    def small_pack(ln0g, ln0b, l1g, l1b, bf2, l2g, l2b, bf1, sc):
        rows = [jnp.zeros((1, d), F32), ln0g.reshape(1, d), ln0b.reshape(1, d), l1g, l1b, bf2, l2g, l2b,
                bf1.reshape(n_f_rows, d), sc_row(sc), jnp.zeros((n_pad, d), F32)]
        return jnp.concatenate(rows, axis=0)

    w_small = small_pack(ln_in_g, ln_in_b, ln1_g, ln1_b, b_ff2, ln2_g, ln2_b, b_ff1, pool_scale)
    m_small = small_pack(m_ln_in_g, m_ln_in_b, m_ln1_g, m_ln1_b, m_b_ff2, m_ln2_g, m_ln2_b, m_b_ff1, m_pool_scale)
    v_small = small_pack(v_ln_in_g, v_ln_in_b, v_ln1_g, v_ln1_b, v_b_ff2, v_ln2_g, v_ln2_b, v_b_ff1, v_pool_scale)
    g_small = jnp.concatenate([sums[:8 + n_f_rows], sc_row(dsc_mine), jnp.zeros((n_pad, d), F32)], axis=0)
    small = (g_small,) + tuple(_small_adamw(w_small, g_small, m_small, v_small))

    def unpack(a):
        sc = a[8 + n_f_rows, :n_groups * pr].reshape(1, n_groups, pr)
        return {"ln_in_g": a[1], "ln_in_b": a[2], "ln1_g": a[3:4], "ln1_b": a[4:5], "b_ff2": a[5:6], "ln2_g": a[6:7],
                "ln2_b": a[7:8], "b_ff1": a[8:8 + n_f_rows].reshape(1, f), "pool_scale": sc}

    shapes = {"w_in": w_in.shape, "w_out": w_out.shape, "w_ff1": w_ff1.shape, "w_ff2": w_ff2.shape, "w_pool": w_pool.shape}
    order = ["ln_in_g", "ln_in_b", "w_in", "w_pool", "pool_scale", "w_out", "ln1_g", "ln1_b", "w_ff1", "b_ff1", "w_ff2",
             "b_ff2", "ln2_g", "ln2_b"]
    outs = []
    for kind in range(4):
        small_k = unpack(small[kind])
        for nm in order:
            outs.append(big[nm][kind].reshape(shapes[nm]) if nm in big else small_k[nm])
    return (loss11.reshape(()), dx.reshape(x.shape), *outs)
```

```python
import functools
import math

import jax
import jax.numpy as jnp
from jax import lax
from jax.experimental import pallas as pl
from jax.experimental.pallas import tpu as pltpu

F32 = jnp.float32
BF16 = jnp.bfloat16
MESH = pl.DeviceIdType.MESH

N_DEV = 8
HEAD_DIM = 128
POOL_WINDOWS = (2, 4, 8, 16)
DEEPNORM_ALPHA = (2.0 * 1) ** 0.25
LN_EPS = 1e-5
ADAM_LR = 0.001
ADAM_B1 = 0.9
ADAM_B2 = 0.999
ADAM_EPS = 1e-08
ADAM_WD = 0.01
ADAM_STEP = 10

V7X_VMEM_LIMIT = 56 * 1024 * 1024
ATT_BLOCK = 256
POOL_CHUNK = 256

NN = (((1,), (0,)), ((), ()))
NT = (((1,), (1,)), ((), ()))
TN = (((0,), (0,)), ((), ()))


def _dot(a, b, dims=NN):
    return lax.dot_general(a, b, dims, preferred_element_type=F32)


def _cparams(sem=None):
    return pltpu.CompilerParams(dimension_semantics=sem, vmem_limit_bytes=V7X_VMEM_LIMIT)


def _ln_stats(r):
    mu = jnp.mean(r, axis=-1, keepdims=True)
    xc = r - mu
    var = jnp.mean(xc * xc, axis=-1, keepdims=True)
    rstd = lax.rsqrt(var + LN_EPS)
    return xc * rstd, rstd


def _ln_bwd(dy, xhat, rstd, g):
    dxh = dy * g
    m1 = jnp.mean(dxh, axis=-1, keepdims=True)
    m2 = jnp.mean(dxh * xhat, axis=-1, keepdims=True)
    dx = rstd * (dxh - m1 - xhat * m2)
    dg = jnp.sum(dy * xhat, axis=0, keepdims=True)
    db = jnp.sum(dy, axis=0, keepdims=True)
    return dx, dg, db


def _acc_rows(first, ref, val):
    @pl.when(first)
    def _():
        ref[...] = val

    @pl.when(jnp.logical_not(first))
    def _():
        ref[...] += val


def _call(body, *, name, grid, in_specs, out_specs, out_shape, inputs, scratch_shapes=(), sem=None, comm=None):
    in_specs, out_specs, out_shape, inputs = list(in_specs), list(out_specs), list(out_shape), list(inputs)
    if comm is None:
        outs = pl.pallas_call(
            body, name=name, grid=grid, in_specs=in_specs, out_specs=out_specs, out_shape=out_shape,
            scratch_shapes=list(scratch_shapes), compiler_params=_cparams(sem))(*inputs)
        return list(outs), []
    arrays, plan, n_copies = comm
    n_in, n_out, nc, n_scr = len(inputs), len(out_shape), len(arrays), len(scratch_shapes)

    def hosted(*refs):
        ins = refs[:n_in]
        outs = refs[n_in + nc:n_in + nc + n_out]
        passed = refs[n_in + nc + n_out:n_in + 2 * nc + n_out]
        scratch = refs[n_in + 2 * nc + n_out:n_in + 2 * nc + n_out + n_scr]
        send_sems, recv_sems = refs[-2], refs[-1]
        ids = [pl.program_id(ax) for ax in range(len(grid))]
        first = functools.reduce(jnp.logical_and, [i_ == 0 for i_ in ids])
        last = functools.reduce(jnp.logical_and, [i_ == g - 1 for i_, g in zip(ids, grid)])

        @pl.when(first)
        def _():
            for cp in _plan_copies(plan, passed, send_sems, recv_sems):
                cp.start()

        body(*ins, *outs, *scratch)

        @pl.when(last)
        def _():
            for cp in _plan_copies(plan, passed, send_sems, recv_sems):
                cp.wait_send()
                cp.wait_recv()

    any_spec = pl.BlockSpec(memory_space=pl.ANY)
    outs = pl.pallas_call(
        hosted, name=name, grid=grid,
        in_specs=in_specs + [any_spec] * nc, out_specs=out_specs + [any_spec] * nc,
        out_shape=out_shape + [jax.ShapeDtypeStruct(a.shape, a.dtype) for a in arrays],
        scratch_shapes=list(scratch_shapes) + [pltpu.SemaphoreType.DMA((n_copies,)), pltpu.SemaphoreType.DMA((n_copies,))],
        input_output_aliases={n_in + i: n_out + i for i in range(nc)},
        compiler_params=pltpu.CompilerParams(dimension_semantics=("arbitrary",) * len(grid),
                                             vmem_limit_bytes=V7X_VMEM_LIMIT, has_side_effects=True),
    )(*inputs, *arrays)
    return list(outs[:n_out]), list(outs[n_out:])


def _matmul(name, a, b, *, dims, grid, a_spec, b_spec, extras=(), extra_specs=(), out_shape, out_specs,
            acc_shape, epilogue, k_axis=2, sem=("parallel", "parallel", "arbitrary"), comm=None, dot_fn=None):
    nk = grid[k_axis]
    n_extra = len(extras)
    n_out = len(out_shape)
    if dot_fn is None:
        def dot_fn(a_ref, b_ref):
            return _dot(a_ref[...], b_ref[...], dims)

    def body(a_ref, b_ref, *rest):
        extra_refs = rest[:n_extra]
        out_refs = rest[n_extra:n_extra + n_out]
        if nk == 1:
            epilogue(dot_fn(a_ref, b_ref), extra_refs, out_refs)
            return
        acc_ref = rest[n_extra + n_out]
        k = pl.program_id(k_axis)

        @pl.when(k == 0)
        def _():
            acc_ref[...] = jnp.zeros(acc_shape, F32)

        acc_ref[...] += dot_fn(a_ref, b_ref)

        @pl.when(k == nk - 1)
        def _():
            epilogue(acc_ref[...], extra_refs, out_refs)

    outs, passed = _call(
        body, name=name, grid=grid, in_specs=[a_spec, b_spec, *extra_specs], out_specs=out_specs, out_shape=out_shape,
        inputs=[a, b, *extras], scratch_shapes=[] if nk == 1 else [pltpu.VMEM(acc_shape, F32)], sem=sem, comm=comm)
    return outs if comm is None else (outs, passed)


def _store_epilogue(dtype):

    def ep(acc, extra_refs, out_refs):
        out_refs[0][...] = acc.astype(dtype)
    return ep


def _ln_in_fwd(x, g, b, tm, comm=None):
    t, d = x.shape

    def body(x_ref, g_ref, b_ref, h_ref, hb_ref):
        xhat, _ = _ln_stats(x_ref[...])
        h = xhat * g_ref[...] + b_ref[...]
        h_ref[...] = h
        hb_ref[...] = h.astype(BF16)

    row = pl.BlockSpec((tm, d), lambda i: (i, 0))
    vec = pl.BlockSpec((1, d), lambda i: (0, 0))
    return _call(
        body, name="ln_in_fwd", grid=(t // tm,), in_specs=[row, vec, vec], out_specs=[row, row],
        out_shape=[jax.ShapeDtypeStruct((t, d), F32), jax.ShapeDtypeStruct((t, d), BF16)],
        inputs=[x, g, b], sem=("parallel",), comm=comm)


def _split3(x):
    hi = x.astype(BF16)
    r = x - hi.astype(F32)
    mid = r.astype(BF16)
    lo = (r - mid.astype(F32)).astype(BF16)
    return hi, mid, lo


def _split2(x):
    hi = x.astype(BF16)
    lo = (x - hi.astype(F32)).astype(BF16)
    return hi, lo


def _pool_fwd(u, wp, sc, t, c, comm=None):
    n_groups = len(POOL_WINDOWS)
    tc = POOL_CHUNK
    n_chunks = t // tc

    def body(u_ref, wp_ref, sc_ref, y_ref, ypre_ref, xp_ref):
        g = pl.program_id(0)
        xp_ref[pl.ds(0, tc), :] = jnp.zeros((tc, c), F32)
        xp_ref[pl.ds(tc, t), :] = u_ref[...]
        out_i = lax.broadcasted_iota(jnp.int32, (tc, 2 * tc), 0)
        in_j = lax.broadcasted_iota(jnp.int32, (tc, 2 * tc), 1)
        lag = tc + out_i - in_j
        t_in_chunk = lax.broadcasted_iota(jnp.int32, (tc, 1), 0)
        for gi, w in enumerate(POOL_WINDOWS):
            @pl.when(g == gi)
            def _(w=w):
                band = jnp.logical_and(lag >= 0, lag < w).astype(BF16)

                def chunk(ci, carry):
                    start = pl.multiple_of(ci * tc, tc)
                    win = xp_ref[pl.ds(start, 2 * tc), :]
                    hi, mid, lo = _split3(win)
                    wsum = _dot(band, hi) + _dot(band, mid) + _dot(band, lo)
                    cnt = jnp.minimum(ci * tc + t_in_chunk + 1, w).astype(F32)
                    ypre = wsum * (1.0 / cnt) - win[tc:, :]
                    ypre_b = ypre.astype(BF16)
                    y = _dot(ypre_b, wp_ref[...]) * sc_ref[...]
                    ypre_ref[pl.ds(start, tc), :] = ypre_b
                    y_ref[pl.ds(start, tc), :] = y.astype(BF16)
                    return carry

                lax.fori_loop(0, n_chunks, chunk, 0)

    col = pl.BlockSpec((t, c), lambda g: (0, g))
    return _call(
        body, name="pool_fwd", grid=(n_groups,),
        in_specs=[col, pl.BlockSpec((None, c, c), lambda g: (g, 0, 0)), pl.BlockSpec((None, 1, c), lambda g: (g, 0, 0))],
        out_specs=[col, col],
        out_shape=[jax.ShapeDtypeStruct((t, n_groups * c), BF16), jax.ShapeDtypeStruct((t, n_groups * c), BF16)],
        inputs=[u, wp, sc], scratch_shapes=[pltpu.VMEM((t + tc, c), F32)], sem=("parallel",), comm=comm)


def _pool_bwd(dmixin, ypre, wp, sc, t, c, comm=None):
    n_groups = len(POOL_WINDOWS)
    tc = POOL_CHUNK
    n_chunks = t // tc

    def body(dy_ref, ypre_ref, wp_ref, sc_ref, du_ref, dwp_ref, dsc_ref, zp_ref):
        g = pl.program_id(0)
        zp_ref[pl.ds(t, tc), :] = jnp.zeros((tc, c), F32)
        dwp_ref[...] = jnp.zeros((c, c), F32)
        dsc_ref[...] = jnp.zeros((1, c), F32)
        out_i = lax.broadcasted_iota(jnp.int32, (tc, 2 * tc), 0)
        in_j = lax.broadcasted_iota(jnp.int32, (tc, 2 * tc), 1)
        lead = in_j - out_i
        t_in_chunk = lax.broadcasted_iota(jnp.int32, (tc, 1), 0)
        for gi, w in enumerate(POOL_WINDOWS):
            @pl.when(g == gi)
            def _(w=w):
                band = jnp.logical_and(lead >= 0, lead < w).astype(BF16)

                def first(ci, carry):
                    start = pl.multiple_of(ci * tc, tc)
                    dy = dy_ref[pl.ds(start, tc), :]
                    yp = ypre_ref[pl.ds(start, tc), :]
                    ymm = _dot(yp, wp_ref[...])
                    dsc_ref[...] += jnp.sum(dy * ymm, axis=0, keepdims=True)
                    dys_b = (dy * sc_ref[...]).astype(BF16)
                    dwp_ref[...] += _dot(yp, dys_b, TN)
                    dyp = _dot(dys_b, wp_ref[...], NT)
                    cnt = jnp.minimum(ci * tc + t_in_chunk + 1, w).astype(F32)
                    zp_ref[pl.ds(start, tc), :] = dyp * (1.0 / cnt)
                    du_ref[pl.ds(start, tc), :] = -dyp
                    return carry

                lax.fori_loop(0, n_chunks, first, 0)

                def second(ci, carry):
                    start = pl.multiple_of(ci * tc, tc)
                    hi, mid, lo = _split3(zp_ref[pl.ds(start, 2 * tc), :])
                    du_ref[pl.ds(start, tc), :] += _dot(band, hi) + _dot(band, mid) + _dot(band, lo)
                    return carry

                lax.fori_loop(0, n_chunks, second, 0)

    col = pl.BlockSpec((t, c), lambda g: (0, g))
    return _call(
        body, name="pool_bwd", grid=(n_groups,),
        in_specs=[col, col, pl.BlockSpec((None, c, c), lambda g: (g, 0, 0)), pl.BlockSpec((None, 1, c), lambda g: (g, 0, 0))],
        out_specs=[col, pl.BlockSpec((None, c, c), lambda g: (g, 0, 0)), pl.BlockSpec((None, 1, c), lambda g: (g, 0, 0))],
        out_shape=[jax.ShapeDtypeStruct((t, n_groups * c), F32), jax.ShapeDtypeStruct((n_groups, c, c), F32),
                   jax.ShapeDtypeStruct((n_groups, 1, c), F32)],
        inputs=[dmixin, ypre, wp, sc], scratch_shapes=[pltpu.VMEM((t + tc, c), F32)], sem=("parallel",), comm=comm)


ROW_PARTS = 2


def _att_consts():
    b = ATT_BLOCK
    rp = b // ROW_PARTS
    row = lax.broadcasted_iota(jnp.int32, (b, b), 0)
    col = lax.broadcasted_iota(jnp.int32, (b, b), 1)
    tri = (row >= col).astype(BF16)
    prow = lax.broadcasted_iota(jnp.int32, (rp, b), 0)
    pcol = lax.broadcasted_iota(jnp.int32, (rp, b), 1)
    causal = [pcol < prow + r * rp for r in range(ROW_PARTS)]
    return tri, causal


def _suffix_sum(x, tri):
    hi, lo = _split2(x)
    return _dot(hi, tri) + _dot(lo, tri)


LOG2_E = 1.4426950408889634


def _att_scores(qb, kb, mask):
    z2 = _dot(qb, kb, NT) * (LOG2_E / math.sqrt(HEAD_DIM))
    sp2 = jnp.maximum(z2, 0.0) + jnp.log2(1.0 + jnp.exp2(-jnp.abs(z2)))
    return z2, sp2, (sp2 if mask is None else jnp.where(mask, sp2, 0.0))


HEADS_PER_STEP = 2
ATT_LANES = HEADS_PER_STEP * HEAD_DIM


def _head_lanes(s):
    return slice(s * HEAD_DIM, (s + 1) * HEAD_DIM)


UNDERFLOW_LOG2 = 160.0


def _sweep_earlier_blocks(i, state, per_chain, block):
    def lowest(st):
        low = st[0]
        for k in range(per_chain, len(st), per_chain):
            low = jnp.minimum(low, st[k])
        return jnp.min(low)

    def more(c):
        return jnp.logical_and(c[0] < i, c[1] < UNDERFLOW_LOG2)

    def trip(c):
        st = block(i - 1 - c[0], c[2:])
        return (c[0] + 1, lowest(st)) + tuple(st)

    return lax.while_loop(more, trip, (jnp.int32(0), lowest(state)) + tuple(state))[2:]


def _attn_fwd(qkv, t, n_heads, comm=None):
    b = ATT_BLOCK
    nq = t // b
    n_steps = n_heads // HEADS_PER_STEP

    rp = b // ROW_PARTS
    chains = [(s, r) for s in range(HEADS_PER_STEP) for r in range(ROW_PARTS)]
    no_mask = [None] * ROW_PARTS

    def body(q_ref, k_ref, v_ref, o_ref):
        tri, causal = _att_consts()

        def blocks(qbs, j, state, masks):
            ks = pl.multiple_of(j * b, b)
            scores = [_att_scores(qbs[ci], k_ref[pl.ds(ks, b), _head_lanes(s)], masks[r]) for ci, (s, r) in enumerate(chains)]
            incls = [_suffix_sum(sc[2], tri) for sc in scores]
            out = []
            for ci, (s, r) in enumerate(chains):
                carry, acc = state[2 * ci], state[2 * ci + 1]
                a = jnp.exp2(scores[ci][0] - (incls[ci] + carry))
                if masks[r] is not None:
                    a = jnp.where(masks[r], a, 0.0)
                out += [carry + incls[ci][:, 0:1], acc + _dot(a.astype(BF16), v_ref[pl.ds(ks, b), _head_lanes(s)])]
            return tuple(out)

        def q_loop(i, _):
            qs = pl.multiple_of(i * b, b)
            qbs = [q_ref[pl.ds(qs + r * rp, rp), _head_lanes(s)] for s, r in chains]
            zero = (jnp.zeros((rp, 1), F32), jnp.zeros((rp, HEAD_DIM), F32)) * len(chains)
            state = blocks(qbs, i, zero, causal)
            state = _sweep_earlier_blocks(i, state, 2, lambda j, st: blocks(qbs, j, st, no_mask))
            for ci, (s, r) in enumerate(chains):
                o_ref[pl.ds(qs + r * rp, rp), _head_lanes(s)] = state[2 * ci + 1]
            return 0

        lax.fori_loop(0, nq, q_loop, 0)

    def heads(off):
        return pl.BlockSpec((t, ATT_LANES), lambda h: (0, off + h))

    return _call(
        body, name="attn_fwd", grid=(n_steps,),
        in_specs=[heads(0), heads(n_steps), heads(2 * n_steps)], out_specs=[heads(0)],
        out_shape=[jax.ShapeDtypeStruct((t, n_heads * HEAD_DIM), F32)],
        inputs=[qkv, qkv, qkv], sem=("parallel",), comm=comm)


def _attn_bwd(qkv, do, o, t, n_heads, comm=None):
    b = ATT_BLOCK
    nq = t // b
    n_steps = n_heads // HEADS_PER_STEP
    scale = 1.0 / math.sqrt(HEAD_DIM)
    rp = b // ROW_PARTS
    chains = [(s, r) for s in range(HEADS_PER_STEP) for r in range(ROW_PARTS)]
    no_mask = [None] * ROW_PARTS

    def body(q_ref, k_ref, v_ref, do_ref, o_ref, dq_ref, dk_ref, dv_ref, qt_ref, dot_ref, dkt_ref, dvt_ref):
        for j in range(nq):
            rows = pl.ds(j * b, b)
            qt_ref[j] = q_ref[rows, :].astype(F32).T.astype(BF16)
            dot_ref[j] = do_ref[rows, :].astype(F32).T.astype(BF16)
        dkt_ref[...] = jnp.zeros((nq, ATT_LANES, b), F32)
        dvt_ref[...] = jnp.zeros((nq, ATT_LANES, b), F32)
        tri, causal = _att_consts()

        def blocks(i, fixed, j, state, masks):
            ks = pl.multiple_of(j * b, b)
            n = len(chains)
            kbs = [k_ref[pl.ds(ks, b), _head_lanes(s)] for s, _ in chains]
            scores = [_att_scores(fixed[ci][0], kbs[ci], masks[r]) for ci, (s, r) in enumerate(chains)]
            incls = [_suffix_sum(sc[2], tri) for sc in scores]
            das = [_dot(fixed[ci][1], v_ref[pl.ds(ks, b), _head_lanes(s)], NT) for ci, (s, r) in enumerate(chains)]
            a_bs, gs = [], []
            for ci, (s, r) in enumerate(chains):
                a = jnp.exp2(scores[ci][0] - (incls[ci] + state[3 * ci]))
                if masks[r] is not None:
                    a = jnp.where(masks[r], a, 0.0)
                a_bs.append(a.astype(BF16))
                gs.append(a_bs[ci].astype(F32) * das[ci])
            g_incls = [_suffix_sum(g, tri) for g in gs]
            dz_bs = []
            for ci, (s, r) in enumerate(chains):
                rest = (fixed[ci][2] - state[3 * ci + 1]) - (g_incls[ci] - gs[ci])
                sig = jnp.exp2(scores[ci][0] - scores[ci][1])
                dz = (gs[ci] - sig * rest) * scale
                if masks[r] is not None:
                    dz = jnp.where(masks[r], dz, 0.0)
                dz_bs.append(dz.astype(BF16))
            out = []
            for ci in range(n):
                out += [state[3 * ci] + incls[ci][:, 0:1], state[3 * ci + 1] + g_incls[ci][:, 0:1],
                        state[3 * ci + 2] + _dot(dz_bs[ci], kbs[ci])]
            for s in range(HEADS_PER_STEP):
                lanes = _head_lanes(s)
                dk_add, dv_add = None, None
                for ci, (cs, r) in enumerate(chains):
                    if cs == s:
                        part = slice(r * rp, (r + 1) * rp)
                        dk_c = _dot(qt_ref[i, lanes, part], dz_bs[ci])
                        dv_c = _dot(dot_ref[i, lanes, part], a_bs[ci])
                        dk_add = dk_c if dk_add is None else dk_add + dk_c
                        dv_add = dv_c if dv_add is None else dv_add + dv_c
                dkt_ref[j, lanes, :] += dk_add
                dvt_ref[j, lanes, :] += dv_add
            return tuple(out)

        def q_loop(i, _):
            qs = pl.multiple_of(i * b, b)
            fixed = []
            for s, r in chains:
                rows = pl.ds(qs + r * rp, rp)
                dob = do_ref[rows, _head_lanes(s)]
                total = jnp.sum(dob.astype(F32) * o_ref[rows, _head_lanes(s)], axis=-1, keepdims=True)
                fixed.append((q_ref[rows, _head_lanes(s)], dob, total))
            zero = (jnp.zeros((rp, 1), F32), jnp.zeros((rp, 1), F32), jnp.zeros((rp, HEAD_DIM), F32)) * len(chains)
            state = blocks(i, fixed, i, zero, causal)
            state = _sweep_earlier_blocks(i, state, 3, lambda j, st: blocks(i, fixed, j, st, no_mask))
            for ci, (s, r) in enumerate(chains):
                dq_ref[pl.ds(qs + r * rp, rp), _head_lanes(s)] = state[3 * ci + 2].astype(BF16)
            return 0

        lax.fori_loop(0, nq, q_loop, 0)
        for j in range(nq):
            rows = pl.ds(j * b, b)
            dk_ref[rows, :] = dkt_ref[j].T.astype(BF16)
            dv_ref[rows, :] = dvt_ref[j].T.astype(BF16)

    def heads(off):
        return pl.BlockSpec((t, ATT_LANES), lambda h: (0, off + h))

    shape = jax.ShapeDtypeStruct((t, n_heads * HEAD_DIM), BF16)
    return _call(
        body, name="attn_bwd", grid=(n_steps,),
        in_specs=[heads(0), heads(n_steps), heads(2 * n_steps), heads(0), heads(0)],
        out_specs=[heads(0)] * 3, out_shape=[shape] * 3, inputs=[qkv, qkv, qkv, do, o],
        scratch_shapes=[pltpu.VMEM((nq, ATT_LANES, b), BF16)] * 2 + [pltpu.VMEM((nq, ATT_LANES, b), F32)] * 2,
        sem=("parallel",), comm=comm)


def _place():
    x, y, c = lax.axis_index("x"), lax.axis_index("y"), lax.axis_index("c")
    return x, y, c


def _flip(v, on):
    return 1 - v if on else v


class _LocalCopy:
    def __init__(self, src, dst, sem):
        self.copy = pltpu.make_async_copy(src, dst, sem)

    def start(self):
        self.copy.start()

    def wait_send(self):
        self.copy.wait()

    def wait_recv(self):
        pass


def _plan_copies(plan, refs, send_sems, recv_sems):
    return [_LocalCopy(src, dst, send_sems.at[k]) if dev is None else
            pltpu.make_async_remote_copy(src_ref=src, dst_ref=dst, send_sem=send_sems.at[k], recv_sem=recv_sems.at[k],
                                         device_id=dev, device_id_type=MESH)
            for k, (src, dst, dev) in enumerate(plan(refs))]


def _copies_now(name, arrays, plan, n_copies):
    n = len(arrays)

    def body(*refs):
        copies = _plan_copies(plan, refs[n:2 * n], refs[2 * n], refs[2 * n + 1])
        for cp in copies:
            cp.start()
        for cp in copies:
            cp.wait_send()
            cp.wait_recv()

    any_spec = pl.BlockSpec(memory_space=pl.ANY)
    return list(pl.pallas_call(
        body, name=name, in_specs=[any_spec] * n, out_specs=[any_spec] * n,
        out_shape=[jax.ShapeDtypeStruct(a.shape, a.dtype) for a in arrays],
        input_output_aliases={i: i for i in range(n)},
        scratch_shapes=[pltpu.SemaphoreType.DMA((n_copies,)), pltpu.SemaphoreType.DMA((n_copies,))],
        compiler_params=pltpu.CompilerParams(has_side_effects=True),
    )(*arrays))


SIBLING, ACROSS_Y, ACROSS_X, DIAGONAL = 1, 2, 4, 6


def _plan_gather_own(peers, rows=None, block_too=False):
    def plan(refs):
        x, y, c = _place()
        mine = refs[0].at[4 * x + 2 * y + c]
        src = refs[1] if block_too else mine
        if rows is not None:
            mine, src = mine.at[pl.ds(*rows)], src.at[pl.ds(*rows)]
        out = [(src, mine, (_flip(x, k & 4), _flip(y, k & 2), _flip(c, k & 1))) for k in peers]
        if block_too:
            out.append((refs[1], refs[0].at[4 * x + 2 * y + c], None))
        return out
    return plan, len(peers) + (1 if block_too else 0)


def _plan_gather_forward(n):
    def plan(refs):
        x, y, c = _place()
        out = []
        for ti in range(n):
            for r in range(1, 4):
                blk = refs[ti].at[4 * _flip(x, r & 2) + 2 * _flip(y, r & 1) + c]
                out.append((blk, blk, (x, y, 1 - c)))
        return out
    return plan, 3 * n


def _join_plans(*parts):
    def plan(refs):
        out, at = [], 0
        for part, n_arrays, _ in parts:
            out += part(refs[at:at + n_arrays])
            at += n_arrays
        return out
    return plan, sum(n_cp for _, _, n_cp in parts)


def _plan_rs_sibling(n):
    def plan(refs):
        x, y, c = _place()
        out = []
        for ti in range(n):
            for r in range(4):
                src = refs[ti].at[4 * _flip(x, r & 2) + 2 * _flip(y, r & 1) + (1 - c)]
                out.append((src, refs[n + ti].at[r], (x, y, 1 - c)))
        return out
    return plan, 4 * n


def _plan_rs_owner(n, relations=(1, 2, 3)):
    def plan(refs):
        x, y, c = _place()
        out = []
        for ti in range(n):
            for r in relations:
                out.append((refs[ti].at[r], refs[n + ti].at[r], (_flip(x, r & 2), _flip(y, r & 1), c)))
        return out
    return plan, len(relations) * n


def _owner_slots():
    x, y, c = _place()
    idx = []
    for r in range(4):
        ox, oy = (1 - x if r & 2 else x), (1 - y if r & 1 else y)
        idx.append(4 * ox + 2 * oy + c)
    return jnp.stack(idx).astype(jnp.int32)


def _row_tile(rows, cols):
    tr = max(8, min(rows, (1 << 19) // cols))
    while rows % tr:
        tr //= 2
    return tr


def _rs_chip_sum(name, slots, partial, from_sibling):
    _, rows, cols = partial.shape
    tr = _row_tile(rows, cols)

    def body(slots_ref, p_ref, s_ref, o_ref):
        o_ref[...] = (p_ref[...] + s_ref[...]).astype(BF16)

    grid_spec = pltpu.PrefetchScalarGridSpec(
        num_scalar_prefetch=1, grid=(3, rows // tr),
        in_specs=[pl.BlockSpec((None, tr, cols), lambda r, i, s: (s[r + 1], i, 0)),
                  pl.BlockSpec((None, tr, cols), lambda r, i, s: (r + 1, i, 0))],
        out_specs=pl.BlockSpec((None, tr, cols), lambda r, i, s: (r + 1, i, 0)))
    return pl.pallas_call(
        body, name=name, grid_spec=grid_spec, out_shape=jax.ShapeDtypeStruct((4, rows, cols), BF16),
        compiler_params=_cparams(("parallel", "parallel")),
    )(slots, partial, from_sibling)


def _adamw(w, g, m, v):
    m = ADAM_B1 * m + (1.0 - ADAM_B1) * g
    v = ADAM_B2 * v + (1.0 - ADAM_B2) * (g * g)
    m_hat = m / (1.0 - ADAM_B1 ** ADAM_STEP)
    v_hat = v / (1.0 - ADAM_B2 ** ADAM_STEP)
    delta = -ADAM_LR * (m_hat / (jnp.sqrt(v_hat) + ADAM_EPS) + ADAM_WD * w)
    return delta, m, v


def _rs_final_adamw(name, slots, partial, from_sibling, from_chips, w, m, v):
    rows, cols = w.shape
    tr = _row_tile(rows, cols)

    def body(slots_ref, p_ref, s_ref, c1_ref, c2_ref, c3_ref, w_ref, m_ref, v_ref, g_ref, d_ref, nm_ref, nv_ref):
        g = p_ref[...] + s_ref[...]
        g = g + c1_ref[...].astype(F32)
        g = g + c2_ref[...].astype(F32)
        g = g + c3_ref[...].astype(F32)
        delta, nm, nv = _adamw(w_ref[...], g, m_ref[...], v_ref[...])
        g_ref[...] = g
        d_ref[...] = delta
        nm_ref[...] = nm
        nv_ref[...] = nv

    def slot(r):
        return pl.BlockSpec((None, tr, cols), lambda i, s: (r, i, 0))

    flat = pl.BlockSpec((tr, cols), lambda i, s: (i, 0))
    grid_spec = pltpu.PrefetchScalarGridSpec(
        num_scalar_prefetch=1, grid=(rows // tr,),
        in_specs=[pl.BlockSpec((None, tr, cols), lambda i, s: (s[0], i, 0)), slot(0), slot(1), slot(2), slot(3), flat, flat, flat],
        out_specs=[flat] * 4)
    return pl.pallas_call(
        body, name=name, grid_spec=grid_spec, out_shape=[jax.ShapeDtypeStruct((rows, cols), F32)] * 4,
        compiler_params=_cparams(("parallel",)),
    )(slots, partial, from_sibling, from_chips, from_chips, from_chips, w, m, v)


def _small_all_reduce(packet):
    rows, d = packet.shape

    def body(p_ref, sum_ref, loss_ref, all_ref, send_sems, recv_sems):
        x, y, c = _place()
        me = 4 * x + 2 * y + c
        all_ref[me] = p_ref[...]
        copies = []
        for k in range(1, N_DEV):
            px, py, pc = (1 - x if k & 4 else x), (1 - y if k & 2 else y), (1 - c if k & 1 else c)
            cp = pltpu.make_async_remote_copy(
                src_ref=p_ref, dst_ref=all_ref.at[me], send_sem=send_sems.at[k], recv_sem=recv_sems.at[k],
                device_id=(px, py, pc), device_id_type=MESH)
            cp.start()
            copies.append(cp)
        for cp in copies:
            cp.wait_recv()
        for cp in copies:
            cp.wait_send()
        total = all_ref[0]
        for j in range(1, N_DEV):
            total = total + all_ref[j]
        sum_ref[...] = total
        loss_ref[...] = jnp.sum(total[0:1, :], axis=-1, keepdims=True)

    vmem = pl.BlockSpec(memory_space=pltpu.VMEM)
    return pl.pallas_call(
        body, name="small_all_reduce",
        in_specs=[vmem], out_specs=[vmem, vmem],
        out_shape=[jax.ShapeDtypeStruct((rows, d), F32), jax.ShapeDtypeStruct((1, 1), F32)],
        scratch_shapes=[pltpu.VMEM((N_DEV, rows, d), F32), pltpu.SemaphoreType.DMA((N_DEV,)), pltpu.SemaphoreType.DMA((N_DEV,))],
        compiler_params=pltpu.CompilerParams(has_side_effects=True),
    )(packet)


def _small_adamw(w, g, m, v):
    def body(w_ref, g_ref, m_ref, v_ref, d_ref, nm_ref, nv_ref):
        delta, nm, nv = _adamw(w_ref[...], g_ref[...], m_ref[...], v_ref[...])
        d_ref[...] = delta
        nm_ref[...] = nm
        nv_ref[...] = nv

    vmem = pl.BlockSpec(memory_space=pltpu.VMEM)
    return pl.pallas_call(
        body, name="small_adamw", in_specs=[vmem] * 4, out_specs=[vmem] * 3,
        out_shape=[jax.ShapeDtypeStruct(w.shape, F32)] * 3,
    )(w, g, m, v)


def kernel(x, ln_in_g, ln_in_b, w_in, w_pool, pool_scale, w_out, ln1_g, ln1_b, w_ff1, b_ff1, w_ff2, b_ff2, ln2_g, ln2_b, loss_target, m_ln_in_g, m_ln_in_b, m_w_in, m_w_pool, m_pool_scale, m_w_out, m_ln1_g, m_ln1_b, m_w_ff1, m_b_ff1, m_w_ff2, m_b_ff2, m_ln2_g, m_ln2_b, v_ln_in_g, v_ln_in_b, v_w_in, v_w_pool, v_pool_scale, v_w_out, v_ln1_g, v_ln1_b, v_w_ff1, v_b_ff1, v_w_ff2, v_b_ff2, v_ln2_g, v_ln2_b):
    t, d = x.shape[1], x.shape[2]
    n_groups = len(POOL_WINDOWS)
    c_pool = w_pool.shape[3]
    p = n_groups * c_pool
    n_heads = (d - p) // HEAD_DIM
    ws_in = w_in.shape[2]
    n_in = N_DEV * ws_in
    ws_out = w_out.shape[1]
    ws_f = w_ff1.shape[2]
    f = N_DEV * ws_f
    pr = w_pool.shape[2]
    assert n_in == p + 3 * n_heads * HEAD_DIM and N_DEV * ws_out == d and N_DEV * pr == c_pool

    tm_big = min(t, 1024)
    tm_ep = min(t, 512)
    tkk = min(t, 2048)
    half_f = min(ws_f, 512)
    per_f = ws_f // half_f

    x2 = x.reshape(t, d)
    target = loss_target.reshape(t, d)
    g0, b0 = ln_in_g.reshape(1, d), ln_in_b.reshape(1, d)

    shards = [w_in.reshape(d, ws_in), w_out.reshape(ws_out, d), w_ff1.reshape(d, ws_f), w_ff2.reshape(ws_f, d),
              w_pool.reshape(n_groups * pr, c_pool)]
    x_, y_, c_ = _place()
    me = 4 * x_ + 2 * y_ + c_
    def landing(block):
        return [lax.empty((N_DEV, *block.shape), block.dtype), block]

    land_in, land_out, land_1, land_2, land_pool = [landing(s.astype(BF16)) for s in shards]
    land_scale = landing(pool_scale.reshape(n_groups, pr))

    def sds(shape, dtype=F32):
        return jax.ShapeDtypeStruct(shape, dtype)

    vec = pl.BlockSpec((1, d), lambda m, n, k: (0, 0))
    row_ep = pl.BlockSpec((tm_ep, d), lambda m, n, k: (m, 0))
    tm_res = min(t, 256)
    row_res = pl.BlockSpec((tm_res, d), lambda m, n, k: (m, 0))
    seq = ("arbitrary", "arbitrary", "arbitrary")

    forward = _plan_gather_forward(1)
    two_level_own = _plan_gather_own([SIBLING, ACROSS_Y, ACROSS_X, DIAGONAL], block_too=True)
    (h0, h0b), first_needed = _ln_in_fwd(
        x2, g0, b0, tm_big,
        comm=(land_in + land_pool + land_scale, *_join_plans(*[(two_level_own[0], 2, two_level_own[1])] * 3)))
    win_g, wpool_g, scale_g = _copies_now("gather_forward_w_in", first_needed[0::2], *_plan_gather_forward(3))
    wp_full = wpool_g.reshape(N_DEV, n_groups, pr, c_pool).transpose(1, 0, 2, 3).reshape(n_groups, c_pool, c_pool)
    sc_full = scale_g.transpose(1, 0, 2).reshape(n_groups, 1, c_pool)

    pool_shards = p // ws_in

    def mm_u(name, first, count, dtype, comm=None):
        return _matmul(
            name, h0b, win_g, dims=NN, grid=(t // tm_big, count, 1),
            a_spec=pl.BlockSpec((tm_big, d), lambda m, n, k: (m, 0)),
            b_spec=pl.BlockSpec((None, d, ws_in), lambda m, n, k: (n + first, 0, 0)),
            out_shape=[sds((t, count * ws_in), dtype)],
            out_specs=[pl.BlockSpec((tm_big, ws_in), lambda m, n, k: (m, n))],
            acc_shape=(tm_big, ws_in), epilogue=_store_epilogue(dtype), comm=comm)

    half = land_1[1].shape[0] // 2
    diag_a, diag_b = _plan_gather_own([DIAGONAL], (0, half), block_too=True), _plan_gather_own([DIAGONAL], (half, half))
    (u_pool,), (w1_diag, _) = mm_u("mm_u_pool", 0, pool_shards, F32, comm=(land_1, *diag_a))
    (qkv,), (wout_part, _) = mm_u("mm_u_qkv", pool_shards, N_DEV - pool_shards, BF16, comm=(land_out, *two_level_own))

    (y_pool, ypre), (wout_g, w1_diag) = _pool_fwd(
        u_pool, wp_full, sc_full, t, c_pool,
        comm=([wout_part, w1_diag], *_join_plans((forward[0], 1, forward[1]), (diag_b[0], 1, diag_b[1]))))
    (o,), (w1_part,) = _attn_fwd(qkv, t, n_heads, comm=([w1_diag], *_plan_gather_own([SIBLING, ACROSS_Y, ACROSS_X])))
    mixin = jnp.concatenate([y_pool, o.astype(BF16)], axis=1)
    wout_2d = wout_g.reshape(d, d)

    def ep_ln1(acc, ex, outs):
        h0_ref, g_ref, b_ref = ex
        r1 = DEEPNORM_ALPHA * h0_ref[...] + acc
        xhat, _ = _ln_stats(r1)
        h1 = xhat * g_ref[...] + b_ref[...]
        outs[0][...] = r1
        outs[1][...] = h1
        outs[2][...] = h1.astype(BF16)

    (r1, h1, h1b), (w1_g,) = _matmul(
        "mm_mix_ln1", mixin, wout_2d, dims=NN, grid=(t // tm_res, 1, 1),
        a_spec=pl.BlockSpec((tm_res, d), lambda m, n, k: (m, 0)),
        b_spec=pl.BlockSpec((d, d), lambda m, n, k: (0, 0)),
        extras=(h0, ln1_g, ln1_b), extra_specs=(row_res, vec, vec),
        out_shape=[sds((t, d)), sds((t, d)), sds((t, d), BF16)], out_specs=[row_res] * 3,
        acc_shape=(tm_res, d), epilogue=ep_ln1, comm=([w1_part], *forward))

    def ep_ff1(acc, ex, outs):
        f1 = acc + ex[0][...]
        outs[0][...] = f1
        r = jnp.maximum(f1, 0.0)
        outs[1][...] = (r * r).astype(BF16)

    ff_tile = pl.BlockSpec((tm_big, half_f), lambda m, n, k: (m, n))
    (f1, act), (w2_part, _) = _matmul(
        "mm_ff1", h1b, w1_g, dims=NN, grid=(t // tm_big, f // half_f, 1),
        a_spec=pl.BlockSpec((tm_big, d), lambda m, n, k: (m, 0)),
        b_spec=pl.BlockSpec((None, d, half_f), lambda m, n, k: (n // per_f, 0, n % per_f)),
        extras=(b_ff1,), extra_specs=(pl.BlockSpec((1, half_f), lambda m, n, k: (0, n)),),
        out_shape=[sds((t, f)), sds((t, f), BF16)], out_specs=[ff_tile, ff_tile],
        acc_shape=(tm_big, half_f), epilogue=ep_ff1, comm=(land_2, *two_level_own))
    (w2_g,) = _copies_now("gather_forward_w_ff2", [w2_part], *forward)

    def ep_ln2(acc, ex, outs):
        h1_ref, tgt_ref, bf2_ref, g_ref, b_ref = ex
        dr2_ref, dr2b_ref, dg_ref, db_ref, dbf2_ref, loss_ref = outs
        first = pl.program_id(0) == 0
        r2 = DEEPNORM_ALPHA * h1_ref[...] + (acc + bf2_ref[...])
        xhat, rstd = _ln_stats(r2)
        err = xhat * g_ref[...] + b_ref[...] - tgt_ref[...]
        dr2, dg, db = _ln_bwd(err * (1.0 / d), xhat, rstd, g_ref[...])
        dr2_ref[...] = dr2
        dr2b_ref[...] = dr2.astype(BF16)
        _acc_rows(first, dg_ref, dg)
        _acc_rows(first, db_ref, db)
        _acc_rows(first, dbf2_ref, jnp.sum(dr2, axis=0, keepdims=True))
        _acc_rows(first, loss_ref, jnp.sum(err * err, axis=0, keepdims=True) * (0.5 / d))

    dr2, dr2b, dg2, db2, dbf2, loss_vec = _matmul(
        "mm_ff2_ln2_loss", act, w2_g, dims=NN, grid=(t // tm_ep, 1, N_DEV),
        a_spec=pl.BlockSpec((tm_ep, ws_f), lambda m, n, k: (m, k)),
        b_spec=pl.BlockSpec((None, ws_f, d), lambda m, n, k: (k, 0, 0)),
        extras=(h1, target, b_ff2, ln2_g, ln2_b), extra_specs=(row_ep, row_ep, vec, vec, vec),
        out_shape=[sds((t, d)), sds((t, d), BF16)] + [sds((1, d))] * 4, out_specs=[row_ep, row_ep, vec, vec, vec, vec],
        acc_shape=(tm_ep, d), epilogue=ep_ln2, sem=seq)

    def ep_dff1(acc, ex, outs):
        df1 = acc * (2.0 * jnp.maximum(ex[0][...], 0.0))
        outs[0][...] = df1.astype(BF16)
        _acc_rows(pl.program_id(1) == 0, outs[1], jnp.sum(df1, axis=0, keepdims=True))

    df_tile = pl.BlockSpec((tm_big, ws_f), lambda n, m, k: (m, n))
    df1b, dbf1 = _matmul(
        "mm_dff1", dr2b, w2_g, dims=NT, grid=(N_DEV, t // tm_big, 1),
        a_spec=pl.BlockSpec((tm_big, d), lambda n, m, k: (m, 0)),
        b_spec=pl.BlockSpec((None, ws_f, d), lambda n, m, k: (n, 0, 0)),
        extras=(f1,), extra_specs=(df_tile,),
        out_shape=[sds((t, f), BF16), sds((1, f))], out_specs=[df_tile, pl.BlockSpec((1, ws_f), lambda n, m, k: (0, n))],
        acc_shape=(tm_big, ws_f), epilogue=ep_dff1, sem=("parallel", "arbitrary", "arbitrary"))

    tn_d = min(d, 1024)
    dw2 = _matmul(
        "mm_dw2", act, dr2b, dims=TN, grid=(N_DEV, d // tn_d, t // tkk),
        a_spec=pl.BlockSpec((tkk, ws_f), lambda m, n, k: (k, m)),
        b_spec=pl.BlockSpec((tkk, tn_d), lambda m, n, k: (k, n)),
        out_shape=[sds((N_DEV, ws_f, d))], out_specs=[pl.BlockSpec((None, ws_f, tn_d), lambda m, n, k: (m, 0, n))],
        acc_shape=(ws_f, tn_d), epilogue=_store_epilogue(F32))[0]

    dw1 = _matmul(
        "mm_dw1", h1b, df1b, dims=TN, grid=(d // tn_d, N_DEV, t // tkk),
        a_spec=pl.BlockSpec((tkk, tn_d), lambda m, n, k: (k, m)),
        b_spec=pl.BlockSpec((tkk, ws_f), lambda m, n, k: (k, n)),
        out_shape=[sds((N_DEV, d, ws_f))], out_specs=[pl.BlockSpec((None, tn_d, ws_f), lambda m, n, k: (n, m, 0))],
        acc_shape=(tn_d, ws_f), epilogue=_store_epilogue(F32))[0]

    def ep_ln1_bwd(acc, ex, outs):
        dr2_ref, r1_ref, g_ref = ex
        first = pl.program_id(0) == 0
        xhat, rstd = _ln_stats(r1_ref[...])
        dr1, dg, db = _ln_bwd(DEEPNORM_ALPHA * dr2_ref[...] + acc, xhat, rstd, g_ref[...])
        outs[0][...] = dr1
        outs[1][...] = dr1.astype(BF16)
        _acc_rows(first, outs[2], dg)
        _acc_rows(first, outs[3], db)

    slots = _owner_slots()

    def to_sibling(parts):
        return (parts + [lax.empty((4, *pt.shape[1:]), F32) for pt in parts], *_plan_rs_sibling(len(parts)))

    def to_owner(names_, parts, from_sib):
        sums = [_rs_chip_sum("rs_chip_sum_" + nm, slots, pt, fs) for nm, pt, fs in zip(names_, parts, from_sib)]
        return (sums + [lax.empty(cs.shape, BF16) for cs in sums], *_plan_rs_owner(len(sums)))

    (dr1, dr1b, dg1, db1), (dw1, dw2, sib_1, sib_2) = _matmul(
        "mm_dh1_ln1_bwd", df1b, w1_g, dims=NT, grid=(t // tm_ep, 1, N_DEV),
        a_spec=pl.BlockSpec((tm_ep, ws_f), lambda m, n, k: (m, k)),
        b_spec=pl.BlockSpec((None, d, ws_f), lambda m, n, k: (k, 0, 0)),
        extras=(dr2, r1, ln1_g), extra_specs=(row_ep, row_ep, vec),
        out_shape=[sds((t, d)), sds((t, d), BF16), sds((1, d)), sds((1, d))], out_specs=[row_ep, row_ep, vec, vec],
        acc_shape=(tm_ep, d), epilogue=ep_ln1_bwd, sem=seq,
        comm=to_sibling([dw1, dw2]))
    own_1 = to_owner(["w_ff1"], [dw1], [sib_1])
    own_2 = to_owner(["w_ff2"], [dw2], [sib_2])[0]

    dwout = _matmul(
        "mm_dwout", mixin, dr1b, dims=TN, grid=(d // tn_d, d // tn_d, t // tkk),
        a_spec=pl.BlockSpec((tkk, tn_d), lambda m, n, k: (k, m)),
        b_spec=pl.BlockSpec((tkk, tn_d), lambda m, n, k: (k, n)),
        out_shape=[sds((d, d))], out_specs=[pl.BlockSpec((tn_d, tn_d), lambda m, n, k: (m, n))],
        acc_shape=(tn_d, tn_d), epilogue=_store_epilogue(F32))[0].reshape(N_DEV, ws_out, d)

    tn_mix = min(tn_d, p, d - p)

    def mm_dmixin(name, first, width, dtype, comm=None):
        return _matmul(
            name, dr1b, wout_2d, dims=NT, grid=(t // tm_big, width // tn_mix, 1),
            a_spec=pl.BlockSpec((tm_big, d), lambda m, n, k: (m, 0)),
            b_spec=pl.BlockSpec((tn_mix, d), lambda m, n, k: (n + first // tn_mix, 0)),
            out_shape=[sds((t, width), dtype)], out_specs=[pl.BlockSpec((tm_big, tn_mix), lambda m, n, k: (m, n))],
            acc_shape=(tm_big, tn_mix), epilogue=_store_epilogue(dtype), comm=comm)

    (dy_pool,), (dwout, sib_out) = mm_dmixin("mm_dmixin_pool", 0, p, F32, comm=to_sibling([dwout]))
    (do,) = mm_dmixin("mm_dmixin_att", p, d - p, BF16)

    (du_pool, dwp, dsc), (_, chips_out) = _pool_bwd(
        dy_pool, ypre, wp_full, sc_full, t, c_pool, comm=to_owner(["w_out"], [dwout], [sib_out]))
    (dq, dk, dv), (_, chips_1) = _attn_bwd(qkv, do, o, t, n_heads, comm=own_1)
    dub = jnp.concatenate([du_pool.astype(BF16), dq, dk, dv], axis=1)

    (dwin,), own_2 = _matmul(
        "mm_dwin", h0b, dub, dims=TN, grid=(d // tn_d, N_DEV, t // tkk),
        a_spec=pl.BlockSpec((tkk, tn_d), lambda m, n, k: (k, m)),
        b_spec=pl.BlockSpec((tkk, ws_in), lambda m, n, k: (k, n)),
        out_shape=[sds((N_DEV, d, ws_in))], out_specs=[pl.BlockSpec((None, tn_d, ws_in), lambda m, n, k: (n, m, 0))],
        acc_shape=(tn_d, ws_in), epilogue=_store_epilogue(F32), comm=(own_2, *_plan_rs_owner(1, (1, 2))))
    dwp_g = dwp.reshape(n_groups, N_DEV, pr, c_pool).transpose(1, 0, 2, 3).reshape(N_DEV, n_groups * pr, c_pool)

    def ep_ln0_bwd(acc, ex, outs):
        dr1_ref, x_ref, g_ref = ex
        first = pl.program_id(0) == 0
        xhat, rstd = _ln_stats(x_ref[...])
        dx, dg, db = _ln_bwd(DEEPNORM_ALPHA * dr1_ref[...] + acc, xhat, rstd, g_ref[...])
        outs[0][...] = dx
        _acc_rows(first, outs[1], dg)
        _acc_rows(first, outs[2], db)

    sib_in = to_sibling([dwin, dwp_g])
    to_diagonal = _plan_rs_owner(1, (3,))
    last_host = (own_2 + sib_in[0], *_join_plans((to_diagonal[0], 2, to_diagonal[1]), (sib_in[1], 4, sib_in[2])))
    def two_blocks(a_ref, b_ref):
        return _dot(a_ref[:, :ws_in], b_ref[0], NT) + _dot(a_ref[:, ws_in:], b_ref[1], NT)

    (dx, dg0, db0), (_, chips_2, dwin, dwp_g, sib_in_, sib_p) = _matmul(
        "mm_dh0_ln0_bwd", dub, win_g, dims=NT, grid=(t // tm_ep, 1, N_DEV // 2),
        a_spec=pl.BlockSpec((tm_ep, 2 * ws_in), lambda m, n, k: (m, k)),
        b_spec=pl.BlockSpec((2, d, ws_in), lambda m, n, k: (k, 0, 0)),
        extras=(dr1, x2, g0), extra_specs=(row_ep, row_ep, vec),
        out_shape=[sds((t, d)), sds((1, d)), sds((1, d))], out_specs=[row_ep, vec, vec],
        acc_shape=(tm_ep, d), epilogue=ep_ln0_bwd, sem=seq,
        comm=last_host, dot_fn=two_blocks)

    _, _, chips_in, chips_p = _copies_now("rs_owner_w_in", *to_owner(["w_in", "w_pool"], [dwin, dwp_g], [sib_in_, sib_p]))
    w_of = {"w_in": shards[0], "w_out": shards[1], "w_ff1": shards[2], "w_ff2": shards[3], "w_pool": shards[4]}
    mv_of = {"w_in": (m_w_in, v_w_in), "w_out": (m_w_out, v_w_out), "w_ff1": (m_w_ff1, v_w_ff1),
             "w_ff2": (m_w_ff2, v_w_ff2), "w_pool": (m_w_pool, v_w_pool)}
    big = {}
    for nm, pt, fs, fc in [("w_ff1", dw1, sib_1, chips_1), ("w_ff2", dw2, sib_2, chips_2), ("w_out", dwout, sib_out, chips_out),
                           ("w_in", dwin, sib_in_, chips_in), ("w_pool", dwp_g, sib_p, chips_p)]:
        w2d = w_of[nm]
        m_, v_ = mv_of[nm]
        big[nm] = _rs_final_adamw("rs_final_adamw_" + nm, slots, pt, fs, fc, w2d, m_.reshape(w2d.shape), v_.reshape(w2d.shape))

    n_f_rows = f // d
    pad_sc = d - p
    packet = jnp.concatenate(
        [loss_vec, dg0, db0, dg1, db1, dbf2, dg2, db2, dbf1.reshape(n_f_rows, d),
         jnp.pad(dsc.reshape(1, p), ((0, 0), (0, pad_sc)))], axis=0)
    n_rows = packet.shape[0]
    n_pad = (-n_rows) % 8
    packet = jnp.pad(packet, ((0, n_pad), (0, 0)))
    sums, loss11 = _small_all_reduce(packet)
    dsc_full = sums[8 + n_f_rows, :p].reshape(n_groups, N_DEV, pr)
    dsc_mine = lax.dynamic_index_in_dim(dsc_full, me, axis=1, keepdims=False)

    def sc_row(a):
        return jnp.pad(a.reshape(1, n_groups * pr), ((0, 0), (0, d - n_groups * pr)))

    def small_pack(ln0g, ln0b, l1g, l1b, bf2, l2g, l2b, bf1, sc):
        rows = [jnp.zeros((1, d), F32), ln0g.reshape(1, d), ln0b.reshape(1, d), l1g, l1b, bf2, l2g, l2b,
                bf1.reshape(n_f_rows, d), sc_row(sc), jnp.zeros((n_pad, d), F32)]
        return jnp.concatenate(rows, axis=0)

    w_small = small_pack(ln_in_g, ln_in_b, ln1_g, ln1_b, b_ff2, ln2_g, ln2_b, b_ff1, pool_scale)
    m_small = small_pack(m_ln_in_g, m_ln_in_b, m_ln1_g, m_ln1_b, m_b_ff2, m_ln2_g, m_ln2_b, m_b_ff1, m_pool_scale)
    v_small = small_pack(v_ln_in_g, v_ln_in_b, v_ln1_g, v_ln1_b, v_b_ff2, v_ln2_g, v_ln2_b, v_b_ff1, v_pool_scale)
    g_small = jnp.concatenate([sums[:8 + n_f_rows], sc_row(dsc_mine), jnp.zeros((n_pad, d), F32)], axis=0)
    small = (g_small,) + tuple(_small_adamw(w_small, g_small, m_small, v_small))

    def unpack(a):
        sc = a[8 + n_f_rows, :n_groups * pr].reshape(1, n_groups, pr)
        return {"ln_in_g": a[1], "ln_in_b": a[2], "ln1_g": a[3:4], "ln1_b": a[4:5], "b_ff2": a[5:6], "ln2_g": a[6:7],
                "ln2_b": a[7:8], "b_ff1": a[8:8 + n_f_rows].reshape(1, f), "pool_scale": sc}

    shapes = {"w_in": w_in.shape, "w_out": w_out.shape, "w_ff1": w_ff1.shape, "w_ff2": w_ff2.shape, "w_pool": w_pool.shape}
    order = ["ln_in_g", "ln_in_b", "w_in", "w_pool", "pool_scale", "w_out", "ln1_g", "ln1_b", "w_ff1", "b_ff1", "w_ff2",
             "b_ff2", "ln2_g", "ln2_b"]
    outs = []
    for kind in range(4):
        small_k = unpack(small[kind])
        for nm in order:
            outs.append(big[nm][kind].reshape(shapes[nm]) if nm in big else small_k[nm])
    return (loss11.reshape(()), dx.reshape(x.shape), *outs)
```

```python
import functools
import math

import jax
import jax.numpy as jnp
from jax import lax
from jax.experimental import pallas as pl
from jax.experimental.pallas import tpu as pltpu

F32 = jnp.float32
BF16 = jnp.bfloat16
MESH = pl.DeviceIdType.MESH

N_DEV = 8
HEAD_DIM = 128
POOL_WINDOWS = (2, 4, 8, 16)
DEEPNORM_ALPHA = (2.0 * 1) ** 0.25
LN_EPS = 1e-5
ADAM_LR = 0.001
ADAM_B1 = 0.9
ADAM_B2 = 0.999
ADAM_EPS = 1e-08
ADAM_WD = 0.01
ADAM_STEP = 10

V7X_VMEM_LIMIT = 56 * 1024 * 1024
ATT_BLOCK = 256
POOL_CHUNK = 256

NN = (((1,), (0,)), ((), ()))
NT = (((1,), (1,)), ((), ()))
TN = (((0,), (0,)), ((), ()))


def _dot(a, b, dims=NN):
    return lax.dot_general(a, b, dims, preferred_element_type=F32)


def _cparams(sem=None):
    return pltpu.CompilerParams(dimension_semantics=sem, vmem_limit_bytes=V7X_VMEM_LIMIT)


def _ln_stats(r):
    mu = jnp.mean(r, axis=-1, keepdims=True)
    xc = r - mu
    var = jnp.mean(xc * xc, axis=-1, keepdims=True)
    rstd = lax.rsqrt(var + LN_EPS)
    return xc * rstd, rstd


def _ln_bwd(dy, xhat, rstd, g):
    dxh = dy * g
    m1 = jnp.mean(dxh, axis=-1, keepdims=True)
    m2 = jnp.mean(dxh * xhat, axis=-1, keepdims=True)
    dx = rstd * (dxh - m1 - xhat * m2)
    dg = jnp.sum(dy * xhat, axis=0, keepdims=True)
    db = jnp.sum(dy, axis=0, keepdims=True)
    return dx, dg, db


def _acc_rows(first, ref, val):
    @pl.when(first)
    def _():
        ref[...] = val

    @pl.when(jnp.logical_not(first))
    def _():
        ref[...] += val


def _call(body, *, name, grid, in_specs, out_specs, out_shape, inputs, scratch_shapes=(), sem=None, comm=None):
    in_specs, out_specs, out_shape, inputs = list(in_specs), list(out_specs), list(out_shape), list(inputs)
    if comm is None:
        outs = pl.pallas_call(
            body, name=name, grid=grid, in_specs=in_specs, out_specs=out_specs, out_shape=out_shape,
            scratch_shapes=list(scratch_shapes), compiler_params=_cparams(sem))(*inputs)
        return list(outs), []
    arrays, plan, n_copies = comm
    n_in, n_out, nc, n_scr = len(inputs), len(out_shape), len(arrays), len(scratch_shapes)

    def hosted(*refs):
        ins = refs[:n_in]
        outs = refs[n_in + nc:n_in + nc + n_out]
        passed = refs[n_in + nc + n_out:n_in + 2 * nc + n_out]
        scratch = refs[n_in + 2 * nc + n_out:n_in + 2 * nc + n_out + n_scr]
        send_sems, recv_sems = refs[-2], refs[-1]
        ids = [pl.program_id(ax) for ax in range(len(grid))]
        first = functools.reduce(jnp.logical_and, [i_ == 0 for i_ in ids])
        last = functools.reduce(jnp.logical_and, [i_ == g - 1 for i_, g in zip(ids, grid)])

        @pl.when(first)
        def _():
            for cp in _plan_copies(plan, passed, send_sems, recv_sems):
                cp.start()

        body(*ins, *outs, *scratch)

        @pl.when(last)
        def _():
            for cp in _plan_copies(plan, passed, send_sems, recv_sems):
                cp.wait_send()
                cp.wait_recv()

    any_spec = pl.BlockSpec(memory_space=pl.ANY)
    outs = pl.pallas_call(
        hosted, name=name, grid=grid,
        in_specs=in_specs + [any_spec] * nc, out_specs=out_specs + [any_spec] * nc,
        out_shape=out_shape + [jax.ShapeDtypeStruct(a.shape, a.dtype) for a in arrays],
        scratch_shapes=list(scratch_shapes) + [pltpu.SemaphoreType.DMA((n_copies,)), pltpu.SemaphoreType.DMA((n_copies,))],
        input_output_aliases={n_in + i: n_out + i for i in range(nc)},
        compiler_params=pltpu.CompilerParams(dimension_semantics=("arbitrary",) * len(grid),
                                             vmem_limit_bytes=V7X_VMEM_LIMIT, has_side_effects=True),
    )(*inputs, *arrays)
    return list(outs[:n_out]), list(outs[n_out:])


def _matmul(name, a, b, *, dims, grid, a_spec, b_spec, extras=(), extra_specs=(), out_shape, out_specs,
            acc_shape, epilogue, k_axis=2, sem=("parallel", "parallel", "arbitrary"), comm=None, dot_fn=None):
    nk = grid[k_axis]
    n_extra = len(extras)
    n_out = len(out_shape)
    if dot_fn is None:
        def dot_fn(a_ref, b_ref):
            return _dot(a_ref[...], b_ref[...], dims)

    def body(a_ref, b_ref, *rest):
        extra_refs = rest[:n_extra]
        out_refs = rest[n_extra:n_extra + n_out]
        if nk == 1:
            epilogue(dot_fn(a_ref, b_ref), extra_refs, out_refs)
            return
        acc_ref = rest[n_extra + n_out]
        k = pl.program_id(k_axis)

        @pl.when(k == 0)
        def _():
            acc_ref[...] = jnp.zeros(acc_shape, F32)

        acc_ref[...] += dot_fn(a_ref, b_ref)

        @pl.when(k == nk - 1)
        def _():
            epilogue(acc_ref[...], extra_refs, out_refs)

    outs, passed = _call(
        body, name=name, grid=grid, in_specs=[a_spec, b_spec, *extra_specs], out_specs=out_specs, out_shape=out_shape,
        inputs=[a, b, *extras], scratch_shapes=[] if nk == 1 else [pltpu.VMEM(acc_shape, F32)], sem=sem, comm=comm)
    return outs if comm is None else (outs, passed)


def _store_epilogue(dtype):

    def ep(acc, extra_refs, out_refs):
        out_refs[0][...] = acc.astype(dtype)
    return ep


def _ln_in_fwd(x, g, b, tm, comm=None):
    t, d = x.shape

    def body(x_ref, g_ref, b_ref, h_ref, hb_ref):
        xhat, _ = _ln_stats(x_ref[...])
        h = xhat * g_ref[...] + b_ref[...]
        h_ref[...] = h
        hb_ref[...] = h.astype(BF16)

    row = pl.BlockSpec((tm, d), lambda i: (i, 0))
    vec = pl.BlockSpec((1, d), lambda i: (0, 0))
    return _call(
        body, name="ln_in_fwd", grid=(t // tm,), in_specs=[row, vec, vec], out_specs=[row, row],
        out_shape=[jax.ShapeDtypeStruct((t, d), F32), jax.ShapeDtypeStruct((t, d), BF16)],
        inputs=[x, g, b], sem=("parallel",), comm=comm)


def _split3(x):
    hi = x.astype(BF16)
    r = x - hi.astype(F32)
    mid = r.astype(BF16)
    lo = (r - mid.astype(F32)).astype(BF16)
    return hi, mid, lo


def _split2(x):
    hi = x.astype(BF16)
    lo = (x - hi.astype(F32)).astype(BF16)
    return hi, lo


def _pool_fwd(u, wp, sc, t, c, comm=None):
    n_groups = len(POOL_WINDOWS)
    tc = POOL_CHUNK
    n_chunks = t // tc

    def body(u_ref, wp_ref, sc_ref, y_ref, ypre_ref, xp_ref):
        g = pl.program_id(0)
        xp_ref[pl.ds(0, tc), :] = jnp.zeros((tc, c), F32)
        xp_ref[pl.ds(tc, t), :] = u_ref[...]
        out_i = lax.broadcasted_iota(jnp.int32, (tc, 2 * tc), 0)
        in_j = lax.broadcasted_iota(jnp.int32, (tc, 2 * tc), 1)
        lag = tc + out_i - in_j
        t_in_chunk = lax.broadcasted_iota(jnp.int32, (tc, 1), 0)
        for gi, w in enumerate(POOL_WINDOWS):
            @pl.when(g == gi)
            def _(w=w):
                band = jnp.logical_and(lag >= 0, lag < w).astype(BF16)

                def chunk(ci, carry):
                    start = pl.multiple_of(ci * tc, tc)
                    win = xp_ref[pl.ds(start, 2 * tc), :]
                    hi, mid, lo = _split3(win)
                    wsum = _dot(band, hi) + _dot(band, mid) + _dot(band, lo)
                    cnt = jnp.minimum(ci * tc + t_in_chunk + 1, w).astype(F32)
                    ypre = wsum * (1.0 / cnt) - win[tc:, :]
                    ypre_b = ypre.astype(BF16)
                    y = _dot(ypre_b, wp_ref[...]) * sc_ref[...]
                    ypre_ref[pl.ds(start, tc), :] = ypre_b
                    y_ref[pl.ds(start, tc), :] = y.astype(BF16)
                    return carry

                lax.fori_loop(0, n_chunks, chunk, 0)

    col = pl.BlockSpec((t, c), lambda g: (0, g))
    return _call(
        body, name="pool_fwd", grid=(n_groups,),
        in_specs=[col, pl.BlockSpec((None, c, c), lambda g: (g, 0, 0)), pl.BlockSpec((None, 1, c), lambda g: (g, 0, 0))],
        out_specs=[col, col],
        out_shape=[jax.ShapeDtypeStruct((t, n_groups * c), BF16), jax.ShapeDtypeStruct((t, n_groups * c), BF16)],
        inputs=[u, wp, sc], scratch_shapes=[pltpu.VMEM((t + tc, c), F32)], sem=("parallel",), comm=comm)


def _pool_bwd(dmixin, ypre, wp, sc, t, c, comm=None):
    n_groups = len(POOL_WINDOWS)
    tc = POOL_CHUNK
    n_chunks = t // tc

    def body(dy_ref, ypre_ref, wp_ref, sc_ref, du_ref, dwp_ref, dsc_ref, zp_ref):
        g = pl.program_id(0)
        zp_ref[pl.ds(t, tc), :] = jnp.zeros((tc, c), F32)
        dwp_ref[...] = jnp.zeros((c, c), F32)
        dsc_ref[...] = jnp.zeros((1, c), F32)
        out_i = lax.broadcasted_iota(jnp.int32, (tc, 2 * tc), 0)
        in_j = lax.broadcasted_iota(jnp.int32, (tc, 2 * tc), 1)
        lead = in_j - out_i
        t_in_chunk = lax.broadcasted_iota(jnp.int32, (tc, 1), 0)
        for gi, w in enumerate(POOL_WINDOWS):
            @pl.when(g == gi)
            def _(w=w):
                band = jnp.logical_and(lead >= 0, lead < w).astype(BF16)

                def first(ci, carry):
                    start = pl.multiple_of(ci * tc, tc)
                    dy = dy_ref[pl.ds(start, tc), :]
                    yp = ypre_ref[pl.ds(start, tc), :]
                    ymm = _dot(yp, wp_ref[...])
                    dsc_ref[...] += jnp.sum(dy * ymm, axis=0, keepdims=True)
                    dys_b = (dy * sc_ref[...]).astype(BF16)
                    dwp_ref[...] += _dot(yp, dys_b, TN)
                    dyp = _dot(dys_b, wp_ref[...], NT)
                    cnt = jnp.minimum(ci * tc + t_in_chunk + 1, w).astype(F32)
                    zp_ref[pl.ds(start, tc), :] = dyp * (1.0 / cnt)
                    du_ref[pl.ds(start, tc), :] = -dyp
                    return carry

                lax.fori_loop(0, n_chunks, first, 0)

                def second(ci, carry):
                    start = pl.multiple_of(ci * tc, tc)
                    hi, mid, lo = _split3(zp_ref[pl.ds(start, 2 * tc), :])
                    du_ref[pl.ds(start, tc), :] += _dot(band, hi) + _dot(band, mid) + _dot(band, lo)
                    return carry

                lax.fori_loop(0, n_chunks, second, 0)

    col = pl.BlockSpec((t, c), lambda g: (0, g))
    return _call(
        body, name="pool_bwd", grid=(n_groups,),
        in_specs=[col, col, pl.BlockSpec((None, c, c), lambda g: (g, 0, 0)), pl.BlockSpec((None, 1, c), lambda g: (g, 0, 0))],
        out_specs=[col, pl.BlockSpec((None, c, c), lambda g: (g, 0, 0)), pl.BlockSpec((None, 1, c), lambda g: (g, 0, 0))],
        out_shape=[jax.ShapeDtypeStruct((t, n_groups * c), F32), jax.ShapeDtypeStruct((n_groups, c, c), F32),
                   jax.ShapeDtypeStruct((n_groups, 1, c), F32)],
        inputs=[dmixin, ypre, wp, sc], scratch_shapes=[pltpu.VMEM((t + tc, c), F32)], sem=("parallel",), comm=comm)


ROW_PARTS = 2


def _att_consts():
    b = ATT_BLOCK
    rp = b // ROW_PARTS
    row = lax.broadcasted_iota(jnp.int32, (b, b), 0)
    col = lax.broadcasted_iota(jnp.int32, (b, b), 1)
    tri = (row >= col).astype(BF16)
    prow = lax.broadcasted_iota(jnp.int32, (rp, b), 0)
    pcol = lax.broadcasted_iota(jnp.int32, (rp, b), 1)
    causal = [pcol < prow + r * rp for r in range(ROW_PARTS)]
    return tri, causal


def _suffix_sum(x, tri):
    hi, lo = _split2(x)
    return _dot(hi, tri) + _dot(lo, tri)


LOG2_E = 1.4426950408889634


def _att_scores(qb, kb, mask):
    z2 = _dot(qb, kb, NT) * (LOG2_E / math.sqrt(HEAD_DIM))
    sp2 = jnp.maximum(z2, 0.0) + jnp.log2(1.0 + jnp.exp2(-jnp.abs(z2)))
    return z2, sp2, (sp2 if mask is None else jnp.where(mask, sp2, 0.0))


HEADS_PER_STEP = 2
ATT_LANES = HEADS_PER_STEP * HEAD_DIM


def _head_lanes(s):
    return slice(s * HEAD_DIM, (s + 1) * HEAD_DIM)


UNDERFLOW_LOG2 = 160.0


def _sweep_earlier_blocks(i, state, per_chain, block):
    def lowest(st):
        low = st[0]
        for k in range(per_chain, len(st), per_chain):
            low = jnp.minimum(low, st[k])
        return jnp.min(low)

    def more(c):
        return jnp.logical_and(c[0] < i, c[1] < UNDERFLOW_LOG2)

    def trip(c):
        st = block(i - 1 - c[0], c[2:])
        return (c[0] + 1, lowest(st)) + tuple(st)

    return lax.while_loop(more, trip, (jnp.int32(0), lowest(state)) + tuple(state))[2:]


def _attn_fwd(qkv, t, n_heads, comm=None):
    b = ATT_BLOCK
    nq = t // b
    n_steps = n_heads // HEADS_PER_STEP

    rp = b // ROW_PARTS
    chains = [(s, r) for s in range(HEADS_PER_STEP) for r in range(ROW_PARTS)]
    no_mask = [None] * ROW_PARTS

    def body(q_ref, k_ref, v_ref, o_ref):
        tri, causal = _att_consts()

        def blocks(qbs, j, state, masks):
            ks = pl.multiple_of(j * b, b)
            scores = [_att_scores(qbs[ci], k_ref[pl.ds(ks, b), _head_lanes(s)], masks[r]) for ci, (s, r) in enumerate(chains)]
            incls = [_suffix_sum(sc[2], tri) for sc in scores]
            out = []
            for ci, (s, r) in enumerate(chains):
                carry, acc = state[2 * ci], state[2 * ci + 1]
                a = jnp.exp2(scores[ci][0] - (incls[ci] + carry))
                if masks[r] is not None:
                    a = jnp.where(masks[r], a, 0.0)
                out += [carry + incls[ci][:, 0:1], acc + _dot(a.astype(BF16), v_ref[pl.ds(ks, b), _head_lanes(s)])]
            return tuple(out)

        def q_loop(i, _):
            qs = pl.multiple_of(i * b, b)
            qbs = [q_ref[pl.ds(qs + r * rp, rp), _head_lanes(s)] for s, r in chains]
            zero = (jnp.zeros((rp, 1), F32), jnp.zeros((rp, HEAD_DIM), F32)) * len(chains)
            state = blocks(qbs, i, zero, causal)
            state = _sweep_earlier_blocks(i, state, 2, lambda j, st: blocks(qbs, j, st, no_mask))
            for ci, (s, r) in enumerate(chains):
                o_ref[pl.ds(qs + r * rp, rp), _head_lanes(s)] = state[2 * ci + 1]
            return 0

        lax.fori_loop(0, nq, q_loop, 0)

    def heads(off):
        return pl.BlockSpec((t, ATT_LANES), lambda h: (0, off + h))

    return _call(
        body, name="attn_fwd", grid=(n_steps,),
        in_specs=[heads(0), heads(n_steps), heads(2 * n_steps)], out_specs=[heads(0)],
        out_shape=[jax.ShapeDtypeStruct((t, n_heads * HEAD_DIM), F32)],
        inputs=[qkv, qkv, qkv], sem=("parallel",), comm=comm)


def _attn_bwd(qkv, do, o, t, n_heads, comm=None):
    b = ATT_BLOCK
    nq = t // b
    n_steps = n_heads // HEADS_PER_STEP
    scale = 1.0 / math.sqrt(HEAD_DIM)
    rp = b // ROW_PARTS
    chains = [(s, r) for s in range(HEADS_PER_STEP) for r in range(ROW_PARTS)]
    no_mask = [None] * ROW_PARTS

    def body(q_ref, k_ref, v_ref, do_ref, o_ref, dq_ref, dk_ref, dv_ref, qt_ref, dot_ref, dkt_ref, dvt_ref):
        for j in range(nq):
            rows = pl.ds(j * b, b)
            qt_ref[j] = q_ref[rows, :].astype(F32).T.astype(BF16)
            dot_ref[j] = do_ref[rows, :].astype(F32).T.astype(BF16)
        dkt_ref[...] = jnp.zeros((nq, ATT_LANES, b), F32)
        dvt_ref[...] = jnp.zeros((nq, ATT_LANES, b), F32)
        tri, causal = _att_consts()

        def blocks(i, fixed, j, state, masks):
            ks = pl.multiple_of(j * b, b)
            n = len(chains)
            kbs = [k_ref[pl.ds(ks, b), _head_lanes(s)] for s, _ in chains]
            scores = [_att_scores(fixed[ci][0], kbs[ci], masks[r]) for ci, (s, r) in enumerate(chains)]
            incls = [_suffix_sum(sc[2], tri) for sc in scores]
            das = [_dot(fixed[ci][1], v_ref[pl.ds(ks, b), _head_lanes(s)], NT) for ci, (s, r) in enumerate(chains)]
            a_bs, gs = [], []
            for ci, (s, r) in enumerate(chains):
                a = jnp.exp2(scores[ci][0] - (incls[ci] + state[3 * ci]))
                if masks[r] is not None:
                    a = jnp.where(masks[r], a, 0.0)
                a_bs.append(a.astype(BF16))
                gs.append(a_bs[ci].astype(F32) * das[ci])
            g_incls = [_suffix_sum(g, tri) for g in gs]
            dz_bs = []
            for ci, (s, r) in enumerate(chains):
                rest = (fixed[ci][2] - state[3 * ci + 1]) - (g_incls[ci] - gs[ci])
                sig = jnp.exp2(scores[ci][0] - scores[ci][1])
                dz = (gs[ci] - sig * rest) * scale
                if masks[r] is not None:
                    dz = jnp.where(masks[r], dz, 0.0)
                dz_bs.append(dz.astype(BF16))
            out = []
            for ci in range(n):
                out += [state[3 * ci] + incls[ci][:, 0:1], state[3 * ci + 1] + g_incls[ci][:, 0:1],
                        state[3 * ci + 2] + _dot(dz_bs[ci], kbs[ci])]
            for s in range(HEADS_PER_STEP):
                lanes = _head_lanes(s)
                dk_add, dv_add = None, None
                for ci, (cs, r) in enumerate(chains):
                    if cs == s:
                        part = slice(r * rp, (r + 1) * rp)
                        dk_c = _dot(qt_ref[i, lanes, part], dz_bs[ci])
                        dv_c = _dot(dot_ref[i, lanes, part], a_bs[ci])
                        dk_add = dk_c if dk_add is None else dk_add + dk_c
                        dv_add = dv_c if dv_add is None else dv_add + dv_c
                dkt_ref[j, lanes, :] += dk_add
                dvt_ref[j, lanes, :] += dv_add
            return tuple(out)

        def q_loop(i, _):
            qs = pl.multiple_of(i * b, b)
            fixed = []
            for s, r in chains:
                rows = pl.ds(qs + r * rp, rp)
                dob = do_ref[rows, _head_lanes(s)]
                total = jnp.sum(dob.astype(F32) * o_ref[rows, _head_lanes(s)], axis=-1, keepdims=True)
                fixed.append((q_ref[rows, _head_lanes(s)], dob, total))
            zero = (jnp.zeros((rp, 1), F32), jnp.zeros((rp, 1), F32), jnp.zeros((rp, HEAD_DIM), F32)) * len(chains)
            state = blocks(i, fixed, i, zero, causal)
            state = _sweep_earlier_blocks(i, state, 3, lambda j, st: blocks(i, fixed, j, st, no_mask))
            for ci, (s, r) in enumerate(chains):
                dq_ref[pl.ds(qs + r * rp, rp), _head_lanes(s)] = state[3 * ci + 2].astype(BF16)
            return 0

        lax.fori_loop(0, nq, q_loop, 0)
        for j in range(nq):
            rows = pl.ds(j * b, b)
            dk_ref[rows, :] = dkt_ref[j].T.astype(BF16)
            dv_ref[rows, :] = dvt_ref[j].T.astype(BF16)

    def heads(off):
        return pl.BlockSpec((t, ATT_LANES), lambda h: (0, off + h))

    shape = jax.ShapeDtypeStruct((t, n_heads * HEAD_DIM), BF16)
    return _call(
        body, name="attn_bwd", grid=(n_steps,),
        in_specs=[heads(0), heads(n_steps), heads(2 * n_steps), heads(0), heads(0)],
        out_specs=[heads(0)] * 3, out_shape=[shape] * 3, inputs=[qkv, qkv, qkv, do, o],
        scratch_shapes=[pltpu.VMEM((nq, ATT_LANES, b), BF16)] * 2 + [pltpu.VMEM((nq, ATT_LANES, b), F32)] * 2,
        sem=("parallel",), comm=comm)


def _place():
    x, y, c = lax.axis_index("x"), lax.axis_index("y"), lax.axis_index("c")
    return x, y, c


def _flip(v, on):
    return 1 - v if on else v


def _plan_copies(plan, refs, send_sems, recv_sems):
    return [pltpu.make_async_remote_copy(src_ref=src, dst_ref=dst, send_sem=send_sems.at[k], recv_sem=recv_sems.at[k],
                                         device_id=dev, device_id_type=MESH)
            for k, (src, dst, dev) in enumerate(plan(refs))]


def _copies_now(name, arrays, plan, n_copies):
    n = len(arrays)

    def body(*refs):
        copies = _plan_copies(plan, refs[n:2 * n], refs[2 * n], refs[2 * n + 1])
        for cp in copies:
            cp.start()
        for cp in copies:
            cp.wait_send()
            cp.wait_recv()

    any_spec = pl.BlockSpec(memory_space=pl.ANY)
    return list(pl.pallas_call(
        body, name=name, in_specs=[any_spec] * n, out_specs=[any_spec] * n,
        out_shape=[jax.ShapeDtypeStruct(a.shape, a.dtype) for a in arrays],
        input_output_aliases={i: i for i in range(n)},
        scratch_shapes=[pltpu.SemaphoreType.DMA((n_copies,)), pltpu.SemaphoreType.DMA((n_copies,))],
        compiler_params=pltpu.CompilerParams(has_side_effects=True),
    )(*arrays))


SIBLING, ACROSS_Y, ACROSS_X, DIAGONAL = 1, 2, 4, 6


def _plan_gather_own(peers, rows=None):
    def plan(refs):
        x, y, c = _place()
        mine = refs[0].at[4 * x + 2 * y + c]
        if rows is not None:
            mine = mine.at[pl.ds(*rows)]
        return [(mine, mine, (_flip(x, k & 4), _flip(y, k & 2), _flip(c, k & 1))) for k in peers]
    return plan, len(peers)


def _plan_gather_forward(n):
    def plan(refs):
        x, y, c = _place()
        out = []
        for ti in range(n):
            for r in range(1, 4):
                blk = refs[ti].at[4 * _flip(x, r & 2) + 2 * _flip(y, r & 1) + c]
                out.append((blk, blk, (x, y, 1 - c)))
        return out
    return plan, 3 * n


def _join_plans(*parts):
    def plan(refs):
        out, at = [], 0
        for part, n_arrays, _ in parts:
            out += part(refs[at:at + n_arrays])
            at += n_arrays
        return out
    return plan, sum(n_cp for _, _, n_cp in parts)


def _plan_rs_sibling(n):
    def plan(refs):
        x, y, c = _place()
        out = []
        for ti in range(n):
            for r in range(4):
                src = refs[ti].at[4 * _flip(x, r & 2) + 2 * _flip(y, r & 1) + (1 - c)]
                out.append((src, refs[n + ti].at[r], (x, y, 1 - c)))
        return out
    return plan, 4 * n


def _plan_rs_owner(n, relations=(1, 2, 3)):
    def plan(refs):
        x, y, c = _place()
        out = []
        for ti in range(n):
            for r in relations:
                out.append((refs[ti].at[r], refs[n + ti].at[r], (_flip(x, r & 2), _flip(y, r & 1), c)))
        return out
    return plan, len(relations) * n


def _owner_slots():
    x, y, c = _place()
    idx = []
    for r in range(4):
        ox, oy = (1 - x if r & 2 else x), (1 - y if r & 1 else y)
        idx.append(4 * ox + 2 * oy + c)
    return jnp.stack(idx).astype(jnp.int32)


def _row_tile(rows, cols):
    tr = max(8, min(rows, (1 << 19) // cols))
    while rows % tr:
        tr //= 2
    return tr


def _rs_chip_sum(name, slots, partial, from_sibling):
    _, rows, cols = partial.shape
    tr = _row_tile(rows, cols)

    def body(slots_ref, p_ref, s_ref, o_ref):
        o_ref[...] = (p_ref[...] + s_ref[...]).astype(BF16)

    grid_spec = pltpu.PrefetchScalarGridSpec(
        num_scalar_prefetch=1, grid=(3, rows // tr),
        in_specs=[pl.BlockSpec((None, tr, cols), lambda r, i, s: (s[r + 1], i, 0)),
                  pl.BlockSpec((None, tr, cols), lambda r, i, s: (r + 1, i, 0))],
        out_specs=pl.BlockSpec((None, tr, cols), lambda r, i, s: (r + 1, i, 0)))
    return pl.pallas_call(
        body, name=name, grid_spec=grid_spec, out_shape=jax.ShapeDtypeStruct((4, rows, cols), BF16),
        compiler_params=_cparams(("parallel", "parallel")),
    )(slots, partial, from_sibling)


def _adamw(w, g, m, v):
    m = ADAM_B1 * m + (1.0 - ADAM_B1) * g
    v = ADAM_B2 * v + (1.0 - ADAM_B2) * (g * g)
    m_hat = m / (1.0 - ADAM_B1 ** ADAM_STEP)
    v_hat = v / (1.0 - ADAM_B2 ** ADAM_STEP)
    delta = -ADAM_LR * (m_hat / (jnp.sqrt(v_hat) + ADAM_EPS) + ADAM_WD * w)
    return delta, m, v


def _rs_final_adamw(name, slots, partial, from_sibling, from_chips, w, m, v):
    rows, cols = w.shape
    tr = _row_tile(rows, cols)

    def body(slots_ref, p_ref, s_ref, c1_ref, c2_ref, c3_ref, w_ref, m_ref, v_ref, g_ref, d_ref, nm_ref, nv_ref):
        g = p_ref[...] + s_ref[...]
        g = g + c1_ref[...].astype(F32)
        g = g + c2_ref[...].astype(F32)
        g = g + c3_ref[...].astype(F32)
        delta, nm, nv = _adamw(w_ref[...], g, m_ref[...], v_ref[...])
        g_ref[...] = g
        d_ref[...] = delta
        nm_ref[...] = nm
        nv_ref[...] = nv

    def slot(r):
        return pl.BlockSpec((None, tr, cols), lambda i, s: (r, i, 0))

    flat = pl.BlockSpec((tr, cols), lambda i, s: (i, 0))
    grid_spec = pltpu.PrefetchScalarGridSpec(
        num_scalar_prefetch=1, grid=(rows // tr,),
        in_specs=[pl.BlockSpec((None, tr, cols), lambda i, s: (s[0], i, 0)), slot(0), slot(1), slot(2), slot(3), flat, flat, flat],
        out_specs=[flat] * 4)
    return pl.pallas_call(
        body, name=name, grid_spec=grid_spec, out_shape=[jax.ShapeDtypeStruct((rows, cols), F32)] * 4,
        compiler_params=_cparams(("parallel",)),
    )(slots, partial, from_sibling, from_chips, from_chips, from_chips, w, m, v)


def _small_all_reduce(packet):
    rows, d = packet.shape

    def body(p_ref, sum_ref, loss_ref, all_ref, send_sems, recv_sems):
        x, y, c = _place()
        me = 4 * x + 2 * y + c
        all_ref[me] = p_ref[...]
        copies = []
        for k in range(1, N_DEV):
            px, py, pc = (1 - x if k & 4 else x), (1 - y if k & 2 else y), (1 - c if k & 1 else c)
            cp = pltpu.make_async_remote_copy(
                src_ref=p_ref, dst_ref=all_ref.at[me], send_sem=send_sems.at[k], recv_sem=recv_sems.at[k],
                device_id=(px, py, pc), device_id_type=MESH)
            cp.start()
            copies.append(cp)
        for cp in copies:
            cp.wait_recv()
        for cp in copies:
            cp.wait_send()
        total = all_ref[0]
        for j in range(1, N_DEV):
            total = total + all_ref[j]
        sum_ref[...] = total
        loss_ref[...] = jnp.sum(total[0:1, :], axis=-1, keepdims=True)

    vmem = pl.BlockSpec(memory_space=pltpu.VMEM)
    return pl.pallas_call(
        body, name="small_all_reduce",
        in_specs=[vmem], out_specs=[vmem, vmem],
        out_shape=[jax.ShapeDtypeStruct((rows, d), F32), jax.ShapeDtypeStruct((1, 1), F32)],
        scratch_shapes=[pltpu.VMEM((N_DEV, rows, d), F32), pltpu.SemaphoreType.DMA((N_DEV,)), pltpu.SemaphoreType.DMA((N_DEV,))],
        compiler_params=pltpu.CompilerParams(has_side_effects=True),
    )(packet)


def _small_adamw(w, g, m, v):
    def body(w_ref, g_ref, m_ref, v_ref, d_ref, nm_ref, nv_ref):
        delta, nm, nv = _adamw(w_ref[...], g_ref[...], m_ref[...], v_ref[...])
        d_ref[...] = delta
        nm_ref[...] = nm
        nv_ref[...] = nv

    vmem = pl.BlockSpec(memory_space=pltpu.VMEM)
    return pl.pallas_call(
        body, name="small_adamw", in_specs=[vmem] * 4, out_specs=[vmem] * 3,
        out_shape=[jax.ShapeDtypeStruct(w.shape, F32)] * 3,
    )(w, g, m, v)


def kernel(x, ln_in_g, ln_in_b, w_in, w_pool, pool_scale, w_out, ln1_g, ln1_b, w_ff1, b_ff1, w_ff2, b_ff2, ln2_g, ln2_b, loss_target, m_ln_in_g, m_ln_in_b, m_w_in, m_w_pool, m_pool_scale, m_w_out, m_ln1_g, m_ln1_b, m_w_ff1, m_b_ff1, m_w_ff2, m_b_ff2, m_ln2_g, m_ln2_b, v_ln_in_g, v_ln_in_b, v_w_in, v_w_pool, v_pool_scale, v_w_out, v_ln1_g, v_ln1_b, v_w_ff1, v_b_ff1, v_w_ff2, v_b_ff2, v_ln2_g, v_ln2_b):
    t, d = x.shape[1], x.shape[2]
    n_groups = len(POOL_WINDOWS)
    c_pool = w_pool.shape[3]
    p = n_groups * c_pool
    n_heads = (d - p) // HEAD_DIM
    ws_in = w_in.shape[2]
    n_in = N_DEV * ws_in
    ws_out = w_out.shape[1]
    ws_f = w_ff1.shape[2]
    f = N_DEV * ws_f
    pr = w_pool.shape[2]
    assert n_in == p + 3 * n_heads * HEAD_DIM and N_DEV * ws_out == d and N_DEV * pr == c_pool

    tm_big = min(t, 1024)
    tm_ep = min(t, 512)
    tkk = min(t, 2048)
    half_f = min(ws_f, 1024)
    per_f = ws_f // half_f

    x2 = x.reshape(t, d)
    target = loss_target.reshape(t, d)
    g0, b0 = ln_in_g.reshape(1, d), ln_in_b.reshape(1, d)

    shards = [w_in.reshape(d, ws_in), w_out.reshape(ws_out, d), w_ff1.reshape(d, ws_f), w_ff2.reshape(ws_f, d),
              w_pool.reshape(n_groups * pr, c_pool)]
    x_, y_, c_ = _place()
    me = 4 * x_ + 2 * y_ + c_
    def landing(block):
        return lax.dynamic_update_index_in_dim(lax.empty((N_DEV, *block.shape), block.dtype), block, me, 0)

    land_in, land_out, land_1, land_2, land_pool = [landing(s.astype(BF16)) for s in shards]
    land_scale = landing(pool_scale.reshape(n_groups, pr))

    def sds(shape, dtype=F32):
        return jax.ShapeDtypeStruct(shape, dtype)

    vec = pl.BlockSpec((1, d), lambda m, n, k: (0, 0))
    row_ep = pl.BlockSpec((tm_ep, d), lambda m, n, k: (m, 0))
    tm_res = min(t, 256)
    row_res = pl.BlockSpec((tm_res, d), lambda m, n, k: (m, 0))
    seq = ("arbitrary", "arbitrary", "arbitrary")

    two_level = _plan_gather_own([SIBLING, ACROSS_Y, ACROSS_X, DIAGONAL])
    forward = _plan_gather_forward(1)
    first_needed = [land_in, land_pool, land_scale]
    (h0, h0b), first_needed = _ln_in_fwd(
        x2, g0, b0, tm_big, comm=(first_needed, *_join_plans(*[(two_level[0], 1, two_level[1])] * 3)))
    win_g, wpool_g, scale_g = _copies_now("gather_forward_w_in", first_needed, *_plan_gather_forward(3))
    wp_full = wpool_g.reshape(N_DEV, n_groups, pr, c_pool).transpose(1, 0, 2, 3).reshape(n_groups, c_pool, c_pool)
    sc_full = scale_g.transpose(1, 0, 2).reshape(n_groups, 1, c_pool)

    pool_shards = p // ws_in

    def mm_u(name, first, count, dtype, comm=None):
        return _matmul(
            name, h0b, win_g, dims=NN, grid=(t // tm_big, count, 1),
            a_spec=pl.BlockSpec((tm_big, d), lambda m, n, k: (m, 0)),
            b_spec=pl.BlockSpec((None, d, ws_in), lambda m, n, k: (n + first, 0, 0)),
            out_shape=[sds((t, count * ws_in), dtype)],
            out_specs=[pl.BlockSpec((tm_big, ws_in), lambda m, n, k: (m, n))],
            acc_shape=(tm_big, ws_in), epilogue=_store_epilogue(dtype), comm=comm)

    half = land_1.shape[1] // 2
    diag_a, diag_b = _plan_gather_own([DIAGONAL], (0, half)), _plan_gather_own([DIAGONAL], (half, half))
    (u_pool,), (w1_diag,) = mm_u("mm_u_pool", 0, pool_shards, F32, comm=([land_1], *diag_a))
    (qkv,), (wout_part,) = mm_u("mm_u_qkv", pool_shards, N_DEV - pool_shards, BF16, comm=([land_out], *two_level))

    (y_pool, ypre), (wout_g, w1_diag) = _pool_fwd(
        u_pool, wp_full, sc_full, t, c_pool,
        comm=([wout_part, w1_diag], *_join_plans((forward[0], 1, forward[1]), (diag_b[0], 1, diag_b[1]))))
    (o,), (w1_part,) = _attn_fwd(qkv, t, n_heads, comm=([w1_diag], *_plan_gather_own([SIBLING, ACROSS_Y, ACROSS_X])))
    mixin = jnp.concatenate([y_pool, o.astype(BF16)], axis=1)
    wout_2d = wout_g.reshape(d, d)

    def ep_ln1(acc, ex, outs):
        h0_ref, g_ref, b_ref = ex
        r1 = DEEPNORM_ALPHA * h0_ref[...] + acc
        xhat, _ = _ln_stats(r1)
        h1 = xhat * g_ref[...] + b_ref[...]
        outs[0][...] = r1
        outs[1][...] = h1
        outs[2][...] = h1.astype(BF16)

    (r1, h1, h1b), (w1_g,) = _matmul(
        "mm_mix_ln1", mixin, wout_2d, dims=NN, grid=(t // tm_res, 1, 1),
        a_spec=pl.BlockSpec((tm_res, d), lambda m, n, k: (m, 0)),
        b_spec=pl.BlockSpec((d, d), lambda m, n, k: (0, 0)),
        extras=(h0, ln1_g, ln1_b), extra_specs=(row_res, vec, vec),
        out_shape=[sds((t, d)), sds((t, d)), sds((t, d), BF16)], out_specs=[row_res] * 3,
        acc_shape=(tm_res, d), epilogue=ep_ln1, comm=([w1_part], *forward))

    def ep_ff1(acc, ex, outs):
        f1 = acc + ex[0][...]
        outs[0][...] = f1
        r = jnp.maximum(f1, 0.0)
        outs[1][...] = (r * r).astype(BF16)

    ff_tile = pl.BlockSpec((tm_big, half_f), lambda m, n, k: (m, n))
    (f1, act), (w2_part,) = _matmul(
        "mm_ff1", h1b, w1_g, dims=NN, grid=(t // tm_big, f // half_f, 1),
        a_spec=pl.BlockSpec((tm_big, d), lambda m, n, k: (m, 0)),
        b_spec=pl.BlockSpec((None, d, half_f), lambda m, n, k: (n // per_f, 0, n % per_f)),
        extras=(b_ff1,), extra_specs=(pl.BlockSpec((1, half_f), lambda m, n, k: (0, n)),),
        out_shape=[sds((t, f)), sds((t, f), BF16)], out_specs=[ff_tile, ff_tile],
        acc_shape=(tm_big, half_f), epilogue=ep_ff1, comm=([land_2], *two_level))
    (w2_g,) = _copies_now("gather_forward_w_ff2", [w2_part], *forward)

    def ep_ln2(acc, ex, outs):
        h1_ref, tgt_ref, bf2_ref, g_ref, b_ref = ex
        dr2_ref, dr2b_ref, dg_ref, db_ref, dbf2_ref, loss_ref = outs
        first = pl.program_id(0) == 0
        r2 = DEEPNORM_ALPHA * h1_ref[...] + (acc + bf2_ref[...])
        xhat, rstd = _ln_stats(r2)
        err = xhat * g_ref[...] + b_ref[...] - tgt_ref[...]
        dr2, dg, db = _ln_bwd(err * (1.0 / d), xhat, rstd, g_ref[...])
        dr2_ref[...] = dr2
        dr2b_ref[...] = dr2.astype(BF16)
        _acc_rows(first, dg_ref, dg)
        _acc_rows(first, db_ref, db)
        _acc_rows(first, dbf2_ref, jnp.sum(dr2, axis=0, keepdims=True))
        _acc_rows(first, loss_ref, jnp.sum(err * err, axis=0, keepdims=True) * (0.5 / d))

    dr2, dr2b, dg2, db2, dbf2, loss_vec = _matmul(
        "mm_ff2_ln2_loss", act, w2_g, dims=NN, grid=(t // tm_ep, 1, N_DEV),
        a_spec=pl.BlockSpec((tm_ep, ws_f), lambda m, n, k: (m, k)),
        b_spec=pl.BlockSpec((None, ws_f, d), lambda m, n, k: (k, 0, 0)),
        extras=(h1, target, b_ff2, ln2_g, ln2_b), extra_specs=(row_ep, row_ep, vec, vec, vec),
        out_shape=[sds((t, d)), sds((t, d), BF16)] + [sds((1, d))] * 4, out_specs=[row_ep, row_ep, vec, vec, vec, vec],
        acc_shape=(tm_ep, d), epilogue=ep_ln2, sem=seq)

    def ep_dff1(acc, ex, outs):
        df1 = acc * (2.0 * jnp.maximum(ex[0][...], 0.0))
        outs[0][...] = df1.astype(BF16)
        _acc_rows(pl.program_id(1) == 0, outs[1], jnp.sum(df1, axis=0, keepdims=True))

    df_tile = pl.BlockSpec((tm_big, ws_f), lambda n, m, k: (m, n))
    df1b, dbf1 = _matmul(
        "mm_dff1", dr2b, w2_g, dims=NT, grid=(N_DEV, t // tm_big, 1),
        a_spec=pl.BlockSpec((tm_big, d), lambda n, m, k: (m, 0)),
        b_spec=pl.BlockSpec((None, ws_f, d), lambda n, m, k: (n, 0, 0)),
        extras=(f1,), extra_specs=(df_tile,),
        out_shape=[sds((t, f), BF16), sds((1, f))], out_specs=[df_tile, pl.BlockSpec((1, ws_f), lambda n, m, k: (0, n))],
        acc_shape=(tm_big, ws_f), epilogue=ep_dff1, sem=("parallel", "arbitrary", "arbitrary"))

    tn_d = min(d, 1024)
    dw2 = _matmul(
        "mm_dw2", act, dr2b, dims=TN, grid=(N_DEV, d // tn_d, t // tkk),
        a_spec=pl.BlockSpec((tkk, ws_f), lambda m, n, k: (k, m)),
        b_spec=pl.BlockSpec((tkk, tn_d), lambda m, n, k: (k, n)),
        out_shape=[sds((N_DEV, ws_f, d))], out_specs=[pl.BlockSpec((None, ws_f, tn_d), lambda m, n, k: (m, 0, n))],
        acc_shape=(ws_f, tn_d), epilogue=_store_epilogue(F32))[0]

    dw1 = _matmul(
        "mm_dw1", h1b, df1b, dims=TN, grid=(d // tn_d, N_DEV, t // tkk),
        a_spec=pl.BlockSpec((tkk, tn_d), lambda m, n, k: (k, m)),
        b_spec=pl.BlockSpec((tkk, ws_f), lambda m, n, k: (k, n)),
        out_shape=[sds((N_DEV, d, ws_f))], out_specs=[pl.BlockSpec((None, tn_d, ws_f), lambda m, n, k: (n, m, 0))],
        acc_shape=(tn_d, ws_f), epilogue=_store_epilogue(F32))[0]

    def ep_ln1_bwd(acc, ex, outs):
        dr2_ref, r1_ref, g_ref = ex
        first = pl.program_id(0) == 0
        xhat, rstd = _ln_stats(r1_ref[...])
        dr1, dg, db = _ln_bwd(DEEPNORM_ALPHA * dr2_ref[...] + acc, xhat, rstd, g_ref[...])
        outs[0][...] = dr1
        outs[1][...] = dr1.astype(BF16)
        _acc_rows(first, outs[2], dg)
        _acc_rows(first, outs[3], db)

    slots = _owner_slots()

    def to_sibling(parts):
        return (parts + [lax.empty((4, *pt.shape[1:]), F32) for pt in parts], *_plan_rs_sibling(len(parts)))

    def to_owner(names_, parts, from_sib):
        sums = [_rs_chip_sum("rs_chip_sum_" + nm, slots, pt, fs) for nm, pt, fs in zip(names_, parts, from_sib)]
        return (sums + [lax.empty(cs.shape, BF16) for cs in sums], *_plan_rs_owner(len(sums)))

    (dr1, dr1b, dg1, db1), (dw1, sib_1) = _matmul(
        "mm_dh1_ln1_bwd", df1b, w1_g, dims=NT, grid=(t // tm_ep, 1, N_DEV),
        a_spec=pl.BlockSpec((tm_ep, ws_f), lambda m, n, k: (m, k)),
        b_spec=pl.BlockSpec((None, d, ws_f), lambda m, n, k: (k, 0, 0)),
        extras=(dr2, r1, ln1_g), extra_specs=(row_ep, row_ep, vec),
        out_shape=[sds((t, d)), sds((t, d), BF16), sds((1, d)), sds((1, d))], out_specs=[row_ep, row_ep, vec, vec],
        acc_shape=(tm_ep, d), epilogue=ep_ln1_bwd, sem=seq,
        comm=to_sibling([dw1]))
    own_1 = to_owner(["w_ff1"], [dw1], [sib_1])

    (dwout,), (dw2, sib_2) = _matmul(
        "mm_dwout", mixin, dr1b, dims=TN, grid=(d // tn_d, d // tn_d, t // tkk),
        a_spec=pl.BlockSpec((tkk, tn_d), lambda m, n, k: (k, m)),
        b_spec=pl.BlockSpec((tkk, tn_d), lambda m, n, k: (k, n)),
        out_shape=[sds((d, d))], out_specs=[pl.BlockSpec((tn_d, tn_d), lambda m, n, k: (m, n))],
        acc_shape=(tn_d, tn_d), epilogue=_store_epilogue(F32), comm=to_sibling([dw2]))
    dwout = dwout.reshape(N_DEV, ws_out, d)
    own_2 = to_owner(["w_ff2"], [dw2], [sib_2])[0]

    tn_mix = min(tn_d, p, d - p)

    def mm_dmixin(name, first, width, dtype, comm=None):
        return _matmul(
            name, dr1b, wout_2d, dims=NT, grid=(t // tm_big, width // tn_mix, 1),
            a_spec=pl.BlockSpec((tm_big, d), lambda m, n, k: (m, 0)),
            b_spec=pl.BlockSpec((tn_mix, d), lambda m, n, k: (n + first // tn_mix, 0)),
            out_shape=[sds((t, width), dtype)], out_specs=[pl.BlockSpec((tm_big, tn_mix), lambda m, n, k: (m, n))],
            acc_shape=(tm_big, tn_mix), epilogue=_store_epilogue(dtype), comm=comm)

    (dy_pool,), (dwout, sib_out) = mm_dmixin("mm_dmixin_pool", 0, p, F32, comm=to_sibling([dwout]))
    (do,) = mm_dmixin("mm_dmixin_att", p, d - p, BF16)

    (du_pool, dwp, dsc), (_, chips_out) = _pool_bwd(
        dy_pool, ypre, wp_full, sc_full, t, c_pool, comm=to_owner(["w_out"], [dwout], [sib_out]))
    (dq, dk, dv), (_, chips_1) = _attn_bwd(qkv, do, o, t, n_heads, comm=own_1)
    dub = jnp.concatenate([du_pool.astype(BF16), dq, dk, dv], axis=1)

    (dwin,), own_2 = _matmul(
        "mm_dwin", h0b, dub, dims=TN, grid=(d // tn_d, N_DEV, t // tkk),
        a_spec=pl.BlockSpec((tkk, tn_d), lambda m, n, k: (k, m)),
        b_spec=pl.BlockSpec((tkk, ws_in), lambda m, n, k: (k, n)),
        out_shape=[sds((N_DEV, d, ws_in))], out_specs=[pl.BlockSpec((None, tn_d, ws_in), lambda m, n, k: (n, m, 0))],
        acc_shape=(tn_d, ws_in), epilogue=_store_epilogue(F32), comm=(own_2, *_plan_rs_owner(1, (1, 2))))
    dwp_g = dwp.reshape(n_groups, N_DEV, pr, c_pool).transpose(1, 0, 2, 3).reshape(N_DEV, n_groups * pr, c_pool)

    def ep_ln0_bwd(acc, ex, outs):
        dr1_ref, x_ref, g_ref = ex
        first = pl.program_id(0) == 0
        xhat, rstd = _ln_stats(x_ref[...])
        dx, dg, db = _ln_bwd(DEEPNORM_ALPHA * dr1_ref[...] + acc, xhat, rstd, g_ref[...])
        outs[0][...] = dx
        _acc_rows(first, outs[1], dg)
        _acc_rows(first, outs[2], db)

    sib_in = to_sibling([dwin, dwp_g])
    to_diagonal = _plan_rs_owner(1, (3,))
    last_host = (own_2 + sib_in[0], *_join_plans((to_diagonal[0], 2, to_diagonal[1]), (sib_in[1], 4, sib_in[2])))
    def two_blocks(a_ref, b_ref):
        return _dot(a_ref[:, :ws_in], b_ref[0], NT) + _dot(a_ref[:, ws_in:], b_ref[1], NT)

    (dx, dg0, db0), (_, chips_2, dwin, dwp_g, sib_in_, sib_p) = _matmul(
        "mm_dh0_ln0_bwd", dub, win_g, dims=NT, grid=(t // tm_ep, 1, N_DEV // 2),
        a_spec=pl.BlockSpec((tm_ep, 2 * ws_in), lambda m, n, k: (m, k)),
        b_spec=pl.BlockSpec((2, d, ws_in), lambda m, n, k: (k, 0, 0)),
        extras=(dr1, x2, g0), extra_specs=(row_ep, row_ep, vec),
        out_shape=[sds((t, d)), sds((1, d)), sds((1, d))], out_specs=[row_ep, vec, vec],
        acc_shape=(tm_ep, d), epilogue=ep_ln0_bwd, sem=seq,
        comm=last_host, dot_fn=two_blocks)

    _, _, chips_in, chips_p = _copies_now("rs_owner_w_in", *to_owner(["w_in", "w_pool"], [dwin, dwp_g], [sib_in_, sib_p]))
    w_of = {"w_in": shards[0], "w_out": shards[1], "w_ff1": shards[2], "w_ff2": shards[3], "w_pool": shards[4]}
    mv_of = {"w_in": (m_w_in, v_w_in), "w_out": (m_w_out, v_w_out), "w_ff1": (m_w_ff1, v_w_ff1),
             "w_ff2": (m_w_ff2, v_w_ff2), "w_pool": (m_w_pool, v_w_pool)}
    big = {}
    for nm, pt, fs, fc in [("w_ff1", dw1, sib_1, chips_1), ("w_ff2", dw2, sib_2, chips_2), ("w_out", dwout, sib_out, chips_out),
                           ("w_in", dwin, sib_in_, chips_in), ("w_pool", dwp_g, sib_p, chips_p)]:
        w2d = w_of[nm]
        m_, v_ = mv_of[nm]
        big[nm] = _rs_final_adamw("rs_final_adamw_" + nm, slots, pt, fs, fc, w2d, m_.reshape(w2d.shape), v_.reshape(w2d.shape))

    n_f_rows = f // d
    pad_sc = d - p
    packet = jnp.concatenate(
        [loss_vec, dg0, db0, dg1, db1, dbf2, dg2, db2, dbf1.reshape(n_f_rows, d),
         jnp.pad(dsc.reshape(1, p), ((0, 0), (0, pad_sc)))], axis=0)
    n_rows = packet.shape[0]
    n_pad = (-n_rows) % 8
    packet = jnp.pad(packet, ((0, n_pad), (0, 0)))
    sums, loss11 = _small_all_reduce(packet)
    dsc_full = sums[8 + n_f_rows, :p].reshape(n_groups, N_DEV, pr)
    dsc_mine = lax.dynamic_index_in_dim(dsc_full, me, axis=1, keepdims=False)

    def sc_row(a):
        return jnp.pad(a.reshape(1, n_groups * pr), ((0, 0), (0, d - n_groups * pr)))

    def small_pack(ln0g, ln0b, l1g, l1b, bf2, l2g, l2b, bf1, sc):
        rows = [jnp.zeros((1, d), F32), ln0g.reshape(1, d), ln0b.reshape(1, d), l1g, l1b, bf2, l2g, l2b,
                bf1.reshape(n_f_rows, d), sc_row(sc), jnp.zeros((n_pad, d), F32)]
        return jnp.concatenate(rows, axis=0)

    w_small = small_pack(ln_in_g, ln_in_b, ln1_g, ln1_b, b_ff2, ln2_g, ln2_b, b_ff1, pool_scale)
    m_small = small_pack(m_ln_in_g, m_ln_in_b, m_ln1_g, m_ln1_b, m_b_ff2, m_ln2_g, m_ln2_b, m_b_ff1, m_pool_scale)
    v_small = small_pack(v_ln_in_g, v_ln_in_b, v_ln1_g, v_ln1_b, v_b_ff2, v_ln2_g, v_ln2_b, v_b_ff1, v_pool_scale)
    g_small = jnp.concatenate([sums[:8 + n_f_rows], sc_row(dsc_mine), jnp.zeros((n_pad, d), F32)], axis=0)
    small = (g_small,) + tuple(_small_adamw(w_small, g_small, m_small, v_small))

    def unpack(a):
        sc = a[8 + n_f_rows, :n_groups * pr].reshape(1, n_groups, pr)
        return {"ln_in_g": a[1], "ln_in_b": a[2], "ln1_g": a[3:4], "ln1_b": a[4:5], "b_ff2": a[5:6], "ln2_g": a[6:7],
                "ln2_b": a[7:8], "b_ff1": a[8:8 + n_f_rows].reshape(1, f), "pool_scale": sc}

    shapes = {"w_in": w_in.shape, "w_out": w_out.shape, "w_ff1": w_ff1.shape, "w_ff2": w_ff2.shape, "w_pool": w_pool.shape}
    order = ["ln_in_g", "ln_in_b", "w_in", "w_pool", "pool_scale", "w_out", "ln1_g", "ln1_b", "w_ff1", "b_ff1", "w_ff2",
             "b_ff2", "ln2_g", "ln2_b"]
    outs = []
    for kind in range(4):
        small_k = unpack(small[kind])
        for nm in order:
            outs.append(big[nm][kind].reshape(shapes[nm]) if nm in big else small_k[nm])
    return (loss11.reshape(()), dx.reshape(x.shape), *outs)
```

```python
import functools
import math

import jax
import jax.numpy as jnp
from jax import lax
from jax.experimental import pallas as pl
from jax.experimental.pallas import tpu as pltpu

F32 = jnp.float32
BF16 = jnp.bfloat16
MESH = pl.DeviceIdType.MESH

N_DEV = 8
HEAD_DIM = 128
POOL_WINDOWS = (2, 4, 8, 16)
DEEPNORM_ALPHA = (2.0 * 1) ** 0.25
LN_EPS = 1e-5
ADAM_LR = 0.001
ADAM_B1 = 0.9
ADAM_B2 = 0.999
ADAM_EPS = 1e-08
ADAM_WD = 0.01
ADAM_STEP = 10

V7X_VMEM_LIMIT = 56 * 1024 * 1024
ATT_BLOCK = 256
POOL_CHUNK = 256

NN = (((1,), (0,)), ((), ()))
NT = (((1,), (1,)), ((), ()))
TN = (((0,), (0,)), ((), ()))


def _dot(a, b, dims=NN):
    return lax.dot_general(a, b, dims, preferred_element_type=F32)


def _cparams(sem=None):
    return pltpu.CompilerParams(dimension_semantics=sem, vmem_limit_bytes=V7X_VMEM_LIMIT)


def _ln_stats(r):
    mu = jnp.mean(r, axis=-1, keepdims=True)
    xc = r - mu
    var = jnp.mean(xc * xc, axis=-1, keepdims=True)
    rstd = lax.rsqrt(var + LN_EPS)
    return xc * rstd, rstd


def _ln_bwd(dy, xhat, rstd, g):
    dxh = dy * g
    m1 = jnp.mean(dxh, axis=-1, keepdims=True)
    m2 = jnp.mean(dxh * xhat, axis=-1, keepdims=True)
    dx = rstd * (dxh - m1 - xhat * m2)
    dg = jnp.sum(dy * xhat, axis=0, keepdims=True)
    db = jnp.sum(dy, axis=0, keepdims=True)
    return dx, dg, db


def _acc_rows(first, ref, val):
    @pl.when(first)
    def _():
        ref[...] = val

    @pl.when(jnp.logical_not(first))
    def _():
        ref[...] += val


def _call(body, *, name, grid, in_specs, out_specs, out_shape, inputs, scratch_shapes=(), sem=None, comm=None):
    in_specs, out_specs, out_shape, inputs = list(in_specs), list(out_specs), list(out_shape), list(inputs)
    if comm is None:
        outs = pl.pallas_call(
            body, name=name, grid=grid, in_specs=in_specs, out_specs=out_specs, out_shape=out_shape,
            scratch_shapes=list(scratch_shapes), compiler_params=_cparams(sem))(*inputs)
        return list(outs), []
    arrays, plan, n_copies = comm
    n_in, n_out, nc, n_scr = len(inputs), len(out_shape), len(arrays), len(scratch_shapes)

    def hosted(*refs):
        ins = refs[:n_in]
        outs = refs[n_in + nc:n_in + nc + n_out]
        passed = refs[n_in + nc + n_out:n_in + 2 * nc + n_out]
        scratch = refs[n_in + 2 * nc + n_out:n_in + 2 * nc + n_out + n_scr]
        send_sems, recv_sems = refs[-2], refs[-1]
        ids = [pl.program_id(ax) for ax in range(len(grid))]
        first = functools.reduce(jnp.logical_and, [i_ == 0 for i_ in ids])
        last = functools.reduce(jnp.logical_and, [i_ == g - 1 for i_, g in zip(ids, grid)])

        @pl.when(first)
        def _():
            for cp in _plan_copies(plan, passed, send_sems, recv_sems):
                cp.start()

        body(*ins, *outs, *scratch)

        @pl.when(last)
        def _():
            for cp in _plan_copies(plan, passed, send_sems, recv_sems):
                cp.wait_send()
                cp.wait_recv()

    any_spec = pl.BlockSpec(memory_space=pl.ANY)
    outs = pl.pallas_call(
        hosted, name=name, grid=grid,
        in_specs=in_specs + [any_spec] * nc, out_specs=out_specs + [any_spec] * nc,
        out_shape=out_shape + [jax.ShapeDtypeStruct(a.shape, a.dtype) for a in arrays],
        scratch_shapes=list(scratch_shapes) + [pltpu.SemaphoreType.DMA((n_copies,)), pltpu.SemaphoreType.DMA((n_copies,))],
        input_output_aliases={n_in + i: n_out + i for i in range(nc)},
        compiler_params=pltpu.CompilerParams(dimension_semantics=("arbitrary",) * len(grid),
                                             vmem_limit_bytes=V7X_VMEM_LIMIT, has_side_effects=True),
    )(*inputs, *arrays)
    return list(outs[:n_out]), list(outs[n_out:])


def _matmul(name, a, b, *, dims, grid, a_spec, b_spec, extras=(), extra_specs=(), out_shape, out_specs,
            acc_shape, epilogue, k_axis=2, sem=("parallel", "parallel", "arbitrary"), comm=None, dot_fn=None):
    nk = grid[k_axis]
    n_extra = len(extras)
    n_out = len(out_shape)
    if dot_fn is None:
        def dot_fn(a_ref, b_ref):
            return _dot(a_ref[...], b_ref[...], dims)

    def body(a_ref, b_ref, *rest):
        extra_refs = rest[:n_extra]
        out_refs = rest[n_extra:n_extra + n_out]
        if nk == 1:
            epilogue(dot_fn(a_ref, b_ref), extra_refs, out_refs)
            return
        acc_ref = rest[n_extra + n_out]
        k = pl.program_id(k_axis)

        @pl.when(k == 0)
        def _():
            acc_ref[...] = jnp.zeros(acc_shape, F32)

        acc_ref[...] += dot_fn(a_ref, b_ref)

        @pl.when(k == nk - 1)
        def _():
            epilogue(acc_ref[...], extra_refs, out_refs)

    outs, passed = _call(
        body, name=name, grid=grid, in_specs=[a_spec, b_spec, *extra_specs], out_specs=out_specs, out_shape=out_shape,
        inputs=[a, b, *extras], scratch_shapes=[] if nk == 1 else [pltpu.VMEM(acc_shape, F32)], sem=sem, comm=comm)
    return outs if comm is None else (outs, passed)


def _store_epilogue(dtype):

    def ep(acc, extra_refs, out_refs):
        out_refs[0][...] = acc.astype(dtype)
    return ep


def _ln_in_fwd(x, g, b, tm, comm=None):
    t, d = x.shape

    def body(x_ref, g_ref, b_ref, h_ref, hb_ref):
        xhat, _ = _ln_stats(x_ref[...])
        h = xhat * g_ref[...] + b_ref[...]
        h_ref[...] = h
        hb_ref[...] = h.astype(BF16)

    row = pl.BlockSpec((tm, d), lambda i: (i, 0))
    vec = pl.BlockSpec((1, d), lambda i: (0, 0))
    return _call(
        body, name="ln_in_fwd", grid=(t // tm,), in_specs=[row, vec, vec], out_specs=[row, row],
        out_shape=[jax.ShapeDtypeStruct((t, d), F32), jax.ShapeDtypeStruct((t, d), BF16)],
        inputs=[x, g, b], sem=("parallel",), comm=comm)


def _split3(x):
    hi = x.astype(BF16)
    r = x - hi.astype(F32)
    mid = r.astype(BF16)
    lo = (r - mid.astype(F32)).astype(BF16)
    return hi, mid, lo


def _split2(x):
    hi = x.astype(BF16)
    lo = (x - hi.astype(F32)).astype(BF16)
    return hi, lo


def _pool_fwd(u, wp, sc, t, c, comm=None):
    n_groups = len(POOL_WINDOWS)
    tc = POOL_CHUNK
    n_chunks = t // tc

    def body(u_ref, wp_ref, sc_ref, y_ref, ypre_ref, xp_ref):
        g = pl.program_id(0)
        xp_ref[pl.ds(0, tc), :] = jnp.zeros((tc, c), F32)
        xp_ref[pl.ds(tc, t), :] = u_ref[...]
        out_i = lax.broadcasted_iota(jnp.int32, (tc, 2 * tc), 0)
        in_j = lax.broadcasted_iota(jnp.int32, (tc, 2 * tc), 1)
        lag = tc + out_i - in_j
        t_in_chunk = lax.broadcasted_iota(jnp.int32, (tc, 1), 0)
        for gi, w in enumerate(POOL_WINDOWS):
            @pl.when(g == gi)
            def _(w=w):
                band = jnp.logical_and(lag >= 0, lag < w).astype(BF16)

                def chunk(ci, carry):
                    start = pl.multiple_of(ci * tc, tc)
                    win = xp_ref[pl.ds(start, 2 * tc), :]
                    hi, mid, lo = _split3(win)
                    wsum = _dot(band, hi) + _dot(band, mid) + _dot(band, lo)
                    cnt = jnp.minimum(ci * tc + t_in_chunk + 1, w).astype(F32)
                    ypre = wsum * (1.0 / cnt) - win[tc:, :]
                    ypre_b = ypre.astype(BF16)
                    y = _dot(ypre_b, wp_ref[...]) * sc_ref[...]
                    ypre_ref[pl.ds(start, tc), :] = ypre_b
                    y_ref[pl.ds(start, tc), :] = y.astype(BF16)
                    return carry

                lax.fori_loop(0, n_chunks, chunk, 0)

    col = pl.BlockSpec((t, c), lambda g: (0, g))
    return _call(
        body, name="pool_fwd", grid=(n_groups,),
        in_specs=[col, pl.BlockSpec((None, c, c), lambda g: (g, 0, 0)), pl.BlockSpec((None, 1, c), lambda g: (g, 0, 0))],
        out_specs=[col, col],
        out_shape=[jax.ShapeDtypeStruct((t, n_groups * c), BF16), jax.ShapeDtypeStruct((t, n_groups * c), BF16)],
        inputs=[u, wp, sc], scratch_shapes=[pltpu.VMEM((t + tc, c), F32)], sem=("parallel",), comm=comm)


def _pool_bwd(dmixin, ypre, wp, sc, t, c, comm=None):
    n_groups = len(POOL_WINDOWS)
    tc = POOL_CHUNK
    n_chunks = t // tc

    def body(dy_ref, ypre_ref, wp_ref, sc_ref, du_ref, dwp_ref, dsc_ref, zp_ref):
        g = pl.program_id(0)
        zp_ref[pl.ds(t, tc), :] = jnp.zeros((tc, c), F32)
        dwp_ref[...] = jnp.zeros((c, c), F32)
        dsc_ref[...] = jnp.zeros((1, c), F32)
        out_i = lax.broadcasted_iota(jnp.int32, (tc, 2 * tc), 0)
        in_j = lax.broadcasted_iota(jnp.int32, (tc, 2 * tc), 1)
        lead = in_j - out_i
        t_in_chunk = lax.broadcasted_iota(jnp.int32, (tc, 1), 0)
        for gi, w in enumerate(POOL_WINDOWS):
            @pl.when(g == gi)
            def _(w=w):
                band = jnp.logical_and(lead >= 0, lead < w).astype(BF16)

                def first(ci, carry):
                    start = pl.multiple_of(ci * tc, tc)
                    dy = dy_ref[pl.ds(start, tc), :]
                    yp = ypre_ref[pl.ds(start, tc), :]
                    ymm = _dot(yp, wp_ref[...])
                    dsc_ref[...] += jnp.sum(dy * ymm, axis=0, keepdims=True)
                    dys_b = (dy * sc_ref[...]).astype(BF16)
                    dwp_ref[...] += _dot(yp, dys_b, TN)
                    dyp = _dot(dys_b, wp_ref[...], NT)
                    cnt = jnp.minimum(ci * tc + t_in_chunk + 1, w).astype(F32)
                    zp_ref[pl.ds(start, tc), :] = dyp * (1.0 / cnt)
                    du_ref[pl.ds(start, tc), :] = -dyp
                    return carry

                lax.fori_loop(0, n_chunks, first, 0)

                def second(ci, carry):
                    start = pl.multiple_of(ci * tc, tc)
                    hi, mid, lo = _split3(zp_ref[pl.ds(start, 2 * tc), :])
                    du_ref[pl.ds(start, tc), :] += _dot(band, hi) + _dot(band, mid) + _dot(band, lo)
                    return carry

                lax.fori_loop(0, n_chunks, second, 0)

    col = pl.BlockSpec((t, c), lambda g: (0, g))
    return _call(
        body, name="pool_bwd", grid=(n_groups,),
        in_specs=[col, col, pl.BlockSpec((None, c, c), lambda g: (g, 0, 0)), pl.BlockSpec((None, 1, c), lambda g: (g, 0, 0))],
        out_specs=[col, pl.BlockSpec((None, c, c), lambda g: (g, 0, 0)), pl.BlockSpec((None, 1, c), lambda g: (g, 0, 0))],
        out_shape=[jax.ShapeDtypeStruct((t, n_groups * c), F32), jax.ShapeDtypeStruct((n_groups, c, c), F32),
                   jax.ShapeDtypeStruct((n_groups, 1, c), F32)],
        inputs=[dmixin, ypre, wp, sc], scratch_shapes=[pltpu.VMEM((t + tc, c), F32)], sem=("parallel",), comm=comm)


ROW_PARTS = 2


def _att_consts():
    b = ATT_BLOCK
    rp = b // ROW_PARTS
    row = lax.broadcasted_iota(jnp.int32, (b, b), 0)
    col = lax.broadcasted_iota(jnp.int32, (b, b), 1)
    tri = (row >= col).astype(BF16)
    prow = lax.broadcasted_iota(jnp.int32, (rp, b), 0)
    pcol = lax.broadcasted_iota(jnp.int32, (rp, b), 1)
    causal = [pcol < prow + r * rp for r in range(ROW_PARTS)]
    return tri, causal


def _suffix_sum(x, tri):
    hi, lo = _split2(x)
    return _dot(hi, tri) + _dot(lo, tri)


LOG2_E = 1.4426950408889634


def _att_scores(qb, kb, mask):
    z2 = _dot(qb, kb, NT) * (LOG2_E / math.sqrt(HEAD_DIM))
    sp2 = jnp.maximum(z2, 0.0) + jnp.log2(1.0 + jnp.exp2(-jnp.abs(z2)))
    return z2, sp2, (sp2 if mask is None else jnp.where(mask, sp2, 0.0))


HEADS_PER_STEP = 2
ATT_LANES = HEADS_PER_STEP * HEAD_DIM


def _head_lanes(s):
    return slice(s * HEAD_DIM, (s + 1) * HEAD_DIM)


UNDERFLOW_LOG2 = 160.0


def _sweep_earlier_blocks(i, state, per_chain, block):
    def lowest(st):
        low = st[0]
        for k in range(per_chain, len(st), per_chain):
            low = jnp.minimum(low, st[k])
        return jnp.min(low)

    def more(c):
        return jnp.logical_and(c[0] < i, c[1] < UNDERFLOW_LOG2)

    def trip(c):
        st = block(i - 1 - c[0], c[2:])
        return (c[0] + 1, lowest(st)) + tuple(st)

    return lax.while_loop(more, trip, (jnp.int32(0), lowest(state)) + tuple(state))[2:]


def _attn_fwd(qkv, t, n_heads, comm=None):
    b = ATT_BLOCK
    nq = t // b
    n_steps = n_heads // HEADS_PER_STEP

    rp = b // ROW_PARTS
    chains = [(s, r) for s in range(HEADS_PER_STEP) for r in range(ROW_PARTS)]
    no_mask = [None] * ROW_PARTS

    def body(q_ref, k_ref, v_ref, o_ref):
        tri, causal = _att_consts()

        def blocks(qbs, j, state, masks):
            ks = pl.multiple_of(j * b, b)
            scores = [_att_scores(qbs[ci], k_ref[pl.ds(ks, b), _head_lanes(s)], masks[r]) for ci, (s, r) in enumerate(chains)]
            incls = [_suffix_sum(sc[2], tri) for sc in scores]
            out = []
            for ci, (s, r) in enumerate(chains):
                carry, acc = state[2 * ci], state[2 * ci + 1]
                a = jnp.exp2(scores[ci][0] - (incls[ci] + carry))
                if masks[r] is not None:
                    a = jnp.where(masks[r], a, 0.0)
                out += [carry + incls[ci][:, 0:1], acc + _dot(a.astype(BF16), v_ref[pl.ds(ks, b), _head_lanes(s)])]
            return tuple(out)

        def q_loop(i, _):
            qs = pl.multiple_of(i * b, b)
            qbs = [q_ref[pl.ds(qs + r * rp, rp), _head_lanes(s)] for s, r in chains]
            zero = (jnp.zeros((rp, 1), F32), jnp.zeros((rp, HEAD_DIM), F32)) * len(chains)
            state = blocks(qbs, i, zero, causal)
            state = _sweep_earlier_blocks(i, state, 2, lambda j, st: blocks(qbs, j, st, no_mask))
            for ci, (s, r) in enumerate(chains):
                o_ref[pl.ds(qs + r * rp, rp), _head_lanes(s)] = state[2 * ci + 1]
            return 0

        lax.fori_loop(0, nq, q_loop, 0)

    def heads(off):
        return pl.BlockSpec((t, ATT_LANES), lambda h: (0, off + h))

    return _call(
        body, name="attn_fwd", grid=(n_steps,),
        in_specs=[heads(0), heads(n_steps), heads(2 * n_steps)], out_specs=[heads(0)],
        out_shape=[jax.ShapeDtypeStruct((t, n_heads * HEAD_DIM), F32)],
        inputs=[qkv, qkv, qkv], sem=("parallel",), comm=comm)


def _attn_bwd(qkv, do, o, t, n_heads, comm=None):
    b = ATT_BLOCK
    nq = t // b
    n_steps = n_heads // HEADS_PER_STEP
    scale = 1.0 / math.sqrt(HEAD_DIM)
    rp = b // ROW_PARTS
    chains = [(s, r) for s in range(HEADS_PER_STEP) for r in range(ROW_PARTS)]
    no_mask = [None] * ROW_PARTS

    def body(q_ref, k_ref, v_ref, do_ref, o_ref, dq_ref, dk_ref, dv_ref, qt_ref, dot_ref, dkt_ref, dvt_ref):
        for j in range(nq):
            rows = pl.ds(j * b, b)
            qt_ref[j] = q_ref[rows, :].astype(F32).T.astype(BF16)
            dot_ref[j] = do_ref[rows, :].astype(F32).T.astype(BF16)
        dkt_ref[...] = jnp.zeros((nq, ATT_LANES, b), F32)
        dvt_ref[...] = jnp.zeros((nq, ATT_LANES, b), F32)
        tri, causal = _att_consts()

        def blocks(i, fixed, j, state, masks):
            ks = pl.multiple_of(j * b, b)
            n = len(chains)
            kbs = [k_ref[pl.ds(ks, b), _head_lanes(s)] for s, _ in chains]
            scores = [_att_scores(fixed[ci][0], kbs[ci], masks[r]) for ci, (s, r) in enumerate(chains)]
            incls = [_suffix_sum(sc[2], tri) for sc in scores]
            das = [_dot(fixed[ci][1], v_ref[pl.ds(ks, b), _head_lanes(s)], NT) for ci, (s, r) in enumerate(chains)]
            a_bs, gs = [], []
            for ci, (s, r) in enumerate(chains):
                a = jnp.exp2(scores[ci][0] - (incls[ci] + state[3 * ci]))
                if masks[r] is not None:
                    a = jnp.where(masks[r], a, 0.0)
                a_bs.append(a.astype(BF16))
                gs.append(a_bs[ci].astype(F32) * das[ci])
            g_incls = [_suffix_sum(g, tri) for g in gs]
            dz_bs = []
            for ci, (s, r) in enumerate(chains):
                rest = (fixed[ci][2] - state[3 * ci + 1]) - (g_incls[ci] - gs[ci])
                sig = jnp.exp2(scores[ci][0] - scores[ci][1])
                dz = (gs[ci] - sig * rest) * scale
                if masks[r] is not None:
                    dz = jnp.where(masks[r], dz, 0.0)
                dz_bs.append(dz.astype(BF16))
            out = []
            for ci in range(n):
                out += [state[3 * ci] + incls[ci][:, 0:1], state[3 * ci + 1] + g_incls[ci][:, 0:1],
                        state[3 * ci + 2] + _dot(dz_bs[ci], kbs[ci])]
            for s in range(HEADS_PER_STEP):
                lanes = _head_lanes(s)
                dk_add, dv_add = None, None
                for ci, (cs, r) in enumerate(chains):
                    if cs == s:
                        part = slice(r * rp, (r + 1) * rp)
                        dk_c = _dot(qt_ref[i, lanes, part], dz_bs[ci])
                        dv_c = _dot(dot_ref[i, lanes, part], a_bs[ci])
                        dk_add = dk_c if dk_add is None else dk_add + dk_c
                        dv_add = dv_c if dv_add is None else dv_add + dv_c
                dkt_ref[j, lanes, :] += dk_add
                dvt_ref[j, lanes, :] += dv_add
            return tuple(out)

        def q_loop(i, _):
            qs = pl.multiple_of(i * b, b)
            fixed = []
            for s, r in chains:
                rows = pl.ds(qs + r * rp, rp)
                dob = do_ref[rows, _head_lanes(s)]
                total = jnp.sum(dob.astype(F32) * o_ref[rows, _head_lanes(s)], axis=-1, keepdims=True)
                fixed.append((q_ref[rows, _head_lanes(s)], dob, total))
            zero = (jnp.zeros((rp, 1), F32), jnp.zeros((rp, 1), F32), jnp.zeros((rp, HEAD_DIM), F32)) * len(chains)
            state = blocks(i, fixed, i, zero, causal)
            state = _sweep_earlier_blocks(i, state, 3, lambda j, st: blocks(i, fixed, j, st, no_mask))
            for ci, (s, r) in enumerate(chains):
                dq_ref[pl.ds(qs + r * rp, rp), _head_lanes(s)] = state[3 * ci + 2].astype(BF16)
            return 0

        lax.fori_loop(0, nq, q_loop, 0)
        for j in range(nq):
            rows = pl.ds(j * b, b)
            dk_ref[rows, :] = dkt_ref[j].T.astype(BF16)
            dv_ref[rows, :] = dvt_ref[j].T.astype(BF16)

    def heads(off):
        return pl.BlockSpec((t, ATT_LANES), lambda h: (0, off + h))

    shape = jax.ShapeDtypeStruct((t, n_heads * HEAD_DIM), BF16)
    return _call(
        body, name="attn_bwd", grid=(n_steps,),
        in_specs=[heads(0), heads(n_steps), heads(2 * n_steps), heads(0), heads(0)],
        out_specs=[heads(0)] * 3, out_shape=[shape] * 3, inputs=[qkv, qkv, qkv, do, o],
        scratch_shapes=[pltpu.VMEM((nq, ATT_LANES, b), BF16)] * 2 + [pltpu.VMEM((nq, ATT_LANES, b), F32)] * 2,
        sem=("parallel",), comm=comm)


def _place():
    x, y, c = lax.axis_index("x"), lax.axis_index("y"), lax.axis_index("c")
    return x, y, c


def _flip(v, on):
    return 1 - v if on else v


def _plan_copies(plan, refs, send_sems, recv_sems):
    return [pltpu.make_async_remote_copy(src_ref=src, dst_ref=dst, send_sem=send_sems.at[k], recv_sem=recv_sems.at[k],
                                         device_id=dev, device_id_type=MESH)
            for k, (src, dst, dev) in enumerate(plan(refs))]


def _copies_now(name, arrays, plan, n_copies):
    n = len(arrays)

    def body(*refs):
        copies = _plan_copies(plan, refs[n:2 * n], refs[2 * n], refs[2 * n + 1])
        for cp in copies:
            cp.start()
        for cp in copies:
            cp.wait_send()
            cp.wait_recv()

    any_spec = pl.BlockSpec(memory_space=pl.ANY)
    return list(pl.pallas_call(
        body, name=name, in_specs=[any_spec] * n, out_specs=[any_spec] * n,
        out_shape=[jax.ShapeDtypeStruct(a.shape, a.dtype) for a in arrays],
        input_output_aliases={i: i for i in range(n)},
        scratch_shapes=[pltpu.SemaphoreType.DMA((n_copies,)), pltpu.SemaphoreType.DMA((n_copies,))],
        compiler_params=pltpu.CompilerParams(has_side_effects=True),
    )(*arrays))


SIBLING, ACROSS_Y, ACROSS_X, DIAGONAL = 1, 2, 4, 6


def _plan_gather_own(peers, rows=None):
    def plan(refs):
        x, y, c = _place()
        mine = refs[0].at[4 * x + 2 * y + c]
        if rows is not None:
            mine = mine.at[pl.ds(*rows)]
        return [(mine, mine, (_flip(x, k & 4), _flip(y, k & 2), _flip(c, k & 1))) for k in peers]
    return plan, len(peers)


def _plan_gather_forward(n):
    def plan(refs):
        x, y, c = _place()
        out = []
        for ti in range(n):
            for r in range(1, 4):
                blk = refs[ti].at[4 * _flip(x, r & 2) + 2 * _flip(y, r & 1) + c]
                out.append((blk, blk, (x, y, 1 - c)))
        return out
    return plan, 3 * n


def _join_plans(*parts):
    def plan(refs):
        out, at = [], 0
        for part, n_arrays, _ in parts:
            out += part(refs[at:at + n_arrays])
            at += n_arrays
        return out
    return plan, sum(n_cp for _, _, n_cp in parts)


def _plan_rs_sibling(n):
    def plan(refs):
        x, y, c = _place()
        out = []
        for ti in range(n):
            for r in range(4):
                src = refs[ti].at[4 * _flip(x, r & 2) + 2 * _flip(y, r & 1) + (1 - c)]
                out.append((src, refs[n + ti].at[r], (x, y, 1 - c)))
        return out
    return plan, 4 * n


def _plan_rs_owner(n, relations=(1, 2, 3)):
    def plan(refs):
        x, y, c = _place()
        out = []
        for ti in range(n):
            for r in relations:
                out.append((refs[ti].at[r], refs[n + ti].at[r], (_flip(x, r & 2), _flip(y, r & 1), c)))
        return out
    return plan, len(relations) * n


def _owner_slots():
    x, y, c = _place()
    idx = []
    for r in range(4):
        ox, oy = (1 - x if r & 2 else x), (1 - y if r & 1 else y)
        idx.append(4 * ox + 2 * oy + c)
    return jnp.stack(idx).astype(jnp.int32)


def _row_tile(rows, cols):
    tr = max(8, min(rows, (1 << 19) // cols))
    while rows % tr:
        tr //= 2
    return tr


def _rs_chip_sum(name, slots, partial, from_sibling):
    _, rows, cols = partial.shape
    tr = _row_tile(rows, cols)

    def body(slots_ref, p_ref, s_ref, o_ref):
        o_ref[...] = (p_ref[...] + s_ref[...]).astype(BF16)

    grid_spec = pltpu.PrefetchScalarGridSpec(
        num_scalar_prefetch=1, grid=(3, rows // tr),
        in_specs=[pl.BlockSpec((None, tr, cols), lambda r, i, s: (s[r + 1], i, 0)),
                  pl.BlockSpec((None, tr, cols), lambda r, i, s: (r + 1, i, 0))],
        out_specs=pl.BlockSpec((None, tr, cols), lambda r, i, s: (r + 1, i, 0)))
    return pl.pallas_call(
        body, name=name, grid_spec=grid_spec, out_shape=jax.ShapeDtypeStruct((4, rows, cols), BF16),
        compiler_params=_cparams(("parallel", "parallel")),
    )(slots, partial, from_sibling)


def _adamw(w, g, m, v):
    m = ADAM_B1 * m + (1.0 - ADAM_B1) * g
    v = ADAM_B2 * v + (1.0 - ADAM_B2) * (g * g)
    m_hat = m / (1.0 - ADAM_B1 ** ADAM_STEP)
    v_hat = v / (1.0 - ADAM_B2 ** ADAM_STEP)
    delta = -ADAM_LR * (m_hat / (jnp.sqrt(v_hat) + ADAM_EPS) + ADAM_WD * w)
    return delta, m, v


def _rs_final_adamw(name, slots, partial, from_sibling, from_chips, w, m, v):
    rows, cols = w.shape
    tr = _row_tile(rows, cols)

    def body(slots_ref, p_ref, s_ref, c1_ref, c2_ref, c3_ref, w_ref, m_ref, v_ref, g_ref, d_ref, nm_ref, nv_ref):
        g = p_ref[...] + s_ref[...]
        g = g + c1_ref[...].astype(F32)
        g = g + c2_ref[...].astype(F32)
        g = g + c3_ref[...].astype(F32)
        delta, nm, nv = _adamw(w_ref[...], g, m_ref[...], v_ref[...])
        g_ref[...] = g
        d_ref[...] = delta
        nm_ref[...] = nm
        nv_ref[...] = nv

    def slot(r):
        return pl.BlockSpec((None, tr, cols), lambda i, s: (r, i, 0))

    flat = pl.BlockSpec((tr, cols), lambda i, s: (i, 0))
    grid_spec = pltpu.PrefetchScalarGridSpec(
        num_scalar_prefetch=1, grid=(rows // tr,),
        in_specs=[pl.BlockSpec((None, tr, cols), lambda i, s: (s[0], i, 0)), slot(0), slot(1), slot(2), slot(3), flat, flat, flat],
        out_specs=[flat] * 4)
    return pl.pallas_call(
        body, name=name, grid_spec=grid_spec, out_shape=[jax.ShapeDtypeStruct((rows, cols), F32)] * 4,
        compiler_params=_cparams(("parallel",)),
    )(slots, partial, from_sibling, from_chips, from_chips, from_chips, w, m, v)


def _small_all_reduce(packet):
    rows, d = packet.shape

    def body(p_ref, sum_ref, loss_ref, all_ref, send_sems, recv_sems):
        x, y, c = _place()
        me = 4 * x + 2 * y + c
        all_ref[me] = p_ref[...]
        copies = []
        for k in range(1, N_DEV):
            px, py, pc = (1 - x if k & 4 else x), (1 - y if k & 2 else y), (1 - c if k & 1 else c)
            cp = pltpu.make_async_remote_copy(
                src_ref=p_ref, dst_ref=all_ref.at[me], send_sem=send_sems.at[k], recv_sem=recv_sems.at[k],
                device_id=(px, py, pc), device_id_type=MESH)
            cp.start()
            copies.append(cp)
        for cp in copies:
            cp.wait_recv()
        for cp in copies:
            cp.wait_send()
        total = all_ref[0]
        for j in range(1, N_DEV):
            total = total + all_ref[j]
        sum_ref[...] = total
        loss_ref[...] = jnp.sum(total[0:1, :], axis=-1, keepdims=True)

    vmem = pl.BlockSpec(memory_space=pltpu.VMEM)
    return pl.pallas_call(
        body, name="small_all_reduce",
        in_specs=[vmem], out_specs=[vmem, vmem],
        out_shape=[jax.ShapeDtypeStruct((rows, d), F32), jax.ShapeDtypeStruct((1, 1), F32)],
        scratch_shapes=[pltpu.VMEM((N_DEV, rows, d), F32), pltpu.SemaphoreType.DMA((N_DEV,)), pltpu.SemaphoreType.DMA((N_DEV,))],
        compiler_params=pltpu.CompilerParams(has_side_effects=True),
    )(packet)


def _small_adamw(w, g, m, v):
    def body(w_ref, g_ref, m_ref, v_ref, d_ref, nm_ref, nv_ref):
        delta, nm, nv = _adamw(w_ref[...], g_ref[...], m_ref[...], v_ref[...])
        d_ref[...] = delta
        nm_ref[...] = nm
        nv_ref[...] = nv

    vmem = pl.BlockSpec(memory_space=pltpu.VMEM)
    return pl.pallas_call(
        body, name="small_adamw", in_specs=[vmem] * 4, out_specs=[vmem] * 3,
        out_shape=[jax.ShapeDtypeStruct(w.shape, F32)] * 3,
    )(w, g, m, v)


def kernel(x, ln_in_g, ln_in_b, w_in, w_pool, pool_scale, w_out, ln1_g, ln1_b, w_ff1, b_ff1, w_ff2, b_ff2, ln2_g, ln2_b, loss_target, m_ln_in_g, m_ln_in_b, m_w_in, m_w_pool, m_pool_scale, m_w_out, m_ln1_g, m_ln1_b, m_w_ff1, m_b_ff1, m_w_ff2, m_b_ff2, m_ln2_g, m_ln2_b, v_ln_in_g, v_ln_in_b, v_w_in, v_w_pool, v_pool_scale, v_w_out, v_ln1_g, v_ln1_b, v_w_ff1, v_b_ff1, v_w_ff2, v_b_ff2, v_ln2_g, v_ln2_b):
    t, d = x.shape[1], x.shape[2]
    n_groups = len(POOL_WINDOWS)
    c_pool = w_pool.shape[3]
    p = n_groups * c_pool
    n_heads = (d - p) // HEAD_DIM
    ws_in = w_in.shape[2]
    n_in = N_DEV * ws_in
    ws_out = w_out.shape[1]
    ws_f = w_ff1.shape[2]
    f = N_DEV * ws_f
    pr = w_pool.shape[2]
    assert n_in == p + 3 * n_heads * HEAD_DIM and N_DEV * ws_out == d and N_DEV * pr == c_pool

    tm_big = min(t, 1024)
    tm_ep = min(t, 512)
    tkk = min(t, 4096)
    half_f = min(ws_f, 512)
    per_f = ws_f // half_f

    x2 = x.reshape(t, d)
    target = loss_target.reshape(t, d)
    g0, b0 = ln_in_g.reshape(1, d), ln_in_b.reshape(1, d)

    shards = [w_in.reshape(d, ws_in), w_out.reshape(ws_out, d), w_ff1.reshape(d, ws_f), w_ff2.reshape(ws_f, d),
              w_pool.reshape(n_groups * pr, c_pool)]
    x_, y_, c_ = _place()
    me = 4 * x_ + 2 * y_ + c_
    def landing(block):
        return lax.dynamic_update_index_in_dim(lax.empty((N_DEV, *block.shape), block.dtype), block, me, 0)

    land_in, land_out, land_1, land_2, land_pool = [landing(s.astype(BF16)) for s in shards]
    land_scale = landing(pool_scale.reshape(n_groups, pr))

    def sds(shape, dtype=F32):
        return jax.ShapeDtypeStruct(shape, dtype)

    vec = pl.BlockSpec((1, d), lambda m, n, k: (0, 0))
    row_ep = pl.BlockSpec((tm_ep, d), lambda m, n, k: (m, 0))
    tm_res = min(t, 256)
    row_res = pl.BlockSpec((tm_res, d), lambda m, n, k: (m, 0))
    seq = ("arbitrary", "arbitrary", "arbitrary")

    two_level = _plan_gather_own([SIBLING, ACROSS_Y, ACROSS_X, DIAGONAL])
    forward = _plan_gather_forward(1)
    first_needed = [land_in, land_pool, land_scale]
    (h0, h0b), first_needed = _ln_in_fwd(
        x2, g0, b0, tm_big, comm=(first_needed, *_join_plans(*[(two_level[0], 1, two_level[1])] * 3)))
    win_g, wpool_g, scale_g = _copies_now("gather_forward_w_in", first_needed, *_plan_gather_forward(3))
    wp_full = wpool_g.reshape(N_DEV, n_groups, pr, c_pool).transpose(1, 0, 2, 3).reshape(n_groups, c_pool, c_pool)
    sc_full = scale_g.transpose(1, 0, 2).reshape(n_groups, 1, c_pool)

    pool_shards = p // ws_in

    def mm_u(name, first, count, dtype, comm=None):
        return _matmul(
            name, h0b, win_g, dims=NN, grid=(t // tm_big, count, 1),
            a_spec=pl.BlockSpec((tm_big, d), lambda m, n, k: (m, 0)),
            b_spec=pl.BlockSpec((None, d, ws_in), lambda m, n, k: (n + first, 0, 0)),
            out_shape=[sds((t, count * ws_in), dtype)],
            out_specs=[pl.BlockSpec((tm_big, ws_in), lambda m, n, k: (m, n))],
            acc_shape=(tm_big, ws_in), epilogue=_store_epilogue(dtype), comm=comm)

    half = land_1.shape[1] // 2
    diag_a, diag_b = _plan_gather_own([DIAGONAL], (0, half)), _plan_gather_own([DIAGONAL], (half, half))
    (u_pool,), (w1_diag,) = mm_u("mm_u_pool", 0, pool_shards, F32, comm=([land_1], *diag_a))
    (qkv,), (wout_part,) = mm_u("mm_u_qkv", pool_shards, N_DEV - pool_shards, BF16, comm=([land_out], *two_level))

    (y_pool, ypre), (wout_g, w1_diag) = _pool_fwd(
        u_pool, wp_full, sc_full, t, c_pool,
        comm=([wout_part, w1_diag], *_join_plans((forward[0], 1, forward[1]), (diag_b[0], 1, diag_b[1]))))
    (o,), (w1_part,) = _attn_fwd(qkv, t, n_heads, comm=([w1_diag], *_plan_gather_own([SIBLING, ACROSS_Y, ACROSS_X])))
    mixin = jnp.concatenate([y_pool, o.astype(BF16)], axis=1)
    wout_2d = wout_g.reshape(d, d)

    def ep_ln1(acc, ex, outs):
        h0_ref, g_ref, b_ref = ex
        r1 = DEEPNORM_ALPHA * h0_ref[...] + acc
        xhat, _ = _ln_stats(r1)
        h1 = xhat * g_ref[...] + b_ref[...]
        outs[0][...] = r1
        outs[1][...] = h1
        outs[2][...] = h1.astype(BF16)

    (r1, h1, h1b), (w1_g,) = _matmul(
        "mm_mix_ln1", mixin, wout_2d, dims=NN, grid=(t // tm_res, 1, 1),
        a_spec=pl.BlockSpec((tm_res, d), lambda m, n, k: (m, 0)),
        b_spec=pl.BlockSpec((d, d), lambda m, n, k: (0, 0)),
        extras=(h0, ln1_g, ln1_b), extra_specs=(row_res, vec, vec),
        out_shape=[sds((t, d)), sds((t, d)), sds((t, d), BF16)], out_specs=[row_res] * 3,
        acc_shape=(tm_res, d), epilogue=ep_ln1, comm=([w1_part], *forward))

    def ep_ff1(acc, ex, outs):
        f1 = acc + ex[0][...]
        outs[0][...] = f1
        r = jnp.maximum(f1, 0.0)
        outs[1][...] = (r * r).astype(BF16)

    ff_tile = pl.BlockSpec((tm_big, half_f), lambda m, n, k: (m, n))
    (f1, act), (w2_part,) = _matmul(
        "mm_ff1", h1b, w1_g, dims=NN, grid=(t // tm_big, f // half_f, 1),
        a_spec=pl.BlockSpec((tm_big, d), lambda m, n, k: (m, 0)),
        b_spec=pl.BlockSpec((None, d, half_f), lambda m, n, k: (n // per_f, 0, n % per_f)),
        extras=(b_ff1,), extra_specs=(pl.BlockSpec((1, half_f), lambda m, n, k: (0, n)),),
        out_shape=[sds((t, f)), sds((t, f), BF16)], out_specs=[ff_tile, ff_tile],
        acc_shape=(tm_big, half_f), epilogue=ep_ff1, comm=([land_2], *two_level))
    (w2_g,) = _copies_now("gather_forward_w_ff2", [w2_part], *forward)

    def ep_ln2(acc, ex, outs):
        h1_ref, tgt_ref, bf2_ref, g_ref, b_ref = ex
        dr2_ref, dr2b_ref, dg_ref, db_ref, dbf2_ref, loss_ref = outs
        first = pl.program_id(0) == 0
        r2 = DEEPNORM_ALPHA * h1_ref[...] + (acc + bf2_ref[...])
        xhat, rstd = _ln_stats(r2)
        err = xhat * g_ref[...] + b_ref[...] - tgt_ref[...]
        dr2, dg, db = _ln_bwd(err * (1.0 / d), xhat, rstd, g_ref[...])
        dr2_ref[...] = dr2
        dr2b_ref[...] = dr2.astype(BF16)
        _acc_rows(first, dg_ref, dg)
        _acc_rows(first, db_ref, db)
        _acc_rows(first, dbf2_ref, jnp.sum(dr2, axis=0, keepdims=True))
        _acc_rows(first, loss_ref, jnp.sum(err * err, axis=0, keepdims=True) * (0.5 / d))

    dr2, dr2b, dg2, db2, dbf2, loss_vec = _matmul(
        "mm_ff2_ln2_loss", act, w2_g, dims=NN, grid=(t // tm_ep, 1, N_DEV),
        a_spec=pl.BlockSpec((tm_ep, ws_f), lambda m, n, k: (m, k)),
        b_spec=pl.BlockSpec((None, ws_f, d), lambda m, n, k: (k, 0, 0)),
        extras=(h1, target, b_ff2, ln2_g, ln2_b), extra_specs=(row_ep, row_ep, vec, vec, vec),
        out_shape=[sds((t, d)), sds((t, d), BF16)] + [sds((1, d))] * 4, out_specs=[row_ep, row_ep, vec, vec, vec, vec],
        acc_shape=(tm_ep, d), epilogue=ep_ln2, sem=seq)

    def ep_dff1(acc, ex, outs):
        df1 = acc * (2.0 * jnp.maximum(ex[0][...], 0.0))
        outs[0][...] = df1.astype(BF16)
        _acc_rows(pl.program_id(1) == 0, outs[1], jnp.sum(df1, axis=0, keepdims=True))

    df_tile = pl.BlockSpec((tm_big, ws_f), lambda n, m, k: (m, n))
    df1b, dbf1 = _matmul(
        "mm_dff1", dr2b, w2_g, dims=NT, grid=(N_DEV, t // tm_big, 1),
        a_spec=pl.BlockSpec((tm_big, d), lambda n, m, k: (m, 0)),
        b_spec=pl.BlockSpec((None, ws_f, d), lambda n, m, k: (n, 0, 0)),
        extras=(f1,), extra_specs=(df_tile,),
        out_shape=[sds((t, f), BF16), sds((1, f))], out_specs=[df_tile, pl.BlockSpec((1, ws_f), lambda n, m, k: (0, n))],
        acc_shape=(tm_big, ws_f), epilogue=ep_dff1, sem=("parallel", "arbitrary", "arbitrary"))

    tn_d = min(d, 1024)
    dw2 = _matmul(
        "mm_dw2", act, dr2b, dims=TN, grid=(N_DEV, d // tn_d, t // tkk),
        a_spec=pl.BlockSpec((tkk, ws_f), lambda m, n, k: (k, m)),
        b_spec=pl.BlockSpec((tkk, tn_d), lambda m, n, k: (k, n)),
        out_shape=[sds((N_DEV, ws_f, d))], out_specs=[pl.BlockSpec((None, ws_f, tn_d), lambda m, n, k: (m, 0, n))],
        acc_shape=(ws_f, tn_d), epilogue=_store_epilogue(F32))[0]

    dw1 = _matmul(
        "mm_dw1", h1b, df1b, dims=TN, grid=(d // tn_d, N_DEV, t // tkk),
        a_spec=pl.BlockSpec((tkk, tn_d), lambda m, n, k: (k, m)),
        b_spec=pl.BlockSpec((tkk, ws_f), lambda m, n, k: (k, n)),
        out_shape=[sds((N_DEV, d, ws_f))], out_specs=[pl.BlockSpec((None, tn_d, ws_f), lambda m, n, k: (n, m, 0))],
        acc_shape=(tn_d, ws_f), epilogue=_store_epilogue(F32))[0]

    def ep_ln1_bwd(acc, ex, outs):
        dr2_ref, r1_ref, g_ref = ex
        first = pl.program_id(0) == 0
        xhat, rstd = _ln_stats(r1_ref[...])
        dr1, dg, db = _ln_bwd(DEEPNORM_ALPHA * dr2_ref[...] + acc, xhat, rstd, g_ref[...])
        outs[0][...] = dr1
        outs[1][...] = dr1.astype(BF16)
        _acc_rows(first, outs[2], dg)
        _acc_rows(first, outs[3], db)

    slots = _owner_slots()

    def to_sibling(parts):
        return (parts + [lax.empty((4, *pt.shape[1:]), F32) for pt in parts], *_plan_rs_sibling(len(parts)))

    def to_owner(names_, parts, from_sib):
        sums = [_rs_chip_sum("rs_chip_sum_" + nm, slots, pt, fs) for nm, pt, fs in zip(names_, parts, from_sib)]
        return (sums + [lax.empty(cs.shape, BF16) for cs in sums], *_plan_rs_owner(len(sums)))

    (dr1, dr1b, dg1, db1), (dw1, dw2, sib_1, sib_2) = _matmul(
        "mm_dh1_ln1_bwd", df1b, w1_g, dims=NT, grid=(t // tm_ep, 1, N_DEV),
        a_spec=pl.BlockSpec((tm_ep, ws_f), lambda m, n, k: (m, k)),
        b_spec=pl.BlockSpec((None, d, ws_f), lambda m, n, k: (k, 0, 0)),
        extras=(dr2, r1, ln1_g), extra_specs=(row_ep, row_ep, vec),
        out_shape=[sds((t, d)), sds((t, d), BF16), sds((1, d)), sds((1, d))], out_specs=[row_ep, row_ep, vec, vec],
        acc_shape=(tm_ep, d), epilogue=ep_ln1_bwd, sem=seq,
        comm=to_sibling([dw1, dw2]))
    own_1 = to_owner(["w_ff1"], [dw1], [sib_1])
    own_2 = to_owner(["w_ff2"], [dw2], [sib_2])[0]

    dwout = _matmul(
        "mm_dwout", mixin, dr1b, dims=TN, grid=(d // tn_d, d // tn_d, t // tkk),
        a_spec=pl.BlockSpec((tkk, tn_d), lambda m, n, k: (k, m)),
        b_spec=pl.BlockSpec((tkk, tn_d), lambda m, n, k: (k, n)),
        out_shape=[sds((d, d))], out_specs=[pl.BlockSpec((tn_d, tn_d), lambda m, n, k: (m, n))],
        acc_shape=(tn_d, tn_d), epilogue=_store_epilogue(F32))[0].reshape(N_DEV, ws_out, d)

    tn_mix = min(tn_d, p, d - p)

    def mm_dmixin(name, first, width, dtype, comm=None):
        return _matmul(
            name, dr1b, wout_2d, dims=NT, grid=(t // tm_big, width // tn_mix, 1),
            a_spec=pl.BlockSpec((tm_big, d), lambda m, n, k: (m, 0)),
            b_spec=pl.BlockSpec((tn_mix, d), lambda m, n, k: (n + first // tn_mix, 0)),
            out_shape=[sds((t, width), dtype)], out_specs=[pl.BlockSpec((tm_big, tn_mix), lambda m, n, k: (m, n))],
            acc_shape=(tm_big, tn_mix), epilogue=_store_epilogue(dtype), comm=comm)

    (dy_pool,), (dwout, sib_out) = mm_dmixin("mm_dmixin_pool", 0, p, F32, comm=to_sibling([dwout]))
    (do,) = mm_dmixin("mm_dmixin_att", p, d - p, BF16)

    (du_pool, dwp, dsc), (_, chips_out) = _pool_bwd(
        dy_pool, ypre, wp_full, sc_full, t, c_pool, comm=to_owner(["w_out"], [dwout], [sib_out]))
    (dq, dk, dv), (_, chips_1) = _attn_bwd(qkv, do, o, t, n_heads, comm=own_1)
    dub = jnp.concatenate([du_pool.astype(BF16), dq, dk, dv], axis=1)

    (dwin,), own_2 = _matmul(
        "mm_dwin", h0b, dub, dims=TN, grid=(d // tn_d, N_DEV, t // tkk),
        a_spec=pl.BlockSpec((tkk, tn_d), lambda m, n, k: (k, m)),
        b_spec=pl.BlockSpec((tkk, ws_in), lambda m, n, k: (k, n)),
        out_shape=[sds((N_DEV, d, ws_in))], out_specs=[pl.BlockSpec((None, tn_d, ws_in), lambda m, n, k: (n, m, 0))],
        acc_shape=(tn_d, ws_in), epilogue=_store_epilogue(F32), comm=(own_2, *_plan_rs_owner(1, (1, 2))))
    dwp_g = dwp.reshape(n_groups, N_DEV, pr, c_pool).transpose(1, 0, 2, 3).reshape(N_DEV, n_groups * pr, c_pool)

    def ep_ln0_bwd(acc, ex, outs):
        dr1_ref, x_ref, g_ref = ex
        first = pl.program_id(0) == 0
        xhat, rstd = _ln_stats(x_ref[...])
        dx, dg, db = _ln_bwd(DEEPNORM_ALPHA * dr1_ref[...] + acc, xhat, rstd, g_ref[...])
        outs[0][...] = dx
        _acc_rows(first, outs[1], dg)
        _acc_rows(first, outs[2], db)

    sib_in = to_sibling([dwin, dwp_g])
    to_diagonal = _plan_rs_owner(1, (3,))
    last_host = (own_2 + sib_in[0], *_join_plans((to_diagonal[0], 2, to_diagonal[1]), (sib_in[1], 4, sib_in[2])))
    def two_blocks(a_ref, b_ref):
        return _dot(a_ref[:, :ws_in], b_ref[0], NT) + _dot(a_ref[:, ws_in:], b_ref[1], NT)

    (dx, dg0, db0), (_, chips_2, dwin, dwp_g, sib_in_, sib_p) = _matmul(
        "mm_dh0_ln0_bwd", dub, win_g, dims=NT, grid=(t // tm_ep, 1, N_DEV // 2),
        a_spec=pl.BlockSpec((tm_ep, 2 * ws_in), lambda m, n, k: (m, k)),
        b_spec=pl.BlockSpec((2, d, ws_in), lambda m, n, k: (k, 0, 0)),
        extras=(dr1, x2, g0), extra_specs=(row_ep, row_ep, vec),
        out_shape=[sds((t, d)), sds((1, d)), sds((1, d))], out_specs=[row_ep, vec, vec],
        acc_shape=(tm_ep, d), epilogue=ep_ln0_bwd, sem=seq,
        comm=last_host, dot_fn=two_blocks)

    _, _, chips_in, chips_p = _copies_now("rs_owner_w_in", *to_owner(["w_in", "w_pool"], [dwin, dwp_g], [sib_in_, sib_p]))
    w_of = {"w_in": shards[0], "w_out": shards[1], "w_ff1": shards[2], "w_ff2": shards[3], "w_pool": shards[4]}
    mv_of = {"w_in": (m_w_in, v_w_in), "w_out": (m_w_out, v_w_out), "w_ff1": (m_w_ff1, v_w_ff1),
             "w_ff2": (m_w_ff2, v_w_ff2), "w_pool": (m_w_pool, v_w_pool)}
    big = {}
    for nm, pt, fs, fc in [("w_ff1", dw1, sib_1, chips_1), ("w_ff2", dw2, sib_2, chips_2), ("w_out", dwout, sib_out, chips_out),
                           ("w_in", dwin, sib_in_, chips_in), ("w_pool", dwp_g, sib_p, chips_p)]:
        w2d = w_of[nm]
        m_, v_ = mv_of[nm]
        big[nm] = _rs_final_adamw("rs_final_adamw_" + nm, slots, pt, fs, fc, w2d, m_.reshape(w2d.shape), v_.reshape(w2d.shape))

    n_f_rows = f // d
    pad_sc = d - p
    packet = jnp.concatenate(
        [loss_vec, dg0, db0, dg1, db1, dbf2, dg2, db2, dbf1.reshape(n_f_rows, d),
         jnp.pad(dsc.reshape(1, p), ((0, 0), (0, pad_sc)))], axis=0)
    n_rows = packet.shape[0]
    n_pad = (-n_rows) % 8
    packet = jnp.pad(packet, ((0, n_pad), (0, 0)))
    sums, loss11 = _small_all_reduce(packet)
    dsc_full = sums[8 + n_f_rows, :p].reshape(n_groups, N_DEV, pr)
    dsc_mine = lax.dynamic_index_in_dim(dsc_full, me, axis=1, keepdims=False)

    def sc_row(a):
        return jnp.pad(a.reshape(1, n_groups * pr), ((0, 0), (0, d - n_groups * pr)))

    def small_pack(ln0g, ln0b, l1g, l1b, bf2, l2g, l2b, bf1, sc):
        rows = [jnp.zeros((1, d), F32), ln0g.reshape(1, d), ln0b.reshape(1, d), l1g, l1b, bf2, l2g, l2b,
                bf1.reshape(n_f_rows, d), sc_row(sc), jnp.zeros((n_pad, d), F32)]
        return jnp.concatenate(rows, axis=0)

    w_small = small_pack(ln_in_g, ln_in_b, ln1_g, ln1_b, b_ff2, ln2_g, ln2_b, b_ff1, pool_scale)
    m_small = small_pack(m_ln_in_g, m_ln_in_b, m_ln1_g, m_ln1_b, m_b_ff2, m_ln2_g, m_ln2_b, m_b_ff1, m_pool_scale)
    v_small = small_pack(v_ln_in_g, v_ln_in_b, v_ln1_g, v_ln1_b, v_b_ff2, v_ln2_g, v_ln2_b, v_b_ff1, v_pool_scale)
    g_small = jnp.concatenate([sums[:8 + n_f_rows], sc_row(dsc_mine), jnp.zeros((n_pad, d), F32)], axis=0)
    small = (g_small,) + tuple(_small_adamw(w_small, g_small, m_small, v_small))

    def unpack(a):
        sc = a[8 + n_f_rows, :n_groups * pr].reshape(1, n_groups, pr)
        return {"ln_in_g": a[1], "ln_in_b": a[2], "ln1_g": a[3:4], "ln1_b": a[4:5], "b_ff2": a[5:6], "ln2_g": a[6:7],
                "ln2_b": a[7:8], "b_ff1": a[8:8 + n_f_rows].reshape(1, f), "pool_scale": sc}

    shapes = {"w_in": w_in.shape, "w_out": w_out.shape, "w_ff1": w_ff1.shape, "w_ff2": w_ff2.shape, "w_pool": w_pool.shape}
    order = ["ln_in_g", "ln_in_b", "w_in", "w_pool", "pool_scale", "w_out", "ln1_g", "ln1_b", "w_ff1", "b_ff1", "w_ff2",
             "b_ff2", "ln2_g", "ln2_b"]
    outs = []
    for kind in range(4):
        small_k = unpack(small[kind])
        for nm in order:
            outs.append(big[nm][kind].reshape(shapes[nm]) if nm in big else small_k[nm])
    return (loss11.reshape(()), dx.reshape(x.shape), *outs)
```

```python
import functools
import math

import jax
import jax.numpy as jnp
from jax import lax
from jax.experimental import pallas as pl
from jax.experimental.pallas import tpu as pltpu

F32 = jnp.float32
BF16 = jnp.bfloat16
MESH = pl.DeviceIdType.MESH

N_DEV = 8
HEAD_DIM = 128
POOL_WINDOWS = (2, 4, 8, 16)
DEEPNORM_ALPHA = (2.0 * 1) ** 0.25
LN_EPS = 1e-5
ADAM_LR = 0.001
ADAM_B1 = 0.9
ADAM_B2 = 0.999
ADAM_EPS = 1e-08
ADAM_WD = 0.01
ADAM_STEP = 10

V7X_VMEM_LIMIT = 56 * 1024 * 1024
ATT_BLOCK = 256
POOL_CHUNK = 256

NN = (((1,), (0,)), ((), ()))
NT = (((1,), (1,)), ((), ()))
TN = (((0,), (0,)), ((), ()))


def _dot(a, b, dims=NN):
    return lax.dot_general(a, b, dims, preferred_element_type=F32)


def _cparams(sem=None):
    return pltpu.CompilerParams(dimension_semantics=sem, vmem_limit_bytes=V7X_VMEM_LIMIT)


def _ln_stats(r):
    mu = jnp.mean(r, axis=-1, keepdims=True)
    xc = r - mu
    var = jnp.mean(xc * xc, axis=-1, keepdims=True)
    rstd = lax.rsqrt(var + LN_EPS)
    return xc * rstd, rstd


def _ln_bwd(dy, xhat, rstd, g):
    dxh = dy * g
    m1 = jnp.mean(dxh, axis=-1, keepdims=True)
    m2 = jnp.mean(dxh * xhat, axis=-1, keepdims=True)
    dx = rstd * (dxh - m1 - xhat * m2)
    dg = jnp.sum(dy * xhat, axis=0, keepdims=True)
    db = jnp.sum(dy, axis=0, keepdims=True)
    return dx, dg, db


def _acc_rows(first, ref, val):
    @pl.when(first)
    def _():
        ref[...] = val

    @pl.when(jnp.logical_not(first))
    def _():
        ref[...] += val


def _call(body, *, name, grid, in_specs, out_specs, out_shape, inputs, scratch_shapes=(), sem=None, comm=None):
    in_specs, out_specs, out_shape, inputs = list(in_specs), list(out_specs), list(out_shape), list(inputs)
    if comm is None:
        outs = pl.pallas_call(
            body, name=name, grid=grid, in_specs=in_specs, out_specs=out_specs, out_shape=out_shape,
            scratch_shapes=list(scratch_shapes), compiler_params=_cparams(sem))(*inputs)
        return list(outs), []
    arrays, plan, n_copies = comm
    n_in, n_out, nc, n_scr = len(inputs), len(out_shape), len(arrays), len(scratch_shapes)

    def hosted(*refs):
        ins = refs[:n_in]
        outs = refs[n_in + nc:n_in + nc + n_out]
        passed = refs[n_in + nc + n_out:n_in + 2 * nc + n_out]
        scratch = refs[n_in + 2 * nc + n_out:n_in + 2 * nc + n_out + n_scr]
        send_sems, recv_sems = refs[-2], refs[-1]
        ids = [pl.program_id(ax) for ax in range(len(grid))]
        first = functools.reduce(jnp.logical_and, [i_ == 0 for i_ in ids])
        last = functools.reduce(jnp.logical_and, [i_ == g - 1 for i_, g in zip(ids, grid)])

        @pl.when(first)
        def _():
            for cp in _plan_copies(plan, passed, send_sems, recv_sems):
                cp.start()

        body(*ins, *outs, *scratch)

        @pl.when(last)
        def _():
            for cp in _plan_copies(plan, passed, send_sems, recv_sems):
                cp.wait_send()
                cp.wait_recv()

    any_spec = pl.BlockSpec(memory_space=pl.ANY)
    outs = pl.pallas_call(
        hosted, name=name, grid=grid,
        in_specs=in_specs + [any_spec] * nc, out_specs=out_specs + [any_spec] * nc,
        out_shape=out_shape + [jax.ShapeDtypeStruct(a.shape, a.dtype) for a in arrays],
        scratch_shapes=list(scratch_shapes) + [pltpu.SemaphoreType.DMA((n_copies,)), pltpu.SemaphoreType.DMA((n_copies,))],
        input_output_aliases={n_in + i: n_out + i for i in range(nc)},
        compiler_params=pltpu.CompilerParams(dimension_semantics=("arbitrary",) * len(grid),
                                             vmem_limit_bytes=V7X_VMEM_LIMIT, has_side_effects=True),
    )(*inputs, *arrays)
    return list(outs[:n_out]), list(outs[n_out:])


def _matmul(name, a, b, *, dims, grid, a_spec, b_spec, extras=(), extra_specs=(), out_shape, out_specs,
            acc_shape, epilogue, k_axis=2, sem=("parallel", "parallel", "arbitrary"), comm=None, dot_fn=None):
    nk = grid[k_axis]
    n_extra = len(extras)
    n_out = len(out_shape)
    if dot_fn is None:
        def dot_fn(a_ref, b_ref):
            return _dot(a_ref[...], b_ref[...], dims)

    def body(a_ref, b_ref, *rest):
        extra_refs = rest[:n_extra]
        out_refs = rest[n_extra:n_extra + n_out]
        if nk == 1:
            epilogue(dot_fn(a_ref, b_ref), extra_refs, out_refs)
            return
        acc_ref = rest[n_extra + n_out]
        k = pl.program_id(k_axis)

        @pl.when(k == 0)
        def _():
            acc_ref[...] = jnp.zeros(acc_shape, F32)

        acc_ref[...] += dot_fn(a_ref, b_ref)

        @pl.when(k == nk - 1)
        def _():
            epilogue(acc_ref[...], extra_refs, out_refs)

    outs, passed = _call(
        body, name=name, grid=grid, in_specs=[a_spec, b_spec, *extra_specs], out_specs=out_specs, out_shape=out_shape,
        inputs=[a, b, *extras], scratch_shapes=[] if nk == 1 else [pltpu.VMEM(acc_shape, F32)], sem=sem, comm=comm)
    return outs if comm is None else (outs, passed)


def _store_epilogue(dtype):

    def ep(acc, extra_refs, out_refs):
        out_refs[0][...] = acc.astype(dtype)
    return ep


def _ln_in_fwd(x, g, b, tm, comm=None):
    t, d = x.shape

    def body(x_ref, g_ref, b_ref, h_ref, hb_ref):
        xhat, _ = _ln_stats(x_ref[...])
        h = xhat * g_ref[...] + b_ref[...]
        h_ref[...] = h
        hb_ref[...] = h.astype(BF16)

    row = pl.BlockSpec((tm, d), lambda i: (i, 0))
    vec = pl.BlockSpec((1, d), lambda i: (0, 0))
    return _call(
        body, name="ln_in_fwd", grid=(t // tm,), in_specs=[row, vec, vec], out_specs=[row, row],
        out_shape=[jax.ShapeDtypeStruct((t, d), F32), jax.ShapeDtypeStruct((t, d), BF16)],
        inputs=[x, g, b], sem=("parallel",), comm=comm)


def _split3(x):
    hi = x.astype(BF16)
    r = x - hi.astype(F32)
    mid = r.astype(BF16)
    lo = (r - mid.astype(F32)).astype(BF16)
    return hi, mid, lo


def _split2(x):
    hi = x.astype(BF16)
    lo = (x - hi.astype(F32)).astype(BF16)
    return hi, lo


def _pool_fwd(u, wp, sc, t, c, comm=None):
    n_groups = len(POOL_WINDOWS)
    tc = POOL_CHUNK
    n_chunks = t // tc

    def body(u_ref, wp_ref, sc_ref, y_ref, ypre_ref, xp_ref):
        g = pl.program_id(0)
        xp_ref[pl.ds(0, tc), :] = jnp.zeros((tc, c), F32)
        xp_ref[pl.ds(tc, t), :] = u_ref[...]
        out_i = lax.broadcasted_iota(jnp.int32, (tc, 2 * tc), 0)
        in_j = lax.broadcasted_iota(jnp.int32, (tc, 2 * tc), 1)
        lag = tc + out_i - in_j
        t_in_chunk = lax.broadcasted_iota(jnp.int32, (tc, 1), 0)
        for gi, w in enumerate(POOL_WINDOWS):
            @pl.when(g == gi)
            def _(w=w):
                band = jnp.logical_and(lag >= 0, lag < w).astype(BF16)

                def chunk(ci, carry):
                    start = pl.multiple_of(ci * tc, tc)
                    win = xp_ref[pl.ds(start, 2 * tc), :]
                    hi, mid, lo = _split3(win)
                    wsum = _dot(band, hi) + _dot(band, mid) + _dot(band, lo)
                    cnt = jnp.minimum(ci * tc + t_in_chunk + 1, w).astype(F32)
                    ypre = wsum * (1.0 / cnt) - win[tc:, :]
                    ypre_b = ypre.astype(BF16)
                    y = _dot(ypre_b, wp_ref[...]) * sc_ref[...]
                    ypre_ref[pl.ds(start, tc), :] = ypre_b
                    y_ref[pl.ds(start, tc), :] = y.astype(BF16)
                    return carry

                lax.fori_loop(0, n_chunks, chunk, 0)

    col = pl.BlockSpec((t, c), lambda g: (0, g))
    return _call(
        body, name="pool_fwd", grid=(n_groups,),
        in_specs=[col, pl.BlockSpec((None, c, c), lambda g: (g, 0, 0)), pl.BlockSpec((None, 1, c), lambda g: (g, 0, 0))],
        out_specs=[col, col],
        out_shape=[jax.ShapeDtypeStruct((t, n_groups * c), BF16), jax.ShapeDtypeStruct((t, n_groups * c), BF16)],
        inputs=[u, wp, sc], scratch_shapes=[pltpu.VMEM((t + tc, c), F32)], sem=("parallel",), comm=comm)


def _pool_bwd(dmixin, ypre, wp, sc, t, c, comm=None):
    n_groups = len(POOL_WINDOWS)
    tc = POOL_CHUNK
    n_chunks = t // tc

    def body(dy_ref, ypre_ref, wp_ref, sc_ref, du_ref, dwp_ref, dsc_ref, zp_ref):
        g = pl.program_id(0)
        zp_ref[pl.ds(t, tc), :] = jnp.zeros((tc, c), F32)
        dwp_ref[...] = jnp.zeros((c, c), F32)
        dsc_ref[...] = jnp.zeros((1, c), F32)
        out_i = lax.broadcasted_iota(jnp.int32, (tc, 2 * tc), 0)
        in_j = lax.broadcasted_iota(jnp.int32, (tc, 2 * tc), 1)
        lead = in_j - out_i
        t_in_chunk = lax.broadcasted_iota(jnp.int32, (tc, 1), 0)
        for gi, w in enumerate(POOL_WINDOWS):
            @pl.when(g == gi)
            def _(w=w):
                band = jnp.logical_and(lead >= 0, lead < w).astype(BF16)

                def first(ci, carry):
                    start = pl.multiple_of(ci * tc, tc)
                    dy = dy_ref[pl.ds(start, tc), :]
                    yp = ypre_ref[pl.ds(start, tc), :]
                    ymm = _dot(yp, wp_ref[...])
                    dsc_ref[...] += jnp.sum(dy * ymm, axis=0, keepdims=True)
                    dys_b = (dy * sc_ref[...]).astype(BF16)
                    dwp_ref[...] += _dot(yp, dys_b, TN)
                    dyp = _dot(dys_b, wp_ref[...], NT)
                    cnt = jnp.minimum(ci * tc + t_in_chunk + 1, w).astype(F32)
                    zp_ref[pl.ds(start, tc), :] = dyp * (1.0 / cnt)
                    du_ref[pl.ds(start, tc), :] = -dyp
                    return carry

                lax.fori_loop(0, n_chunks, first, 0)

                def second(ci, carry):
                    start = pl.multiple_of(ci * tc, tc)
                    hi, mid, lo = _split3(zp_ref[pl.ds(start, 2 * tc), :])
                    du_ref[pl.ds(start, tc), :] += _dot(band, hi) + _dot(band, mid) + _dot(band, lo)
                    return carry

                lax.fori_loop(0, n_chunks, second, 0)

    col = pl.BlockSpec((t, c), lambda g: (0, g))
    return _call(
        body, name="pool_bwd", grid=(n_groups,),
        in_specs=[col, col, pl.BlockSpec((None, c, c), lambda g: (g, 0, 0)), pl.BlockSpec((None, 1, c), lambda g: (g, 0, 0))],
        out_specs=[col, pl.BlockSpec((None, c, c), lambda g: (g, 0, 0)), pl.BlockSpec((None, 1, c), lambda g: (g, 0, 0))],
        out_shape=[jax.ShapeDtypeStruct((t, n_groups * c), F32), jax.ShapeDtypeStruct((n_groups, c, c), F32),
                   jax.ShapeDtypeStruct((n_groups, 1, c), F32)],
        inputs=[dmixin, ypre, wp, sc], scratch_shapes=[pltpu.VMEM((t + tc, c), F32)], sem=("parallel",), comm=comm)


ROW_PARTS = 2


def _att_consts():
    b = ATT_BLOCK
    rp = b // ROW_PARTS
    row = lax.broadcasted_iota(jnp.int32, (b, b), 0)
    col = lax.broadcasted_iota(jnp.int32, (b, b), 1)
    tri = (row >= col).astype(BF16)
    prow = lax.broadcasted_iota(jnp.int32, (rp, b), 0)
    pcol = lax.broadcasted_iota(jnp.int32, (rp, b), 1)
    causal = [pcol < prow + r * rp for r in range(ROW_PARTS)]
    return tri, causal


def _suffix_sum(x, tri):
    hi, lo = _split2(x)
    return _dot(hi, tri) + _dot(lo, tri)


LOG2_E = 1.4426950408889634


def _att_scores(qb, kb, mask):
    z2 = _dot(qb, kb, NT) * (LOG2_E / math.sqrt(HEAD_DIM))
    sp2 = jnp.maximum(z2, 0.0) + jnp.log2(1.0 + jnp.exp2(-jnp.abs(z2)))
    return z2, sp2, (sp2 if mask is None else jnp.where(mask, sp2, 0.0))


HEADS_PER_STEP = 2
ATT_LANES = HEADS_PER_STEP * HEAD_DIM
FWD_HEADS_PER_STEP = 4


def _head_lanes(s):
    return slice(s * HEAD_DIM, (s + 1) * HEAD_DIM)


UNDERFLOW_LOG2 = 160.0


def _sweep_earlier_blocks(i, state, per_chain, block):
    def lowest(st):
        low = st[0]
        for k in range(per_chain, len(st), per_chain):
            low = jnp.minimum(low, st[k])
        return jnp.min(low)

    def more(c):
        return jnp.logical_and(c[0] < i, c[1] < UNDERFLOW_LOG2)

    def trip(c):
        st = block(i - 1 - c[0], c[2:])
        return (c[0] + 1, lowest(st)) + tuple(st)

    return lax.while_loop(more, trip, (jnp.int32(0), lowest(state)) + tuple(state))[2:]


def _attn_fwd(qkv, t, n_heads, comm=None):
    b = ATT_BLOCK
    nq = t // b
    heads_per_step = FWD_HEADS_PER_STEP if n_heads % FWD_HEADS_PER_STEP == 0 else HEADS_PER_STEP
    n_steps = n_heads // heads_per_step

    rp = b // ROW_PARTS
    chains = [(s, r) for s in range(heads_per_step) for r in range(ROW_PARTS)]
    no_mask = [None] * ROW_PARTS

    def body(q_ref, k_ref, v_ref, o_ref):
        tri, causal = _att_consts()

        def blocks(qbs, j, state, masks):
            ks = pl.multiple_of(j * b, b)
            scores = [_att_scores(qbs[ci], k_ref[pl.ds(ks, b), _head_lanes(s)], masks[r]) for ci, (s, r) in enumerate(chains)]
            incls = [_suffix_sum(sc[2], tri) for sc in scores]
            out = []
            for ci, (s, r) in enumerate(chains):
                carry, acc = state[2 * ci], state[2 * ci + 1]
                a = jnp.exp2(scores[ci][0] - (incls[ci] + carry))
                if masks[r] is not None:
                    a = jnp.where(masks[r], a, 0.0)
                out += [carry + incls[ci][:, 0:1], acc + _dot(a.astype(BF16), v_ref[pl.ds(ks, b), _head_lanes(s)])]
            return tuple(out)

        def q_loop(i, _):
            qs = pl.multiple_of(i * b, b)
            qbs = [q_ref[pl.ds(qs + r * rp, rp), _head_lanes(s)] for s, r in chains]
            zero = (jnp.zeros((rp, 1), F32), jnp.zeros((rp, HEAD_DIM), F32)) * len(chains)
            state = blocks(qbs, i, zero, causal)
            state = _sweep_earlier_blocks(i, state, 2, lambda j, st: blocks(qbs, j, st, no_mask))
            for ci, (s, r) in enumerate(chains):
                o_ref[pl.ds(qs + r * rp, rp), _head_lanes(s)] = state[2 * ci + 1]
            return 0

        lax.fori_loop(0, nq, q_loop, 0)

    def heads(off):
        return pl.BlockSpec((t, heads_per_step * HEAD_DIM), lambda h: (0, off + h))

    return _call(
        body, name="attn_fwd", grid=(n_steps,),
        in_specs=[heads(0), heads(n_steps), heads(2 * n_steps)], out_specs=[heads(0)],
        out_shape=[jax.ShapeDtypeStruct((t, n_heads * HEAD_DIM), F32)],
        inputs=[qkv, qkv, qkv], sem=("parallel",), comm=comm)


def _attn_bwd(qkv, do, o, t, n_heads, comm=None):
    b = ATT_BLOCK
    nq = t // b
    n_steps = n_heads // HEADS_PER_STEP
    scale = 1.0 / math.sqrt(HEAD_DIM)
    rp = b // ROW_PARTS
    chains = [(s, r) for s in range(HEADS_PER_STEP) for r in range(ROW_PARTS)]
    no_mask = [None] * ROW_PARTS

    def body(q_ref, k_ref, v_ref, do_ref, o_ref, dq_ref, dk_ref, dv_ref, qt_ref, dot_ref, dkt_ref, dvt_ref):
        for j in range(nq):
            rows = pl.ds(j * b, b)
            qt_ref[j] = q_ref[rows, :].astype(F32).T.astype(BF16)
            dot_ref[j] = do_ref[rows, :].astype(F32).T.astype(BF16)
        dkt_ref[...] = jnp.zeros((nq, ATT_LANES, b), F32)
        dvt_ref[...] = jnp.zeros((nq, ATT_LANES, b), F32)
        tri, causal = _att_consts()

        def blocks(i, fixed, j, state, masks):
            ks = pl.multiple_of(j * b, b)
            n = len(chains)
            kbs = [k_ref[pl.ds(ks, b), _head_lanes(s)] for s, _ in chains]
            scores = [_att_scores(fixed[ci][0], kbs[ci], masks[r]) for ci, (s, r) in enumerate(chains)]
            incls = [_suffix_sum(sc[2], tri) for sc in scores]
            das = [_dot(fixed[ci][1], v_ref[pl.ds(ks, b), _head_lanes(s)], NT) for ci, (s, r) in enumerate(chains)]
            a_bs, gs = [], []
            for ci, (s, r) in enumerate(chains):
                a = jnp.exp2(scores[ci][0] - (incls[ci] + state[3 * ci]))
                if masks[r] is not None:
                    a = jnp.where(masks[r], a, 0.0)
                a_bs.append(a.astype(BF16))
                gs.append(a_bs[ci].astype(F32) * das[ci])
            g_incls = [_suffix_sum(g, tri) for g in gs]
            dz_bs = []
            for ci, (s, r) in enumerate(chains):
                rest = (fixed[ci][2] - state[3 * ci + 1]) - (g_incls[ci] - gs[ci])
                sig = jnp.exp2(scores[ci][0] - scores[ci][1])
                dz = (gs[ci] - sig * rest) * scale
                if masks[r] is not None:
                    dz = jnp.where(masks[r], dz, 0.0)
                dz_bs.append(dz.astype(BF16))
            out = []
            for ci in range(n):
                out += [state[3 * ci] + incls[ci][:, 0:1], state[3 * ci + 1] + g_incls[ci][:, 0:1],
                        state[3 * ci + 2] + _dot(dz_bs[ci], kbs[ci])]
            for s in range(HEADS_PER_STEP):
                lanes = _head_lanes(s)
                dk_add, dv_add = None, None
                for ci, (cs, r) in enumerate(chains):
                    if cs == s:
                        part = slice(r * rp, (r + 1) * rp)
                        dk_c = _dot(qt_ref[i, lanes, part], dz_bs[ci])
                        dv_c = _dot(dot_ref[i, lanes, part], a_bs[ci])
                        dk_add = dk_c if dk_add is None else dk_add + dk_c
                        dv_add = dv_c if dv_add is None else dv_add + dv_c
                dkt_ref[j, lanes, :] += dk_add
                dvt_ref[j, lanes, :] += dv_add
            return tuple(out)

        def q_loop(i, _):
            qs = pl.multiple_of(i * b, b)
            fixed = []
            for s, r in chains:
                rows = pl.ds(qs + r * rp, rp)
                dob = do_ref[rows, _head_lanes(s)]
                total = jnp.sum(dob.astype(F32) * o_ref[rows, _head_lanes(s)], axis=-1, keepdims=True)
                fixed.append((q_ref[rows, _head_lanes(s)], dob, total))
            zero = (jnp.zeros((rp, 1), F32), jnp.zeros((rp, 1), F32), jnp.zeros((rp, HEAD_DIM), F32)) * len(chains)
            state = blocks(i, fixed, i, zero, causal)
            state = _sweep_earlier_blocks(i, state, 3, lambda j, st: blocks(i, fixed, j, st, no_mask))
            for ci, (s, r) in enumerate(chains):
                dq_ref[pl.ds(qs + r * rp, rp), _head_lanes(s)] = state[3 * ci + 2].astype(BF16)
            return 0

        lax.fori_loop(0, nq, q_loop, 0)
        for j in range(nq):
            rows = pl.ds(j * b, b)
            dk_ref[rows, :] = dkt_ref[j].T.astype(BF16)
            dv_ref[rows, :] = dvt_ref[j].T.astype(BF16)

    def heads(off):
        return pl.BlockSpec((t, ATT_LANES), lambda h: (0, off + h))

    shape = jax.ShapeDtypeStruct((t, n_heads * HEAD_DIM), BF16)
    return _call(
        body, name="attn_bwd", grid=(n_steps,),
        in_specs=[heads(0), heads(n_steps), heads(2 * n_steps), heads(0), heads(0)],
        out_specs=[heads(0)] * 3, out_shape=[shape] * 3, inputs=[qkv, qkv, qkv, do, o],
        scratch_shapes=[pltpu.VMEM((nq, ATT_LANES, b), BF16)] * 2 + [pltpu.VMEM((nq, ATT_LANES, b), F32)] * 2,
        sem=("parallel",), comm=comm)


def _place():
    x, y, c = lax.axis_index("x"), lax.axis_index("y"), lax.axis_index("c")
    return x, y, c


def _flip(v, on):
    return 1 - v if on else v


def _plan_copies(plan, refs, send_sems, recv_sems):
    return [pltpu.make_async_remote_copy(src_ref=src, dst_ref=dst, send_sem=send_sems.at[k], recv_sem=recv_sems.at[k],
                                         device_id=dev, device_id_type=MESH)
            for k, (src, dst, dev) in enumerate(plan(refs))]


def _copies_now(name, arrays, plan, n_copies):
    n = len(arrays)

    def body(*refs):
        copies = _plan_copies(plan, refs[n:2 * n], refs[2 * n], refs[2 * n + 1])
        for cp in copies:
            cp.start()
        for cp in copies:
            cp.wait_send()
            cp.wait_recv()

    any_spec = pl.BlockSpec(memory_space=pl.ANY)
    return list(pl.pallas_call(
        body, name=name, in_specs=[any_spec] * n, out_specs=[any_spec] * n,
        out_shape=[jax.ShapeDtypeStruct(a.shape, a.dtype) for a in arrays],
        input_output_aliases={i: i for i in range(n)},
        scratch_shapes=[pltpu.SemaphoreType.DMA((n_copies,)), pltpu.SemaphoreType.DMA((n_copies,))],
        compiler_params=pltpu.CompilerParams(has_side_effects=True),
    )(*arrays))


SIBLING, ACROSS_Y, ACROSS_X, DIAGONAL = 1, 2, 4, 6


def _plan_gather_own(peers, rows=None):
    def plan(refs):
        x, y, c = _place()
        mine = refs[0].at[4 * x + 2 * y + c]
        if rows is not None:
            mine = mine.at[pl.ds(*rows)]
        return [(mine, mine, (_flip(x, k & 4), _flip(y, k & 2), _flip(c, k & 1))) for k in peers]
    return plan, len(peers)


def _plan_gather_forward(n):
    def plan(refs):
        x, y, c = _place()
        out = []
        for ti in range(n):
            for r in range(1, 4):
                blk = refs[ti].at[4 * _flip(x, r & 2) + 2 * _flip(y, r & 1) + c]
                out.append((blk, blk, (x, y, 1 - c)))
        return out
    return plan, 3 * n


def _join_plans(*parts):
    def plan(refs):
        out, at = [], 0
        for part, n_arrays, _ in parts:
            out += part(refs[at:at + n_arrays])
            at += n_arrays
        return out
    return plan, sum(n_cp for _, _, n_cp in parts)


def _plan_rs_sibling(n):
    def plan(refs):
        x, y, c = _place()
        out = []
        for ti in range(n):
            for r in range(4):
                src = refs[ti].at[4 * _flip(x, r & 2) + 2 * _flip(y, r & 1) + (1 - c)]
                out.append((src, refs[n + ti].at[r], (x, y, 1 - c)))
        return out
    return plan, 4 * n


def _plan_rs_owner(n, relations=(1, 2, 3)):
    def plan(refs):
        x, y, c = _place()
        out = []
        for ti in range(n):
            for r in relations:
                out.append((refs[ti].at[r], refs[n + ti].at[r], (_flip(x, r & 2), _flip(y, r & 1), c)))
        return out
    return plan, len(relations) * n


def _owner_slots():
    x, y, c = _place()
    idx = []
    for r in range(4):
        ox, oy = (1 - x if r & 2 else x), (1 - y if r & 1 else y)
        idx.append(4 * ox + 2 * oy + c)
    return jnp.stack(idx).astype(jnp.int32)


def _row_tile(rows, cols):
    tr = max(8, min(rows, (1 << 19) // cols))
    while rows % tr:
        tr //= 2
    return tr


def _rs_chip_sum(name, slots, partial, from_sibling):
    _, rows, cols = partial.shape
    tr = _row_tile(rows, cols)

    def body(slots_ref, p_ref, s_ref, o_ref):
        o_ref[...] = (p_ref[...] + s_ref[...]).astype(BF16)

    grid_spec = pltpu.PrefetchScalarGridSpec(
        num_scalar_prefetch=1, grid=(3, rows // tr),
        in_specs=[pl.BlockSpec((None, tr, cols), lambda r, i, s: (s[r + 1], i, 0)),
                  pl.BlockSpec((None, tr, cols), lambda r, i, s: (r + 1, i, 0))],
        out_specs=pl.BlockSpec((None, tr, cols), lambda r, i, s: (r + 1, i, 0)))
    return pl.pallas_call(
        body, name=name, grid_spec=grid_spec, out_shape=jax.ShapeDtypeStruct((4, rows, cols), BF16),
        compiler_params=_cparams(("parallel", "parallel")),
    )(slots, partial, from_sibling)


def _adamw(w, g, m, v):
    m = ADAM_B1 * m + (1.0 - ADAM_B1) * g
    v = ADAM_B2 * v + (1.0 - ADAM_B2) * (g * g)
    m_hat = m / (1.0 - ADAM_B1 ** ADAM_STEP)
    v_hat = v / (1.0 - ADAM_B2 ** ADAM_STEP)
    delta = -ADAM_LR * (m_hat / (jnp.sqrt(v_hat) + ADAM_EPS) + ADAM_WD * w)
    return delta, m, v


def _rs_final_adamw(name, slots, partial, from_sibling, from_chips, w, m, v):
    rows, cols = w.shape
    tr = _row_tile(rows, cols)

    def body(slots_ref, p_ref, s_ref, c1_ref, c2_ref, c3_ref, w_ref, m_ref, v_ref, g_ref, d_ref, nm_ref, nv_ref):
        g = p_ref[...] + s_ref[...]
        g = g + c1_ref[...].astype(F32)
        g = g + c2_ref[...].astype(F32)
        g = g + c3_ref[...].astype(F32)
        delta, nm, nv = _adamw(w_ref[...], g, m_ref[...], v_ref[...])
        g_ref[...] = g
        d_ref[...] = delta
        nm_ref[...] = nm
        nv_ref[...] = nv

    def slot(r):
        return pl.BlockSpec((None, tr, cols), lambda i, s: (r, i, 0))

    flat = pl.BlockSpec((tr, cols), lambda i, s: (i, 0))
    grid_spec = pltpu.PrefetchScalarGridSpec(
        num_scalar_prefetch=1, grid=(rows // tr,),
        in_specs=[pl.BlockSpec((None, tr, cols), lambda i, s: (s[0], i, 0)), slot(0), slot(1), slot(2), slot(3), flat, flat, flat],
        out_specs=[flat] * 4)
    return pl.pallas_call(
        body, name=name, grid_spec=grid_spec, out_shape=[jax.ShapeDtypeStruct((rows, cols), F32)] * 4,
        compiler_params=_cparams(("parallel",)),
    )(slots, partial, from_sibling, from_chips, from_chips, from_chips, w, m, v)


def _small_all_reduce(packet):
    rows, d = packet.shape

    def body(p_ref, sum_ref, loss_ref, all_ref, send_sems, recv_sems):
        x, y, c = _place()
        me = 4 * x + 2 * y + c
        all_ref[me] = p_ref[...]
        copies = []
        for k in range(1, N_DEV):
            px, py, pc = (1 - x if k & 4 else x), (1 - y if k & 2 else y), (1 - c if k & 1 else c)
            cp = pltpu.make_async_remote_copy(
                src_ref=p_ref, dst_ref=all_ref.at[me], send_sem=send_sems.at[k], recv_sem=recv_sems.at[k],
                device_id=(px, py, pc), device_id_type=MESH)
            cp.start()
            copies.append(cp)
        for cp in copies:
            cp.wait_recv()
        for cp in copies:
            cp.wait_send()
        total = all_ref[0]
        for j in range(1, N_DEV):
            total = total + all_ref[j]
        sum_ref[...] = total
        loss_ref[...] = jnp.sum(total[0:1, :], axis=-1, keepdims=True)

    vmem = pl.BlockSpec(memory_space=pltpu.VMEM)
    return pl.pallas_call(
        body, name="small_all_reduce",
        in_specs=[vmem], out_specs=[vmem, vmem],
        out_shape=[jax.ShapeDtypeStruct((rows, d), F32), jax.ShapeDtypeStruct((1, 1), F32)],
        scratch_shapes=[pltpu.VMEM((N_DEV, rows, d), F32), pltpu.SemaphoreType.DMA((N_DEV,)), pltpu.SemaphoreType.DMA((N_DEV,))],
        compiler_params=pltpu.CompilerParams(has_side_effects=True),
    )(packet)


def _small_adamw(w, g, m, v):
    def body(w_ref, g_ref, m_ref, v_ref, d_ref, nm_ref, nv_ref):
        delta, nm, nv = _adamw(w_ref[...], g_ref[...], m_ref[...], v_ref[...])
        d_ref[...] = delta
        nm_ref[...] = nm
        nv_ref[...] = nv

    vmem = pl.BlockSpec(memory_space=pltpu.VMEM)
    return pl.pallas_call(
        body, name="small_adamw", in_specs=[vmem] * 4, out_specs=[vmem] * 3,
        out_shape=[jax.ShapeDtypeStruct(w.shape, F32)] * 3,
    )(w, g, m, v)


def kernel(x, ln_in_g, ln_in_b, w_in, w_pool, pool_scale, w_out, ln1_g, ln1_b, w_ff1, b_ff1, w_ff2, b_ff2, ln2_g, ln2_b, loss_target, m_ln_in_g, m_ln_in_b, m_w_in, m_w_pool, m_pool_scale, m_w_out, m_ln1_g, m_ln1_b, m_w_ff1, m_b_ff1, m_w_ff2, m_b_ff2, m_ln2_g, m_ln2_b, v_ln_in_g, v_ln_in_b, v_w_in, v_w_pool, v_pool_scale, v_w_out, v_ln1_g, v_ln1_b, v_w_ff1, v_b_ff1, v_w_ff2, v_b_ff2, v_ln2_g, v_ln2_b):
    t, d = x.shape[1], x.shape[2]
    n_groups = len(POOL_WINDOWS)
    c_pool = w_pool.shape[3]
    p = n_groups * c_pool
    n_heads = (d - p) // HEAD_DIM
    ws_in = w_in.shape[2]
    n_in = N_DEV * ws_in
    ws_out = w_out.shape[1]
    ws_f = w_ff1.shape[2]
    f = N_DEV * ws_f
    pr = w_pool.shape[2]
    assert n_in == p + 3 * n_heads * HEAD_DIM and N_DEV * ws_out == d and N_DEV * pr == c_pool

    tm_big = min(t, 1024)
    tm_ep = min(t, 512)
    tkk = min(t, 4096)
    half_f = min(ws_f, 512)
    per_f = ws_f // half_f

    x2 = x.reshape(t, d)
    target = loss_target.reshape(t, d)
    g0, b0 = ln_in_g.reshape(1, d), ln_in_b.reshape(1, d)

    shards = [w_in.reshape(d, ws_in), w_out.reshape(ws_out, d), w_ff1.reshape(d, ws_f), w_ff2.reshape(ws_f, d),
              w_pool.reshape(n_groups * pr, c_pool)]
    x_, y_, c_ = _place()
    me = 4 * x_ + 2 * y_ + c_
    def landing(block):
        return lax.dynamic_update_index_in_dim(lax.empty((N_DEV, *block.shape), block.dtype), block, me, 0)

    land_in, land_out, land_1, land_2, land_pool = [landing(s.astype(BF16)) for s in shards]
    land_scale = landing(pool_scale.reshape(n_groups, pr))

    def sds(shape, dtype=F32):
        return jax.ShapeDtypeStruct(shape, dtype)

    vec = pl.BlockSpec((1, d), lambda m, n, k: (0, 0))
    row_ep = pl.BlockSpec((tm_ep, d), lambda m, n, k: (m, 0))
    tm_res = min(t, 256)
    row_res = pl.BlockSpec((tm_res, d), lambda m, n, k: (m, 0))
    seq = ("arbitrary", "arbitrary", "arbitrary")

    two_level = _plan_gather_own([SIBLING, ACROSS_Y, ACROSS_X, DIAGONAL])
    forward = _plan_gather_forward(1)
    first_needed = [land_in, land_pool, land_scale]
    (h0, h0b), first_needed = _ln_in_fwd(
        x2, g0, b0, tm_big, comm=(first_needed, *_join_plans(*[(two_level[0], 1, two_level[1])] * 3)))
    win_g, wpool_g, scale_g = _copies_now("gather_forward_w_in", first_needed, *_plan_gather_forward(3))
    wp_full = wpool_g.reshape(N_DEV, n_groups, pr, c_pool).transpose(1, 0, 2, 3).reshape(n_groups, c_pool, c_pool)
    sc_full = scale_g.transpose(1, 0, 2).reshape(n_groups, 1, c_pool)

    pool_shards = p // ws_in

    def mm_u(name, first, count, dtype, comm=None):
        return _matmul(
            name, h0b, win_g, dims=NN, grid=(t // tm_big, count, 1),
            a_spec=pl.BlockSpec((tm_big, d), lambda m, n, k: (m, 0)),
            b_spec=pl.BlockSpec((None, d, ws_in), lambda m, n, k: (n + first, 0, 0)),
            out_shape=[sds((t, count * ws_in), dtype)],
            out_specs=[pl.BlockSpec((tm_big, ws_in), lambda m, n, k: (m, n))],
            acc_shape=(tm_big, ws_in), epilogue=_store_epilogue(dtype), comm=comm)

    half = land_1.shape[1] // 2
    diag_a, diag_b = _plan_gather_own([DIAGONAL], (0, half)), _plan_gather_own([DIAGONAL], (half, half))
    (u_pool,), (w1_diag,) = mm_u("mm_u_pool", 0, pool_shards, F32, comm=([land_1], *diag_a))
    (qkv,), (wout_part,) = mm_u("mm_u_qkv", pool_shards, N_DEV - pool_shards, BF16, comm=([land_out], *two_level))

    (y_pool, ypre), (wout_g, w1_diag) = _pool_fwd(
        u_pool, wp_full, sc_full, t, c_pool,
        comm=([wout_part, w1_diag], *_join_plans((forward[0], 1, forward[1]), (diag_b[0], 1, diag_b[1]))))
    (o,), (w1_part,) = _attn_fwd(qkv, t, n_heads, comm=([w1_diag], *_plan_gather_own([SIBLING, ACROSS_Y, ACROSS_X])))
    mixin = jnp.concatenate([y_pool, o.astype(BF16)], axis=1)
    wout_2d = wout_g.reshape(d, d)

    def ep_ln1(acc, ex, outs):
        h0_ref, g_ref, b_ref = ex
        r1 = DEEPNORM_ALPHA * h0_ref[...] + acc
        xhat, _ = _ln_stats(r1)
        h1 = xhat * g_ref[...] + b_ref[...]
        outs[0][...] = r1
        outs[1][...] = h1
        outs[2][...] = h1.astype(BF16)

    (r1, h1, h1b), (w1_g,) = _matmul(
        "mm_mix_ln1", mixin, wout_2d, dims=NN, grid=(t // tm_res, 1, 1),
        a_spec=pl.BlockSpec((tm_res, d), lambda m, n, k: (m, 0)),
        b_spec=pl.BlockSpec((d, d), lambda m, n, k: (0, 0)),
        extras=(h0, ln1_g, ln1_b), extra_specs=(row_res, vec, vec),
        out_shape=[sds((t, d)), sds((t, d)), sds((t, d), BF16)], out_specs=[row_res] * 3,
        acc_shape=(tm_res, d), epilogue=ep_ln1, comm=([w1_part], *forward))

    def ep_ff1(acc, ex, outs):
        f1 = acc + ex[0][...]
        outs[0][...] = f1
        r = jnp.maximum(f1, 0.0)
        outs[1][...] = (r * r).astype(BF16)

    ff_tile = pl.BlockSpec((tm_big, half_f), lambda m, n, k: (m, n))
    (f1, act), (w2_part,) = _matmul(
        "mm_ff1", h1b, w1_g, dims=NN, grid=(t // tm_big, f // half_f, 1),
        a_spec=pl.BlockSpec((tm_big, d), lambda m, n, k: (m, 0)),
        b_spec=pl.BlockSpec((None, d, half_f), lambda m, n, k: (n // per_f, 0, n % per_f)),
        extras=(b_ff1,), extra_specs=(pl.BlockSpec((1, half_f), lambda m, n, k: (0, n)),),
        out_shape=[sds((t, f)), sds((t, f), BF16)], out_specs=[ff_tile, ff_tile],
        acc_shape=(tm_big, half_f), epilogue=ep_ff1, comm=([land_2], *two_level))
    (w2_g,) = _copies_now("gather_forward_w_ff2", [w2_part], *forward)

    def ep_ln2(acc, ex, outs):
        h1_ref, tgt_ref, bf2_ref, g_ref, b_ref = ex
        dr2_ref, dr2b_ref, dg_ref, db_ref, dbf2_ref, loss_ref = outs
        first = pl.program_id(0) == 0
        r2 = DEEPNORM_ALPHA * h1_ref[...] + (acc + bf2_ref[...])
        xhat, rstd = _ln_stats(r2)
        err = xhat * g_ref[...] + b_ref[...] - tgt_ref[...]
        dr2, dg, db = _ln_bwd(err * (1.0 / d), xhat, rstd, g_ref[...])
        dr2_ref[...] = dr2
        dr2b_ref[...] = dr2.astype(BF16)
        _acc_rows(first, dg_ref, dg)
        _acc_rows(first, db_ref, db)
        _acc_rows(first, dbf2_ref, jnp.sum(dr2, axis=0, keepdims=True))
        _acc_rows(first, loss_ref, jnp.sum(err * err, axis=0, keepdims=True) * (0.5 / d))

    dr2, dr2b, dg2, db2, dbf2, loss_vec = _matmul(
        "mm_ff2_ln2_loss", act, w2_g, dims=NN, grid=(t // tm_ep, 1, N_DEV),
        a_spec=pl.BlockSpec((tm_ep, ws_f), lambda m, n, k: (m, k)),
        b_spec=pl.BlockSpec((None, ws_f, d), lambda m, n, k: (k, 0, 0)),
        extras=(h1, target, b_ff2, ln2_g, ln2_b), extra_specs=(row_ep, row_ep, vec, vec, vec),
        out_shape=[sds((t, d)), sds((t, d), BF16)] + [sds((1, d))] * 4, out_specs=[row_ep, row_ep, vec, vec, vec, vec],
        acc_shape=(tm_ep, d), epilogue=ep_ln2, sem=seq)

    def ep_dff1(acc, ex, outs):
        df1 = acc * (2.0 * jnp.maximum(ex[0][...], 0.0))
        outs[0][...] = df1.astype(BF16)
        _acc_rows(pl.program_id(1) == 0, outs[1], jnp.sum(df1, axis=0, keepdims=True))

    df_tile = pl.BlockSpec((tm_big, ws_f), lambda n, m, k: (m, n))
    df1b, dbf1 = _matmul(
        "mm_dff1", dr2b, w2_g, dims=NT, grid=(N_DEV, t // tm_big, 1),
        a_spec=pl.BlockSpec((tm_big, d), lambda n, m, k: (m, 0)),
        b_spec=pl.BlockSpec((None, ws_f, d), lambda n, m, k: (n, 0, 0)),
        extras=(f1,), extra_specs=(df_tile,),
        out_shape=[sds((t, f), BF16), sds((1, f))], out_specs=[df_tile, pl.BlockSpec((1, ws_f), lambda n, m, k: (0, n))],
        acc_shape=(tm_big, ws_f), epilogue=ep_dff1, sem=("parallel", "arbitrary", "arbitrary"))

    tn_d = min(d, 1024)
    dw2 = _matmul(
        "mm_dw2", act, dr2b, dims=TN, grid=(N_DEV, d // tn_d, t // tkk),
        a_spec=pl.BlockSpec((tkk, ws_f), lambda m, n, k: (k, m)),
        b_spec=pl.BlockSpec((tkk, tn_d), lambda m, n, k: (k, n)),
        out_shape=[sds((N_DEV, ws_f, d))], out_specs=[pl.BlockSpec((None, ws_f, tn_d), lambda m, n, k: (m, 0, n))],
        acc_shape=(ws_f, tn_d), epilogue=_store_epilogue(F32))[0]

    dw1 = _matmul(
        "mm_dw1", h1b, df1b, dims=TN, grid=(d // tn_d, N_DEV, t // tkk),
        a_spec=pl.BlockSpec((tkk, tn_d), lambda m, n, k: (k, m)),
        b_spec=pl.BlockSpec((tkk, ws_f), lambda m, n, k: (k, n)),
        out_shape=[sds((N_DEV, d, ws_f))], out_specs=[pl.BlockSpec((None, tn_d, ws_f), lambda m, n, k: (n, m, 0))],
        acc_shape=(tn_d, ws_f), epilogue=_store_epilogue(F32))[0]

    def ep_ln1_bwd(acc, ex, outs):
        dr2_ref, r1_ref, g_ref = ex
        first = pl.program_id(0) == 0
        xhat, rstd = _ln_stats(r1_ref[...])
        dr1, dg, db = _ln_bwd(DEEPNORM_ALPHA * dr2_ref[...] + acc, xhat, rstd, g_ref[...])
        outs[0][...] = dr1
        outs[1][...] = dr1.astype(BF16)
        _acc_rows(first, outs[2], dg)
        _acc_rows(first, outs[3], db)

    slots = _owner_slots()

    def to_sibling(parts):
        return (parts + [lax.empty((4, *pt.shape[1:]), F32) for pt in parts], *_plan_rs_sibling(len(parts)))

    def to_owner(names_, parts, from_sib):
        sums = [_rs_chip_sum("rs_chip_sum_" + nm, slots, pt, fs) for nm, pt, fs in zip(names_, parts, from_sib)]
        return (sums + [lax.empty(cs.shape, BF16) for cs in sums], *_plan_rs_owner(len(sums)))

    (dr1, dr1b, dg1, db1), (dw1, dw2, sib_1, sib_2) = _matmul(
        "mm_dh1_ln1_bwd", df1b, w1_g, dims=NT, grid=(t // tm_ep, 1, N_DEV),
        a_spec=pl.BlockSpec((tm_ep, ws_f), lambda m, n, k: (m, k)),
        b_spec=pl.BlockSpec((None, d, ws_f), lambda m, n, k: (k, 0, 0)),
        extras=(dr2, r1, ln1_g), extra_specs=(row_ep, row_ep, vec),
        out_shape=[sds((t, d)), sds((t, d), BF16), sds((1, d)), sds((1, d))], out_specs=[row_ep, row_ep, vec, vec],
        acc_shape=(tm_ep, d), epilogue=ep_ln1_bwd, sem=seq,
        comm=to_sibling([dw1, dw2]))
    own_1 = to_owner(["w_ff1"], [dw1], [sib_1])
    own_2 = to_owner(["w_ff2"], [dw2], [sib_2])[0]

    dwout = _matmul(
        "mm_dwout", mixin, dr1b, dims=TN, grid=(d // tn_d, d // tn_d, t // tkk),
        a_spec=pl.BlockSpec((tkk, tn_d), lambda m, n, k: (k, m)),
        b_spec=pl.BlockSpec((tkk, tn_d), lambda m, n, k: (k, n)),
        out_shape=[sds((d, d))], out_specs=[pl.BlockSpec((tn_d, tn_d), lambda m, n, k: (m, n))],
        acc_shape=(tn_d, tn_d), epilogue=_store_epilogue(F32))[0].reshape(N_DEV, ws_out, d)

    tn_mix = min(tn_d, p, d - p)

    def mm_dmixin(name, first, width, dtype, comm=None):
        return _matmul(
            name, dr1b, wout_2d, dims=NT, grid=(t // tm_big, width // tn_mix, 1),
            a_spec=pl.BlockSpec((tm_big, d), lambda m, n, k: (m, 0)),
            b_spec=pl.BlockSpec((tn_mix, d), lambda m, n, k: (n + first // tn_mix, 0)),
            out_shape=[sds((t, width), dtype)], out_specs=[pl.BlockSpec((tm_big, tn_mix), lambda m, n, k: (m, n))],
            acc_shape=(tm_big, tn_mix), epilogue=_store_epilogue(dtype), comm=comm)

    (dy_pool,), (dwout, sib_out) = mm_dmixin("mm_dmixin_pool", 0, p, F32, comm=to_sibling([dwout]))
    (do,) = mm_dmixin("mm_dmixin_att", p, d - p, BF16)

    (du_pool, dwp, dsc), (_, chips_out) = _pool_bwd(
        dy_pool, ypre, wp_full, sc_full, t, c_pool, comm=to_owner(["w_out"], [dwout], [sib_out]))
    (dq, dk, dv), (_, chips_1) = _attn_bwd(qkv, do, o, t, n_heads, comm=own_1)
    dub = jnp.concatenate([du_pool.astype(BF16), dq, dk, dv], axis=1)

    (dwin,), own_2 = _matmul(
        "mm_dwin", h0b, dub, dims=TN, grid=(d // tn_d, N_DEV, t // tkk),
        a_spec=pl.BlockSpec((tkk, tn_d), lambda m, n, k: (k, m)),
        b_spec=pl.BlockSpec((tkk, ws_in), lambda m, n, k: (k, n)),
        out_shape=[sds((N_DEV, d, ws_in))], out_specs=[pl.BlockSpec((None, tn_d, ws_in), lambda m, n, k: (n, m, 0))],
        acc_shape=(tn_d, ws_in), epilogue=_store_epilogue(F32), comm=(own_2, *_plan_rs_owner(1, (1, 2))))
    dwp_g = dwp.reshape(n_groups, N_DEV, pr, c_pool).transpose(1, 0, 2, 3).reshape(N_DEV, n_groups * pr, c_pool)

    def ep_ln0_bwd(acc, ex, outs):
        dr1_ref, x_ref, g_ref = ex
        first = pl.program_id(0) == 0
        xhat, rstd = _ln_stats(x_ref[...])
        dx, dg, db = _ln_bwd(DEEPNORM_ALPHA * dr1_ref[...] + acc, xhat, rstd, g_ref[...])
        outs[0][...] = dx
        _acc_rows(first, outs[1], dg)
        _acc_rows(first, outs[2], db)

    sib_in = to_sibling([dwin, dwp_g])
    to_diagonal = _plan_rs_owner(1, (3,))
    last_host = (own_2 + sib_in[0], *_join_plans((to_diagonal[0], 2, to_diagonal[1]), (sib_in[1], 4, sib_in[2])))
    def two_blocks(a_ref, b_ref):
        return _dot(a_ref[:, :ws_in], b_ref[0], NT) + _dot(a_ref[:, ws_in:], b_ref[1], NT)

    (dx, dg0, db0), (_, chips_2, dwin, dwp_g, sib_in_, sib_p) = _matmul(
        "mm_dh0_ln0_bwd", dub, win_g, dims=NT, grid=(t // tm_ep, 1, N_DEV // 2),
        a_spec=pl.BlockSpec((tm_ep, 2 * ws_in), lambda m, n, k: (m, k)),
        b_spec=pl.BlockSpec((2, d, ws_in), lambda m, n, k: (k, 0, 0)),
        extras=(dr1, x2, g0), extra_specs=(row_ep, row_ep, vec),
        out_shape=[sds((t, d)), sds((1, d)), sds((1, d))], out_specs=[row_ep, vec, vec],
        acc_shape=(tm_ep, d), epilogue=ep_ln0_bwd, sem=seq,
        comm=last_host, dot_fn=two_blocks)

    _, _, chips_in, chips_p = _copies_now("rs_owner_w_in", *to_owner(["w_in", "w_pool"], [dwin, dwp_g], [sib_in_, sib_p]))
    w_of = {"w_in": shards[0], "w_out": shards[1], "w_ff1": shards[2], "w_ff2": shards[3], "w_pool": shards[4]}
    mv_of = {"w_in": (m_w_in, v_w_in), "w_out": (m_w_out, v_w_out), "w_ff1": (m_w_ff1, v_w_ff1),
             "w_ff2": (m_w_ff2, v_w_ff2), "w_pool": (m_w_pool, v_w_pool)}
    big = {}
    for nm, pt, fs, fc in [("w_ff1", dw1, sib_1, chips_1), ("w_ff2", dw2, sib_2, chips_2), ("w_out", dwout, sib_out, chips_out),
                           ("w_in", dwin, sib_in_, chips_in), ("w_pool", dwp_g, sib_p, chips_p)]:
        w2d = w_of[nm]
        m_, v_ = mv_of[nm]
        big[nm] = _rs_final_adamw("rs_final_adamw_" + nm, slots, pt, fs, fc, w2d, m_.reshape(w2d.shape), v_.reshape(w2d.shape))

    n_f_rows = f // d
    pad_sc = d - p
    packet = jnp.concatenate(
        [loss_vec, dg0, db0, dg1, db1, dbf2, dg2, db2, dbf1.reshape(n_f_rows, d),
         jnp.pad(dsc.reshape(1, p), ((0, 0), (0, pad_sc)))], axis=0)
    n_rows = packet.shape[0]
    n_pad = (-n_rows) % 8
    packet = jnp.pad(packet, ((0, n_pad), (0, 0)))
    sums, loss11 = _small_all_reduce(packet)
    dsc_full = sums[8 + n_f_rows, :p].reshape(n_groups, N_DEV, pr)
    dsc_mine = lax.dynamic_index_in_dim(dsc_full, me, axis=1, keepdims=False)

    def sc_row(a):
        return jnp.pad(a.reshape(1, n_groups * pr), ((0, 0), (0, d - n_groups * pr)))

    def small_pack(ln0g, ln0b, l1g, l1b, bf2, l2g, l2b, bf1, sc):
        rows = [jnp.zeros((1, d), F32), ln0g.reshape(1, d), ln0b.reshape(1, d), l1g, l1b, bf2, l2g, l2b,
                bf1.reshape(n_f_rows, d), sc_row(sc), jnp.zeros((n_pad, d), F32)]
        return jnp.concatenate(rows, axis=0)

    w_small = small_pack(ln_in_g, ln_in_b, ln1_g, ln1_b, b_ff2, ln2_g, ln2_b, b_ff1, pool_scale)
    m_small = small_pack(m_ln_in_g, m_ln_in_b, m_ln1_g, m_ln1_b, m_b_ff2, m_ln2_g, m_ln2_b, m_b_ff1, m_pool_scale)
    v_small = small_pack(v_ln_in_g, v_ln_in_b, v_ln1_g, v_ln1_b, v_b_ff2, v_ln2_g, v_ln2_b, v_b_ff1, v_pool_scale)
    g_small = jnp.concatenate([sums[:8 + n_f_rows], sc_row(dsc_mine), jnp.zeros((n_pad, d), F32)], axis=0)
    small = (g_small,) + tuple(_small_adamw(w_small, g_small, m_small, v_small))

    def unpack(a):
        sc = a[8 + n_f_rows, :n_groups * pr].reshape(1, n_groups, pr)
        return {"ln_in_g": a[1], "ln_in_b": a[2], "ln1_g": a[3:4], "ln1_b": a[4:5], "b_ff2": a[5:6], "ln2_g": a[6:7],
                "ln2_b": a[7:8], "b_ff1": a[8:8 + n_f_rows].reshape(1, f), "pool_scale": sc}

    shapes = {"w_in": w_in.shape, "w_out": w_out.shape, "w_ff1": w_ff1.shape, "w_ff2": w_ff2.shape, "w_pool": w_pool.shape}
    order = ["ln_in_g", "ln_in_b", "w_in", "w_pool", "pool_scale", "w_out", "ln1_g", "ln1_b", "w_ff1", "b_ff1", "w_ff2",
             "b_ff2", "ln2_g", "ln2_b"]
    outs = []
    for kind in range(4):
        small_k = unpack(small[kind])
        for nm in order:
            outs.append(big[nm][kind].reshape(shapes[nm]) if nm in big else small_k[nm])
    return (loss11.reshape(()), dx.reshape(x.shape), *outs)
```

```python
import functools
import math

import jax
import jax.numpy as jnp
from jax import lax
from jax.experimental import pallas as pl
from jax.experimental.pallas import tpu as pltpu

F32 = jnp.float32
BF16 = jnp.bfloat16
MESH = pl.DeviceIdType.MESH

N_DEV = 8
HEAD_DIM = 128
POOL_WINDOWS = (2, 4, 8, 16)
DEEPNORM_ALPHA = (2.0 * 1) ** 0.25
LN_EPS = 1e-5
ADAM_LR = 0.001
ADAM_B1 = 0.9
ADAM_B2 = 0.999
ADAM_EPS = 1e-08
ADAM_WD = 0.01
ADAM_STEP = 10

V7X_VMEM_LIMIT = 56 * 1024 * 1024
ATT_BLOCK = 256
POOL_CHUNK = 256

NN = (((1,), (0,)), ((), ()))
NT = (((1,), (1,)), ((), ()))
TN = (((0,), (0,)), ((), ()))


def _dot(a, b, dims=NN):
    return lax.dot_general(a, b, dims, preferred_element_type=F32)


def _cparams(sem=None):
    return pltpu.CompilerParams(dimension_semantics=sem, vmem_limit_bytes=V7X_VMEM_LIMIT)


def _ln_stats(r):
    mu = jnp.mean(r, axis=-1, keepdims=True)
    xc = r - mu
    var = jnp.mean(xc * xc, axis=-1, keepdims=True)
    rstd = lax.rsqrt(var + LN_EPS)
    return xc * rstd, rstd


def _ln_bwd(dy, xhat, rstd, g):
    dxh = dy * g
    m1 = jnp.mean(dxh, axis=-1, keepdims=True)
    m2 = jnp.mean(dxh * xhat, axis=-1, keepdims=True)
    dx = rstd * (dxh - m1 - xhat * m2)
    dg = jnp.sum(dy * xhat, axis=0, keepdims=True)
    db = jnp.sum(dy, axis=0, keepdims=True)
    return dx, dg, db


def _acc_rows(first, ref, val):
    @pl.when(first)
    def _():
        ref[...] = val

    @pl.when(jnp.logical_not(first))
    def _():
        ref[...] += val


def _call(body, *, name, grid, in_specs, out_specs, out_shape, inputs, scratch_shapes=(), sem=None, comm=None):
    in_specs, out_specs, out_shape, inputs = list(in_specs), list(out_specs), list(out_shape), list(inputs)
    if comm is None:
        outs = pl.pallas_call(
            body, name=name, grid=grid, in_specs=in_specs, out_specs=out_specs, out_shape=out_shape,
            scratch_shapes=list(scratch_shapes), compiler_params=_cparams(sem))(*inputs)
        return list(outs), []
    arrays, plan, n_copies = comm
    n_in, n_out, nc, n_scr = len(inputs), len(out_shape), len(arrays), len(scratch_shapes)

    def hosted(*refs):
        ins = refs[:n_in]
        outs = refs[n_in + nc:n_in + nc + n_out]
        passed = refs[n_in + nc + n_out:n_in + 2 * nc + n_out]
        scratch = refs[n_in + 2 * nc + n_out:n_in + 2 * nc + n_out + n_scr]
        send_sems, recv_sems = refs[-2], refs[-1]
        ids = [pl.program_id(ax) for ax in range(len(grid))]
        first = functools.reduce(jnp.logical_and, [i_ == 0 for i_ in ids])
        last = functools.reduce(jnp.logical_and, [i_ == g - 1 for i_, g in zip(ids, grid)])

        @pl.when(first)
        def _():
            for cp in _plan_copies(plan, passed, send_sems, recv_sems):
                cp.start()

        body(*ins, *outs, *scratch)

        @pl.when(last)
        def _():
            for cp in _plan_copies(plan, passed, send_sems, recv_sems):
                cp.wait_send()
                cp.wait_recv()

    any_spec = pl.BlockSpec(memory_space=pl.ANY)
    outs = pl.pallas_call(
        hosted, name=name, grid=grid,
        in_specs=in_specs + [any_spec] * nc, out_specs=out_specs + [any_spec] * nc,
        out_shape=out_shape + [jax.ShapeDtypeStruct(a.shape, a.dtype) for a in arrays],
        scratch_shapes=list(scratch_shapes) + [pltpu.SemaphoreType.DMA((n_copies,)), pltpu.SemaphoreType.DMA((n_copies,))],
        input_output_aliases={n_in + i: n_out + i for i in range(nc)},
        compiler_params=pltpu.CompilerParams(dimension_semantics=("arbitrary",) * len(grid),
                                             vmem_limit_bytes=V7X_VMEM_LIMIT, has_side_effects=True),
    )(*inputs, *arrays)
    return list(outs[:n_out]), list(outs[n_out:])


def _matmul(name, a, b, *, dims, grid, a_spec, b_spec, extras=(), extra_specs=(), out_shape, out_specs,
            acc_shape, epilogue, k_axis=2, sem=("parallel", "parallel", "arbitrary"), comm=None, dot_fn=None):
    nk = grid[k_axis]
    n_extra = len(extras)
    n_out = len(out_shape)
    if dot_fn is None:
        def dot_fn(a_ref, b_ref):
            return _dot(a_ref[...], b_ref[...], dims)

    def body(a_ref, b_ref, *rest):
        extra_refs = rest[:n_extra]
        out_refs = rest[n_extra:n_extra + n_out]
        if nk == 1:
            epilogue(dot_fn(a_ref, b_ref), extra_refs, out_refs)
            return
        acc_ref = rest[n_extra + n_out]
        k = pl.program_id(k_axis)

        @pl.when(k == 0)
        def _():
            acc_ref[...] = jnp.zeros(acc_shape, F32)

        acc_ref[...] += dot_fn(a_ref, b_ref)

        @pl.when(k == nk - 1)
        def _():
            epilogue(acc_ref[...], extra_refs, out_refs)

    outs, passed = _call(
        body, name=name, grid=grid, in_specs=[a_spec, b_spec, *extra_specs], out_specs=out_specs, out_shape=out_shape,
        inputs=[a, b, *extras], scratch_shapes=[] if nk == 1 else [pltpu.VMEM(acc_shape, F32)], sem=sem, comm=comm)
    return outs if comm is None else (outs, passed)


def _store_epilogue(dtype):

    def ep(acc, extra_refs, out_refs):
        out_refs[0][...] = acc.astype(dtype)
    return ep


def _ln_in_fwd(x, g, b, tm, comm=None):
    t, d = x.shape

    def body(x_ref, g_ref, b_ref, h_ref, hb_ref):
        xhat, _ = _ln_stats(x_ref[...])
        h = xhat * g_ref[...] + b_ref[...]
        h_ref[...] = h
        hb_ref[...] = h.astype(BF16)

    row = pl.BlockSpec((tm, d), lambda i: (i, 0))
    vec = pl.BlockSpec((1, d), lambda i: (0, 0))
    return _call(
        body, name="ln_in_fwd", grid=(t // tm,), in_specs=[row, vec, vec], out_specs=[row, row],
        out_shape=[jax.ShapeDtypeStruct((t, d), F32), jax.ShapeDtypeStruct((t, d), BF16)],
        inputs=[x, g, b], sem=("parallel",), comm=comm)


def _split3(x):
    hi = x.astype(BF16)
    r = x - hi.astype(F32)
    mid = r.astype(BF16)
    lo = (r - mid.astype(F32)).astype(BF16)
    return hi, mid, lo


def _split2(x):
    hi = x.astype(BF16)
    lo = (x - hi.astype(F32)).astype(BF16)
    return hi, lo


def _pool_fwd(u, wp, sc, t, c, comm=None):
    n_groups = len(POOL_WINDOWS)
    tc = POOL_CHUNK
    n_chunks = t // tc

    def body(u_ref, wp_ref, sc_ref, y_ref, ypre_ref, xp_ref):
        g = pl.program_id(0)
        xp_ref[pl.ds(0, tc), :] = jnp.zeros((tc, c), F32)
        xp_ref[pl.ds(tc, t), :] = u_ref[...]
        out_i = lax.broadcasted_iota(jnp.int32, (tc, 2 * tc), 0)
        in_j = lax.broadcasted_iota(jnp.int32, (tc, 2 * tc), 1)
        lag = tc + out_i - in_j
        t_in_chunk = lax.broadcasted_iota(jnp.int32, (tc, 1), 0)
        for gi, w in enumerate(POOL_WINDOWS):
            @pl.when(g == gi)
            def _(w=w):
                band = jnp.logical_and(lag >= 0, lag < w).astype(BF16)

                def chunk(ci, carry):
                    start = pl.multiple_of(ci * tc, tc)
                    win = xp_ref[pl.ds(start, 2 * tc), :]
                    hi, mid, lo = _split3(win)
                    wsum = _dot(band, hi) + _dot(band, mid) + _dot(band, lo)
                    cnt = jnp.minimum(ci * tc + t_in_chunk + 1, w).astype(F32)
                    ypre = wsum * (1.0 / cnt) - win[tc:, :]
                    ypre_b = ypre.astype(BF16)
                    y = _dot(ypre_b, wp_ref[...]) * sc_ref[...]
                    ypre_ref[pl.ds(start, tc), :] = ypre_b
                    y_ref[pl.ds(start, tc), :] = y.astype(BF16)
                    return carry

                lax.fori_loop(0, n_chunks, chunk, 0)

    col = pl.BlockSpec((t, c), lambda g: (0, g))
    return _call(
        body, name="pool_fwd", grid=(n_groups,),
        in_specs=[col, pl.BlockSpec((None, c, c), lambda g: (g, 0, 0)), pl.BlockSpec((None, 1, c), lambda g: (g, 0, 0))],
        out_specs=[col, col],
        out_shape=[jax.ShapeDtypeStruct((t, n_groups * c), BF16), jax.ShapeDtypeStruct((t, n_groups * c), BF16)],
        inputs=[u, wp, sc], scratch_shapes=[pltpu.VMEM((t + tc, c), F32)], sem=("parallel",), comm=comm)


def _pool_bwd(dmixin, ypre, wp, sc, t, c, comm=None):
    n_groups = len(POOL_WINDOWS)
    tc = POOL_CHUNK
    n_chunks = t // tc

    def body(dy_ref, ypre_ref, wp_ref, sc_ref, du_ref, dwp_ref, dsc_ref, zp_ref):
        g = pl.program_id(0)
        zp_ref[pl.ds(t, tc), :] = jnp.zeros((tc, c), F32)
        dwp_ref[...] = jnp.zeros((c, c), F32)
        dsc_ref[...] = jnp.zeros((1, c), F32)
        out_i = lax.broadcasted_iota(jnp.int32, (tc, 2 * tc), 0)
        in_j = lax.broadcasted_iota(jnp.int32, (tc, 2 * tc), 1)
        lead = in_j - out_i
        t_in_chunk = lax.broadcasted_iota(jnp.int32, (tc, 1), 0)
        for gi, w in enumerate(POOL_WINDOWS):
            @pl.when(g == gi)
            def _(w=w):
                band = jnp.logical_and(lead >= 0, lead < w).astype(BF16)

                def first(ci, carry):
                    start = pl.multiple_of(ci * tc, tc)
                    dy = dy_ref[pl.ds(start, tc), :]
                    yp = ypre_ref[pl.ds(start, tc), :]
                    ymm = _dot(yp, wp_ref[...])
                    dsc_ref[...] += jnp.sum(dy * ymm, axis=0, keepdims=True)
                    dys_b = (dy * sc_ref[...]).astype(BF16)
                    dwp_ref[...] += _dot(yp, dys_b, TN)
                    dyp = _dot(dys_b, wp_ref[...], NT)
                    cnt = jnp.minimum(ci * tc + t_in_chunk + 1, w).astype(F32)
                    zp_ref[pl.ds(start, tc), :] = dyp * (1.0 / cnt)
                    du_ref[pl.ds(start, tc), :] = -dyp
                    return carry

                lax.fori_loop(0, n_chunks, first, 0)

                def second(ci, carry):
                    start = pl.multiple_of(ci * tc, tc)
                    hi, mid, lo = _split3(zp_ref[pl.ds(start, 2 * tc), :])
                    du_ref[pl.ds(start, tc), :] += _dot(band, hi) + _dot(band, mid) + _dot(band, lo)
                    return carry

                lax.fori_loop(0, n_chunks, second, 0)

    col = pl.BlockSpec((t, c), lambda g: (0, g))
    return _call(
        body, name="pool_bwd", grid=(n_groups,),
        in_specs=[col, col, pl.BlockSpec((None, c, c), lambda g: (g, 0, 0)), pl.BlockSpec((None, 1, c), lambda g: (g, 0, 0))],
        out_specs=[col, pl.BlockSpec((None, c, c), lambda g: (g, 0, 0)), pl.BlockSpec((None, 1, c), lambda g: (g, 0, 0))],
        out_shape=[jax.ShapeDtypeStruct((t, n_groups * c), F32), jax.ShapeDtypeStruct((n_groups, c, c), F32),
                   jax.ShapeDtypeStruct((n_groups, 1, c), F32)],
        inputs=[dmixin, ypre, wp, sc], scratch_shapes=[pltpu.VMEM((t + tc, c), F32)], sem=("parallel",), comm=comm)


ROW_PARTS = 2


def _att_consts():
    b = ATT_BLOCK
    rp = b // ROW_PARTS
    row = lax.broadcasted_iota(jnp.int32, (b, b), 0)
    col = lax.broadcasted_iota(jnp.int32, (b, b), 1)
    tri = (row >= col).astype(BF16)
    prow = lax.broadcasted_iota(jnp.int32, (rp, b), 0)
    pcol = lax.broadcasted_iota(jnp.int32, (rp, b), 1)
    causal = [pcol < prow + r * rp for r in range(ROW_PARTS)]
    return tri, causal


def _suffix_sum(x, tri):
    hi, lo = _split2(x)
    return _dot(hi, tri) + _dot(lo, tri)


LOG2_E = 1.4426950408889634


def _att_scores(qb, kb, mask):
    z2 = _dot(qb, kb, NT) * (LOG2_E / math.sqrt(HEAD_DIM))
    sp2 = jnp.maximum(z2, 0.0) + jnp.log2(1.0 + jnp.exp2(-jnp.abs(z2)))
    return z2, sp2, (sp2 if mask is None else jnp.where(mask, sp2, 0.0))


HEADS_PER_STEP = 2
ATT_LANES = HEADS_PER_STEP * HEAD_DIM
FWD_HEADS_PER_STEP = 4


def _head_lanes(s):
    return slice(s * HEAD_DIM, (s + 1) * HEAD_DIM)


UNDERFLOW_LOG2 = 160.0


def _sweep_earlier_blocks(i, state, per_chain, block):
    def lowest(st):
        low = st[0]
        for k in range(per_chain, len(st), per_chain):
            low = jnp.minimum(low, st[k])
        return jnp.min(low)

    def more(c):
        return jnp.logical_and(c[0] < i, c[1] < UNDERFLOW_LOG2)

    def trip(c):
        st = block(i - 1 - c[0], c[2:])
        return (c[0] + 1, lowest(st)) + tuple(st)

    return lax.while_loop(more, trip, (jnp.int32(0), lowest(state)) + tuple(state))[2:]


def _attn_fwd(qkv, t, n_heads, comm=None):
    b = ATT_BLOCK
    nq = t // b
    heads_per_step = FWD_HEADS_PER_STEP if n_heads % FWD_HEADS_PER_STEP == 0 else HEADS_PER_STEP
    n_steps = n_heads // heads_per_step

    rp = b // ROW_PARTS
    chains = [(s, r) for s in range(heads_per_step) for r in range(ROW_PARTS)]
    no_mask = [None] * ROW_PARTS

    def body(q_ref, k_ref, v_ref, o_ref):
        tri, causal = _att_consts()

        def blocks(qbs, j, state, masks):
            ks = pl.multiple_of(j * b, b)
            scores = [_att_scores(qbs[ci], k_ref[pl.ds(ks, b), _head_lanes(s)], masks[r]) for ci, (s, r) in enumerate(chains)]
            incls = [_suffix_sum(sc[2], tri) for sc in scores]
            out = []
            for ci, (s, r) in enumerate(chains):
                carry, acc = state[2 * ci], state[2 * ci + 1]
                a = jnp.exp2(scores[ci][0] - (incls[ci] + carry))
                if masks[r] is not None:
                    a = jnp.where(masks[r], a, 0.0)
                out += [carry + incls[ci][:, 0:1], acc + _dot(a.astype(BF16), v_ref[pl.ds(ks, b), _head_lanes(s)])]
            return tuple(out)

        def q_loop(i, _):
            qs = pl.multiple_of(i * b, b)
            qbs = [q_ref[pl.ds(qs + r * rp, rp), _head_lanes(s)] for s, r in chains]
            zero = (jnp.zeros((rp, 1), F32), jnp.zeros((rp, HEAD_DIM), F32)) * len(chains)
            state = blocks(qbs, i, zero, causal)
            state = _sweep_earlier_blocks(i, state, 2, lambda j, st: blocks(qbs, j, st, no_mask))
            for ci, (s, r) in enumerate(chains):
                o_ref[pl.ds(qs + r * rp, rp), _head_lanes(s)] = state[2 * ci + 1]
            return 0

        lax.fori_loop(0, nq, q_loop, 0)

    def heads(off):
        return pl.BlockSpec((t, heads_per_step * HEAD_DIM), lambda h: (0, off + h))

    return _call(
        body, name="attn_fwd", grid=(n_steps,),
        in_specs=[heads(0), heads(n_steps), heads(2 * n_steps)], out_specs=[heads(0)],
        out_shape=[jax.ShapeDtypeStruct((t, n_heads * HEAD_DIM), F32)],
        inputs=[qkv, qkv, qkv], sem=("parallel",), comm=comm)


def _attn_bwd(qkv, do, o, t, n_heads, comm=None):
    b = ATT_BLOCK
    nq = t // b
    n_steps = n_heads // HEADS_PER_STEP
    scale = 1.0 / math.sqrt(HEAD_DIM)
    rp = b // ROW_PARTS
    chains = [(s, r) for s in range(HEADS_PER_STEP) for r in range(ROW_PARTS)]
    no_mask = [None] * ROW_PARTS

    def body(q_ref, k_ref, v_ref, do_ref, o_ref, dq_ref, dk_ref, dv_ref, qt_ref, dot_ref, dkt_ref, dvt_ref):
        for j in range(nq):
            rows = pl.ds(j * b, b)
            qt_ref[j] = q_ref[rows, :].astype(F32).T.astype(BF16)
            dot_ref[j] = do_ref[rows, :].astype(F32).T.astype(BF16)
        dkt_ref[...] = jnp.zeros((nq, ATT_LANES, b), F32)
        dvt_ref[...] = jnp.zeros((nq, ATT_LANES, b), F32)
        tri, causal = _att_consts()

        def blocks(i, fixed, j, state, masks):
            ks = pl.multiple_of(j * b, b)
            n = len(chains)
            kbs = [k_ref[pl.ds(ks, b), _head_lanes(s)] for s, _ in chains]
            scores = [_att_scores(fixed[ci][0], kbs[ci], masks[r]) for ci, (s, r) in enumerate(chains)]
            incls = [_suffix_sum(sc[2], tri) for sc in scores]
            das = [_dot(fixed[ci][1], v_ref[pl.ds(ks, b), _head_lanes(s)], NT) for ci, (s, r) in enumerate(chains)]
            a_bs, gs = [], []
            for ci, (s, r) in enumerate(chains):
                a = jnp.exp2(scores[ci][0] - (incls[ci] + state[3 * ci]))
                if masks[r] is not None:
                    a = jnp.where(masks[r], a, 0.0)
                a_bs.append(a.astype(BF16))
                gs.append(a_bs[ci].astype(F32) * das[ci])
            g_incls = [_suffix_sum(g, tri) for g in gs]
            dz_bs = []
            for ci, (s, r) in enumerate(chains):
                rest = (fixed[ci][2] - state[3 * ci + 1]) - (g_incls[ci] - gs[ci])
                sig = jnp.exp2(scores[ci][0] - scores[ci][1])
                dz = (gs[ci] - sig * rest) * scale
                if masks[r] is not None:
                    dz = jnp.where(masks[r], dz, 0.0)
                dz_bs.append(dz.astype(BF16))
            out = []
            for ci in range(n):
                out += [state[3 * ci] + incls[ci][:, 0:1], state[3 * ci + 1] + g_incls[ci][:, 0:1],
                        state[3 * ci + 2] + _dot(dz_bs[ci], kbs[ci])]
            for s in range(HEADS_PER_STEP):
                lanes = _head_lanes(s)
                dk_add, dv_add = None, None
                for ci, (cs, r) in enumerate(chains):
                    if cs == s:
                        part = slice(r * rp, (r + 1) * rp)
                        dk_c = _dot(qt_ref[i, lanes, part], dz_bs[ci])
                        dv_c = _dot(dot_ref[i, lanes, part], a_bs[ci])
                        dk_add = dk_c if dk_add is None else dk_add + dk_c
                        dv_add = dv_c if dv_add is None else dv_add + dv_c
                dkt_ref[j, lanes, :] += dk_add
                dvt_ref[j, lanes, :] += dv_add
            return tuple(out)

        def q_loop(i, _):
            qs = pl.multiple_of(i * b, b)
            fixed = []
            for s, r in chains:
                rows = pl.ds(qs + r * rp, rp)
                dob = do_ref[rows, _head_lanes(s)]
                total = jnp.sum(dob.astype(F32) * o_ref[rows, _head_lanes(s)], axis=-1, keepdims=True)
                fixed.append((q_ref[rows, _head_lanes(s)], dob, total))
            zero = (jnp.zeros((rp, 1), F32), jnp.zeros((rp, 1), F32), jnp.zeros((rp, HEAD_DIM), F32)) * len(chains)
            state = blocks(i, fixed, i, zero, causal)
            state = _sweep_earlier_blocks(i, state, 3, lambda j, st: blocks(i, fixed, j, st, no_mask))
            for ci, (s, r) in enumerate(chains):
                dq_ref[pl.ds(qs + r * rp, rp), _head_lanes(s)] = state[3 * ci + 2].astype(BF16)
            return 0

        lax.fori_loop(0, nq, q_loop, 0)
        for j in range(nq):
            rows = pl.ds(j * b, b)
            dk_ref[rows, :] = dkt_ref[j].T.astype(BF16)
            dv_ref[rows, :] = dvt_ref[j].T.astype(BF16)

    def heads(off):
        return pl.BlockSpec((t, ATT_LANES), lambda h: (0, off + h))

    shape = jax.ShapeDtypeStruct((t, n_heads * HEAD_DIM), BF16)
    return _call(
        body, name="attn_bwd", grid=(n_steps,),
        in_specs=[heads(0), heads(n_steps), heads(2 * n_steps), heads(0), heads(0)],
        out_specs=[heads(0)] * 3, out_shape=[shape] * 3, inputs=[qkv, qkv, qkv, do, o],
        scratch_shapes=[pltpu.VMEM((nq, ATT_LANES, b), BF16)] * 2 + [pltpu.VMEM((nq, ATT_LANES, b), F32)] * 2,
        sem=("parallel",), comm=comm)


def _place():
    x, y, c = lax.axis_index("x"), lax.axis_index("y"), lax.axis_index("c")
    return x, y, c


def _flip(v, on):
    return 1 - v if on else v


def _plan_copies(plan, refs, send_sems, recv_sems):
    return [pltpu.make_async_remote_copy(src_ref=src, dst_ref=dst, send_sem=send_sems.at[k], recv_sem=recv_sems.at[k],
                                         device_id=dev, device_id_type=MESH)
            for k, (src, dst, dev) in enumerate(plan(refs))]


def _copies_now(name, arrays, plan, n_copies):
    n = len(arrays)

    def body(*refs):
        copies = _plan_copies(plan, refs[n:2 * n], refs[2 * n], refs[2 * n + 1])
        for cp in copies:
            cp.start()
        for cp in copies:
            cp.wait_send()
            cp.wait_recv()

    any_spec = pl.BlockSpec(memory_space=pl.ANY)
    return list(pl.pallas_call(
        body, name=name, in_specs=[any_spec] * n, out_specs=[any_spec] * n,
        out_shape=[jax.ShapeDtypeStruct(a.shape, a.dtype) for a in arrays],
        input_output_aliases={i: i for i in range(n)},
        scratch_shapes=[pltpu.SemaphoreType.DMA((n_copies,)), pltpu.SemaphoreType.DMA((n_copies,))],
        compiler_params=pltpu.CompilerParams(has_side_effects=True),
    )(*arrays))


SIBLING, ACROSS_Y, ACROSS_X, DIAGONAL = 1, 2, 4, 6


def _plan_gather_own(peers, rows=None):
    def plan(refs):
        x, y, c = _place()
        mine = refs[0].at[4 * x + 2 * y + c]
        if rows is not None:
            mine = mine.at[pl.ds(*rows)]
        return [(mine, mine, (_flip(x, k & 4), _flip(y, k & 2), _flip(c, k & 1))) for k in peers]
    return plan, len(peers)


def _plan_gather_forward(n):
    def plan(refs):
        x, y, c = _place()
        out = []
        for ti in range(n):
            for r in range(1, 4):
                blk = refs[ti].at[4 * _flip(x, r & 2) + 2 * _flip(y, r & 1) + c]
                out.append((blk, blk, (x, y, 1 - c)))
        return out
    return plan, 3 * n


def _join_plans(*parts):
    def plan(refs):
        out, at = [], 0
        for part, n_arrays, _ in parts:
            out += part(refs[at:at + n_arrays])
            at += n_arrays
        return out
    return plan, sum(n_cp for _, _, n_cp in parts)


def _plan_rs_sibling(n):
    def plan(refs):
        x, y, c = _place()
        out = []
        for ti in range(n):
            for r in range(4):
                src = refs[ti].at[4 * _flip(x, r & 2) + 2 * _flip(y, r & 1) + (1 - c)]
                out.append((src, refs[n + ti].at[r], (x, y, 1 - c)))
        return out
    return plan, 4 * n


def _plan_rs_owner(n, relations=(1, 2, 3)):
    def plan(refs):
        x, y, c = _place()
        out = []
        for ti in range(n):
            for r in relations:
                out.append((refs[ti].at[r], refs[n + ti].at[r], (_flip(x, r & 2), _flip(y, r & 1), c)))
        return out
    return plan, len(relations) * n


def _owner_slots():
    x, y, c = _place()
    idx = []
    for r in range(4):
        ox, oy = (1 - x if r & 2 else x), (1 - y if r & 1 else y)
        idx.append(4 * ox + 2 * oy + c)
    return jnp.stack(idx).astype(jnp.int32)


def _row_tile(rows, cols):
    tr = max(8, min(rows, (1 << 19) // cols))
    while rows % tr:
        tr //= 2
    return tr


def _rs_chip_sum(name, slots, partial, from_sibling):
    _, rows, cols = partial.shape
    tr = _row_tile(rows, cols)

    def body(slots_ref, p_ref, s_ref, o_ref):
        o_ref[...] = (p_ref[...] + s_ref[...]).astype(BF16)

    grid_spec = pltpu.PrefetchScalarGridSpec(
        num_scalar_prefetch=1, grid=(3, rows // tr),
        in_specs=[pl.BlockSpec((None, tr, cols), lambda r, i, s: (s[r + 1], i, 0)),
                  pl.BlockSpec((None, tr, cols), lambda r, i, s: (r + 1, i, 0))],
        out_specs=pl.BlockSpec((None, tr, cols), lambda r, i, s: (r + 1, i, 0)))
    return pl.pallas_call(
        body, name=name, grid_spec=grid_spec, out_shape=jax.ShapeDtypeStruct((4, rows, cols), BF16),
        compiler_params=_cparams(("parallel", "parallel")),
    )(slots, partial, from_sibling)


def _adamw(w, g, m, v):
    m = ADAM_B1 * m + (1.0 - ADAM_B1) * g
    v = ADAM_B2 * v + (1.0 - ADAM_B2) * (g * g)
    m_hat = m / (1.0 - ADAM_B1 ** ADAM_STEP)
    v_hat = v / (1.0 - ADAM_B2 ** ADAM_STEP)
    delta = -ADAM_LR * (m_hat / (jnp.sqrt(v_hat) + ADAM_EPS) + ADAM_WD * w)
    return delta, m, v


def _rs_final_adamw(name, slots, partial, from_sibling, from_chips, w, m, v):
    rows, cols = w.shape
    tr = _row_tile(rows, cols)

    def body(slots_ref, p_ref, s_ref, c1_ref, c2_ref, c3_ref, w_ref, m_ref, v_ref, g_ref, d_ref, nm_ref, nv_ref):
        g = p_ref[...] + s_ref[...]
        g = g + c1_ref[...].astype(F32)
        g = g + c2_ref[...].astype(F32)
        g = g + c3_ref[...].astype(F32)
        delta, nm, nv = _adamw(w_ref[...], g, m_ref[...], v_ref[...])
        g_ref[...] = g
        d_ref[...] = delta
        nm_ref[...] = nm
        nv_ref[...] = nv

    def slot(r):
        return pl.BlockSpec((None, tr, cols), lambda i, s: (r, i, 0))

    flat = pl.BlockSpec((tr, cols), lambda i, s: (i, 0))
    grid_spec = pltpu.PrefetchScalarGridSpec(
        num_scalar_prefetch=1, grid=(rows // tr,),
        in_specs=[pl.BlockSpec((None, tr, cols), lambda i, s: (s[0], i, 0)), slot(0), slot(1), slot(2), slot(3), flat, flat, flat],
        out_specs=[flat] * 4)
    return pl.pallas_call(
        body, name=name, grid_spec=grid_spec, out_shape=[jax.ShapeDtypeStruct((rows, cols), F32)] * 4,
        compiler_params=_cparams(("parallel",)),
    )(slots, partial, from_sibling, from_chips, from_chips, from_chips, w, m, v)


def _small_all_reduce(packet):
    rows, d = packet.shape

    def body(p_ref, sum_ref, loss_ref, all_ref, send_sems, recv_sems):
        x, y, c = _place()
        me = 4 * x + 2 * y + c
        all_ref[me] = p_ref[...]
        copies = []
        for k in range(1, N_DEV):
            px, py, pc = (1 - x if k & 4 else x), (1 - y if k & 2 else y), (1 - c if k & 1 else c)
            cp = pltpu.make_async_remote_copy(
                src_ref=p_ref, dst_ref=all_ref.at[me], send_sem=send_sems.at[k], recv_sem=recv_sems.at[k],
                device_id=(px, py, pc), device_id_type=MESH)
            cp.start()
            copies.append(cp)
        for cp in copies:
            cp.wait_recv()
        for cp in copies:
            cp.wait_send()
        total = all_ref[0]
        for j in range(1, N_DEV):
            total = total + all_ref[j]
        sum_ref[...] = total
        loss_ref[...] = jnp.sum(total[0:1, :], axis=-1, keepdims=True)

    vmem = pl.BlockSpec(memory_space=pltpu.VMEM)
    return pl.pallas_call(
        body, name="small_all_reduce",
        in_specs=[vmem], out_specs=[vmem, vmem],
        out_shape=[jax.ShapeDtypeStruct((rows, d), F32), jax.ShapeDtypeStruct((1, 1), F32)],
        scratch_shapes=[pltpu.VMEM((N_DEV, rows, d), F32), pltpu.SemaphoreType.DMA((N_DEV,)), pltpu.SemaphoreType.DMA((N_DEV,))],
        compiler_params=pltpu.CompilerParams(has_side_effects=True),
    )(packet)


def _small_adamw(w, g, m, v):
    def body(w_ref, g_ref, m_ref, v_ref, d_ref, nm_ref, nv_ref):
        delta, nm, nv = _adamw(w_ref[...], g_ref[...], m_ref[...], v_ref[...])
        d_ref[...] = delta
        nm_ref[...] = nm
        nv_ref[...] = nv

    vmem = pl.BlockSpec(memory_space=pltpu.VMEM)
    return pl.pallas_call(
        body, name="small_adamw", in_specs=[vmem] * 4, out_specs=[vmem] * 3,
        out_shape=[jax.ShapeDtypeStruct(w.shape, F32)] * 3,
    )(w, g, m, v)


def kernel(x, ln_in_g, ln_in_b, w_in, w_pool, pool_scale, w_out, ln1_g, ln1_b, w_ff1, b_ff1, w_ff2, b_ff2, ln2_g, ln2_b, loss_target, m_ln_in_g, m_ln_in_b, m_w_in, m_w_pool, m_pool_scale, m_w_out, m_ln1_g, m_ln1_b, m_w_ff1, m_b_ff1, m_w_ff2, m_b_ff2, m_ln2_g, m_ln2_b, v_ln_in_g, v_ln_in_b, v_w_in, v_w_pool, v_pool_scale, v_w_out, v_ln1_g, v_ln1_b, v_w_ff1, v_b_ff1, v_w_ff2, v_b_ff2, v_ln2_g, v_ln2_b):
    t, d = x.shape[1], x.shape[2]
    n_groups = len(POOL_WINDOWS)
    c_pool = w_pool.shape[3]
    p = n_groups * c_pool
    n_heads = (d - p) // HEAD_DIM
    ws_in = w_in.shape[2]
    n_in = N_DEV * ws_in
    ws_out = w_out.shape[1]
    ws_f = w_ff1.shape[2]
    f = N_DEV * ws_f
    pr = w_pool.shape[2]
    assert n_in == p + 3 * n_heads * HEAD_DIM and N_DEV * ws_out == d and N_DEV * pr == c_pool

    tm_big = min(t, 1024)
    tm_ep = min(t, 512)
    tkk = min(t, 4096)
    half_f = min(ws_f, 512)
    per_f = ws_f // half_f

    x2 = x.reshape(t, d)
    target = loss_target.reshape(t, d)
    g0, b0 = ln_in_g.reshape(1, d), ln_in_b.reshape(1, d)

    shards = [w_in.reshape(d, ws_in), w_out.reshape(ws_out, d), w_ff1.reshape(d, ws_f), w_ff2.reshape(ws_f, d),
              w_pool.reshape(n_groups * pr, c_pool)]
    x_, y_, c_ = _place()
    me = 4 * x_ + 2 * y_ + c_
    def landing(block):
        return lax.dynamic_update_index_in_dim(lax.empty((N_DEV, *block.shape), block.dtype), block, me, 0)

    land_in, land_out, land_1, land_2, land_pool = [landing(s.astype(BF16)) for s in shards]
    land_scale = landing(pool_scale.reshape(n_groups, pr))

    def sds(shape, dtype=F32):
        return jax.ShapeDtypeStruct(shape, dtype)

    vec = pl.BlockSpec((1, d), lambda m, n, k: (0, 0))
    row_ep = pl.BlockSpec((tm_ep, d), lambda m, n, k: (m, 0))
    tm_res = min(t, 256)
    row_res = pl.BlockSpec((tm_res, d), lambda m, n, k: (m, 0))
    seq = ("arbitrary", "arbitrary", "arbitrary")

    two_level = _plan_gather_own([SIBLING, ACROSS_Y, ACROSS_X, DIAGONAL])
    forward = _plan_gather_forward(1)
    first_needed = [land_in, land_pool, land_scale]
    (h0, h0b), first_needed = _ln_in_fwd(
        x2, g0, b0, tm_big, comm=(first_needed, *_join_plans(*[(two_level[0], 1, two_level[1])] * 3)))
    win_g, wpool_g, scale_g = _copies_now("gather_forward_w_in", first_needed, *_plan_gather_forward(3))
    wp_full = wpool_g.reshape(N_DEV, n_groups, pr, c_pool).transpose(1, 0, 2, 3).reshape(n_groups, c_pool, c_pool)
    sc_full = scale_g.transpose(1, 0, 2).reshape(n_groups, 1, c_pool)

    pool_shards = p // ws_in

    def mm_u(name, first, count, dtype, comm=None):
        return _matmul(
            name, h0b, win_g, dims=NN, grid=(t // tm_big, count, 1),
            a_spec=pl.BlockSpec((tm_big, d), lambda m, n, k: (m, 0)),
            b_spec=pl.BlockSpec((None, d, ws_in), lambda m, n, k: (n + first, 0, 0)),
            out_shape=[sds((t, count * ws_in), dtype)],
            out_specs=[pl.BlockSpec((tm_big, ws_in), lambda m, n, k: (m, n))],
            acc_shape=(tm_big, ws_in), epilogue=_store_epilogue(dtype), comm=comm)

    half = land_1.shape[1] // 2
    diag_a, diag_b = _plan_gather_own([DIAGONAL], (0, half)), _plan_gather_own([DIAGONAL], (half, half))
    (u_pool,), (w1_diag,) = mm_u("mm_u_pool", 0, pool_shards, F32, comm=([land_1], *diag_a))
    (qkv,), (wout_part,) = mm_u("mm_u_qkv", pool_shards, N_DEV - pool_shards, BF16, comm=([land_out], *two_level))

    (y_pool, ypre), (wout_g, w1_diag) = _pool_fwd(
        u_pool, wp_full, sc_full, t, c_pool,
        comm=([wout_part, w1_diag], *_join_plans((forward[0], 1, forward[1]), (diag_b[0], 1, diag_b[1]))))
    (o,), (w1_part,) = _attn_fwd(qkv, t, n_heads, comm=([w1_diag], *_plan_gather_own([SIBLING, ACROSS_Y, ACROSS_X])))
    mixin = jnp.concatenate([y_pool, o.astype(BF16)], axis=1)
    wout_2d = wout_g.reshape(d, d)

    def ep_ln1(acc, ex, outs):
        h0_ref, g_ref, b_ref = ex
        r1 = DEEPNORM_ALPHA * h0_ref[...] + acc
        xhat, _ = _ln_stats(r1)
        h1 = xhat * g_ref[...] + b_ref[...]
        outs[0][...] = r1
        outs[1][...] = h1
        outs[2][...] = h1.astype(BF16)

    (r1, h1, h1b), (w1_g,) = _matmul(
        "mm_mix_ln1", mixin, wout_2d, dims=NN, grid=(t // tm_res, 1, 1),
        a_spec=pl.BlockSpec((tm_res, d), lambda m, n, k: (m, 0)),
        b_spec=pl.BlockSpec((d, d), lambda m, n, k: (0, 0)),
        extras=(h0, ln1_g, ln1_b), extra_specs=(row_res, vec, vec),
        out_shape=[sds((t, d)), sds((t, d)), sds((t, d), BF16)], out_specs=[row_res] * 3,
        acc_shape=(tm_res, d), epilogue=ep_ln1, comm=([w1_part], *forward))

    def ep_ff1(acc, ex, outs):
        f1 = acc + ex[0][...]
        outs[0][...] = f1
        r = jnp.maximum(f1, 0.0)
        outs[1][...] = (r * r).astype(BF16)

    ff_tile = pl.BlockSpec((tm_big, half_f), lambda m, n, k: (m, n))
    (f1, act), (w2_part,) = _matmul(
        "mm_ff1", h1b, w1_g, dims=NN, grid=(t // tm_big, f // half_f, 1),
        a_spec=pl.BlockSpec((tm_big, d), lambda m, n, k: (m, 0)),
        b_spec=pl.BlockSpec((None, d, half_f), lambda m, n, k: (n // per_f, 0, n % per_f)),
        extras=(b_ff1,), extra_specs=(pl.BlockSpec((1, half_f), lambda m, n, k: (0, n)),),
        out_shape=[sds((t, f)), sds((t, f), BF16)], out_specs=[ff_tile, ff_tile],
        acc_shape=(tm_big, half_f), epilogue=ep_ff1, comm=([land_2], *two_level))
    (w2_g,) = _copies_now("gather_forward_w_ff2", [w2_part], *forward)

    def ep_ln2(acc, ex, outs):
        h1_ref, tgt_ref, bf2_ref, g_ref, b_ref = ex
        dr2_ref, dr2b_ref, dg_ref, db_ref, dbf2_ref, loss_ref = outs
        first = pl.program_id(0) == 0
        r2 = DEEPNORM_ALPHA * h1_ref[...] + (acc + bf2_ref[...])
        xhat, rstd = _ln_stats(r2)
        err = xhat * g_ref[...] + b_ref[...] - tgt_ref[...]
        dr2, dg, db = _ln_bwd(err * (1.0 / d), xhat, rstd, g_ref[...])
        dr2_ref[...] = dr2
        dr2b_ref[...] = dr2.astype(BF16)
        _acc_rows(first, dg_ref, dg)
        _acc_rows(first, db_ref, db)
        _acc_rows(first, dbf2_ref, jnp.sum(dr2, axis=0, keepdims=True))
        _acc_rows(first, loss_ref, jnp.sum(err * err, axis=0, keepdims=True) * (0.5 / d))

    dr2, dr2b, dg2, db2, dbf2, loss_vec = _matmul(
        "mm_ff2_ln2_loss", act, w2_g, dims=NN, grid=(t // tm_ep, 1, N_DEV),
        a_spec=pl.BlockSpec((tm_ep, ws_f), lambda m, n, k: (m, k)),
        b_spec=pl.BlockSpec((None, ws_f, d), lambda m, n, k: (k, 0, 0)),
        extras=(h1, target, b_ff2, ln2_g, ln2_b), extra_specs=(row_ep, row_ep, vec, vec, vec),
        out_shape=[sds((t, d)), sds((t, d), BF16)] + [sds((1, d))] * 4, out_specs=[row_ep, row_ep, vec, vec, vec, vec],
        acc_shape=(tm_ep, d), epilogue=ep_ln2, sem=seq)

    def ep_dff1(acc, ex, outs):
        df1 = acc * (2.0 * jnp.maximum(ex[0][...], 0.0))
        outs[0][...] = df1.astype(BF16)
        _acc_rows(pl.program_id(1) == 0, outs[1], jnp.sum(df1, axis=0, keepdims=True))

    df_tile = pl.BlockSpec((tm_big, ws_f), lambda n, m, k: (m, n))
    df1b, dbf1 = _matmul(
        "mm_dff1", dr2b, w2_g, dims=NT, grid=(N_DEV, t // tm_big, 1),
        a_spec=pl.BlockSpec((tm_big, d), lambda n, m, k: (m, 0)),
        b_spec=pl.BlockSpec((None, ws_f, d), lambda n, m, k: (n, 0, 0)),
        extras=(f1,), extra_specs=(df_tile,),
        out_shape=[sds((t, f), BF16), sds((1, f))], out_specs=[df_tile, pl.BlockSpec((1, ws_f), lambda n, m, k: (0, n))],
        acc_shape=(tm_big, ws_f), epilogue=ep_dff1, sem=("parallel", "arbitrary", "arbitrary"))

    tn_d = min(d, 1024)
    dw2 = _matmul(
        "mm_dw2", act, dr2b, dims=TN, grid=(d // tn_d, N_DEV, t // tkk),
        a_spec=pl.BlockSpec((tkk, ws_f), lambda n, m, k: (k, m)),
        b_spec=pl.BlockSpec((tkk, tn_d), lambda n, m, k: (k, n)),
        out_shape=[sds((N_DEV, ws_f, d))], out_specs=[pl.BlockSpec((None, ws_f, tn_d), lambda n, m, k: (m, 0, n))],
        acc_shape=(ws_f, tn_d), epilogue=_store_epilogue(F32))[0]

    dw1 = _matmul(
        "mm_dw1", h1b, df1b, dims=TN, grid=(d // tn_d, N_DEV, t // tkk),
        a_spec=pl.BlockSpec((tkk, tn_d), lambda m, n, k: (k, m)),
        b_spec=pl.BlockSpec((tkk, ws_f), lambda m, n, k: (k, n)),
        out_shape=[sds((N_DEV, d, ws_f))], out_specs=[pl.BlockSpec((None, tn_d, ws_f), lambda m, n, k: (n, m, 0))],
        acc_shape=(tn_d, ws_f), epilogue=_store_epilogue(F32))[0]

    def ep_ln1_bwd(acc, ex, outs):
        dr2_ref, r1_ref, g_ref = ex
        first = pl.program_id(0) == 0
        xhat, rstd = _ln_stats(r1_ref[...])
        dr1, dg, db = _ln_bwd(DEEPNORM_ALPHA * dr2_ref[...] + acc, xhat, rstd, g_ref[...])
        outs[0][...] = dr1
        outs[1][...] = dr1.astype(BF16)
        _acc_rows(first, outs[2], dg)
        _acc_rows(first, outs[3], db)

    slots = _owner_slots()

    def to_sibling(parts):
        return (parts + [lax.empty((4, *pt.shape[1:]), F32) for pt in parts], *_plan_rs_sibling(len(parts)))

    def to_owner(names_, parts, from_sib):
        sums = [_rs_chip_sum("rs_chip_sum_" + nm, slots, pt, fs) for nm, pt, fs in zip(names_, parts, from_sib)]
        return (sums + [lax.empty(cs.shape, BF16) for cs in sums], *_plan_rs_owner(len(sums)))

    (dr1, dr1b, dg1, db1), (dw1, dw2, sib_1, sib_2) = _matmul(
        "mm_dh1_ln1_bwd", df1b, w1_g, dims=NT, grid=(t // tm_ep, 1, N_DEV),
        a_spec=pl.BlockSpec((tm_ep, ws_f), lambda m, n, k: (m, k)),
        b_spec=pl.BlockSpec((None, d, ws_f), lambda m, n, k: (k, 0, 0)),
        extras=(dr2, r1, ln1_g), extra_specs=(row_ep, row_ep, vec),
        out_shape=[sds((t, d)), sds((t, d), BF16), sds((1, d)), sds((1, d))], out_specs=[row_ep, row_ep, vec, vec],
        acc_shape=(tm_ep, d), epilogue=ep_ln1_bwd, sem=seq,
        comm=to_sibling([dw1, dw2]))
    own_1 = to_owner(["w_ff1"], [dw1], [sib_1])
    own_2 = to_owner(["w_ff2"], [dw2], [sib_2])[0]

    dwout = _matmul(
        "mm_dwout", mixin, dr1b, dims=TN, grid=(d // tn_d, d // tn_d, t // tkk),
        a_spec=pl.BlockSpec((tkk, tn_d), lambda m, n, k: (k, m)),
        b_spec=pl.BlockSpec((tkk, tn_d), lambda m, n, k: (k, n)),
        out_shape=[sds((d, d))], out_specs=[pl.BlockSpec((tn_d, tn_d), lambda m, n, k: (m, n))],
        acc_shape=(tn_d, tn_d), epilogue=_store_epilogue(F32))[0].reshape(N_DEV, ws_out, d)

    tn_mix = min(tn_d, p, d - p)

    def mm_dmixin(name, first, width, dtype, comm=None):
        return _matmul(
            name, dr1b, wout_2d, dims=NT, grid=(t // tm_big, width // tn_mix, 1),
            a_spec=pl.BlockSpec((tm_big, d), lambda m, n, k: (m, 0)),
            b_spec=pl.BlockSpec((tn_mix, d), lambda m, n, k: (n + first // tn_mix, 0)),
            out_shape=[sds((t, width), dtype)], out_specs=[pl.BlockSpec((tm_big, tn_mix), lambda m, n, k: (m, n))],
            acc_shape=(tm_big, tn_mix), epilogue=_store_epilogue(dtype), comm=comm)

    (dy_pool,), (dwout, sib_out) = mm_dmixin("mm_dmixin_pool", 0, p, F32, comm=to_sibling([dwout]))
    (do,) = mm_dmixin("mm_dmixin_att", p, d - p, BF16)

    (du_pool, dwp, dsc), (_, chips_out) = _pool_bwd(
        dy_pool, ypre, wp_full, sc_full, t, c_pool, comm=to_owner(["w_out"], [dwout], [sib_out]))
    (dq, dk, dv), (_, chips_1) = _attn_bwd(qkv, do, o, t, n_heads, comm=own_1)
    dub = jnp.concatenate([du_pool.astype(BF16), dq, dk, dv], axis=1)

    (dwin,), own_2 = _matmul(
        "mm_dwin", h0b, dub, dims=TN, grid=(d // tn_d, N_DEV, t // tkk),
        a_spec=pl.BlockSpec((tkk, tn_d), lambda m, n, k: (k, m)),
        b_spec=pl.BlockSpec((tkk, ws_in), lambda m, n, k: (k, n)),
        out_shape=[sds((N_DEV, d, ws_in))], out_specs=[pl.BlockSpec((None, tn_d, ws_in), lambda m, n, k: (n, m, 0))],
        acc_shape=(tn_d, ws_in), epilogue=_store_epilogue(F32), comm=(own_2, *_plan_rs_owner(1, (1, 2))))
    dwp_g = dwp.reshape(n_groups, N_DEV, pr, c_pool).transpose(1, 0, 2, 3).reshape(N_DEV, n_groups * pr, c_pool)

    def ep_ln0_bwd(acc, ex, outs):
        dr1_ref, x_ref, g_ref = ex
        first = pl.program_id(0) == 0
        xhat, rstd = _ln_stats(x_ref[...])
        dx, dg, db = _ln_bwd(DEEPNORM_ALPHA * dr1_ref[...] + acc, xhat, rstd, g_ref[...])
        outs[0][...] = dx
        _acc_rows(first, outs[1], dg)
        _acc_rows(first, outs[2], db)

    sib_in = to_sibling([dwin, dwp_g])
    to_diagonal = _plan_rs_owner(1, (3,))
    last_host = (own_2 + sib_in[0], *_join_plans((to_diagonal[0], 2, to_diagonal[1]), (sib_in[1], 4, sib_in[2])))
    def two_blocks(a_ref, b_ref):
        return _dot(a_ref[:, :ws_in], b_ref[0], NT) + _dot(a_ref[:, ws_in:], b_ref[1], NT)

    (dx, dg0, db0), (_, chips_2, dwin, dwp_g, sib_in_, sib_p) = _matmul(
        "mm_dh0_ln0_bwd", dub, win_g, dims=NT, grid=(t // tm_ep, 1, N_DEV // 2),
        a_spec=pl.BlockSpec((tm_ep, 2 * ws_in), lambda m, n, k: (m, k)),
        b_spec=pl.BlockSpec((2, d, ws_in), lambda m, n, k: (k, 0, 0)),
        extras=(dr1, x2, g0), extra_specs=(row_ep, row_ep, vec),
        out_shape=[sds((t, d)), sds((1, d)), sds((1, d))], out_specs=[row_ep, vec, vec],
        acc_shape=(tm_ep, d), epilogue=ep_ln0_bwd, sem=seq,
        comm=last_host, dot_fn=two_blocks)

    _, _, chips_in, chips_p = _copies_now("rs_owner_w_in", *to_owner(["w_in", "w_pool"], [dwin, dwp_g], [sib_in_, sib_p]))
    w_of = {"w_in": shards[0], "w_out": shards[1], "w_ff1": shards[2], "w_ff2": shards[3], "w_pool": shards[4]}
    mv_of = {"w_in": (m_w_in, v_w_in), "w_out": (m_w_out, v_w_out), "w_ff1": (m_w_ff1, v_w_ff1),
             "w_ff2": (m_w_ff2, v_w_ff2), "w_pool": (m_w_pool, v_w_pool)}
    big = {}
    for nm, pt, fs, fc in [("w_ff1", dw1, sib_1, chips_1), ("w_ff2", dw2, sib_2, chips_2), ("w_out", dwout, sib_out, chips_out),
                           ("w_in", dwin, sib_in_, chips_in), ("w_pool", dwp_g, sib_p, chips_p)]:
        w2d = w_of[nm]
        m_, v_ = mv_of[nm]
        big[nm] = _rs_final_adamw("rs_final_adamw_" + nm, slots, pt, fs, fc, w2d, m_.reshape(w2d.shape), v_.reshape(w2d.shape))

    n_f_rows = f // d
    pad_sc = d - p
    packet = jnp.concatenate(
        [loss_vec, dg0, db0, dg1, db1, dbf2, dg2, db2, dbf1.reshape(n_f_rows, d),
         jnp.pad(dsc.reshape(1, p), ((0, 0), (0, pad_sc)))], axis=0)
    n_rows = packet.shape[0]
    n_pad = (-n_rows) % 8
    packet = jnp.pad(packet, ((0, n_pad), (0, 0)))
    sums, loss11 = _small_all_reduce(packet)
    dsc_full = sums[8 + n_f_rows, :p].reshape(n_groups, N_DEV, pr)
    dsc_mine = lax.dynamic_index_in_dim(dsc_full, me, axis=1, keepdims=False)

    def sc_row(a):
        return jnp.pad(a.reshape(1, n_groups * pr), ((0, 0), (0, d - n_groups * pr)))

    def small_pack(ln0g, ln0b, l1g, l1b, bf2, l2g, l2b, bf1, sc):
        rows = [jnp.zeros((1, d), F32), ln0g.reshape(1, d), ln0b.reshape(1, d), l1g, l1b, bf2, l2g, l2b,
                bf1.reshape(n_f_rows, d), sc_row(sc), jnp.zeros((n_pad, d), F32)]
        return jnp.concatenate(rows, axis=0)

    w_small = small_pack(ln_in_g, ln_in_b, ln1_g, ln1_b, b_ff2, ln2_g, ln2_b, b_ff1, pool_scale)
    m_small = small_pack(m_ln_in_g, m_ln_in_b, m_ln1_g, m_ln1_b, m_b_ff2, m_ln2_g, m_ln2_b, m_b_ff1, m_pool_scale)
    v_small = small_pack(v_ln_in_g, v_ln_in_b, v_ln1_g, v_ln1_b, v_b_ff2, v_ln2_g, v_ln2_b, v_b_ff1, v_pool_scale)
    g_small = jnp.concatenate([sums[:8 + n_f_rows], sc_row(dsc_mine), jnp.zeros((n_pad, d), F32)], axis=0)
    small = (g_small,) + tuple(_small_adamw(w_small, g_small, m_small, v_small))

    def unpack(a):
        sc = a[8 + n_f_rows, :n_groups * pr].reshape(1, n_groups, pr)
        return {"ln_in_g": a[1], "ln_in_b": a[2], "ln1_g": a[3:4], "ln1_b": a[4:5], "b_ff2": a[5:6], "ln2_g": a[6:7],
                "ln2_b": a[7:8], "b_ff1": a[8:8 + n_f_rows].reshape(1, f), "pool_scale": sc}

    shapes = {"w_in": w_in.shape, "w_out": w_out.shape, "w_ff1": w_ff1.shape, "w_ff2": w_ff2.shape, "w_pool": w_pool.shape}
    order = ["ln_in_g", "ln_in_b", "w_in", "w_pool", "pool_scale", "w_out", "ln1_g", "ln1_b", "w_ff1", "b_ff1", "w_ff2",
             "b_ff2", "ln2_g", "ln2_b"]
    outs = []
    for kind in range(4):
        small_k = unpack(small[kind])
        for nm in order:
            outs.append(big[nm][kind].reshape(shapes[nm]) if nm in big else small_k[nm])
    return (loss11.reshape(()), dx.reshape(x.shape), *outs)
```

```python
import functools
import math

import jax
import jax.numpy as jnp
from jax import lax
from jax.experimental import pallas as pl
from jax.experimental.pallas import tpu as pltpu

F32 = jnp.float32
BF16 = jnp.bfloat16
MESH = pl.DeviceIdType.MESH

N_DEV = 8
HEAD_DIM = 128
POOL_WINDOWS = (2, 4, 8, 16)
DEEPNORM_ALPHA = (2.0 * 1) ** 0.25
LN_EPS = 1e-5
ADAM_LR = 0.001
ADAM_B1 = 0.9
ADAM_B2 = 0.999
ADAM_EPS = 1e-08
ADAM_WD = 0.01
ADAM_STEP = 10

V7X_VMEM_LIMIT = 56 * 1024 * 1024
ATT_BLOCK = 256
POOL_CHUNK = 256

NN = (((1,), (0,)), ((), ()))
NT = (((1,), (1,)), ((), ()))
TN = (((0,), (0,)), ((), ()))


def _dot(a, b, dims=NN):
    return lax.dot_general(a, b, dims, preferred_element_type=F32)


def _cparams(sem=None):
    return pltpu.CompilerParams(dimension_semantics=sem, vmem_limit_bytes=V7X_VMEM_LIMIT)


def _ln_stats(r):
    mu = jnp.mean(r, axis=-1, keepdims=True)
    xc = r - mu
    var = jnp.mean(xc * xc, axis=-1, keepdims=True)
    rstd = lax.rsqrt(var + LN_EPS)
    return xc * rstd, rstd


def _ln_bwd(dy, xhat, rstd, g):
    dxh = dy * g
    m1 = jnp.mean(dxh, axis=-1, keepdims=True)
    m2 = jnp.mean(dxh * xhat, axis=-1, keepdims=True)
    dx = rstd * (dxh - m1 - xhat * m2)
    dg = jnp.sum(dy * xhat, axis=0, keepdims=True)
    db = jnp.sum(dy, axis=0, keepdims=True)
    return dx, dg, db


def _acc_rows(first, ref, val):
    @pl.when(first)
    def _():
        ref[...] = val

    @pl.when(jnp.logical_not(first))
    def _():
        ref[...] += val


def _call(body, *, name, grid, in_specs, out_specs, out_shape, inputs, scratch_shapes=(), sem=None, comm=None):
    in_specs, out_specs, out_shape, inputs = list(in_specs), list(out_specs), list(out_shape), list(inputs)
    if comm is None:
        outs = pl.pallas_call(
            body, name=name, grid=grid, in_specs=in_specs, out_specs=out_specs, out_shape=out_shape,
            scratch_shapes=list(scratch_shapes), compiler_params=_cparams(sem))(*inputs)
        return list(outs), []
    arrays, plan, n_copies = comm
    n_in, n_out, nc, n_scr = len(inputs), len(out_shape), len(arrays), len(scratch_shapes)

    def hosted(*refs):
        ins = refs[:n_in]
        outs = refs[n_in + nc:n_in + nc + n_out]
        passed = refs[n_in + nc + n_out:n_in + 2 * nc + n_out]
        scratch = refs[n_in + 2 * nc + n_out:n_in + 2 * nc + n_out + n_scr]
        send_sems, recv_sems = refs[-2], refs[-1]
        ids = [pl.program_id(ax) for ax in range(len(grid))]
        first = functools.reduce(jnp.logical_and, [i_ == 0 for i_ in ids])
        last = functools.reduce(jnp.logical_and, [i_ == g - 1 for i_, g in zip(ids, grid)])

        @pl.when(first)
        def _():
            for cp in _plan_copies(plan, passed, send_sems, recv_sems):
                cp.start()

        body(*ins, *outs, *scratch)

        @pl.when(last)
        def _():
            for cp in _plan_copies(plan, passed, send_sems, recv_sems):
                cp.wait_send()
                cp.wait_recv()

    any_spec = pl.BlockSpec(memory_space=pl.ANY)
    outs = pl.pallas_call(
        hosted, name=name, grid=grid,
        in_specs=in_specs + [any_spec] * nc, out_specs=out_specs + [any_spec] * nc,
        out_shape=out_shape + [jax.ShapeDtypeStruct(a.shape, a.dtype) for a in arrays],
        scratch_shapes=list(scratch_shapes) + [pltpu.SemaphoreType.DMA((n_copies,)), pltpu.SemaphoreType.DMA((n_copies,))],
        input_output_aliases={n_in + i: n_out + i for i in range(nc)},
        compiler_params=pltpu.CompilerParams(dimension_semantics=("arbitrary",) * len(grid),
                                             vmem_limit_bytes=V7X_VMEM_LIMIT, has_side_effects=True),
    )(*inputs, *arrays)
    return list(outs[:n_out]), list(outs[n_out:])


def _matmul(name, a, b, *, dims, grid, a_spec, b_spec, extras=(), extra_specs=(), out_shape, out_specs,
            acc_shape, epilogue, k_axis=2, sem=("parallel", "parallel", "arbitrary"), comm=None, dot_fn=None):
    nk = grid[k_axis]
    n_extra = len(extras)
    n_out = len(out_shape)
    if dot_fn is None:
        def dot_fn(a_ref, b_ref):
            return _dot(a_ref[...], b_ref[...], dims)

    def body(a_ref, b_ref, *rest):
        extra_refs = rest[:n_extra]
        out_refs = rest[n_extra:n_extra + n_out]
        if nk == 1:
            epilogue(dot_fn(a_ref, b_ref), extra_refs, out_refs)
            return
        acc_ref = rest[n_extra + n_out]
        k = pl.program_id(k_axis)

        @pl.when(k == 0)
        def _():
            acc_ref[...] = jnp.zeros(acc_shape, F32)

        acc_ref[...] += dot_fn(a_ref, b_ref)

        @pl.when(k == nk - 1)
        def _():
            epilogue(acc_ref[...], extra_refs, out_refs)

    outs, passed = _call(
        body, name=name, grid=grid, in_specs=[a_spec, b_spec, *extra_specs], out_specs=out_specs, out_shape=out_shape,
        inputs=[a, b, *extras], scratch_shapes=[] if nk == 1 else [pltpu.VMEM(acc_shape, F32)], sem=sem, comm=comm)
    return outs if comm is None else (outs, passed)


def _store_epilogue(dtype):

    def ep(acc, extra_refs, out_refs):
        out_refs[0][...] = acc.astype(dtype)
    return ep


def _ln_in_fwd(x, g, b, tm, comm=None):
    t, d = x.shape

    def body(x_ref, g_ref, b_ref, h_ref, hb_ref):
        xhat, _ = _ln_stats(x_ref[...])
        h = xhat * g_ref[...] + b_ref[...]
        h_ref[...] = h
        hb_ref[...] = h.astype(BF16)

    row = pl.BlockSpec((tm, d), lambda i: (i, 0))
    vec = pl.BlockSpec((1, d), lambda i: (0, 0))
    return _call(
        body, name="ln_in_fwd", grid=(t // tm,), in_specs=[row, vec, vec], out_specs=[row, row],
        out_shape=[jax.ShapeDtypeStruct((t, d), F32), jax.ShapeDtypeStruct((t, d), BF16)],
        inputs=[x, g, b], sem=("parallel",), comm=comm)


def _split3(x):
    hi = x.astype(BF16)
    r = x - hi.astype(F32)
    mid = r.astype(BF16)
    lo = (r - mid.astype(F32)).astype(BF16)
    return hi, mid, lo


def _split2(x):
    hi = x.astype(BF16)
    lo = (x - hi.astype(F32)).astype(BF16)
    return hi, lo


def _pool_fwd(u, wp, sc, t, c, comm=None):
    n_groups = len(POOL_WINDOWS)
    tc = POOL_CHUNK
    n_chunks = t // tc

    def body(u_ref, wp_ref, sc_ref, y_ref, ypre_ref, xp_ref):
        g = pl.program_id(0)
        xp_ref[pl.ds(0, tc), :] = jnp.zeros((tc, c), F32)
        xp_ref[pl.ds(tc, t), :] = u_ref[...]
        out_i = lax.broadcasted_iota(jnp.int32, (tc, 2 * tc), 0)
        in_j = lax.broadcasted_iota(jnp.int32, (tc, 2 * tc), 1)
        lag = tc + out_i - in_j
        t_in_chunk = lax.broadcasted_iota(jnp.int32, (tc, 1), 0)
        for gi, w in enumerate(POOL_WINDOWS):
            @pl.when(g == gi)
            def _(w=w):
                band = jnp.logical_and(lag >= 0, lag < w).astype(BF16)

                def chunk(ci, carry):
                    start = pl.multiple_of(ci * tc, tc)
                    win = xp_ref[pl.ds(start, 2 * tc), :]
                    hi, mid, lo = _split3(win)
                    wsum = _dot(band, hi) + _dot(band, mid) + _dot(band, lo)
                    cnt = jnp.minimum(ci * tc + t_in_chunk + 1, w).astype(F32)
                    ypre = wsum * (1.0 / cnt) - win[tc:, :]
                    ypre_b = ypre.astype(BF16)
                    y = _dot(ypre_b, wp_ref[...]) * sc_ref[...]
                    ypre_ref[pl.ds(start, tc), :] = ypre_b
                    y_ref[pl.ds(start, tc), :] = y.astype(BF16)
                    return carry

                lax.fori_loop(0, n_chunks, chunk, 0)

    col = pl.BlockSpec((t, c), lambda g: (0, g))
    return _call(
        body, name="pool_fwd", grid=(n_groups,),
        in_specs=[col, pl.BlockSpec((None, c, c), lambda g: (g, 0, 0)), pl.BlockSpec((None, 1, c), lambda g: (g, 0, 0))],
        out_specs=[col, col],
        out_shape=[jax.ShapeDtypeStruct((t, n_groups * c), BF16), jax.ShapeDtypeStruct((t, n_groups * c), BF16)],
        inputs=[u, wp, sc], scratch_shapes=[pltpu.VMEM((t + tc, c), F32)], sem=("parallel",), comm=comm)


def _pool_bwd(dmixin, ypre, wp, sc, t, c, comm=None):
    n_groups = len(POOL_WINDOWS)
    tc = POOL_CHUNK
    n_chunks = t // tc

    def body(dy_ref, ypre_ref, wp_ref, sc_ref, du_ref, dwp_ref, dsc_ref, zp_ref, neg_ref):
        g = pl.program_id(0)
        zp_ref[pl.ds(t, tc), :] = jnp.zeros((tc, c), F32)
        dwp_ref[...] = jnp.zeros((c, c), F32)
        dsc_ref[...] = jnp.zeros((1, c), F32)
        out_i = lax.broadcasted_iota(jnp.int32, (tc, 2 * tc), 0)
        in_j = lax.broadcasted_iota(jnp.int32, (tc, 2 * tc), 1)
        lead = in_j - out_i
        t_in_chunk = lax.broadcasted_iota(jnp.int32, (tc, 1), 0)
        for gi, w in enumerate(POOL_WINDOWS):
            @pl.when(g == gi)
            def _(w=w):
                band = jnp.logical_and(lead >= 0, lead < w).astype(BF16)

                def first(ci, carry):
                    start = pl.multiple_of(ci * tc, tc)
                    dy = dy_ref[pl.ds(start, tc), :]
                    yp = ypre_ref[pl.ds(start, tc), :]
                    ymm = _dot(yp, wp_ref[...])
                    dsc_ref[...] += jnp.sum(dy * ymm, axis=0, keepdims=True)
                    dys_b = (dy * sc_ref[...]).astype(BF16)
                    dwp_ref[...] += _dot(yp, dys_b, TN)
                    dyp = _dot(dys_b, wp_ref[...], NT)
                    cnt = jnp.minimum(ci * tc + t_in_chunk + 1, w).astype(F32)
                    zp_ref[pl.ds(start, tc), :] = dyp * (1.0 / cnt)
                    neg_ref[pl.ds(start, tc), :] = -dyp
                    return carry

                lax.fori_loop(0, n_chunks, first, 0)

                def second(ci, carry):
                    start = pl.multiple_of(ci * tc, tc)
                    hi, mid, lo = _split3(zp_ref[pl.ds(start, 2 * tc), :])
                    ahead = _dot(band, hi) + _dot(band, mid) + _dot(band, lo)
                    du_ref[pl.ds(start, tc), :] = (neg_ref[pl.ds(start, tc), :] + ahead).astype(BF16)
                    return carry

                lax.fori_loop(0, n_chunks, second, 0)

    col = pl.BlockSpec((t, c), lambda g: (0, g))
    return _call(
        body, name="pool_bwd", grid=(n_groups,),
        in_specs=[col, col, pl.BlockSpec((None, c, c), lambda g: (g, 0, 0)), pl.BlockSpec((None, 1, c), lambda g: (g, 0, 0))],
        out_specs=[col, pl.BlockSpec((None, c, c), lambda g: (g, 0, 0)), pl.BlockSpec((None, 1, c), lambda g: (g, 0, 0))],
        out_shape=[jax.ShapeDtypeStruct((t, n_groups * c), BF16), jax.ShapeDtypeStruct((n_groups, c, c), F32),
                   jax.ShapeDtypeStruct((n_groups, 1, c), F32)],
        inputs=[dmixin, ypre, wp, sc], scratch_shapes=[pltpu.VMEM((t + tc, c), F32), pltpu.VMEM((t, c), F32)],
        sem=("parallel",), comm=comm)


ROW_PARTS = 2


def _att_consts():
    b = ATT_BLOCK
    rp = b // ROW_PARTS
    row = lax.broadcasted_iota(jnp.int32, (b, b), 0)
    col = lax.broadcasted_iota(jnp.int32, (b, b), 1)
    tri = (row >= col).astype(BF16)
    prow = lax.broadcasted_iota(jnp.int32, (rp, b), 0)
    pcol = lax.broadcasted_iota(jnp.int32, (rp, b), 1)
    causal = [pcol < prow + r * rp for r in range(ROW_PARTS)]
    return tri, causal


def _suffix_sum(x, tri):
    hi, lo = _split2(x)
    return _dot(hi, tri) + _dot(lo, tri)


LOG2_E = 1.4426950408889634


def _att_scores(qb, kb, mask):
    z2 = _dot(qb, kb, NT) * (LOG2_E / math.sqrt(HEAD_DIM))
    sp2 = jnp.maximum(z2, 0.0) + jnp.log2(1.0 + jnp.exp2(-jnp.abs(z2)))
    return z2, sp2, (sp2 if mask is None else jnp.where(mask, sp2, 0.0))


HEADS_PER_STEP = 2
ATT_LANES = HEADS_PER_STEP * HEAD_DIM
FWD_HEADS_PER_STEP = 4


def _head_lanes(s):
    return slice(s * HEAD_DIM, (s + 1) * HEAD_DIM)


UNDERFLOW_LOG2 = 160.0


def _sweep_earlier_blocks(i, state, per_chain, block):
    def lowest(st):
        low = st[0]
        for k in range(per_chain, len(st), per_chain):
            low = jnp.minimum(low, st[k])
        return jnp.min(low)

    def more(c):
        return jnp.logical_and(c[0] < i, c[1] < UNDERFLOW_LOG2)

    def trip(c):
        st = block(i - 1 - c[0], c[2:])
        return (c[0] + 1, lowest(st)) + tuple(st)

    return lax.while_loop(more, trip, (jnp.int32(0), lowest(state)) + tuple(state))[2:]


def _attn_fwd(qkv, t, n_heads, comm=None):
    b = ATT_BLOCK
    nq = t // b
    heads_per_step = FWD_HEADS_PER_STEP if n_heads % FWD_HEADS_PER_STEP == 0 else HEADS_PER_STEP
    n_steps = n_heads // heads_per_step

    rp = b // ROW_PARTS
    chains = [(s, r) for s in range(heads_per_step) for r in range(ROW_PARTS)]
    no_mask = [None] * ROW_PARTS

    def body(q_ref, k_ref, v_ref, o_ref):
        tri, causal = _att_consts()

        def blocks(qbs, j, state, masks):
            ks = pl.multiple_of(j * b, b)
            scores = [_att_scores(qbs[ci], k_ref[pl.ds(ks, b), _head_lanes(s)], masks[r]) for ci, (s, r) in enumerate(chains)]
            incls = [_suffix_sum(sc[2], tri) for sc in scores]
            out = []
            for ci, (s, r) in enumerate(chains):
                carry, acc = state[2 * ci], state[2 * ci + 1]
                a = jnp.exp2(scores[ci][0] - (incls[ci] + carry))
                if masks[r] is not None:
                    a = jnp.where(masks[r], a, 0.0)
                out += [carry + incls[ci][:, 0:1], acc + _dot(a.astype(BF16), v_ref[pl.ds(ks, b), _head_lanes(s)])]
            return tuple(out)

        def q_loop(i, _):
            qs = pl.multiple_of(i * b, b)
            qbs = [q_ref[pl.ds(qs + r * rp, rp), _head_lanes(s)] for s, r in chains]
            zero = (jnp.zeros((rp, 1), F32), jnp.zeros((rp, HEAD_DIM), F32)) * len(chains)
            state = blocks(qbs, i, zero, causal)
            state = _sweep_earlier_blocks(i, state, 2, lambda j, st: blocks(qbs, j, st, no_mask))
            for ci, (s, r) in enumerate(chains):
                o_ref[pl.ds(qs + r * rp, rp), _head_lanes(s)] = state[2 * ci + 1]
            return 0

        lax.fori_loop(0, nq, q_loop, 0)

    def heads(off):
        return pl.BlockSpec((t, heads_per_step * HEAD_DIM), lambda h: (0, off + h))

    return _call(
        body, name="attn_fwd", grid=(n_steps,),
        in_specs=[heads(0), heads(n_steps), heads(2 * n_steps)], out_specs=[heads(0)],
        out_shape=[jax.ShapeDtypeStruct((t, n_heads * HEAD_DIM), F32)],
        inputs=[qkv, qkv, qkv], sem=("parallel",), comm=comm)


def _attn_bwd(qkv, do, o, du_pool, t, n_heads, comm=None):
    b = ATT_BLOCK
    nq = t // b
    n_steps = n_heads // HEADS_PER_STEP
    scale = 1.0 / math.sqrt(HEAD_DIM)
    rp = b // ROW_PARTS
    chains = [(s, r) for s in range(HEADS_PER_STEP) for r in range(ROW_PARTS)]
    no_mask = [None] * ROW_PARTS

    def body(q_ref, k_ref, v_ref, do_ref, o_ref, dup_ref, du_ref, qt_ref, dot_ref, dkt_ref, dvt_ref):
        dq_ref, dk_ref, dv_ref = du_ref.at[0], du_ref.at[1], du_ref.at[2]
        du_ref[3] = dup_ref[...]
        for j in range(nq):
            rows = pl.ds(j * b, b)
            qt_ref[j] = q_ref[rows, :].astype(F32).T.astype(BF16)
            dot_ref[j] = do_ref[rows, :].astype(F32).T.astype(BF16)
        dkt_ref[...] = jnp.zeros((nq, ATT_LANES, b), F32)
        dvt_ref[...] = jnp.zeros((nq, ATT_LANES, b), F32)
        tri, causal = _att_consts()

        def blocks(i, fixed, j, state, masks):
            ks = pl.multiple_of(j * b, b)
            n = len(chains)
            kbs = [k_ref[pl.ds(ks, b), _head_lanes(s)] for s, _ in chains]
            scores = [_att_scores(fixed[ci][0], kbs[ci], masks[r]) for ci, (s, r) in enumerate(chains)]
            incls = [_suffix_sum(sc[2], tri) for sc in scores]
            das = [_dot(fixed[ci][1], v_ref[pl.ds(ks, b), _head_lanes(s)], NT) for ci, (s, r) in enumerate(chains)]
            a_bs, gs = [], []
            for ci, (s, r) in enumerate(chains):
                a = jnp.exp2(scores[ci][0] - (incls[ci] + state[3 * ci]))
                if masks[r] is not None:
                    a = jnp.where(masks[r], a, 0.0)
                a_bs.append(a.astype(BF16))
                gs.append(a_bs[ci].astype(F32) * das[ci])
            g_incls = [_suffix_sum(g, tri) for g in gs]
            dz_bs = []
            for ci, (s, r) in enumerate(chains):
                rest = (fixed[ci][2] - state[3 * ci + 1]) - (g_incls[ci] - gs[ci])
                sig = jnp.exp2(scores[ci][0] - scores[ci][1])
                dz = (gs[ci] - sig * rest) * scale
                if masks[r] is not None:
                    dz = jnp.where(masks[r], dz, 0.0)
                dz_bs.append(dz.astype(BF16))
            out = []
            for ci in range(n):
                out += [state[3 * ci] + incls[ci][:, 0:1], state[3 * ci + 1] + g_incls[ci][:, 0:1],
                        state[3 * ci + 2] + _dot(dz_bs[ci], kbs[ci])]
            for s in range(HEADS_PER_STEP):
                lanes = _head_lanes(s)
                dk_add, dv_add = None, None
                for ci, (cs, r) in enumerate(chains):
                    if cs == s:
                        part = slice(r * rp, (r + 1) * rp)
                        dk_c = _dot(qt_ref[i, lanes, part], dz_bs[ci])
                        dv_c = _dot(dot_ref[i, lanes, part], a_bs[ci])
                        dk_add = dk_c if dk_add is None else dk_add + dk_c
                        dv_add = dv_c if dv_add is None else dv_add + dv_c
                dkt_ref[j, lanes, :] += dk_add
                dvt_ref[j, lanes, :] += dv_add
            return tuple(out)

        def q_loop(i, _):
            qs = pl.multiple_of(i * b, b)
            fixed = []
            for s, r in chains:
                rows = pl.ds(qs + r * rp, rp)
                dob = do_ref[rows, _head_lanes(s)]
                total = jnp.sum(dob.astype(F32) * o_ref[rows, _head_lanes(s)], axis=-1, keepdims=True)
                fixed.append((q_ref[rows, _head_lanes(s)], dob, total))
            zero = (jnp.zeros((rp, 1), F32), jnp.zeros((rp, 1), F32), jnp.zeros((rp, HEAD_DIM), F32)) * len(chains)
            state = blocks(i, fixed, i, zero, causal)
            state = _sweep_earlier_blocks(i, state, 3, lambda j, st: blocks(i, fixed, j, st, no_mask))
            for ci, (s, r) in enumerate(chains):
                dq_ref[pl.ds(qs + r * rp, rp), _head_lanes(s)] = state[3 * ci + 2].astype(BF16)
            return 0

        lax.fori_loop(0, nq, q_loop, 0)
        for j in range(nq):
            rows = pl.ds(j * b, b)
            dk_ref[rows, :] = dkt_ref[j].T.astype(BF16)
            dv_ref[rows, :] = dvt_ref[j].T.astype(BF16)

    def heads(off):
        return pl.BlockSpec((t, ATT_LANES), lambda h: (0, off + h))

    return _call(
        body, name="attn_bwd", grid=(n_steps,),
        in_specs=[heads(0), heads(n_steps), heads(2 * n_steps), heads(0), heads(0),
                  pl.BlockSpec((t, ATT_LANES), lambda h: (0, h), pipeline_mode=pl.Buffered(1))],
        out_specs=[pl.BlockSpec((4, t, ATT_LANES), lambda h: (0, 0, h), pipeline_mode=pl.Buffered(1))],
        out_shape=[jax.ShapeDtypeStruct((4, t, n_heads * HEAD_DIM), BF16)], inputs=[qkv, qkv, qkv, do, o, du_pool],
        scratch_shapes=[pltpu.VMEM((nq, ATT_LANES, b), BF16)] * 2 + [pltpu.VMEM((nq, ATT_LANES, b), F32)] * 2,
        sem=("parallel",), comm=comm)


def _place():
    x, y, c = lax.axis_index("x"), lax.axis_index("y"), lax.axis_index("c")
    return x, y, c


def _flip(v, on):
    return 1 - v if on else v


def _plan_copies(plan, refs, send_sems, recv_sems):
    return [pltpu.make_async_remote_copy(src_ref=src, dst_ref=dst, send_sem=send_sems.at[k], recv_sem=recv_sems.at[k],
                                         device_id=dev, device_id_type=MESH)
            for k, (src, dst, dev) in enumerate(plan(refs))]


def _copies_now(name, arrays, plan, n_copies):
    n = len(arrays)

    def body(*refs):
        copies = _plan_copies(plan, refs[n:2 * n], refs[2 * n], refs[2 * n + 1])
        for cp in copies:
            cp.start()
        for cp in copies:
            cp.wait_send()
            cp.wait_recv()

    any_spec = pl.BlockSpec(memory_space=pl.ANY)
    return list(pl.pallas_call(
        body, name=name, in_specs=[any_spec] * n, out_specs=[any_spec] * n,
        out_shape=[jax.ShapeDtypeStruct(a.shape, a.dtype) for a in arrays],
        input_output_aliases={i: i for i in range(n)},
        scratch_shapes=[pltpu.SemaphoreType.DMA((n_copies,)), pltpu.SemaphoreType.DMA((n_copies,))],
        compiler_params=pltpu.CompilerParams(has_side_effects=True),
    )(*arrays))


SIBLING, ACROSS_Y, ACROSS_X, DIAGONAL = 1, 2, 4, 6


def _plan_gather_own(peers, rows=None):
    def plan(refs):
        x, y, c = _place()
        mine = refs[0].at[4 * x + 2 * y + c]
        if rows is not None:
            mine = mine.at[pl.ds(*rows)]
        return [(mine, mine, (_flip(x, k & 4), _flip(y, k & 2), _flip(c, k & 1))) for k in peers]
    return plan, len(peers)


def _plan_gather_forward(n):
    def plan(refs):
        x, y, c = _place()
        out = []
        for ti in range(n):
            for r in range(1, 4):
                blk = refs[ti].at[4 * _flip(x, r & 2) + 2 * _flip(y, r & 1) + c]
                out.append((blk, blk, (x, y, 1 - c)))
        return out
    return plan, 3 * n


def _join_plans(*parts):
    def plan(refs):
        out, at = [], 0
        for part, n_arrays, _ in parts:
            out += part(refs[at:at + n_arrays])
            at += n_arrays
        return out
    return plan, sum(n_cp for _, _, n_cp in parts)


def _plan_rs_sibling(n):
    def plan(refs):
        x, y, c = _place()
        out = []
        for ti in range(n):
            for r in range(4):
                src = refs[ti].at[4 * _flip(x, r & 2) + 2 * _flip(y, r & 1) + (1 - c)]
                out.append((src, refs[n + ti].at[r], (x, y, 1 - c)))
        return out
    return plan, 4 * n


def _plan_rs_owner(n, relations=(1, 2, 3)):
    def plan(refs):
        x, y, c = _place()
        out = []
        for ti in range(n):
            for r in relations:
                out.append((refs[ti].at[r], refs[n + ti].at[r], (_flip(x, r & 2), _flip(y, r & 1), c)))
        return out
    return plan, len(relations) * n


def _owner_slots():
    x, y, c = _place()
    idx = []
    for r in range(4):
        ox, oy = (1 - x if r & 2 else x), (1 - y if r & 1 else y)
        idx.append(4 * ox + 2 * oy + c)
    return jnp.stack(idx).astype(jnp.int32)


def _row_tile(rows, cols):
    tr = max(8, min(rows, (1 << 19) // cols))
    while rows % tr:
        tr //= 2
    return tr


def _rs_chip_sum(name, slots, partial, from_sibling):
    _, rows, cols = partial.shape
    tr = _row_tile(rows, cols)

    def body(slots_ref, p_ref, s_ref, o_ref):
        o_ref[...] = (p_ref[...] + s_ref[...]).astype(BF16)

    grid_spec = pltpu.PrefetchScalarGridSpec(
        num_scalar_prefetch=1, grid=(3, rows // tr),
        in_specs=[pl.BlockSpec((None, tr, cols), lambda r, i, s: (s[r + 1], i, 0)),
                  pl.BlockSpec((None, tr, cols), lambda r, i, s: (r + 1, i, 0))],
        out_specs=pl.BlockSpec((None, tr, cols), lambda r, i, s: (r + 1, i, 0)))
    return pl.pallas_call(
        body, name=name, grid_spec=grid_spec, out_shape=jax.ShapeDtypeStruct((4, rows, cols), BF16),
        compiler_params=_cparams(("parallel", "parallel")),
    )(slots, partial, from_sibling)


def _adamw(w, g, m, v):
    m = ADAM_B1 * m + (1.0 - ADAM_B1) * g
    v = ADAM_B2 * v + (1.0 - ADAM_B2) * (g * g)
    m_hat = m / (1.0 - ADAM_B1 ** ADAM_STEP)
    v_hat = v / (1.0 - ADAM_B2 ** ADAM_STEP)
    delta = -ADAM_LR * (m_hat / (jnp.sqrt(v_hat) + ADAM_EPS) + ADAM_WD * w)
    return delta, m, v


def _rs_final_adamw(name, slots, partial, from_sibling, from_chips, w, m, v):
    rows, cols = w.shape
    tr = _row_tile(rows, cols)

    def body(slots_ref, p_ref, s_ref, c1_ref, c2_ref, c3_ref, w_ref, m_ref, v_ref, g_ref, d_ref, nm_ref, nv_ref):
        g = p_ref[...] + s_ref[...]
        g = g + c1_ref[...].astype(F32)
        g = g + c2_ref[...].astype(F32)
        g = g + c3_ref[...].astype(F32)
        delta, nm, nv = _adamw(w_ref[...], g, m_ref[...], v_ref[...])
        g_ref[...] = g
        d_ref[...] = delta
        nm_ref[...] = nm
        nv_ref[...] = nv

    def slot(r):
        return pl.BlockSpec((None, tr, cols), lambda i, s: (r, i, 0))

    flat = pl.BlockSpec((tr, cols), lambda i, s: (i, 0))
    grid_spec = pltpu.PrefetchScalarGridSpec(
        num_scalar_prefetch=1, grid=(rows // tr,),
        in_specs=[pl.BlockSpec((None, tr, cols), lambda i, s: (s[0], i, 0)), slot(0), slot(1), slot(2), slot(3), flat, flat, flat],
        out_specs=[flat] * 4)
    return pl.pallas_call(
        body, name=name, grid_spec=grid_spec, out_shape=[jax.ShapeDtypeStruct((rows, cols), F32)] * 4,
        compiler_params=_cparams(("parallel",)),
    )(slots, partial, from_sibling, from_chips, from_chips, from_chips, w, m, v)


def _small_all_reduce(packet):
    rows, d = packet.shape

    def body(p_ref, sum_ref, loss_ref, all_ref, send_sems, recv_sems):
        x, y, c = _place()
        me = 4 * x + 2 * y + c
        all_ref[me] = p_ref[...]
        copies = []
        for k in range(1, N_DEV):
            px, py, pc = (1 - x if k & 4 else x), (1 - y if k & 2 else y), (1 - c if k & 1 else c)
            cp = pltpu.make_async_remote_copy(
                src_ref=p_ref, dst_ref=all_ref.at[me], send_sem=send_sems.at[k], recv_sem=recv_sems.at[k],
                device_id=(px, py, pc), device_id_type=MESH)
            cp.start()
            copies.append(cp)
        for cp in copies:
            cp.wait_recv()
        for cp in copies:
            cp.wait_send()
        total = all_ref[0]
        for j in range(1, N_DEV):
            total = total + all_ref[j]
        sum_ref[...] = total
        loss_ref[...] = jnp.sum(total[0:1, :], axis=-1, keepdims=True)

    vmem = pl.BlockSpec(memory_space=pltpu.VMEM)
    return pl.pallas_call(
        body, name="small_all_reduce",
        in_specs=[vmem], out_specs=[vmem, vmem],
        out_shape=[jax.ShapeDtypeStruct((rows, d), F32), jax.ShapeDtypeStruct((1, 1), F32)],
        scratch_shapes=[pltpu.VMEM((N_DEV, rows, d), F32), pltpu.SemaphoreType.DMA((N_DEV,)), pltpu.SemaphoreType.DMA((N_DEV,))],
        compiler_params=pltpu.CompilerParams(has_side_effects=True),
    )(packet)


def _small_adamw(w, g, m, v):
    def body(w_ref, g_ref, m_ref, v_ref, d_ref, nm_ref, nv_ref):
        delta, nm, nv = _adamw(w_ref[...], g_ref[...], m_ref[...], v_ref[...])
        d_ref[...] = delta
        nm_ref[...] = nm
        nv_ref[...] = nv

    vmem = pl.BlockSpec(memory_space=pltpu.VMEM)
    return pl.pallas_call(
        body, name="small_adamw", in_specs=[vmem] * 4, out_specs=[vmem] * 3,
        out_shape=[jax.ShapeDtypeStruct(w.shape, F32)] * 3,
    )(w, g, m, v)


def kernel(x, ln_in_g, ln_in_b, w_in, w_pool, pool_scale, w_out, ln1_g, ln1_b, w_ff1, b_ff1, w_ff2, b_ff2, ln2_g, ln2_b, loss_target, m_ln_in_g, m_ln_in_b, m_w_in, m_w_pool, m_pool_scale, m_w_out, m_ln1_g, m_ln1_b, m_w_ff1, m_b_ff1, m_w_ff2, m_b_ff2, m_ln2_g, m_ln2_b, v_ln_in_g, v_ln_in_b, v_w_in, v_w_pool, v_pool_scale, v_w_out, v_ln1_g, v_ln1_b, v_w_ff1, v_b_ff1, v_w_ff2, v_b_ff2, v_ln2_g, v_ln2_b):
    t, d = x.shape[1], x.shape[2]
    n_groups = len(POOL_WINDOWS)
    c_pool = w_pool.shape[3]
    p = n_groups * c_pool
    n_heads = (d - p) // HEAD_DIM
    ws_in = w_in.shape[2]
    n_in = N_DEV * ws_in
    ws_out = w_out.shape[1]
    ws_f = w_ff1.shape[2]
    f = N_DEV * ws_f
    pr = w_pool.shape[2]
    assert n_in == p + 3 * n_heads * HEAD_DIM and N_DEV * ws_out == d and N_DEV * pr == c_pool

    tm_big = min(t, 1024)
    tm_ep = min(t, 512)
    tkk = min(t, 4096)
    half_f = min(ws_f, 512)
    per_f = ws_f // half_f

    x2 = x.reshape(t, d)
    target = loss_target.reshape(t, d)
    g0, b0 = ln_in_g.reshape(1, d), ln_in_b.reshape(1, d)

    shards = [w_in.reshape(d, ws_in), w_out.reshape(ws_out, d), w_ff1.reshape(d, ws_f), w_ff2.reshape(ws_f, d),
              w_pool.reshape(n_groups * pr, c_pool)]
    x_, y_, c_ = _place()
    me = 4 * x_ + 2 * y_ + c_
    def landing(block):
        return lax.dynamic_update_index_in_dim(lax.empty((N_DEV, *block.shape), block.dtype), block, me, 0)

    land_in, land_out, land_1, land_2, land_pool = [landing(s.astype(BF16)) for s in shards]
    land_scale = landing(pool_scale.reshape(n_groups, pr))

    def sds(shape, dtype=F32):
        return jax.ShapeDtypeStruct(shape, dtype)

    vec = pl.BlockSpec((1, d), lambda m, n, k: (0, 0))
    row_ep = pl.BlockSpec((tm_ep, d), lambda m, n, k: (m, 0))
    tm_res = min(t, 256)
    row_res = pl.BlockSpec((tm_res, d), lambda m, n, k: (m, 0))
    seq = ("arbitrary", "arbitrary", "arbitrary")

    two_level = _plan_gather_own([SIBLING, ACROSS_Y, ACROSS_X, DIAGONAL])
    forward = _plan_gather_forward(1)
    first_needed = [land_in, land_pool, land_scale]
    (h0, h0b), first_needed = _ln_in_fwd(
        x2, g0, b0, tm_big, comm=(first_needed, *_join_plans(*[(two_level[0], 1, two_level[1])] * 3)))
    win_g, wpool_g, scale_g = _copies_now("gather_forward_w_in", first_needed, *_plan_gather_forward(3))
    wp_full = wpool_g.reshape(N_DEV, n_groups, pr, c_pool).transpose(1, 0, 2, 3).reshape(n_groups, c_pool, c_pool)
    sc_full = scale_g.transpose(1, 0, 2).reshape(n_groups, 1, c_pool)

    pool_shards = p // ws_in

    def mm_u(name, first, count, dtype, comm=None):
        return _matmul(
            name, h0b, win_g, dims=NN, grid=(t // tm_big, count, 1),
            a_spec=pl.BlockSpec((tm_big, d), lambda m, n, k: (m, 0)),
            b_spec=pl.BlockSpec((None, d, ws_in), lambda m, n, k: (n + first, 0, 0)),
            out_shape=[sds((t, count * ws_in), dtype)],
            out_specs=[pl.BlockSpec((tm_big, ws_in), lambda m, n, k: (m, n))],
            acc_shape=(tm_big, ws_in), epilogue=_store_epilogue(dtype), comm=comm)

    half = land_1.shape[1] // 2
    diag_a, diag_b = _plan_gather_own([DIAGONAL], (0, half)), _plan_gather_own([DIAGONAL], (half, half))
    (u_pool,), (w1_diag,) = mm_u("mm_u_pool", 0, pool_shards, F32, comm=([land_1], *diag_a))
    (qkv,), (wout_part,) = mm_u("mm_u_qkv", pool_shards, N_DEV - pool_shards, BF16, comm=([land_out], *two_level))

    (y_pool, ypre), (wout_g, w1_diag) = _pool_fwd(
        u_pool, wp_full, sc_full, t, c_pool,
        comm=([wout_part, w1_diag], *_join_plans((forward[0], 1, forward[1]), (diag_b[0], 1, diag_b[1]))))
    (o,), (w1_part,) = _attn_fwd(qkv, t, n_heads, comm=([w1_diag], *_plan_gather_own([SIBLING, ACROSS_Y, ACROSS_X])))
    mixin = jnp.concatenate([y_pool, o.astype(BF16)], axis=1)
    wout_2d = wout_g.reshape(d, d)

    def ep_ln1(acc, ex, outs):
        h0_ref, g_ref, b_ref = ex
        r1 = DEEPNORM_ALPHA * h0_ref[...] + acc
        xhat, _ = _ln_stats(r1)
        h1 = xhat * g_ref[...] + b_ref[...]
        outs[0][...] = r1
        outs[1][...] = h1
        outs[2][...] = h1.astype(BF16)

    (r1, h1, h1b), (w1_g,) = _matmul(
        "mm_mix_ln1", mixin, wout_2d, dims=NN, grid=(t // tm_res, 1, 1),
        a_spec=pl.BlockSpec((tm_res, d), lambda m, n, k: (m, 0)),
        b_spec=pl.BlockSpec((d, d), lambda m, n, k: (0, 0)),
        extras=(h0, ln1_g, ln1_b), extra_specs=(row_res, vec, vec),
        out_shape=[sds((t, d)), sds((t, d)), sds((t, d), BF16)], out_specs=[row_res] * 3,
        acc_shape=(tm_res, d), epilogue=ep_ln1, comm=([w1_part], *forward))

    def ep_ff1(acc, ex, outs):
        f1 = acc + ex[0][...]
        outs[0][...] = f1
        r = jnp.maximum(f1, 0.0)
        outs[1][...] = (r * r).astype(BF16)

    ff_tile = pl.BlockSpec((tm_big, half_f), lambda m, n, k: (m, n))
    (f1, act), (w2_part,) = _matmul(
        "mm_ff1", h1b, w1_g, dims=NN, grid=(t // tm_big, f // half_f, 1),
        a_spec=pl.BlockSpec((tm_big, d), lambda m, n, k: (m, 0)),
        b_spec=pl.BlockSpec((None, d, half_f), lambda m, n, k: (n // per_f, 0, n % per_f)),
        extras=(b_ff1,), extra_specs=(pl.BlockSpec((1, half_f), lambda m, n, k: (0, n)),),
        out_shape=[sds((t, f)), sds((t, f), BF16)], out_specs=[ff_tile, ff_tile],
        acc_shape=(tm_big, half_f), epilogue=ep_ff1, comm=([land_2], *two_level))
    (w2_g,) = _copies_now("gather_forward_w_ff2", [w2_part], *forward)

    def ep_ln2(acc, ex, outs):
        h1_ref, tgt_ref, bf2_ref, g_ref, b_ref = ex
        dr2_ref, dr2b_ref, dg_ref, db_ref, dbf2_ref, loss_ref = outs
        first = pl.program_id(0) == 0
        r2 = DEEPNORM_ALPHA * h1_ref[...] + (acc + bf2_ref[...])
        xhat, rstd = _ln_stats(r2)
        err = xhat * g_ref[...] + b_ref[...] - tgt_ref[...]
        dr2, dg, db = _ln_bwd(err * (1.0 / d), xhat, rstd, g_ref[...])
        dr2_ref[...] = dr2
        dr2b_ref[...] = dr2.astype(BF16)
        _acc_rows(first, dg_ref, dg)
        _acc_rows(first, db_ref, db)
        _acc_rows(first, dbf2_ref, jnp.sum(dr2, axis=0, keepdims=True))
        _acc_rows(first, loss_ref, jnp.sum(err * err, axis=0, keepdims=True) * (0.5 / d))

    dr2, dr2b, dg2, db2, dbf2, loss_vec = _matmul(
        "mm_ff2_ln2_loss", act, w2_g, dims=NN, grid=(t // tm_ep, 1, N_DEV),
        a_spec=pl.BlockSpec((tm_ep, ws_f), lambda m, n, k: (m, k)),
        b_spec=pl.BlockSpec((None, ws_f, d), lambda m, n, k: (k, 0, 0)),
        extras=(h1, target, b_ff2, ln2_g, ln2_b), extra_specs=(row_ep, row_ep, vec, vec, vec),
        out_shape=[sds((t, d)), sds((t, d), BF16)] + [sds((1, d))] * 4, out_specs=[row_ep, row_ep, vec, vec, vec, vec],
        acc_shape=(tm_ep, d), epilogue=ep_ln2, sem=seq)

    def ep_dff1(acc, ex, outs):
        df1 = acc * (2.0 * jnp.maximum(ex[0][...], 0.0))
        outs[0][...] = df1.astype(BF16)
        _acc_rows(pl.program_id(1) == 0, outs[1], jnp.sum(df1, axis=0, keepdims=True))

    df_tile = pl.BlockSpec((tm_big, ws_f), lambda n, m, k: (m, n))
    df1b, dbf1 = _matmul(
        "mm_dff1", dr2b, w2_g, dims=NT, grid=(N_DEV, t // tm_big, 1),
        a_spec=pl.BlockSpec((tm_big, d), lambda n, m, k: (m, 0)),
        b_spec=pl.BlockSpec((None, ws_f, d), lambda n, m, k: (n, 0, 0)),
        extras=(f1,), extra_specs=(df_tile,),
        out_shape=[sds((t, f), BF16), sds((1, f))], out_specs=[df_tile, pl.BlockSpec((1, ws_f), lambda n, m, k: (0, n))],
        acc_shape=(tm_big, ws_f), epilogue=ep_dff1, sem=("parallel", "arbitrary", "arbitrary"))

    tn_d = min(d, 1024)
    dw2 = _matmul(
        "mm_dw2", act, dr2b, dims=TN, grid=(N_DEV, d // tn_d, t // tkk),
        a_spec=pl.BlockSpec((tkk, ws_f), lambda m, n, k: (k, m)),
        b_spec=pl.BlockSpec((tkk, tn_d), lambda m, n, k: (k, n)),
        out_shape=[sds((N_DEV, ws_f, d))], out_specs=[pl.BlockSpec((None, ws_f, tn_d), lambda m, n, k: (m, 0, n))],
        acc_shape=(ws_f, tn_d), epilogue=_store_epilogue(F32))[0]

    dw1 = _matmul(
        "mm_dw1", h1b, df1b, dims=TN, grid=(d // tn_d, N_DEV, t // tkk),
        a_spec=pl.BlockSpec((tkk, tn_d), lambda m, n, k: (k, m)),
        b_spec=pl.BlockSpec((tkk, ws_f), lambda m, n, k: (k, n)),
        out_shape=[sds((N_DEV, d, ws_f))], out_specs=[pl.BlockSpec((None, tn_d, ws_f), lambda m, n, k: (n, m, 0))],
        acc_shape=(tn_d, ws_f), epilogue=_store_epilogue(F32))[0]

    def ep_ln1_bwd(acc, ex, outs):
        dr2_ref, r1_ref, g_ref = ex
        first = pl.program_id(0) == 0
        xhat, rstd = _ln_stats(r1_ref[...])
        dr1, dg, db = _ln_bwd(DEEPNORM_ALPHA * dr2_ref[...] + acc, xhat, rstd, g_ref[...])
        outs[0][...] = dr1
        outs[1][...] = dr1.astype(BF16)
        _acc_rows(first, outs[2], dg)
        _acc_rows(first, outs[3], db)

    slots = _owner_slots()

    def to_sibling(parts):
        return (parts + [lax.empty((4, *pt.shape[1:]), F32) for pt in parts], *_plan_rs_sibling(len(parts)))

    def to_owner(names_, parts, from_sib):
        sums = [_rs_chip_sum("rs_chip_sum_" + nm, slots, pt, fs) for nm, pt, fs in zip(names_, parts, from_sib)]
        return (sums + [lax.empty(cs.shape, BF16) for cs in sums], *_plan_rs_owner(len(sums)))

    (dr1, dr1b, dg1, db1), (dw1, dw2, sib_1, sib_2) = _matmul(
        "mm_dh1_ln1_bwd", df1b, w1_g, dims=NT, grid=(t // tm_ep, 1, N_DEV),
        a_spec=pl.BlockSpec((tm_ep, ws_f), lambda m, n, k: (m, k)),
        b_spec=pl.BlockSpec((None, d, ws_f), lambda m, n, k: (k, 0, 0)),
        extras=(dr2, r1, ln1_g), extra_specs=(row_ep, row_ep, vec),
        out_shape=[sds((t, d)), sds((t, d), BF16), sds((1, d)), sds((1, d))], out_specs=[row_ep, row_ep, vec, vec],
        acc_shape=(tm_ep, d), epilogue=ep_ln1_bwd, sem=seq,
        comm=to_sibling([dw1, dw2]))
    own_1 = to_owner(["w_ff1"], [dw1], [sib_1])
    own_2 = to_owner(["w_ff2"], [dw2], [sib_2])[0]

    dwout = _matmul(
        "mm_dwout", mixin, dr1b, dims=TN, grid=(d // tn_d, d // tn_d, t // tkk),
        a_spec=pl.BlockSpec((tkk, tn_d), lambda m, n, k: (k, m)),
        b_spec=pl.BlockSpec((tkk, tn_d), lambda m, n, k: (k, n)),
        out_shape=[sds((d, d))], out_specs=[pl.BlockSpec((tn_d, tn_d), lambda m, n, k: (m, n))],
        acc_shape=(tn_d, tn_d), epilogue=_store_epilogue(F32))[0].reshape(N_DEV, ws_out, d)

    tn_mix = min(tn_d, p, d - p)

    def mm_dmixin(name, first, width, dtype, comm=None):
        return _matmul(
            name, dr1b, wout_2d, dims=NT, grid=(t // tm_big, width // tn_mix, 1),
            a_spec=pl.BlockSpec((tm_big, d), lambda m, n, k: (m, 0)),
            b_spec=pl.BlockSpec((tn_mix, d), lambda m, n, k: (n + first // tn_mix, 0)),
            out_shape=[sds((t, width), dtype)], out_specs=[pl.BlockSpec((tm_big, tn_mix), lambda m, n, k: (m, n))],
            acc_shape=(tm_big, tn_mix), epilogue=_store_epilogue(dtype), comm=comm)

    (dy_pool,), (dwout, sib_out) = mm_dmixin("mm_dmixin_pool", 0, p, F32, comm=to_sibling([dwout]))
    (do,) = mm_dmixin("mm_dmixin_att", p, d - p, BF16)

    (du_pool, dwp, dsc), (_, chips_out) = _pool_bwd(
        dy_pool, ypre, wp_full, sc_full, t, c_pool, comm=to_owner(["w_out"], [dwout], [sib_out]))
    assert 2 * ws_in == p == d - p
    (du4,), (_, chips_1) = _attn_bwd(qkv, do, o, du_pool, t, n_heads, comm=own_1)

    (dwin,), own_2 = _matmul(
        "mm_dwin", h0b, du4, dims=TN, grid=(d // tn_d, N_DEV, t // tkk),
        a_spec=pl.BlockSpec((tkk, tn_d), lambda m, n, k: (k, m)),
        b_spec=pl.BlockSpec((None, tkk, ws_in), lambda m, n, k: ((n // 2 + 3) % 4, k, n % 2)),
        out_shape=[sds((N_DEV, d, ws_in))], out_specs=[pl.BlockSpec((None, tn_d, ws_in), lambda m, n, k: (n, m, 0))],
        acc_shape=(tn_d, ws_in), epilogue=_store_epilogue(F32), comm=(own_2, *_plan_rs_owner(1, (1, 2))))
    dwp_g = dwp.reshape(n_groups, N_DEV, pr, c_pool).transpose(1, 0, 2, 3).reshape(N_DEV, n_groups * pr, c_pool)

    def ep_ln0_bwd(acc, ex, outs):
        dr1_ref, x_ref, g_ref = ex
        first = pl.program_id(0) == 0
        xhat, rstd = _ln_stats(x_ref[...])
        dx, dg, db = _ln_bwd(DEEPNORM_ALPHA * dr1_ref[...] + acc, xhat, rstd, g_ref[...])
        outs[0][...] = dx
        _acc_rows(first, outs[1], dg)
        _acc_rows(first, outs[2], db)

    sib_in = to_sibling([dwin, dwp_g])
    to_diagonal = _plan_rs_owner(1, (3,))
    last_host = (own_2 + sib_in[0], *_join_plans((to_diagonal[0], 2, to_diagonal[1]), (sib_in[1], 4, sib_in[2])))
    def two_blocks(a_ref, b_ref):
        return _dot(a_ref[:, :ws_in], b_ref[0], NT) + _dot(a_ref[:, ws_in:], b_ref[1], NT)

    (dx, dg0, db0), (_, chips_2, dwin, dwp_g, sib_in_, sib_p) = _matmul(
        "mm_dh0_ln0_bwd", du4, win_g, dims=NT, grid=(t // tm_ep, 1, N_DEV // 2),
        a_spec=pl.BlockSpec((None, tm_ep, 2 * ws_in), lambda m, n, k: (k, m, 0)),
        b_spec=pl.BlockSpec((2, d, ws_in), lambda m, n, k: ((k + 1) % 4, 0, 0)),
        extras=(dr1, x2, g0), extra_specs=(row_ep, row_ep, vec),
        out_shape=[sds((t, d)), sds((1, d)), sds((1, d))], out_specs=[row_ep, vec, vec],
        acc_shape=(tm_ep, d), epilogue=ep_ln0_bwd, sem=seq,
        comm=last_host, dot_fn=two_blocks)

    _, _, chips_in, chips_p = _copies_now("rs_owner_w_in", *to_owner(["w_in", "w_pool"], [dwin, dwp_g], [sib_in_, sib_p]))
    w_of = {"w_in": shards[0], "w_out": shards[1], "w_ff1": shards[2], "w_ff2": shards[3], "w_pool": shards[4]}
    mv_of = {"w_in": (m_w_in, v_w_in), "w_out": (m_w_out, v_w_out), "w_ff1": (m_w_ff1, v_w_ff1),
             "w_ff2": (m_w_ff2, v_w_ff2), "w_pool": (m_w_pool, v_w_pool)}
    big = {}
    for nm, pt, fs, fc in [("w_ff1", dw1, sib_1, chips_1), ("w_ff2", dw2, sib_2, chips_2), ("w_out", dwout, sib_out, chips_out),
                           ("w_in", dwin, sib_in_, chips_in), ("w_pool", dwp_g, sib_p, chips_p)]:
        w2d = w_of[nm]
        m_, v_ = mv_of[nm]
        big[nm] = _rs_final_adamw("rs_final_adamw_" + nm, slots, pt, fs, fc, w2d, m_.reshape(w2d.shape), v_.reshape(w2d.shape))

    n_f_rows = f // d
    pad_sc = d - p
    packet = jnp.concatenate(
        [loss_vec, dg0, db0, dg1, db1, dbf2, dg2, db2, dbf1.reshape(n_f_rows, d),
         jnp.pad(dsc.reshape(1, p), ((0, 0), (0, pad_sc)))], axis=0)
    n_rows = packet.shape[0]
    n_pad = (-n_rows) % 8
    packet = jnp.pad(packet, ((0, n_pad), (0, 0)))
    sums, loss11 = _small_all_reduce(packet)
    dsc_full = sums[8 + n_f_rows, :p].reshape(n_groups, N_DEV, pr)
    dsc_mine = lax.dynamic_index_in_dim(dsc_full, me, axis=1, keepdims=False)

    def sc_row(a):
        return jnp.pad(a.reshape(1, n_groups * pr), ((0, 0), (0, d - n_groups * pr)))

    def small_pack(ln0g, ln0b, l1g, l1b, bf2, l2g, l2b, bf1, sc):
        rows = [jnp.zeros((1, d), F32), ln0g.reshape(1, d), ln0b.reshape(1, d), l1g, l1b, bf2, l2g, l2b,
                bf1.reshape(n_f_rows, d), sc_row(sc), jnp.zeros((n_pad, d), F32)]
        return jnp.concatenate(rows, axis=0)

    w_small = small_pack(ln_in_g, ln_in_b, ln1_g, ln1_b, b_ff2, ln2_g, ln2_b, b_ff1, pool_scale)
    m_small = small_pack(m_ln_in_g, m_ln_in_b, m_ln1_g, m_ln1_b, m_b_ff2, m_ln2_g, m_ln2_b, m_b_ff1, m_pool_scale)
    v_small = small_pack(v_ln_in_g, v_ln_in_b, v_ln1_g, v_ln1_b, v_b_ff2, v_ln2_g, v_ln2_b, v_b_ff1, v_pool_scale)
    g_small = jnp.concatenate([sums[:8 + n_f_rows], sc_row(dsc_mine), jnp.zeros((n_pad, d), F32)], axis=0)
    small = (g_small,) + tuple(_small_adamw(w_small, g_small, m_small, v_small))

    def unpack(a):
        sc = a[8 + n_f_rows, :n_groups * pr].reshape(1, n_groups, pr)
        return {"ln_in_g": a[1], "ln_in_b": a[2], "ln1_g": a[3:4], "ln1_b": a[4:5], "b_ff2": a[5:6], "ln2_g": a[6:7],
                "ln2_b": a[7:8], "b_ff1": a[8:8 + n_f_rows].reshape(1, f), "pool_scale": sc}

    shapes = {"w_in": w_in.shape, "w_out": w_out.shape, "w_ff1": w_ff1.shape, "w_ff2": w_ff2.shape, "w_pool": w_pool.shape}
    order = ["ln_in_g", "ln_in_b", "w_in", "w_pool", "pool_scale", "w_out", "ln1_g", "ln1_b", "w_ff1", "b_ff1", "w_ff2",
             "b_ff2", "ln2_g", "ln2_b"]
    outs = []
    for kind in range(4):
        small_k = unpack(small[kind])
        for nm in order:
            outs.append(big[nm][kind].reshape(shapes[nm]) if nm in big else small_k[nm])
    return (loss11.reshape(()), dx.reshape(x.shape), *outs)
```
